```python
import math
import jax
import jax.numpy as jnp
from jax import lax
import numpy as np

D_MODEL = 1024
BATCH = 8
SEQ = 4096
DEPTH = 4

CHUNK = 64

MIX_WIDTH = D_MODEL
POOL_WIDTH = MIX_WIDTH // 2
POOL_WINDOWS = (2, 4, 8, 16)
POOL_GROUPS = len(POOL_WINDOWS)
POOL_GROUP_DIM = POOL_WIDTH // POOL_GROUPS
HEAD_DIM = 64
ATTN_WIDTH = MIX_WIDTH - POOL_WIDTH
ATTN_HEADS = ATTN_WIDTH // HEAD_DIM

OFF_POOL_U = 0
OFF_POOL_G = OFF_POOL_U + POOL_WIDTH
OFF_Q = OFF_POOL_G + POOL_WIDTH
OFF_K = OFF_Q + ATTN_WIDTH
OFF_V = OFF_K + ATTN_WIDTH
OFF_ATTN_G = OFF_V + ATTN_WIDTH
OFF_F = OFF_ATTN_G + ATTN_WIDTH
IN_COLS = OFF_F + ATTN_HEADS

Q_BLOCK = 128
RMS_EPS = 1e-6
NEG_INF = -1e30

kernel_name = "hymba_pool_fox_trunk"


def rmsnorm(x, g):
    xf = x.astype(jnp.float32)
    y = xf * lax.rsqrt(jnp.mean(xf * xf, axis=-1, keepdims=True) + RMS_EPS)
    return (y * g.astype(jnp.float32)).astype(x.dtype)


def multi_scale_pool(u, pool_w, pool_scale):
    b, s, _ = u.shape
    uf = u.astype(jnp.float32)
    cs = jnp.concatenate([jnp.zeros((b, 1, POOL_WIDTH), jnp.float32), jnp.cumsum(uf, axis=1)], axis=1)
    t = jnp.arange(s)
    parts = []
    for gi, w in enumerate(POOL_WINDOWS):
        sl = slice(gi * POOL_GROUP_DIM, (gi + 1) * POOL_GROUP_DIM)
        c_g = cs[:, :, sl]
        lower = jnp.concatenate([jnp.zeros((b, w - 1, POOL_GROUP_DIM), jnp.float32), c_g[:, : s + 1 - w]], axis=1)
        count = jnp.minimum(t + 1, w).astype(jnp.float32)[None, :, None]
        parts.append((c_g[:, 1:] - lower) / count - uf[:, :, sl])
    d = jnp.stack(parts, axis=2)
    y = jnp.einsum('bsgc,gcd->bsgd', d, pool_w.astype(jnp.float32)).reshape(b, s, POOL_WIDTH)
    y = y * pool_scale.astype(jnp.float32)
    return y.astype(u.dtype)


def forgetting_attention(q, k, v, log_f):
    b, s, h, dh = q.shape
    q = q.transpose(0, 2, 1, 3)
    k = k.transpose(0, 2, 1, 3)
    v = v.transpose(0, 2, 1, 3)
    c = jnp.cumsum(log_f, axis=1).transpose(0, 2, 1)
    scale = 1.0 / math.sqrt(dh)
    outs = []
    for i in range(s // Q_BLOCK):
        q0, q1 = i * Q_BLOCK, (i + 1) * Q_BLOCK
        qb = q[:, :, q0:q1]
        kb = k[:, :, :q1]
        vb = v[:, :, :q1]
        logits = jnp.einsum('bhqd,bhkd->bhqk', qb, kb).astype(jnp.float32) * scale
        logits = logits + (c[:, :, q0:q1, None] - c[:, :, None, :q1])
        mask = jnp.arange(q0, q1)[:, None] >= jnp.arange(q1)[None, :]
        logits = jnp.where(mask[None, None], logits, NEG_INF)
        p = jax.nn.softmax(logits, axis=-1)
        outs.append(jnp.einsum('bhqk,bhkd->bhqd', p.astype(vb.dtype), vb))
    o = jnp.concatenate(outs, axis=2)
    return o.transpose(0, 2, 1, 3).reshape(b, s, h * dh)


def _fwd_setup_inputs(seed: int = 0) -> dict:
    key = jax.random.key(seed)
    ks = jax.random.split(key, 9)
    x = jax.random.normal(ks[0], (BATCH, SEQ, D_MODEL), jnp.float32)
    norm_g = 1.0 + 0.02 * jax.random.normal(ks[1], (DEPTH, D_MODEL), jnp.float32)
    w_in = jax.random.normal(ks[2], (DEPTH, D_MODEL, IN_COLS), jnp.float32) * D_MODEL ** -0.5
    forget_bias = jax.random.uniform(ks[3], (DEPTH, ATTN_HEADS), jnp.float32, minval=1.0, maxval=3.0)
    pool_w = jax.random.normal(ks[4], (DEPTH, POOL_GROUPS, POOL_GROUP_DIM, POOL_GROUP_DIM), jnp.float32) * POOL_GROUP_DIM ** -0.5
    pool_scale = 1.0 + 0.02 * jax.random.normal(ks[5], (DEPTH, POOL_WIDTH), jnp.float32)
    w_out = jax.random.normal(ks[6], (DEPTH, MIX_WIDTH, D_MODEL), jnp.float32) * MIX_WIDTH ** -0.5
    final_g = 1.0 + 0.02 * jax.random.normal(ks[7], (D_MODEL,), jnp.float32)
    return {"x": x, "norm_g": norm_g, "w_in": w_in, "forget_bias": forget_bias,
            "pool_w": pool_w, "pool_scale": pool_scale, "w_out": w_out, "final_g": final_g}


def _fwd_reference(x, norm_g, w_in, forget_bias, pool_w, pool_scale, w_out, final_g):
    b, s, _ = x.shape
    for layer in range(DEPTH):
        h = rmsnorm(x, norm_g[layer])
        proj = h @ w_in[layer]
        pool_u = proj[..., OFF_POOL_U:OFF_POOL_G]
        pool_g = proj[..., OFF_POOL_G:OFF_Q]
        q = proj[..., OFF_Q:OFF_K].reshape(b, s, ATTN_HEADS, HEAD_DIM)
        k = proj[..., OFF_K:OFF_V].reshape(b, s, ATTN_HEADS, HEAD_DIM)
        v = proj[..., OFF_V:OFF_ATTN_G].reshape(b, s, ATTN_HEADS, HEAD_DIM)
        attn_g = proj[..., OFF_ATTN_G:OFF_F]
        log_f = jax.nn.log_sigmoid(proj[..., OFF_F:IN_COLS].astype(jnp.float32)
                                   + forget_bias[layer].astype(jnp.float32))
        pool_out = multi_scale_pool(pool_u, pool_w[layer], pool_scale[layer]) * jax.nn.silu(pool_g)
        attn_out = forgetting_attention(q, k, v, log_f) * jax.nn.silu(attn_g)
        mixed = jnp.concatenate([pool_out, attn_out], axis=-1)
        x = x + mixed @ w_out[layer]
    return rmsnorm(x, final_g)


import jax as _jax
import jax.numpy as _jnp

TWIN_FORMAT = 'train_step'
FWD_PARAMS = ['x', 'norm_g', 'w_in', 'forget_bias', 'pool_w', 'pool_scale', 'w_out', 'final_g']
TWIN_WEIGHTS = ['norm_g', 'w_in', 'forget_bias', 'pool_w', 'pool_scale', 'w_out', 'final_g']
TWIN_DIFF_INPUT = 'x'
TWIN_INPUTS = ['x', 'norm_g', 'w_in', 'forget_bias', 'pool_w', 'pool_scale', 'w_out', 'final_g', 'loss_target', 'm_norm_g', 'm_w_in', 'm_forget_bias', 'm_pool_w', 'm_pool_scale', 'm_w_out', 'm_final_g', 'v_norm_g', 'v_w_in', 'v_forget_bias', 'v_pool_w', 'v_pool_scale', 'v_w_out', 'v_final_g']
TWIN_OUTPUTS = ['loss', 'grad_x', 'grad_norm_g', 'grad_w_in', 'grad_forget_bias', 'grad_pool_w', 'grad_pool_scale', 'grad_w_out', 'grad_final_g', 'delta_norm_g', 'delta_w_in', 'delta_forget_bias', 'delta_pool_w', 'delta_pool_scale', 'delta_w_out', 'delta_final_g', 'new_m_norm_g', 'new_m_w_in', 'new_m_forget_bias', 'new_m_pool_w', 'new_m_pool_scale', 'new_m_w_out', 'new_m_final_g', 'new_v_norm_g', 'new_v_w_in', 'new_v_forget_bias', 'new_v_pool_w', 'new_v_pool_scale', 'new_v_w_out', 'new_v_final_g']
TWIN_LEAF_KINDS = {'loss': 'loss', 'grad_x': 'grad_x', 'grad_norm_g': 'grad_w', 'grad_w_in': 'grad_w', 'grad_forget_bias': 'grad_w', 'grad_pool_w': 'grad_w', 'grad_pool_scale': 'grad_w', 'grad_w_out': 'grad_w', 'grad_final_g': 'grad_w', 'delta_norm_g': 'delta_w', 'delta_w_in': 'delta_w', 'delta_forget_bias': 'delta_w', 'delta_pool_w': 'delta_w', 'delta_pool_scale': 'delta_w', 'delta_w_out': 'delta_w', 'delta_final_g': 'delta_w', 'new_m_norm_g': 'new_m', 'new_m_w_in': 'new_m', 'new_m_forget_bias': 'new_m', 'new_m_pool_w': 'new_m', 'new_m_pool_scale': 'new_m', 'new_m_w_out': 'new_m', 'new_m_final_g': 'new_m', 'new_v_norm_g': 'new_v', 'new_v_w_in': 'new_v', 'new_v_forget_bias': 'new_v', 'new_v_pool_w': 'new_v', 'new_v_pool_scale': 'new_v', 'new_v_w_out': 'new_v', 'new_v_final_g': 'new_v'}


def _forward(args):
    return _fwd_reference(*[args[k] for k in FWD_PARAMS])


def _output_shape():
    out = _jax.eval_shape(lambda: _forward(_fwd_setup_inputs(0)))
    return out.shape, out.dtype

N_MICROBATCH = 1
ADAM_LR = 0.001
ADAM_B1 = 0.9
ADAM_B2 = 0.999
ADAM_EPS = 1e-08
ADAM_WD = 0.01
ADAM_STEP = 10
PER_EXAMPLE_BATCH_AXIS = {'x': 0, 'loss_target': 0}
SHARED_INPUTS = []
_WEIGHT_DTYPES = {'norm_g': _jnp.float32, 'w_in': _jnp.float32, 'forget_bias': _jnp.float32, 'pool_w': _jnp.float32, 'pool_scale': _jnp.float32, 'w_out': _jnp.float32, 'final_g': _jnp.float32}
MOMENT_SCALE = {'norm_g': 1.049228e-01, 'w_in': 6.012798e-02, 'forget_bias': 2.094895e-01, 'pool_w': 8.762176e-02, 'pool_scale': 8.973433e-02, 'w_out': 6.885595e-02, 'final_g': 3.194622e+01}


def _to_microbatches(a, axis):
    t = _jnp.moveaxis(a, axis, 0)
    t = t.reshape((N_MICROBATCH, t.shape[0] // N_MICROBATCH) + t.shape[1:])
    return _jnp.moveaxis(t, 1, axis + 1)


def setup_inputs(seed: int = 0) -> dict:
    inp = _fwd_setup_inputs(seed)
    key = _jax.random.fold_in(_jax.random.key(seed), 7919)
    shape, _ = _output_shape()
    out = dict(inp)
    out["loss_target"] = _jax.random.normal(_jax.random.fold_in(key, 0), shape, _jnp.float32)
    for i, name in enumerate(TWIN_WEIGHTS):
        w = inp[name].astype(_jnp.float32)
        if MOMENT_SCALE is None:
            s = _jnp.sqrt(_jnp.mean(_jnp.square(w)) + 1e-30)
        else:
            s = MOMENT_SCALE[name]
        km, kv = _jax.random.split(_jax.random.fold_in(key, i + 1))
        out[name] = w
        out["m_" + name] = s * _jax.random.normal(km, w.shape, _jnp.float32)
        out["v_" + name] = (s * s) * _jax.random.uniform(kv, w.shape, _jnp.float32, 0.5, 1.5)
    if N_MICROBATCH > 1:
        for name, axis in PER_EXAMPLE_BATCH_AXIS.items():
            out[name] = _to_microbatches(out[name], axis)
    return {'x': out['x'], 'norm_g': out['norm_g'], 'w_in': out['w_in'], 'forget_bias': out['forget_bias'], 'pool_w': out['pool_w'], 'pool_scale': out['pool_scale'], 'w_out': out['w_out'], 'final_g': out['final_g'], 'loss_target': out['loss_target'], 'm_norm_g': out['m_norm_g'], 'm_w_in': out['m_w_in'], 'm_forget_bias': out['m_forget_bias'], 'm_pool_w': out['m_pool_w'], 'm_pool_scale': out['m_pool_scale'], 'm_w_out': out['m_w_out'], 'm_final_g': out['m_final_g'], 'v_norm_g': out['v_norm_g'], 'v_w_in': out['v_w_in'], 'v_forget_bias': out['v_forget_bias'], 'v_pool_w': out['v_pool_w'], 'v_pool_scale': out['v_pool_scale'], 'v_w_out': out['v_w_out'], 'v_final_g': out['v_final_g']}


def _loss(weights, diff, rest, loss_target):
    with _jax.named_scope("forward"):
        args = {**rest, TWIN_DIFF_INPUT: diff, **{k: w.astype(_WEIGHT_DTYPES[k]) for k, w in weights.items()}}
        y = _forward(args)
    with _jax.named_scope("loss_head"):
        err = _jnp.square(y.astype(_jnp.float32) - loss_target)
        return 0.5 * _jnp.sum(_jnp.mean(err, axis=-1)) if err.ndim else 0.5 * err


def _adamw(w, g, m, v):
    m = ADAM_B1 * m + (1.0 - ADAM_B1) * g
    v = ADAM_B2 * v + (1.0 - ADAM_B2) * _jnp.square(g)
    m_hat = m / (1.0 - ADAM_B1 ** ADAM_STEP)
    v_hat = v / (1.0 - ADAM_B2 ** ADAM_STEP)
    delta = -ADAM_LR * (m_hat / (_jnp.sqrt(v_hat) + ADAM_EPS) + ADAM_WD * w)
    return delta, m, v


def reference(x, norm_g, w_in, forget_bias, pool_w, pool_scale, w_out, final_g, loss_target, m_norm_g, m_w_in, m_forget_bias, m_pool_w, m_pool_scale, m_w_out, m_final_g, v_norm_g, v_w_in, v_forget_bias, v_pool_w, v_pool_scale, v_w_out, v_final_g):
    given = dict(x=x, norm_g=norm_g, w_in=w_in, forget_bias=forget_bias, pool_w=pool_w, pool_scale=pool_scale, w_out=w_out, final_g=final_g, loss_target=loss_target, m_norm_g=m_norm_g, m_w_in=m_w_in, m_forget_bias=m_forget_bias, m_pool_w=m_pool_w, m_pool_scale=m_pool_scale, m_w_out=m_w_out, m_final_g=m_final_g, v_norm_g=v_norm_g, v_w_in=v_w_in, v_forget_bias=v_forget_bias, v_pool_w=v_pool_w, v_pool_scale=v_pool_scale, v_w_out=v_w_out, v_final_g=v_final_g)
    weights = {n: given[n] for n in TWIN_WEIGHTS}
    shared = {n: given[n] for n in SHARED_INPUTS}
    per_example = {n: given[n] for n in ['x']}
    grad_fn = _jax.value_and_grad(_loss, argnums=(0, 1))

    def one_microbatch(ex, loss_target):
        ex = dict(ex)
        diff = ex.pop(TWIN_DIFF_INPUT)
        return grad_fn(weights, diff, {**shared, **ex}, loss_target)

    if N_MICROBATCH == 1:
        loss, (grad_w, grad_x) = one_microbatch(per_example, given["loss_target"])
    else:
        def body(carry, xs):
            loss_sum, grad_sum = carry
            l_k, (gw_k, gx_k) = one_microbatch(xs[0], xs[1])
            with _jax.named_scope("update"):
                return (loss_sum + l_k, _jax.tree.map(_jnp.add, grad_sum, gw_k)), gx_k

        init = (_jnp.zeros((), _jnp.float32), _jax.tree.map(_jnp.zeros_like, weights))
        (loss, grad_w), grad_x = _jax.lax.scan(body, init, (per_example, given["loss_target"]))
    with _jax.named_scope("update"):
        delta_w, new_m, new_v = {}, {}, {}
        for n in TWIN_WEIGHTS:
            delta_w[n], new_m[n], new_v[n] = _adamw(weights[n], grad_w[n], given["m_" + n], given["v_" + n])
    return (loss, grad_x, *[grad_w[n] for n in TWIN_WEIGHTS], *[delta_w[n] for n in TWIN_WEIGHTS],
            *[new_m[n] for n in TWIN_WEIGHTS], *[new_v[n] for n in TWIN_WEIGHTS])
```

```python
import functools
import math

import jax
import jax.numpy as jnp
from jax import lax
from jax.experimental import pallas as pl
from jax.experimental.pallas import tpu as pltpu

F32 = jnp.float32
BF16 = jnp.bfloat16
MESH = pl.DeviceIdType.MESH

RMS_EPS = 1e-6
NEG_INF = -1e30
HEAD_DIM = 64
POOL_WINDOWS = (2, 4, 8, 16)
MAX_WINDOW = 16
LANES = 128
N_DEV = 8

ADAM_LR = 0.001
ADAM_B1 = 0.9
ADAM_B2 = 0.999
ADAM_EPS = 1e-08
ADAM_WD = 0.01
ADAM_STEP = 10

TM = 512
TN = 512
TQ = 512
TB = 256
VMEM_LIMIT = 56 * 1024 * 1024

NT = (((1,), (1,)), ((), ()))
TN_DIMS = (((0,), (0,)), ((), ()))


def _params(*sem):
    return pltpu.CompilerParams(dimension_semantics=sem, vmem_limit_bytes=VMEM_LIMIT)


def _sigmoid(x):
    return 1.0 / (1.0 + jnp.exp(-x))


def _inproj_fwd(x, gam, wmain, wf):
    S, D = x.shape
    N = wmain.shape[1]
    tm, tn = min(TM, S), TN

    def body(x_ref, g_ref, w_ref, wf_ref, proj_ref, h_ref, z_ref):
        @pl.when(pl.program_id(1) == 0)
        def _():
            xf = x_ref[...]
            r = lax.rsqrt(jnp.mean(xf * xf, axis=-1, keepdims=True) + RMS_EPS)
            h = ((xf * r) * g_ref[...]).astype(BF16)
            h_ref[...] = h
            z_ref[...] = jnp.dot(h, wf_ref[...], preferred_element_type=F32)

        proj_ref[...] = jnp.dot(h_ref[...], w_ref[...], preferred_element_type=F32)

    return pl.pallas_call(
        body, name="inproj_fwd", grid=(S // tm, N // tn),
        in_specs=[pl.BlockSpec((tm, D), lambda i, j: (i, 0)),
                  pl.BlockSpec((1, D), lambda i, j: (0, 0)),
                  pl.BlockSpec((D, tn), lambda i, j: (0, j)),
                  pl.BlockSpec((D, LANES), lambda i, j: (0, 0))],
        out_specs=[pl.BlockSpec((tm, tn), lambda i, j: (i, j)),
                   pl.BlockSpec((tm, D), lambda i, j: (i, 0)),
                   pl.BlockSpec((tm, LANES), lambda i, j: (i, 0))],
        out_shape=[jax.ShapeDtypeStruct((S, N), F32),
                   jax.ShapeDtypeStruct((S, D), BF16),
                   jax.ShapeDtypeStruct((S, LANES), F32)],
        compiler_params=_params("parallel", "arbitrary"),
    )(x, gam, wmain, wf)


def _fgate_fwd(z, bias, n_heads):
    S = z.shape[0]
    tb = min(TB, S)

    def body(z_ref, b_ref, ccol_ref, crow_ref, c_scr):
        lane = lax.broadcasted_iota(jnp.int32, (tb, LANES), 1)
        tri = (lax.broadcasted_iota(jnp.int32, (tb, tb), 0)
               >= lax.broadcasted_iota(jnp.int32, (tb, tb), 1)).astype(F32)

        def block(i, carry):
            r0 = pl.multiple_of(i * tb, tb)
            zz = z_ref[pl.ds(r0, tb), :] + b_ref[...]
            lf = jnp.minimum(zz, 0.0) - jnp.log(1.0 + jnp.exp(-jnp.abs(zz)))
            lf = jnp.where(lane < n_heads, lf, 0.0)
            c = jnp.dot(tri, lf, preferred_element_type=F32, precision=lax.Precision.HIGHEST) + carry
            c_scr[pl.ds(r0, tb), :] = c
            return c[tb - 1:tb, :]

        lax.fori_loop(0, S // tb, block, jnp.zeros((1, LANES), F32))
        c = c_scr[...]
        ct = c.T
        for h in range(n_heads):
            ccol_ref[h] = jnp.broadcast_to(c[:, h:h + 1], (S, LANES))
            crow_ref[h] = jnp.broadcast_to(ct[h:h + 1, :], (8, S))

    return pl.pallas_call(
        body, name="fgate_fwd",
        out_shape=[jax.ShapeDtypeStruct((n_heads, S, LANES), F32),
                   jax.ShapeDtypeStruct((n_heads, 8, S), F32)],
        scratch_shapes=[pltpu.VMEM((S, LANES), F32)],
        compiler_params=pltpu.CompilerParams(vmem_limit_bytes=VMEM_LIMIT),
    )(z, bias)


def _window_mean_minus_self(u, pad_ref, w, S):
    pad_ref[0:MAX_WINDOW, :] = jnp.zeros((MAX_WINDOW, LANES), F32)
    pad_ref[MAX_WINDOW:MAX_WINDOW + S, :] = u
    acc = u
    for j in range(1, w):
        acc = acc + pad_ref[MAX_WINDOW - j:MAX_WINDOW - j + S, :]
    t = lax.broadcasted_iota(jnp.int32, (S, LANES), 0)
    cnt = jnp.minimum(t + 1, w).astype(F32)
    return acc / cnt - u, cnt


def _pool_fwd(proj, pool_w, pool_scale):
    S = proj.shape[0]
    G = len(POOL_WINDOWS)

    def body(u_ref, w_ref, s_ref, y_ref, pad_ref):
        g = pl.program_id(0)
        for gi, w in enumerate(POOL_WINDOWS):
            @pl.when(g == gi)
            def _():
                d, _ = _window_mean_minus_self(u_ref[...], pad_ref, w, S)
                y = jnp.dot(d.astype(BF16), w_ref[0].astype(BF16), preferred_element_type=F32)
                y_ref[...] = y * s_ref[...]

    return pl.pallas_call(
        body, name="pool_fwd", grid=(G,),
        in_specs=[pl.BlockSpec((S, LANES), lambda g: (0, g)),
                  pl.BlockSpec((1, LANES, LANES), lambda g: (g, 0, 0)),
                  pl.BlockSpec((1, LANES), lambda g: (0, g))],
        out_specs=pl.BlockSpec((S, LANES), lambda g: (0, g)),
        out_shape=jax.ShapeDtypeStruct((S, G * LANES), F32),
        scratch_shapes=[pltpu.VMEM((S + MAX_WINDOW, LANES), F32)],
        compiler_params=_params("arbitrary"),
    )(proj, pool_w, pool_scale)


def _attn_fwd(proj, ccol, crow):
    S = proj.shape[0]
    W = proj.shape[1] // 6
    P = W // LANES
    tq = min(TQ, S)
    nq = S // tq
    qc, kc, vc = 2 * P, 3 * P, 4 * P
    scale = 1.0 / math.sqrt(HEAD_DIM)

    def body(q_ref, k_ref, v_ref, cq_ref, ck_ref, o_ref, lse_ref, qm_scr, m_scr, l_scr, acc_scr):
        i, j = pl.program_id(1), pl.program_id(2)
        lane = lax.broadcasted_iota(jnp.int32, (tq, LANES), 1)

        @pl.when(j == 0)
        def _():
            qs = (q_ref[...] * scale).astype(BF16)
            qm_scr[0] = jnp.where(lane < HEAD_DIM, qs, jnp.zeros_like(qs))
            qm_scr[1] = jnp.where(lane >= HEAD_DIM, qs, jnp.zeros_like(qs))
            m_scr[...] = jnp.full(m_scr.shape, NEG_INF, F32)
            l_scr[...] = jnp.zeros(l_scr.shape, F32)
            acc_scr[...] = jnp.zeros(acc_scr.shape, F32)

        @pl.when(j <= i)
        def _():
            k2 = k_ref[...].astype(BF16)
            v2 = v_ref[...].astype(BF16)
            row = lax.broadcasted_iota(jnp.int32, (tq, tq), 0)
            col = lax.broadcasted_iota(jnp.int32, (tq, tq), 1)
            keep = jnp.logical_or(j < i, row >= col)
            for a in range(2):
                s = lax.dot_general(qm_scr[a], k2, NT, preferred_element_type=F32)
                s = s + (jnp.tile(cq_ref[a], (1, tq // LANES)) - jnp.tile(ck_ref[a], (tq // 8, 1)))
                s = jnp.where(keep, s, NEG_INF)
                m_prev = m_scr[a]
                m_new = jnp.maximum(m_prev, jnp.max(s, axis=1, keepdims=True))
                p = jnp.exp(s - jnp.tile(m_new, (1, tq // LANES)))
                alpha = jnp.exp(m_prev - m_new)
                l_scr[a] = alpha * l_scr[a] + jnp.sum(p, axis=1, keepdims=True)
                acc_scr[a] = alpha * acc_scr[a] + jnp.dot(p.astype(BF16), v2, preferred_element_type=F32)
                m_scr[a] = m_new

        @pl.when(j == i)
        def _():
            o_ref[...] = jnp.where(lane < HEAD_DIM, acc_scr[0] / l_scr[0], acc_scr[1] / l_scr[1])
            lse_ref[0] = m_scr[0] + jnp.log(l_scr[0])
            lse_ref[1] = m_scr[1] + jnp.log(l_scr[1])

    return pl.pallas_call(
        body, name="attn_fwd", grid=(P, nq, nq),
        in_specs=[pl.BlockSpec((tq, LANES), lambda p, i, j: (i, qc + p)),
                  pl.BlockSpec((tq, LANES), lambda p, i, j: (jnp.minimum(j, i), kc + p)),
                  pl.BlockSpec((tq, LANES), lambda p, i, j: (jnp.minimum(j, i), vc + p)),
                  pl.BlockSpec((2, tq, LANES), lambda p, i, j: (p, i, 0)),
                  pl.BlockSpec((2, 8, tq), lambda p, i, j: (p, 0, jnp.minimum(j, i)))],
        out_specs=[pl.BlockSpec((tq, LANES), lambda p, i, j: (i, p)),
                   pl.BlockSpec((2, tq, LANES), lambda p, i, j: (p, i, 0))],
        out_shape=[jax.ShapeDtypeStruct((S, W), F32),
                   jax.ShapeDtypeStruct((2 * P, S, LANES), F32)],
        scratch_shapes=[pltpu.VMEM((2, tq, LANES), BF16),
                        pltpu.VMEM((2, tq, LANES), F32),
                        pltpu.VMEM((2, tq, LANES), F32),
                        pltpu.VMEM((2, tq, LANES), F32)],
        compiler_params=_params("parallel", "parallel", "arbitrary"),
    )(proj, proj, proj, ccol, crow)


def _outproj_fwd(ypool, o, proj, x, wout):
    S, D = x.shape
    W = D // 2
    tm, tn = min(TM, S), TN

    def body(y_ref, o_ref, pg_ref, ag_ref, x_ref, w_ref, xn_ref, mix_ref):
        @pl.when(pl.program_id(1) == 0)
        def _():
            pg, ag = pg_ref[...], ag_ref[...]
            mix_ref[:, 0:W] = (y_ref[...] * (pg * _sigmoid(pg))).astype(BF16)
            mix_ref[:, W:D] = (o_ref[...] * (ag * _sigmoid(ag))).astype(BF16)

        xn_ref[...] = x_ref[...] + jnp.dot(mix_ref[...], w_ref[...], preferred_element_type=F32)

    return pl.pallas_call(
        body, name="outproj_fwd", grid=(S // tm, D // tn),
        in_specs=[pl.BlockSpec((tm, W), lambda i, j: (i, 0)),
                  pl.BlockSpec((tm, W), lambda i, j: (i, 0)),
                  pl.BlockSpec((tm, W), lambda i, j: (i, 1)),
                  pl.BlockSpec((tm, W), lambda i, j: (i, 5)),
                  pl.BlockSpec((tm, tn), lambda i, j: (i, j)),
                  pl.BlockSpec((D, tn), lambda i, j: (0, j))],
        out_specs=[pl.BlockSpec((tm, tn), lambda i, j: (i, j)),
                   pl.BlockSpec((tm, D), lambda i, j: (i, 0))],
        out_shape=[jax.ShapeDtypeStruct((S, D), F32),
                   jax.ShapeDtypeStruct((S, D), BF16)],
        compiler_params=_params("parallel", "arbitrary"),
    )(ypool, o, proj, proj, x, wout)


def _loss_head(x, gam, target):
    S, D = x.shape
    tm = min(TM, S)

    def body(x_ref, g_ref, t_ref, dx_ref, loss_ref, dg_ref):
        @pl.when(pl.program_id(0) == 0)
        def _():
            loss_ref[...] = jnp.zeros(loss_ref.shape, F32)
            dg_ref[...] = jnp.zeros(dg_ref.shape, F32)

        xf, gam_v = x_ref[...], g_ref[...]
        r = lax.rsqrt(jnp.mean(xf * xf, axis=-1, keepdims=True) + RMS_EPS)
        xhat = xf * r
        err = xhat * gam_v - t_ref[...]
        part = jnp.sum(jnp.sum(err * err, axis=-1, keepdims=True), axis=0, keepdims=True)
        loss_ref[...] += part * (0.5 / D)
        dy = err * (1.0 / D)
        dg_ref[...] += jnp.sum(dy * xhat, axis=0, keepdims=True)
        dxhat = dy * gam_v
        dx_ref[...] = r * (dxhat - xhat * jnp.mean(dxhat * xhat, axis=-1, keepdims=True))

    return pl.pallas_call(
        body, name="loss_head", grid=(S // tm,),
        in_specs=[pl.BlockSpec((tm, D), lambda i: (i, 0)),
                  pl.BlockSpec((1, D), lambda i: (0, 0)),
                  pl.BlockSpec((tm, D), lambda i: (i, 0))],
        out_specs=[pl.BlockSpec((tm, D), lambda i: (i, 0)),
                   pl.BlockSpec((8, LANES), lambda i: (0, 0)),
                   pl.BlockSpec((1, D), lambda i: (0, 0))],
        out_shape=[jax.ShapeDtypeStruct((S, D), F32),
                   jax.ShapeDtypeStruct((8, LANES), F32),
                   jax.ShapeDtypeStruct((1, D), F32)],
        compiler_params=_params("arbitrary"),
    )(x, gam, target)


def _outproj_bwd_dx(g, wout, ypool, o, proj):
    S, D = g.shape
    W = D // 2
    tm = min(TM, S)

    def body(g_ref, w_ref, y_ref, o_ref, gate_ref, da_ref, dgate_ref):
        j = pl.program_id(1)
        dmix = lax.dot_general(g_ref[...].astype(BF16), w_ref[...], NT, preferred_element_type=F32)
        val = jnp.where(j == 0, y_ref[...], o_ref[...])
        gt = gate_ref[...]
        sg = _sigmoid(gt)
        da_ref[...] = dmix * (gt * sg)
        dgate_ref[...] = (dmix * val * (sg * (1.0 + gt * (1.0 - sg)))).astype(BF16)

    return pl.pallas_call(
        body, name="outproj_bwd_dx", grid=(S // tm, 2),
        in_specs=[pl.BlockSpec((tm, D), lambda i, j: (i, 0)),
                  pl.BlockSpec((W, D), lambda i, j: (j, 0)),
                  pl.BlockSpec((tm, W), lambda i, j: (i, 0)),
                  pl.BlockSpec((tm, W), lambda i, j: (i, 0)),
                  pl.BlockSpec((tm, W), lambda i, j: (i, 1 + 4 * j))],
        out_specs=[pl.BlockSpec((tm, W), lambda i, j: (i, j)),
                   pl.BlockSpec((tm, W), lambda i, j: (i, j))],
        out_shape=[jax.ShapeDtypeStruct((S, D), F32),
                   jax.ShapeDtypeStruct((S, D), BF16)],
        compiler_params=_params("parallel", "arbitrary"),
    )(g, wout, ypool, o, proj)


def _matmul_tn(a, b, name):
    S, M = a.shape
    N = b.shape[1]
    ts, tn = min(TM, S), min(TN, N)

    def body(a_ref, b_ref, out_ref):
        @pl.when(pl.program_id(1) == 0)
        def _():
            out_ref[...] = jnp.zeros(out_ref.shape, F32)

        out_ref[...] += lax.dot_general(a_ref[...], b_ref[...].astype(BF16), TN_DIMS,
                                        preferred_element_type=F32)

    return pl.pallas_call(
        body, name=name, grid=(N // tn, S // ts),
        in_specs=[pl.BlockSpec((ts, M), lambda n, k: (k, 0)),
                  pl.BlockSpec((ts, tn), lambda n, k: (k, n))],
        out_specs=pl.BlockSpec((M, tn), lambda n, k: (0, n)),
        out_shape=jax.ShapeDtypeStruct((M, N), F32),
        compiler_params=_params("parallel", "arbitrary"),
    )(a, b)


def _attn_delta(da, o):
    S, D = da.shape
    W = D // 2
    P = W // LANES
    tq = min(TQ, S)

    def body(do_ref, o_ref, dl_ref):
        lane = lax.broadcasted_iota(jnp.int32, (tq, LANES), 1)
        prod = do_ref[...] * o_ref[...]
        first = jnp.sum(jnp.where(lane < HEAD_DIM, prod, 0.0), axis=1, keepdims=True)
        second = jnp.sum(jnp.where(lane >= HEAD_DIM, prod, 0.0), axis=1, keepdims=True)
        dl_ref[0] = jnp.broadcast_to(first, (tq, LANES))
        dl_ref[1] = jnp.broadcast_to(second, (tq, LANES))

    return pl.pallas_call(
        body, name="attn_delta", grid=(P, S // tq),
        in_specs=[pl.BlockSpec((tq, LANES), lambda p, i: (i, P + p)),
                  pl.BlockSpec((tq, LANES), lambda p, i: (i, p))],
        out_specs=pl.BlockSpec((2, tq, LANES), lambda p, i: (p, i, 0)),
        out_shape=jax.ShapeDtypeStruct((2 * P, S, LANES), F32),
        compiler_params=_params("parallel", "parallel"),
    )(da, o)


def _attn_bwd(proj, da, lse, delta, ccol, crow):
    S = proj.shape[0]
    W = proj.shape[1] // 6
    P = W // LANES
    tq = min(TQ, S)
    nq = S // tq
    qc, kc, vc = 2 * P, 3 * P, 4 * P
    scale = 1.0 / math.sqrt(HEAD_DIM)

    def body(q_ref, k_ref, v_ref, do_ref, lse_ref, dl_ref, cq_ref, ck_ref,
             dq_ref, dk_ref, dv_ref, dcq_ref, dcn_ref, km_scr, dk_scr, dv_scr, dc_scr):
        j, i = pl.program_id(1), pl.program_id(2)
        lane = lax.broadcasted_iota(jnp.int32, (tq, LANES), 1)
        halves = (lane < HEAD_DIM, lane >= HEAD_DIM)

        @pl.when(jnp.logical_and(j == 0, i == 0))
        def _():
            dq_ref[...] = jnp.zeros(dq_ref.shape, F32)
            dcq_ref[...] = jnp.zeros(dcq_ref.shape, F32)

        @pl.when(i == 0)
        def _():
            k2 = k_ref[...].astype(BF16)
            km_scr[0] = jnp.where(halves[0], k2, jnp.zeros_like(k2))
            km_scr[1] = jnp.where(halves[1], k2, jnp.zeros_like(k2))
            dk_scr[...] = jnp.zeros(dk_scr.shape, F32)
            dv_scr[...] = jnp.zeros(dv_scr.shape, F32)
            dc_scr[...] = jnp.zeros(dc_scr.shape, F32)

        @pl.when(i >= j)
        def _():
            qs = (q_ref[...] * scale).astype(BF16)
            do2 = do_ref[...].astype(BF16)
            k2 = k_ref[...].astype(BF16)
            v2 = v_ref[...].astype(BF16)
            row = lax.broadcasted_iota(jnp.int32, (tq, tq), 0)
            col = lax.broadcasted_iota(jnp.int32, (tq, tq), 1)
            keep = jnp.logical_or(i > j, row >= col)
            r0 = pl.multiple_of(i * tq, tq)
            for a in range(2):
                qa = jnp.where(halves[a], qs, jnp.zeros_like(qs))
                doa = jnp.where(halves[a], do2, jnp.zeros_like(do2))
                s = lax.dot_general(qa, k2, NT, preferred_element_type=F32)
                s = s + (jnp.tile(cq_ref[a], (1, tq // LANES)) - jnp.tile(ck_ref[a], (tq // 8, 1)))
                s = jnp.where(keep, s, NEG_INF)
                p = jnp.exp(s - jnp.tile(lse_ref[a], (1, tq // LANES)))
                dp = lax.dot_general(doa, v2, NT, preferred_element_type=F32)
                ds = p * (dp - jnp.tile(dl_ref[a], (1, tq // LANES)))
                dsb = ds.astype(BF16)
                dv_scr[...] += lax.dot_general(p.astype(BF16), doa, TN_DIMS, preferred_element_type=F32)
                dk_scr[...] += lax.dot_general(dsb, qa, TN_DIMS, preferred_element_type=F32)
                dq_ref[pl.ds(r0, tq), :] += jnp.dot(dsb, km_scr[a], preferred_element_type=F32) * scale
                dc_scr[a] += jnp.sum(ds, axis=0, keepdims=True)
                dcq_ref[a, pl.ds(r0, tq), :] += jnp.sum(ds, axis=1, keepdims=True)

        @pl.when(i == nq - 1)
        def _():
            dk_ref[...] = dk_scr[...].astype(BF16)
            dv_ref[...] = dv_scr[...].astype(BF16)
            dcn_ref[...] = dc_scr[...]

    qrow = lambda p, j, i: jnp.maximum(i, j)
    return pl.pallas_call(
        body, name="attn_bwd", grid=(P, nq, nq),
        in_specs=[pl.BlockSpec((tq, LANES), lambda p, j, i: (qrow(p, j, i), qc + p)),
                  pl.BlockSpec((tq, LANES), lambda p, j, i: (j, kc + p)),
                  pl.BlockSpec((tq, LANES), lambda p, j, i: (j, vc + p)),
                  pl.BlockSpec((tq, LANES), lambda p, j, i: (qrow(p, j, i), P + p)),
                  pl.BlockSpec((2, tq, LANES), lambda p, j, i: (p, qrow(p, j, i), 0)),
                  pl.BlockSpec((2, tq, LANES), lambda p, j, i: (p, qrow(p, j, i), 0)),
                  pl.BlockSpec((2, tq, LANES), lambda p, j, i: (p, qrow(p, j, i), 0)),
                  pl.BlockSpec((2, 8, tq), lambda p, j, i: (p, 0, j))],
        out_specs=[pl.BlockSpec((S, LANES), lambda p, j, i: (0, p)),
                   pl.BlockSpec((tq, LANES), lambda p, j, i: (j, p)),
                   pl.BlockSpec((tq, LANES), lambda p, j, i: (j, p)),
                   pl.BlockSpec((2, S, LANES), lambda p, j, i: (p, 0, 0)),
                   pl.BlockSpec((2, 8, tq), lambda p, j, i: (p, 0, j))],
        out_shape=[jax.ShapeDtypeStruct((S, W), F32),
                   jax.ShapeDtypeStruct((S, W), BF16),
                   jax.ShapeDtypeStruct((S, W), BF16),
                   jax.ShapeDtypeStruct((2 * P, S, LANES), F32),
                   jax.ShapeDtypeStruct((2 * P, 8, S), F32)],
        scratch_shapes=[pltpu.VMEM((2, tq, LANES), BF16),
                        pltpu.VMEM((tq, LANES), F32),
                        pltpu.VMEM((tq, LANES), F32),
                        pltpu.VMEM((2, 8, tq), F32)],
        compiler_params=_params("parallel", "arbitrary", "arbitrary"),
    )(proj, proj, proj, da, lse, delta, ccol, crow)


def _fgate_bwd(dcq, dcn, z, bias, n_heads):
    S = z.shape[0]
    tb = min(TB, S)
    nb = S // tb

    def body(dcq_ref, dcn_ref, z_ref, b_ref, dz_ref, db_ref, t_scr):
        sub = lax.broadcasted_iota(jnp.int32, (LANES, S), 0)
        lane = lax.broadcasted_iota(jnp.int32, (S, LANES), 1)
        rows = jnp.zeros((LANES, S), F32)
        cols = jnp.zeros((S, LANES), F32)
        for h in range(n_heads):
            rows = jnp.where(sub == h, jnp.tile(dcn_ref[h], (LANES // 8, 1)), rows)
            cols = jnp.where(lane == h, dcq_ref[h], cols)
        t_scr[...] = cols - rows.T
        tri = (lax.broadcasted_iota(jnp.int32, (tb, tb), 1)
               >= lax.broadcasted_iota(jnp.int32, (tb, tb), 0)).astype(F32)

        def block(n, carry):
            carry_rc, db = carry
            r0 = pl.multiple_of((nb - 1 - n) * tb, tb)
            rc = jnp.dot(tri, t_scr[pl.ds(r0, tb), :], preferred_element_type=F32,
                         precision=lax.Precision.HIGHEST) + carry_rc
            zz = z_ref[pl.ds(r0, tb), :] + b_ref[...]
            dz = rc * _sigmoid(-zz)
            dz_ref[pl.ds(r0, tb), :] = dz.astype(BF16)
            return rc[0:1, :], db + jnp.sum(dz, axis=0, keepdims=True)

        zero = jnp.zeros((1, LANES), F32)
        _, db = lax.fori_loop(0, nb, block, (zero, zero))
        db_ref[...] = db

    return pl.pallas_call(
        body, name="fgate_bwd",
        out_shape=[jax.ShapeDtypeStruct((S, LANES), BF16),
                   jax.ShapeDtypeStruct((1, LANES), F32)],
        scratch_shapes=[pltpu.VMEM((S, LANES), F32)],
        compiler_params=pltpu.CompilerParams(vmem_limit_bytes=VMEM_LIMIT),
    )(dcq, dcn, z, bias)


def _pool_bwd(proj, da, pool_w, pool_scale):
    S = proj.shape[0]
    G = len(POOL_WINDOWS)

    def body(u_ref, dy_ref, w_ref, s_ref, du_ref, dw_ref, ds_ref, pad_ref):
        g = pl.program_id(0)
        for gi, w in enumerate(POOL_WINDOWS):
            @pl.when(g == gi)
            def _():
                d, cnt = _window_mean_minus_self(u_ref[...], pad_ref, w, S)
                db = d.astype(BF16)
                wb = w_ref[0].astype(BF16)
                yraw = jnp.dot(db, wb, preferred_element_type=F32)
                dy = dy_ref[...]
                ds_ref[...] = jnp.sum(dy * yraw, axis=0, keepdims=True)
                dzb = (dy * s_ref[...]).astype(BF16)
                dw_ref[0] = lax.dot_general(db, dzb, TN_DIMS, preferred_element_type=F32)
                dd = lax.dot_general(dzb, wb, NT, preferred_element_type=F32)
                pad_ref[0:S, :] = dd / cnt
                pad_ref[S:S + MAX_WINDOW, :] = jnp.zeros((MAX_WINDOW, LANES), F32)
                acc = -dd
                for j in range(w):
                    acc = acc + pad_ref[j:j + S, :]
                du_ref[...] = acc.astype(BF16)

    return pl.pallas_call(
        body, name="pool_bwd", grid=(G,),
        in_specs=[pl.BlockSpec((S, LANES), lambda g: (0, g)),
                  pl.BlockSpec((S, LANES), lambda g: (0, g)),
                  pl.BlockSpec((1, LANES, LANES), lambda g: (g, 0, 0)),
                  pl.BlockSpec((1, LANES), lambda g: (0, g))],
        out_specs=[pl.BlockSpec((S, LANES), lambda g: (0, g)),
                   pl.BlockSpec((1, LANES, LANES), lambda g: (g, 0, 0)),
                   pl.BlockSpec((1, LANES), lambda g: (0, g))],
        out_shape=[jax.ShapeDtypeStruct((S, G * LANES), BF16),
                   jax.ShapeDtypeStruct((G, LANES, LANES), F32),
                   jax.ShapeDtypeStruct((1, G * LANES), F32)],
        scratch_shapes=[pltpu.VMEM((S + MAX_WINDOW, LANES), F32)],
        compiler_params=_params("arbitrary"),
    )(proj, da, pool_w, pool_scale)


def _inproj_bwd_dx(dproj, dzf, wmain, wf, x, gam, g):
    S, D = x.shape
    N = wmain.shape[1]
    tm = min(TM // 2, S)

    def body(dp_ref, dz_ref, w_ref, wf_ref, x_ref, gam_ref, g_ref, dx_ref, dg_ref):
        @pl.when(pl.program_id(0) == 0)
        def _():
            dg_ref[...] = jnp.zeros(dg_ref.shape, F32)

        dh = lax.dot_general(dp_ref[...], w_ref[...], NT, preferred_element_type=F32)
        dh = dh + lax.dot_general(dz_ref[...], wf_ref[...], NT, preferred_element_type=F32)
        xf = x_ref[...]
        r = lax.rsqrt(jnp.mean(xf * xf, axis=-1, keepdims=True) + RMS_EPS)
        xhat = xf * r
        dg_ref[...] += jnp.sum(dh * xhat, axis=0, keepdims=True)
        dxhat = dh * gam_ref[...]
        dx_ref[...] = g_ref[...] + r * (dxhat - xhat * jnp.mean(dxhat * xhat, axis=-1, keepdims=True))

    return pl.pallas_call(
        body, name="inproj_bwd_dx", grid=(S // tm,),
        in_specs=[pl.BlockSpec((tm, N), lambda i: (i, 0)),
                  pl.BlockSpec((tm, LANES), lambda i: (i, 0)),
                  pl.BlockSpec((D, N), lambda i: (0, 0)),
                  pl.BlockSpec((D, LANES), lambda i: (0, 0)),
                  pl.BlockSpec((tm, D), lambda i: (i, 0)),
                  pl.BlockSpec((1, D), lambda i: (0, 0)),
                  pl.BlockSpec((tm, D), lambda i: (i, 0))],
        out_specs=[pl.BlockSpec((tm, D), lambda i: (i, 0)),
                   pl.BlockSpec((1, D), lambda i: (0, 0))],
        out_shape=[jax.ShapeDtypeStruct((S, D), F32),
                   jax.ShapeDtypeStruct((1, D), F32)],
        compiler_params=_params("arbitrary"),
    )(dproj, dzf, wmain, wf, x, gam, g)


def _adamw(w, m, v, gparts, name, rows):
    A, R, C = w.shape
    tr = min(rows, R)
    c1 = 1.0 / (1.0 - ADAM_B1 ** ADAM_STEP)
    c2 = 1.0 / (1.0 - ADAM_B2 ** ADAM_STEP)

    def body(w_ref, m_ref, v_ref, gp_ref, g_ref, d_ref, nm_ref, nv_ref):
        g = gp_ref[0, 0]
        for s in range(1, N_DEV):
            g = g + gp_ref[s, 0]
        nm = ADAM_B1 * m_ref[0] + (1.0 - ADAM_B1) * g
        nv = ADAM_B2 * v_ref[0] + (1.0 - ADAM_B2) * (g * g)
        g_ref[0] = g
        nm_ref[0] = nm
        nv_ref[0] = nv
        d_ref[0] = -ADAM_LR * ((nm * c1) / (jnp.sqrt(nv * c2) + ADAM_EPS) + ADAM_WD * w_ref[0])

    spec = pl.BlockSpec((1, tr, C), lambda a, r: (a, r, 0))
    shape = jax.ShapeDtypeStruct((A, R, C), F32)
    return pl.pallas_call(
        body, name=name, grid=(A, R // tr),
        in_specs=[spec, spec, spec, pl.BlockSpec((N_DEV, 1, tr, C), lambda a, r: (0, a, r, 0))],
        out_specs=[spec, spec, spec, spec],
        out_shape=[shape, shape, shape, shape],
        compiler_params=_params("parallel", "parallel"),
    )(w, m, v, gparts)


def _position():
    return lax.axis_index("x"), lax.axis_index("y"), lax.axis_index("c")


def _index(dev):
    return 4 * dev[0] + 2 * dev[1] + dev[2]


def _all_gather(arrs, slots, out_shapes, name):
    n_arr = len(arrs)

    def body(*refs):
        ins, outs = refs[:n_arr], refs[n_arr:2 * n_arr]
        send_sems, recv_sems, local_sems = refs[2 * n_arr:]
        x, y, c = _position()
        me, sibling = (x, y, c), (x, y, 1 - c)
        chips = [(1 - x, y), (x, 1 - y), (1 - x, 1 - y)]

        def copy(a, k, block, to, src=None):
            part = slots[a](outs[a], _index(block))
            return pltpu.make_async_remote_copy(
                src_ref=part if src is None else src, dst_ref=part,
                send_sem=send_sems.at[a, k], recv_sem=recv_sems.at[a, k],
                device_id=to, device_id_type=MESH)

        mine = [pltpu.make_async_copy(ins[a], slots[a](outs[a], _index(me)), local_sems.at[a])
                for a in range(n_arr)]
        for cp in mine:
            cp.start()
        first = []
        for a in range(n_arr):
            first.append(copy(a, 0, me, sibling, src=ins[a]))
            first += [copy(a, 1 + j, me, (*chip, c), src=ins[a]) for j, chip in enumerate(chips)]
        for cp in first:
            cp.start()
        passed = []
        for j, chip in enumerate(chips):
            for a in range(n_arr):
                copy(a, 1 + j, (*chip, c), me).wait_recv()
                fwd = copy(a, 4 + j, (*chip, c), sibling)
                fwd.start()
                passed.append(fwd)
        for a in range(n_arr):
            copy(a, 0, sibling, me).wait_recv()
            for j, chip in enumerate(chips):
                copy(a, 4 + j, (*chip, 1 - c), me).wait_recv()
        for cp in first + passed:
            cp.wait_send()
        for cp in mine:
            cp.wait()

    any_spec = pl.BlockSpec(memory_space=pl.ANY)
    return pl.pallas_call(
        body, name=name,
        in_specs=[any_spec] * n_arr, out_specs=[any_spec] * n_arr, out_shape=out_shapes,
        scratch_shapes=[pltpu.SemaphoreType.DMA((n_arr, 7)), pltpu.SemaphoreType.DMA((n_arr, 7)),
                        pltpu.SemaphoreType.DMA((n_arr,))],
    )(*arrs)


def _exchange(arrs, parts, out_shapes, name):
    n_arr = len(arrs)

    def body(*refs):
        ins, outs = refs[:n_arr], refs[n_arr:2 * n_arr]
        send_sems, recv_sems, local_sems = refs[2 * n_arr:]
        x, y, c = _position()
        me = (x, y, c)
        peers = [(x ^ ((k >> 2) & 1), y ^ ((k >> 1) & 1), c ^ (k & 1)) for k in range(1, N_DEV)]

        def copy(a, k, peer):
            return pltpu.make_async_remote_copy(
                src_ref=parts[a](ins[a], _index(peer)), dst_ref=outs[a].at[_index(me)],
                send_sem=send_sems.at[a, k], recv_sem=recv_sems.at[a, k],
                device_id=peer, device_id_type=MESH)

        def landed(a, k, peer):
            return pltpu.make_async_remote_copy(
                src_ref=parts[a](ins[a], _index(peer)), dst_ref=outs[a].at[_index(peer)],
                send_sem=send_sems.at[a, k], recv_sem=recv_sems.at[a, k],
                device_id=peer, device_id_type=MESH)

        mine = [pltpu.make_async_copy(parts[a](ins[a], _index(me)), outs[a].at[_index(me)], local_sems.at[a])
                for a in range(n_arr)]
        for cp in mine:
            cp.start()
        sends = [copy(a, k, peer) for a in range(n_arr) for k, peer in enumerate(peers)]
        for cp in sends:
            cp.start()
        for a in range(n_arr):
            for k, peer in enumerate(peers):
                landed(a, k, peer).wait_recv()
        for cp in sends:
            cp.wait_send()
        for cp in mine:
            cp.wait()

    any_spec = pl.BlockSpec(memory_space=pl.ANY)
    return pl.pallas_call(
        body, name=name,
        in_specs=[any_spec] * n_arr, out_specs=[any_spec] * n_arr, out_shape=out_shapes,
        scratch_shapes=[pltpu.SemaphoreType.DMA((n_arr, 7)), pltpu.SemaphoreType.DMA((n_arr, 7)),
                        pltpu.SemaphoreType.DMA((n_arr,))],
    )(*arrs)


def _pack(parts, rows):
    flat = jnp.concatenate([p.reshape(-1) for p in parts])
    return jnp.pad(flat, (0, rows * LANES - flat.shape[0])).reshape(1, rows, LANES)


def _unpack(packed, like):
    flat = packed.reshape(-1)
    out, off = [], 0
    for p in like:
        out.append(flat[off:off + p.size].reshape(p.shape))
        off += p.size
    return out


def _local_step(x, target, norm_g, wmain, wf, forget_bias, pool_w, pool_scale, wout, final_g):
    L = norm_g.shape[0]
    S, D = x.shape
    W = D // 2
    H = W // HEAD_DIM
    bias = jnp.pad(forget_bias, ((0, 0), (0, LANES - H)))

    saved = []
    for l in range(L):
        proj, h, z = _inproj_fwd(x, norm_g[l:l + 1], wmain[l], wf[l])
        ccol, crow = _fgate_fwd(z, bias[l:l + 1], H)
        ypool = _pool_fwd(proj, pool_w[l], pool_scale[l:l + 1])
        o, lse = _attn_fwd(proj, ccol, crow)
        x_new, mixed = _outproj_fwd(ypool, o, proj, x, wout[l])
        saved.append((x, proj, h, z, ccol, crow, ypool, o, lse, mixed))
        x = x_new

    g, loss, d_final_g = _loss_head(x, final_g.reshape(1, D), target)

    d_norm_g, d_wmain, d_wf, d_bias, d_pool_w, d_pool_scale, d_wout = [], [], [], [], [], [], []
    for l in reversed(range(L)):
        x_in, proj, h, z, ccol, crow, ypool, o, lse, mixed = saved[l]
        d_wout.append(_matmul_tn(mixed, g, "outproj_bwd_dw"))
        da, dgate = _outproj_bwd_dx(g, wout[l], ypool, o, proj)
        delta = _attn_delta(da, o)
        dq, dk, dv, dcq, dcn = _attn_bwd(proj, da, lse, delta, ccol, crow)
        dzf, db = _fgate_bwd(dcq, dcn, z, bias[l:l + 1], H)
        dpu, dpw, dps = _pool_bwd(proj, da, pool_w[l], pool_scale[l:l + 1])
        dproj = jnp.concatenate([dpu, dgate[:, :W], dq.astype(BF16), dk, dv, dgate[:, W:]], axis=1)
        d_wmain.append(_matmul_tn(h, dproj, "inproj_bwd_dw"))
        d_wf.append(_matmul_tn(h, dzf, "inproj_bwd_dwf"))
        g, dgam = _inproj_bwd_dx(dproj, dzf, wmain[l], wf[l], x_in, norm_g[l:l + 1], g)
        d_norm_g.append(dgam[0])
        d_bias.append(db[0, :H])
        d_pool_w.append(dpw)
        d_pool_scale.append(dps[0])

    stack = lambda parts: jnp.stack(parts[::-1])
    grads = dict(norm_g=stack(d_norm_g), wmain=stack(d_wmain), wf=stack(d_wf)[:, :, :H],
                 forget_bias=stack(d_bias), pool_w=stack(d_pool_w), pool_scale=stack(d_pool_scale),
                 w_out=stack(d_wout), final_g=d_final_g[0])
    return loss[0, 0], g, grads


def kernel(x, norm_g, w_in, forget_bias, pool_w, pool_scale, w_out, final_g, loss_target, m_norm_g, m_w_in, m_forget_bias, m_pool_w, m_pool_scale, m_w_out, m_final_g, v_norm_g, v_w_in, v_forget_bias, v_pool_w, v_pool_scale, v_w_out, v_final_g):
    L, D, cols = w_in.shape
    rows_out = w_out.shape[1]
    W = D // 2
    H = W // HEAD_DIM
    n_cols = N_DEV * cols

    win_all, wout_all = _all_gather(
        [w_in.astype(BF16), w_out.astype(BF16)],
        [lambda ref, n: ref.at[n], lambda ref, n: ref.at[:, pl.ds(n * rows_out, rows_out), :]],
        [jax.ShapeDtypeStruct((N_DEV, L, D, cols), BF16), jax.ShapeDtypeStruct((L, D, D), BF16)],
        "gather_weights")
    w_full = jnp.transpose(win_all, (1, 2, 0, 3)).reshape(L, D, n_cols)
    wmain = w_full[:, :, :6 * W]
    wf = jnp.pad(w_full[:, :, 6 * W:], ((0, 0), (0, 0), (0, LANES - H)))

    loss, dx, grads = _local_step(x[0], loss_target[0], norm_g, wmain, wf, forget_bias, pool_w,
                                  pool_scale, wout_all, final_g)
    loss = lax.psum(loss, ("x", "y", "c"))

    dw_in = jnp.concatenate([grads["wmain"], grads["wf"]], axis=2)
    dw_in = jnp.transpose(dw_in.reshape(L, D, N_DEV, cols), (2, 0, 1, 3))
    gin_parts, gout_parts = _exchange(
        [dw_in, grads["w_out"]],
        [lambda ref, n: ref.at[n], lambda ref, n: ref.at[:, pl.ds(n * rows_out, rows_out), :]],
        [jax.ShapeDtypeStruct((N_DEV, L, D, cols), F32), jax.ShapeDtypeStruct((N_DEV, L, rows_out, D), F32)],
        "exchange_grads")
    g_w_in, d_w_in, nm_w_in, nv_w_in = _adamw(w_in, m_w_in, v_w_in, gin_parts, "adamw_w_in", 256)
    g_w_out, d_w_out, nm_w_out, nv_w_out = _adamw(w_out, m_w_out, v_w_out, gout_parts, "adamw_w_out", 128)

    names = ["norm_g", "forget_bias", "pool_w", "pool_scale", "final_g"]
    small_w = [norm_g, forget_bias, pool_w, pool_scale, final_g]
    small_m = [m_norm_g, m_forget_bias, m_pool_w, m_pool_scale, m_final_g]
    small_v = [v_norm_g, v_forget_bias, v_pool_w, v_pool_scale, v_final_g]
    total = sum(p.size for p in small_w)
    rows = -(-total // (8 * LANES)) * 8
    (gs_all,) = _all_gather(
        [_pack([grads[n] for n in names], rows)],
        [lambda ref, n: ref.at[n]],
        [jax.ShapeDtypeStruct((N_DEV, 1, rows, LANES), F32)], "gather_small_grads")
    packed = _adamw(_pack(small_w, rows), _pack(small_m, rows), _pack(small_v, rows), gs_all, "adamw_small", rows)
    g_s, d_s, nm_s, nv_s = [_unpack(p, small_w) for p in packed]

    def order(big_in, big_out, small):
        return (small[0], big_in, small[1], small[2], small[3], big_out, small[4])

    return (loss, dx[None], *order(g_w_in, g_w_out, g_s), *order(d_w_in, d_w_out, d_s),
            *order(nm_w_in, nm_w_out, nm_s), *order(nv_w_in, nv_w_out, nv_s))
```

```python
import functools
import math

import jax
import jax.numpy as jnp
from jax import lax
from jax.experimental import pallas as pl
from jax.experimental.pallas import tpu as pltpu

F32 = jnp.float32
BF16 = jnp.bfloat16
MESH = pl.DeviceIdType.MESH

RMS_EPS = 1e-6
NEG_INF = -1e30
HEAD_DIM = 64
POOL_WINDOWS = (2, 4, 8, 16)
MAX_WINDOW = 16
LANES = 128
N_DEV = 8

ADAM_LR = 0.001
ADAM_B1 = 0.9
ADAM_B2 = 0.999
ADAM_EPS = 1e-08
ADAM_WD = 0.01
ADAM_STEP = 10

TM = 512
TN = 512
TQ = 512
TB = 256
VMEM_LIMIT = 56 * 1024 * 1024

NT = (((1,), (1,)), ((), ()))
TN_DIMS = (((0,), (0,)), ((), ()))


def _params(*sem):
    return pltpu.CompilerParams(dimension_semantics=sem, vmem_limit_bytes=VMEM_LIMIT)


def _sigmoid(x):
    return 1.0 / (1.0 + jnp.exp(-x))


def _inproj_fwd(x, gam, wmain, wf, after=()):
    S, D = x.shape
    N = wmain.shape[1]
    tm, tn = min(TM, S), TN

    def body(x_ref, g_ref, w_ref, wf_ref, *rest):
        proj_ref, h_ref, z_ref = rest[-3:]

        @pl.when(pl.program_id(1) == 0)
        def _():
            xf = x_ref[...]
            r = lax.rsqrt(jnp.mean(xf * xf, axis=-1, keepdims=True) + RMS_EPS)
            h = ((xf * r) * g_ref[...]).astype(BF16)
            h_ref[...] = h
            z_ref[...] = jnp.dot(h, wf_ref[...], preferred_element_type=F32)

        proj_ref[...] = jnp.dot(h_ref[...], w_ref[...], preferred_element_type=F32)

    return pl.pallas_call(
        body, name="inproj_fwd", grid=(S // tm, N // tn),
        in_specs=[pl.BlockSpec((tm, D), lambda i, j: (i, 0)),
                  pl.BlockSpec((1, D), lambda i, j: (0, 0)),
                  pl.BlockSpec((D, tn), lambda i, j: (0, j)),
                  pl.BlockSpec((D, LANES), lambda i, j: (0, 0))] + [pl.BlockSpec(memory_space=pl.ANY)] * len(after),
        out_specs=[pl.BlockSpec((tm, tn), lambda i, j: (i, j)),
                   pl.BlockSpec((tm, D), lambda i, j: (i, 0)),
                   pl.BlockSpec((tm, LANES), lambda i, j: (i, 0))],
        out_shape=[jax.ShapeDtypeStruct((S, N), F32),
                   jax.ShapeDtypeStruct((S, D), BF16),
                   jax.ShapeDtypeStruct((S, LANES), F32)],
        compiler_params=_params("parallel", "arbitrary"),
    )(x, gam, wmain, wf, *after)


def _fgate_fwd(z, bias, n_heads):
    S = z.shape[0]
    tb = min(TB, S)

    def body(z_ref, b_ref, ccol_ref, crow_ref, c_scr):
        lane = lax.broadcasted_iota(jnp.int32, (tb, LANES), 1)
        tri = (lax.broadcasted_iota(jnp.int32, (tb, tb), 0)
               >= lax.broadcasted_iota(jnp.int32, (tb, tb), 1)).astype(F32)

        def block(i, carry):
            r0 = pl.multiple_of(i * tb, tb)
            zz = z_ref[pl.ds(r0, tb), :] + b_ref[...]
            lf = jnp.minimum(zz, 0.0) - jnp.log(1.0 + jnp.exp(-jnp.abs(zz)))
            lf = jnp.where(lane < n_heads, lf, 0.0)
            c = jnp.dot(tri, lf, preferred_element_type=F32, precision=lax.Precision.HIGHEST) + carry
            c_scr[pl.ds(r0, tb), :] = c
            return c[tb - 1:tb, :]

        lax.fori_loop(0, S // tb, block, jnp.zeros((1, LANES), F32))
        c = c_scr[...]
        ct = c.T
        for h in range(n_heads):
            ccol_ref[h] = jnp.broadcast_to(c[:, h:h + 1], (S, LANES))
            crow_ref[h] = jnp.broadcast_to(ct[h:h + 1, :], (8, S))

    return pl.pallas_call(
        body, name="fgate_fwd",
        out_shape=[jax.ShapeDtypeStruct((n_heads, S, LANES), F32),
                   jax.ShapeDtypeStruct((n_heads, 8, S), F32)],
        scratch_shapes=[pltpu.VMEM((S, LANES), F32)],
        compiler_params=pltpu.CompilerParams(vmem_limit_bytes=VMEM_LIMIT),
    )(z, bias)


def _window_mean_minus_self(u, pad_ref, w, S):
    pad_ref[0:MAX_WINDOW, :] = jnp.zeros((MAX_WINDOW, LANES), F32)
    pad_ref[MAX_WINDOW:MAX_WINDOW + S, :] = u
    acc = u
    for j in range(1, w):
        acc = acc + pad_ref[MAX_WINDOW - j:MAX_WINDOW - j + S, :]
    t = lax.broadcasted_iota(jnp.int32, (S, LANES), 0)
    cnt = jnp.minimum(t + 1, w).astype(F32)
    return acc / cnt - u, cnt


def _pool_fwd(proj, pool_w, pool_scale):
    S = proj.shape[0]
    G = len(POOL_WINDOWS)

    def body(u_ref, w_ref, s_ref, y_ref, pad_ref):
        g = pl.program_id(0)
        for gi, w in enumerate(POOL_WINDOWS):
            @pl.when(g == gi)
            def _():
                d, _ = _window_mean_minus_self(u_ref[...], pad_ref, w, S)
                y = jnp.dot(d.astype(BF16), w_ref[0].astype(BF16), preferred_element_type=F32)
                y_ref[...] = y * s_ref[...]

    return pl.pallas_call(
        body, name="pool_fwd", grid=(G,),
        in_specs=[pl.BlockSpec((S, LANES), lambda g: (0, g)),
                  pl.BlockSpec((1, LANES, LANES), lambda g: (g, 0, 0)),
                  pl.BlockSpec((1, LANES), lambda g: (0, g))],
        out_specs=pl.BlockSpec((S, LANES), lambda g: (0, g)),
        out_shape=jax.ShapeDtypeStruct((S, G * LANES), F32),
        scratch_shapes=[pltpu.VMEM((S + MAX_WINDOW, LANES), F32)],
        compiler_params=_params("arbitrary"),
    )(proj, pool_w, pool_scale)


def _attn_fwd(proj, ccol, crow):
    S = proj.shape[0]
    W = proj.shape[1] // 6
    P = W // LANES
    tq = min(TQ, S)
    nq = S // tq
    qc, kc, vc = 2 * P, 3 * P, 4 * P
    scale = 1.0 / math.sqrt(HEAD_DIM)

    def body(q_ref, k_ref, v_ref, cq_ref, ck_ref, o_ref, lse_ref, qm_scr, m_scr, l_scr, acc_scr):
        i, j = pl.program_id(1), pl.program_id(2)
        lane = lax.broadcasted_iota(jnp.int32, (tq, LANES), 1)

        @pl.when(j == 0)
        def _():
            qs = (q_ref[...] * scale).astype(BF16)
            qm_scr[0] = jnp.where(lane < HEAD_DIM, qs, jnp.zeros_like(qs))
            qm_scr[1] = jnp.where(lane >= HEAD_DIM, qs, jnp.zeros_like(qs))
            m_scr[...] = jnp.full(m_scr.shape, NEG_INF, F32)
            l_scr[...] = jnp.zeros(l_scr.shape, F32)
            acc_scr[...] = jnp.zeros(acc_scr.shape, F32)

        @pl.when(j <= i)
        def _():
            k2 = k_ref[...].astype(BF16)
            v2 = v_ref[...].astype(BF16)
            row = lax.broadcasted_iota(jnp.int32, (tq, tq), 0)
            col = lax.broadcasted_iota(jnp.int32, (tq, tq), 1)
            keep = jnp.logical_or(j < i, row >= col)
            for a in range(2):
                s = lax.dot_general(qm_scr[a], k2, NT, preferred_element_type=F32)
                s = s + (jnp.tile(cq_ref[a], (1, tq // LANES)) - jnp.tile(ck_ref[a], (tq // 8, 1)))
                s = jnp.where(keep, s, NEG_INF)
                m_prev = m_scr[a]
                m_new = jnp.maximum(m_prev, jnp.max(s, axis=1, keepdims=True))
                p = jnp.exp(s - jnp.tile(m_new, (1, tq // LANES)))
                alpha = jnp.exp(m_prev - m_new)
                l_scr[a] = alpha * l_scr[a] + jnp.sum(p, axis=1, keepdims=True)
                acc_scr[a] = alpha * acc_scr[a] + jnp.dot(p.astype(BF16), v2, preferred_element_type=F32)
                m_scr[a] = m_new

        @pl.when(j == i)
        def _():
            o_ref[...] = jnp.where(lane < HEAD_DIM, acc_scr[0] / l_scr[0], acc_scr[1] / l_scr[1])
            lse_ref[0] = m_scr[0] + jnp.log(l_scr[0])
            lse_ref[1] = m_scr[1] + jnp.log(l_scr[1])

    return pl.pallas_call(
        body, name="attn_fwd", grid=(P, nq, nq),
        in_specs=[pl.BlockSpec((tq, LANES), lambda p, i, j: (i, qc + p)),
                  pl.BlockSpec((tq, LANES), lambda p, i, j: (jnp.minimum(j, i), kc + p)),
                  pl.BlockSpec((tq, LANES), lambda p, i, j: (jnp.minimum(j, i), vc + p)),
                  pl.BlockSpec((2, tq, LANES), lambda p, i, j: (p, i, 0)),
                  pl.BlockSpec((2, 8, tq), lambda p, i, j: (p, 0, jnp.minimum(j, i)))],
        out_specs=[pl.BlockSpec((tq, LANES), lambda p, i, j: (i, p)),
                   pl.BlockSpec((2, tq, LANES), lambda p, i, j: (p, i, 0))],
        out_shape=[jax.ShapeDtypeStruct((S, W), F32),
                   jax.ShapeDtypeStruct((2 * P, S, LANES), F32)],
        scratch_shapes=[pltpu.VMEM((2, tq, LANES), BF16),
                        pltpu.VMEM((2, tq, LANES), F32),
                        pltpu.VMEM((2, tq, LANES), F32),
                        pltpu.VMEM((2, tq, LANES), F32)],
        compiler_params=_params("parallel", "parallel", "arbitrary"),
    )(proj, proj, proj, ccol, crow)


def _outproj_fwd(ypool, o, proj, x, wout):
    S, D = x.shape
    W = D // 2
    tm, tn = min(TM, S), TN

    def body(y_ref, o_ref, pg_ref, ag_ref, x_ref, w_ref, xn_ref, mix_ref):
        @pl.when(pl.program_id(1) == 0)
        def _():
            pg, ag = pg_ref[...], ag_ref[...]
            mix_ref[:, 0:W] = (y_ref[...] * (pg * _sigmoid(pg))).astype(BF16)
            mix_ref[:, W:D] = (o_ref[...] * (ag * _sigmoid(ag))).astype(BF16)

        xn_ref[...] = x_ref[...] + jnp.dot(mix_ref[...], w_ref[...], preferred_element_type=F32)

    return pl.pallas_call(
        body, name="outproj_fwd", grid=(S // tm, D // tn),
        in_specs=[pl.BlockSpec((tm, W), lambda i, j: (i, 0)),
                  pl.BlockSpec((tm, W), lambda i, j: (i, 0)),
                  pl.BlockSpec((tm, W), lambda i, j: (i, 1)),
                  pl.BlockSpec((tm, W), lambda i, j: (i, 5)),
                  pl.BlockSpec((tm, tn), lambda i, j: (i, j)),
                  pl.BlockSpec((D, tn), lambda i, j: (0, j))],
        out_specs=[pl.BlockSpec((tm, tn), lambda i, j: (i, j)),
                   pl.BlockSpec((tm, D), lambda i, j: (i, 0))],
        out_shape=[jax.ShapeDtypeStruct((S, D), F32),
                   jax.ShapeDtypeStruct((S, D), BF16)],
        compiler_params=_params("parallel", "arbitrary"),
    )(ypool, o, proj, proj, x, wout)


def _loss_head(x, gam, target):
    S, D = x.shape
    tm = min(TM, S)

    def body(x_ref, g_ref, t_ref, dx_ref, loss_ref, dg_ref):
        @pl.when(pl.program_id(0) == 0)
        def _():
            loss_ref[...] = jnp.zeros(loss_ref.shape, F32)
            dg_ref[...] = jnp.zeros(dg_ref.shape, F32)

        xf, gam_v = x_ref[...], g_ref[...]
        r = lax.rsqrt(jnp.mean(xf * xf, axis=-1, keepdims=True) + RMS_EPS)
        xhat = xf * r
        err = xhat * gam_v - t_ref[...]
        part = jnp.sum(jnp.sum(err * err, axis=-1, keepdims=True), axis=0, keepdims=True)
        loss_ref[...] += part * (0.5 / D)
        dy = err * (1.0 / D)
        dg_ref[...] += jnp.sum(dy * xhat, axis=0, keepdims=True)
        dxhat = dy * gam_v
        dx_ref[...] = r * (dxhat - xhat * jnp.mean(dxhat * xhat, axis=-1, keepdims=True))

    return pl.pallas_call(
        body, name="loss_head", grid=(S // tm,),
        in_specs=[pl.BlockSpec((tm, D), lambda i: (i, 0)),
                  pl.BlockSpec((1, D), lambda i: (0, 0)),
                  pl.BlockSpec((tm, D), lambda i: (i, 0))],
        out_specs=[pl.BlockSpec((tm, D), lambda i: (i, 0)),
                   pl.BlockSpec((8, LANES), lambda i: (0, 0)),
                   pl.BlockSpec((1, D), lambda i: (0, 0))],
        out_shape=[jax.ShapeDtypeStruct((S, D), F32),
                   jax.ShapeDtypeStruct((8, LANES), F32),
                   jax.ShapeDtypeStruct((1, D), F32)],
        compiler_params=_params("arbitrary"),
    )(x, gam, target)


def _outproj_bwd_dx(g, wout, ypool, o, proj):
    S, D = g.shape
    W = D // 2
    tm = min(TM, S)

    def body(g_ref, w_ref, y_ref, o_ref, gate_ref, da_ref, dgate_ref):
        j = pl.program_id(1)
        dmix = lax.dot_general(g_ref[...].astype(BF16), w_ref[...], NT, preferred_element_type=F32)
        val = jnp.where(j == 0, y_ref[...], o_ref[...])
        gt = gate_ref[...]
        sg = _sigmoid(gt)
        da_ref[...] = dmix * (gt * sg)
        dgate_ref[...] = (dmix * val * (sg * (1.0 + gt * (1.0 - sg)))).astype(BF16)

    return pl.pallas_call(
        body, name="outproj_bwd_dx", grid=(S // tm, 2),
        in_specs=[pl.BlockSpec((tm, D), lambda i, j: (i, 0)),
                  pl.BlockSpec((W, D), lambda i, j: (j, 0)),
                  pl.BlockSpec((tm, W), lambda i, j: (i, 0)),
                  pl.BlockSpec((tm, W), lambda i, j: (i, 0)),
                  pl.BlockSpec((tm, W), lambda i, j: (i, 1 + 4 * j))],
        out_specs=[pl.BlockSpec((tm, W), lambda i, j: (i, j)),
                   pl.BlockSpec((tm, W), lambda i, j: (i, j))],
        out_shape=[jax.ShapeDtypeStruct((S, D), F32),
                   jax.ShapeDtypeStruct((S, D), BF16)],
        compiler_params=_params("parallel", "arbitrary"),
    )(g, wout, ypool, o, proj)


def _matmul_tn(a, b, name, after=()):
    S, M = a.shape
    N = b.shape[1]
    ts, tn = min(TM, S), min(TN, N)

    def body(a_ref, b_ref, *rest):
        out_ref = rest[-1]

        @pl.when(pl.program_id(1) == 0)
        def _():
            out_ref[...] = jnp.zeros(out_ref.shape, F32)

        out_ref[...] += lax.dot_general(a_ref[...], b_ref[...].astype(BF16), TN_DIMS,
                                        preferred_element_type=F32)

    return pl.pallas_call(
        body, name=name, grid=(N // tn, S // ts),
        in_specs=[pl.BlockSpec((ts, M), lambda n, k: (k, 0)),
                  pl.BlockSpec((ts, tn), lambda n, k: (k, n))] + [pl.BlockSpec(memory_space=pl.ANY)] * len(after),
        out_specs=pl.BlockSpec((M, tn), lambda n, k: (0, n)),
        out_shape=jax.ShapeDtypeStruct((M, N), F32),
        compiler_params=_params("parallel", "arbitrary"),
    )(a, b, *after)


def _attn_delta(da, o):
    S, D = da.shape
    W = D // 2
    P = W // LANES
    tq = min(TQ, S)

    def body(do_ref, o_ref, dl_ref):
        lane = lax.broadcasted_iota(jnp.int32, (tq, LANES), 1)
        prod = do_ref[...] * o_ref[...]
        first = jnp.sum(jnp.where(lane < HEAD_DIM, prod, 0.0), axis=1, keepdims=True)
        second = jnp.sum(jnp.where(lane >= HEAD_DIM, prod, 0.0), axis=1, keepdims=True)
        dl_ref[0] = jnp.broadcast_to(first, (tq, LANES))
        dl_ref[1] = jnp.broadcast_to(second, (tq, LANES))

    return pl.pallas_call(
        body, name="attn_delta", grid=(P, S // tq),
        in_specs=[pl.BlockSpec((tq, LANES), lambda p, i: (i, P + p)),
                  pl.BlockSpec((tq, LANES), lambda p, i: (i, p))],
        out_specs=pl.BlockSpec((2, tq, LANES), lambda p, i: (p, i, 0)),
        out_shape=jax.ShapeDtypeStruct((2 * P, S, LANES), F32),
        compiler_params=_params("parallel", "parallel"),
    )(da, o)


def _attn_bwd(proj, da, lse, delta, ccol, crow):
    S = proj.shape[0]
    W = proj.shape[1] // 6
    P = W // LANES
    tq = min(TQ, S)
    nq = S // tq
    qc, kc, vc = 2 * P, 3 * P, 4 * P
    scale = 1.0 / math.sqrt(HEAD_DIM)

    def body(q_ref, k_ref, v_ref, do_ref, lse_ref, dl_ref, cq_ref, ck_ref,
             dq_ref, dk_ref, dv_ref, dcq_ref, dcn_ref, km_scr, dk_scr, dv_scr, dc_scr):
        j, i = pl.program_id(1), pl.program_id(2)
        lane = lax.broadcasted_iota(jnp.int32, (tq, LANES), 1)
        halves = (lane < HEAD_DIM, lane >= HEAD_DIM)

        @pl.when(jnp.logical_and(j == 0, i == 0))
        def _():
            dq_ref[...] = jnp.zeros(dq_ref.shape, F32)
            dcq_ref[...] = jnp.zeros(dcq_ref.shape, F32)

        @pl.when(i == 0)
        def _():
            k2 = k_ref[...].astype(BF16)
            km_scr[0] = jnp.where(halves[0], k2, jnp.zeros_like(k2))
            km_scr[1] = jnp.where(halves[1], k2, jnp.zeros_like(k2))
            dk_scr[...] = jnp.zeros(dk_scr.shape, F32)
            dv_scr[...] = jnp.zeros(dv_scr.shape, F32)
            dc_scr[...] = jnp.zeros(dc_scr.shape, F32)

        @pl.when(i >= j)
        def _():
            qs = (q_ref[...] * scale).astype(BF16)
            do2 = do_ref[...].astype(BF16)
            k2 = k_ref[...].astype(BF16)
            v2 = v_ref[...].astype(BF16)
            row = lax.broadcasted_iota(jnp.int32, (tq, tq), 0)
            col = lax.broadcasted_iota(jnp.int32, (tq, tq), 1)
            keep = jnp.logical_or(i > j, row >= col)
            r0 = pl.multiple_of(i * tq, tq)
            for a in range(2):
                qa = jnp.where(halves[a], qs, jnp.zeros_like(qs))
                doa = jnp.where(halves[a], do2, jnp.zeros_like(do2))
                s = lax.dot_general(qa, k2, NT, preferred_element_type=F32)
                s = s + (jnp.tile(cq_ref[a], (1, tq // LANES)) - jnp.tile(ck_ref[a], (tq // 8, 1)))
                s = jnp.where(keep, s, NEG_INF)
                p = jnp.exp(s - jnp.tile(lse_ref[a], (1, tq // LANES)))
                dp = lax.dot_general(doa, v2, NT, preferred_element_type=F32)
                ds = p * (dp - jnp.tile(dl_ref[a], (1, tq // LANES)))
                dsb = ds.astype(BF16)
                dv_scr[...] += lax.dot_general(p.astype(BF16), doa, TN_DIMS, preferred_element_type=F32)
                dk_scr[...] += lax.dot_general(dsb, qa, TN_DIMS, preferred_element_type=F32)
                dq_ref[pl.ds(r0, tq), :] += jnp.dot(dsb, km_scr[a], preferred_element_type=F32) * scale
                dc_scr[a] += jnp.sum(ds, axis=0, keepdims=True)
                dcq_ref[a, pl.ds(r0, tq), :] += jnp.sum(ds, axis=1, keepdims=True)

        @pl.when(i == nq - 1)
        def _():
            dk_ref[...] = dk_scr[...].astype(BF16)
            dv_ref[...] = dv_scr[...].astype(BF16)
            dcn_ref[...] = dc_scr[...]

    qrow = lambda p, j, i: jnp.maximum(i, j)
    return pl.pallas_call(
        body, name="attn_bwd", grid=(P, nq, nq),
        in_specs=[pl.BlockSpec((tq, LANES), lambda p, j, i: (qrow(p, j, i), qc + p)),
                  pl.BlockSpec((tq, LANES), lambda p, j, i: (j, kc + p)),
                  pl.BlockSpec((tq, LANES), lambda p, j, i: (j, vc + p)),
                  pl.BlockSpec((tq, LANES), lambda p, j, i: (qrow(p, j, i), P + p)),
                  pl.BlockSpec((2, tq, LANES), lambda p, j, i: (p, qrow(p, j, i), 0)),
                  pl.BlockSpec((2, tq, LANES), lambda p, j, i: (p, qrow(p, j, i), 0)),
                  pl.BlockSpec((2, tq, LANES), lambda p, j, i: (p, qrow(p, j, i), 0)),
                  pl.BlockSpec((2, 8, tq), lambda p, j, i: (p, 0, j))],
        out_specs=[pl.BlockSpec((S, LANES), lambda p, j, i: (0, p)),
                   pl.BlockSpec((tq, LANES), lambda p, j, i: (j, p)),
                   pl.BlockSpec((tq, LANES), lambda p, j, i: (j, p)),
                   pl.BlockSpec((2, S, LANES), lambda p, j, i: (p, 0, 0)),
                   pl.BlockSpec((2, 8, tq), lambda p, j, i: (p, 0, j))],
        out_shape=[jax.ShapeDtypeStruct((S, W), F32),
                   jax.ShapeDtypeStruct((S, W), BF16),
                   jax.ShapeDtypeStruct((S, W), BF16),
                   jax.ShapeDtypeStruct((2 * P, S, LANES), F32),
                   jax.ShapeDtypeStruct((2 * P, 8, S), F32)],
        scratch_shapes=[pltpu.VMEM((2, tq, LANES), BF16),
                        pltpu.VMEM((tq, LANES), F32),
                        pltpu.VMEM((tq, LANES), F32),
                        pltpu.VMEM((2, 8, tq), F32)],
        compiler_params=_params("parallel", "arbitrary", "arbitrary"),
    )(proj, proj, proj, da, lse, delta, ccol, crow)


def _fgate_bwd(dcq, dcn, z, bias, n_heads):
    S = z.shape[0]
    tb = min(TB, S)
    nb = S // tb

    def body(dcq_ref, dcn_ref, z_ref, b_ref, dz_ref, db_ref, t_scr):
        sub = lax.broadcasted_iota(jnp.int32, (LANES, S), 0)
        lane = lax.broadcasted_iota(jnp.int32, (S, LANES), 1)
        rows = jnp.zeros((LANES, S), F32)
        cols = jnp.zeros((S, LANES), F32)
        for h in range(n_heads):
            rows = jnp.where(sub == h, jnp.tile(dcn_ref[h], (LANES // 8, 1)), rows)
            cols = jnp.where(lane == h, dcq_ref[h], cols)
        t_scr[...] = cols - rows.T
        tri = (lax.broadcasted_iota(jnp.int32, (tb, tb), 1)
               >= lax.broadcasted_iota(jnp.int32, (tb, tb), 0)).astype(F32)

        def block(n, carry):
            carry_rc, db = carry
            r0 = pl.multiple_of((nb - 1 - n) * tb, tb)
            rc = jnp.dot(tri, t_scr[pl.ds(r0, tb), :], preferred_element_type=F32,
                         precision=lax.Precision.HIGHEST) + carry_rc
            zz = z_ref[pl.ds(r0, tb), :] + b_ref[...]
            dz = rc * _sigmoid(-zz)
            dz_ref[pl.ds(r0, tb), :] = dz.astype(BF16)
            return rc[0:1, :], db + jnp.sum(dz, axis=0, keepdims=True)

        zero = jnp.zeros((1, LANES), F32)
        _, db = lax.fori_loop(0, nb, block, (zero, zero))
        db_ref[...] = db

    return pl.pallas_call(
        body, name="fgate_bwd",
        out_shape=[jax.ShapeDtypeStruct((S, LANES), BF16),
                   jax.ShapeDtypeStruct((1, LANES), F32)],
        scratch_shapes=[pltpu.VMEM((S, LANES), F32)],
        compiler_params=pltpu.CompilerParams(vmem_limit_bytes=VMEM_LIMIT),
    )(dcq, dcn, z, bias)


def _pool_bwd(proj, da, pool_w, pool_scale):
    S = proj.shape[0]
    G = len(POOL_WINDOWS)

    def body(u_ref, dy_ref, w_ref, s_ref, du_ref, dw_ref, ds_ref, pad_ref):
        g = pl.program_id(0)
        for gi, w in enumerate(POOL_WINDOWS):
            @pl.when(g == gi)
            def _():
                d, cnt = _window_mean_minus_self(u_ref[...], pad_ref, w, S)
                db = d.astype(BF16)
                wb = w_ref[0].astype(BF16)
                yraw = jnp.dot(db, wb, preferred_element_type=F32)
                dy = dy_ref[...]
                ds_ref[...] = jnp.sum(dy * yraw, axis=0, keepdims=True)
                dzb = (dy * s_ref[...]).astype(BF16)
                dw_ref[0] = lax.dot_general(db, dzb, TN_DIMS, preferred_element_type=F32)
                dd = lax.dot_general(dzb, wb, NT, preferred_element_type=F32)
                pad_ref[0:S, :] = dd / cnt
                pad_ref[S:S + MAX_WINDOW, :] = jnp.zeros((MAX_WINDOW, LANES), F32)
                acc = -dd
                for j in range(w):
                    acc = acc + pad_ref[j:j + S, :]
                du_ref[...] = acc.astype(BF16)

    return pl.pallas_call(
        body, name="pool_bwd", grid=(G,),
        in_specs=[pl.BlockSpec((S, LANES), lambda g: (0, g)),
                  pl.BlockSpec((S, LANES), lambda g: (0, g)),
                  pl.BlockSpec((1, LANES, LANES), lambda g: (g, 0, 0)),
                  pl.BlockSpec((1, LANES), lambda g: (0, g))],
        out_specs=[pl.BlockSpec((S, LANES), lambda g: (0, g)),
                   pl.BlockSpec((1, LANES, LANES), lambda g: (g, 0, 0)),
                   pl.BlockSpec((1, LANES), lambda g: (0, g))],
        out_shape=[jax.ShapeDtypeStruct((S, G * LANES), BF16),
                   jax.ShapeDtypeStruct((G, LANES, LANES), F32),
                   jax.ShapeDtypeStruct((1, G * LANES), F32)],
        scratch_shapes=[pltpu.VMEM((S + MAX_WINDOW, LANES), F32)],
        compiler_params=_params("arbitrary"),
    )(proj, da, pool_w, pool_scale)


def _inproj_bwd_dx(dproj, dzf, wmain, wf, x, gam, g):
    S, D = x.shape
    N = wmain.shape[1]
    tm = min(TM // 2, S)

    def body(dp_ref, dz_ref, w_ref, wf_ref, x_ref, gam_ref, g_ref, dx_ref, dg_ref):
        @pl.when(pl.program_id(0) == 0)
        def _():
            dg_ref[...] = jnp.zeros(dg_ref.shape, F32)

        dh = lax.dot_general(dp_ref[...], w_ref[...], NT, preferred_element_type=F32)
        dh = dh + lax.dot_general(dz_ref[...], wf_ref[...], NT, preferred_element_type=F32)
        xf = x_ref[...]
        r = lax.rsqrt(jnp.mean(xf * xf, axis=-1, keepdims=True) + RMS_EPS)
        xhat = xf * r
        dg_ref[...] += jnp.sum(dh * xhat, axis=0, keepdims=True)
        dxhat = dh * gam_ref[...]
        dx_ref[...] = g_ref[...] + r * (dxhat - xhat * jnp.mean(dxhat * xhat, axis=-1, keepdims=True))

    return pl.pallas_call(
        body, name="inproj_bwd_dx", grid=(S // tm,),
        in_specs=[pl.BlockSpec((tm, N), lambda i: (i, 0)),
                  pl.BlockSpec((tm, LANES), lambda i: (i, 0)),
                  pl.BlockSpec((D, N), lambda i: (0, 0)),
                  pl.BlockSpec((D, LANES), lambda i: (0, 0)),
                  pl.BlockSpec((tm, D), lambda i: (i, 0)),
                  pl.BlockSpec((1, D), lambda i: (0, 0)),
                  pl.BlockSpec((tm, D), lambda i: (i, 0))],
        out_specs=[pl.BlockSpec((tm, D), lambda i: (i, 0)),
                   pl.BlockSpec((1, D), lambda i: (0, 0))],
        out_shape=[jax.ShapeDtypeStruct((S, D), F32),
                   jax.ShapeDtypeStruct((1, D), F32)],
        compiler_params=_params("arbitrary"),
    )(dproj, dzf, wmain, wf, x, gam, g)


def _adamw(w, m, v, gparts, name, rows):
    A, R, C = w.shape
    tr = min(rows, R)
    c1 = 1.0 / (1.0 - ADAM_B1 ** ADAM_STEP)
    c2 = 1.0 / (1.0 - ADAM_B2 ** ADAM_STEP)

    def body(w_ref, m_ref, v_ref, *rest):
        gp_refs = rest[:A]
        g_ref, d_ref, nm_ref, nv_ref = rest[A:]
        for l in range(A):
            @pl.when(pl.program_id(0) == l)
            def _():
                g = gp_refs[l][0]
                for s in range(1, N_DEV):
                    g = g + gp_refs[l][s]
                nm = ADAM_B1 * m_ref[0] + (1.0 - ADAM_B1) * g
                nv = ADAM_B2 * v_ref[0] + (1.0 - ADAM_B2) * (g * g)
                g_ref[0] = g
                nm_ref[0] = nm
                nv_ref[0] = nv
                d_ref[0] = -ADAM_LR * ((nm * c1) / (jnp.sqrt(nv * c2) + ADAM_EPS) + ADAM_WD * w_ref[0])

    spec = pl.BlockSpec((1, tr, C), lambda a, r: (a, r, 0))
    gp_specs = [pl.BlockSpec((N_DEV, tr, C), lambda a, r, l=l: (0, jnp.where(a == l, r, 0), 0)) for l in range(A)]
    shape = jax.ShapeDtypeStruct((A, R, C), F32)
    return pl.pallas_call(
        body, name=name, grid=(A, R // tr),
        in_specs=[spec, spec, spec] + gp_specs,
        out_specs=[spec, spec, spec, spec],
        out_shape=[shape, shape, shape, shape],
        compiler_params=_params("arbitrary", "arbitrary"),
    )(w, m, v, *gparts)


def _position():
    return lax.axis_index("x"), lax.axis_index("y"), lax.axis_index("c")


def _index(dev):
    return 4 * dev[0] + 2 * dev[1] + dev[2]


def _all_gather(arrs, slots, out_shapes, name):
    n_arr = len(arrs)

    def body(*refs):
        ins, outs = refs[:n_arr], refs[n_arr:2 * n_arr]
        send_sems, recv_sems, local_sems = refs[2 * n_arr:]
        x, y, c = _position()
        me, sibling = (x, y, c), (x, y, 1 - c)
        chips = [(1 - x, y), (x, 1 - y), (1 - x, 1 - y)]

        def copy(a, k, block, to, src=None):
            part = slots[a](outs[a], _index(block))
            return pltpu.make_async_remote_copy(
                src_ref=part if src is None else src, dst_ref=part,
                send_sem=send_sems.at[a, k], recv_sem=recv_sems.at[a, k],
                device_id=to, device_id_type=MESH)

        mine = [pltpu.make_async_copy(ins[a], slots[a](outs[a], _index(me)), local_sems.at[a])
                for a in range(n_arr)]
        for cp in mine:
            cp.start()
        first = []
        for a in range(n_arr):
            first.append(copy(a, 0, me, sibling, src=ins[a]))
            first += [copy(a, 1 + j, me, (*chip, c), src=ins[a]) for j, chip in enumerate(chips)]
        for cp in first:
            cp.start()
        passed = []
        for j, chip in enumerate(chips):
            for a in range(n_arr):
                copy(a, 1 + j, (*chip, c), me).wait_recv()
                fwd = copy(a, 4 + j, (*chip, c), sibling)
                fwd.start()
                passed.append(fwd)
        for a in range(n_arr):
            copy(a, 0, sibling, me).wait_recv()
            for j, chip in enumerate(chips):
                copy(a, 4 + j, (*chip, 1 - c), me).wait_recv()
        for cp in first + passed:
            cp.wait_send()
        for cp in mine:
            cp.wait()

    any_spec = pl.BlockSpec(memory_space=pl.ANY)
    return pl.pallas_call(
        body, name=name,
        in_specs=[any_spec] * n_arr, out_specs=[any_spec] * n_arr, out_shape=out_shapes,
        scratch_shapes=[pltpu.SemaphoreType.DMA((n_arr, 7)), pltpu.SemaphoreType.DMA((n_arr, 7)),
                        pltpu.SemaphoreType.DMA((n_arr,))],
    )(*arrs)


def _split_copies(srcs, lands, send_sems, recv_sems, src_part, land_part):
    x, y, c = _position()
    me = _index((x, y, c))
    copies = []
    for a in range(len(srcs)):
        for k in range(1, N_DEV):
            peer = (x ^ ((k >> 2) & 1), y ^ ((k >> 1) & 1), c ^ (k & 1))
            copies.append(pltpu.make_async_remote_copy(
                src_ref=src_part(a, srcs[a], _index(peer)), dst_ref=land_part(a, lands[a], me, k),
                send_sem=send_sems[a].at[k - 1], recv_sem=recv_sems[a].at[k - 1],
                device_id=peer, device_id_type=MESH))
    return copies


def _split_start(srcs, lands, src_part, land_part, name):
    n = len(srcs)

    def body(*refs):
        src_refs, land_refs = refs[:n], refs[n:2 * n]
        send_sems, recv_sems = refs[2 * n:3 * n], refs[3 * n:4 * n]
        token = refs[-1]
        for cp in _split_copies(src_refs, land_refs, send_sems, recv_sems, src_part, land_part):
            cp.start()
        token[...] = jnp.zeros(token.shape, token.dtype)

    hbm = pl.BlockSpec(memory_space=pltpu.HBM)
    sem = pl.BlockSpec(memory_space=pltpu.SEMAPHORE)
    operands = [pltpu.with_memory_space_constraint(t, pltpu.HBM) for t in (*srcs, *lands)]
    out = pl.pallas_call(
        body, name=name,
        in_specs=[hbm] * (2 * n),
        out_specs=[sem] * (2 * n) + [hbm] * (2 * n) + [pl.BlockSpec(memory_space=pltpu.VMEM)],
        out_shape=[pltpu.SemaphoreType.DMA((N_DEV - 1,))] * (2 * n)
        + [pltpu.HBM(t.shape, t.dtype) for t in operands] + [jax.ShapeDtypeStruct((8, LANES), F32)],
        input_output_aliases={i: 2 * n + i for i in range(2 * n)},
        compiler_params=pltpu.CompilerParams(has_side_effects=pltpu.SideEffectType.DATAFLOW_SIDE_EFFECTING),
    )(*operands)
    return out[:2 * n], out[2 * n:3 * n], out[3 * n:4 * n], out[-1]


def _split_wait(sems, srcs, lands, src_part, land_part, after, name):
    n = len(srcs)

    def body(*refs):
        src_refs, land_refs = refs[:n], refs[n:2 * n]
        send_sems, recv_sems = refs[2 * n:3 * n], refs[3 * n:4 * n]
        for cp in _split_copies(src_refs, land_refs, send_sems, recv_sems, src_part, land_part):
            cp.wait_send()
            cp.wait_recv()

    hbm = pl.BlockSpec(memory_space=pltpu.HBM)
    sem = pl.BlockSpec(memory_space=pltpu.SEMAPHORE)
    out = pl.pallas_call(
        body, name=name,
        in_specs=[hbm] * (2 * n) + [sem] * (2 * n) + [pl.BlockSpec(memory_space=pl.ANY)] * len(after),
        out_specs=[hbm] * (2 * n),
        out_shape=[pltpu.HBM(t.shape, t.dtype) for t in (*srcs, *lands)],
        input_output_aliases={i: i for i in range(2 * n)},
        compiler_params=pltpu.CompilerParams(has_side_effects=pltpu.SideEffectType.DATAFLOW_SIDE_EFFECTING),
    )(*srcs, *lands, *sems, *after)
    return out[n:]


def _pack(parts, rows):
    flat = jnp.concatenate([p.reshape(-1) for p in parts])
    return jnp.pad(flat, (0, rows * LANES - flat.shape[0])).reshape(1, rows, LANES)


def _unpack(packed, like):
    flat = packed.reshape(-1)
    out, off = [], 0
    for p in like:
        out.append(flat[off:off + p.size].reshape(p.shape))
        off += p.size
    return out


def _local_step(x, target, norm_g, forget_bias, pool_w, pool_scale, final_g, weights, on_grads, first_after=()):
    L = norm_g.shape[0]
    S, D = x.shape
    W = D // 2
    H = W // HEAD_DIM
    bias = jnp.pad(forget_bias, ((0, 0), (0, LANES - H)))

    saved = []
    after = tuple(first_after)
    for l in range(L):
        wmain, wf, wout = weights(l, x)
        proj, h, z = _inproj_fwd(x, norm_g[l:l + 1], wmain, wf, after)
        after = ()
        ccol, crow = _fgate_fwd(z, bias[l:l + 1], H)
        ypool = _pool_fwd(proj, pool_w[l], pool_scale[l:l + 1])
        o, lse = _attn_fwd(proj, ccol, crow)
        x_new, mixed = _outproj_fwd(ypool, o, proj, x, wout)
        saved.append((x, proj, h, z, ccol, crow, ypool, o, lse, mixed, wmain, wf, wout))
        x = x_new

    g, loss, d_final_g = _loss_head(x, final_g.reshape(1, D), target)

    d_norm_g, d_bias, d_pool_w, d_pool_scale = [], [], [], []
    for l in reversed(range(L)):
        x_in, proj, h, z, ccol, crow, ypool, o, lse, mixed, wmain, wf, wout = saved[l]
        d_wout = _matmul_tn(mixed, g, "outproj_bwd_dw", after)
        da, dgate = _outproj_bwd_dx(g, wout, ypool, o, proj)
        delta = _attn_delta(da, o)
        dq, dk, dv, dcq, dcn = _attn_bwd(proj, da, lse, delta, ccol, crow)
        dzf, db = _fgate_bwd(dcq, dcn, z, bias[l:l + 1], H)
        dpu, dpw, dps = _pool_bwd(proj, da, pool_w[l], pool_scale[l:l + 1])
        dproj = jnp.concatenate([dpu, dgate[:, :W], dq.astype(BF16), dk, dv, dgate[:, W:]], axis=1)
        d_wmain = _matmul_tn(h, dproj, "inproj_bwd_dw")
        d_wf = _matmul_tn(h, dzf, "inproj_bwd_dwf")
        after = tuple(on_grads(l, d_wmain, d_wf[:, :H], d_wout))
        g, dgam = _inproj_bwd_dx(dproj, dzf, wmain, wf, x_in, norm_g[l:l + 1], g)
        d_norm_g.append(dgam[0])
        d_bias.append(db[0, :H])
        d_pool_w.append(dpw)
        d_pool_scale.append(dps[0])

    stack = lambda parts: jnp.stack(parts[::-1])
    grads = dict(norm_g=stack(d_norm_g), forget_bias=stack(d_bias), pool_w=stack(d_pool_w),
                 pool_scale=stack(d_pool_scale), final_g=d_final_g[0])
    return loss[0, 0], g, grads


def kernel(x, norm_g, w_in, forget_bias, pool_w, pool_scale, w_out, final_g, loss_target, m_norm_g, m_w_in, m_forget_bias, m_pool_w, m_pool_scale, m_w_out, m_final_g, v_norm_g, v_w_in, v_forget_bias, v_pool_w, v_pool_scale, v_w_out, v_final_g):
    L, D, cols = w_in.shape
    rows_out = w_out.shape[1]
    W = D // 2
    H = W // HEAD_DIM
    me = _index(_position())

    win_b, wout_b = w_in.astype(BF16), w_out.astype(BF16)

    def gather_src(a, ref, peer):
        return ref

    def gather_land(a, ref, mine, k):
        return ref.at[mine] if a == 0 else ref.at[pl.ds(mine * rows_out, rows_out), :]

    gathers, tokens = [], []
    for l in range(L):
        lands = [jnp.broadcast_to(win_b[l][None], (N_DEV, D, cols)), jnp.tile(wout_b[l], (N_DEV, 1))]
        sems, srcs, lands, token = _split_start([win_b[l], wout_b[l]], lands, gather_src, gather_land,
                                                f"gather_start_{l}")
        gathers.append((sems, srcs, lands))
        tokens.append(token)

    def weights(l, x_in):
        sems, srcs, lands = gathers[l]
        win_all, wout_full = _split_wait(sems, srcs, lands, gather_src, gather_land, (x_in,), f"gather_wait_{l}")
        w_full = jnp.transpose(win_all, (1, 0, 2)).reshape(D, N_DEV * cols)
        return w_full[:, :6 * W], jnp.pad(w_full[:, 6 * W:], ((0, 0), (0, LANES - H))), wout_full

    def exchange_src(a, ref, peer):
        return ref.at[peer] if a == 0 else ref.at[pl.ds(peer * rows_out, rows_out), :]

    def exchange_land(a, ref, mine, k):
        return ref.at[k]

    exchanges = {}

    def on_grads(l, d_wmain, d_wf, d_wout):
        dw_in = jnp.concatenate([d_wmain, d_wf], axis=1)
        dw_in = jnp.transpose(dw_in.reshape(D, N_DEV, cols), (1, 0, 2))
        own_in = lax.dynamic_index_in_dim(dw_in, me, 0, keepdims=True)
        own_out = lax.dynamic_slice_in_dim(d_wout, me * rows_out, rows_out, 0)[None]
        lands = [lax.dynamic_update_slice(jnp.zeros((N_DEV, D, cols), F32), own_in, (0, 0, 0)),
                 lax.dynamic_update_slice(jnp.zeros((N_DEV, rows_out, D), F32), own_out, (0, 0, 0))]
        sems, srcs, lands, token = _split_start([dw_in, d_wout], lands, exchange_src, exchange_land,
                                                f"exchange_start_{l}")
        exchanges[l] = (sems, srcs, lands)
        return (token,)

    loss, dx, grads = _local_step(x[0], loss_target[0], norm_g, forget_bias, pool_w, pool_scale, final_g,
                                  weights, on_grads, tokens)
    loss = lax.psum(loss, ("x", "y", "c"))

    gin_parts, gout_parts = [], []
    for l in range(L):
        sems, srcs, lands = exchanges[l]
        land_in, land_out = _split_wait(sems, srcs, lands, exchange_src, exchange_land, (dx,), f"exchange_wait_{l}")
        gin_parts.append(land_in)
        gout_parts.append(land_out)
    g_w_in, d_w_in, nm_w_in, nv_w_in = _adamw(w_in, m_w_in, v_w_in, gin_parts, "adamw_w_in", 256)
    g_w_out, d_w_out, nm_w_out, nv_w_out = _adamw(w_out, m_w_out, v_w_out, gout_parts, "adamw_w_out", 128)

    names = ["norm_g", "forget_bias", "pool_w", "pool_scale", "final_g"]
    small_w = [norm_g, forget_bias, pool_w, pool_scale, final_g]
    small_m = [m_norm_g, m_forget_bias, m_pool_w, m_pool_scale, m_final_g]
    small_v = [v_norm_g, v_forget_bias, v_pool_w, v_pool_scale, v_final_g]
    total = sum(p.size for p in small_w)
    rows = -(-total // (8 * LANES)) * 8
    (gs_all,) = _all_gather(
        [_pack([grads[n] for n in names], rows)],
        [lambda ref, n: ref.at[n]],
        [jax.ShapeDtypeStruct((N_DEV, 1, rows, LANES), F32)], "gather_small_grads")
    packed = _adamw(_pack(small_w, rows), _pack(small_m, rows), _pack(small_v, rows),
                    [gs_all.reshape(N_DEV, rows, LANES)], "adamw_small", rows)
    g_s, d_s, nm_s, nv_s = [_unpack(p, small_w) for p in packed]

    def order(big_in, big_out, small):
        return (small[0], big_in, small[1], small[2], small[3], big_out, small[4])

    return (loss, dx[None], *order(g_w_in, g_w_out, g_s), *order(d_w_in, d_w_out, d_s),
            *order(nm_w_in, nm_w_out, nm_s), *order(nv_w_in, nv_w_out, nv_s))
```

```python
import functools
import math

import jax
import jax.numpy as jnp
from jax import lax
from jax.experimental import pallas as pl
from jax.experimental.pallas import tpu as pltpu

F32 = jnp.float32
BF16 = jnp.bfloat16
MESH = pl.DeviceIdType.MESH

RMS_EPS = 1e-6
NEG_INF = -1e30
HEAD_DIM = 64
POOL_WINDOWS = (2, 4, 8, 16)
MAX_WINDOW = 16
LANES = 128
N_DEV = 8

ADAM_LR = 0.001
ADAM_B1 = 0.9
ADAM_B2 = 0.999
ADAM_EPS = 1e-08
ADAM_WD = 0.01
ADAM_STEP = 10

TM = 512
TN = 512
TQ = 512
TB = 256
VMEM_LIMIT = 56 * 1024 * 1024

NT = (((1,), (1,)), ((), ()))
TN_DIMS = (((0,), (0,)), ((), ()))


def _params(*sem):
    return pltpu.CompilerParams(dimension_semantics=sem, vmem_limit_bytes=VMEM_LIMIT)


def _sigmoid(x):
    return 1.0 / (1.0 + jnp.exp(-x))


def _inproj_fwd(x, gam, wmain, wf, after=()):
    S, D = x.shape
    N = wmain.shape[1]
    tm, tn = min(TM, S), TN

    def body(x_ref, g_ref, w_ref, wf_ref, *rest):
        proj_ref, h_ref, z_ref = rest[-3:]

        @pl.when(pl.program_id(1) == 0)
        def _():
            xf = x_ref[...]
            r = lax.rsqrt(jnp.mean(xf * xf, axis=-1, keepdims=True) + RMS_EPS)
            h = ((xf * r) * g_ref[...]).astype(BF16)
            h_ref[...] = h
            z_ref[...] = jnp.dot(h, wf_ref[...], preferred_element_type=F32)

        proj_ref[...] = jnp.dot(h_ref[...], w_ref[...], preferred_element_type=F32)

    return pl.pallas_call(
        body, name="inproj_fwd", grid=(S // tm, N // tn),
        in_specs=[pl.BlockSpec((tm, D), lambda i, j: (i, 0)),
                  pl.BlockSpec((1, D), lambda i, j: (0, 0)),
                  pl.BlockSpec((D, tn), lambda i, j: (0, j)),
                  pl.BlockSpec((D, LANES), lambda i, j: (0, 0))] + [pl.BlockSpec(memory_space=pl.ANY)] * len(after),
        out_specs=[pl.BlockSpec((tm, tn), lambda i, j: (i, j)),
                   pl.BlockSpec((tm, D), lambda i, j: (i, 0)),
                   pl.BlockSpec((tm, LANES), lambda i, j: (i, 0))],
        out_shape=[jax.ShapeDtypeStruct((S, N), F32),
                   jax.ShapeDtypeStruct((S, D), BF16),
                   jax.ShapeDtypeStruct((S, LANES), F32)],
        compiler_params=_params("parallel", "arbitrary"),
    )(x, gam, wmain, wf, *after)


def _fgate_fwd(z, bias, n_heads):
    S = z.shape[0]
    tb = min(TB, S)

    def body(z_ref, b_ref, ccol_ref, crow_ref, c_scr):
        lane = lax.broadcasted_iota(jnp.int32, (tb, LANES), 1)
        tri = (lax.broadcasted_iota(jnp.int32, (tb, tb), 0)
               >= lax.broadcasted_iota(jnp.int32, (tb, tb), 1)).astype(F32)

        def block(i, carry):
            r0 = pl.multiple_of(i * tb, tb)
            zz = z_ref[pl.ds(r0, tb), :] + b_ref[...]
            lf = jnp.minimum(zz, 0.0) - jnp.log(1.0 + jnp.exp(-jnp.abs(zz)))
            lf = jnp.where(lane < n_heads, lf, 0.0)
            c = jnp.dot(tri, lf, preferred_element_type=F32, precision=lax.Precision.HIGHEST) + carry
            c_scr[pl.ds(r0, tb), :] = c
            return c[tb - 1:tb, :]

        lax.fori_loop(0, S // tb, block, jnp.zeros((1, LANES), F32))
        c = c_scr[...]
        ct = c.T
        for h in range(n_heads):
            ccol_ref[h] = jnp.broadcast_to(c[:, h:h + 1], (S, LANES))
            crow_ref[h] = jnp.broadcast_to(ct[h:h + 1, :], (8, S))

    return pl.pallas_call(
        body, name="fgate_fwd",
        out_shape=[jax.ShapeDtypeStruct((n_heads, S, LANES), F32),
                   jax.ShapeDtypeStruct((n_heads, 8, S), F32)],
        scratch_shapes=[pltpu.VMEM((S, LANES), F32)],
        compiler_params=pltpu.CompilerParams(vmem_limit_bytes=VMEM_LIMIT),
    )(z, bias)


def _window_mean_minus_self(u, pad_ref, w, S):
    pad_ref[0:MAX_WINDOW, :] = jnp.zeros((MAX_WINDOW, LANES), F32)
    pad_ref[MAX_WINDOW:MAX_WINDOW + S, :] = u
    acc = u
    for j in range(1, w):
        acc = acc + pad_ref[MAX_WINDOW - j:MAX_WINDOW - j + S, :]
    t = lax.broadcasted_iota(jnp.int32, (S, LANES), 0)
    cnt = jnp.minimum(t + 1, w).astype(F32)
    return acc / cnt - u, cnt


def _pool_fwd(proj, pool_w, pool_scale):
    S = proj.shape[0]
    G = len(POOL_WINDOWS)

    def body(u_ref, w_ref, s_ref, y_ref, pad_ref):
        g = pl.program_id(0)
        for gi, w in enumerate(POOL_WINDOWS):
            @pl.when(g == gi)
            def _():
                d, _ = _window_mean_minus_self(u_ref[...], pad_ref, w, S)
                y = jnp.dot(d.astype(BF16), w_ref[0].astype(BF16), preferred_element_type=F32)
                y_ref[...] = y * s_ref[...]

    return pl.pallas_call(
        body, name="pool_fwd", grid=(G,),
        in_specs=[pl.BlockSpec((S, LANES), lambda g: (0, g)),
                  pl.BlockSpec((1, LANES, LANES), lambda g: (g, 0, 0)),
                  pl.BlockSpec((1, LANES), lambda g: (0, g))],
        out_specs=pl.BlockSpec((S, LANES), lambda g: (0, g)),
        out_shape=jax.ShapeDtypeStruct((S, G * LANES), F32),
        scratch_shapes=[pltpu.VMEM((S + MAX_WINDOW, LANES), F32)],
        compiler_params=_params("arbitrary"),
    )(proj, pool_w, pool_scale)


def _attn_fwd(proj, ccol, crow):
    S = proj.shape[0]
    W = proj.shape[1] // 6
    P = W // LANES
    tq = min(TQ, S)
    nq = S // tq
    qc, kc, vc = 2 * P, 3 * P, 4 * P
    scale = 1.0 / math.sqrt(HEAD_DIM)

    def body(q_ref, k_ref, v_ref, cq_ref, ck_ref, o_ref, lse_ref, qm_scr, m_scr, l_scr, acc_scr):
        i, j = pl.program_id(1), pl.program_id(2)
        lane = lax.broadcasted_iota(jnp.int32, (tq, LANES), 1)

        @pl.when(j == 0)
        def _():
            qs = (q_ref[...] * scale).astype(BF16)
            qm_scr[0] = jnp.where(lane < HEAD_DIM, qs, jnp.zeros_like(qs))
            qm_scr[1] = jnp.where(lane >= HEAD_DIM, qs, jnp.zeros_like(qs))
            m_scr[...] = jnp.full(m_scr.shape, NEG_INF, F32)
            l_scr[...] = jnp.zeros(l_scr.shape, F32)
            acc_scr[...] = jnp.zeros(acc_scr.shape, F32)

        @pl.when(j <= i)
        def _():
            k2 = k_ref[...].astype(BF16)
            v2 = v_ref[...].astype(BF16)
            row = lax.broadcasted_iota(jnp.int32, (tq, tq), 0)
            col = lax.broadcasted_iota(jnp.int32, (tq, tq), 1)
            keep = jnp.logical_or(j < i, row >= col)
            for a in range(2):
                s = lax.dot_general(qm_scr[a], k2, NT, preferred_element_type=F32)
                s = s + (jnp.tile(cq_ref[a], (1, tq // LANES)) - jnp.tile(ck_ref[a], (tq // 8, 1)))
                s = jnp.where(keep, s, NEG_INF)
                m_prev = m_scr[a]
                m_new = jnp.maximum(m_prev, jnp.max(s, axis=1, keepdims=True))
                p = jnp.exp(s - jnp.tile(m_new, (1, tq // LANES)))
                alpha = jnp.exp(m_prev - m_new)
                l_scr[a] = alpha * l_scr[a] + jnp.sum(p, axis=1, keepdims=True)
                acc_scr[a] = alpha * acc_scr[a] + jnp.dot(p.astype(BF16), v2, preferred_element_type=F32)
                m_scr[a] = m_new

        @pl.when(j == i)
        def _():
            o_ref[...] = jnp.where(lane < HEAD_DIM, acc_scr[0] / l_scr[0], acc_scr[1] / l_scr[1])
            lse_ref[0] = m_scr[0] + jnp.log(l_scr[0])
            lse_ref[1] = m_scr[1] + jnp.log(l_scr[1])

    return pl.pallas_call(
        body, name="attn_fwd", grid=(P, nq, nq),
        in_specs=[pl.BlockSpec((tq, LANES), lambda p, i, j: (i, qc + p)),
                  pl.BlockSpec((tq, LANES), lambda p, i, j: (jnp.minimum(j, i), kc + p)),
                  pl.BlockSpec((tq, LANES), lambda p, i, j: (jnp.minimum(j, i), vc + p)),
                  pl.BlockSpec((2, tq, LANES), lambda p, i, j: (p, i, 0)),
                  pl.BlockSpec((2, 8, tq), lambda p, i, j: (p, 0, jnp.minimum(j, i)))],
        out_specs=[pl.BlockSpec((tq, LANES), lambda p, i, j: (i, p)),
                   pl.BlockSpec((2, tq, LANES), lambda p, i, j: (p, i, 0))],
        out_shape=[jax.ShapeDtypeStruct((S, W), F32),
                   jax.ShapeDtypeStruct((2 * P, S, LANES), F32)],
        scratch_shapes=[pltpu.VMEM((2, tq, LANES), BF16),
                        pltpu.VMEM((2, tq, LANES), F32),
                        pltpu.VMEM((2, tq, LANES), F32),
                        pltpu.VMEM((2, tq, LANES), F32)],
        compiler_params=_params("parallel", "parallel", "arbitrary"),
    )(proj, proj, proj, ccol, crow)


def _outproj_fwd(ypool, o, proj, x, wout):
    S, D = x.shape
    W = D // 2
    tm, tn = min(TM, S), TN

    def body(y_ref, o_ref, pg_ref, ag_ref, x_ref, w_ref, xn_ref, mix_ref):
        @pl.when(pl.program_id(1) == 0)
        def _():
            pg, ag = pg_ref[...], ag_ref[...]
            mix_ref[:, 0:W] = (y_ref[...] * (pg * _sigmoid(pg))).astype(BF16)
            mix_ref[:, W:D] = (o_ref[...] * (ag * _sigmoid(ag))).astype(BF16)

        xn_ref[...] = x_ref[...] + jnp.dot(mix_ref[...], w_ref[...], preferred_element_type=F32)

    return pl.pallas_call(
        body, name="outproj_fwd", grid=(S // tm, D // tn),
        in_specs=[pl.BlockSpec((tm, W), lambda i, j: (i, 0)),
                  pl.BlockSpec((tm, W), lambda i, j: (i, 0)),
                  pl.BlockSpec((tm, W), lambda i, j: (i, 1)),
                  pl.BlockSpec((tm, W), lambda i, j: (i, 5)),
                  pl.BlockSpec((tm, tn), lambda i, j: (i, j)),
                  pl.BlockSpec((D, tn), lambda i, j: (0, j))],
        out_specs=[pl.BlockSpec((tm, tn), lambda i, j: (i, j)),
                   pl.BlockSpec((tm, D), lambda i, j: (i, 0))],
        out_shape=[jax.ShapeDtypeStruct((S, D), F32),
                   jax.ShapeDtypeStruct((S, D), BF16)],
        compiler_params=_params("parallel", "arbitrary"),
    )(ypool, o, proj, proj, x, wout)


def _loss_head(x, gam, target):
    S, D = x.shape
    tm = min(TM, S)

    def body(x_ref, g_ref, t_ref, dx_ref, loss_ref, dg_ref):
        @pl.when(pl.program_id(0) == 0)
        def _():
            loss_ref[...] = jnp.zeros(loss_ref.shape, F32)
            dg_ref[...] = jnp.zeros(dg_ref.shape, F32)

        xf, gam_v = x_ref[...], g_ref[...]
        r = lax.rsqrt(jnp.mean(xf * xf, axis=-1, keepdims=True) + RMS_EPS)
        xhat = xf * r
        err = xhat * gam_v - t_ref[...]
        part = jnp.sum(jnp.sum(err * err, axis=-1, keepdims=True), axis=0, keepdims=True)
        loss_ref[...] += part * (0.5 / D)
        dy = err * (1.0 / D)
        dg_ref[...] += jnp.sum(dy * xhat, axis=0, keepdims=True)
        dxhat = dy * gam_v
        dx_ref[...] = r * (dxhat - xhat * jnp.mean(dxhat * xhat, axis=-1, keepdims=True))

    return pl.pallas_call(
        body, name="loss_head", grid=(S // tm,),
        in_specs=[pl.BlockSpec((tm, D), lambda i: (i, 0)),
                  pl.BlockSpec((1, D), lambda i: (0, 0)),
                  pl.BlockSpec((tm, D), lambda i: (i, 0))],
        out_specs=[pl.BlockSpec((tm, D), lambda i: (i, 0)),
                   pl.BlockSpec((8, LANES), lambda i: (0, 0)),
                   pl.BlockSpec((1, D), lambda i: (0, 0))],
        out_shape=[jax.ShapeDtypeStruct((S, D), F32),
                   jax.ShapeDtypeStruct((8, LANES), F32),
                   jax.ShapeDtypeStruct((1, D), F32)],
        compiler_params=_params("arbitrary"),
    )(x, gam, target)


def _outproj_bwd_dx(g, wout, ypool, o, proj):
    S, D = g.shape
    W = D // 2
    tm = min(TM, S)

    def body(g_ref, w_ref, y_ref, o_ref, gate_ref, da_ref, dgate_ref):
        j = pl.program_id(1)
        dmix = lax.dot_general(g_ref[...].astype(BF16), w_ref[...], NT, preferred_element_type=F32)
        val = jnp.where(j == 0, y_ref[...], o_ref[...])
        gt = gate_ref[...]
        sg = _sigmoid(gt)
        da_ref[...] = dmix * (gt * sg)
        dgate_ref[...] = (dmix * val * (sg * (1.0 + gt * (1.0 - sg)))).astype(BF16)

    return pl.pallas_call(
        body, name="outproj_bwd_dx", grid=(S // tm, 2),
        in_specs=[pl.BlockSpec((tm, D), lambda i, j: (i, 0)),
                  pl.BlockSpec((W, D), lambda i, j: (j, 0)),
                  pl.BlockSpec((tm, W), lambda i, j: (i, 0)),
                  pl.BlockSpec((tm, W), lambda i, j: (i, 0)),
                  pl.BlockSpec((tm, W), lambda i, j: (i, 1 + 4 * j))],
        out_specs=[pl.BlockSpec((tm, W), lambda i, j: (i, j)),
                   pl.BlockSpec((tm, W), lambda i, j: (i, j))],
        out_shape=[jax.ShapeDtypeStruct((S, D), F32),
                   jax.ShapeDtypeStruct((S, D), BF16)],
        compiler_params=_params("parallel", "arbitrary"),
    )(g, wout, ypool, o, proj)


def _matmul_tn(a, b, name, after=()):
    S, M = a.shape
    N = b.shape[1]
    ts, tn = min(TM, S), min(TN, N)

    def body(a_ref, b_ref, *rest):
        out_ref = rest[-1]

        @pl.when(pl.program_id(1) == 0)
        def _():
            out_ref[...] = jnp.zeros(out_ref.shape, F32)

        out_ref[...] += lax.dot_general(a_ref[...], b_ref[...].astype(BF16), TN_DIMS,
                                        preferred_element_type=F32)

    return pl.pallas_call(
        body, name=name, grid=(N // tn, S // ts),
        in_specs=[pl.BlockSpec((ts, M), lambda n, k: (k, 0)),
                  pl.BlockSpec((ts, tn), lambda n, k: (k, n))] + [pl.BlockSpec(memory_space=pl.ANY)] * len(after),
        out_specs=pl.BlockSpec((M, tn), lambda n, k: (0, n)),
        out_shape=jax.ShapeDtypeStruct((M, N), F32),
        compiler_params=_params("parallel", "arbitrary"),
    )(a, b, *after)


def _attn_delta(da, o):
    S, D = da.shape
    W = D // 2
    P = W // LANES
    tq = min(TQ, S)

    def body(do_ref, o_ref, dl_ref):
        lane = lax.broadcasted_iota(jnp.int32, (tq, LANES), 1)
        prod = do_ref[...] * o_ref[...]
        first = jnp.sum(jnp.where(lane < HEAD_DIM, prod, 0.0), axis=1, keepdims=True)
        second = jnp.sum(jnp.where(lane >= HEAD_DIM, prod, 0.0), axis=1, keepdims=True)
        dl_ref[0] = jnp.broadcast_to(first, (tq, LANES))
        dl_ref[1] = jnp.broadcast_to(second, (tq, LANES))

    return pl.pallas_call(
        body, name="attn_delta", grid=(P, S // tq),
        in_specs=[pl.BlockSpec((tq, LANES), lambda p, i: (i, P + p)),
                  pl.BlockSpec((tq, LANES), lambda p, i: (i, p))],
        out_specs=pl.BlockSpec((2, tq, LANES), lambda p, i: (p, i, 0)),
        out_shape=jax.ShapeDtypeStruct((2 * P, S, LANES), F32),
        compiler_params=_params("parallel", "parallel"),
    )(da, o)


def _attn_bwd(proj, da, lse, delta, ccol, crow):
    S = proj.shape[0]
    W = proj.shape[1] // 6
    P = W // LANES
    tq = min(TQ, S)
    nq = S // tq
    qc, kc, vc = 2 * P, 3 * P, 4 * P
    scale = 1.0 / math.sqrt(HEAD_DIM)

    def body(q_ref, k_ref, v_ref, do_ref, lse_ref, dl_ref, cq_ref, ck_ref,
             dq_ref, dk_ref, dv_ref, dcq_ref, dcn_ref, km_scr, dk_scr, dv_scr, dc_scr):
        j, i = pl.program_id(1), pl.program_id(2)
        lane = lax.broadcasted_iota(jnp.int32, (tq, LANES), 1)
        halves = (lane < HEAD_DIM, lane >= HEAD_DIM)

        @pl.when(jnp.logical_and(j == 0, i == 0))
        def _():
            dq_ref[...] = jnp.zeros(dq_ref.shape, F32)
            dcq_ref[...] = jnp.zeros(dcq_ref.shape, F32)

        @pl.when(i == 0)
        def _():
            k2 = k_ref[...].astype(BF16)
            km_scr[0] = jnp.where(halves[0], k2, jnp.zeros_like(k2))
            km_scr[1] = jnp.where(halves[1], k2, jnp.zeros_like(k2))
            dk_scr[...] = jnp.zeros(dk_scr.shape, F32)
            dv_scr[...] = jnp.zeros(dv_scr.shape, F32)
            dc_scr[...] = jnp.zeros(dc_scr.shape, F32)

        @pl.when(i >= j)
        def _():
            qs = (q_ref[...] * scale).astype(BF16)
            do2 = do_ref[...].astype(BF16)
            k2 = k_ref[...].astype(BF16)
            v2 = v_ref[...].astype(BF16)
            row = lax.broadcasted_iota(jnp.int32, (tq, tq), 0)
            col = lax.broadcasted_iota(jnp.int32, (tq, tq), 1)
            keep = jnp.logical_or(i > j, row >= col)
            r0 = pl.multiple_of(i * tq, tq)
            for a in range(2):
                qa = jnp.where(halves[a], qs, jnp.zeros_like(qs))
                doa = jnp.where(halves[a], do2, jnp.zeros_like(do2))
                s = lax.dot_general(qa, k2, NT, preferred_element_type=F32)
                s = s + (jnp.tile(cq_ref[a], (1, tq // LANES)) - jnp.tile(ck_ref[a], (tq // 8, 1)))
                s = jnp.where(keep, s, NEG_INF)
                p = jnp.exp(s - jnp.tile(lse_ref[a], (1, tq // LANES)))
                dp = lax.dot_general(doa, v2, NT, preferred_element_type=F32)
                ds = p * (dp - jnp.tile(dl_ref[a], (1, tq // LANES)))
                dsb = ds.astype(BF16)
                dv_scr[...] += lax.dot_general(p.astype(BF16), doa, TN_DIMS, preferred_element_type=F32)
                dk_scr[...] += lax.dot_general(dsb, qa, TN_DIMS, preferred_element_type=F32)
                dq_ref[pl.ds(r0, tq), :] += jnp.dot(dsb, km_scr[a], preferred_element_type=F32) * scale
                dc_scr[a] += jnp.sum(ds, axis=0, keepdims=True)
                dcq_ref[a, pl.ds(r0, tq), :] += jnp.sum(ds, axis=1, keepdims=True)

        @pl.when(i == nq - 1)
        def _():
            dk_ref[...] = dk_scr[...].astype(BF16)
            dv_ref[...] = dv_scr[...].astype(BF16)
            dcn_ref[...] = dc_scr[...]

    qrow = lambda p, j, i: jnp.maximum(i, j)
    return pl.pallas_call(
        body, name="attn_bwd", grid=(P, nq, nq),
        in_specs=[pl.BlockSpec((tq, LANES), lambda p, j, i: (qrow(p, j, i), qc + p)),
                  pl.BlockSpec((tq, LANES), lambda p, j, i: (j, kc + p)),
                  pl.BlockSpec((tq, LANES), lambda p, j, i: (j, vc + p)),
                  pl.BlockSpec((tq, LANES), lambda p, j, i: (qrow(p, j, i), P + p)),
                  pl.BlockSpec((2, tq, LANES), lambda p, j, i: (p, qrow(p, j, i), 0)),
                  pl.BlockSpec((2, tq, LANES), lambda p, j, i: (p, qrow(p, j, i), 0)),
                  pl.BlockSpec((2, tq, LANES), lambda p, j, i: (p, qrow(p, j, i), 0)),
                  pl.BlockSpec((2, 8, tq), lambda p, j, i: (p, 0, j))],
        out_specs=[pl.BlockSpec((S, LANES), lambda p, j, i: (0, p)),
                   pl.BlockSpec((tq, LANES), lambda p, j, i: (j, p)),
                   pl.BlockSpec((tq, LANES), lambda p, j, i: (j, p)),
                   pl.BlockSpec((2, S, LANES), lambda p, j, i: (p, 0, 0)),
                   pl.BlockSpec((2, 8, tq), lambda p, j, i: (p, 0, j))],
        out_shape=[jax.ShapeDtypeStruct((S, W), F32),
                   jax.ShapeDtypeStruct((S, W), BF16),
                   jax.ShapeDtypeStruct((S, W), BF16),
                   jax.ShapeDtypeStruct((2 * P, S, LANES), F32),
                   jax.ShapeDtypeStruct((2 * P, 8, S), F32)],
        scratch_shapes=[pltpu.VMEM((2, tq, LANES), BF16),
                        pltpu.VMEM((tq, LANES), F32),
                        pltpu.VMEM((tq, LANES), F32),
                        pltpu.VMEM((2, 8, tq), F32)],
        compiler_params=_params("parallel", "arbitrary", "arbitrary"),
    )(proj, proj, proj, da, lse, delta, ccol, crow)


def _fgate_bwd(dcq, dcn, z, bias, n_heads):
    S = z.shape[0]
    tb = min(TB, S)
    nb = S // tb

    def body(dcq_ref, dcn_ref, z_ref, b_ref, dz_ref, db_ref, t_scr):
        sub = lax.broadcasted_iota(jnp.int32, (LANES, S), 0)
        lane = lax.broadcasted_iota(jnp.int32, (S, LANES), 1)
        rows = jnp.zeros((LANES, S), F32)
        cols = jnp.zeros((S, LANES), F32)
        for h in range(n_heads):
            rows = jnp.where(sub == h, jnp.tile(dcn_ref[h], (LANES // 8, 1)), rows)
            cols = jnp.where(lane == h, dcq_ref[h], cols)
        t_scr[...] = cols - rows.T
        tri = (lax.broadcasted_iota(jnp.int32, (tb, tb), 1)
               >= lax.broadcasted_iota(jnp.int32, (tb, tb), 0)).astype(F32)

        def block(n, carry):
            carry_rc, db = carry
            r0 = pl.multiple_of((nb - 1 - n) * tb, tb)
            rc = jnp.dot(tri, t_scr[pl.ds(r0, tb), :], preferred_element_type=F32,
                         precision=lax.Precision.HIGHEST) + carry_rc
            zz = z_ref[pl.ds(r0, tb), :] + b_ref[...]
            dz = rc * _sigmoid(-zz)
            dz_ref[pl.ds(r0, tb), :] = dz.astype(BF16)
            return rc[0:1, :], db + jnp.sum(dz, axis=0, keepdims=True)

        zero = jnp.zeros((1, LANES), F32)
        _, db = lax.fori_loop(0, nb, block, (zero, zero))
        db_ref[...] = db

    return pl.pallas_call(
        body, name="fgate_bwd",
        out_shape=[jax.ShapeDtypeStruct((S, LANES), BF16),
                   jax.ShapeDtypeStruct((1, LANES), F32)],
        scratch_shapes=[pltpu.VMEM((S, LANES), F32)],
        compiler_params=pltpu.CompilerParams(vmem_limit_bytes=VMEM_LIMIT),
    )(dcq, dcn, z, bias)


def _pool_bwd(proj, da, pool_w, pool_scale):
    S = proj.shape[0]
    G = len(POOL_WINDOWS)

    def body(u_ref, dy_ref, w_ref, s_ref, du_ref, dw_ref, ds_ref, pad_ref):
        g = pl.program_id(0)
        for gi, w in enumerate(POOL_WINDOWS):
            @pl.when(g == gi)
            def _():
                d, cnt = _window_mean_minus_self(u_ref[...], pad_ref, w, S)
                db = d.astype(BF16)
                wb = w_ref[0].astype(BF16)
                yraw = jnp.dot(db, wb, preferred_element_type=F32)
                dy = dy_ref[...]
                ds_ref[...] = jnp.sum(dy * yraw, axis=0, keepdims=True)
                dzb = (dy * s_ref[...]).astype(BF16)
                dw_ref[0] = lax.dot_general(db, dzb, TN_DIMS, preferred_element_type=F32)
                dd = lax.dot_general(dzb, wb, NT, preferred_element_type=F32)
                pad_ref[0:S, :] = dd / cnt
                pad_ref[S:S + MAX_WINDOW, :] = jnp.zeros((MAX_WINDOW, LANES), F32)
                acc = -dd
                for j in range(w):
                    acc = acc + pad_ref[j:j + S, :]
                du_ref[...] = acc.astype(BF16)

    return pl.pallas_call(
        body, name="pool_bwd", grid=(G,),
        in_specs=[pl.BlockSpec((S, LANES), lambda g: (0, g)),
                  pl.BlockSpec((S, LANES), lambda g: (0, g)),
                  pl.BlockSpec((1, LANES, LANES), lambda g: (g, 0, 0)),
                  pl.BlockSpec((1, LANES), lambda g: (0, g))],
        out_specs=[pl.BlockSpec((S, LANES), lambda g: (0, g)),
                   pl.BlockSpec((1, LANES, LANES), lambda g: (g, 0, 0)),
                   pl.BlockSpec((1, LANES), lambda g: (0, g))],
        out_shape=[jax.ShapeDtypeStruct((S, G * LANES), BF16),
                   jax.ShapeDtypeStruct((G, LANES, LANES), F32),
                   jax.ShapeDtypeStruct((1, G * LANES), F32)],
        scratch_shapes=[pltpu.VMEM((S + MAX_WINDOW, LANES), F32)],
        compiler_params=_params("arbitrary"),
    )(proj, da, pool_w, pool_scale)


def _inproj_bwd_dx(dproj, dzf, wmain, wf, x, gam, g, after=()):
    S, D = x.shape
    N = wmain.shape[1]
    tm = min(TM // 2, S)

    def body(dp_ref, dz_ref, w_ref, wf_ref, x_ref, gam_ref, g_ref, *rest):
        dx_ref, dg_ref = rest[-2:]

        @pl.when(pl.program_id(0) == 0)
        def _():
            dg_ref[...] = jnp.zeros(dg_ref.shape, F32)

        dh = lax.dot_general(dp_ref[...], w_ref[...], NT, preferred_element_type=F32)
        dh = dh + lax.dot_general(dz_ref[...], wf_ref[...], NT, preferred_element_type=F32)
        xf = x_ref[...]
        r = lax.rsqrt(jnp.mean(xf * xf, axis=-1, keepdims=True) + RMS_EPS)
        xhat = xf * r
        dg_ref[...] += jnp.sum(dh * xhat, axis=0, keepdims=True)
        dxhat = dh * gam_ref[...]
        dx_ref[...] = g_ref[...] + r * (dxhat - xhat * jnp.mean(dxhat * xhat, axis=-1, keepdims=True))

    return pl.pallas_call(
        body, name="inproj_bwd_dx", grid=(S // tm,),
        in_specs=[pl.BlockSpec((tm, N), lambda i: (i, 0)),
                  pl.BlockSpec((tm, LANES), lambda i: (i, 0)),
                  pl.BlockSpec((D, N), lambda i: (0, 0)),
                  pl.BlockSpec((D, LANES), lambda i: (0, 0)),
                  pl.BlockSpec((tm, D), lambda i: (i, 0)),
                  pl.BlockSpec((1, D), lambda i: (0, 0)),
                  pl.BlockSpec((tm, D), lambda i: (i, 0))] + [pl.BlockSpec(memory_space=pl.ANY)] * len(after),
        out_specs=[pl.BlockSpec((tm, D), lambda i: (i, 0)),
                   pl.BlockSpec((1, D), lambda i: (0, 0))],
        out_shape=[jax.ShapeDtypeStruct((S, D), F32),
                   jax.ShapeDtypeStruct((1, D), F32)],
        compiler_params=_params("arbitrary"),
    )(dproj, dzf, wmain, wf, x, gam, g, *after)


def _adamw(w, m, v, gparts, name, rows):
    A, R, C = w.shape
    tr = min(rows, R)
    c1 = 1.0 / (1.0 - ADAM_B1 ** ADAM_STEP)
    c2 = 1.0 / (1.0 - ADAM_B2 ** ADAM_STEP)

    def body(w_ref, m_ref, v_ref, *rest):
        gp_refs = rest[:A]
        g_ref, d_ref, nm_ref, nv_ref = rest[A:]
        for l in range(A):
            @pl.when(pl.program_id(0) == l)
            def _():
                g = gp_refs[l][0]
                for s in range(1, N_DEV):
                    g = g + gp_refs[l][s]
                nm = ADAM_B1 * m_ref[0] + (1.0 - ADAM_B1) * g
                nv = ADAM_B2 * v_ref[0] + (1.0 - ADAM_B2) * (g * g)
                g_ref[0] = g
                nm_ref[0] = nm
                nv_ref[0] = nv
                d_ref[0] = -ADAM_LR * ((nm * c1) / (jnp.sqrt(nv * c2) + ADAM_EPS) + ADAM_WD * w_ref[0])

    spec = pl.BlockSpec((1, tr, C), lambda a, r: (a, r, 0))
    gp_specs = [pl.BlockSpec((N_DEV, tr, C), lambda a, r, l=l: (0, jnp.where(a == l, r, 0), 0)) for l in range(A)]
    shape = jax.ShapeDtypeStruct((A, R, C), F32)
    return pl.pallas_call(
        body, name=name, grid=(A, R // tr),
        in_specs=[spec, spec, spec] + gp_specs,
        out_specs=[spec, spec, spec, spec],
        out_shape=[shape, shape, shape, shape],
        compiler_params=_params("arbitrary", "arbitrary"),
    )(w, m, v, *gparts)


def _position():
    return lax.axis_index("x"), lax.axis_index("y"), lax.axis_index("c")


def _index(dev):
    return 4 * dev[0] + 2 * dev[1] + dev[2]


def _all_gather(arrs, slots, out_shapes, name):
    n_arr = len(arrs)

    def body(*refs):
        ins, outs = refs[:n_arr], refs[n_arr:2 * n_arr]
        send_sems, recv_sems, local_sems = refs[2 * n_arr:]
        x, y, c = _position()
        me, sibling = (x, y, c), (x, y, 1 - c)
        chips = [(1 - x, y), (x, 1 - y), (1 - x, 1 - y)]

        def copy(a, k, block, to, src=None):
            part = slots[a](outs[a], _index(block))
            return pltpu.make_async_remote_copy(
                src_ref=part if src is None else src, dst_ref=part,
                send_sem=send_sems.at[a, k], recv_sem=recv_sems.at[a, k],
                device_id=to, device_id_type=MESH)

        mine = [pltpu.make_async_copy(ins[a], slots[a](outs[a], _index(me)), local_sems.at[a])
                for a in range(n_arr)]
        for cp in mine:
            cp.start()
        first = []
        for a in range(n_arr):
            first.append(copy(a, 0, me, sibling, src=ins[a]))
            first += [copy(a, 1 + j, me, (*chip, c), src=ins[a]) for j, chip in enumerate(chips)]
        for cp in first:
            cp.start()
        passed = []
        for j, chip in enumerate(chips):
            for a in range(n_arr):
                copy(a, 1 + j, (*chip, c), me).wait_recv()
                fwd = copy(a, 4 + j, (*chip, c), sibling)
                fwd.start()
                passed.append(fwd)
        for a in range(n_arr):
            copy(a, 0, sibling, me).wait_recv()
            for j, chip in enumerate(chips):
                copy(a, 4 + j, (*chip, 1 - c), me).wait_recv()
        for cp in first + passed:
            cp.wait_send()
        for cp in mine:
            cp.wait()

    any_spec = pl.BlockSpec(memory_space=pl.ANY)
    return pl.pallas_call(
        body, name=name,
        in_specs=[any_spec] * n_arr, out_specs=[any_spec] * n_arr, out_shape=out_shapes,
        scratch_shapes=[pltpu.SemaphoreType.DMA((n_arr, 7)), pltpu.SemaphoreType.DMA((n_arr, 7)),
                        pltpu.SemaphoreType.DMA((n_arr,))],
    )(*arrs)


def _split_copies(srcs, lands, send_sems, recv_sems, kinds):
    x, y, c = _position()
    me = _index((x, y, c))
    copies = []
    for a, (src_part, land_part) in enumerate(kinds):
        for k in range(1, N_DEV):
            peer = (x ^ ((k >> 2) & 1), y ^ ((k >> 1) & 1), c ^ (k & 1))
            copies.append(pltpu.make_async_remote_copy(
                src_ref=src_part(srcs[a], _index(peer)), dst_ref=land_part(lands[a], me, k),
                send_sem=send_sems[a].at[k - 1], recv_sem=recv_sems[a].at[k - 1],
                device_id=peer, device_id_type=MESH))
    return copies


def _split_start(srcs, lands, kinds, name):
    n = len(srcs)

    def body(*refs):
        src_refs, land_refs = refs[:n], refs[n:2 * n]
        send_sems, recv_sems = refs[2 * n:3 * n], refs[3 * n:4 * n]
        token = refs[-1]
        for cp in _split_copies(src_refs, land_refs, send_sems, recv_sems, kinds):
            cp.start()
        token[...] = jnp.zeros(token.shape, token.dtype)

    hbm = pl.BlockSpec(memory_space=pltpu.HBM)
    sem = pl.BlockSpec(memory_space=pltpu.SEMAPHORE)
    operands = [pltpu.with_memory_space_constraint(t, pltpu.HBM) for t in (*srcs, *lands)]
    out = pl.pallas_call(
        body, name=name,
        in_specs=[hbm] * (2 * n),
        out_specs=[sem] * (2 * n) + [hbm] * (2 * n) + [pl.BlockSpec(memory_space=pltpu.VMEM)],
        out_shape=[pltpu.SemaphoreType.DMA((N_DEV - 1,))] * (2 * n)
        + [pltpu.HBM(t.shape, t.dtype) for t in operands] + [jax.ShapeDtypeStruct((8, LANES), F32)],
        input_output_aliases={i: 2 * n + i for i in range(2 * n)},
        compiler_params=pltpu.CompilerParams(has_side_effects=pltpu.SideEffectType.DATAFLOW_SIDE_EFFECTING),
    )(*operands)
    return [(out[a], out[n + a], out[2 * n + a], out[3 * n + a]) for a in range(n)], out[-1]


def _split_wait(started, kinds, after, name):
    n = len(started)
    sems = [t[0] for t in started] + [t[1] for t in started]
    srcs = [t[2] for t in started]
    lands = [t[3] for t in started]

    def body(*refs):
        src_refs, land_refs = refs[:n], refs[n:2 * n]
        send_sems, recv_sems = refs[2 * n:3 * n], refs[3 * n:4 * n]
        for cp in _split_copies(src_refs, land_refs, send_sems, recv_sems, kinds):
            cp.wait_send()
            cp.wait_recv()

    hbm = pl.BlockSpec(memory_space=pltpu.HBM)
    sem = pl.BlockSpec(memory_space=pltpu.SEMAPHORE)
    out = pl.pallas_call(
        body, name=name,
        in_specs=[hbm] * (2 * n) + [sem] * (2 * n) + [pl.BlockSpec(memory_space=pl.ANY)] * len(after),
        out_specs=[hbm] * (2 * n),
        out_shape=[pltpu.HBM(t.shape, t.dtype) for t in (*srcs, *lands)],
        input_output_aliases={i: i for i in range(2 * n)},
        compiler_params=pltpu.CompilerParams(has_side_effects=pltpu.SideEffectType.DATAFLOW_SIDE_EFFECTING),
    )(*srcs, *lands, *sems, *after)
    return out[n:]


def _pack(parts, rows):
    flat = jnp.concatenate([p.reshape(-1) for p in parts])
    return jnp.pad(flat, (0, rows * LANES - flat.shape[0])).reshape(1, rows, LANES)


def _unpack(packed, like):
    flat = packed.reshape(-1)
    out, off = [], 0
    for p in like:
        out.append(flat[off:off + p.size].reshape(p.shape))
        off += p.size
    return out


def _local_step(x, target, norm_g, forget_bias, pool_w, pool_scale, final_g, weights_in, weights_out, on_grads,
                first_after=()):
    L = norm_g.shape[0]
    S, D = x.shape
    W = D // 2
    H = W // HEAD_DIM
    bias = jnp.pad(forget_bias, ((0, 0), (0, LANES - H)))

    saved = []
    after = tuple(first_after)
    for l in range(L):
        wmain, wf = weights_in(l, x)
        proj, h, z = _inproj_fwd(x, norm_g[l:l + 1], wmain, wf, after)
        after = ()
        ccol, crow = _fgate_fwd(z, bias[l:l + 1], H)
        ypool = _pool_fwd(proj, pool_w[l], pool_scale[l:l + 1])
        o, lse = _attn_fwd(proj, ccol, crow)
        wout = weights_out(l, o)
        x_new, mixed = _outproj_fwd(ypool, o, proj, x, wout)
        saved.append((x, proj, h, z, ccol, crow, ypool, o, lse, mixed, wmain, wf, wout))
        x = x_new

    g, loss, d_final_g = _loss_head(x, final_g.reshape(1, D), target)

    d_norm_g, d_bias, d_pool_w, d_pool_scale = [], [], [], []
    for l in reversed(range(L)):
        x_in, proj, h, z, ccol, crow, ypool, o, lse, mixed, wmain, wf, wout = saved[l]
        d_wout = _matmul_tn(mixed, g, "outproj_bwd_dw")
        da, dgate = _outproj_bwd_dx(g, wout, ypool, o, proj)
        delta = _attn_delta(da, o)
        dq, dk, dv, dcq, dcn = _attn_bwd(proj, da, lse, delta, ccol, crow)
        dzf, db = _fgate_bwd(dcq, dcn, z, bias[l:l + 1], H)
        dpu, dpw, dps = _pool_bwd(proj, da, pool_w[l], pool_scale[l:l + 1])
        dproj = jnp.concatenate([dpu, dgate[:, :W], dq.astype(BF16), dk, dv, dgate[:, W:]], axis=1)
        d_wmain = _matmul_tn(h, dproj, "inproj_bwd_dw")
        d_wf = _matmul_tn(h, dzf, "inproj_bwd_dwf")
        after = tuple(on_grads(l, d_wmain, d_wf[:, :H], d_wout))
        g, dgam = _inproj_bwd_dx(dproj, dzf, wmain, wf, x_in, norm_g[l:l + 1], g, after)
        d_norm_g.append(dgam[0])
        d_bias.append(db[0, :H])
        d_pool_w.append(dpw)
        d_pool_scale.append(dps[0])

    stack = lambda parts: jnp.stack(parts[::-1])
    grads = dict(norm_g=stack(d_norm_g), forget_bias=stack(d_bias), pool_w=stack(d_pool_w),
                 pool_scale=stack(d_pool_scale), final_g=d_final_g[0])
    return loss[0, 0], g, grads


def kernel(x, norm_g, w_in, forget_bias, pool_w, pool_scale, w_out, final_g, loss_target, m_norm_g, m_w_in, m_forget_bias, m_pool_w, m_pool_scale, m_w_out, m_final_g, v_norm_g, v_w_in, v_forget_bias, v_pool_w, v_pool_scale, v_w_out, v_final_g):
    L, D, cols = w_in.shape
    rows_out = w_out.shape[1]
    W = D // 2
    H = W // HEAD_DIM
    me = _index(_position())

    win_b, wout_b = w_in.astype(BF16), w_out.astype(BF16)
    gather_in = (lambda ref, peer: ref, lambda ref, mine, k: ref.at[mine])
    gather_out = (lambda ref, peer: ref, lambda ref, mine, k: ref.at[pl.ds(mine * rows_out, rows_out), :])

    def land_in(l):
        return jnp.broadcast_to(win_b[l][None], (N_DEV, D, cols))

    def land_out(l):
        return jnp.tile(wout_b[l], (N_DEV, 1))

    (first_in,), _ = _split_start([win_b[0]], [land_in(0)], [gather_in], "gather_start_first")
    rest_srcs = [wout_b[0]] + [w[l] for l in range(1, L) for w in (win_b, wout_b)]
    rest_lands = [land_out(0)] + [land(l) for l in range(1, L) for land in (land_in, land_out)]
    rest_kinds = [gather_out] + [gather_in, gather_out] * (L - 1)
    rest, rest_token = _split_start(rest_srcs, rest_lands, rest_kinds, "gather_start_rest")

    def weights_in(l, x_in):
        started = first_in if l == 0 else rest[2 * l - 1]
        (win_all,) = _split_wait([started], [gather_in], (x_in,), f"gather_wait_in_{l}")
        w_full = jnp.transpose(win_all, (1, 0, 2)).reshape(D, N_DEV * cols)
        return w_full[:, :6 * W], jnp.pad(w_full[:, 6 * W:], ((0, 0), (0, LANES - H)))

    def weights_out(l, o):
        (wout_full,) = _split_wait([rest[2 * l]], [gather_out], (o,), f"gather_wait_out_{l}")
        return wout_full

    exchange_kinds = [(lambda ref, peer: ref.at[peer], lambda ref, mine, k: ref.at[k]),
                      (lambda ref, peer: ref.at[pl.ds(peer * rows_out, rows_out), :], lambda ref, mine, k: ref.at[k])]
    exchanges = {}

    def on_grads(l, d_wmain, d_wf, d_wout):
        dw_in = jnp.concatenate([d_wmain, d_wf], axis=1)
        dw_in = jnp.transpose(dw_in.reshape(D, N_DEV, cols), (1, 0, 2))
        own_in = lax.dynamic_index_in_dim(dw_in, me, 0, keepdims=True)
        own_out = lax.dynamic_slice_in_dim(d_wout, me * rows_out, rows_out, 0)[None]
        lands = [lax.dynamic_update_slice(jnp.zeros((N_DEV, D, cols), F32), own_in, (0, 0, 0)),
                 lax.dynamic_update_slice(jnp.zeros((N_DEV, rows_out, D), F32), own_out, (0, 0, 0))]
        exchanges[l], token = _split_start([dw_in, d_wout], lands, exchange_kinds, f"exchange_start_{l}")
        return (token,)

    loss, dx, grads = _local_step(x[0], loss_target[0], norm_g, forget_bias, pool_w, pool_scale, final_g,
                                  weights_in, weights_out, on_grads, (rest_token,))
    loss = lax.psum(loss, ("x", "y", "c"))

    names = ["norm_g", "forget_bias", "pool_w", "pool_scale", "final_g"]
    small_w = [norm_g, forget_bias, pool_w, pool_scale, final_g]
    small_m = [m_norm_g, m_forget_bias, m_pool_w, m_pool_scale, m_final_g]
    small_v = [v_norm_g, v_forget_bias, v_pool_w, v_pool_scale, v_final_g]
    total = sum(p.size for p in small_w)
    rows = -(-total // (8 * LANES)) * 8
    (gs_all,) = _all_gather(
        [_pack([grads[n] for n in names], rows)],
        [lambda ref, n: ref.at[n]],
        [jax.ShapeDtypeStruct((N_DEV, 1, rows, LANES), F32)], "gather_small_grads")
    packed = _adamw(_pack(small_w, rows), _pack(small_m, rows), _pack(small_v, rows),
                    [gs_all.reshape(N_DEV, rows, LANES)], "adamw_small", rows)
    g_s, d_s, nm_s, nv_s = [_unpack(p, small_w) for p in packed]

    gin_parts, gout_parts = [], []
    for l in range(L):
        got_in, got_out = _split_wait(exchanges[l], exchange_kinds, (dx, packed[1]), f"exchange_wait_{l}")
        gin_parts.append(got_in)
        gout_parts.append(got_out)
    g_w_in, d_w_in, nm_w_in, nv_w_in = _adamw(w_in, m_w_in, v_w_in, gin_parts, "adamw_w_in", 256)
    g_w_out, d_w_out, nm_w_out, nv_w_out = _adamw(w_out, m_w_out, v_w_out, gout_parts, "adamw_w_out", 128)

    def order(big_in, big_out, small):
        return (small[0], big_in, small[1], small[2], small[3], big_out, small[4])

    return (loss, dx[None], *order(g_w_in, g_w_out, g_s), *order(d_w_in, d_w_out, d_s),
            *order(nm_w_in, nm_w_out, nm_s), *order(nv_w_in, nv_w_out, nv_s))
```

```python
import math

import numpy as np

import jax
import jax.numpy as jnp
from jax import lax
from jax.experimental import pallas as pl
from jax.experimental.pallas import tpu as pltpu

F32 = jnp.float32
BF16 = jnp.bfloat16
MESH = pl.DeviceIdType.MESH

RMS_EPS = 1e-6
NEG_INF = -1e30
HEAD_DIM = 64
POOL_WINDOWS = (2, 4, 8, 16)
MAX_WINDOW = 16
LANES = 128
N_DEV = 8

ADAM_LR = 0.001
ADAM_B1 = 0.9
ADAM_B2 = 0.999
ADAM_EPS = 1e-08
ADAM_WD = 0.01
ADAM_STEP = 10

TM = 512
TN = 512
TQ = 512
TB = 256
VMEM_LIMIT = 56 * 1024 * 1024

NT = (((1,), (1,)), ((), ()))
TN_DIMS = (((0,), (0,)), ((), ()))

SLOT_C, SLOT_ONE, SLOT_LSE = 0, 3, 6


def _params(*sem):
    return pltpu.CompilerParams(dimension_semantics=sem, vmem_limit_bytes=VMEM_LIMIT)


def _sigmoid(x):
    return 1.0 / (1.0 + jnp.exp(-x))


def _split3(x):
    hi = x.astype(BF16).astype(F32)
    rest = x - hi
    mid = rest.astype(BF16).astype(F32)
    return hi, mid, rest - mid


def _after_specs(after):
    return [pl.BlockSpec(memory_space=pl.ANY)] * len(after)


def _inproj_fwd(x, gam, wmain, wf, after=()):
    S, D = x.shape
    N = wmain.shape[1]
    tm, tn = min(TM, S), TN

    def body(x_ref, g_ref, w_ref, wf_ref, *rest):
        proj_ref, h_ref, z_ref = rest[-3:]

        @pl.when(pl.program_id(1) == 0)
        def _():
            xf = x_ref[...]
            r = lax.rsqrt(jnp.mean(xf * xf, axis=-1, keepdims=True) + RMS_EPS)
            h = ((xf * r) * g_ref[...]).astype(BF16)
            h_ref[...] = h
            z_ref[...] = jnp.dot(h, wf_ref[...], preferred_element_type=F32)

        proj_ref[...] = jnp.dot(h_ref[...], w_ref[...], preferred_element_type=F32)

    return pl.pallas_call(
        body, name="inproj_fwd", grid=(S // tm, N // tn),
        in_specs=[pl.BlockSpec((tm, D), lambda i, j: (i, 0)),
                  pl.BlockSpec((1, D), lambda i, j: (0, 0)),
                  pl.BlockSpec((D, tn), lambda i, j: (0, j)),
                  pl.BlockSpec((D, LANES), lambda i, j: (0, 0))] + _after_specs(after),
        out_specs=[pl.BlockSpec((tm, tn), lambda i, j: (i, j)),
                   pl.BlockSpec((tm, D), lambda i, j: (i, 0)),
                   pl.BlockSpec((tm, LANES), lambda i, j: (i, 0))],
        out_shape=[jax.ShapeDtypeStruct((S, N), F32),
                   jax.ShapeDtypeStruct((S, D), BF16),
                   jax.ShapeDtypeStruct((S, LANES), F32)],
        compiler_params=_params("parallel", "arbitrary"),
    )(x, gam, wmain, wf, *after)


def _fgate_fwd(z, bias, n_heads):
    S = z.shape[0]
    tb = min(TB, S)
    P = n_heads // 2

    def body(z_ref, b_ref, qaug_ref, kaug_ref):
        lane = lax.broadcasted_iota(jnp.int32, (tb, LANES), 1)
        slot = lane & (HEAD_DIM - 1)
        tri = (lax.broadcasted_iota(jnp.int32, (tb, tb), 0)
               >= lax.broadcasted_iota(jnp.int32, (tb, tb), 1)).astype(F32)
        q_ones = jnp.logical_and(slot >= SLOT_ONE, slot < SLOT_ONE + 3)
        k_ones = jnp.logical_or(slot < SLOT_C + 3, jnp.logical_and(slot >= SLOT_LSE, slot < SLOT_LSE + 3))
        q_base = jnp.where(q_ones, 1.0, 0.0)
        k_base = jnp.where(k_ones, 1.0, 0.0)

        def block(i, carry):
            r0 = pl.multiple_of(i * tb, tb)
            zz = z_ref[pl.ds(r0, tb), :] + b_ref[...]
            lf = jnp.minimum(zz, 0.0) - jnp.log(1.0 + jnp.exp(-jnp.abs(zz)))
            lf = jnp.where(lane < n_heads, lf, 0.0)
            c = jnp.dot(tri, lf, preferred_element_type=F32, precision=lax.Precision.HIGHEST) + carry
            parts = _split3(c)
            for p in range(P):
                qa, ka = q_base, k_base
                for h, base in ((2 * p, HEAD_DIM), (2 * p + 1, 0)):
                    for n, part in enumerate(parts):
                        col = jnp.broadcast_to(part[:, h:h + 1], (tb, LANES))
                        qa = jnp.where(lane == base + SLOT_C + n, col, qa)
                        ka = jnp.where(lane == base + SLOT_ONE + n, -col, ka)
                qaug_ref[pl.ds(r0, tb), p * LANES:(p + 1) * LANES] = qa.astype(BF16)
                kaug_ref[pl.ds(r0, tb), p * LANES:(p + 1) * LANES] = ka.astype(BF16)
            return c[tb - 1:tb, :]

        lax.fori_loop(0, S // tb, block, jnp.zeros((1, LANES), F32))

    return pl.pallas_call(
        body, name="fgate_fwd",
        out_shape=[jax.ShapeDtypeStruct((S, P * LANES), BF16),
                   jax.ShapeDtypeStruct((S, P * LANES), BF16)],
        compiler_params=pltpu.CompilerParams(vmem_limit_bytes=VMEM_LIMIT),
    )(z, bias)


def _window_mean_minus_self(u, pad_ref, w, S):
    pad_ref[0:MAX_WINDOW, :] = jnp.zeros((MAX_WINDOW, LANES), F32)
    pad_ref[MAX_WINDOW:MAX_WINDOW + S, :] = u
    acc = u
    for j in range(1, w):
        acc = acc + pad_ref[MAX_WINDOW - j:MAX_WINDOW - j + S, :]
    t = lax.broadcasted_iota(jnp.int32, (S, LANES), 0)
    cnt = jnp.minimum(t + 1, w).astype(F32)
    return acc / cnt - u, cnt


def _pool_fwd(proj, pool_w, pool_scale):
    S = proj.shape[0]
    G = len(POOL_WINDOWS)

    def body(u_ref, w_ref, s_ref, y_ref, pad_ref):
        g = pl.program_id(0)
        for gi, w in enumerate(POOL_WINDOWS):
            @pl.when(g == gi)
            def _():
                d, _ = _window_mean_minus_self(u_ref[...], pad_ref, w, S)
                y = jnp.dot(d.astype(BF16), w_ref[0].astype(BF16), preferred_element_type=F32)
                y_ref[...] = y * s_ref[...]

    return pl.pallas_call(
        body, name="pool_fwd", grid=(G,),
        in_specs=[pl.BlockSpec((S, LANES), lambda g: (0, g)),
                  pl.BlockSpec((1, LANES, LANES), lambda g: (g, 0, 0)),
                  pl.BlockSpec((1, LANES), lambda g: (0, g))],
        out_specs=pl.BlockSpec((S, LANES), lambda g: (0, g)),
        out_shape=jax.ShapeDtypeStruct((S, G * LANES), F32),
        scratch_shapes=[pltpu.VMEM((S + MAX_WINDOW, LANES), F32)],
        compiler_params=_params("arbitrary"),
    )(proj, pool_w, pool_scale)


def _head_halves(rows):
    lane = lax.broadcasted_iota(jnp.int32, (rows, LANES), 1)
    return lane, (lane < HEAD_DIM, lane >= HEAD_DIM)


def _causal_steps(n, by_key):
    pairs = [(i, j) for j in range(n) for i in range(j, n)] if by_key else \
            [(i, j) for i in range(n) for j in range(i + 1)]
    return (np.array([p[0] for p in pairs], np.int32), np.array([p[1] for p in pairs], np.int32))


def _attn_fwd(proj, qaug, kaug):
    S = proj.shape[0]
    W = proj.shape[1] // 6
    P = W // LANES
    tq = min(TQ, S)
    nq = S // tq
    qc, kc, vc = 2 * P, 3 * P, 4 * P
    scale = 1.0 / math.sqrt(HEAD_DIM)
    i_tab, j_tab = _causal_steps(nq, by_key=False)

    def body(it_ref, jt_ref, q_ref, k_ref, v_ref, qa_ref, ka_ref, o_ref, lse_ref, qm_scr, m_scr, acc_scr):
        step = pl.program_id(1)
        i, j = it_ref[step], jt_ref[step]
        lane, halves = _head_halves(tq)
        v_ones = ((lane & (HEAD_DIM - 1)) < 3).astype(BF16)

        @pl.when(j == 0)
        def _():
            qs = (q_ref[...] * scale).astype(BF16)
            qm_scr[0] = jnp.where(halves[0], qs, qa_ref[...])
            qm_scr[1] = jnp.where(halves[1], qs, qa_ref[...])
            m_scr[...] = jnp.full(m_scr.shape, NEG_INF, F32)
            acc_scr[...] = jnp.zeros(acc_scr.shape, F32)

        def update(on_diagonal):
            k2 = k_ref[...].astype(BF16)
            v2 = v_ref[...].astype(BF16)
            if on_diagonal:
                keep = (lax.broadcasted_iota(jnp.int32, (tq, tq), 0)
                        >= lax.broadcasted_iota(jnp.int32, (tq, tq), 1))
            for a in range(2):
                ka = jnp.where(halves[a], k2, ka_ref[...])
                va = jnp.where(halves[a], v2, v_ones)
                s = lax.dot_general(qm_scr[a], ka, NT, preferred_element_type=F32)
                if on_diagonal:
                    s = jnp.where(keep, s, NEG_INF)
                m_prev = m_scr[a]
                m_new = jnp.maximum(m_prev, jnp.max(s, axis=1, keepdims=True))
                p = jnp.exp(s - jnp.tile(m_new, (1, tq // LANES)))
                acc_scr[a] = jnp.exp(m_prev - m_new) * acc_scr[a] + jnp.dot(p.astype(BF16), va,
                                                                              preferred_element_type=F32)
                m_scr[a] = m_new

        @pl.when(j < i)
        def _():
            update(False)

        @pl.when(j == i)
        def _():
            update(True)
            acc_a, acc_b = acc_scr[0], acc_scr[1]
            l_a, l_b = acc_a[:, HEAD_DIM:HEAD_DIM + 1], acc_b[:, 0:1]
            o_ref[...] = jnp.where(halves[0], acc_a / l_a, acc_b / l_b)
            lse_ref[...] = jnp.where(halves[0], m_scr[1] + jnp.log(l_b), m_scr[0] + jnp.log(l_a))

    grid_spec = pltpu.PrefetchScalarGridSpec(
        num_scalar_prefetch=2, grid=(P, len(i_tab)),
        in_specs=[pl.BlockSpec((tq, LANES), lambda p, s, it, jt: (it[s], qc + p)),
                  pl.BlockSpec((tq, LANES), lambda p, s, it, jt: (jt[s], kc + p)),
                  pl.BlockSpec((tq, LANES), lambda p, s, it, jt: (jt[s], vc + p)),
                  pl.BlockSpec((tq, LANES), lambda p, s, it, jt: (it[s], p)),
                  pl.BlockSpec((tq, LANES), lambda p, s, it, jt: (jt[s], p))],
        out_specs=[pl.BlockSpec((tq, LANES), lambda p, s, it, jt: (it[s], p)),
                   pl.BlockSpec((tq, LANES), lambda p, s, it, jt: (it[s], p))],
        scratch_shapes=[pltpu.VMEM((2, tq, LANES), BF16),
                        pltpu.VMEM((2, tq, LANES), F32),
                        pltpu.VMEM((2, tq, LANES), F32)])
    return pl.pallas_call(
        body, name="attn_fwd", grid_spec=grid_spec,
        out_shape=[jax.ShapeDtypeStruct((S, W), F32), jax.ShapeDtypeStruct((S, W), F32)],
        compiler_params=_params("parallel", "arbitrary"),
    )(jnp.asarray(i_tab), jnp.asarray(j_tab), proj, proj, proj, qaug, kaug)


def _outproj_fwd(ypool, o, proj, x, wout):
    S, D = x.shape
    W = D // 2
    tm, tn = min(TM, S), TN

    def body(y_ref, o_ref, pg_ref, ag_ref, x_ref, w_ref, xn_ref, mix_ref):
        @pl.when(pl.program_id(1) == 0)
        def _():
            pg, ag = pg_ref[...], ag_ref[...]
            mix_ref[:, 0:W] = (y_ref[...] * (pg * _sigmoid(pg))).astype(BF16)
            mix_ref[:, W:D] = (o_ref[...] * (ag * _sigmoid(ag))).astype(BF16)

        xn_ref[...] = x_ref[...] + jnp.dot(mix_ref[...], w_ref[...], preferred_element_type=F32)

    return pl.pallas_call(
        body, name="outproj_fwd", grid=(S // tm, D // tn),
        in_specs=[pl.BlockSpec((tm, W), lambda i, j: (i, 0)),
                  pl.BlockSpec((tm, W), lambda i, j: (i, 0)),
                  pl.BlockSpec((tm, W), lambda i, j: (i, 1)),
                  pl.BlockSpec((tm, W), lambda i, j: (i, 5)),
                  pl.BlockSpec((tm, tn), lambda i, j: (i, j)),
                  pl.BlockSpec((D, tn), lambda i, j: (0, j))],
        out_specs=[pl.BlockSpec((tm, tn), lambda i, j: (i, j)),
                   pl.BlockSpec((tm, D), lambda i, j: (i, 0))],
        out_shape=[jax.ShapeDtypeStruct((S, D), F32),
                   jax.ShapeDtypeStruct((S, D), BF16)],
        compiler_params=_params("parallel", "arbitrary"),
    )(ypool, o, proj, proj, x, wout)


def _loss_head(x, gam, target):
    S, D = x.shape
    tm = min(TM, S)

    def body(x_ref, g_ref, t_ref, dx_ref, loss_ref, dg_ref):
        @pl.when(pl.program_id(0) == 0)
        def _():
            loss_ref[...] = jnp.zeros(loss_ref.shape, F32)
            dg_ref[...] = jnp.zeros(dg_ref.shape, F32)

        xf, gam_v = x_ref[...], g_ref[...]
        r = lax.rsqrt(jnp.mean(xf * xf, axis=-1, keepdims=True) + RMS_EPS)
        xhat = xf * r
        err = xhat * gam_v - t_ref[...]
        part = jnp.sum(jnp.sum(err * err, axis=-1, keepdims=True), axis=0, keepdims=True)
        loss_ref[...] += part * (0.5 / D)
        dy = err * (1.0 / D)
        dg_ref[...] += jnp.sum(dy * xhat, axis=0, keepdims=True)
        dxhat = dy * gam_v
        dx_ref[...] = r * (dxhat - xhat * jnp.mean(dxhat * xhat, axis=-1, keepdims=True))

    return pl.pallas_call(
        body, name="loss_head", grid=(S // tm,),
        in_specs=[pl.BlockSpec((tm, D), lambda i: (i, 0)),
                  pl.BlockSpec((1, D), lambda i: (0, 0)),
                  pl.BlockSpec((tm, D), lambda i: (i, 0))],
        out_specs=[pl.BlockSpec((tm, D), lambda i: (i, 0)),
                   pl.BlockSpec((8, LANES), lambda i: (0, 0)),
                   pl.BlockSpec((1, D), lambda i: (0, 0))],
        out_shape=[jax.ShapeDtypeStruct((S, D), F32),
                   jax.ShapeDtypeStruct((8, LANES), F32),
                   jax.ShapeDtypeStruct((1, D), F32)],
        compiler_params=_params("arbitrary"),
    )(x, gam, target)


def _outproj_bwd_dx(g, wout, ypool, o, proj):
    S, D = g.shape
    W = D // 2
    tm = min(TM, S)

    def body(g_ref, w_ref, y_ref, o_ref, gate_ref, da_ref, dgate_ref):
        j = pl.program_id(1)
        dmix = lax.dot_general(g_ref[...].astype(BF16), w_ref[...], NT, preferred_element_type=F32)
        val = jnp.where(j == 0, y_ref[...], o_ref[...])
        gt = gate_ref[...]
        sg = _sigmoid(gt)
        da_ref[...] = dmix * (gt * sg)
        dgate_ref[...] = (dmix * val * (sg * (1.0 + gt * (1.0 - sg)))).astype(BF16)

    return pl.pallas_call(
        body, name="outproj_bwd_dx", grid=(S // tm, 2),
        in_specs=[pl.BlockSpec((tm, D), lambda i, j: (i, 0)),
                  pl.BlockSpec((W, D), lambda i, j: (j, 0)),
                  pl.BlockSpec((tm, W), lambda i, j: (i, 0)),
                  pl.BlockSpec((tm, W), lambda i, j: (i, 0)),
                  pl.BlockSpec((tm, W), lambda i, j: (i, 1 + 4 * j))],
        out_specs=[pl.BlockSpec((tm, W), lambda i, j: (i, j)),
                   pl.BlockSpec((tm, W), lambda i, j: (i, j))],
        out_shape=[jax.ShapeDtypeStruct((S, D), F32),
                   jax.ShapeDtypeStruct((S, D), BF16)],
        compiler_params=_params("parallel", "arbitrary"),
    )(g, wout, ypool, o, proj)


def _matmul_tn(a, b, name):
    S, M = a.shape
    N = b.shape[1]
    ts, tn = min(TM, S), min(TN, N)

    def body(a_ref, b_ref, out_ref):
        @pl.when(pl.program_id(1) == 0)
        def _():
            out_ref[...] = jnp.zeros(out_ref.shape, F32)

        out_ref[...] += lax.dot_general(a_ref[...], b_ref[...].astype(BF16), TN_DIMS,
                                        preferred_element_type=F32)

    return pl.pallas_call(
        body, name=name, grid=(N // tn, S // ts),
        in_specs=[pl.BlockSpec((ts, M), lambda n, k: (k, 0)),
                  pl.BlockSpec((ts, tn), lambda n, k: (k, n))],
        out_specs=pl.BlockSpec((M, tn), lambda n, k: (0, n)),
        out_shape=jax.ShapeDtypeStruct((M, N), F32),
        compiler_params=_params("parallel", "arbitrary"),
    )(a, b)


def _attn_bwd_prep(da, o, lse, qaug):
    S, D = da.shape
    W = D // 2
    P = W // LANES
    tq = min(TQ, S)

    def body(do_ref, o_ref, lse_ref, qa_ref, qb_ref, da_ref):
        lane, halves = _head_halves(tq)
        slot = lane & (HEAD_DIM - 1)
        prod = do_ref[...] * o_ref[...]
        d_a = jnp.sum(jnp.where(halves[0], prod, 0.0), axis=1, keepdims=True)
        d_b = jnp.sum(jnp.where(halves[1], prod, 0.0), axis=1, keepdims=True)
        aug = jnp.zeros((tq, LANES), F32)
        for n, part in enumerate(_split3(jnp.where(halves[0], d_b, d_a))):
            aug = jnp.where(slot == SLOT_C + n, -part, aug)
        da_ref[...] = aug.astype(BF16)
        aug = qa_ref[...].astype(F32)
        for n, part in enumerate(_split3(lse_ref[...])):
            aug = jnp.where(slot == SLOT_LSE + n, -part, aug)
        qb_ref[...] = aug.astype(BF16)

    spec = pl.BlockSpec((tq, LANES), lambda p, i: (i, p))
    return pl.pallas_call(
        body, name="attn_bwd_prep", grid=(P, S // tq),
        in_specs=[pl.BlockSpec((tq, LANES), lambda p, i: (i, P + p)), spec, spec, spec],
        out_specs=[spec, spec],
        out_shape=[jax.ShapeDtypeStruct((S, W), BF16), jax.ShapeDtypeStruct((S, W), BF16)],
        compiler_params=_params("parallel", "parallel"),
    )(da, o, lse, qaug)


def _attn_bwd(proj, da, qaug, kaug, doaug):
    S = proj.shape[0]
    W = proj.shape[1] // 6
    P = W // LANES
    tq = min(TQ, S)
    nq = S // tq
    qc, kc, vc = 2 * P, 3 * P, 4 * P
    scale = 1.0 / math.sqrt(HEAD_DIM)
    i_tab, j_tab = _causal_steps(nq, by_key=True)
    n_steps = len(i_tab)

    def body(it_ref, jt_ref, q_ref, k_ref, v_ref, do_ref, qa_ref, ka_ref, da_ref,
             dq_ref, dk_ref, dv_ref, dqx_ref, dkx_ref, km_scr, vm_scr, dk_scr, dv_scr, dq_scr):
        step = pl.program_id(1)
        i, j = it_ref[step], jt_ref[step]
        lane, halves = _head_halves(tq)

        @pl.when(step == 0)
        def _():
            dq_scr[...] = jnp.zeros(dq_scr.shape, F32)

        def update(on_diagonal):
            qs = (q_ref[...] * scale).astype(BF16)
            do2 = do_ref[...].astype(BF16)
            r0 = pl.multiple_of(i * tq, tq)
            if on_diagonal:
                keep = (lax.broadcasted_iota(jnp.int32, (tq, tq), 0)
                        >= lax.broadcasted_iota(jnp.int32, (tq, tq), 1))
            for a in range(2):
                qa = jnp.where(halves[a], qs, qa_ref[...])
                doa = jnp.where(halves[a], do2, da_ref[...])
                do0 = jnp.where(halves[a], do2, jnp.zeros_like(do2))
                s = lax.dot_general(qa, km_scr[a], NT, preferred_element_type=F32)
                if on_diagonal:
                    s = jnp.where(keep, s, NEG_INF)
                p = jnp.exp(s)
                ds = p * lax.dot_general(doa, vm_scr[a], NT, preferred_element_type=F32)
                dsb = ds.astype(BF16)
                dv_scr[...] += lax.dot_general(p.astype(BF16), do0, TN_DIMS, preferred_element_type=F32)
                dk_scr[a] += lax.dot_general(dsb, qa, TN_DIMS, preferred_element_type=F32)
                dq_scr[a, pl.ds(r0, tq), :] += jnp.dot(dsb, km_scr[a], preferred_element_type=F32)

        @pl.when(i == j)
        def _():
            k2 = k_ref[...].astype(BF16)
            v2 = v_ref[...].astype(BF16)
            v_ones = ((lane & (HEAD_DIM - 1)) < 3).astype(BF16)
            for a in range(2):
                km_scr[a] = jnp.where(halves[a], k2, ka_ref[...])
                vm_scr[a] = jnp.where(halves[a], v2, v_ones)
            dk_scr[...] = jnp.zeros(dk_scr.shape, F32)
            dv_scr[...] = jnp.zeros(dv_scr.shape, F32)
            update(True)

        @pl.when(i > j)
        def _():
            update(False)

        @pl.when(i == nq - 1)
        def _():
            dk_ref[...] = jnp.where(halves[0], dk_scr[0], dk_scr[1]).astype(BF16)
            dkx_ref[...] = jnp.where(halves[0], dk_scr[1], dk_scr[0])
            dv_ref[...] = dv_scr[...].astype(BF16)

        @pl.when(step == n_steps - 1)
        def _():
            row_lane, row_halves = _head_halves(S)
            dq_ref[...] = (jnp.where(row_halves[0], dq_scr[0], dq_scr[1]) * scale).astype(BF16)
            dqx_ref[...] = jnp.where(row_halves[0], dq_scr[1], dq_scr[0])

    def q_tile(col):
        return pl.BlockSpec((tq, LANES), lambda p, s, it, jt: (it[s], col + p))

    def k_tile(col):
        return pl.BlockSpec((tq, LANES), lambda p, s, it, jt: (jt[s], col + p))

    whole = pl.BlockSpec((S, LANES), lambda p, s, it, jt: (0, p))
    grid_spec = pltpu.PrefetchScalarGridSpec(
        num_scalar_prefetch=2, grid=(P, n_steps),
        in_specs=[q_tile(qc), k_tile(kc), k_tile(vc), q_tile(P), q_tile(0), k_tile(0), q_tile(0)],
        out_specs=[whole, k_tile(0), k_tile(0), whole, k_tile(0)],
        scratch_shapes=[pltpu.VMEM((2, tq, LANES), BF16),
                        pltpu.VMEM((2, tq, LANES), BF16),
                        pltpu.VMEM((2, tq, LANES), F32),
                        pltpu.VMEM((tq, LANES), F32),
                        pltpu.VMEM((2, S, LANES), F32)])
    return pl.pallas_call(
        body, name="attn_bwd", grid_spec=grid_spec,
        out_shape=[jax.ShapeDtypeStruct((S, W), BF16),
                   jax.ShapeDtypeStruct((S, W), BF16),
                   jax.ShapeDtypeStruct((S, W), BF16),
                   jax.ShapeDtypeStruct((S, W), F32),
                   jax.ShapeDtypeStruct((S, W), F32)],
        compiler_params=_params("parallel", "arbitrary"),
    )(jnp.asarray(i_tab), jnp.asarray(j_tab), proj, proj, proj, da, qaug, kaug, doaug)


def _fgate_bwd(dqx, dkx, z, bias, n_heads):
    S = z.shape[0]
    tb = min(TB, S)
    nb = S // tb

    def body(dqx_ref, dkx_ref, z_ref, b_ref, dz_ref, db_ref):
        lane = lax.broadcasted_iota(jnp.int32, (tb, LANES), 1)
        tri = (lax.broadcasted_iota(jnp.int32, (tb, tb), 1)
               >= lax.broadcasted_iota(jnp.int32, (tb, tb), 0)).astype(F32)

        def block(n, carry):
            carry_rc, db = carry
            r0 = pl.multiple_of((nb - 1 - n) * tb, tb)
            rows = dqx_ref[pl.ds(r0, tb), :]
            cols = dkx_ref[pl.ds(r0, tb), :]
            dc = jnp.zeros((tb, LANES), F32)
            for h in range(n_heads):
                at = (h // 2) * LANES + (HEAD_DIM if h % 2 == 0 else 0)
                one = rows[:, at + SLOT_C:at + SLOT_C + 1] - cols[:, at + SLOT_ONE:at + SLOT_ONE + 1]
                dc = jnp.where(lane == h, jnp.broadcast_to(one, (tb, LANES)), dc)
            rc = jnp.dot(tri, dc, preferred_element_type=F32, precision=lax.Precision.HIGHEST) + carry_rc
            zz = z_ref[pl.ds(r0, tb), :] + b_ref[...]
            dz = rc * _sigmoid(-zz)
            dz_ref[pl.ds(r0, tb), :] = dz.astype(BF16)
            return rc[0:1, :], db + jnp.sum(dz, axis=0, keepdims=True)

        zero = jnp.zeros((1, LANES), F32)
        _, db = lax.fori_loop(0, nb, block, (zero, zero))
        db_ref[...] = db

    return pl.pallas_call(
        body, name="fgate_bwd",
        out_shape=[jax.ShapeDtypeStruct((S, LANES), BF16),
                   jax.ShapeDtypeStruct((1, LANES), F32)],
        compiler_params=pltpu.CompilerParams(vmem_limit_bytes=VMEM_LIMIT),
    )(dqx, dkx, z, bias)


def _pool_bwd(proj, da, pool_w, pool_scale):
    S = proj.shape[0]
    G = len(POOL_WINDOWS)

    def body(u_ref, dy_ref, w_ref, s_ref, du_ref, dw_ref, ds_ref, pad_ref):
        g = pl.program_id(0)
        for gi, w in enumerate(POOL_WINDOWS):
            @pl.when(g == gi)
            def _():
                d, cnt = _window_mean_minus_self(u_ref[...], pad_ref, w, S)
                db = d.astype(BF16)
                wb = w_ref[0].astype(BF16)
                yraw = jnp.dot(db, wb, preferred_element_type=F32)
                dy = dy_ref[...]
                ds_ref[...] = jnp.sum(dy * yraw, axis=0, keepdims=True)
                dzb = (dy * s_ref[...]).astype(BF16)
                dw_ref[0] = lax.dot_general(db, dzb, TN_DIMS, preferred_element_type=F32)
                dd = lax.dot_general(dzb, wb, NT, preferred_element_type=F32)
                pad_ref[0:S, :] = dd / cnt
                pad_ref[S:S + MAX_WINDOW, :] = jnp.zeros((MAX_WINDOW, LANES), F32)
                acc = -dd
                for j in range(w):
                    acc = acc + pad_ref[j:j + S, :]
                du_ref[...] = acc.astype(BF16)

    return pl.pallas_call(
        body, name="pool_bwd", grid=(G,),
        in_specs=[pl.BlockSpec((S, LANES), lambda g: (0, g)),
                  pl.BlockSpec((S, LANES), lambda g: (0, g)),
                  pl.BlockSpec((1, LANES, LANES), lambda g: (g, 0, 0)),
                  pl.BlockSpec((1, LANES), lambda g: (0, g))],
        out_specs=[pl.BlockSpec((S, LANES), lambda g: (0, g)),
                   pl.BlockSpec((1, LANES, LANES), lambda g: (g, 0, 0)),
                   pl.BlockSpec((1, LANES), lambda g: (0, g))],
        out_shape=[jax.ShapeDtypeStruct((S, G * LANES), BF16),
                   jax.ShapeDtypeStruct((G, LANES, LANES), F32),
                   jax.ShapeDtypeStruct((1, G * LANES), F32)],
        scratch_shapes=[pltpu.VMEM((S + MAX_WINDOW, LANES), F32)],
        compiler_params=_params("arbitrary"),
    )(proj, da, pool_w, pool_scale)


def _inproj_bwd_dx(dproj, dzf, wmain, wf, x, gam, g, after=()):
    S, D = x.shape
    N = wmain.shape[1]
    tm = min(TM // 2, S)

    def body(dp_ref, dz_ref, w_ref, wf_ref, x_ref, gam_ref, g_ref, *rest):
        dx_ref, dg_ref = rest[-2:]

        @pl.when(pl.program_id(0) == 0)
        def _():
            dg_ref[...] = jnp.zeros(dg_ref.shape, F32)

        dh = lax.dot_general(dp_ref[...], w_ref[...], NT, preferred_element_type=F32)
        dh = dh + lax.dot_general(dz_ref[...], wf_ref[...], NT, preferred_element_type=F32)
        xf = x_ref[...]
        r = lax.rsqrt(jnp.mean(xf * xf, axis=-1, keepdims=True) + RMS_EPS)
        xhat = xf * r
        dg_ref[...] += jnp.sum(dh * xhat, axis=0, keepdims=True)
        dxhat = dh * gam_ref[...]
        dx_ref[...] = g_ref[...] + r * (dxhat - xhat * jnp.mean(dxhat * xhat, axis=-1, keepdims=True))

    return pl.pallas_call(
        body, name="inproj_bwd_dx", grid=(S // tm,),
        in_specs=[pl.BlockSpec((tm, N), lambda i: (i, 0)),
                  pl.BlockSpec((tm, LANES), lambda i: (i, 0)),
                  pl.BlockSpec((D, N), lambda i: (0, 0)),
                  pl.BlockSpec((D, LANES), lambda i: (0, 0)),
                  pl.BlockSpec((tm, D), lambda i: (i, 0)),
                  pl.BlockSpec((1, D), lambda i: (0, 0)),
                  pl.BlockSpec((tm, D), lambda i: (i, 0))] + _after_specs(after),
        out_specs=[pl.BlockSpec((tm, D), lambda i: (i, 0)),
                   pl.BlockSpec((1, D), lambda i: (0, 0))],
        out_shape=[jax.ShapeDtypeStruct((S, D), F32),
                   jax.ShapeDtypeStruct((1, D), F32)],
        compiler_params=_params("arbitrary"),
    )(dproj, dzf, wmain, wf, x, gam, g, *after)


def _adamw(w, m, v, gsets, name, rows):
    A, R, C = w.shape
    tr = min(rows, R)
    c1 = 1.0 / (1.0 - ADAM_B1 ** ADAM_STEP)
    c2 = 1.0 / (1.0 - ADAM_B2 ** ADAM_STEP)
    counts = [len(gs) for gs in gsets]

    def body(w_ref, m_ref, v_ref, *rest):
        g_ref, d_ref, nm_ref, nv_ref = rest[-4:]
        at = 0
        for a in range(A):
            part_refs = rest[at:at + counts[a]]
            at += counts[a]

            @pl.when(pl.program_id(0) == a)
            def _():
                g = None
                for ref in part_refs:
                    for s in range(ref.shape[0]):
                        term = ref[s].astype(F32)
                        g = term if g is None else g + term
                nm = ADAM_B1 * m_ref[0] + (1.0 - ADAM_B1) * g
                nv = ADAM_B2 * v_ref[0] + (1.0 - ADAM_B2) * (g * g)
                g_ref[0] = g
                nm_ref[0] = nm
                nv_ref[0] = nv
                d_ref[0] = -ADAM_LR * ((nm * c1) / (jnp.sqrt(nv * c2) + ADAM_EPS) + ADAM_WD * w_ref[0])

    spec = pl.BlockSpec((1, tr, C), lambda a, r: (a, r, 0))
    part_specs = [pl.BlockSpec((part.shape[0], tr, C), lambda a, r, l=l: (0, jnp.where(a == l, r, 0), 0))
                  for l, gs in enumerate(gsets) for part in gs]
    shape = jax.ShapeDtypeStruct((A, R, C), F32)
    return pl.pallas_call(
        body, name=name, grid=(A, R // tr),
        in_specs=[spec, spec, spec] + part_specs,
        out_specs=[spec, spec, spec, spec],
        out_shape=[shape, shape, shape, shape],
        compiler_params=_params("arbitrary", "arbitrary"),
    )(w, m, v, *[part for gs in gsets for part in gs])


def _position():
    return lax.axis_index("x"), lax.axis_index("y"), lax.axis_index("c")


def _index(dev):
    return 4 * dev[0] + 2 * dev[1] + dev[2]


def _all_gather(arrs, slots, out_shapes, name):
    n_arr = len(arrs)

    def body(*refs):
        ins, outs = refs[:n_arr], refs[n_arr:2 * n_arr]
        send_sems, recv_sems, local_sems = refs[2 * n_arr:]
        x, y, c = _position()
        me, sibling = (x, y, c), (x, y, 1 - c)
        chips = [(1 - x, y), (x, 1 - y), (1 - x, 1 - y)]

        def copy(a, k, block, to, src=None):
            part = slots[a](outs[a], _index(block))
            return pltpu.make_async_remote_copy(
                src_ref=part if src is None else src, dst_ref=part,
                send_sem=send_sems.at[a, k], recv_sem=recv_sems.at[a, k],
                device_id=to, device_id_type=MESH)

        mine = [pltpu.make_async_copy(ins[a], slots[a](outs[a], _index(me)), local_sems.at[a])
                for a in range(n_arr)]
        for cp in mine:
            cp.start()
        first = []
        for a in range(n_arr):
            first.append(copy(a, 0, me, sibling, src=ins[a]))
            first += [copy(a, 1 + j, me, (*chip, c), src=ins[a]) for j, chip in enumerate(chips)]
        for cp in first:
            cp.start()
        passed = []
        for j, chip in enumerate(chips):
            for a in range(n_arr):
                copy(a, 1 + j, (*chip, c), me).wait_recv()
                fwd = copy(a, 4 + j, (*chip, c), sibling)
                fwd.start()
                passed.append(fwd)
        for a in range(n_arr):
            copy(a, 0, sibling, me).wait_recv()
            for j, chip in enumerate(chips):
                copy(a, 4 + j, (*chip, 1 - c), me).wait_recv()
        for cp in first + passed:
            cp.wait_send()
        for cp in mine:
            cp.wait()

    any_spec = pl.BlockSpec(memory_space=pl.ANY)
    return pl.pallas_call(
        body, name=name,
        in_specs=[any_spec] * n_arr, out_specs=[any_spec] * n_arr, out_shape=out_shapes,
        scratch_shapes=[pltpu.SemaphoreType.DMA((n_arr, 7)), pltpu.SemaphoreType.DMA((n_arr, 7)),
                        pltpu.SemaphoreType.DMA((n_arr,))],
    )(*arrs)


def _split_copies(srcs, lands, send_sems, recv_sems, kinds):
    x, y, c = _position()
    me = _index((x, y, c))
    copies = []
    for a, (src_part, land_part) in enumerate(kinds):
        for k in range(1, N_DEV):
            peer = (x ^ ((k >> 2) & 1), y ^ ((k >> 1) & 1), c ^ (k & 1))
            copies.append(pltpu.make_async_remote_copy(
                src_ref=src_part(srcs[a], _index(peer)), dst_ref=land_part(lands[a], me, k),
                send_sem=send_sems[a].at[k - 1], recv_sem=recv_sems[a].at[k - 1],
                device_id=peer, device_id_type=MESH))
    return copies


def _split_start(srcs, lands, kinds, name, after=()):
    n = len(srcs)

    def body(*refs):
        src_refs, land_refs = refs[:n], refs[n:2 * n]
        outs = refs[2 * n + len(after):]
        send_sems, recv_sems = outs[:n], outs[n:2 * n]
        token = outs[-1]
        for cp in _split_copies(src_refs, land_refs, send_sems, recv_sems, kinds):
            cp.start()
        token[...] = jnp.zeros(token.shape, token.dtype)

    hbm = pl.BlockSpec(memory_space=pltpu.HBM)
    sem = pl.BlockSpec(memory_space=pltpu.SEMAPHORE)
    operands = [pltpu.with_memory_space_constraint(t, pltpu.HBM) for t in (*srcs, *lands)]
    out = pl.pallas_call(
        body, name=name,
        in_specs=[hbm] * (2 * n) + _after_specs(after),
        out_specs=[sem] * (2 * n) + [hbm] * (2 * n) + [pl.BlockSpec(memory_space=pltpu.VMEM)],
        out_shape=[pltpu.SemaphoreType.DMA((N_DEV - 1,))] * (2 * n)
        + [pltpu.HBM(t.shape, t.dtype) for t in operands] + [jax.ShapeDtypeStruct((8, LANES), F32)],
        input_output_aliases={i: 2 * n + i for i in range(2 * n)},
        compiler_params=pltpu.CompilerParams(has_side_effects=pltpu.SideEffectType.DATAFLOW_SIDE_EFFECTING),
    )(*operands, *after)
    return [(out[a], out[n + a], out[2 * n + a], out[3 * n + a]) for a in range(n)], out[-1]


def _split_wait(started, kinds, after, name):
    n = len(started)
    sems = [t[0] for t in started] + [t[1] for t in started]
    srcs = [t[2] for t in started]
    lands = [t[3] for t in started]

    def body(*refs):
        src_refs, land_refs = refs[:n], refs[n:2 * n]
        send_sems, recv_sems = refs[2 * n:3 * n], refs[3 * n:4 * n]
        for cp in _split_copies(src_refs, land_refs, send_sems, recv_sems, kinds):
            cp.wait_send()
            cp.wait_recv()

    hbm = pl.BlockSpec(memory_space=pltpu.HBM)
    sem = pl.BlockSpec(memory_space=pltpu.SEMAPHORE)
    out = pl.pallas_call(
        body, name=name,
        in_specs=[hbm] * (2 * n) + [sem] * (2 * n) + _after_specs(after),
        out_specs=[hbm] * (2 * n),
        out_shape=[pltpu.HBM(t.shape, t.dtype) for t in (*srcs, *lands)],
        input_output_aliases={i: i for i in range(2 * n)},
        compiler_params=pltpu.CompilerParams(has_side_effects=pltpu.SideEffectType.DATAFLOW_SIDE_EFFECTING),
    )(*srcs, *lands, *sems, *after)
    return out[n:]


def _pack(parts, rows):
    flat = jnp.concatenate([p.reshape(-1) for p in parts])
    return jnp.pad(flat, (0, rows * LANES - flat.shape[0])).reshape(1, rows, LANES)


def _unpack(packed, like):
    flat = packed.reshape(-1)
    out, off = [], 0
    for p in like:
        out.append(flat[off:off + p.size].reshape(p.shape))
        off += p.size
    return out


def _local_step(x, target, norm_g, forget_bias, pool_w, pool_scale, final_g, weights_in, weights_out, on_grads,
                first_after=()):
    L = norm_g.shape[0]
    S, D = x.shape
    W = D // 2
    H = W // HEAD_DIM
    bias = jnp.pad(forget_bias, ((0, 0), (0, LANES - H)))

    saved = []
    after = tuple(first_after)
    for l in range(L):
        wmain, wf = weights_in(l, x)
        proj, h, z = _inproj_fwd(x, norm_g[l:l + 1], wmain, wf, after)
        after = ()
        qaug, kaug = _fgate_fwd(z, bias[l:l + 1], H)
        ypool = _pool_fwd(proj, pool_w[l], pool_scale[l:l + 1])
        o, lse = _attn_fwd(proj, qaug, kaug)
        wout = weights_out(l, o)
        x_new, mixed = _outproj_fwd(ypool, o, proj, x, wout)
        saved.append((x, proj, h, z, qaug, kaug, ypool, o, lse, mixed, wmain, wf, wout))
        x = x_new

    g, loss, d_final_g = _loss_head(x, final_g.reshape(1, D), target)

    d_norm_g, d_bias, d_pool_w, d_pool_scale = [], [], [], []
    for l in reversed(range(L)):
        x_in, proj, h, z, qaug, kaug, ypool, o, lse, mixed, wmain, wf, wout = saved[l]
        d_wout = _matmul_tn(mixed, g, "outproj_bwd_dw")
        da, dgate = _outproj_bwd_dx(g, wout, ypool, o, proj)
        qaug_b, doaug = _attn_bwd_prep(da, o, lse, qaug)
        dq, dk, dv, dqx, dkx = _attn_bwd(proj, da, qaug_b, kaug, doaug)
        dzf, db = _fgate_bwd(dqx, dkx, z, bias[l:l + 1], H)
        dpu, dpw, dps = _pool_bwd(proj, da, pool_w[l], pool_scale[l:l + 1])
        dproj = jnp.concatenate([dpu, dgate[:, :W], dq, dk, dv, dgate[:, W:]], axis=1)
        d_wmain = _matmul_tn(h, dproj, "inproj_bwd_dw")
        d_wf = _matmul_tn(h, dzf, "inproj_bwd_dwf")
        after = tuple(on_grads(l, d_wmain, d_wf[:, :H], d_wout))
        g, dgam = _inproj_bwd_dx(dproj, dzf, wmain, wf, x_in, norm_g[l:l + 1], g, after)
        d_norm_g.append(dgam[0])
        d_bias.append(db[0, :H])
        d_pool_w.append(dpw)
        d_pool_scale.append(dps[0])

    stack = lambda parts: jnp.stack(parts[::-1])
    grads = dict(norm_g=stack(d_norm_g), forget_bias=stack(d_bias), pool_w=stack(d_pool_w),
                 pool_scale=stack(d_pool_scale), final_g=d_final_g[0])
    return loss[0, 0], g, grads


def kernel(x, norm_g, w_in, forget_bias, pool_w, pool_scale, w_out, final_g, loss_target, m_norm_g, m_w_in, m_forget_bias, m_pool_w, m_pool_scale, m_w_out, m_final_g, v_norm_g, v_w_in, v_forget_bias, v_pool_w, v_pool_scale, v_w_out, v_final_g):
    L, D, cols = w_in.shape
    rows_out = w_out.shape[1]
    W = D // 2
    H = W // HEAD_DIM
    me = _index(_position())

    win_b, wout_b = w_in.astype(BF16), w_out.astype(BF16)
    gather_in = (lambda ref, peer: ref, lambda ref, mine, k: ref.at[mine])
    gather_out = (lambda ref, peer: ref, lambda ref, mine, k: ref.at[pl.ds(mine * rows_out, rows_out), :])

    (first_in,) = _all_gather([win_b[0]], [lambda ref, n: ref.at[n]],
                              [jax.ShapeDtypeStruct((N_DEV, D, cols), BF16)], "gather_first")
    rest_srcs = [wout_b[0]] + [w[l] for l in range(1, L) for w in (win_b, wout_b)]
    rest_lands = [jnp.tile(wout_b[0], (N_DEV, 1))]
    for l in range(1, L):
        rest_lands += [jnp.broadcast_to(win_b[l][None], (N_DEV, D, cols)), jnp.tile(wout_b[l], (N_DEV, 1))]
    rest_kinds = [gather_out] + [gather_in, gather_out] * (L - 1)
    rest, rest_token = _split_start(rest_srcs, rest_lands, rest_kinds, "gather_start_rest", (first_in,))

    def weights_in(l, x_in):
        if l == 0:
            win_all = first_in
        else:
            (win_all,) = _split_wait([rest[2 * l - 1]], [gather_in], (x_in,), f"gather_wait_in_{l}")
        w_full = jnp.transpose(win_all, (1, 0, 2)).reshape(D, N_DEV * cols)
        return w_full[:, :6 * W], jnp.pad(w_full[:, 6 * W:], ((0, 0), (0, LANES - H)))

    def weights_out(l, o):
        (wout_full,) = _split_wait([rest[2 * l]], [gather_out], (o,), f"gather_wait_out_{l}")
        return wout_full

    exchange_kinds = [(lambda ref, peer: ref.at[peer], lambda ref, mine, k: ref.at[k - 1]),
                      (lambda ref, peer: ref.at[pl.ds(peer * rows_out, rows_out), :],
                       lambda ref, mine, k: ref.at[k - 1])]
    exchanges, own_parts = {}, {}

    def on_grads(l, d_wmain, d_wf, d_wout):
        dw_in = jnp.concatenate([d_wmain, d_wf], axis=1)
        own_parts[l] = (lax.dynamic_slice_in_dim(dw_in, me * cols, cols, 1)[None],
                        lax.dynamic_slice_in_dim(d_wout, me * rows_out, rows_out, 0)[None])
        srcs = [jnp.transpose(dw_in.astype(BF16).reshape(D, N_DEV, cols), (1, 0, 2)), d_wout.astype(BF16)]
        lands = [lax.empty((N_DEV - 1, D, cols), BF16), lax.empty((N_DEV - 1, rows_out, D), BF16)]
        exchanges[l], token = _split_start(srcs, lands, exchange_kinds, f"exchange_start_{l}")
        return (token,)

    loss, dx, grads = _local_step(x[0], loss_target[0], norm_g, forget_bias, pool_w, pool_scale, final_g,
                                  weights_in, weights_out, on_grads, (rest_token,))
    loss = lax.psum(loss, ("x", "y", "c"))

    names = ["norm_g", "forget_bias", "pool_w", "pool_scale", "final_g"]
    small_w = [norm_g, forget_bias, pool_w, pool_scale, final_g]
    small_m = [m_norm_g, m_forget_bias, m_pool_w, m_pool_scale, m_final_g]
    small_v = [v_norm_g, v_forget_bias, v_pool_w, v_pool_scale, v_final_g]
    total = sum(p.size for p in small_w)
    rows = -(-total // (8 * LANES)) * 8
    (gs_all,) = _all_gather(
        [_pack([grads[n] for n in names], rows)],
        [lambda ref, n: ref.at[n]],
        [jax.ShapeDtypeStruct((N_DEV, 1, rows, LANES), F32)], "gather_small_grads")
    packed = _adamw(_pack(small_w, rows), _pack(small_m, rows), _pack(small_v, rows),
                    [[gs_all.reshape(N_DEV, rows, LANES)]], "adamw_small", rows)
    g_s, d_s, nm_s, nv_s = [_unpack(p, small_w) for p in packed]

    gin_sets, gout_sets = [], []
    for l in range(L):
        got_in, got_out = _split_wait(exchanges[l], exchange_kinds, (dx, packed[1]), f"exchange_wait_{l}")
        gin_sets.append([own_parts[l][0], got_in])
        gout_sets.append([own_parts[l][1], got_out])
    g_w_in, d_w_in, nm_w_in, nv_w_in = _adamw(w_in, m_w_in, v_w_in, gin_sets, "adamw_w_in", 256)
    g_w_out, d_w_out, nm_w_out, nv_w_out = _adamw(w_out, m_w_out, v_w_out, gout_sets, "adamw_w_out", 128)

    def order(big_in, big_out, small):
        return (small[0], big_in, small[1], small[2], small[3], big_out, small[4])

    return (loss, dx[None], *order(g_w_in, g_w_out, g_s), *order(d_w_in, d_w_out, d_s),
            *order(nm_w_in, nm_w_out, nm_s), *order(nv_w_in, nv_w_out, nv_s))
```

```python
import math

import numpy as np

import jax
import jax.numpy as jnp
from jax import lax
from jax.experimental import pallas as pl
from jax.experimental.pallas import tpu as pltpu

F32 = jnp.float32
BF16 = jnp.bfloat16
MESH = pl.DeviceIdType.MESH

RMS_EPS = 1e-6
NEG_INF = -1e30
HEAD_DIM = 64
POOL_WINDOWS = (2, 4, 8, 16)
MAX_WINDOW = 16
LANES = 128
N_DEV = 8

ADAM_LR = 0.001
ADAM_B1 = 0.9
ADAM_B2 = 0.999
ADAM_EPS = 1e-08
ADAM_WD = 0.01
ADAM_STEP = 10

TM = 512
TN = 512
TQ = 512
TB = 256
VMEM_LIMIT = 56 * 1024 * 1024

NT = (((1,), (1,)), ((), ()))
TN_DIMS = (((0,), (0,)), ((), ()))

SLOT_C, SLOT_ONE, SLOT_LSE = 0, 3, 6


def _params(*sem):
    return pltpu.CompilerParams(dimension_semantics=sem, vmem_limit_bytes=VMEM_LIMIT)


def _sigmoid(x):
    return 1.0 / (1.0 + jnp.exp(-x))


def _split3(x):
    hi = x.astype(BF16).astype(F32)
    rest = x - hi
    mid = rest.astype(BF16).astype(F32)
    return hi, mid, rest - mid


def _after_specs(after):
    return [pl.BlockSpec(memory_space=pl.ANY)] * len(after)


def _inproj_fwd(x, gam, wmain, wf, after=()):
    S, D = x.shape
    N = wmain.shape[1]
    tm, tn = min(TM, S), TN

    def body(x_ref, g_ref, w_ref, wf_ref, *rest):
        proj_ref, h_ref, z_ref = rest[-3:]
        xf = x_ref[...]
        r = lax.rsqrt(jnp.mean(xf * xf, axis=-1, keepdims=True) + RMS_EPS)
        h = ((xf * r) * g_ref[...]).astype(BF16)
        h_ref[...] = h
        z_ref[...] = jnp.dot(h, wf_ref[...], preferred_element_type=F32)
        for n in range(N // tn):
            cols = slice(n * tn, (n + 1) * tn)
            proj_ref[:, cols] = jnp.dot(h, w_ref[:, cols], preferred_element_type=F32).astype(BF16)

    return pl.pallas_call(
        body, name="inproj_fwd", grid=(S // tm,),
        in_specs=[pl.BlockSpec((tm, D), lambda i: (i, 0)),
                  pl.BlockSpec((1, D), lambda i: (0, 0)),
                  pl.BlockSpec((D, N), lambda i: (0, 0)),
                  pl.BlockSpec((D, LANES), lambda i: (0, 0))] + _after_specs(after),
        out_specs=[pl.BlockSpec((tm, N), lambda i: (i, 0)),
                   pl.BlockSpec((tm, D), lambda i: (i, 0)),
                   pl.BlockSpec((tm, LANES), lambda i: (i, 0))],
        out_shape=[jax.ShapeDtypeStruct((S, N), BF16),
                   jax.ShapeDtypeStruct((S, D), BF16),
                   jax.ShapeDtypeStruct((S, LANES), F32)],
        compiler_params=_params("parallel"),
    )(x, gam, wmain, wf, *after)


def _fgate_fwd(z, bias, n_heads):
    S = z.shape[0]
    tb = min(TB, S)
    P = n_heads // 2

    def body(z_ref, b_ref, qaug_ref, kaug_ref):
        lane = lax.broadcasted_iota(jnp.int32, (tb, LANES), 1)
        tri = (lax.broadcasted_iota(jnp.int32, (tb, tb), 0)
               >= lax.broadcasted_iota(jnp.int32, (tb, tb), 1)).astype(F32)
        head = lax.broadcasted_iota(jnp.int32, (LANES, P * LANES), 0)
        col = lax.broadcasted_iota(jnp.int32, (LANES, P * LANES), 1)
        home = (head >> 1) * LANES + jnp.where((head & 1) == 0, HEAD_DIM, 0)
        is_head = head < n_heads
        place_q = [jnp.logical_and(is_head, col == home + SLOT_C + n).astype(BF16) for n in range(3)]
        place_k = [jnp.logical_and(is_head, col == home + SLOT_ONE + n).astype(BF16) for n in range(3)]
        slot = lax.broadcasted_iota(jnp.int32, (tb, P * LANES), 1) & (HEAD_DIM - 1)
        q_ones = jnp.logical_and(slot >= SLOT_ONE, slot < SLOT_ONE + 3).astype(F32)
        k_ones = jnp.logical_or(slot < SLOT_C + 3,
                                jnp.logical_and(slot >= SLOT_LSE, slot < SLOT_LSE + 3)).astype(F32)

        def block(i, carry):
            r0 = pl.multiple_of(i * tb, tb)
            zz = z_ref[pl.ds(r0, tb), :] + b_ref[...]
            lf = jnp.minimum(zz, 0.0) - jnp.log(1.0 + jnp.exp(-jnp.abs(zz)))
            lf = jnp.where(lane < n_heads, lf, 0.0)
            c = jnp.dot(tri, lf, preferred_element_type=F32, precision=lax.Precision.HIGHEST) + carry
            qa, ka = q_ones, k_ones
            for n, part in enumerate(_split3(c)):
                qa = qa + jnp.dot(part.astype(BF16), place_q[n], preferred_element_type=F32)
                ka = ka - jnp.dot(part.astype(BF16), place_k[n], preferred_element_type=F32)
            qaug_ref[pl.ds(r0, tb), :] = qa.astype(BF16)
            kaug_ref[pl.ds(r0, tb), :] = ka.astype(BF16)
            return c[tb - 1:tb, :]

        lax.fori_loop(0, S // tb, block, jnp.zeros((1, LANES), F32))

    return pl.pallas_call(
        body, name="fgate_fwd",
        out_shape=[jax.ShapeDtypeStruct((S, P * LANES), BF16),
                   jax.ShapeDtypeStruct((S, P * LANES), BF16)],
        compiler_params=pltpu.CompilerParams(vmem_limit_bytes=VMEM_LIMIT),
    )(z, bias)


def _window_mean_minus_self(u, pad_ref, w, S):
    pad_ref[0:MAX_WINDOW, :] = jnp.zeros((MAX_WINDOW, LANES), F32)
    pad_ref[MAX_WINDOW:MAX_WINDOW + S, :] = u
    acc = u
    for j in range(1, w):
        acc = acc + pad_ref[MAX_WINDOW - j:MAX_WINDOW - j + S, :]
    t = lax.broadcasted_iota(jnp.int32, (S, LANES), 0)
    cnt = jnp.minimum(t + 1, w).astype(F32)
    return acc / cnt - u, cnt


def _pool_fwd(proj, pool_w, pool_scale):
    S = proj.shape[0]
    G = len(POOL_WINDOWS)

    def body(u_ref, w_ref, s_ref, y_ref, pad_ref):
        g = pl.program_id(0)
        for gi, w in enumerate(POOL_WINDOWS):
            @pl.when(g == gi)
            def _():
                d, _ = _window_mean_minus_self(u_ref[...].astype(F32), pad_ref, w, S)
                y = jnp.dot(d.astype(BF16), w_ref[0].astype(BF16), preferred_element_type=F32)
                y_ref[...] = (y * s_ref[...]).astype(BF16)

    return pl.pallas_call(
        body, name="pool_fwd", grid=(G,),
        in_specs=[pl.BlockSpec((S, LANES), lambda g: (0, g)),
                  pl.BlockSpec((1, LANES, LANES), lambda g: (g, 0, 0)),
                  pl.BlockSpec((1, LANES), lambda g: (0, g))],
        out_specs=pl.BlockSpec((S, LANES), lambda g: (0, g)),
        out_shape=jax.ShapeDtypeStruct((S, G * LANES), BF16),
        scratch_shapes=[pltpu.VMEM((S + MAX_WINDOW, LANES), F32)],
        compiler_params=_params("arbitrary"),
    )(proj, pool_w, pool_scale)


def _head_halves(rows):
    lane = lax.broadcasted_iota(jnp.int32, (rows, LANES), 1)
    return lane, (lane < HEAD_DIM, lane >= HEAD_DIM)


def _causal_steps(n, by_key):
    pairs = [(i, j) for j in range(n) for i in range(j, n)] if by_key else \
            [(i, j) for i in range(n) for j in range(i + 1)]
    return (np.array([p[0] for p in pairs], np.int32), np.array([p[1] for p in pairs], np.int32))


def _attn_fwd(proj, qaug, kaug):
    S = proj.shape[0]
    W = proj.shape[1] // 6
    P = W // LANES
    tq = min(TQ, S)
    nq = S // tq
    qc, kc, vc = 2 * P, 3 * P, 4 * P
    scale = 1.0 / math.sqrt(HEAD_DIM)
    i_tab, j_tab = _causal_steps(nq, by_key=False)

    def body(it_ref, jt_ref, q_ref, k_ref, v_ref, qa_ref, ka_ref, o_ref, lse_ref, qm_scr, m_scr, acc_scr):
        step = pl.program_id(1)
        i, j = it_ref[step], jt_ref[step]
        lane, halves = _head_halves(tq)
        v_ones = ((lane & (HEAD_DIM - 1)) < 3).astype(BF16)

        @pl.when(j == 0)
        def _():
            qs = (q_ref[...] * scale).astype(BF16)
            qm_scr[0] = jnp.where(halves[0], qs, qa_ref[...])
            qm_scr[1] = jnp.where(halves[1], qs, qa_ref[...])
            m_scr[...] = jnp.full(m_scr.shape, NEG_INF, F32)
            acc_scr[...] = jnp.zeros(acc_scr.shape, F32)

        def update(on_diagonal):
            k2 = k_ref[...].astype(BF16)
            v2 = v_ref[...].astype(BF16)
            if on_diagonal:
                keep = (lax.broadcasted_iota(jnp.int32, (tq, tq), 0)
                        >= lax.broadcasted_iota(jnp.int32, (tq, tq), 1))
            for a in range(2):
                ka = jnp.where(halves[a], k2, ka_ref[...])
                va = jnp.where(halves[a], v2, v_ones)
                s = lax.dot_general(qm_scr[a], ka, NT, preferred_element_type=F32)
                if on_diagonal:
                    s = jnp.where(keep, s, NEG_INF)
                m_prev = m_scr[a]
                m_new = jnp.maximum(m_prev, jnp.max(s, axis=1, keepdims=True))
                p = jnp.exp(s - jnp.tile(m_new, (1, tq // LANES)))
                acc_scr[a] = jnp.exp(m_prev - m_new) * acc_scr[a] + jnp.dot(p.astype(BF16), va,
                                                                              preferred_element_type=F32)
                m_scr[a] = m_new

        @pl.when(j < i)
        def _():
            update(False)

        @pl.when(j == i)
        def _():
            update(True)
            acc_a, acc_b = acc_scr[0], acc_scr[1]
            l_a, l_b = acc_a[:, HEAD_DIM:HEAD_DIM + 1], acc_b[:, 0:1]
            o_ref[...] = jnp.where(halves[0], acc_a / l_a, acc_b / l_b).astype(BF16)
            lse_ref[...] = jnp.where(halves[0], m_scr[1] + jnp.log(l_b), m_scr[0] + jnp.log(l_a))

    grid_spec = pltpu.PrefetchScalarGridSpec(
        num_scalar_prefetch=2, grid=(P, len(i_tab)),
        in_specs=[pl.BlockSpec((tq, LANES), lambda p, s, it, jt: (it[s], qc + p)),
                  pl.BlockSpec((tq, LANES), lambda p, s, it, jt: (jt[s], kc + p)),
                  pl.BlockSpec((tq, LANES), lambda p, s, it, jt: (jt[s], vc + p)),
                  pl.BlockSpec((tq, LANES), lambda p, s, it, jt: (it[s], p)),
                  pl.BlockSpec((tq, LANES), lambda p, s, it, jt: (jt[s], p))],
        out_specs=[pl.BlockSpec((tq, LANES), lambda p, s, it, jt: (it[s], p)),
                   pl.BlockSpec((tq, LANES), lambda p, s, it, jt: (it[s], p))],
        scratch_shapes=[pltpu.VMEM((2, tq, LANES), BF16),
                        pltpu.VMEM((2, tq, LANES), F32),
                        pltpu.VMEM((2, tq, LANES), F32)])
    return pl.pallas_call(
        body, name="attn_fwd", grid_spec=grid_spec,
        out_shape=[jax.ShapeDtypeStruct((S, W), BF16), jax.ShapeDtypeStruct((S, W), F32)],
        compiler_params=_params("parallel", "arbitrary"),
    )(jnp.asarray(i_tab), jnp.asarray(j_tab), proj, proj, proj, qaug, kaug)


def _outproj_fwd(ypool, o, proj, x, wout):
    S, D = x.shape
    W = D // 2
    tm, tn = min(TM, S), TN

    def body(y_ref, o_ref, pg_ref, ag_ref, x_ref, w_ref, xn_ref, mix_ref):
        pg, ag = pg_ref[...].astype(F32), ag_ref[...].astype(F32)
        mix_ref[:, 0:W] = (y_ref[...].astype(F32) * (pg * _sigmoid(pg))).astype(BF16)
        mix_ref[:, W:D] = (o_ref[...].astype(F32) * (ag * _sigmoid(ag))).astype(BF16)
        for n in range(D // tn):
            cols = slice(n * tn, (n + 1) * tn)
            xn_ref[:, cols] = x_ref[:, cols] + jnp.dot(mix_ref[...], w_ref[:, cols], preferred_element_type=F32)

    return pl.pallas_call(
        body, name="outproj_fwd", grid=(S // tm,),
        in_specs=[pl.BlockSpec((tm, W), lambda i: (i, 0)),
                  pl.BlockSpec((tm, W), lambda i: (i, 0)),
                  pl.BlockSpec((tm, W), lambda i: (i, 1)),
                  pl.BlockSpec((tm, W), lambda i: (i, 5)),
                  pl.BlockSpec((tm, D), lambda i: (i, 0)),
                  pl.BlockSpec((D, D), lambda i: (0, 0))],
        out_specs=[pl.BlockSpec((tm, D), lambda i: (i, 0)),
                   pl.BlockSpec((tm, D), lambda i: (i, 0))],
        out_shape=[jax.ShapeDtypeStruct((S, D), F32),
                   jax.ShapeDtypeStruct((S, D), BF16)],
        compiler_params=_params("parallel"),
    )(ypool, o, proj, proj, x, wout)


def _loss_head(x, gam, target):
    S, D = x.shape
    tm = min(TM, S)

    def body(x_ref, g_ref, t_ref, dx_ref, loss_ref, dg_ref):
        @pl.when(pl.program_id(0) == 0)
        def _():
            loss_ref[...] = jnp.zeros(loss_ref.shape, F32)
            dg_ref[...] = jnp.zeros(dg_ref.shape, F32)

        xf, gam_v = x_ref[...], g_ref[...]
        r = lax.rsqrt(jnp.mean(xf * xf, axis=-1, keepdims=True) + RMS_EPS)
        xhat = xf * r
        err = xhat * gam_v - t_ref[...]
        part = jnp.sum(jnp.sum(err * err, axis=-1, keepdims=True), axis=0, keepdims=True)
        loss_ref[...] += part * (0.5 / D)
        dy = err * (1.0 / D)
        dg_ref[...] += jnp.sum(dy * xhat, axis=0, keepdims=True)
        dxhat = dy * gam_v
        dx_ref[...] = r * (dxhat - xhat * jnp.mean(dxhat * xhat, axis=-1, keepdims=True))

    return pl.pallas_call(
        body, name="loss_head", grid=(S // tm,),
        in_specs=[pl.BlockSpec((tm, D), lambda i: (i, 0)),
                  pl.BlockSpec((1, D), lambda i: (0, 0)),
                  pl.BlockSpec((tm, D), lambda i: (i, 0))],
        out_specs=[pl.BlockSpec((tm, D), lambda i: (i, 0)),
                   pl.BlockSpec((8, LANES), lambda i: (0, 0)),
                   pl.BlockSpec((1, D), lambda i: (0, 0))],
        out_shape=[jax.ShapeDtypeStruct((S, D), F32),
                   jax.ShapeDtypeStruct((8, LANES), F32),
                   jax.ShapeDtypeStruct((1, D), F32)],
        compiler_params=_params("arbitrary"),
    )(x, gam, target)


def _outproj_bwd(g, wout, mixed, ypool, o, proj):
    S, D = g.shape
    W = D // 2
    tm = min(TM, S)

    def body(g_ref, w_ref, mix_ref, y_ref, o_ref, pg_ref, ag_ref, dw_ref, da_ref, dgate_ref):
        @pl.when(pl.program_id(0) == 0)
        def _():
            dw_ref[...] = jnp.zeros(dw_ref.shape, F32)

        gb = g_ref[...].astype(BF16)
        dw_ref[...] += lax.dot_general(mix_ref[...], gb, TN_DIMS, preferred_element_type=F32)
        for half, (val_ref, gate_ref) in enumerate(((y_ref, pg_ref), (o_ref, ag_ref))):
            cols = slice(half * W, (half + 1) * W)
            dmix = lax.dot_general(gb, w_ref[cols, :], NT, preferred_element_type=F32)
            gt = gate_ref[...].astype(F32)
            sg = _sigmoid(gt)
            da_ref[:, cols] = (dmix * (gt * sg)).astype(BF16)
            dgate_ref[:, cols] = (dmix * val_ref[...].astype(F32) * (sg * (1.0 + gt * (1.0 - sg)))).astype(BF16)

    rows = lambda width, col: pl.BlockSpec((tm, width), lambda i: (i, col))
    return pl.pallas_call(
        body, name="outproj_bwd", grid=(S // tm,),
        in_specs=[rows(D, 0), pl.BlockSpec((D, D), lambda i: (0, 0)), rows(D, 0), rows(W, 0), rows(W, 0),
                  rows(W, 1), rows(W, 5)],
        out_specs=[pl.BlockSpec((D, D), lambda i: (0, 0)), rows(D, 0), rows(D, 0)],
        out_shape=[jax.ShapeDtypeStruct((D, D), F32),
                   jax.ShapeDtypeStruct((S, D), BF16),
                   jax.ShapeDtypeStruct((S, D), BF16)],
        compiler_params=_params("arbitrary"),
    )(g, wout, mixed, ypool, o, proj, proj)


def _inproj_bwd_dw(h, dproj, dzf):
    S, D = h.shape
    N = dproj.shape[1]
    ts, tn = min(TM, S), TN

    def body(h_ref, dp_ref, dz_ref, dw_ref, dwf_ref):
        @pl.when(pl.program_id(0) == 0)
        def _():
            dw_ref[...] = jnp.zeros(dw_ref.shape, F32)
            dwf_ref[...] = jnp.zeros(dwf_ref.shape, F32)

        ht = h_ref[...].T
        dwf_ref[...] += jnp.dot(ht, dz_ref[...], preferred_element_type=F32)
        for n in range(N // tn):
            cols = slice(n * tn, (n + 1) * tn)
            dw_ref[:, cols] += jnp.dot(ht, dp_ref[:, cols], preferred_element_type=F32)

    return pl.pallas_call(
        body, name="inproj_bwd_dw", grid=(S // ts,),
        in_specs=[pl.BlockSpec((ts, D), lambda k: (k, 0)),
                  pl.BlockSpec((ts, N), lambda k: (k, 0)),
                  pl.BlockSpec((ts, LANES), lambda k: (k, 0))],
        out_specs=[pl.BlockSpec((D, N), lambda k: (0, 0)),
                   pl.BlockSpec((D, LANES), lambda k: (0, 0))],
        out_shape=[jax.ShapeDtypeStruct((D, N), F32),
                   jax.ShapeDtypeStruct((D, LANES), F32)],
        compiler_params=_params("arbitrary"),
    )(h, dproj, dzf)


def _attn_bwd_prep(da, o, lse, qaug):
    S, D = da.shape
    W = D // 2
    P = W // LANES
    tq = min(TQ, S)

    def body(do_ref, o_ref, lse_ref, qa_ref, qb_ref, da_ref):
        lane, halves = _head_halves(tq)
        slot = lane & (HEAD_DIM - 1)
        prod = do_ref[...].astype(F32) * o_ref[...].astype(F32)
        d_a = jnp.sum(jnp.where(halves[0], prod, 0.0), axis=1, keepdims=True)
        d_b = jnp.sum(jnp.where(halves[1], prod, 0.0), axis=1, keepdims=True)
        aug = jnp.zeros((tq, LANES), F32)
        for n, part in enumerate(_split3(jnp.where(halves[0], d_b, d_a))):
            aug = jnp.where(slot == SLOT_C + n, -part, aug)
        da_ref[...] = aug.astype(BF16)
        aug = qa_ref[...].astype(F32)
        for n, part in enumerate(_split3(lse_ref[...])):
            aug = jnp.where(slot == SLOT_LSE + n, -part, aug)
        qb_ref[...] = aug.astype(BF16)

    spec = pl.BlockSpec((tq, LANES), lambda p, i: (i, p))
    return pl.pallas_call(
        body, name="attn_bwd_prep", grid=(P, S // tq),
        in_specs=[pl.BlockSpec((tq, LANES), lambda p, i: (i, P + p)), spec, spec, spec],
        out_specs=[spec, spec],
        out_shape=[jax.ShapeDtypeStruct((S, W), BF16), jax.ShapeDtypeStruct((S, W), BF16)],
        compiler_params=_params("parallel", "parallel"),
    )(da, o, lse, qaug)


def _attn_bwd(proj, da, qaug, kaug, doaug):
    S = proj.shape[0]
    W = proj.shape[1] // 6
    P = W // LANES
    tq = min(TQ, S)
    nq = S // tq
    qc, kc, vc = 2 * P, 3 * P, 4 * P
    scale = 1.0 / math.sqrt(HEAD_DIM)
    i_tab, j_tab = _causal_steps(nq, by_key=True)
    n_steps = len(i_tab)

    def body(it_ref, jt_ref, q_ref, k_ref, v_ref, do_ref, qa_ref, ka_ref, da_ref,
             dq_ref, dk_ref, dv_ref, dqx_ref, dkx_ref, km_scr, vm_scr, dk_scr, dv_scr, dq_scr):
        step = pl.program_id(1)
        i, j = it_ref[step], jt_ref[step]
        lane, halves = _head_halves(tq)

        @pl.when(step == 0)
        def _():
            dq_scr[...] = jnp.zeros(dq_scr.shape, F32)

        def update(on_diagonal):
            qs = (q_ref[...] * scale).astype(BF16)
            do2 = do_ref[...].astype(BF16)
            r0 = pl.multiple_of(i * tq, tq)
            if on_diagonal:
                keep = (lax.broadcasted_iota(jnp.int32, (tq, tq), 0)
                        >= lax.broadcasted_iota(jnp.int32, (tq, tq), 1))
            for a in range(2):
                qa = jnp.where(halves[a], qs, qa_ref[...])
                doa = jnp.where(halves[a], do2, da_ref[...])
                do0 = jnp.where(halves[a], do2, jnp.zeros_like(do2))
                s = lax.dot_general(qa, km_scr[a], NT, preferred_element_type=F32)
                if on_diagonal:
                    s = jnp.where(keep, s, NEG_INF)
                p = jnp.exp(s)
                ds = p * lax.dot_general(doa, vm_scr[a], NT, preferred_element_type=F32)
                dsb = ds.astype(BF16)
                dv_scr[...] += lax.dot_general(p.astype(BF16), do0, TN_DIMS, preferred_element_type=F32)
                dk_scr[a] += lax.dot_general(dsb, qa, TN_DIMS, preferred_element_type=F32)
                dq_scr[a, pl.ds(r0, tq), :] += jnp.dot(dsb, km_scr[a], preferred_element_type=F32)

        @pl.when(i == j)
        def _():
            k2 = k_ref[...].astype(BF16)
            v2 = v_ref[...].astype(BF16)
            v_ones = ((lane & (HEAD_DIM - 1)) < 3).astype(BF16)
            for a in range(2):
                km_scr[a] = jnp.where(halves[a], k2, ka_ref[...])
                vm_scr[a] = jnp.where(halves[a], v2, v_ones)
            dk_scr[...] = jnp.zeros(dk_scr.shape, F32)
            dv_scr[...] = jnp.zeros(dv_scr.shape, F32)
            update(True)

        @pl.when(i > j)
        def _():
            update(False)

        @pl.when(i == nq - 1)
        def _():
            dk_ref[...] = jnp.where(halves[0], dk_scr[0], dk_scr[1]).astype(BF16)
            dkx_ref[...] = jnp.where(halves[0], dk_scr[1], dk_scr[0])
            dv_ref[...] = dv_scr[...].astype(BF16)

        @pl.when(step == n_steps - 1)
        def _():
            row_lane, row_halves = _head_halves(S)
            dq_ref[...] = (jnp.where(row_halves[0], dq_scr[0], dq_scr[1]) * scale).astype(BF16)
            dqx_ref[...] = jnp.where(row_halves[0], dq_scr[1], dq_scr[0])

    def q_tile(col):
        return pl.BlockSpec((tq, LANES), lambda p, s, it, jt: (it[s], col + p))

    def k_tile(col):
        return pl.BlockSpec((tq, LANES), lambda p, s, it, jt: (jt[s], col + p))

    whole = pl.BlockSpec((S, LANES), lambda p, s, it, jt: (0, p))
    grid_spec = pltpu.PrefetchScalarGridSpec(
        num_scalar_prefetch=2, grid=(P, n_steps),
        in_specs=[q_tile(qc), k_tile(kc), k_tile(vc), q_tile(P), q_tile(0), k_tile(0), q_tile(0)],
        out_specs=[whole, k_tile(0), k_tile(0), whole, k_tile(0)],
        scratch_shapes=[pltpu.VMEM((2, tq, LANES), BF16),
                        pltpu.VMEM((2, tq, LANES), BF16),
                        pltpu.VMEM((2, tq, LANES), F32),
                        pltpu.VMEM((tq, LANES), F32),
                        pltpu.VMEM((2, S, LANES), F32)])
    return pl.pallas_call(
        body, name="attn_bwd", grid_spec=grid_spec,
        out_shape=[jax.ShapeDtypeStruct((S, W), BF16),
                   jax.ShapeDtypeStruct((S, W), BF16),
                   jax.ShapeDtypeStruct((S, W), BF16),
                   jax.ShapeDtypeStruct((S, W), F32),
                   jax.ShapeDtypeStruct((S, W), F32)],
        compiler_params=_params("parallel", "arbitrary"),
    )(jnp.asarray(i_tab), jnp.asarray(j_tab), proj, proj, proj, da, qaug, kaug, doaug)


def _fgate_bwd(dqx, dkx, z, bias, n_heads):
    S = z.shape[0]
    tb = min(TB, S)
    nb = S // tb

    def body(dqx_ref, dkx_ref, z_ref, b_ref, dz_ref, db_ref):
        tri = (lax.broadcasted_iota(jnp.int32, (tb, tb), 1)
               >= lax.broadcasted_iota(jnp.int32, (tb, tb), 0)).astype(F32)
        n_cols = dqx_ref.shape[1]
        col = lax.broadcasted_iota(jnp.int32, (n_cols, LANES), 0)
        head = lax.broadcasted_iota(jnp.int32, (n_cols, LANES), 1)
        home = (head >> 1) * LANES + jnp.where((head & 1) == 0, HEAD_DIM, 0)
        is_head = head < n_heads
        pick_rows = jnp.logical_and(is_head, col == home + SLOT_C).astype(F32)
        pick_cols = jnp.logical_and(is_head, col == home + SLOT_ONE).astype(F32)

        def block(n, carry):
            carry_rc, db = carry
            r0 = pl.multiple_of((nb - 1 - n) * tb, tb)
            dc = (jnp.dot(dqx_ref[pl.ds(r0, tb), :], pick_rows, preferred_element_type=F32,
                          precision=lax.Precision.HIGHEST)
                  - jnp.dot(dkx_ref[pl.ds(r0, tb), :], pick_cols, preferred_element_type=F32,
                            precision=lax.Precision.HIGHEST))
            rc = jnp.dot(tri, dc, preferred_element_type=F32, precision=lax.Precision.HIGHEST) + carry_rc
            zz = z_ref[pl.ds(r0, tb), :] + b_ref[...]
            dz = rc * _sigmoid(-zz)
            dz_ref[pl.ds(r0, tb), :] = dz.astype(BF16)
            return rc[0:1, :], db + jnp.sum(dz, axis=0, keepdims=True)

        zero = jnp.zeros((1, LANES), F32)
        _, db = lax.fori_loop(0, nb, block, (zero, zero))
        db_ref[...] = db

    return pl.pallas_call(
        body, name="fgate_bwd",
        out_shape=[jax.ShapeDtypeStruct((S, LANES), BF16),
                   jax.ShapeDtypeStruct((1, LANES), F32)],
        compiler_params=pltpu.CompilerParams(vmem_limit_bytes=VMEM_LIMIT),
    )(dqx, dkx, z, bias)


def _pool_bwd(proj, da, pool_w, pool_scale):
    S = proj.shape[0]
    G = len(POOL_WINDOWS)

    def body(u_ref, dy_ref, w_ref, s_ref, du_ref, dw_ref, ds_ref, pad_ref):
        g = pl.program_id(0)
        for gi, w in enumerate(POOL_WINDOWS):
            @pl.when(g == gi)
            def _():
                d, cnt = _window_mean_minus_self(u_ref[...].astype(F32), pad_ref, w, S)
                db = d.astype(BF16)
                wb = w_ref[0].astype(BF16)
                yraw = jnp.dot(db, wb, preferred_element_type=F32)
                dy = dy_ref[...].astype(F32)
                ds_ref[...] = jnp.sum(dy * yraw, axis=0, keepdims=True)
                dzb = (dy * s_ref[...]).astype(BF16)
                dw_ref[0] = lax.dot_general(db, dzb, TN_DIMS, preferred_element_type=F32)
                dd = lax.dot_general(dzb, wb, NT, preferred_element_type=F32)
                pad_ref[0:S, :] = dd / cnt
                pad_ref[S:S + MAX_WINDOW, :] = jnp.zeros((MAX_WINDOW, LANES), F32)
                acc = -dd
                for j in range(w):
                    acc = acc + pad_ref[j:j + S, :]
                du_ref[...] = acc.astype(BF16)

    return pl.pallas_call(
        body, name="pool_bwd", grid=(G,),
        in_specs=[pl.BlockSpec((S, LANES), lambda g: (0, g)),
                  pl.BlockSpec((S, LANES), lambda g: (0, g)),
                  pl.BlockSpec((1, LANES, LANES), lambda g: (g, 0, 0)),
                  pl.BlockSpec((1, LANES), lambda g: (0, g))],
        out_specs=[pl.BlockSpec((S, LANES), lambda g: (0, g)),
                   pl.BlockSpec((1, LANES, LANES), lambda g: (g, 0, 0)),
                   pl.BlockSpec((1, LANES), lambda g: (0, g))],
        out_shape=[jax.ShapeDtypeStruct((S, G * LANES), BF16),
                   jax.ShapeDtypeStruct((G, LANES, LANES), F32),
                   jax.ShapeDtypeStruct((1, G * LANES), F32)],
        scratch_shapes=[pltpu.VMEM((S + MAX_WINDOW, LANES), F32)],
        compiler_params=_params("arbitrary"),
    )(proj, da, pool_w, pool_scale)


def _inproj_bwd_dx(dproj, dzf, wmain, wf, x, gam, g, after=()):
    S, D = x.shape
    N = wmain.shape[1]
    tm = min(TM // 2, S)

    def body(dp_ref, dz_ref, w_ref, wf_ref, x_ref, gam_ref, g_ref, *rest):
        dx_ref, dg_ref = rest[-2:]

        @pl.when(pl.program_id(0) == 0)
        def _():
            dg_ref[...] = jnp.zeros(dg_ref.shape, F32)

        dh = lax.dot_general(dp_ref[...], w_ref[...], NT, preferred_element_type=F32)
        dh = dh + lax.dot_general(dz_ref[...], wf_ref[...], NT, preferred_element_type=F32)
        xf = x_ref[...]
        r = lax.rsqrt(jnp.mean(xf * xf, axis=-1, keepdims=True) + RMS_EPS)
        xhat = xf * r
        dg_ref[...] += jnp.sum(dh * xhat, axis=0, keepdims=True)
        dxhat = dh * gam_ref[...]
        dx_ref[...] = g_ref[...] + r * (dxhat - xhat * jnp.mean(dxhat * xhat, axis=-1, keepdims=True))

    return pl.pallas_call(
        body, name="inproj_bwd_dx", grid=(S // tm,),
        in_specs=[pl.BlockSpec((tm, N), lambda i: (i, 0)),
                  pl.BlockSpec((tm, LANES), lambda i: (i, 0)),
                  pl.BlockSpec((D, N), lambda i: (0, 0)),
                  pl.BlockSpec((D, LANES), lambda i: (0, 0)),
                  pl.BlockSpec((tm, D), lambda i: (i, 0)),
                  pl.BlockSpec((1, D), lambda i: (0, 0)),
                  pl.BlockSpec((tm, D), lambda i: (i, 0))] + _after_specs(after),
        out_specs=[pl.BlockSpec((tm, D), lambda i: (i, 0)),
                   pl.BlockSpec((1, D), lambda i: (0, 0))],
        out_shape=[jax.ShapeDtypeStruct((S, D), F32),
                   jax.ShapeDtypeStruct((1, D), F32)],
        compiler_params=_params("arbitrary"),
    )(dproj, dzf, wmain, wf, x, gam, g, *after)


def _adamw(w, m, v, gsets, name, rows):
    A, R, C = w.shape
    tr = min(rows, R)
    c1 = 1.0 / (1.0 - ADAM_B1 ** ADAM_STEP)
    c2 = 1.0 / (1.0 - ADAM_B2 ** ADAM_STEP)
    counts = [len(gs) for gs in gsets]

    def body(w_ref, m_ref, v_ref, *rest):
        g_ref, d_ref, nm_ref, nv_ref = rest[-4:]
        at = 0
        for a in range(A):
            part_refs = rest[at:at + counts[a]]
            at += counts[a]

            @pl.when(pl.program_id(0) == a)
            def _():
                g = None
                for ref in part_refs:
                    for s in range(ref.shape[0]):
                        term = ref[s].astype(F32)
                        g = term if g is None else g + term
                nm = ADAM_B1 * m_ref[0] + (1.0 - ADAM_B1) * g
                nv = ADAM_B2 * v_ref[0] + (1.0 - ADAM_B2) * (g * g)
                g_ref[0] = g
                nm_ref[0] = nm
                nv_ref[0] = nv
                d_ref[0] = -ADAM_LR * ((nm * c1) / (jnp.sqrt(nv * c2) + ADAM_EPS) + ADAM_WD * w_ref[0])

    spec = pl.BlockSpec((1, tr, C), lambda a, r: (a, r, 0))
    part_specs = [pl.BlockSpec((part.shape[0], tr, C), lambda a, r, l=l: (0, jnp.where(a == l, r, 0), 0))
                  for l, gs in enumerate(gsets) for part in gs]
    shape = jax.ShapeDtypeStruct((A, R, C), F32)
    return pl.pallas_call(
        body, name=name, grid=(A, R // tr),
        in_specs=[spec, spec, spec] + part_specs,
        out_specs=[spec, spec, spec, spec],
        out_shape=[shape, shape, shape, shape],
        compiler_params=_params("arbitrary", "arbitrary"),
    )(w, m, v, *[part for gs in gsets for part in gs])


def _position():
    return lax.axis_index("x"), lax.axis_index("y"), lax.axis_index("c")


def _index(dev):
    return 4 * dev[0] + 2 * dev[1] + dev[2]


def _all_gather(arrs, slots, out_shapes, name):
    n_arr = len(arrs)

    def body(*refs):
        ins, outs = refs[:n_arr], refs[n_arr:2 * n_arr]
        send_sems, recv_sems, local_sems = refs[2 * n_arr:]
        x, y, c = _position()
        me, sibling = (x, y, c), (x, y, 1 - c)
        chips = [(1 - x, y), (x, 1 - y), (1 - x, 1 - y)]

        def copy(a, k, block, to, src=None):
            part = slots[a](outs[a], _index(block))
            return pltpu.make_async_remote_copy(
                src_ref=part if src is None else src, dst_ref=part,
                send_sem=send_sems.at[a, k], recv_sem=recv_sems.at[a, k],
                device_id=to, device_id_type=MESH)

        mine = [pltpu.make_async_copy(ins[a], slots[a](outs[a], _index(me)), local_sems.at[a])
                for a in range(n_arr)]
        for cp in mine:
            cp.start()
        first = []
        for a in range(n_arr):
            first.append(copy(a, 0, me, sibling, src=ins[a]))
            first += [copy(a, 1 + j, me, (*chip, c), src=ins[a]) for j, chip in enumerate(chips)]
        for cp in first:
            cp.start()
        passed = []
        for j, chip in enumerate(chips):
            for a in range(n_arr):
                copy(a, 1 + j, (*chip, c), me).wait_recv()
                fwd = copy(a, 4 + j, (*chip, c), sibling)
                fwd.start()
                passed.append(fwd)
        for a in range(n_arr):
            copy(a, 0, sibling, me).wait_recv()
            for j, chip in enumerate(chips):
                copy(a, 4 + j, (*chip, 1 - c), me).wait_recv()
        for cp in first + passed:
            cp.wait_send()
        for cp in mine:
            cp.wait()

    any_spec = pl.BlockSpec(memory_space=pl.ANY)
    return pl.pallas_call(
        body, name=name,
        in_specs=[any_spec] * n_arr, out_specs=[any_spec] * n_arr, out_shape=out_shapes,
        scratch_shapes=[pltpu.SemaphoreType.DMA((n_arr, 7)), pltpu.SemaphoreType.DMA((n_arr, 7)),
                        pltpu.SemaphoreType.DMA((n_arr,))],
    )(*arrs)


def _split_copies(srcs, lands, send_sems, recv_sems, kinds):
    x, y, c = _position()
    me = _index((x, y, c))
    copies = []
    for a, (src_part, land_part) in enumerate(kinds):
        for k in range(1, N_DEV):
            peer = (x ^ ((k >> 2) & 1), y ^ ((k >> 1) & 1), c ^ (k & 1))
            copies.append(pltpu.make_async_remote_copy(
                src_ref=src_part(srcs[a], _index(peer)), dst_ref=land_part(lands[a], me, k),
                send_sem=send_sems[a].at[k - 1], recv_sem=recv_sems[a].at[k - 1],
                device_id=peer, device_id_type=MESH))
    return copies


def _split_start(srcs, lands, kinds, name, after=()):
    n = len(srcs)

    def body(*refs):
        src_refs, land_refs = refs[:n], refs[n:2 * n]
        outs = refs[2 * n + len(after):]
        send_sems, recv_sems = outs[:n], outs[n:2 * n]
        token = outs[-1]
        for cp in _split_copies(src_refs, land_refs, send_sems, recv_sems, kinds):
            cp.start()
        token[...] = jnp.zeros(token.shape, token.dtype)

    hbm = pl.BlockSpec(memory_space=pltpu.HBM)
    sem = pl.BlockSpec(memory_space=pltpu.SEMAPHORE)
    operands = [pltpu.with_memory_space_constraint(t, pltpu.HBM) for t in (*srcs, *lands)]
    out = pl.pallas_call(
        body, name=name,
        in_specs=[hbm] * (2 * n) + _after_specs(after),
        out_specs=[sem] * (2 * n) + [hbm] * (2 * n) + [pl.BlockSpec(memory_space=pltpu.VMEM)],
        out_shape=[pltpu.SemaphoreType.DMA((N_DEV - 1,))] * (2 * n)
        + [pltpu.HBM(t.shape, t.dtype) for t in operands] + [jax.ShapeDtypeStruct((8, LANES), F32)],
        input_output_aliases={i: 2 * n + i for i in range(2 * n)},
        compiler_params=pltpu.CompilerParams(has_side_effects=pltpu.SideEffectType.DATAFLOW_SIDE_EFFECTING),
    )(*operands, *after)
    return [(out[a], out[n + a], out[2 * n + a], out[3 * n + a]) for a in range(n)], out[-1]


def _split_wait(started, kinds, after, name):
    n = len(started)
    sems = [t[0] for t in started] + [t[1] for t in started]
    srcs = [t[2] for t in started]
    lands = [t[3] for t in started]

    def body(*refs):
        src_refs, land_refs = refs[:n], refs[n:2 * n]
        send_sems, recv_sems = refs[2 * n:3 * n], refs[3 * n:4 * n]
        for cp in _split_copies(src_refs, land_refs, send_sems, recv_sems, kinds):
            cp.wait_send()
            cp.wait_recv()

    hbm = pl.BlockSpec(memory_space=pltpu.HBM)
    sem = pl.BlockSpec(memory_space=pltpu.SEMAPHORE)
    out = pl.pallas_call(
        body, name=name,
        in_specs=[hbm] * (2 * n) + [sem] * (2 * n) + _after_specs(after),
        out_specs=[hbm] * (2 * n),
        out_shape=[pltpu.HBM(t.shape, t.dtype) for t in (*srcs, *lands)],
        input_output_aliases={i: i for i in range(2 * n)},
        compiler_params=pltpu.CompilerParams(has_side_effects=pltpu.SideEffectType.DATAFLOW_SIDE_EFFECTING),
    )(*srcs, *lands, *sems, *after)
    return out[n:]


def _pack(parts, rows):
    flat = jnp.concatenate([p.reshape(-1) for p in parts])
    return jnp.pad(flat, (0, rows * LANES - flat.shape[0])).reshape(1, rows, LANES)


def _unpack(packed, like):
    flat = packed.reshape(-1)
    out, off = [], 0
    for p in like:
        out.append(flat[off:off + p.size].reshape(p.shape))
        off += p.size
    return out


def _local_step(x, target, norm_g, forget_bias, pool_w, pool_scale, final_g, weights_in, weights_out, on_grads,
                first_after=()):
    L = norm_g.shape[0]
    S, D = x.shape
    W = D // 2
    H = W // HEAD_DIM
    bias = jnp.pad(forget_bias, ((0, 0), (0, LANES - H)))

    saved = []
    after = tuple(first_after)
    for l in range(L):
        wmain, wf = weights_in(l, x)
        proj, h, z = _inproj_fwd(x, norm_g[l:l + 1], wmain, wf, after)
        after = ()
        qaug, kaug = _fgate_fwd(z, bias[l:l + 1], H)
        ypool = _pool_fwd(proj, pool_w[l], pool_scale[l:l + 1])
        o, lse = _attn_fwd(proj, qaug, kaug)
        wout = weights_out(l, o)
        x_new, mixed = _outproj_fwd(ypool, o, proj, x, wout)
        saved.append((x, proj, h, z, qaug, kaug, ypool, o, lse, mixed, wmain, wf, wout))
        x = x_new

    g, loss, d_final_g = _loss_head(x, final_g.reshape(1, D), target)

    d_norm_g, d_bias, d_pool_w, d_pool_scale = [], [], [], []
    for l in reversed(range(L)):
        x_in, proj, h, z, qaug, kaug, ypool, o, lse, mixed, wmain, wf, wout = saved[l]
        d_wout, da, dgate = _outproj_bwd(g, wout, mixed, ypool, o, proj)
        qaug_b, doaug = _attn_bwd_prep(da, o, lse, qaug)
        dq, dk, dv, dqx, dkx = _attn_bwd(proj, da, qaug_b, kaug, doaug)
        dzf, db = _fgate_bwd(dqx, dkx, z, bias[l:l + 1], H)
        dpu, dpw, dps = _pool_bwd(proj, da, pool_w[l], pool_scale[l:l + 1])
        dproj = jnp.concatenate([dpu, dgate[:, :W], dq, dk, dv, dgate[:, W:]], axis=1)
        d_wmain, d_wf = _inproj_bwd_dw(h, dproj, dzf)
        after = tuple(on_grads(l, d_wmain, d_wf[:, :H], d_wout))
        g, dgam = _inproj_bwd_dx(dproj, dzf, wmain, wf, x_in, norm_g[l:l + 1], g, after)
        d_norm_g.append(dgam[0])
        d_bias.append(db[0, :H])
        d_pool_w.append(dpw)
        d_pool_scale.append(dps[0])

    stack = lambda parts: jnp.stack(parts[::-1])
    grads = dict(norm_g=stack(d_norm_g), forget_bias=stack(d_bias), pool_w=stack(d_pool_w),
                 pool_scale=stack(d_pool_scale), final_g=d_final_g[0])
    return loss[0, 0], g, grads


def kernel(x, norm_g, w_in, forget_bias, pool_w, pool_scale, w_out, final_g, loss_target, m_norm_g, m_w_in, m_forget_bias, m_pool_w, m_pool_scale, m_w_out, m_final_g, v_norm_g, v_w_in, v_forget_bias, v_pool_w, v_pool_scale, v_w_out, v_final_g):
    L, D, cols = w_in.shape
    rows_out = w_out.shape[1]
    W = D // 2
    H = W // HEAD_DIM
    me = _index(_position())

    win_b, wout_b = w_in.astype(BF16), w_out.astype(BF16)
    gather_in = (lambda ref, peer: ref, lambda ref, mine, k: ref.at[mine])
    gather_out = (lambda ref, peer: ref, lambda ref, mine, k: ref.at[pl.ds(mine * rows_out, rows_out), :])

    (first_in,) = _all_gather([win_b[0]], [lambda ref, n: ref.at[n]],
                              [jax.ShapeDtypeStruct((N_DEV, D, cols), BF16)], "gather_first")
    rest_srcs = [wout_b[0]] + [w[l] for l in range(1, L) for w in (win_b, wout_b)]
    rest_lands = [jnp.tile(wout_b[0], (N_DEV, 1))]
    for l in range(1, L):
        rest_lands += [jnp.broadcast_to(win_b[l][None], (N_DEV, D, cols)), jnp.tile(wout_b[l], (N_DEV, 1))]
    rest_kinds = [gather_out] + [gather_in, gather_out] * (L - 1)
    rest, rest_token = _split_start(rest_srcs, rest_lands, rest_kinds, "gather_start_rest", (first_in,))

    def weights_in(l, x_in):
        if l == 0:
            win_all = first_in
        else:
            (win_all,) = _split_wait([rest[2 * l - 1]], [gather_in], (x_in,), f"gather_wait_in_{l}")
        w_full = jnp.transpose(win_all, (1, 0, 2)).reshape(D, N_DEV * cols)
        return w_full[:, :6 * W], jnp.pad(w_full[:, 6 * W:], ((0, 0), (0, LANES - H)))

    def weights_out(l, o):
        (wout_full,) = _split_wait([rest[2 * l]], [gather_out], (o,), f"gather_wait_out_{l}")
        return wout_full

    exchange_kinds = [(lambda ref, peer: ref.at[peer], lambda ref, mine, k: ref.at[k - 1]),
                      (lambda ref, peer: ref.at[pl.ds(peer * rows_out, rows_out), :],
                       lambda ref, mine, k: ref.at[k - 1])]
    exchanges, own_parts = {}, {}

    def on_grads(l, d_wmain, d_wf, d_wout):
        dw_in = jnp.concatenate([d_wmain, d_wf], axis=1)
        own_parts[l] = (lax.dynamic_slice_in_dim(dw_in, me * cols, cols, 1)[None],
                        lax.dynamic_slice_in_dim(d_wout, me * rows_out, rows_out, 0)[None])
        srcs = [jnp.transpose(dw_in.astype(BF16).reshape(D, N_DEV, cols), (1, 0, 2)), d_wout.astype(BF16)]
        lands = [lax.empty((N_DEV - 1, D, cols), BF16), lax.empty((N_DEV - 1, rows_out, D), BF16)]
        exchanges[l], token = _split_start(srcs, lands, exchange_kinds, f"exchange_start_{l}")
        return (token,)

    loss, dx, grads = _local_step(x[0], loss_target[0], norm_g, forget_bias, pool_w, pool_scale, final_g,
                                  weights_in, weights_out, on_grads, (rest_token,))
    loss = lax.psum(loss, ("x", "y", "c"))

    names = ["norm_g", "forget_bias", "pool_w", "pool_scale", "final_g"]
    small_w = [norm_g, forget_bias, pool_w, pool_scale, final_g]
    small_m = [m_norm_g, m_forget_bias, m_pool_w, m_pool_scale, m_final_g]
    small_v = [v_norm_g, v_forget_bias, v_pool_w, v_pool_scale, v_final_g]
    total = sum(p.size for p in small_w)
    rows = -(-total // (8 * LANES)) * 8
    (gs_all,) = _all_gather(
        [_pack([grads[n] for n in names], rows)],
        [lambda ref, n: ref.at[n]],
        [jax.ShapeDtypeStruct((N_DEV, 1, rows, LANES), F32)], "gather_small_grads")
    packed = _adamw(_pack(small_w, rows), _pack(small_m, rows), _pack(small_v, rows),
                    [[gs_all.reshape(N_DEV, rows, LANES)]], "adamw_small", rows)
    g_s, d_s, nm_s, nv_s = [_unpack(p, small_w) for p in packed]

    gin_sets, gout_sets = [], []
    for l in range(L):
        got_in, got_out = _split_wait(exchanges[l], exchange_kinds, (dx, packed[1]), f"exchange_wait_{l}")
        gin_sets.append([own_parts[l][0], got_in])
        gout_sets.append([own_parts[l][1], got_out])
    g_w_in, d_w_in, nm_w_in, nv_w_in = _adamw(w_in, m_w_in, v_w_in, gin_sets, "adamw_w_in", 256)
    g_w_out, d_w_out, nm_w_out, nv_w_out = _adamw(w_out, m_w_out, v_w_out, gout_sets, "adamw_w_out", 128)

    def order(big_in, big_out, small):
        return (small[0], big_in, small[1], small[2], small[3], big_out, small[4])

    return (loss, dx[None], *order(g_w_in, g_w_out, g_s), *order(d_w_in, d_w_out, d_s),
            *order(nm_w_in, nm_w_out, nm_s), *order(nv_w_in, nv_w_out, nv_s))
```

```python
import math

import jax
import jax.numpy as jnp
from jax import lax
from jax.experimental import pallas as pl
from jax.experimental.pallas import tpu as pltpu

F32 = jnp.float32
BF16 = jnp.bfloat16
MESH = pl.DeviceIdType.MESH

RMS_EPS = 1e-6
NEG_INF = -1e30
HEAD_DIM = 64
POOL_WINDOWS = (2, 4, 8, 16)
MAX_WINDOW = 16
LANES = 128
N_DEV = 8

ADAM_LR = 0.001
ADAM_B1 = 0.9
ADAM_B2 = 0.999
ADAM_EPS = 1e-08
ADAM_WD = 0.01
ADAM_STEP = 10

TM = 512
TN = 512
TQ = 512
TB = 256
VMEM_LIMIT = 56 * 1024 * 1024

NT = (((1,), (1,)), ((), ()))
TN_DIMS = (((0,), (0,)), ((), ()))

SLOT_C, SLOT_ONE, SLOT_LSE = 0, 3, 6


def _params(*sem):
    return pltpu.CompilerParams(dimension_semantics=sem, vmem_limit_bytes=VMEM_LIMIT)


def _sigmoid(x):
    return 1.0 / (1.0 + jnp.exp(-x))


def _split3(x):
    hi = x.astype(BF16).astype(F32)
    rest = x - hi
    mid = rest.astype(BF16).astype(F32)
    return hi, mid, rest - mid


def _after_specs(after):
    return [pl.BlockSpec(memory_space=pl.ANY)] * len(after)


def _inproj_fwd(x, gam, wmain, wf, after=()):
    S, D = x.shape
    N = wmain.shape[1]
    tm, tn = min(TM, S), TN

    def body(x_ref, g_ref, w_ref, wf_ref, *rest):
        proj_ref, h_ref, z_ref = rest[-3:]
        xf = x_ref[...]
        r = lax.rsqrt(jnp.mean(xf * xf, axis=-1, keepdims=True) + RMS_EPS)
        h = ((xf * r) * g_ref[...]).astype(BF16)
        h_ref[...] = h
        z_ref[...] = jnp.dot(h, wf_ref[...], preferred_element_type=F32)
        for n in range(N // tn):
            cols = slice(n * tn, (n + 1) * tn)
            proj_ref[:, cols] = jnp.dot(h, w_ref[:, cols], preferred_element_type=F32).astype(BF16)

    return pl.pallas_call(
        body, name="inproj_fwd", grid=(S // tm,),
        in_specs=[pl.BlockSpec((tm, D), lambda i: (i, 0)),
                  pl.BlockSpec((1, D), lambda i: (0, 0)),
                  pl.BlockSpec((D, N), lambda i: (0, 0)),
                  pl.BlockSpec((D, LANES), lambda i: (0, 0))] + _after_specs(after),
        out_specs=[pl.BlockSpec((tm, N), lambda i: (i, 0)),
                   pl.BlockSpec((tm, D), lambda i: (i, 0)),
                   pl.BlockSpec((tm, LANES), lambda i: (i, 0))],
        out_shape=[jax.ShapeDtypeStruct((S, N), BF16),
                   jax.ShapeDtypeStruct((S, D), BF16),
                   jax.ShapeDtypeStruct((S, LANES), F32)],
        compiler_params=_params("parallel"),
    )(x, gam, wmain, wf, *after)


def _fgate_fwd(z, bias, n_heads):
    S = z.shape[0]
    tb = min(TB, S)
    P = n_heads // 2

    def body(z_ref, b_ref, qaug_ref, kaug_ref):
        lane = lax.broadcasted_iota(jnp.int32, (tb, LANES), 1)
        tri = (lax.broadcasted_iota(jnp.int32, (tb, tb), 0)
               >= lax.broadcasted_iota(jnp.int32, (tb, tb), 1)).astype(F32)
        head = lax.broadcasted_iota(jnp.int32, (LANES, P * LANES), 0)
        col = lax.broadcasted_iota(jnp.int32, (LANES, P * LANES), 1)
        home = (head >> 1) * LANES + jnp.where((head & 1) == 0, HEAD_DIM, 0)
        is_head = head < n_heads
        place_q = [jnp.logical_and(is_head, col == home + SLOT_C + n).astype(BF16) for n in range(3)]
        place_k = [jnp.logical_and(is_head, col == home + SLOT_ONE + n).astype(BF16) for n in range(3)]
        slot = lax.broadcasted_iota(jnp.int32, (tb, P * LANES), 1) & (HEAD_DIM - 1)
        q_ones = jnp.logical_and(slot >= SLOT_ONE, slot < SLOT_ONE + 3).astype(F32)
        k_ones = jnp.logical_or(slot < SLOT_C + 3,
                                jnp.logical_and(slot >= SLOT_LSE, slot < SLOT_LSE + 3)).astype(F32)

        local = []
        for b in range(S // tb):
            zz = z_ref[b * tb:(b + 1) * tb, :] + b_ref[...]
            lf = jnp.minimum(zz, 0.0) - jnp.log(1.0 + jnp.exp(-jnp.abs(zz)))
            lf = jnp.where(lane < n_heads, lf, 0.0)
            local.append(jnp.dot(tri, lf, preferred_element_type=F32, precision=lax.Precision.HIGHEST))
        carry = jnp.zeros((1, LANES), F32)
        for b, part_sum in enumerate(local):
            c = part_sum + carry
            carry = c[tb - 1:tb, :]
            qa, ka = q_ones, k_ones
            for n, part in enumerate(_split3(c)):
                qa = qa + jnp.dot(part.astype(BF16), place_q[n], preferred_element_type=F32)
                ka = ka - jnp.dot(part.astype(BF16), place_k[n], preferred_element_type=F32)
            qaug_ref[b * tb:(b + 1) * tb, :] = qa.astype(BF16)
            kaug_ref[b * tb:(b + 1) * tb, :] = ka.astype(BF16)

    return pl.pallas_call(
        body, name="fgate_fwd",
        out_shape=[jax.ShapeDtypeStruct((S, P * LANES), BF16),
                   jax.ShapeDtypeStruct((S, P * LANES), BF16)],
        compiler_params=pltpu.CompilerParams(vmem_limit_bytes=VMEM_LIMIT),
    )(z, bias)


def _window_mean_minus_self(u, pad_ref, w, S):
    pad_ref[0:MAX_WINDOW, :] = jnp.zeros((MAX_WINDOW, LANES), F32)
    pad_ref[MAX_WINDOW:MAX_WINDOW + S, :] = u
    acc = u
    for j in range(1, w):
        acc = acc + pad_ref[MAX_WINDOW - j:MAX_WINDOW - j + S, :]
    t = lax.broadcasted_iota(jnp.int32, (S, LANES), 0)
    cnt = jnp.minimum(t + 1, w).astype(F32)
    return acc / cnt - u, cnt


def _pool_fwd(proj, pool_w, pool_scale):
    S = proj.shape[0]
    G = len(POOL_WINDOWS)

    def body(u_ref, w_ref, s_ref, y_ref, pad_ref):
        g = pl.program_id(0)
        for gi, w in enumerate(POOL_WINDOWS):
            @pl.when(g == gi)
            def _():
                d, _ = _window_mean_minus_self(u_ref[...].astype(F32), pad_ref, w, S)
                y = jnp.dot(d.astype(BF16), w_ref[0].astype(BF16), preferred_element_type=F32)
                y_ref[...] = (y * s_ref[...]).astype(BF16)

    return pl.pallas_call(
        body, name="pool_fwd", grid=(G,),
        in_specs=[pl.BlockSpec((S, LANES), lambda g: (0, g)),
                  pl.BlockSpec((1, LANES, LANES), lambda g: (g, 0, 0)),
                  pl.BlockSpec((1, LANES), lambda g: (0, g))],
        out_specs=pl.BlockSpec((S, LANES), lambda g: (0, g)),
        out_shape=jax.ShapeDtypeStruct((S, G * LANES), BF16),
        scratch_shapes=[pltpu.VMEM((S + MAX_WINDOW, LANES), F32)],
        compiler_params=_params("arbitrary"),
    )(proj, pool_w, pool_scale)


def _head_halves(rows):
    lane = lax.broadcasted_iota(jnp.int32, (rows, LANES), 1)
    return lane, (lane < HEAD_DIM, lane >= HEAD_DIM)


def _attn_fwd(proj, qaug, kaug):
    S = proj.shape[0]
    W = proj.shape[1] // 6
    P = W // LANES
    tq = min(TQ, S)
    nq = S // tq
    qc, kc, vc = 2 * P, 3 * P, 4 * P
    scale = 1.0 / math.sqrt(HEAD_DIM)

    def body(q_ref, k_ref, v_ref, qa_ref, ka_ref, o_ref, lse_ref, qm_scr, m_scr, acc_scr):
        i = pl.program_id(1)
        lane, halves = _head_halves(tq)
        v_ones = ((lane & (HEAD_DIM - 1)) < 3).astype(BF16)
        qs = q_ref[...] * scale
        qm_scr[0] = jnp.where(halves[0], qs, qa_ref[...])
        qm_scr[1] = jnp.where(halves[1], qs, qa_ref[...])
        m_scr[...] = jnp.full(m_scr.shape, NEG_INF, F32)
        acc_scr[...] = jnp.zeros(acc_scr.shape, F32)

        def update(j, on_diagonal):
            keys = pl.ds(pl.multiple_of(j * tq, tq), tq)
            k2, v2, kaug_t = k_ref[keys, :], v_ref[keys, :], ka_ref[keys, :]
            if on_diagonal:
                keep = (lax.broadcasted_iota(jnp.int32, (tq, tq), 0)
                        >= lax.broadcasted_iota(jnp.int32, (tq, tq), 1))
            for a in range(2):
                ka = jnp.where(halves[a], k2, kaug_t)
                va = jnp.where(halves[a], v2, v_ones)
                s = lax.dot_general(qm_scr[a], ka, NT, preferred_element_type=F32)
                if on_diagonal:
                    s = jnp.where(keep, s, NEG_INF)
                m_prev = m_scr[a]
                m_new = jnp.maximum(m_prev, jnp.max(s, axis=1, keepdims=True))
                p = jnp.exp(s - jnp.tile(m_new, (1, tq // LANES)))
                acc_scr[a] = jnp.exp(m_prev - m_new) * acc_scr[a] + jnp.dot(p.astype(BF16), va,
                                                                              preferred_element_type=F32)
                m_scr[a] = m_new

        def below_diagonal(j, carry):
            update(j, False)
            return carry

        lax.fori_loop(0, i, below_diagonal, 0)
        update(i, True)
        acc_a, acc_b = acc_scr[0], acc_scr[1]
        l_a, l_b = acc_a[:, HEAD_DIM:HEAD_DIM + 1], acc_b[:, 0:1]
        o_ref[...] = jnp.where(halves[0], acc_a / l_a, acc_b / l_b).astype(BF16)
        lse_ref[...] = jnp.where(halves[0], m_scr[1] + jnp.log(l_b), m_scr[0] + jnp.log(l_a))

    tile = lambda col: pl.BlockSpec((tq, LANES), lambda p, i: (i, col + p))
    whole = lambda col: pl.BlockSpec((S, LANES), lambda p, i: (0, col + p))
    return pl.pallas_call(
        body, name="attn_fwd", grid=(P, nq),
        in_specs=[tile(qc), whole(kc), whole(vc), tile(0), whole(0)],
        out_specs=[tile(0), tile(0)],
        out_shape=[jax.ShapeDtypeStruct((S, W), BF16), jax.ShapeDtypeStruct((S, W), F32)],
        scratch_shapes=[pltpu.VMEM((2, tq, LANES), BF16),
                        pltpu.VMEM((2, tq, LANES), F32),
                        pltpu.VMEM((2, tq, LANES), F32)],
        compiler_params=_params("parallel", "arbitrary"),
    )(proj, proj, proj, qaug, kaug)


def _outproj_fwd(ypool, o, proj, x, wout):
    S, D = x.shape
    W = D // 2
    tm, tn = min(TM, S), TN

    def body(y_ref, o_ref, pg_ref, ag_ref, x_ref, w_ref, xn_ref, mix_ref):
        pg, ag = pg_ref[...].astype(F32), ag_ref[...].astype(F32)
        mix_ref[:, 0:W] = (y_ref[...].astype(F32) * (pg * _sigmoid(pg))).astype(BF16)
        mix_ref[:, W:D] = (o_ref[...].astype(F32) * (ag * _sigmoid(ag))).astype(BF16)
        for n in range(D // tn):
            cols = slice(n * tn, (n + 1) * tn)
            xn_ref[:, cols] = x_ref[:, cols] + jnp.dot(mix_ref[...], w_ref[:, cols], preferred_element_type=F32)

    return pl.pallas_call(
        body, name="outproj_fwd", grid=(S // tm,),
        in_specs=[pl.BlockSpec((tm, W), lambda i: (i, 0)),
                  pl.BlockSpec((tm, W), lambda i: (i, 0)),
                  pl.BlockSpec((tm, W), lambda i: (i, 1)),
                  pl.BlockSpec((tm, W), lambda i: (i, 5)),
                  pl.BlockSpec((tm, D), lambda i: (i, 0)),
                  pl.BlockSpec((D, D), lambda i: (0, 0))],
        out_specs=[pl.BlockSpec((tm, D), lambda i: (i, 0)),
                   pl.BlockSpec((tm, D), lambda i: (i, 0))],
        out_shape=[jax.ShapeDtypeStruct((S, D), F32),
                   jax.ShapeDtypeStruct((S, D), BF16)],
        compiler_params=_params("parallel"),
    )(ypool, o, proj, proj, x, wout)


def _loss_head(x, gam, target):
    S, D = x.shape
    tm = min(TM, S)

    def body(x_ref, g_ref, t_ref, dx_ref, loss_ref, dg_ref):
        @pl.when(pl.program_id(0) == 0)
        def _():
            loss_ref[...] = jnp.zeros(loss_ref.shape, F32)
            dg_ref[...] = jnp.zeros(dg_ref.shape, F32)

        xf, gam_v = x_ref[...], g_ref[...]
        r = lax.rsqrt(jnp.mean(xf * xf, axis=-1, keepdims=True) + RMS_EPS)
        xhat = xf * r
        err = xhat * gam_v - t_ref[...]
        part = jnp.sum(jnp.sum(err * err, axis=-1, keepdims=True), axis=0, keepdims=True)
        loss_ref[...] += part * (0.5 / D)
        dy = err * (1.0 / D)
        dg_ref[...] += jnp.sum(dy * xhat, axis=0, keepdims=True)
        dxhat = dy * gam_v
        dx_ref[...] = r * (dxhat - xhat * jnp.mean(dxhat * xhat, axis=-1, keepdims=True))

    return pl.pallas_call(
        body, name="loss_head", grid=(S // tm,),
        in_specs=[pl.BlockSpec((tm, D), lambda i: (i, 0)),
                  pl.BlockSpec((1, D), lambda i: (0, 0)),
                  pl.BlockSpec((tm, D), lambda i: (i, 0))],
        out_specs=[pl.BlockSpec((tm, D), lambda i: (i, 0)),
                   pl.BlockSpec((8, LANES), lambda i: (0, 0)),
                   pl.BlockSpec((1, D), lambda i: (0, 0))],
        out_shape=[jax.ShapeDtypeStruct((S, D), F32),
                   jax.ShapeDtypeStruct((8, LANES), F32),
                   jax.ShapeDtypeStruct((1, D), F32)],
        compiler_params=_params("arbitrary"),
    )(x, gam, target)


def _outproj_bwd(g, wout, mixed, ypool, o, proj):
    S, D = g.shape
    W = D // 2
    tm = min(TM, S)

    def body(g_ref, w_ref, mix_ref, y_ref, o_ref, pg_ref, ag_ref, dw_ref, da_ref, dgate_ref):
        @pl.when(pl.program_id(0) == 0)
        def _():
            dw_ref[...] = jnp.zeros(dw_ref.shape, F32)

        gb = g_ref[...].astype(BF16)
        dw_ref[...] += lax.dot_general(mix_ref[...], gb, TN_DIMS, preferred_element_type=F32)
        for half, (val_ref, gate_ref) in enumerate(((y_ref, pg_ref), (o_ref, ag_ref))):
            cols = slice(half * W, (half + 1) * W)
            dmix = lax.dot_general(gb, w_ref[cols, :], NT, preferred_element_type=F32)
            gt = gate_ref[...].astype(F32)
            sg = _sigmoid(gt)
            da_ref[:, cols] = (dmix * (gt * sg)).astype(BF16)
            dgate_ref[:, cols] = (dmix * val_ref[...].astype(F32) * (sg * (1.0 + gt * (1.0 - sg)))).astype(BF16)

    rows = lambda width, col: pl.BlockSpec((tm, width), lambda i: (i, col))
    return pl.pallas_call(
        body, name="outproj_bwd", grid=(S // tm,),
        in_specs=[rows(D, 0), pl.BlockSpec((D, D), lambda i: (0, 0)), rows(D, 0), rows(W, 0), rows(W, 0),
                  rows(W, 1), rows(W, 5)],
        out_specs=[pl.BlockSpec((D, D), lambda i: (0, 0)), rows(D, 0), rows(D, 0)],
        out_shape=[jax.ShapeDtypeStruct((D, D), F32),
                   jax.ShapeDtypeStruct((S, D), BF16),
                   jax.ShapeDtypeStruct((S, D), BF16)],
        compiler_params=_params("arbitrary"),
    )(g, wout, mixed, ypool, o, proj, proj)


def _inproj_bwd_dw(h, dproj, dzf):
    S, D = h.shape
    N = dproj.shape[1]
    ts, tn = min(TM, S), TN

    def body(h_ref, dp_ref, dz_ref, dw_ref, dwf_ref):
        @pl.when(pl.program_id(0) == 0)
        def _():
            dw_ref[...] = jnp.zeros(dw_ref.shape, F32)
            dwf_ref[...] = jnp.zeros(dwf_ref.shape, F32)

        ht = h_ref[...].T
        dwf_ref[...] += jnp.dot(ht, dz_ref[...], preferred_element_type=F32)
        for n in range(N // tn):
            cols = slice(n * tn, (n + 1) * tn)
            dw_ref[:, cols] += jnp.dot(ht, dp_ref[:, cols], preferred_element_type=F32)

    return pl.pallas_call(
        body, name="inproj_bwd_dw", grid=(S // ts,),
        in_specs=[pl.BlockSpec((ts, D), lambda k: (k, 0)),
                  pl.BlockSpec((ts, N), lambda k: (k, 0)),
                  pl.BlockSpec((ts, LANES), lambda k: (k, 0))],
        out_specs=[pl.BlockSpec((D, N), lambda k: (0, 0)),
                   pl.BlockSpec((D, LANES), lambda k: (0, 0))],
        out_shape=[jax.ShapeDtypeStruct((D, N), F32),
                   jax.ShapeDtypeStruct((D, LANES), F32)],
        compiler_params=_params("arbitrary"),
    )(h, dproj, dzf)


def _attn_bwd_prep(da, o, lse, qaug):
    S, D = da.shape
    W = D // 2
    P = W // LANES
    tq = min(TQ, S)

    def body(do_ref, o_ref, lse_ref, qa_ref, qb_ref, da_ref):
        lane, halves = _head_halves(tq)
        slot = lane & (HEAD_DIM - 1)
        prod = do_ref[...].astype(F32) * o_ref[...].astype(F32)
        d_a = jnp.sum(jnp.where(halves[0], prod, 0.0), axis=1, keepdims=True)
        d_b = jnp.sum(jnp.where(halves[1], prod, 0.0), axis=1, keepdims=True)
        aug = jnp.zeros((tq, LANES), F32)
        for n, part in enumerate(_split3(jnp.where(halves[0], d_b, d_a))):
            aug = jnp.where(slot == SLOT_C + n, -part, aug)
        da_ref[...] = aug.astype(BF16)
        aug = qa_ref[...].astype(F32)
        for n, part in enumerate(_split3(lse_ref[...])):
            aug = jnp.where(slot == SLOT_LSE + n, -part, aug)
        qb_ref[...] = aug.astype(BF16)

    spec = pl.BlockSpec((tq, LANES), lambda p, i: (i, p))
    return pl.pallas_call(
        body, name="attn_bwd_prep", grid=(P, S // tq),
        in_specs=[pl.BlockSpec((tq, LANES), lambda p, i: (i, P + p)), spec, spec, spec],
        out_specs=[spec, spec],
        out_shape=[jax.ShapeDtypeStruct((S, W), BF16), jax.ShapeDtypeStruct((S, W), BF16)],
        compiler_params=_params("parallel", "parallel"),
    )(da, o, lse, qaug)


def _attn_bwd(proj, da, qaug, kaug, doaug):
    S = proj.shape[0]
    W = proj.shape[1] // 6
    P = W // LANES
    tq = min(TQ, S)
    nq = S // tq
    qc, kc, vc = 2 * P, 3 * P, 4 * P
    scale = 1.0 / math.sqrt(HEAD_DIM)

    def body(q_ref, k_ref, v_ref, do_ref, qa_ref, ka_ref, da_ref,
             dq_ref, dk_ref, dv_ref, dqx_ref, dkx_ref, km_scr, vm_scr, dk_scr, dv_scr, dq_scr):
        j = pl.program_id(1)
        lane, halves = _head_halves(tq)

        @pl.when(j == 0)
        def _():
            dq_scr[...] = jnp.zeros(dq_scr.shape, F32)

        v_ones = ((lane & (HEAD_DIM - 1)) < 3).astype(BF16)
        for a in range(2):
            km_scr[a] = jnp.where(halves[a], k_ref[...], ka_ref[...])
            vm_scr[a] = jnp.where(halves[a], v_ref[...], v_ones)
        dk_scr[...] = jnp.zeros(dk_scr.shape, F32)
        dv_scr[...] = jnp.zeros(dv_scr.shape, F32)

        def update(i, on_diagonal):
            rows = pl.ds(pl.multiple_of(i * tq, tq), tq)
            qs = q_ref[rows, :] * scale
            do2, qaug_t, doaug_t = do_ref[rows, :], qa_ref[rows, :], da_ref[rows, :]
            if on_diagonal:
                keep = (lax.broadcasted_iota(jnp.int32, (tq, tq), 0)
                        >= lax.broadcasted_iota(jnp.int32, (tq, tq), 1))
            for a in range(2):
                qa = jnp.where(halves[a], qs, qaug_t)
                doa = jnp.where(halves[a], do2, doaug_t)
                do0 = jnp.where(halves[a], do2, jnp.zeros_like(do2))
                s = lax.dot_general(qa, km_scr[a], NT, preferred_element_type=F32)
                if on_diagonal:
                    s = jnp.where(keep, s, NEG_INF)
                p = jnp.exp(s)
                ds = p * lax.dot_general(doa, vm_scr[a], NT, preferred_element_type=F32)
                dsb = ds.astype(BF16)
                dv_scr[...] += lax.dot_general(p.astype(BF16), do0, TN_DIMS, preferred_element_type=F32)
                dk_scr[a] += lax.dot_general(dsb, qa, TN_DIMS, preferred_element_type=F32)
                dq_scr[a, rows, :] += jnp.dot(dsb, km_scr[a], preferred_element_type=F32)

        def below_diagonal(i, carry):
            update(i, False)
            return carry

        update(j, True)
        lax.fori_loop(j + 1, nq, below_diagonal, 0)
        dk_ref[...] = jnp.where(halves[0], dk_scr[0], dk_scr[1]).astype(BF16)
        dkx_ref[...] = jnp.where(halves[0], dk_scr[1], dk_scr[0])
        dv_ref[...] = dv_scr[...].astype(BF16)

        @pl.when(j == nq - 1)
        def _():
            row_lane, row_halves = _head_halves(S)
            dq_ref[...] = (jnp.where(row_halves[0], dq_scr[0], dq_scr[1]) * scale).astype(BF16)
            dqx_ref[...] = jnp.where(row_halves[0], dq_scr[1], dq_scr[0])

    tile = lambda col: pl.BlockSpec((tq, LANES), lambda p, j: (j, col + p))
    whole = lambda col: pl.BlockSpec((S, LANES), lambda p, j: (0, col + p))
    return pl.pallas_call(
        body, name="attn_bwd", grid=(P, nq),
        in_specs=[whole(qc), tile(kc), tile(vc), whole(P), whole(0), tile(0), whole(0)],
        out_specs=[whole(0), tile(0), tile(0), whole(0), tile(0)],
        out_shape=[jax.ShapeDtypeStruct((S, W), BF16),
                   jax.ShapeDtypeStruct((S, W), BF16),
                   jax.ShapeDtypeStruct((S, W), BF16),
                   jax.ShapeDtypeStruct((S, W), F32),
                   jax.ShapeDtypeStruct((S, W), F32)],
        scratch_shapes=[pltpu.VMEM((2, tq, LANES), BF16),
                        pltpu.VMEM((2, tq, LANES), BF16),
                        pltpu.VMEM((2, tq, LANES), F32),
                        pltpu.VMEM((tq, LANES), F32),
                        pltpu.VMEM((2, S, LANES), F32)],
        compiler_params=_params("parallel", "arbitrary"),
    )(proj, proj, proj, da, qaug, kaug, doaug)


def _fgate_bwd(dqx, dkx, z, bias, n_heads):
    S = z.shape[0]
    tb = min(TB, S)
    nb = S // tb

    def body(dqx_ref, dkx_ref, z_ref, b_ref, dz_ref, db_ref):
        tri = (lax.broadcasted_iota(jnp.int32, (tb, tb), 1)
               >= lax.broadcasted_iota(jnp.int32, (tb, tb), 0)).astype(F32)
        n_cols = dqx_ref.shape[1]
        col = lax.broadcasted_iota(jnp.int32, (n_cols, LANES), 0)
        head = lax.broadcasted_iota(jnp.int32, (n_cols, LANES), 1)
        home = (head >> 1) * LANES + jnp.where((head & 1) == 0, HEAD_DIM, 0)
        is_head = head < n_heads
        pick_rows = jnp.logical_and(is_head, col == home + SLOT_C).astype(F32)
        pick_cols = jnp.logical_and(is_head, col == home + SLOT_ONE).astype(F32)

        local = []
        for b in range(nb):
            rows = slice(b * tb, (b + 1) * tb)
            dc = (jnp.dot(dqx_ref[rows, :], pick_rows, preferred_element_type=F32, precision=lax.Precision.HIGH)
                  - jnp.dot(dkx_ref[rows, :], pick_cols, preferred_element_type=F32, precision=lax.Precision.HIGH))
            local.append(jnp.dot(tri, dc, preferred_element_type=F32, precision=lax.Precision.HIGHEST))
        carry = jnp.zeros((1, LANES), F32)
        db = jnp.zeros((1, LANES), F32)
        for b in reversed(range(nb)):
            rows = slice(b * tb, (b + 1) * tb)
            rc = local[b] + carry
            carry = rc[0:1, :]
            dz = rc * _sigmoid(-(z_ref[rows, :] + b_ref[...]))
            dz_ref[rows, :] = dz.astype(BF16)
            db = db + jnp.sum(dz, axis=0, keepdims=True)
        db_ref[...] = db

    return pl.pallas_call(
        body, name="fgate_bwd",
        out_shape=[jax.ShapeDtypeStruct((S, LANES), BF16),
                   jax.ShapeDtypeStruct((1, LANES), F32)],
        compiler_params=pltpu.CompilerParams(vmem_limit_bytes=VMEM_LIMIT),
    )(dqx, dkx, z, bias)


def _pool_bwd(proj, da, pool_w, pool_scale):
    S = proj.shape[0]
    G = len(POOL_WINDOWS)

    def body(u_ref, dy_ref, w_ref, s_ref, du_ref, dw_ref, ds_ref, pad_ref):
        g = pl.program_id(0)
        for gi, w in enumerate(POOL_WINDOWS):
            @pl.when(g == gi)
            def _():
                d, cnt = _window_mean_minus_self(u_ref[...].astype(F32), pad_ref, w, S)
                db = d.astype(BF16)
                wb = w_ref[0].astype(BF16)
                yraw = jnp.dot(db, wb, preferred_element_type=F32)
                dy = dy_ref[...].astype(F32)
                ds_ref[...] = jnp.sum(dy * yraw, axis=0, keepdims=True)
                dzb = (dy * s_ref[...]).astype(BF16)
                dw_ref[0] = lax.dot_general(db, dzb, TN_DIMS, preferred_element_type=F32)
                dd = lax.dot_general(dzb, wb, NT, preferred_element_type=F32)
                pad_ref[0:S, :] = dd / cnt
                pad_ref[S:S + MAX_WINDOW, :] = jnp.zeros((MAX_WINDOW, LANES), F32)
                acc = -dd
                for j in range(w):
                    acc = acc + pad_ref[j:j + S, :]
                du_ref[...] = acc.astype(BF16)

    return pl.pallas_call(
        body, name="pool_bwd", grid=(G,),
        in_specs=[pl.BlockSpec((S, LANES), lambda g: (0, g)),
                  pl.BlockSpec((S, LANES), lambda g: (0, g)),
                  pl.BlockSpec((1, LANES, LANES), lambda g: (g, 0, 0)),
                  pl.BlockSpec((1, LANES), lambda g: (0, g))],
        out_specs=[pl.BlockSpec((S, LANES), lambda g: (0, g)),
                   pl.BlockSpec((1, LANES, LANES), lambda g: (g, 0, 0)),
                   pl.BlockSpec((1, LANES), lambda g: (0, g))],
        out_shape=[jax.ShapeDtypeStruct((S, G * LANES), BF16),
                   jax.ShapeDtypeStruct((G, LANES, LANES), F32),
                   jax.ShapeDtypeStruct((1, G * LANES), F32)],
        scratch_shapes=[pltpu.VMEM((S + MAX_WINDOW, LANES), F32)],
        compiler_params=_params("arbitrary"),
    )(proj, da, pool_w, pool_scale)


def _inproj_bwd_dx(dproj, dzf, wmain, wf, x, gam, g, after=()):
    S, D = x.shape
    N = wmain.shape[1]
    tm = min(TM // 2, S)

    def body(dp_ref, dz_ref, w_ref, wf_ref, x_ref, gam_ref, g_ref, *rest):
        dx_ref, dg_ref = rest[-2:]

        @pl.when(pl.program_id(0) == 0)
        def _():
            dg_ref[...] = jnp.zeros(dg_ref.shape, F32)

        dh = lax.dot_general(dp_ref[...], w_ref[...], NT, preferred_element_type=F32)
        dh = dh + lax.dot_general(dz_ref[...], wf_ref[...], NT, preferred_element_type=F32)
        xf = x_ref[...]
        r = lax.rsqrt(jnp.mean(xf * xf, axis=-1, keepdims=True) + RMS_EPS)
        xhat = xf * r
        dg_ref[...] += jnp.sum(dh * xhat, axis=0, keepdims=True)
        dxhat = dh * gam_ref[...]
        dx_ref[...] = g_ref[...] + r * (dxhat - xhat * jnp.mean(dxhat * xhat, axis=-1, keepdims=True))

    return pl.pallas_call(
        body, name="inproj_bwd_dx", grid=(S // tm,),
        in_specs=[pl.BlockSpec((tm, N), lambda i: (i, 0)),
                  pl.BlockSpec((tm, LANES), lambda i: (i, 0)),
                  pl.BlockSpec((D, N), lambda i: (0, 0)),
                  pl.BlockSpec((D, LANES), lambda i: (0, 0)),
                  pl.BlockSpec((tm, D), lambda i: (i, 0)),
                  pl.BlockSpec((1, D), lambda i: (0, 0)),
                  pl.BlockSpec((tm, D), lambda i: (i, 0))] + _after_specs(after),
        out_specs=[pl.BlockSpec((tm, D), lambda i: (i, 0)),
                   pl.BlockSpec((1, D), lambda i: (0, 0))],
        out_shape=[jax.ShapeDtypeStruct((S, D), F32),
                   jax.ShapeDtypeStruct((1, D), F32)],
        compiler_params=_params("arbitrary"),
    )(dproj, dzf, wmain, wf, x, gam, g, *after)


def _adamw(w, m, v, gsets, name, rows):
    A, R, C = w.shape
    tr = min(rows, R)
    c1 = 1.0 / (1.0 - ADAM_B1 ** ADAM_STEP)
    c2 = 1.0 / (1.0 - ADAM_B2 ** ADAM_STEP)
    counts = [len(gs) for gs in gsets]

    def body(w_ref, m_ref, v_ref, *rest):
        g_ref, d_ref, nm_ref, nv_ref = rest[-4:]
        at = 0
        for a in range(A):
            part_refs = rest[at:at + counts[a]]
            at += counts[a]

            @pl.when(pl.program_id(0) == a)
            def _():
                g = None
                for ref in part_refs:
                    for s in range(ref.shape[0]):
                        term = ref[s].astype(F32)
                        g = term if g is None else g + term
                nm = ADAM_B1 * m_ref[0] + (1.0 - ADAM_B1) * g
                nv = ADAM_B2 * v_ref[0] + (1.0 - ADAM_B2) * (g * g)
                g_ref[0] = g
                nm_ref[0] = nm
                nv_ref[0] = nv
                d_ref[0] = -ADAM_LR * ((nm * c1) / (jnp.sqrt(nv * c2) + ADAM_EPS) + ADAM_WD * w_ref[0])

    spec = pl.BlockSpec((1, tr, C), lambda a, r: (a, r, 0))
    part_specs = [pl.BlockSpec((part.shape[0], tr, C), lambda a, r, l=l: (0, jnp.where(a == l, r, 0), 0))
                  for l, gs in enumerate(gsets) for part in gs]
    shape = jax.ShapeDtypeStruct((A, R, C), F32)
    return pl.pallas_call(
        body, name=name, grid=(A, R // tr),
        in_specs=[spec, spec, spec] + part_specs,
        out_specs=[spec, spec, spec, spec],
        out_shape=[shape, shape, shape, shape],
        compiler_params=_params("arbitrary", "arbitrary"),
    )(w, m, v, *[part for gs in gsets for part in gs])


def _position():
    return lax.axis_index("x"), lax.axis_index("y"), lax.axis_index("c")


def _index(dev):
    return 4 * dev[0] + 2 * dev[1] + dev[2]


def _all_gather(arrs, slots, out_shapes, name):
    n_arr = len(arrs)

    def body(*refs):
        ins, outs = refs[:n_arr], refs[n_arr:2 * n_arr]
        send_sems, recv_sems, local_sems = refs[2 * n_arr:]
        x, y, c = _position()
        me, sibling = (x, y, c), (x, y, 1 - c)
        chips = [(1 - x, y), (x, 1 - y), (1 - x, 1 - y)]

        def copy(a, k, block, to, src=None):
            part = slots[a](outs[a], _index(block))
            return pltpu.make_async_remote_copy(
                src_ref=part if src is None else src, dst_ref=part,
                send_sem=send_sems.at[a, k], recv_sem=recv_sems.at[a, k],
                device_id=to, device_id_type=MESH)

        mine = [pltpu.make_async_copy(ins[a], slots[a](outs[a], _index(me)), local_sems.at[a])
                for a in range(n_arr)]
        for cp in mine:
            cp.start()
        first = []
        for a in range(n_arr):
            first.append(copy(a, 0, me, sibling, src=ins[a]))
            first += [copy(a, 1 + j, me, (*chip, c), src=ins[a]) for j, chip in enumerate(chips)]
        for cp in first:
            cp.start()
        passed = []
        for j, chip in enumerate(chips):
            for a in range(n_arr):
                copy(a, 1 + j, (*chip, c), me).wait_recv()
                fwd = copy(a, 4 + j, (*chip, c), sibling)
                fwd.start()
                passed.append(fwd)
        for a in range(n_arr):
            copy(a, 0, sibling, me).wait_recv()
            for j, chip in enumerate(chips):
                copy(a, 4 + j, (*chip, 1 - c), me).wait_recv()
        for cp in first + passed:
            cp.wait_send()
        for cp in mine:
            cp.wait()

    any_spec = pl.BlockSpec(memory_space=pl.ANY)
    return pl.pallas_call(
        body, name=name,
        in_specs=[any_spec] * n_arr, out_specs=[any_spec] * n_arr, out_shape=out_shapes,
        scratch_shapes=[pltpu.SemaphoreType.DMA((n_arr, 7)), pltpu.SemaphoreType.DMA((n_arr, 7)),
                        pltpu.SemaphoreType.DMA((n_arr,))],
    )(*arrs)


def _split_copies(srcs, lands, send_sems, recv_sems, kinds):
    x, y, c = _position()
    me = _index((x, y, c))
    copies = []
    for a, (src_part, land_part) in enumerate(kinds):
        for k in range(1, N_DEV):
            peer = (x ^ ((k >> 2) & 1), y ^ ((k >> 1) & 1), c ^ (k & 1))
            copies.append(pltpu.make_async_remote_copy(
                src_ref=src_part(srcs[a], _index(peer)), dst_ref=land_part(lands[a], me, k),
                send_sem=send_sems[a].at[k - 1], recv_sem=recv_sems[a].at[k - 1],
                device_id=peer, device_id_type=MESH))
    return copies


def _split_start(srcs, lands, kinds, name, after=()):
    n = len(srcs)

    def body(*refs):
        src_refs, land_refs = refs[:n], refs[n:2 * n]
        outs = refs[2 * n + len(after):]
        send_sems, recv_sems = outs[:n], outs[n:2 * n]
        token = outs[-1]
        for cp in _split_copies(src_refs, land_refs, send_sems, recv_sems, kinds):
            cp.start()
        token[...] = jnp.zeros(token.shape, token.dtype)

    hbm = pl.BlockSpec(memory_space=pltpu.HBM)
    sem = pl.BlockSpec(memory_space=pltpu.SEMAPHORE)
    operands = [pltpu.with_memory_space_constraint(t, pltpu.HBM) for t in (*srcs, *lands)]
    out = pl.pallas_call(
        body, name=name,
        in_specs=[hbm] * (2 * n) + _after_specs(after),
        out_specs=[sem] * (2 * n) + [hbm] * (2 * n) + [pl.BlockSpec(memory_space=pltpu.VMEM)],
        out_shape=[pltpu.SemaphoreType.DMA((N_DEV - 1,))] * (2 * n)
        + [pltpu.HBM(t.shape, t.dtype) for t in operands] + [jax.ShapeDtypeStruct((8, LANES), F32)],
        input_output_aliases={i: 2 * n + i for i in range(2 * n)},
        compiler_params=pltpu.CompilerParams(has_side_effects=pltpu.SideEffectType.DATAFLOW_SIDE_EFFECTING),
    )(*operands, *after)
    return [(out[a], out[n + a], out[2 * n + a], out[3 * n + a]) for a in range(n)], out[-1]


def _split_wait(started, kinds, after, name):
    n = len(started)
    sems = [t[0] for t in started] + [t[1] for t in started]
    srcs = [t[2] for t in started]
    lands = [t[3] for t in started]

    def body(*refs):
        src_refs, land_refs = refs[:n], refs[n:2 * n]
        send_sems, recv_sems = refs[2 * n:3 * n], refs[3 * n:4 * n]
        for cp in _split_copies(src_refs, land_refs, send_sems, recv_sems, kinds):
            cp.wait_send()
            cp.wait_recv()

    hbm = pl.BlockSpec(memory_space=pltpu.HBM)
    sem = pl.BlockSpec(memory_space=pltpu.SEMAPHORE)
    out = pl.pallas_call(
        body, name=name,
        in_specs=[hbm] * (2 * n) + [sem] * (2 * n) + _after_specs(after),
        out_specs=[hbm] * (2 * n),
        out_shape=[pltpu.HBM(t.shape, t.dtype) for t in (*srcs, *lands)],
        input_output_aliases={i: i for i in range(2 * n)},
        compiler_params=pltpu.CompilerParams(has_side_effects=pltpu.SideEffectType.DATAFLOW_SIDE_EFFECTING),
    )(*srcs, *lands, *sems, *after)
    return out[n:]


def _pack(parts, rows):
    flat = jnp.concatenate([p.reshape(-1) for p in parts])
    return jnp.pad(flat, (0, rows * LANES - flat.shape[0])).reshape(1, rows, LANES)


def _unpack(packed, like):
    flat = packed.reshape(-1)
    out, off = [], 0
    for p in like:
        out.append(flat[off:off + p.size].reshape(p.shape))
        off += p.size
    return out


def _local_step(x, target, norm_g, forget_bias, pool_w, pool_scale, final_g, weights_in, weights_out, on_grads,
                first_after=()):
    L = norm_g.shape[0]
    S, D = x.shape
    W = D // 2
    H = W // HEAD_DIM
    bias = jnp.pad(forget_bias, ((0, 0), (0, LANES - H)))

    saved = []
    after = tuple(first_after)
    for l in range(L):
        wmain, wf = weights_in(l, x)
        proj, h, z = _inproj_fwd(x, norm_g[l:l + 1], wmain, wf, after)
        after = ()
        qaug, kaug = _fgate_fwd(z, bias[l:l + 1], H)
        ypool = _pool_fwd(proj, pool_w[l], pool_scale[l:l + 1])
        o, lse = _attn_fwd(proj, qaug, kaug)
        wout = weights_out(l, o)
        x_new, mixed = _outproj_fwd(ypool, o, proj, x, wout)
        saved.append((x, proj, h, z, qaug, kaug, ypool, o, lse, mixed, wmain, wf, wout))
        x = x_new

    g, loss, d_final_g = _loss_head(x, final_g.reshape(1, D), target)

    d_norm_g, d_bias, d_pool_w, d_pool_scale = [], [], [], []
    for l in reversed(range(L)):
        x_in, proj, h, z, qaug, kaug, ypool, o, lse, mixed, wmain, wf, wout = saved[l]
        d_wout, da, dgate = _outproj_bwd(g, wout, mixed, ypool, o, proj)
        qaug_b, doaug = _attn_bwd_prep(da, o, lse, qaug)
        dq, dk, dv, dqx, dkx = _attn_bwd(proj, da, qaug_b, kaug, doaug)
        dzf, db = _fgate_bwd(dqx, dkx, z, bias[l:l + 1], H)
        dpu, dpw, dps = _pool_bwd(proj, da, pool_w[l], pool_scale[l:l + 1])
        dproj = jnp.concatenate([dpu, dgate[:, :W], dq, dk, dv, dgate[:, W:]], axis=1)
        d_wmain, d_wf = _inproj_bwd_dw(h, dproj, dzf)
        after = tuple(on_grads(l, d_wmain, d_wf[:, :H], d_wout))
        g, dgam = _inproj_bwd_dx(dproj, dzf, wmain, wf, x_in, norm_g[l:l + 1], g, after)
        d_norm_g.append(dgam[0])
        d_bias.append(db[0, :H])
        d_pool_w.append(dpw)
        d_pool_scale.append(dps[0])

    stack = lambda parts: jnp.stack(parts[::-1])
    grads = dict(norm_g=stack(d_norm_g), forget_bias=stack(d_bias), pool_w=stack(d_pool_w),
                 pool_scale=stack(d_pool_scale), final_g=d_final_g[0])
    return loss[0, 0], g, grads


def kernel(x, norm_g, w_in, forget_bias, pool_w, pool_scale, w_out, final_g, loss_target, m_norm_g, m_w_in, m_forget_bias, m_pool_w, m_pool_scale, m_w_out, m_final_g, v_norm_g, v_w_in, v_forget_bias, v_pool_w, v_pool_scale, v_w_out, v_final_g):
    L, D, cols = w_in.shape
    rows_out = w_out.shape[1]
    W = D // 2
    H = W // HEAD_DIM
    me = _index(_position())

    win_b, wout_b = w_in.astype(BF16), w_out.astype(BF16)
    gather_in = (lambda ref, peer: ref, lambda ref, mine, k: ref.at[mine])
    gather_out = (lambda ref, peer: ref, lambda ref, mine, k: ref.at[pl.ds(mine * rows_out, rows_out), :])

    (first_in,) = _all_gather([win_b[0]], [lambda ref, n: ref.at[n]],
                              [jax.ShapeDtypeStruct((N_DEV, D, cols), BF16)], "gather_first")
    rest_srcs = [wout_b[0]] + [w[l] for l in range(1, L) for w in (win_b, wout_b)]
    rest_lands = [jnp.tile(wout_b[0], (N_DEV, 1))]
    for l in range(1, L):
        rest_lands += [jnp.broadcast_to(win_b[l][None], (N_DEV, D, cols)), jnp.tile(wout_b[l], (N_DEV, 1))]
    rest_kinds = [gather_out] + [gather_in, gather_out] * (L - 1)
    rest, rest_token = _split_start(rest_srcs, rest_lands, rest_kinds, "gather_start_rest", (first_in,))

    def weights_in(l, x_in):
        if l == 0:
            win_all = first_in
        else:
            (win_all,) = _split_wait([rest[2 * l - 1]], [gather_in], (x_in,), f"gather_wait_in_{l}")
        w_full = jnp.transpose(win_all, (1, 0, 2)).reshape(D, N_DEV * cols)
        return w_full[:, :6 * W], jnp.pad(w_full[:, 6 * W:], ((0, 0), (0, LANES - H)))

    def weights_out(l, o):
        (wout_full,) = _split_wait([rest[2 * l]], [gather_out], (o,), f"gather_wait_out_{l}")
        return wout_full

    exchange_kinds = [(lambda ref, peer: ref.at[peer], lambda ref, mine, k: ref.at[k - 1]),
                      (lambda ref, peer: ref.at[pl.ds(peer * rows_out, rows_out), :],
                       lambda ref, mine, k: ref.at[k - 1])]
    exchanges, own_parts = {}, {}

    def on_grads(l, d_wmain, d_wf, d_wout):
        dw_in = jnp.concatenate([d_wmain, d_wf], axis=1)
        dw_in = jnp.transpose(dw_in.reshape(D, N_DEV, cols), (1, 0, 2))
        own_parts[l] = (lax.dynamic_index_in_dim(dw_in, me, 0, keepdims=True),
                        lax.dynamic_slice_in_dim(d_wout, me * rows_out, rows_out, 0)[None])
        srcs = [dw_in.astype(BF16), d_wout.astype(BF16)]
        lands = [lax.empty((N_DEV - 1, D, cols), BF16), lax.empty((N_DEV - 1, rows_out, D), BF16)]
        exchanges[l], token = _split_start(srcs, lands, exchange_kinds, f"exchange_start_{l}")
        return (token,)

    loss, dx, grads = _local_step(x[0], loss_target[0], norm_g, forget_bias, pool_w, pool_scale, final_g,
                                  weights_in, weights_out, on_grads, (rest_token,))
    loss = lax.psum(loss, ("x", "y", "c"))

    names = ["norm_g", "forget_bias", "pool_w", "pool_scale", "final_g"]
    small_w = [norm_g, forget_bias, pool_w, pool_scale, final_g]
    small_m = [m_norm_g, m_forget_bias, m_pool_w, m_pool_scale, m_final_g]
    small_v = [v_norm_g, v_forget_bias, v_pool_w, v_pool_scale, v_final_g]
    total = sum(p.size for p in small_w)
    rows = -(-total // (8 * LANES)) * 8
    (gs_all,) = _all_gather(
        [_pack([grads[n] for n in names], rows)],
        [lambda ref, n: ref.at[n]],
        [jax.ShapeDtypeStruct((N_DEV, 1, rows, LANES), F32)], "gather_small_grads")
    packed = _adamw(_pack(small_w, rows), _pack(small_m, rows), _pack(small_v, rows),
                    [[gs_all.reshape(N_DEV, rows, LANES)]], "adamw_small", rows)
    g_s, d_s, nm_s, nv_s = [_unpack(p, small_w) for p in packed]

    gin_sets, gout_sets = [], []
    for l in range(L):
        got_in, got_out = _split_wait(exchanges[l], exchange_kinds, (dx, packed[1]), f"exchange_wait_{l}")
        gin_sets.append([own_parts[l][0], got_in])
        gout_sets.append([own_parts[l][1], got_out])
    g_w_in, d_w_in, nm_w_in, nv_w_in = _adamw(w_in, m_w_in, v_w_in, gin_sets, "adamw_w_in", 256)
    g_w_out, d_w_out, nm_w_out, nv_w_out = _adamw(w_out, m_w_out, v_w_out, gout_sets, "adamw_w_out", 128)

    def order(big_in, big_out, small):
        return (small[0], big_in, small[1], small[2], small[3], big_out, small[4])

    return (loss, dx[None], *order(g_w_in, g_w_out, g_s), *order(d_w_in, d_w_out, d_s),
            *order(nm_w_in, nm_w_out, nm_s), *order(nv_w_in, nv_w_out, nv_s))
```

```python
import math

import jax
import jax.numpy as jnp
from jax import lax
from jax.experimental import pallas as pl
from jax.experimental.pallas import tpu as pltpu

F32 = jnp.float32
BF16 = jnp.bfloat16
MESH = pl.DeviceIdType.MESH

RMS_EPS = 1e-6
NEG_INF = -1e30
HEAD_DIM = 64
POOL_WINDOWS = (2, 4, 8, 16)
MAX_WINDOW = 16
LANES = 128
N_DEV = 8

ADAM_LR = 0.001
ADAM_B1 = 0.9
ADAM_B2 = 0.999
ADAM_EPS = 1e-08
ADAM_WD = 0.01
ADAM_STEP = 10

TM = 512
TN = 512
TQ = 512
TB = 256
VMEM_LIMIT = 56 * 1024 * 1024

NT = (((1,), (1,)), ((), ()))
TN_DIMS = (((0,), (0,)), ((), ()))

SLOT_C, SLOT_ONE, SLOT_LSE = 0, 3, 6


def _params(*sem):
    return pltpu.CompilerParams(dimension_semantics=sem, vmem_limit_bytes=VMEM_LIMIT)


def _sigmoid(x):
    return 1.0 / (1.0 + jnp.exp(-x))


def _split3(x):
    hi = x.astype(BF16).astype(F32)
    rest = x - hi
    mid = rest.astype(BF16).astype(F32)
    return hi, mid, rest - mid


def _after_specs(after):
    return [pl.BlockSpec(memory_space=pl.ANY)] * len(after)


def _inproj_fwd(x, gam, wmain, wf, after=()):
    S, D = x.shape
    N = wmain.shape[1]
    tm, tn = min(TM, S), TN

    def body(x_ref, g_ref, w_ref, wf_ref, *rest):
        proj_ref, h_ref, z_ref = rest[-3:]
        xf = x_ref[...]
        r = lax.rsqrt(jnp.mean(xf * xf, axis=-1, keepdims=True) + RMS_EPS)
        h = ((xf * r) * g_ref[...]).astype(BF16)
        h_ref[...] = h
        z_ref[...] = jnp.dot(h, wf_ref[...], preferred_element_type=F32)
        for n in range(N // tn):
            cols = slice(n * tn, (n + 1) * tn)
            proj_ref[:, cols] = jnp.dot(h, w_ref[:, cols], preferred_element_type=F32).astype(BF16)

    return pl.pallas_call(
        body, name="inproj_fwd", grid=(S // tm,),
        in_specs=[pl.BlockSpec((tm, D), lambda i: (i, 0)),
                  pl.BlockSpec((1, D), lambda i: (0, 0)),
                  pl.BlockSpec((D, N), lambda i: (0, 0)),
                  pl.BlockSpec((D, LANES), lambda i: (0, 0))] + _after_specs(after),
        out_specs=[pl.BlockSpec((tm, N), lambda i: (i, 0)),
                   pl.BlockSpec((tm, D), lambda i: (i, 0)),
                   pl.BlockSpec((tm, LANES), lambda i: (i, 0))],
        out_shape=[jax.ShapeDtypeStruct((S, N), BF16),
                   jax.ShapeDtypeStruct((S, D), BF16),
                   jax.ShapeDtypeStruct((S, LANES), F32)],
        compiler_params=_params("parallel"),
    )(x, gam, wmain, wf, *after)


def _fgate_fwd(z, bias, n_heads):
    S = z.shape[0]
    tb = min(TB, S)
    P = n_heads // 2

    def body(z_ref, b_ref, qaug_ref, kaug_ref):
        lane = lax.broadcasted_iota(jnp.int32, (tb, LANES), 1)
        tri = (lax.broadcasted_iota(jnp.int32, (tb, tb), 0)
               >= lax.broadcasted_iota(jnp.int32, (tb, tb), 1)).astype(F32)
        head = lax.broadcasted_iota(jnp.int32, (LANES, P * LANES), 0)
        col = lax.broadcasted_iota(jnp.int32, (LANES, P * LANES), 1)
        home = (head >> 1) * LANES + jnp.where((head & 1) == 0, HEAD_DIM, 0)
        is_head = head < n_heads
        place_q = [jnp.logical_and(is_head, col == home + SLOT_C + n).astype(BF16) for n in range(3)]
        place_k = [jnp.logical_and(is_head, col == home + SLOT_ONE + n).astype(BF16) for n in range(3)]
        slot = lax.broadcasted_iota(jnp.int32, (tb, P * LANES), 1) & (HEAD_DIM - 1)
        q_ones = jnp.logical_and(slot >= SLOT_ONE, slot < SLOT_ONE + 3).astype(F32)
        k_ones = jnp.logical_or(slot < SLOT_C + 3,
                                jnp.logical_and(slot >= SLOT_LSE, slot < SLOT_LSE + 3)).astype(F32)

        local = []
        for b in range(S // tb):
            zz = z_ref[b * tb:(b + 1) * tb, :] + b_ref[...]
            lf = jnp.minimum(zz, 0.0) - jnp.log(1.0 + jnp.exp(-jnp.abs(zz)))
            lf = jnp.where(lane < n_heads, lf, 0.0)
            local.append(jnp.dot(tri, lf, preferred_element_type=F32, precision=lax.Precision.HIGHEST))
        carry = jnp.zeros((1, LANES), F32)
        for b, part_sum in enumerate(local):
            c = part_sum + carry
            carry = c[tb - 1:tb, :]
            qa, ka = q_ones, k_ones
            for n, part in enumerate(_split3(c)):
                qa = qa + jnp.dot(part.astype(BF16), place_q[n], preferred_element_type=F32)
                ka = ka - jnp.dot(part.astype(BF16), place_k[n], preferred_element_type=F32)
            qaug_ref[b * tb:(b + 1) * tb, :] = qa.astype(BF16)
            kaug_ref[b * tb:(b + 1) * tb, :] = ka.astype(BF16)

    return pl.pallas_call(
        body, name="fgate_fwd",
        out_shape=[jax.ShapeDtypeStruct((S, P * LANES), BF16),
                   jax.ShapeDtypeStruct((S, P * LANES), BF16)],
        compiler_params=pltpu.CompilerParams(vmem_limit_bytes=VMEM_LIMIT),
    )(z, bias)


def _window_mean_minus_self(u, pad_ref, w, S):
    pad_ref[0:MAX_WINDOW, :] = jnp.zeros((MAX_WINDOW, LANES), F32)
    pad_ref[MAX_WINDOW:MAX_WINDOW + S, :] = u
    acc = u
    for j in range(1, w):
        acc = acc + pad_ref[MAX_WINDOW - j:MAX_WINDOW - j + S, :]
    t = lax.broadcasted_iota(jnp.int32, (S, LANES), 0)
    cnt = jnp.minimum(t + 1, w).astype(F32)
    return acc / cnt - u, cnt


def _pool_fwd(proj, pool_w, pool_scale):
    S = proj.shape[0]
    G = len(POOL_WINDOWS)

    def body(u_ref, w_ref, s_ref, y_ref, pad_ref):
        g = pl.program_id(0)
        for gi, w in enumerate(POOL_WINDOWS):
            @pl.when(g == gi)
            def _():
                d, _ = _window_mean_minus_self(u_ref[...].astype(F32), pad_ref, w, S)
                y = jnp.dot(d.astype(BF16), w_ref[0].astype(BF16), preferred_element_type=F32)
                y_ref[...] = (y * s_ref[...]).astype(BF16)

    return pl.pallas_call(
        body, name="pool_fwd", grid=(G,),
        in_specs=[pl.BlockSpec((S, LANES), lambda g: (0, g)),
                  pl.BlockSpec((1, LANES, LANES), lambda g: (g, 0, 0)),
                  pl.BlockSpec((1, LANES), lambda g: (0, g))],
        out_specs=pl.BlockSpec((S, LANES), lambda g: (0, g)),
        out_shape=jax.ShapeDtypeStruct((S, G * LANES), BF16),
        scratch_shapes=[pltpu.VMEM((S + MAX_WINDOW, LANES), F32)],
        compiler_params=_params("arbitrary"),
    )(proj, pool_w, pool_scale)


def _head_halves(rows):
    lane = lax.broadcasted_iota(jnp.int32, (rows, LANES), 1)
    return lane, (lane < HEAD_DIM, lane >= HEAD_DIM)


def _attn_fwd(proj, qaug, kaug):
    S = proj.shape[0]
    W = proj.shape[1] // 6
    P = W // LANES
    tq = min(TQ, S)
    nq = S // tq
    qc, kc, vc = 2 * P, 3 * P, 4 * P
    scale = 1.0 / math.sqrt(HEAD_DIM)

    def body(q_ref, k_ref, v_ref, qa_ref, ka_ref, o_ref, lse_ref, qm_scr, m_scr, acc_scr):
        i = pl.program_id(1)
        lane, halves = _head_halves(tq)
        v_ones = ((lane & (HEAD_DIM - 1)) < 3).astype(BF16)
        qs = q_ref[...] * scale
        qm_scr[0] = jnp.where(halves[0], qs, qa_ref[...])
        qm_scr[1] = jnp.where(halves[1], qs, qa_ref[...])
        m_scr[...] = jnp.full(m_scr.shape, NEG_INF, F32)
        acc_scr[...] = jnp.zeros(acc_scr.shape, F32)

        def update(j, on_diagonal):
            keys = pl.ds(pl.multiple_of(j * tq, tq), tq)
            k2, v2, kaug_t = k_ref[keys, :], v_ref[keys, :], ka_ref[keys, :]
            if on_diagonal:
                keep = (lax.broadcasted_iota(jnp.int32, (tq, tq), 0)
                        >= lax.broadcasted_iota(jnp.int32, (tq, tq), 1))
            logits = [lax.dot_general(qm_scr[a], jnp.where(halves[a], k2, kaug_t), NT, preferred_element_type=F32)
                      for a in range(2)]
            for a in range(2):
                s = jnp.where(keep, logits[a], NEG_INF) if on_diagonal else logits[a]
                va = jnp.where(halves[a], v2, v_ones)
                m_prev = m_scr[a]
                m_new = jnp.maximum(m_prev, jnp.max(s, axis=1, keepdims=True))
                p = jnp.exp(s - jnp.tile(m_new, (1, tq // LANES)))
                acc_scr[a] = jnp.exp(m_prev - m_new) * acc_scr[a] + jnp.dot(p.astype(BF16), va,
                                                                              preferred_element_type=F32)
                m_scr[a] = m_new

        def below_diagonal(jj, carry):
            update(2 * jj, False)
            update(2 * jj + 1, False)
            return carry

        lax.fori_loop(0, i // 2, below_diagonal, 0)

        @pl.when(i % 2 == 1)
        def _():
            update(i - 1, False)

        update(i, True)
        acc_a, acc_b = acc_scr[0], acc_scr[1]
        l_a, l_b = acc_a[:, HEAD_DIM:HEAD_DIM + 1], acc_b[:, 0:1]
        o_ref[...] = jnp.where(halves[0], acc_a / l_a, acc_b / l_b).astype(BF16)
        lse_ref[...] = jnp.where(halves[0], m_scr[1] + jnp.log(l_b), m_scr[0] + jnp.log(l_a))

    tile = lambda col: pl.BlockSpec((tq, LANES), lambda p, i: (i, col + p))
    whole = lambda col: pl.BlockSpec((S, LANES), lambda p, i: (0, col + p))
    return pl.pallas_call(
        body, name="attn_fwd", grid=(P, nq),
        in_specs=[tile(qc), whole(kc), whole(vc), tile(0), whole(0)],
        out_specs=[tile(0), tile(0)],
        out_shape=[jax.ShapeDtypeStruct((S, W), BF16), jax.ShapeDtypeStruct((S, W), F32)],
        scratch_shapes=[pltpu.VMEM((2, tq, LANES), BF16),
                        pltpu.VMEM((2, tq, LANES), F32),
                        pltpu.VMEM((2, tq, LANES), F32)],
        compiler_params=_params("parallel", "arbitrary"),
    )(proj, proj, proj, qaug, kaug)


def _outproj_fwd(ypool, o, proj, x, wout):
    S, D = x.shape
    W = D // 2
    tm, tn = min(TM, S), TN

    def body(y_ref, o_ref, pg_ref, ag_ref, x_ref, w_ref, xn_ref, mix_ref):
        pg, ag = pg_ref[...].astype(F32), ag_ref[...].astype(F32)
        mix_ref[:, 0:W] = (y_ref[...].astype(F32) * (pg * _sigmoid(pg))).astype(BF16)
        mix_ref[:, W:D] = (o_ref[...].astype(F32) * (ag * _sigmoid(ag))).astype(BF16)
        for n in range(D // tn):
            cols = slice(n * tn, (n + 1) * tn)
            xn_ref[:, cols] = x_ref[:, cols] + jnp.dot(mix_ref[...], w_ref[:, cols], preferred_element_type=F32)

    return pl.pallas_call(
        body, name="outproj_fwd", grid=(S // tm,),
        in_specs=[pl.BlockSpec((tm, W), lambda i: (i, 0)),
                  pl.BlockSpec((tm, W), lambda i: (i, 0)),
                  pl.BlockSpec((tm, W), lambda i: (i, 1)),
                  pl.BlockSpec((tm, W), lambda i: (i, 5)),
                  pl.BlockSpec((tm, D), lambda i: (i, 0)),
                  pl.BlockSpec((D, D), lambda i: (0, 0))],
        out_specs=[pl.BlockSpec((tm, D), lambda i: (i, 0)),
                   pl.BlockSpec((tm, D), lambda i: (i, 0))],
        out_shape=[jax.ShapeDtypeStruct((S, D), F32),
                   jax.ShapeDtypeStruct((S, D), BF16)],
        compiler_params=_params("parallel"),
    )(ypool, o, proj, proj, x, wout)


def _loss_head(x, gam, target):
    S, D = x.shape
    tm = min(TM, S)

    def body(x_ref, g_ref, t_ref, dx_ref, loss_ref, dg_ref):
        @pl.when(pl.program_id(0) == 0)
        def _():
            loss_ref[...] = jnp.zeros(loss_ref.shape, F32)
            dg_ref[...] = jnp.zeros(dg_ref.shape, F32)

        xf, gam_v = x_ref[...], g_ref[...]
        r = lax.rsqrt(jnp.mean(xf * xf, axis=-1, keepdims=True) + RMS_EPS)
        xhat = xf * r
        err = xhat * gam_v - t_ref[...]
        part = jnp.sum(jnp.sum(err * err, axis=-1, keepdims=True), axis=0, keepdims=True)
        loss_ref[...] += part * (0.5 / D)
        dy = err * (1.0 / D)
        dg_ref[...] += jnp.sum(dy * xhat, axis=0, keepdims=True)
        dxhat = dy * gam_v
        dx_ref[...] = r * (dxhat - xhat * jnp.mean(dxhat * xhat, axis=-1, keepdims=True))

    return pl.pallas_call(
        body, name="loss_head", grid=(S // tm,),
        in_specs=[pl.BlockSpec((tm, D), lambda i: (i, 0)),
                  pl.BlockSpec((1, D), lambda i: (0, 0)),
                  pl.BlockSpec((tm, D), lambda i: (i, 0))],
        out_specs=[pl.BlockSpec((tm, D), lambda i: (i, 0)),
                   pl.BlockSpec((8, LANES), lambda i: (0, 0)),
                   pl.BlockSpec((1, D), lambda i: (0, 0))],
        out_shape=[jax.ShapeDtypeStruct((S, D), F32),
                   jax.ShapeDtypeStruct((8, LANES), F32),
                   jax.ShapeDtypeStruct((1, D), F32)],
        compiler_params=_params("arbitrary"),
    )(x, gam, target)


def _outproj_bwd(g, wout, mixed, ypool, o, proj):
    S, D = g.shape
    W = D // 2
    tm = min(TM, S)

    def body(g_ref, w_ref, mix_ref, y_ref, o_ref, pg_ref, ag_ref, dw_ref, da_ref, dgate_ref):
        @pl.when(pl.program_id(0) == 0)
        def _():
            dw_ref[...] = jnp.zeros(dw_ref.shape, F32)

        gb = g_ref[...].astype(BF16)
        dw_ref[...] += lax.dot_general(mix_ref[...], gb, TN_DIMS, preferred_element_type=F32)
        for half, (val_ref, gate_ref) in enumerate(((y_ref, pg_ref), (o_ref, ag_ref))):
            cols = slice(half * W, (half + 1) * W)
            dmix = lax.dot_general(gb, w_ref[cols, :], NT, preferred_element_type=F32)
            gt = gate_ref[...].astype(F32)
            sg = _sigmoid(gt)
            da_ref[:, cols] = (dmix * (gt * sg)).astype(BF16)
            dgate_ref[:, cols] = (dmix * val_ref[...].astype(F32) * (sg * (1.0 + gt * (1.0 - sg)))).astype(BF16)

    rows = lambda width, col: pl.BlockSpec((tm, width), lambda i: (i, col))
    return pl.pallas_call(
        body, name="outproj_bwd", grid=(S // tm,),
        in_specs=[rows(D, 0), pl.BlockSpec((D, D), lambda i: (0, 0)), rows(D, 0), rows(W, 0), rows(W, 0),
                  rows(W, 1), rows(W, 5)],
        out_specs=[pl.BlockSpec((D, D), lambda i: (0, 0)), rows(D, 0), rows(D, 0)],
        out_shape=[jax.ShapeDtypeStruct((D, D), F32),
                   jax.ShapeDtypeStruct((S, D), BF16),
                   jax.ShapeDtypeStruct((S, D), BF16)],
        compiler_params=_params("arbitrary"),
    )(g, wout, mixed, ypool, o, proj, proj)


def _inproj_bwd_dw(h, dproj, dzf):
    S, D = h.shape
    N = dproj.shape[1]
    ts, tn = min(TM, S), TN

    def body(h_ref, dp_ref, dz_ref, dw_ref, dwf_ref):
        @pl.when(pl.program_id(0) == 0)
        def _():
            dw_ref[...] = jnp.zeros(dw_ref.shape, F32)
            dwf_ref[...] = jnp.zeros(dwf_ref.shape, F32)

        ht = h_ref[...].T
        dwf_ref[...] += jnp.dot(ht, dz_ref[...], preferred_element_type=F32)
        for n in range(N // tn):
            cols = slice(n * tn, (n + 1) * tn)
            dw_ref[:, cols] += jnp.dot(ht, dp_ref[:, cols], preferred_element_type=F32)

    return pl.pallas_call(
        body, name="inproj_bwd_dw", grid=(S // ts,),
        in_specs=[pl.BlockSpec((ts, D), lambda k: (k, 0)),
                  pl.BlockSpec((ts, N), lambda k: (k, 0)),
                  pl.BlockSpec((ts, LANES), lambda k: (k, 0))],
        out_specs=[pl.BlockSpec((D, N), lambda k: (0, 0)),
                   pl.BlockSpec((D, LANES), lambda k: (0, 0))],
        out_shape=[jax.ShapeDtypeStruct((D, N), F32),
                   jax.ShapeDtypeStruct((D, LANES), F32)],
        compiler_params=_params("arbitrary"),
    )(h, dproj, dzf)


def _attn_bwd_prep(da, o, lse, qaug):
    S, D = da.shape
    W = D // 2
    P = W // LANES
    tq = min(TQ, S)

    def body(do_ref, o_ref, lse_ref, qa_ref, qb_ref, da_ref):
        lane, halves = _head_halves(tq)
        slot = lane & (HEAD_DIM - 1)
        prod = do_ref[...].astype(F32) * o_ref[...].astype(F32)
        d_a = jnp.sum(jnp.where(halves[0], prod, 0.0), axis=1, keepdims=True)
        d_b = jnp.sum(jnp.where(halves[1], prod, 0.0), axis=1, keepdims=True)
        aug = jnp.zeros((tq, LANES), F32)
        for n, part in enumerate(_split3(jnp.where(halves[0], d_b, d_a))):
            aug = jnp.where(slot == SLOT_C + n, -part, aug)
        da_ref[...] = aug.astype(BF16)
        aug = qa_ref[...].astype(F32)
        for n, part in enumerate(_split3(lse_ref[...])):
            aug = jnp.where(slot == SLOT_LSE + n, -part, aug)
        qb_ref[...] = aug.astype(BF16)

    spec = pl.BlockSpec((tq, LANES), lambda p, i: (i, p))
    return pl.pallas_call(
        body, name="attn_bwd_prep", grid=(P, S // tq),
        in_specs=[pl.BlockSpec((tq, LANES), lambda p, i: (i, P + p)), spec, spec, spec],
        out_specs=[spec, spec],
        out_shape=[jax.ShapeDtypeStruct((S, W), BF16), jax.ShapeDtypeStruct((S, W), BF16)],
        compiler_params=_params("parallel", "parallel"),
    )(da, o, lse, qaug)


def _attn_bwd(proj, da, qaug, kaug, doaug):
    S = proj.shape[0]
    W = proj.shape[1] // 6
    P = W // LANES
    tq = min(TQ, S)
    nq = S // tq
    qc, kc, vc = 2 * P, 3 * P, 4 * P
    scale = 1.0 / math.sqrt(HEAD_DIM)

    def body(q_ref, k_ref, v_ref, do_ref, qa_ref, ka_ref, da_ref,
             dq_ref, dk_ref, dv_ref, dqx_ref, dkx_ref, km_scr, vm_scr, dk_scr, dv_scr, dq_scr):
        j = pl.program_id(1)
        lane, halves = _head_halves(tq)

        @pl.when(j == 0)
        def _():
            dq_scr[...] = jnp.zeros(dq_scr.shape, F32)

        v_ones = ((lane & (HEAD_DIM - 1)) < 3).astype(BF16)
        for a in range(2):
            km_scr[a] = jnp.where(halves[a], k_ref[...], ka_ref[...])
            vm_scr[a] = jnp.where(halves[a], v_ref[...], v_ones)
        dk_scr[...] = jnp.zeros(dk_scr.shape, F32)
        dv_scr[...] = jnp.zeros(dv_scr.shape, F32)

        def update(i, on_diagonal):
            rows = pl.ds(pl.multiple_of(i * tq, tq), tq)
            qs = q_ref[rows, :] * scale
            do2, qaug_t, doaug_t = do_ref[rows, :], qa_ref[rows, :], da_ref[rows, :]
            if on_diagonal:
                keep = (lax.broadcasted_iota(jnp.int32, (tq, tq), 0)
                        >= lax.broadcasted_iota(jnp.int32, (tq, tq), 1))
            qas = [jnp.where(halves[a], qs, qaug_t) for a in range(2)]
            logits = [lax.dot_general(qas[a], km_scr[a], NT, preferred_element_type=F32) for a in range(2)]
            dps = [lax.dot_general(jnp.where(halves[a], do2, doaug_t), vm_scr[a], NT, preferred_element_type=F32)
                   for a in range(2)]
            dv = None
            for a in range(2):
                s = jnp.where(keep, logits[a], NEG_INF) if on_diagonal else logits[a]
                p = jnp.exp(s)
                dsb = (p * dps[a]).astype(BF16)
                do0 = jnp.where(halves[a], do2, jnp.zeros_like(do2))
                dv_a = lax.dot_general(p.astype(BF16), do0, TN_DIMS, preferred_element_type=F32)
                dv = dv_a if dv is None else dv + dv_a
                dk_scr[a] += lax.dot_general(dsb, qas[a], TN_DIMS, preferred_element_type=F32)
                dq_scr[a, rows, :] += jnp.dot(dsb, km_scr[a], preferred_element_type=F32)
            dv_scr[...] += dv

        def below_diagonal(n, carry):
            update(j + 1 + 2 * n, False)
            update(j + 2 + 2 * n, False)
            return carry

        update(j, True)
        below = nq - 1 - j
        lax.fori_loop(0, below // 2, below_diagonal, 0)

        @pl.when(below % 2 == 1)
        def _():
            update(nq - 1, False)
        dk_ref[...] = jnp.where(halves[0], dk_scr[0], dk_scr[1]).astype(BF16)
        dkx_ref[...] = jnp.where(halves[0], dk_scr[1], dk_scr[0])
        dv_ref[...] = dv_scr[...].astype(BF16)

        @pl.when(j == nq - 1)
        def _():
            row_lane, row_halves = _head_halves(S)
            dq_ref[...] = (jnp.where(row_halves[0], dq_scr[0], dq_scr[1]) * scale).astype(BF16)
            dqx_ref[...] = jnp.where(row_halves[0], dq_scr[1], dq_scr[0])

    tile = lambda col: pl.BlockSpec((tq, LANES), lambda p, j: (j, col + p))
    whole = lambda col: pl.BlockSpec((S, LANES), lambda p, j: (0, col + p))
    return pl.pallas_call(
        body, name="attn_bwd", grid=(P, nq),
        in_specs=[whole(qc), tile(kc), tile(vc), whole(P), whole(0), tile(0), whole(0)],
        out_specs=[whole(0), tile(0), tile(0), whole(0), tile(0)],
        out_shape=[jax.ShapeDtypeStruct((S, W), BF16),
                   jax.ShapeDtypeStruct((S, W), BF16),
                   jax.ShapeDtypeStruct((S, W), BF16),
                   jax.ShapeDtypeStruct((S, W), F32),
                   jax.ShapeDtypeStruct((S, W), F32)],
        scratch_shapes=[pltpu.VMEM((2, tq, LANES), BF16),
                        pltpu.VMEM((2, tq, LANES), BF16),
                        pltpu.VMEM((2, tq, LANES), F32),
                        pltpu.VMEM((tq, LANES), F32),
                        pltpu.VMEM((2, S, LANES), F32)],
        compiler_params=_params("parallel", "arbitrary"),
    )(proj, proj, proj, da, qaug, kaug, doaug)


def _fgate_bwd(dqx, dkx, z, bias, n_heads):
    S = z.shape[0]
    tb = min(TB, S)
    nb = S // tb

    def body(dqx_ref, dkx_ref, z_ref, b_ref, dz_ref, db_ref):
        tri = (lax.broadcasted_iota(jnp.int32, (tb, tb), 1)
               >= lax.broadcasted_iota(jnp.int32, (tb, tb), 0)).astype(F32)
        n_cols = dqx_ref.shape[1]
        col = lax.broadcasted_iota(jnp.int32, (n_cols, LANES), 0)
        head = lax.broadcasted_iota(jnp.int32, (n_cols, LANES), 1)
        home = (head >> 1) * LANES + jnp.where((head & 1) == 0, HEAD_DIM, 0)
        is_head = head < n_heads
        pick_rows = jnp.logical_and(is_head, col == home + SLOT_C).astype(F32)
        pick_cols = jnp.logical_and(is_head, col == home + SLOT_ONE).astype(F32)

        local = []
        for b in range(nb):
            rows = slice(b * tb, (b + 1) * tb)
            dc = (jnp.dot(dqx_ref[rows, :], pick_rows, preferred_element_type=F32, precision=lax.Precision.HIGH)
                  - jnp.dot(dkx_ref[rows, :], pick_cols, preferred_element_type=F32, precision=lax.Precision.HIGH))
            local.append(jnp.dot(tri, dc, preferred_element_type=F32, precision=lax.Precision.HIGHEST))
        carry = jnp.zeros((1, LANES), F32)
        db = jnp.zeros((1, LANES), F32)
        for b in reversed(range(nb)):
            rows = slice(b * tb, (b + 1) * tb)
            rc = local[b] + carry
            carry = rc[0:1, :]
            dz = rc * _sigmoid(-(z_ref[rows, :] + b_ref[...]))
            dz_ref[rows, :] = dz.astype(BF16)
            db = db + jnp.sum(dz, axis=0, keepdims=True)
        db_ref[...] = db

    return pl.pallas_call(
        body, name="fgate_bwd",
        out_shape=[jax.ShapeDtypeStruct((S, LANES), BF16),
                   jax.ShapeDtypeStruct((1, LANES), F32)],
        compiler_params=pltpu.CompilerParams(vmem_limit_bytes=VMEM_LIMIT),
    )(dqx, dkx, z, bias)


def _pool_bwd(proj, da, pool_w, pool_scale):
    S = proj.shape[0]
    G = len(POOL_WINDOWS)

    def body(u_ref, dy_ref, w_ref, s_ref, du_ref, dw_ref, ds_ref, pad_ref):
        g = pl.program_id(0)
        for gi, w in enumerate(POOL_WINDOWS):
            @pl.when(g == gi)
            def _():
                d, cnt = _window_mean_minus_self(u_ref[...].astype(F32), pad_ref, w, S)
                db = d.astype(BF16)
                wb = w_ref[0].astype(BF16)
                yraw = jnp.dot(db, wb, preferred_element_type=F32)
                dy = dy_ref[...].astype(F32)
                ds_ref[...] = jnp.sum(dy * yraw, axis=0, keepdims=True)
                dzb = (dy * s_ref[...]).astype(BF16)
                dw_ref[0] = lax.dot_general(db, dzb, TN_DIMS, preferred_element_type=F32)
                dd = lax.dot_general(dzb, wb, NT, preferred_element_type=F32)
                pad_ref[0:S, :] = dd / cnt
                pad_ref[S:S + MAX_WINDOW, :] = jnp.zeros((MAX_WINDOW, LANES), F32)
                acc = -dd
                for j in range(w):
                    acc = acc + pad_ref[j:j + S, :]
                du_ref[...] = acc.astype(BF16)

    return pl.pallas_call(
        body, name="pool_bwd", grid=(G,),
        in_specs=[pl.BlockSpec((S, LANES), lambda g: (0, g)),
                  pl.BlockSpec((S, LANES), lambda g: (0, g)),
                  pl.BlockSpec((1, LANES, LANES), lambda g: (g, 0, 0)),
                  pl.BlockSpec((1, LANES), lambda g: (0, g))],
        out_specs=[pl.BlockSpec((S, LANES), lambda g: (0, g)),
                   pl.BlockSpec((1, LANES, LANES), lambda g: (g, 0, 0)),
                   pl.BlockSpec((1, LANES), lambda g: (0, g))],
        out_shape=[jax.ShapeDtypeStruct((S, G * LANES), BF16),
                   jax.ShapeDtypeStruct((G, LANES, LANES), F32),
                   jax.ShapeDtypeStruct((1, G * LANES), F32)],
        scratch_shapes=[pltpu.VMEM((S + MAX_WINDOW, LANES), F32)],
        compiler_params=_params("arbitrary"),
    )(proj, da, pool_w, pool_scale)


def _inproj_bwd_dx(dproj, dzf, wmain, wf, x, gam, g, after=()):
    S, D = x.shape
    N = wmain.shape[1]
    tm = min(TM // 2, S)

    def body(dp_ref, dz_ref, w_ref, wf_ref, x_ref, gam_ref, g_ref, *rest):
        dx_ref, dg_ref = rest[-2:]

        @pl.when(pl.program_id(0) == 0)
        def _():
            dg_ref[...] = jnp.zeros(dg_ref.shape, F32)

        dh = lax.dot_general(dp_ref[...], w_ref[...], NT, preferred_element_type=F32)
        dh = dh + lax.dot_general(dz_ref[...], wf_ref[...], NT, preferred_element_type=F32)
        xf = x_ref[...]
        r = lax.rsqrt(jnp.mean(xf * xf, axis=-1, keepdims=True) + RMS_EPS)
        xhat = xf * r
        dg_ref[...] += jnp.sum(dh * xhat, axis=0, keepdims=True)
        dxhat = dh * gam_ref[...]
        dx_ref[...] = g_ref[...] + r * (dxhat - xhat * jnp.mean(dxhat * xhat, axis=-1, keepdims=True))

    return pl.pallas_call(
        body, name="inproj_bwd_dx", grid=(S // tm,),
        in_specs=[pl.BlockSpec((tm, N), lambda i: (i, 0)),
                  pl.BlockSpec((tm, LANES), lambda i: (i, 0)),
                  pl.BlockSpec((D, N), lambda i: (0, 0)),
                  pl.BlockSpec((D, LANES), lambda i: (0, 0)),
                  pl.BlockSpec((tm, D), lambda i: (i, 0)),
                  pl.BlockSpec((1, D), lambda i: (0, 0)),
                  pl.BlockSpec((tm, D), lambda i: (i, 0))] + _after_specs(after),
        out_specs=[pl.BlockSpec((tm, D), lambda i: (i, 0)),
                   pl.BlockSpec((1, D), lambda i: (0, 0))],
        out_shape=[jax.ShapeDtypeStruct((S, D), F32),
                   jax.ShapeDtypeStruct((1, D), F32)],
        compiler_params=_params("arbitrary"),
    )(dproj, dzf, wmain, wf, x, gam, g, *after)


def _adamw(w, m, v, gsets, name, rows):
    A, R, C = w.shape
    tr = min(rows, R)
    c1 = 1.0 / (1.0 - ADAM_B1 ** ADAM_STEP)
    c2 = 1.0 / (1.0 - ADAM_B2 ** ADAM_STEP)
    counts = [len(gs) for gs in gsets]

    def body(w_ref, m_ref, v_ref, *rest):
        g_ref, d_ref, nm_ref, nv_ref = rest[-4:]
        at = 0
        for a in range(A):
            part_refs = rest[at:at + counts[a]]
            at += counts[a]

            @pl.when(pl.program_id(0) == a)
            def _():
                g = None
                for ref in part_refs:
                    for s in range(ref.shape[0]):
                        term = ref[s].astype(F32)
                        g = term if g is None else g + term
                nm = ADAM_B1 * m_ref[0] + (1.0 - ADAM_B1) * g
                nv = ADAM_B2 * v_ref[0] + (1.0 - ADAM_B2) * (g * g)
                g_ref[0] = g
                nm_ref[0] = nm
                nv_ref[0] = nv
                d_ref[0] = -ADAM_LR * ((nm * c1) / (jnp.sqrt(nv * c2) + ADAM_EPS) + ADAM_WD * w_ref[0])

    spec = pl.BlockSpec((1, tr, C), lambda a, r: (a, r, 0))
    part_specs = [pl.BlockSpec((part.shape[0], tr, C), lambda a, r, l=l: (0, jnp.where(a == l, r, 0), 0))
                  for l, gs in enumerate(gsets) for part in gs]
    shape = jax.ShapeDtypeStruct((A, R, C), F32)
    return pl.pallas_call(
        body, name=name, grid=(A, R // tr),
        in_specs=[spec, spec, spec] + part_specs,
        out_specs=[spec, spec, spec, spec],
        out_shape=[shape, shape, shape, shape],
        compiler_params=_params("arbitrary", "arbitrary"),
    )(w, m, v, *[part for gs in gsets for part in gs])


def _position():
    return lax.axis_index("x"), lax.axis_index("y"), lax.axis_index("c")


def _index(dev):
    return 4 * dev[0] + 2 * dev[1] + dev[2]


def _all_gather(arrs, slots, out_shapes, name):
    n_arr = len(arrs)

    def body(*refs):
        ins, outs = refs[:n_arr], refs[n_arr:2 * n_arr]
        send_sems, recv_sems, local_sems = refs[2 * n_arr:]
        x, y, c = _position()
        me, sibling = (x, y, c), (x, y, 1 - c)
        chips = [(1 - x, y), (x, 1 - y), (1 - x, 1 - y)]

        def copy(a, k, block, to, src=None):
            part = slots[a](outs[a], _index(block))
            return pltpu.make_async_remote_copy(
                src_ref=part if src is None else src, dst_ref=part,
                send_sem=send_sems.at[a, k], recv_sem=recv_sems.at[a, k],
                device_id=to, device_id_type=MESH)

        mine = [pltpu.make_async_copy(ins[a], slots[a](outs[a], _index(me)), local_sems.at[a])
                for a in range(n_arr)]
        for cp in mine:
            cp.start()
        first = []
        for a in range(n_arr):
            first.append(copy(a, 0, me, sibling, src=ins[a]))
            first += [copy(a, 1 + j, me, (*chip, c), src=ins[a]) for j, chip in enumerate(chips)]
        for cp in first:
            cp.start()
        passed = []
        for j, chip in enumerate(chips):
            for a in range(n_arr):
                copy(a, 1 + j, (*chip, c), me).wait_recv()
                fwd = copy(a, 4 + j, (*chip, c), sibling)
                fwd.start()
                passed.append(fwd)
        for a in range(n_arr):
            copy(a, 0, sibling, me).wait_recv()
            for j, chip in enumerate(chips):
                copy(a, 4 + j, (*chip, 1 - c), me).wait_recv()
        for cp in first + passed:
            cp.wait_send()
        for cp in mine:
            cp.wait()

    any_spec = pl.BlockSpec(memory_space=pl.ANY)
    return pl.pallas_call(
        body, name=name,
        in_specs=[any_spec] * n_arr, out_specs=[any_spec] * n_arr, out_shape=out_shapes,
        scratch_shapes=[pltpu.SemaphoreType.DMA((n_arr, 7)), pltpu.SemaphoreType.DMA((n_arr, 7)),
                        pltpu.SemaphoreType.DMA((n_arr,))],
    )(*arrs)


def _split_copies(srcs, lands, send_sems, recv_sems, kinds):
    x, y, c = _position()
    me = _index((x, y, c))
    copies = []
    for a, (src_part, land_part) in enumerate(kinds):
        for k in range(1, N_DEV):
            peer = (x ^ ((k >> 2) & 1), y ^ ((k >> 1) & 1), c ^ (k & 1))
            copies.append(pltpu.make_async_remote_copy(
                src_ref=src_part(srcs[a], _index(peer)), dst_ref=land_part(lands[a], me, k),
                send_sem=send_sems[a].at[k - 1], recv_sem=recv_sems[a].at[k - 1],
                device_id=peer, device_id_type=MESH))
    return copies


def _split_start(srcs, lands, kinds, name, after=()):
    n = len(srcs)

    def body(*refs):
        src_refs, land_refs = refs[:n], refs[n:2 * n]
        outs = refs[2 * n + len(after):]
        send_sems, recv_sems = outs[:n], outs[n:2 * n]
        token = outs[-1]
        for cp in _split_copies(src_refs, land_refs, send_sems, recv_sems, kinds):
            cp.start()
        token[...] = jnp.zeros(token.shape, token.dtype)

    hbm = pl.BlockSpec(memory_space=pltpu.HBM)
    sem = pl.BlockSpec(memory_space=pltpu.SEMAPHORE)
    operands = [pltpu.with_memory_space_constraint(t, pltpu.HBM) for t in (*srcs, *lands)]
    out = pl.pallas_call(
        body, name=name,
        in_specs=[hbm] * (2 * n) + _after_specs(after),
        out_specs=[sem] * (2 * n) + [hbm] * (2 * n) + [pl.BlockSpec(memory_space=pltpu.VMEM)],
        out_shape=[pltpu.SemaphoreType.DMA((N_DEV - 1,))] * (2 * n)
        + [pltpu.HBM(t.shape, t.dtype) for t in operands] + [jax.ShapeDtypeStruct((8, LANES), F32)],
        input_output_aliases={i: 2 * n + i for i in range(2 * n)},
        compiler_params=pltpu.CompilerParams(has_side_effects=pltpu.SideEffectType.DATAFLOW_SIDE_EFFECTING),
    )(*operands, *after)
    return [(out[a], out[n + a], out[2 * n + a], out[3 * n + a]) for a in range(n)], out[-1]


def _split_wait(started, kinds, after, name):
    n = len(started)
    sems = [t[0] for t in started] + [t[1] for t in started]
    srcs = [t[2] for t in started]
    lands = [t[3] for t in started]

    def body(*refs):
        src_refs, land_refs = refs[:n], refs[n:2 * n]
        send_sems, recv_sems = refs[2 * n:3 * n], refs[3 * n:4 * n]
        for cp in _split_copies(src_refs, land_refs, send_sems, recv_sems, kinds):
            cp.wait_send()
            cp.wait_recv()

    hbm = pl.BlockSpec(memory_space=pltpu.HBM)
    sem = pl.BlockSpec(memory_space=pltpu.SEMAPHORE)
    out = pl.pallas_call(
        body, name=name,
        in_specs=[hbm] * (2 * n) + [sem] * (2 * n) + _after_specs(after),
        out_specs=[hbm] * (2 * n),
        out_shape=[pltpu.HBM(t.shape, t.dtype) for t in (*srcs, *lands)],
        input_output_aliases={i: i for i in range(2 * n)},
        compiler_params=pltpu.CompilerParams(has_side_effects=pltpu.SideEffectType.DATAFLOW_SIDE_EFFECTING),
    )(*srcs, *lands, *sems, *after)
    return out[n:]


def _as_rows(p):
    if p.size % LANES == 0:
        rows = p.reshape(-1, LANES)
    else:
        rows = p.reshape(-1, p.shape[-1])
        rows = jnp.pad(rows, ((0, 0), (0, LANES - rows.shape[1])))
    return jnp.pad(rows, ((0, -rows.shape[0] % 8), (0, 0)))


def _pack(parts):
    return jnp.concatenate([_as_rows(p) for p in parts])[None]


def _unpack(packed, like):
    out, at = [], 0
    for p in like:
        whole = p.size % LANES == 0
        n = p.size // LANES if whole else p.size // p.shape[-1]
        rows = packed[0, at:at + n]
        out.append((rows if whole else rows[:, :p.shape[-1]]).reshape(p.shape))
        at += n + (-n % 8)
    return out


def _local_step(x, target, norm_g, forget_bias, pool_w, pool_scale, final_g, weights_in, weights_out, on_grads,
                first_after=()):
    L = norm_g.shape[0]
    S, D = x.shape
    W = D // 2
    H = W // HEAD_DIM
    bias = jnp.pad(forget_bias, ((0, 0), (0, LANES - H)))

    saved = []
    after = tuple(first_after)
    for l in range(L):
        wmain, wf = weights_in(l, x)
        proj, h, z = _inproj_fwd(x, norm_g[l:l + 1], wmain, wf, after)
        after = ()
        qaug, kaug = _fgate_fwd(z, bias[l:l + 1], H)
        ypool = _pool_fwd(proj, pool_w[l], pool_scale[l:l + 1])
        o, lse = _attn_fwd(proj, qaug, kaug)
        wout = weights_out(l, o)
        x_new, mixed = _outproj_fwd(ypool, o, proj, x, wout)
        saved.append((x, proj, h, z, qaug, kaug, ypool, o, lse, mixed, wmain, wf, wout))
        x = x_new

    g, loss, d_final_g = _loss_head(x, final_g.reshape(1, D), target)

    d_norm_g, d_bias, d_pool_w, d_pool_scale = [], [], [], []
    for l in reversed(range(L)):
        x_in, proj, h, z, qaug, kaug, ypool, o, lse, mixed, wmain, wf, wout = saved[l]
        d_wout, da, dgate = _outproj_bwd(g, wout, mixed, ypool, o, proj)
        qaug_b, doaug = _attn_bwd_prep(da, o, lse, qaug)
        dq, dk, dv, dqx, dkx = _attn_bwd(proj, da, qaug_b, kaug, doaug)
        dzf, db = _fgate_bwd(dqx, dkx, z, bias[l:l + 1], H)
        dpu, dpw, dps = _pool_bwd(proj, da, pool_w[l], pool_scale[l:l + 1])
        dproj = jnp.concatenate([dpu, dgate[:, :W], dq, dk, dv, dgate[:, W:]], axis=1)
        d_wmain, d_wf = _inproj_bwd_dw(h, dproj, dzf)
        after = tuple(on_grads(l, d_wmain, d_wf[:, :H], d_wout))
        g, dgam = _inproj_bwd_dx(dproj, dzf, wmain, wf, x_in, norm_g[l:l + 1], g, after)
        d_norm_g.append(dgam[0])
        d_bias.append(db[0, :H])
        d_pool_w.append(dpw)
        d_pool_scale.append(dps[0])

    stack = lambda parts: jnp.stack(parts[::-1])
    grads = dict(norm_g=stack(d_norm_g), forget_bias=stack(d_bias), pool_w=stack(d_pool_w),
                 pool_scale=stack(d_pool_scale), final_g=d_final_g[0])
    return loss[0, 0], g, grads


def kernel(x, norm_g, w_in, forget_bias, pool_w, pool_scale, w_out, final_g, loss_target, m_norm_g, m_w_in, m_forget_bias, m_pool_w, m_pool_scale, m_w_out, m_final_g, v_norm_g, v_w_in, v_forget_bias, v_pool_w, v_pool_scale, v_w_out, v_final_g):
    L, D, cols = w_in.shape
    rows_out = w_out.shape[1]
    W = D // 2
    H = W // HEAD_DIM
    me = _index(_position())

    win_b, wout_b = w_in.astype(BF16), w_out.astype(BF16)
    gather_in = (lambda ref, peer: ref, lambda ref, mine, k: ref.at[mine])
    gather_out = (lambda ref, peer: ref, lambda ref, mine, k: ref.at[pl.ds(mine * rows_out, rows_out), :])

    (first_in,) = _all_gather([win_b[0]], [lambda ref, n: ref.at[n]],
                              [jax.ShapeDtypeStruct((N_DEV, D, cols), BF16)], "gather_first")
    rest_srcs = [wout_b[0]] + [w[l] for l in range(1, L) for w in (win_b, wout_b)]
    rest_lands = [jnp.tile(wout_b[0], (N_DEV, 1))]
    for l in range(1, L):
        rest_lands += [jnp.broadcast_to(win_b[l][None], (N_DEV, D, cols)), jnp.tile(wout_b[l], (N_DEV, 1))]
    rest_kinds = [gather_out] + [gather_in, gather_out] * (L - 1)
    rest, rest_token = _split_start(rest_srcs, rest_lands, rest_kinds, "gather_start_rest", (first_in,))

    def weights_in(l, x_in):
        if l == 0:
            win_all = first_in
        else:
            (win_all,) = _split_wait([rest[2 * l - 1]], [gather_in], (x_in,), f"gather_wait_in_{l}")
        w_full = jnp.transpose(win_all, (1, 0, 2)).reshape(D, N_DEV * cols)
        return w_full[:, :6 * W], jnp.pad(w_full[:, 6 * W:], ((0, 0), (0, LANES - H)))

    def weights_out(l, o):
        (wout_full,) = _split_wait([rest[2 * l]], [gather_out], (o,), f"gather_wait_out_{l}")
        return wout_full

    exchange_kinds = [(lambda ref, peer: ref.at[peer], lambda ref, mine, k: ref.at[k - 1]),
                      (lambda ref, peer: ref.at[pl.ds(peer * rows_out, rows_out), :],
                       lambda ref, mine, k: ref.at[k - 1])]
    exchanges, own_parts = {}, {}

    def on_grads(l, d_wmain, d_wf, d_wout):
        dw_in = jnp.concatenate([d_wmain, d_wf], axis=1)
        own_parts[l] = (lax.dynamic_slice_in_dim(dw_in, me * cols, cols, 1)[None],
                        lax.dynamic_slice_in_dim(d_wout, me * rows_out, rows_out, 0)[None])
        srcs = [jnp.transpose(dw_in.astype(BF16).reshape(D, N_DEV, cols), (1, 0, 2)), d_wout.astype(BF16)]
        lands = [lax.empty((N_DEV - 1, D, cols), BF16), lax.empty((N_DEV - 1, rows_out, D), BF16)]
        exchanges[l], token = _split_start(srcs, lands, exchange_kinds, f"exchange_start_{l}")
        return (token,)

    loss, dx, grads = _local_step(x[0], loss_target[0], norm_g, forget_bias, pool_w, pool_scale, final_g,
                                  weights_in, weights_out, on_grads, (rest_token,))
    loss = lax.psum(loss, ("x", "y", "c"))

    names = ["norm_g", "forget_bias", "pool_w", "pool_scale", "final_g"]
    small_w = [norm_g, forget_bias, pool_w, pool_scale, final_g]
    small_m = [m_norm_g, m_forget_bias, m_pool_w, m_pool_scale, m_final_g]
    small_v = [v_norm_g, v_forget_bias, v_pool_w, v_pool_scale, v_final_g]
    gs = _pack([grads[n] for n in names])
    rows = gs.shape[1]
    (gs_all,) = _all_gather([gs], [lambda ref, n: ref.at[n]],
                            [jax.ShapeDtypeStruct((N_DEV, 1, rows, LANES), F32)], "gather_small_grads")
    packed = _adamw(_pack(small_w), _pack(small_m), _pack(small_v),
                    [[gs_all.reshape(N_DEV, rows, LANES)]], "adamw_small", rows)
    g_s, d_s, nm_s, nv_s = [_unpack(p, small_w) for p in packed]

    gin_sets, gout_sets = [], []
    for l in range(L):
        got_in, got_out = _split_wait(exchanges[l], exchange_kinds, (dx, packed[1]), f"exchange_wait_{l}")
        gin_sets.append([own_parts[l][0], got_in])
        gout_sets.append([own_parts[l][1], got_out])
    g_w_in, d_w_in, nm_w_in, nv_w_in = _adamw(w_in, m_w_in, v_w_in, gin_sets, "adamw_w_in", 256)
    g_w_out, d_w_out, nm_w_out, nv_w_out = _adamw(w_out, m_w_out, v_w_out, gout_sets, "adamw_w_out", 128)

    def order(big_in, big_out, small):
        return (small[0], big_in, small[1], small[2], small[3], big_out, small[4])

    return (loss, dx[None], *order(g_w_in, g_w_out, g_s), *order(d_w_in, d_w_out, d_s),
            *order(nm_w_in, nm_w_out, nm_s), *order(nv_w_in, nv_w_out, nv_s))
```

```python
import math

import jax
import jax.numpy as jnp
from jax import lax
from jax.experimental import pallas as pl
from jax.experimental.pallas import tpu as pltpu

F32 = jnp.float32
BF16 = jnp.bfloat16
MESH = pl.DeviceIdType.MESH

RMS_EPS = 1e-6
NEG_INF = -1e30
HEAD_DIM = 64
POOL_WINDOWS = (2, 4, 8, 16)
MAX_WINDOW = 16
LANES = 128
N_DEV = 8

ADAM_LR = 0.001
ADAM_B1 = 0.9
ADAM_B2 = 0.999
ADAM_EPS = 1e-08
ADAM_WD = 0.01
ADAM_STEP = 10

TM = 512
TN = 512
TQ = 512
TB = 256
VMEM_LIMIT = 56 * 1024 * 1024

NT = (((1,), (1,)), ((), ()))
TN_DIMS = (((0,), (0,)), ((), ()))

SLOT_C, SLOT_ONE, SLOT_LSE = 0, 3, 6


def _params(*sem):
    return pltpu.CompilerParams(dimension_semantics=sem, vmem_limit_bytes=VMEM_LIMIT)


def _sigmoid(x):
    return 1.0 / (1.0 + jnp.exp(-x))


def _split3(x):
    hi = x.astype(BF16).astype(F32)
    rest = x - hi
    mid = rest.astype(BF16).astype(F32)
    return hi, mid, rest - mid


def _after_specs(after):
    return [pl.BlockSpec(memory_space=pl.ANY)] * len(after)


def _inproj_fwd(x, gam, wmain, wf, after=()):
    S, D = x.shape
    N = wmain.shape[1]
    tm, tn = min(TM, S), TN

    def body(x_ref, g_ref, w_ref, wf_ref, *rest):
        proj_ref, h_ref, z_ref = rest[-3:]
        xf = x_ref[...]
        r = lax.rsqrt(jnp.mean(xf * xf, axis=-1, keepdims=True) + RMS_EPS)
        h = ((xf * r) * g_ref[...]).astype(BF16)
        h_ref[...] = h
        z_ref[...] = jnp.dot(h, wf_ref[...], preferred_element_type=F32)
        for n in range(N // tn):
            cols = slice(n * tn, (n + 1) * tn)
            proj_ref[:, cols] = jnp.dot(h, w_ref[:, cols], preferred_element_type=F32).astype(BF16)

    return pl.pallas_call(
        body, name="inproj_fwd", grid=(S // tm,),
        in_specs=[pl.BlockSpec((tm, D), lambda i: (i, 0)),
                  pl.BlockSpec((1, D), lambda i: (0, 0)),
                  pl.BlockSpec((D, N), lambda i: (0, 0)),
                  pl.BlockSpec((D, LANES), lambda i: (0, 0))] + _after_specs(after),
        out_specs=[pl.BlockSpec((tm, N), lambda i: (i, 0)),
                   pl.BlockSpec((tm, D), lambda i: (i, 0)),
                   pl.BlockSpec((tm, LANES), lambda i: (i, 0))],
        out_shape=[jax.ShapeDtypeStruct((S, N), BF16),
                   jax.ShapeDtypeStruct((S, D), BF16),
                   jax.ShapeDtypeStruct((S, LANES), F32)],
        compiler_params=_params("parallel"),
    )(x, gam, wmain, wf, *after)


def _fgate_fwd(z, bias, n_heads):
    S = z.shape[0]
    tb = min(TB, S)
    P = n_heads // 2

    def body(z_ref, b_ref, qaug_ref, kaug_ref):
        lane = lax.broadcasted_iota(jnp.int32, (tb, LANES), 1)
        tri = (lax.broadcasted_iota(jnp.int32, (tb, tb), 0)
               >= lax.broadcasted_iota(jnp.int32, (tb, tb), 1)).astype(F32)
        head = lax.broadcasted_iota(jnp.int32, (LANES, P * LANES), 0)
        col = lax.broadcasted_iota(jnp.int32, (LANES, P * LANES), 1)
        home = (head >> 1) * LANES + jnp.where((head & 1) == 0, HEAD_DIM, 0)
        is_head = head < n_heads
        place_q = [jnp.logical_and(is_head, col == home + SLOT_C + n).astype(BF16) for n in range(3)]
        place_k = [jnp.logical_and(is_head, col == home + SLOT_ONE + n).astype(BF16) for n in range(3)]
        slot = lax.broadcasted_iota(jnp.int32, (tb, P * LANES), 1) & (HEAD_DIM - 1)
        q_ones = jnp.logical_and(slot >= SLOT_ONE, slot < SLOT_ONE + 3).astype(F32)
        k_ones = jnp.logical_or(slot < SLOT_C + 3,
                                jnp.logical_and(slot >= SLOT_LSE, slot < SLOT_LSE + 3)).astype(F32)

        local = []
        for b in range(S // tb):
            zz = z_ref[b * tb:(b + 1) * tb, :] + b_ref[...]
            lf = jnp.minimum(zz, 0.0) - jnp.log(1.0 + jnp.exp(-jnp.abs(zz)))
            lf = jnp.where(lane < n_heads, lf, 0.0)
            local.append(jnp.dot(tri, lf, preferred_element_type=F32, precision=lax.Precision.HIGHEST))
        carry = jnp.zeros((1, LANES), F32)
        for b, part_sum in enumerate(local):
            c = part_sum + carry
            carry = c[tb - 1:tb, :]
            qa, ka = q_ones, k_ones
            for n, part in enumerate(_split3(c)):
                qa = qa + jnp.dot(part.astype(BF16), place_q[n], preferred_element_type=F32)
                ka = ka - jnp.dot(part.astype(BF16), place_k[n], preferred_element_type=F32)
            qaug_ref[b * tb:(b + 1) * tb, :] = qa.astype(BF16)
            kaug_ref[b * tb:(b + 1) * tb, :] = ka.astype(BF16)

    return pl.pallas_call(
        body, name="fgate_fwd",
        out_shape=[jax.ShapeDtypeStruct((S, P * LANES), BF16),
                   jax.ShapeDtypeStruct((S, P * LANES), BF16)],
        compiler_params=pltpu.CompilerParams(vmem_limit_bytes=VMEM_LIMIT),
    )(z, bias)


def _window_mean_minus_self(u, pad_ref, w, S):
    pad_ref[0:MAX_WINDOW, :] = jnp.zeros((MAX_WINDOW, LANES), F32)
    pad_ref[MAX_WINDOW:MAX_WINDOW + S, :] = u
    acc = u
    for j in range(1, w):
        acc = acc + pad_ref[MAX_WINDOW - j:MAX_WINDOW - j + S, :]
    t = lax.broadcasted_iota(jnp.int32, (S, LANES), 0)
    cnt = jnp.minimum(t + 1, w).astype(F32)
    return acc / cnt - u, cnt


def _pool_fwd(proj, pool_w, pool_scale):
    S = proj.shape[0]
    G = len(POOL_WINDOWS)

    def body(u_ref, w_ref, s_ref, y_ref, pad_ref):
        g = pl.program_id(0)
        for gi, w in enumerate(POOL_WINDOWS):
            @pl.when(g == gi)
            def _():
                d, _ = _window_mean_minus_self(u_ref[...].astype(F32), pad_ref, w, S)
                y = jnp.dot(d.astype(BF16), w_ref[0].astype(BF16), preferred_element_type=F32)
                y_ref[...] = (y * s_ref[...]).astype(BF16)

    return pl.pallas_call(
        body, name="pool_fwd", grid=(G,),
        in_specs=[pl.BlockSpec((S, LANES), lambda g: (0, g)),
                  pl.BlockSpec((1, LANES, LANES), lambda g: (g, 0, 0)),
                  pl.BlockSpec((1, LANES), lambda g: (0, g))],
        out_specs=pl.BlockSpec((S, LANES), lambda g: (0, g)),
        out_shape=jax.ShapeDtypeStruct((S, G * LANES), BF16),
        scratch_shapes=[pltpu.VMEM((S + MAX_WINDOW, LANES), F32)],
        compiler_params=_params("arbitrary"),
    )(proj, pool_w, pool_scale)


def _head_halves(rows):
    lane = lax.broadcasted_iota(jnp.int32, (rows, LANES), 1)
    return lane, (lane < HEAD_DIM, lane >= HEAD_DIM)


def _attn_fwd(proj, qaug, kaug):
    S = proj.shape[0]
    W = proj.shape[1] // 6
    P = W // LANES
    tq = min(TQ, S)
    nq = S // tq
    qc, kc, vc = 2 * P, 3 * P, 4 * P
    scale = 1.0 / math.sqrt(HEAD_DIM)

    def body(q_ref, k_ref, v_ref, qa_ref, ka_ref, o_ref, qb_ref, qm_scr, m_scr, acc_scr):
        i = pl.program_id(1)
        lane, halves = _head_halves(tq)
        v_ones = ((lane & (HEAD_DIM - 1)) < 3).astype(BF16)
        qs = q_ref[...] * scale
        qm_scr[0] = jnp.where(halves[0], qs, qa_ref[...])
        qm_scr[1] = jnp.where(halves[1], qs, qa_ref[...])
        m_scr[...] = jnp.full(m_scr.shape, NEG_INF, F32)
        acc_scr[...] = jnp.zeros(acc_scr.shape, F32)

        def update(j, on_diagonal):
            keys = pl.ds(pl.multiple_of(j * tq, tq), tq)
            k2, v2, kaug_t = k_ref[keys, :], v_ref[keys, :], ka_ref[keys, :]
            if on_diagonal:
                keep = (lax.broadcasted_iota(jnp.int32, (tq, tq), 0)
                        >= lax.broadcasted_iota(jnp.int32, (tq, tq), 1))
            logits = [lax.dot_general(qm_scr[a], jnp.where(halves[a], k2, kaug_t), NT, preferred_element_type=F32)
                      for a in range(2)]
            for a in range(2):
                s = jnp.where(keep, logits[a], NEG_INF) if on_diagonal else logits[a]
                va = jnp.where(halves[a], v2, v_ones)
                m_prev = m_scr[a]
                m_new = jnp.maximum(m_prev, jnp.max(s, axis=1, keepdims=True))
                p = jnp.exp(s - jnp.tile(m_new, (1, tq // LANES)))
                acc_scr[a] = jnp.exp(m_prev - m_new) * acc_scr[a] + jnp.dot(p.astype(BF16), va,
                                                                              preferred_element_type=F32)
                m_scr[a] = m_new

        def below_diagonal(jj, carry):
            update(2 * jj, False)
            update(2 * jj + 1, False)
            return carry

        lax.fori_loop(0, i // 2, below_diagonal, 0)

        @pl.when(i % 2 == 1)
        def _():
            update(i - 1, False)

        update(i, True)
        acc_a, acc_b = acc_scr[0], acc_scr[1]
        l_a, l_b = acc_a[:, HEAD_DIM:HEAD_DIM + 1], acc_b[:, 0:1]
        o_ref[...] = jnp.where(halves[0], acc_a / l_a, acc_b / l_b).astype(BF16)
        lse = jnp.where(halves[0], m_scr[1] + jnp.log(l_b), m_scr[0] + jnp.log(l_a))
        slot = lane & (HEAD_DIM - 1)
        aug = qa_ref[...].astype(F32)
        for n, part in enumerate(_split3(lse)):
            aug = jnp.where(slot == SLOT_LSE + n, -part, aug)
        qb_ref[...] = aug.astype(BF16)

    tile = lambda col: pl.BlockSpec((tq, LANES), lambda p, i: (i, col + p))
    whole = lambda col: pl.BlockSpec((S, LANES), lambda p, i: (0, col + p))
    return pl.pallas_call(
        body, name="attn_fwd", grid=(P, nq),
        in_specs=[tile(qc), whole(kc), whole(vc), tile(0), whole(0)],
        out_specs=[tile(0), tile(0)],
        out_shape=[jax.ShapeDtypeStruct((S, W), BF16), jax.ShapeDtypeStruct((S, W), BF16)],
        scratch_shapes=[pltpu.VMEM((2, tq, LANES), BF16),
                        pltpu.VMEM((2, tq, LANES), F32),
                        pltpu.VMEM((2, tq, LANES), F32)],
        compiler_params=_params("parallel", "arbitrary"),
    )(proj, proj, proj, qaug, kaug)


def _outproj_fwd(ypool, o, proj, x, wout):
    S, D = x.shape
    W = D // 2
    tm, tn = min(TM, S), TN

    def body(y_ref, o_ref, pg_ref, ag_ref, x_ref, w_ref, xn_ref, mix_ref):
        pg, ag = pg_ref[...].astype(F32), ag_ref[...].astype(F32)
        mix_ref[:, 0:W] = (y_ref[...].astype(F32) * (pg * _sigmoid(pg))).astype(BF16)
        mix_ref[:, W:D] = (o_ref[...].astype(F32) * (ag * _sigmoid(ag))).astype(BF16)
        for n in range(D // tn):
            cols = slice(n * tn, (n + 1) * tn)
            xn_ref[:, cols] = x_ref[:, cols] + jnp.dot(mix_ref[...], w_ref[:, cols], preferred_element_type=F32)

    return pl.pallas_call(
        body, name="outproj_fwd", grid=(S // tm,),
        in_specs=[pl.BlockSpec((tm, W), lambda i: (i, 0)),
                  pl.BlockSpec((tm, W), lambda i: (i, 0)),
                  pl.BlockSpec((tm, W), lambda i: (i, 1)),
                  pl.BlockSpec((tm, W), lambda i: (i, 5)),
                  pl.BlockSpec((tm, D), lambda i: (i, 0)),
                  pl.BlockSpec((D, D), lambda i: (0, 0))],
        out_specs=[pl.BlockSpec((tm, D), lambda i: (i, 0)),
                   pl.BlockSpec((tm, D), lambda i: (i, 0))],
        out_shape=[jax.ShapeDtypeStruct((S, D), F32),
                   jax.ShapeDtypeStruct((S, D), BF16)],
        compiler_params=_params("parallel"),
    )(ypool, o, proj, proj, x, wout)


def _loss_head(x, gam, target):
    S, D = x.shape
    tm = min(TM, S)

    def body(x_ref, g_ref, t_ref, dx_ref, loss_ref, dg_ref):
        @pl.when(pl.program_id(0) == 0)
        def _():
            loss_ref[...] = jnp.zeros(loss_ref.shape, F32)
            dg_ref[...] = jnp.zeros(dg_ref.shape, F32)

        xf, gam_v = x_ref[...], g_ref[...]
        r = lax.rsqrt(jnp.mean(xf * xf, axis=-1, keepdims=True) + RMS_EPS)
        xhat = xf * r
        err = xhat * gam_v - t_ref[...]
        part = jnp.sum(jnp.sum(err * err, axis=-1, keepdims=True), axis=0, keepdims=True)
        loss_ref[...] += part * (0.5 / D)
        dy = err * (1.0 / D)
        dg_ref[...] += jnp.sum(dy * xhat, axis=0, keepdims=True)
        dxhat = dy * gam_v
        dx_ref[...] = r * (dxhat - xhat * jnp.mean(dxhat * xhat, axis=-1, keepdims=True))

    return pl.pallas_call(
        body, name="loss_head", grid=(S // tm,),
        in_specs=[pl.BlockSpec((tm, D), lambda i: (i, 0)),
                  pl.BlockSpec((1, D), lambda i: (0, 0)),
                  pl.BlockSpec((tm, D), lambda i: (i, 0))],
        out_specs=[pl.BlockSpec((tm, D), lambda i: (i, 0)),
                   pl.BlockSpec((8, LANES), lambda i: (0, 0)),
                   pl.BlockSpec((1, D), lambda i: (0, 0))],
        out_shape=[jax.ShapeDtypeStruct((S, D), F32),
                   jax.ShapeDtypeStruct((8, LANES), F32),
                   jax.ShapeDtypeStruct((1, D), F32)],
        compiler_params=_params("arbitrary"),
    )(x, gam, target)


def _outproj_bwd(g, wout, mixed, ypool, o, proj):
    S, D = g.shape
    W = D // 2
    tm = min(TM, S)

    def body(g_ref, w_ref, mix_ref, y_ref, o_ref, pg_ref, ag_ref, dw_ref, da_ref, dgate_ref, doaug_ref):
        @pl.when(pl.program_id(0) == 0)
        def _():
            dw_ref[...] = jnp.zeros(dw_ref.shape, F32)

        gb = g_ref[...].astype(BF16)
        dw_ref[...] += lax.dot_general(mix_ref[...], gb, TN_DIMS, preferred_element_type=F32)
        for half, (val_ref, gate_ref) in enumerate(((y_ref, pg_ref), (o_ref, ag_ref))):
            cols = slice(half * W, (half + 1) * W)
            dmix = lax.dot_general(gb, w_ref[cols, :], NT, preferred_element_type=F32)
            gt = gate_ref[...].astype(F32)
            sg = _sigmoid(gt)
            da_ref[:, cols] = (dmix * (gt * sg)).astype(BF16)
            dgate_ref[:, cols] = (dmix * val_ref[...].astype(F32) * (sg * (1.0 + gt * (1.0 - sg)))).astype(BF16)

        lane, halves = _head_halves(tm)
        slot = lane & (HEAD_DIM - 1)
        for p in range(W // LANES):
            cols = slice(p * LANES, (p + 1) * LANES)
            prod = da_ref[:, W + p * LANES:W + (p + 1) * LANES].astype(F32) * o_ref[:, cols].astype(F32)
            d_a = jnp.sum(jnp.where(halves[0], prod, 0.0), axis=1, keepdims=True)
            d_b = jnp.sum(jnp.where(halves[1], prod, 0.0), axis=1, keepdims=True)
            aug = jnp.zeros((tm, LANES), F32)
            for n, part in enumerate(_split3(jnp.where(halves[0], d_b, d_a))):
                aug = jnp.where(slot == SLOT_C + n, -part, aug)
            doaug_ref[:, cols] = aug.astype(BF16)

    rows = lambda width, col: pl.BlockSpec((tm, width), lambda i: (i, col))
    return pl.pallas_call(
        body, name="outproj_bwd", grid=(S // tm,),
        in_specs=[rows(D, 0), pl.BlockSpec((D, D), lambda i: (0, 0)), rows(D, 0), rows(W, 0), rows(W, 0),
                  rows(W, 1), rows(W, 5)],
        out_specs=[pl.BlockSpec((D, D), lambda i: (0, 0)), rows(D, 0), rows(D, 0), rows(W, 0)],
        out_shape=[jax.ShapeDtypeStruct((D, D), F32),
                   jax.ShapeDtypeStruct((S, D), BF16),
                   jax.ShapeDtypeStruct((S, D), BF16),
                   jax.ShapeDtypeStruct((S, W), BF16)],
        compiler_params=_params("arbitrary"),
    )(g, wout, mixed, ypool, o, proj, proj)


def _inproj_bwd_dw(h, dproj, dzf):
    S, D = h.shape
    N = dproj.shape[1]
    ts, tn = min(TM, S), TN

    def body(h_ref, dp_ref, dz_ref, dw_ref, dwf_ref):
        @pl.when(pl.program_id(0) == 0)
        def _():
            dw_ref[...] = jnp.zeros(dw_ref.shape, F32)
            dwf_ref[...] = jnp.zeros(dwf_ref.shape, F32)

        ht = h_ref[...].T
        dwf_ref[...] += jnp.dot(ht, dz_ref[...], preferred_element_type=F32)
        for n in range(N // tn):
            cols = slice(n * tn, (n + 1) * tn)
            dw_ref[:, cols] += jnp.dot(ht, dp_ref[:, cols], preferred_element_type=F32)

    return pl.pallas_call(
        body, name="inproj_bwd_dw", grid=(S // ts,),
        in_specs=[pl.BlockSpec((ts, D), lambda k: (k, 0)),
                  pl.BlockSpec((ts, N), lambda k: (k, 0)),
                  pl.BlockSpec((ts, LANES), lambda k: (k, 0))],
        out_specs=[pl.BlockSpec((D, N), lambda k: (0, 0)),
                   pl.BlockSpec((D, LANES), lambda k: (0, 0))],
        out_shape=[jax.ShapeDtypeStruct((D, N), F32),
                   jax.ShapeDtypeStruct((D, LANES), F32)],
        compiler_params=_params("arbitrary"),
    )(h, dproj, dzf)


def _attn_bwd(proj, da, qaug, kaug, doaug):
    S = proj.shape[0]
    W = proj.shape[1] // 6
    P = W // LANES
    tq = min(TQ, S)
    nq = S // tq
    qc, kc, vc = 2 * P, 3 * P, 4 * P
    scale = 1.0 / math.sqrt(HEAD_DIM)

    def body(q_ref, k_ref, v_ref, do_ref, qa_ref, ka_ref, da_ref,
             dq_ref, dk_ref, dv_ref, dqx_ref, dkx_ref, km_scr, vm_scr, dk_scr, dv_scr, dq_scr):
        j = pl.program_id(1)
        lane, halves = _head_halves(tq)

        @pl.when(j == 0)
        def _():
            dq_scr[...] = jnp.zeros(dq_scr.shape, F32)

        v_ones = ((lane & (HEAD_DIM - 1)) < 3).astype(BF16)
        for a in range(2):
            km_scr[a] = jnp.where(halves[a], k_ref[...], ka_ref[...])
            vm_scr[a] = jnp.where(halves[a], v_ref[...], v_ones)
        dk_scr[...] = jnp.zeros(dk_scr.shape, F32)
        dv_scr[...] = jnp.zeros(dv_scr.shape, F32)

        def update(i, on_diagonal):
            rows = pl.ds(pl.multiple_of(i * tq, tq), tq)
            qs = q_ref[rows, :] * scale
            do2, qaug_t, doaug_t = do_ref[rows, :], qa_ref[rows, :], da_ref[rows, :]
            if on_diagonal:
                keep = (lax.broadcasted_iota(jnp.int32, (tq, tq), 0)
                        >= lax.broadcasted_iota(jnp.int32, (tq, tq), 1))
            qas = [jnp.where(halves[a], qs, qaug_t) for a in range(2)]
            logits = [lax.dot_general(qas[a], km_scr[a], NT, preferred_element_type=F32) for a in range(2)]
            dps = [lax.dot_general(jnp.where(halves[a], do2, doaug_t), vm_scr[a], NT, preferred_element_type=F32)
                   for a in range(2)]
            dv = None
            for a in range(2):
                s = jnp.where(keep, logits[a], NEG_INF) if on_diagonal else logits[a]
                p = jnp.exp(s)
                dsb = (p * dps[a]).astype(BF16)
                do0 = jnp.where(halves[a], do2, jnp.zeros_like(do2))
                dv_a = lax.dot_general(p.astype(BF16), do0, TN_DIMS, preferred_element_type=F32)
                dv = dv_a if dv is None else dv + dv_a
                dk_scr[a] += lax.dot_general(dsb, qas[a], TN_DIMS, preferred_element_type=F32)
                dq_scr[a, rows, :] += jnp.dot(dsb, km_scr[a], preferred_element_type=F32)
            dv_scr[...] += dv

        def below_diagonal(n, carry):
            update(j + 1 + 2 * n, False)
            update(j + 2 + 2 * n, False)
            return carry

        update(j, True)
        below = nq - 1 - j
        lax.fori_loop(0, below // 2, below_diagonal, 0)

        @pl.when(below % 2 == 1)
        def _():
            update(nq - 1, False)


        dk_ref[...] = jnp.where(halves[0], dk_scr[0], dk_scr[1]).astype(BF16)
        dkx_ref[...] = jnp.where(halves[0], dk_scr[1], dk_scr[0])
        dv_ref[...] = dv_scr[...].astype(BF16)

        @pl.when(j == nq - 1)
        def _():
            row_lane, row_halves = _head_halves(S)
            dq_ref[...] = (jnp.where(row_halves[0], dq_scr[0], dq_scr[1]) * scale).astype(BF16)
            dqx_ref[...] = jnp.where(row_halves[0], dq_scr[1], dq_scr[0])

    tile = lambda col: pl.BlockSpec((tq, LANES), lambda p, j: (j, col + p))
    whole = lambda col: pl.BlockSpec((S, LANES), lambda p, j: (0, col + p))
    return pl.pallas_call(
        body, name="attn_bwd", grid=(P, nq),
        in_specs=[whole(qc), tile(kc), tile(vc), whole(P), whole(0), tile(0), whole(0)],
        out_specs=[whole(0), tile(0), tile(0), whole(0), tile(0)],
        out_shape=[jax.ShapeDtypeStruct((S, W), BF16),
                   jax.ShapeDtypeStruct((S, W), BF16),
                   jax.ShapeDtypeStruct((S, W), BF16),
                   jax.ShapeDtypeStruct((S, W), F32),
                   jax.ShapeDtypeStruct((S, W), F32)],
        scratch_shapes=[pltpu.VMEM((2, tq, LANES), BF16),
                        pltpu.VMEM((2, tq, LANES), BF16),
                        pltpu.VMEM((2, tq, LANES), F32),
                        pltpu.VMEM((tq, LANES), F32),
                        pltpu.VMEM((2, S, LANES), F32)],
        compiler_params=_params("parallel", "arbitrary"),
    )(proj, proj, proj, da, qaug, kaug, doaug)


def _fgate_bwd(dqx, dkx, z, bias, n_heads):
    S = z.shape[0]
    tb = min(TB, S)
    nb = S // tb

    def body(dqx_ref, dkx_ref, z_ref, b_ref, dz_ref, db_ref):
        tri = (lax.broadcasted_iota(jnp.int32, (tb, tb), 1)
               >= lax.broadcasted_iota(jnp.int32, (tb, tb), 0)).astype(F32)
        n_cols = dqx_ref.shape[1]
        col = lax.broadcasted_iota(jnp.int32, (n_cols, LANES), 0)
        head = lax.broadcasted_iota(jnp.int32, (n_cols, LANES), 1)
        home = (head >> 1) * LANES + jnp.where((head & 1) == 0, HEAD_DIM, 0)
        is_head = head < n_heads
        pick_rows = jnp.logical_and(is_head, col == home + SLOT_C).astype(F32)
        pick_cols = jnp.logical_and(is_head, col == home + SLOT_ONE).astype(F32)

        local = []
        for b in range(nb):
            rows = slice(b * tb, (b + 1) * tb)
            dc = (jnp.dot(dqx_ref[rows, :], pick_rows, preferred_element_type=F32, precision=lax.Precision.HIGHEST)
                  - jnp.dot(dkx_ref[rows, :], pick_cols, preferred_element_type=F32, precision=lax.Precision.HIGHEST))
            local.append(jnp.dot(tri, dc, preferred_element_type=F32, precision=lax.Precision.HIGHEST))
        carry = jnp.zeros((1, LANES), F32)
        db = jnp.zeros((1, LANES), F32)
        for b in reversed(range(nb)):
            rows = slice(b * tb, (b + 1) * tb)
            rc = local[b] + carry
            carry = rc[0:1, :]
            dz = rc * _sigmoid(-(z_ref[rows, :] + b_ref[...]))
            dz_ref[rows, :] = dz.astype(BF16)
            db = db + jnp.sum(dz, axis=0, keepdims=True)
        db_ref[...] = db

    return pl.pallas_call(
        body, name="fgate_bwd",
        out_shape=[jax.ShapeDtypeStruct((S, LANES), BF16),
                   jax.ShapeDtypeStruct((1, LANES), F32)],
        compiler_params=pltpu.CompilerParams(vmem_limit_bytes=VMEM_LIMIT),
    )(dqx, dkx, z, bias)


def _pool_bwd(proj, da, pool_w, pool_scale):
    S = proj.shape[0]
    G = len(POOL_WINDOWS)

    def body(u_ref, dy_ref, w_ref, s_ref, du_ref, dw_ref, ds_ref, pad_ref):
        g = pl.program_id(0)
        for gi, w in enumerate(POOL_WINDOWS):
            @pl.when(g == gi)
            def _():
                d, cnt = _window_mean_minus_self(u_ref[...].astype(F32), pad_ref, w, S)
                db = d.astype(BF16)
                wb = w_ref[0].astype(BF16)
                yraw = jnp.dot(db, wb, preferred_element_type=F32)
                dy = dy_ref[...].astype(F32)
                ds_ref[...] = jnp.sum(dy * yraw, axis=0, keepdims=True)
                dzb = (dy * s_ref[...]).astype(BF16)
                dw_ref[0] = lax.dot_general(db, dzb, TN_DIMS, preferred_element_type=F32)
                dd = lax.dot_general(dzb, wb, NT, preferred_element_type=F32)
                pad_ref[0:S, :] = dd / cnt
                pad_ref[S:S + MAX_WINDOW, :] = jnp.zeros((MAX_WINDOW, LANES), F32)
                acc = -dd
                for j in range(w):
                    acc = acc + pad_ref[j:j + S, :]
                du_ref[...] = acc.astype(BF16)

    return pl.pallas_call(
        body, name="pool_bwd", grid=(G,),
        in_specs=[pl.BlockSpec((S, LANES), lambda g: (0, g)),
                  pl.BlockSpec((S, LANES), lambda g: (0, g)),
                  pl.BlockSpec((1, LANES, LANES), lambda g: (g, 0, 0)),
                  pl.BlockSpec((1, LANES), lambda g: (0, g))],
        out_specs=[pl.BlockSpec((S, LANES), lambda g: (0, g)),
                   pl.BlockSpec((1, LANES, LANES), lambda g: (g, 0, 0)),
                   pl.BlockSpec((1, LANES), lambda g: (0, g))],
        out_shape=[jax.ShapeDtypeStruct((S, G * LANES), BF16),
                   jax.ShapeDtypeStruct((G, LANES, LANES), F32),
                   jax.ShapeDtypeStruct((1, G * LANES), F32)],
        scratch_shapes=[pltpu.VMEM((S + MAX_WINDOW, LANES), F32)],
        compiler_params=_params("arbitrary"),
    )(proj, da, pool_w, pool_scale)


def _inproj_bwd_dx(dproj, dzf, wmain, wf, x, gam, g, after=()):
    S, D = x.shape
    N = wmain.shape[1]
    tm = min(TM // 2, S)

    def body(dp_ref, dz_ref, w_ref, wf_ref, x_ref, gam_ref, g_ref, *rest):
        dx_ref, dg_ref = rest[-2:]

        @pl.when(pl.program_id(0) == 0)
        def _():
            dg_ref[...] = jnp.zeros(dg_ref.shape, F32)

        dh = lax.dot_general(dp_ref[...], w_ref[...], NT, preferred_element_type=F32)
        dh = dh + lax.dot_general(dz_ref[...], wf_ref[...], NT, preferred_element_type=F32)
        xf = x_ref[...]
        r = lax.rsqrt(jnp.mean(xf * xf, axis=-1, keepdims=True) + RMS_EPS)
        xhat = xf * r
        dg_ref[...] += jnp.sum(dh * xhat, axis=0, keepdims=True)
        dxhat = dh * gam_ref[...]
        dx_ref[...] = g_ref[...] + r * (dxhat - xhat * jnp.mean(dxhat * xhat, axis=-1, keepdims=True))

    return pl.pallas_call(
        body, name="inproj_bwd_dx", grid=(S // tm,),
        in_specs=[pl.BlockSpec((tm, N), lambda i: (i, 0)),
                  pl.BlockSpec((tm, LANES), lambda i: (i, 0)),
                  pl.BlockSpec((D, N), lambda i: (0, 0)),
                  pl.BlockSpec((D, LANES), lambda i: (0, 0)),
                  pl.BlockSpec((tm, D), lambda i: (i, 0)),
                  pl.BlockSpec((1, D), lambda i: (0, 0)),
                  pl.BlockSpec((tm, D), lambda i: (i, 0))] + _after_specs(after),
        out_specs=[pl.BlockSpec((tm, D), lambda i: (i, 0)),
                   pl.BlockSpec((1, D), lambda i: (0, 0))],
        out_shape=[jax.ShapeDtypeStruct((S, D), F32),
                   jax.ShapeDtypeStruct((1, D), F32)],
        compiler_params=_params("arbitrary"),
    )(dproj, dzf, wmain, wf, x, gam, g, *after)


def _adamw(w, m, v, gsets, name, rows):
    A, R, C = w.shape
    tr = min(rows, R)
    c1 = 1.0 / (1.0 - ADAM_B1 ** ADAM_STEP)
    c2 = 1.0 / (1.0 - ADAM_B2 ** ADAM_STEP)
    counts = [len(gs) for gs in gsets]

    def body(w_ref, m_ref, v_ref, *rest):
        g_ref, d_ref, nm_ref, nv_ref = rest[-4:]
        at = 0
        for a in range(A):
            part_refs = rest[at:at + counts[a]]
            at += counts[a]

            @pl.when(pl.program_id(0) == a)
            def _():
                g = None
                for ref in part_refs:
                    for s in range(ref.shape[0]):
                        term = ref[s].astype(F32)
                        g = term if g is None else g + term
                nm = ADAM_B1 * m_ref[0] + (1.0 - ADAM_B1) * g
                nv = ADAM_B2 * v_ref[0] + (1.0 - ADAM_B2) * (g * g)
                g_ref[0] = g
                nm_ref[0] = nm
                nv_ref[0] = nv
                d_ref[0] = -ADAM_LR * ((nm * c1) / (jnp.sqrt(nv * c2) + ADAM_EPS) + ADAM_WD * w_ref[0])

    spec = pl.BlockSpec((1, tr, C), lambda a, r: (a, r, 0))
    part_specs = [pl.BlockSpec((part.shape[0], tr, C), lambda a, r, l=l: (0, jnp.where(a == l, r, 0), 0))
                  for l, gs in enumerate(gsets) for part in gs]
    shape = jax.ShapeDtypeStruct((A, R, C), F32)
    return pl.pallas_call(
        body, name=name, grid=(A, R // tr),
        in_specs=[spec, spec, spec] + part_specs,
        out_specs=[spec, spec, spec, spec],
        out_shape=[shape, shape, shape, shape],
        compiler_params=_params("arbitrary", "arbitrary"),
    )(w, m, v, *[part for gs in gsets for part in gs])


def _position():
    return lax.axis_index("x"), lax.axis_index("y"), lax.axis_index("c")


def _index(dev):
    return 4 * dev[0] + 2 * dev[1] + dev[2]


def _all_gather(arrs, slots, out_shapes, name):
    n_arr = len(arrs)

    def body(*refs):
        ins, outs = refs[:n_arr], refs[n_arr:2 * n_arr]
        send_sems, recv_sems, local_sems = refs[2 * n_arr:]
        x, y, c = _position()
        me, sibling = (x, y, c), (x, y, 1 - c)
        chips = [(1 - x, y), (x, 1 - y), (1 - x, 1 - y)]

        def copy(a, k, block, to, src=None):
            part = slots[a](outs[a], _index(block))
            return pltpu.make_async_remote_copy(
                src_ref=part if src is None else src, dst_ref=part,
                send_sem=send_sems.at[a, k], recv_sem=recv_sems.at[a, k],
                device_id=to, device_id_type=MESH)

        mine = [pltpu.make_async_copy(ins[a], slots[a](outs[a], _index(me)), local_sems.at[a])
                for a in range(n_arr)]
        for cp in mine:
            cp.start()
        first = []
        for a in range(n_arr):
            first.append(copy(a, 0, me, sibling, src=ins[a]))
            first += [copy(a, 1 + j, me, (*chip, c), src=ins[a]) for j, chip in enumerate(chips)]
        for cp in first:
            cp.start()
        passed = []
        for j, chip in enumerate(chips):
            for a in range(n_arr):
                copy(a, 1 + j, (*chip, c), me).wait_recv()
                fwd = copy(a, 4 + j, (*chip, c), sibling)
                fwd.start()
                passed.append(fwd)
        for a in range(n_arr):
            copy(a, 0, sibling, me).wait_recv()
            for j, chip in enumerate(chips):
                copy(a, 4 + j, (*chip, 1 - c), me).wait_recv()
        for cp in first + passed:
            cp.wait_send()
        for cp in mine:
            cp.wait()

    any_spec = pl.BlockSpec(memory_space=pl.ANY)
    return pl.pallas_call(
        body, name=name,
        in_specs=[any_spec] * n_arr, out_specs=[any_spec] * n_arr, out_shape=out_shapes,
        scratch_shapes=[pltpu.SemaphoreType.DMA((n_arr, 7)), pltpu.SemaphoreType.DMA((n_arr, 7)),
                        pltpu.SemaphoreType.DMA((n_arr,))],
    )(*arrs)


def _split_copies(srcs, lands, send_sems, recv_sems, kinds):
    x, y, c = _position()
    me = _index((x, y, c))
    copies = []
    for a, (src_part, land_part) in enumerate(kinds):
        for k in range(1, N_DEV):
            peer = (x ^ ((k >> 2) & 1), y ^ ((k >> 1) & 1), c ^ (k & 1))
            copies.append(pltpu.make_async_remote_copy(
                src_ref=src_part(srcs[a], _index(peer)), dst_ref=land_part(lands[a], me, k),
                send_sem=send_sems[a].at[k - 1], recv_sem=recv_sems[a].at[k - 1],
                device_id=peer, device_id_type=MESH))
    return copies


def _split_start(srcs, lands, kinds, name, after=()):
    n = len(srcs)

    def body(*refs):
        src_refs, land_refs = refs[:n], refs[n:2 * n]
        outs = refs[2 * n + len(after):]
        send_sems, recv_sems = outs[:n], outs[n:2 * n]
        token = outs[-1]
        for cp in _split_copies(src_refs, land_refs, send_sems, recv_sems, kinds):
            cp.start()
        token[...] = jnp.zeros(token.shape, token.dtype)

    hbm = pl.BlockSpec(memory_space=pltpu.HBM)
    sem = pl.BlockSpec(memory_space=pltpu.SEMAPHORE)
    operands = [pltpu.with_memory_space_constraint(t, pltpu.HBM) for t in (*srcs, *lands)]
    out = pl.pallas_call(
        body, name=name,
        in_specs=[hbm] * (2 * n) + _after_specs(after),
        out_specs=[sem] * (2 * n) + [hbm] * (2 * n) + [pl.BlockSpec(memory_space=pltpu.VMEM)],
        out_shape=[pltpu.SemaphoreType.DMA((N_DEV - 1,))] * (2 * n)
        + [pltpu.HBM(t.shape, t.dtype) for t in operands] + [jax.ShapeDtypeStruct((8, LANES), F32)],
        input_output_aliases={i: 2 * n + i for i in range(2 * n)},
        compiler_params=pltpu.CompilerParams(has_side_effects=pltpu.SideEffectType.DATAFLOW_SIDE_EFFECTING),
    )(*operands, *after)
    return [(out[a], out[n + a], out[2 * n + a], out[3 * n + a]) for a in range(n)], out[-1]


def _split_wait(started, kinds, after, name):
    n = len(started)
    sems = [t[0] for t in started] + [t[1] for t in started]
    srcs = [t[2] for t in started]
    lands = [t[3] for t in started]

    def body(*refs):
        src_refs, land_refs = refs[:n], refs[n:2 * n]
        send_sems, recv_sems = refs[2 * n:3 * n], refs[3 * n:4 * n]
        for cp in _split_copies(src_refs, land_refs, send_sems, recv_sems, kinds):
            cp.wait_send()
            cp.wait_recv()

    hbm = pl.BlockSpec(memory_space=pltpu.HBM)
    sem = pl.BlockSpec(memory_space=pltpu.SEMAPHORE)
    out = pl.pallas_call(
        body, name=name,
        in_specs=[hbm] * (2 * n) + [sem] * (2 * n) + _after_specs(after),
        out_specs=[hbm] * (2 * n),
        out_shape=[pltpu.HBM(t.shape, t.dtype) for t in (*srcs, *lands)],
        input_output_aliases={i: i for i in range(2 * n)},
        compiler_params=pltpu.CompilerParams(has_side_effects=pltpu.SideEffectType.DATAFLOW_SIDE_EFFECTING),
    )(*srcs, *lands, *sems, *after)
    return out[n:]


def _as_rows(p):
    if p.size % LANES == 0:
        rows = p.reshape(-1, LANES)
    else:
        rows = p.reshape(-1, p.shape[-1])
        rows = jnp.pad(rows, ((0, 0), (0, LANES - rows.shape[1])))
    return jnp.pad(rows, ((0, -rows.shape[0] % 8), (0, 0)))


def _pack(parts):
    return jnp.concatenate([_as_rows(p) for p in parts])[None]


def _unpack(packed, like):
    out, at = [], 0
    for p in like:
        whole = p.size % LANES == 0
        n = p.size // LANES if whole else p.size // p.shape[-1]
        rows = packed[0, at:at + n]
        out.append((rows if whole else rows[:, :p.shape[-1]]).reshape(p.shape))
        at += n + (-n % 8)
    return out


def _local_step(x, target, norm_g, forget_bias, pool_w, pool_scale, final_g, weights_in, weights_out, on_grads,
                first_after=()):
    L = norm_g.shape[0]
    S, D = x.shape
    W = D // 2
    H = W // HEAD_DIM
    bias = jnp.pad(forget_bias, ((0, 0), (0, LANES - H)))

    saved = []
    after = tuple(first_after)
    for l in range(L):
        wmain, wf = weights_in(l, x)
        proj, h, z = _inproj_fwd(x, norm_g[l:l + 1], wmain, wf, after)
        after = ()
        qaug, kaug = _fgate_fwd(z, bias[l:l + 1], H)
        ypool = _pool_fwd(proj, pool_w[l], pool_scale[l:l + 1])
        o, qaug_b = _attn_fwd(proj, qaug, kaug)
        wout = weights_out(l, o)
        x_new, mixed = _outproj_fwd(ypool, o, proj, x, wout)
        saved.append((x, proj, h, z, qaug_b, kaug, ypool, o, mixed, wmain, wf, wout))
        x = x_new

    g, loss, d_final_g = _loss_head(x, final_g.reshape(1, D), target)

    d_norm_g, d_bias, d_pool_w, d_pool_scale = [], [], [], []
    for l in reversed(range(L)):
        x_in, proj, h, z, qaug_b, kaug, ypool, o, mixed, wmain, wf, wout = saved[l]
        d_wout, da, dgate, doaug = _outproj_bwd(g, wout, mixed, ypool, o, proj)
        dq, dk, dv, dqx, dkx = _attn_bwd(proj, da, qaug_b, kaug, doaug)
        dzf, db = _fgate_bwd(dqx, dkx, z, bias[l:l + 1], H)
        dpu, dpw, dps = _pool_bwd(proj, da, pool_w[l], pool_scale[l:l + 1])
        dproj = jnp.concatenate([dpu, dgate[:, :W], dq, dk, dv, dgate[:, W:]], axis=1)
        d_wmain, d_wf = _inproj_bwd_dw(h, dproj, dzf)
        after = tuple(on_grads(l, d_wmain, d_wf[:, :H], d_wout))
        g, dgam = _inproj_bwd_dx(dproj, dzf, wmain, wf, x_in, norm_g[l:l + 1], g, after)
        d_norm_g.append(dgam[0])
        d_bias.append(db[0, :H])
        d_pool_w.append(dpw)
        d_pool_scale.append(dps[0])

    stack = lambda parts: jnp.stack(parts[::-1])
    grads = dict(norm_g=stack(d_norm_g), forget_bias=stack(d_bias), pool_w=stack(d_pool_w),
                 pool_scale=stack(d_pool_scale), final_g=d_final_g[0])
    return loss[0, 0], g, grads


def kernel(x, norm_g, w_in, forget_bias, pool_w, pool_scale, w_out, final_g, loss_target, m_norm_g, m_w_in, m_forget_bias, m_pool_w, m_pool_scale, m_w_out, m_final_g, v_norm_g, v_w_in, v_forget_bias, v_pool_w, v_pool_scale, v_w_out, v_final_g):
    L, D, cols = w_in.shape
    rows_out = w_out.shape[1]
    W = D // 2
    H = W // HEAD_DIM
    me = _index(_position())

    win_b, wout_b = w_in.astype(BF16), w_out.astype(BF16)
    gather_in = (lambda ref, peer: ref, lambda ref, mine, k: ref.at[mine])
    gather_out = (lambda ref, peer: ref, lambda ref, mine, k: ref.at[pl.ds(mine * rows_out, rows_out), :])

    (first_in,) = _all_gather([win_b[0]], [lambda ref, n: ref.at[n]],
                              [jax.ShapeDtypeStruct((N_DEV, D, cols), BF16)], "gather_first")
    rest_srcs = [wout_b[0]] + [w[l] for l in range(1, L) for w in (win_b, wout_b)]
    rest_lands = [jnp.tile(wout_b[0], (N_DEV, 1))]
    for l in range(1, L):
        rest_lands += [jnp.broadcast_to(win_b[l][None], (N_DEV, D, cols)), jnp.tile(wout_b[l], (N_DEV, 1))]
    rest_kinds = [gather_out] + [gather_in, gather_out] * (L - 1)
    rest, rest_token = _split_start(rest_srcs, rest_lands, rest_kinds, "gather_start_rest", (first_in,))

    def weights_in(l, x_in):
        if l == 0:
            win_all = first_in
        else:
            (win_all,) = _split_wait([rest[2 * l - 1]], [gather_in], (x_in,), f"gather_wait_in_{l}")
        w_full = jnp.transpose(win_all, (1, 0, 2)).reshape(D, N_DEV * cols)
        return w_full[:, :6 * W], jnp.pad(w_full[:, 6 * W:], ((0, 0), (0, LANES - H)))

    def weights_out(l, o):
        (wout_full,) = _split_wait([rest[2 * l]], [gather_out], (o,), f"gather_wait_out_{l}")
        return wout_full

    exchange_kinds = [(lambda ref, peer: ref.at[peer], lambda ref, mine, k: ref.at[k - 1]),
                      (lambda ref, peer: ref.at[pl.ds(peer * rows_out, rows_out), :],
                       lambda ref, mine, k: ref.at[k - 1])]
    exchanges, own_parts = {}, {}

    def on_grads(l, d_wmain, d_wf, d_wout):
        dw_in = jnp.concatenate([d_wmain, d_wf], axis=1)
        own_parts[l] = (lax.dynamic_slice_in_dim(dw_in, me * cols, cols, 1)[None],
                        lax.dynamic_slice_in_dim(d_wout, me * rows_out, rows_out, 0)[None])
        srcs = [jnp.transpose(dw_in.astype(BF16).reshape(D, N_DEV, cols), (1, 0, 2)), d_wout.astype(BF16)]
        lands = [lax.empty((N_DEV - 1, D, cols), BF16), lax.empty((N_DEV - 1, rows_out, D), BF16)]
        exchanges[l], token = _split_start(srcs, lands, exchange_kinds, f"exchange_start_{l}")
        return (token,)

    loss, dx, grads = _local_step(x[0], loss_target[0], norm_g, forget_bias, pool_w, pool_scale, final_g,
                                  weights_in, weights_out, on_grads, (rest_token,))
    loss = lax.psum(loss, ("x", "y", "c"))

    names = ["norm_g", "forget_bias", "pool_w", "pool_scale", "final_g"]
    small_w = [norm_g, forget_bias, pool_w, pool_scale, final_g]
    small_m = [m_norm_g, m_forget_bias, m_pool_w, m_pool_scale, m_final_g]
    small_v = [v_norm_g, v_forget_bias, v_pool_w, v_pool_scale, v_final_g]
    gs = _pack([grads[n] for n in names])
    rows = gs.shape[1]
    (gs_all,) = _all_gather([gs], [lambda ref, n: ref.at[n]],
                            [jax.ShapeDtypeStruct((N_DEV, 1, rows, LANES), F32)], "gather_small_grads")
    packed = _adamw(_pack(small_w), _pack(small_m), _pack(small_v),
                    [[gs_all.reshape(N_DEV, rows, LANES)]], "adamw_small", rows)
    g_s, d_s, nm_s, nv_s = [_unpack(p, small_w) for p in packed]

    gin_sets, gout_sets = [], []
    for l in range(L):
        got_in, got_out = _split_wait(exchanges[l], exchange_kinds, (dx, packed[1]), f"exchange_wait_{l}")
        gin_sets.append([own_parts[l][0], got_in])
        gout_sets.append([own_parts[l][1], got_out])
    g_w_in, d_w_in, nm_w_in, nv_w_in = _adamw(w_in, m_w_in, v_w_in, gin_sets, "adamw_w_in", 256)
    g_w_out, d_w_out, nm_w_out, nv_w_out = _adamw(w_out, m_w_out, v_w_out, gout_sets, "adamw_w_out", 128)

    def order(big_in, big_out, small):
        return (small[0], big_in, small[1], small[2], small[3], big_out, small[4])

    return (loss, dx[None], *order(g_w_in, g_w_out, g_s), *order(d_w_in, d_w_out, d_s),
            *order(nm_w_in, nm_w_out, nm_s), *order(nv_w_in, nv_w_out, nv_s))
```

```python
import math

import jax
import jax.numpy as jnp
from jax import lax
from jax.experimental import pallas as pl
from jax.experimental.pallas import tpu as pltpu

F32 = jnp.float32
BF16 = jnp.bfloat16
MESH = pl.DeviceIdType.MESH

RMS_EPS = 1e-6
NEG_INF = -1e30
HEAD_DIM = 64
POOL_WINDOWS = (2, 4, 8, 16)
MAX_WINDOW = 16
LANES = 128
N_DEV = 8

ADAM_LR = 0.001
ADAM_B1 = 0.9
ADAM_B2 = 0.999
ADAM_EPS = 1e-08
ADAM_WD = 0.01
ADAM_STEP = 10

TM = 512
TN = 512
TQ = 512
TB = 256
VMEM_LIMIT = 56 * 1024 * 1024

NT = (((1,), (1,)), ((), ()))
TN_DIMS = (((0,), (0,)), ((), ()))

SLOT_C, SLOT_ONE, SLOT_LSE = 0, 3, 6


def _params(*sem):
    return pltpu.CompilerParams(dimension_semantics=sem, vmem_limit_bytes=VMEM_LIMIT)


def _sigmoid(x):
    return 1.0 / (1.0 + jnp.exp(-x))


def _split3(x):
    hi = x.astype(BF16).astype(F32)
    rest = x - hi
    mid = rest.astype(BF16).astype(F32)
    return hi, mid, rest - mid


def _after_specs(after):
    return [pl.BlockSpec(memory_space=pl.ANY)] * len(after)


def _inproj_fwd(x, gam, wmain, wf, after=()):
    S, D = x.shape
    N = wmain.shape[1]
    tm, tn = min(TM, S), TN

    def body(x_ref, g_ref, w_ref, wf_ref, *rest):
        proj_ref, h_ref, z_ref = rest[-3:]
        xf = x_ref[...]
        r = lax.rsqrt(jnp.mean(xf * xf, axis=-1, keepdims=True) + RMS_EPS)
        h = ((xf * r) * g_ref[...]).astype(BF16)
        h_ref[...] = h
        z_ref[...] = jnp.dot(h, wf_ref[...], preferred_element_type=F32)
        for n in range(N // tn):
            cols = slice(n * tn, (n + 1) * tn)
            proj_ref[:, cols] = jnp.dot(h, w_ref[:, cols], preferred_element_type=F32).astype(BF16)

    return pl.pallas_call(
        body, name="inproj_fwd", grid=(S // tm,),
        in_specs=[pl.BlockSpec((tm, D), lambda i: (i, 0)),
                  pl.BlockSpec((1, D), lambda i: (0, 0)),
                  pl.BlockSpec((D, N), lambda i: (0, 0)),
                  pl.BlockSpec((D, LANES), lambda i: (0, 0))] + _after_specs(after),
        out_specs=[pl.BlockSpec((tm, N), lambda i: (i, 0)),
                   pl.BlockSpec((tm, D), lambda i: (i, 0)),
                   pl.BlockSpec((tm, LANES), lambda i: (i, 0))],
        out_shape=[jax.ShapeDtypeStruct((S, N), BF16),
                   jax.ShapeDtypeStruct((S, D), BF16),
                   jax.ShapeDtypeStruct((S, LANES), F32)],
        compiler_params=_params("parallel"),
    )(x, gam, wmain, wf, *after)


def _fgate_fwd(z, bias, n_heads):
    S = z.shape[0]
    tb = min(TB, S)
    P = n_heads // 2

    def body(z_ref, b_ref, qaug_ref, kaug_ref):
        lane = lax.broadcasted_iota(jnp.int32, (tb, LANES), 1)
        tri = (lax.broadcasted_iota(jnp.int32, (tb, tb), 0)
               >= lax.broadcasted_iota(jnp.int32, (tb, tb), 1)).astype(F32)
        head = lax.broadcasted_iota(jnp.int32, (LANES, P * LANES), 0)
        col = lax.broadcasted_iota(jnp.int32, (LANES, P * LANES), 1)
        home = (head >> 1) * LANES + jnp.where((head & 1) == 0, HEAD_DIM, 0)
        is_head = head < n_heads
        place_q = [jnp.logical_and(is_head, col == home + SLOT_C + n).astype(BF16) for n in range(3)]
        place_k = [jnp.logical_and(is_head, col == home + SLOT_ONE + n).astype(BF16) for n in range(3)]
        slot = lax.broadcasted_iota(jnp.int32, (tb, P * LANES), 1) & (HEAD_DIM - 1)
        q_ones = jnp.logical_and(slot >= SLOT_ONE, slot < SLOT_ONE + 3).astype(F32)
        k_ones = jnp.logical_or(slot < SLOT_C + 3,
                                jnp.logical_and(slot >= SLOT_LSE, slot < SLOT_LSE + 3)).astype(F32)

        local = []
        for b in range(S // tb):
            zz = z_ref[b * tb:(b + 1) * tb, :] + b_ref[...]
            lf = jnp.minimum(zz, 0.0) - jnp.log(1.0 + jnp.exp(-jnp.abs(zz)))
            lf = jnp.where(lane < n_heads, lf, 0.0)
            local.append(jnp.dot(tri, lf, preferred_element_type=F32, precision=lax.Precision.HIGHEST))
        carry = jnp.zeros((1, LANES), F32)
        for b, part_sum in enumerate(local):
            c = part_sum + carry
            carry = c[tb - 1:tb, :]
            qa, ka = q_ones, k_ones
            for n, part in enumerate(_split3(c)):
                qa = qa + jnp.dot(part.astype(BF16), place_q[n], preferred_element_type=F32)
                ka = ka - jnp.dot(part.astype(BF16), place_k[n], preferred_element_type=F32)
            qaug_ref[b * tb:(b + 1) * tb, :] = qa.astype(BF16)
            kaug_ref[b * tb:(b + 1) * tb, :] = ka.astype(BF16)

    return pl.pallas_call(
        body, name="fgate_fwd",
        out_shape=[jax.ShapeDtypeStruct((S, P * LANES), BF16),
                   jax.ShapeDtypeStruct((S, P * LANES), BF16)],
        compiler_params=pltpu.CompilerParams(vmem_limit_bytes=VMEM_LIMIT),
    )(z, bias)


def _window_mean_minus_self(u, pad_ref, w, S):
    pad_ref[0:MAX_WINDOW, :] = jnp.zeros((MAX_WINDOW, LANES), F32)
    pad_ref[MAX_WINDOW:MAX_WINDOW + S, :] = u
    acc = u
    for j in range(1, w):
        acc = acc + pad_ref[MAX_WINDOW - j:MAX_WINDOW - j + S, :]
    t = lax.broadcasted_iota(jnp.int32, (S, LANES), 0)
    cnt = jnp.minimum(t + 1, w).astype(F32)
    return acc / cnt - u, cnt


def _pool_fwd(proj, pool_w, pool_scale):
    S = proj.shape[0]
    G = len(POOL_WINDOWS)

    def body(u_ref, w_ref, s_ref, y_ref, pad_ref):
        g = pl.program_id(0)
        for gi, w in enumerate(POOL_WINDOWS):
            @pl.when(g == gi)
            def _():
                d, _ = _window_mean_minus_self(u_ref[...].astype(F32), pad_ref, w, S)
                y = jnp.dot(d.astype(BF16), w_ref[0].astype(BF16), preferred_element_type=F32)
                y_ref[...] = (y * s_ref[...]).astype(BF16)

    return pl.pallas_call(
        body, name="pool_fwd", grid=(G,),
        in_specs=[pl.BlockSpec((S, LANES), lambda g: (0, g)),
                  pl.BlockSpec((1, LANES, LANES), lambda g: (g, 0, 0)),
                  pl.BlockSpec((1, LANES), lambda g: (0, g))],
        out_specs=pl.BlockSpec((S, LANES), lambda g: (0, g)),
        out_shape=jax.ShapeDtypeStruct((S, G * LANES), BF16),
        scratch_shapes=[pltpu.VMEM((S + MAX_WINDOW, LANES), F32)],
        compiler_params=_params("arbitrary"),
    )(proj, pool_w, pool_scale)


def _head_halves(rows):
    lane = lax.broadcasted_iota(jnp.int32, (rows, LANES), 1)
    return lane, (lane < HEAD_DIM, lane >= HEAD_DIM)


def _attn_fwd(proj, qaug, kaug):
    S = proj.shape[0]
    W = proj.shape[1] // 6
    P = W // LANES
    tq = min(TQ, S)
    nq = S // tq
    qc, kc, vc = 2 * P, 3 * P, 4 * P
    scale = 1.0 / math.sqrt(HEAD_DIM)

    def body(q_ref, k_ref, v_ref, qa_ref, ka_ref, o_ref, qb_ref, qm_scr, m_scr, acc_scr):
        i = pl.program_id(1)
        lane, halves = _head_halves(tq)
        v_ones = ((lane & (HEAD_DIM - 1)) < 3).astype(BF16)
        qs = q_ref[...] * scale
        qm_scr[0] = jnp.where(halves[0], qs, qa_ref[...])
        qm_scr[1] = jnp.where(halves[1], qs, qa_ref[...])
        m_scr[...] = jnp.full(m_scr.shape, NEG_INF, F32)
        acc_scr[...] = jnp.zeros(acc_scr.shape, F32)

        def update(j, on_diagonal):
            keys = pl.ds(pl.multiple_of(j * tq, tq), tq)
            k2, v2, kaug_t = k_ref[keys, :], v_ref[keys, :], ka_ref[keys, :]
            if on_diagonal:
                keep = (lax.broadcasted_iota(jnp.int32, (tq, tq), 0)
                        >= lax.broadcasted_iota(jnp.int32, (tq, tq), 1))
            logits = [lax.dot_general(qm_scr[a], jnp.where(halves[a], k2, kaug_t), NT, preferred_element_type=F32)
                      for a in range(2)]
            for a in range(2):
                s = jnp.where(keep, logits[a], NEG_INF) if on_diagonal else logits[a]
                va = jnp.where(halves[a], v2, v_ones)
                m_prev = m_scr[a]
                m_new = jnp.maximum(m_prev, jnp.max(s, axis=1, keepdims=True))
                p = jnp.exp(s - jnp.tile(m_new, (1, tq // LANES)))
                acc_scr[a] = jnp.exp(m_prev - m_new) * acc_scr[a] + jnp.dot(p.astype(BF16), va,
                                                                              preferred_element_type=F32)
                m_scr[a] = m_new

        def below_diagonal(jj, carry):
            update(2 * jj, False)
            update(2 * jj + 1, False)
            return carry

        lax.fori_loop(0, i // 2, below_diagonal, 0)

        @pl.when(i % 2 == 1)
        def _():
            update(i - 1, False)

        update(i, True)
        acc_a, acc_b = acc_scr[0], acc_scr[1]
        l_a, l_b = acc_a[:, HEAD_DIM:HEAD_DIM + 1], acc_b[:, 0:1]
        o_ref[...] = jnp.where(halves[0], acc_a / l_a, acc_b / l_b).astype(BF16)
        lse = jnp.where(halves[0], m_scr[1] + jnp.log(l_b), m_scr[0] + jnp.log(l_a))
        slot = lane & (HEAD_DIM - 1)
        aug = qa_ref[...].astype(F32)
        for n, part in enumerate(_split3(lse)):
            aug = jnp.where(slot == SLOT_LSE + n, -part, aug)
        qb_ref[...] = aug.astype(BF16)

    tile = lambda col: pl.BlockSpec((tq, LANES), lambda p, i: (i, col + p))
    whole = lambda col: pl.BlockSpec((S, LANES), lambda p, i: (0, col + p))
    return pl.pallas_call(
        body, name="attn_fwd", grid=(P, nq),
        in_specs=[tile(qc), whole(kc), whole(vc), tile(0), whole(0)],
        out_specs=[tile(0), tile(0)],
        out_shape=[jax.ShapeDtypeStruct((S, W), BF16), jax.ShapeDtypeStruct((S, W), BF16)],
        scratch_shapes=[pltpu.VMEM((2, tq, LANES), BF16),
                        pltpu.VMEM((2, tq, LANES), F32),
                        pltpu.VMEM((2, tq, LANES), F32)],
        compiler_params=_params("parallel", "arbitrary"),
    )(proj, proj, proj, qaug, kaug)


def _outproj_fwd(ypool, o, proj, x, wout):
    S, D = x.shape
    W = D // 2
    tm, tn = min(TM, S), TN

    def body(y_ref, o_ref, pg_ref, ag_ref, x_ref, w_ref, xn_ref, mix_ref):
        pg, ag = pg_ref[...].astype(F32), ag_ref[...].astype(F32)
        mix_ref[:, 0:W] = (y_ref[...].astype(F32) * (pg * _sigmoid(pg))).astype(BF16)
        mix_ref[:, W:D] = (o_ref[...].astype(F32) * (ag * _sigmoid(ag))).astype(BF16)
        for n in range(D // tn):
            cols = slice(n * tn, (n + 1) * tn)
            xn_ref[:, cols] = x_ref[:, cols] + jnp.dot(mix_ref[...], w_ref[:, cols], preferred_element_type=F32)

    return pl.pallas_call(
        body, name="outproj_fwd", grid=(S // tm,),
        in_specs=[pl.BlockSpec((tm, W), lambda i: (i, 0)),
                  pl.BlockSpec((tm, W), lambda i: (i, 0)),
                  pl.BlockSpec((tm, W), lambda i: (i, 1)),
                  pl.BlockSpec((tm, W), lambda i: (i, 5)),
                  pl.BlockSpec((tm, D), lambda i: (i, 0)),
                  pl.BlockSpec((D, D), lambda i: (0, 0))],
        out_specs=[pl.BlockSpec((tm, D), lambda i: (i, 0)),
                   pl.BlockSpec((tm, D), lambda i: (i, 0))],
        out_shape=[jax.ShapeDtypeStruct((S, D), F32),
                   jax.ShapeDtypeStruct((S, D), BF16)],
        compiler_params=_params("parallel"),
    )(ypool, o, proj, proj, x, wout)


def _loss_head(x, gam, target):
    S, D = x.shape
    tm = min(TM, S)

    def body(x_ref, g_ref, t_ref, dx_ref, loss_ref, dg_ref):
        @pl.when(pl.program_id(0) == 0)
        def _():
            loss_ref[...] = jnp.zeros(loss_ref.shape, F32)
            dg_ref[...] = jnp.zeros(dg_ref.shape, F32)

        xf, gam_v = x_ref[...], g_ref[...]
        r = lax.rsqrt(jnp.mean(xf * xf, axis=-1, keepdims=True) + RMS_EPS)
        xhat = xf * r
        err = xhat * gam_v - t_ref[...]
        part = jnp.sum(jnp.sum(err * err, axis=-1, keepdims=True), axis=0, keepdims=True)
        loss_ref[...] += part * (0.5 / D)
        dy = err * (1.0 / D)
        dg_ref[...] += jnp.sum(dy * xhat, axis=0, keepdims=True)
        dxhat = dy * gam_v
        dx_ref[...] = r * (dxhat - xhat * jnp.mean(dxhat * xhat, axis=-1, keepdims=True))

    return pl.pallas_call(
        body, name="loss_head", grid=(S // tm,),
        in_specs=[pl.BlockSpec((tm, D), lambda i: (i, 0)),
                  pl.BlockSpec((1, D), lambda i: (0, 0)),
                  pl.BlockSpec((tm, D), lambda i: (i, 0))],
        out_specs=[pl.BlockSpec((tm, D), lambda i: (i, 0)),
                   pl.BlockSpec((8, LANES), lambda i: (0, 0)),
                   pl.BlockSpec((1, D), lambda i: (0, 0))],
        out_shape=[jax.ShapeDtypeStruct((S, D), F32),
                   jax.ShapeDtypeStruct((8, LANES), F32),
                   jax.ShapeDtypeStruct((1, D), F32)],
        compiler_params=_params("arbitrary"),
    )(x, gam, target)


def _outproj_bwd(g, wout, mixed, ypool, o, proj):
    S, D = g.shape
    W = D // 2
    tm = min(TM, S)

    def body(g_ref, w_ref, mix_ref, y_ref, o_ref, pg_ref, ag_ref, dw_ref, da_ref, dgate_ref, doaug_ref):
        @pl.when(pl.program_id(0) == 0)
        def _():
            dw_ref[...] = jnp.zeros(dw_ref.shape, F32)

        gb = g_ref[...].astype(BF16)
        dw_ref[...] += lax.dot_general(mix_ref[...], gb, TN_DIMS, preferred_element_type=F32)
        for half, (val_ref, gate_ref) in enumerate(((y_ref, pg_ref), (o_ref, ag_ref))):
            cols = slice(half * W, (half + 1) * W)
            dmix = lax.dot_general(gb, w_ref[cols, :], NT, preferred_element_type=F32)
            gt = gate_ref[...].astype(F32)
            sg = _sigmoid(gt)
            da_ref[:, cols] = (dmix * (gt * sg)).astype(BF16)
            dgate_ref[:, cols] = (dmix * val_ref[...].astype(F32) * (sg * (1.0 + gt * (1.0 - sg)))).astype(BF16)

        lane, halves = _head_halves(tm)
        slot = lane & (HEAD_DIM - 1)
        for p in range(W // LANES):
            cols = slice(p * LANES, (p + 1) * LANES)
            prod = da_ref[:, W + p * LANES:W + (p + 1) * LANES].astype(F32) * o_ref[:, cols].astype(F32)
            d_a = jnp.sum(jnp.where(halves[0], prod, 0.0), axis=1, keepdims=True)
            d_b = jnp.sum(jnp.where(halves[1], prod, 0.0), axis=1, keepdims=True)
            aug = jnp.zeros((tm, LANES), F32)
            for n, part in enumerate(_split3(jnp.where(halves[0], d_b, d_a))):
                aug = jnp.where(slot == SLOT_C + n, -part, aug)
            doaug_ref[:, cols] = aug.astype(BF16)

    rows = lambda width, col: pl.BlockSpec((tm, width), lambda i: (i, col))
    return pl.pallas_call(
        body, name="outproj_bwd", grid=(S // tm,),
        in_specs=[rows(D, 0), pl.BlockSpec((D, D), lambda i: (0, 0)), rows(D, 0), rows(W, 0), rows(W, 0),
                  rows(W, 1), rows(W, 5)],
        out_specs=[pl.BlockSpec((D, D), lambda i: (0, 0)), rows(D, 0), rows(D, 0), rows(W, 0)],
        out_shape=[jax.ShapeDtypeStruct((D, D), F32),
                   jax.ShapeDtypeStruct((S, D), BF16),
                   jax.ShapeDtypeStruct((S, D), BF16),
                   jax.ShapeDtypeStruct((S, W), BF16)],
        compiler_params=_params("arbitrary"),
    )(g, wout, mixed, ypool, o, proj, proj)


def _section_specs(sections, rows, width):
    specs = [pl.BlockSpec((rows, width), lambda k, c=c: (k, c)) for _, c in sections]
    return specs, [a for a, _ in sections]


def _inproj_bwd_dw(h, sections, dzf):
    S, D = h.shape
    W = D // 2
    n_sec = len(sections)
    ts = min(TM, S)

    def body(h_ref, dz_ref, *rest):
        sec_refs, (dw_ref, dwf_ref) = rest[:n_sec], rest[n_sec:]

        @pl.when(pl.program_id(0) == 0)
        def _():
            dw_ref[...] = jnp.zeros(dw_ref.shape, F32)
            dwf_ref[...] = jnp.zeros(dwf_ref.shape, F32)

        ht = h_ref[...].T
        dwf_ref[...] += jnp.dot(ht, dz_ref[...], preferred_element_type=F32)
        for n, ref in enumerate(sec_refs):
            dw_ref[:, n * W:(n + 1) * W] += jnp.dot(ht, ref[...], preferred_element_type=F32)

    sec_specs, sec_arrays = _section_specs(sections, ts, W)
    return pl.pallas_call(
        body, name="inproj_bwd_dw", grid=(S // ts,),
        in_specs=[pl.BlockSpec((ts, D), lambda k: (k, 0)),
                  pl.BlockSpec((ts, LANES), lambda k: (k, 0))] + sec_specs,
        out_specs=[pl.BlockSpec((D, n_sec * W), lambda k: (0, 0)),
                   pl.BlockSpec((D, LANES), lambda k: (0, 0))],
        out_shape=[jax.ShapeDtypeStruct((D, n_sec * W), F32),
                   jax.ShapeDtypeStruct((D, LANES), F32)],
        compiler_params=_params("arbitrary"),
    )(h, dzf, *sec_arrays)


def _attn_bwd(proj, da, qaug, kaug, doaug):
    S = proj.shape[0]
    W = proj.shape[1] // 6
    P = W // LANES
    tq = min(TQ, S)
    nq = S // tq
    qc, kc, vc = 2 * P, 3 * P, 4 * P
    scale = 1.0 / math.sqrt(HEAD_DIM)

    def body(q_ref, k_ref, v_ref, do_ref, qa_ref, ka_ref, da_ref,
             dq_ref, dk_ref, dv_ref, dqx_ref, dkx_ref, km_scr, vm_scr, dk_scr, dv_scr, dq_scr):
        j = pl.program_id(1)
        lane, halves = _head_halves(tq)

        @pl.when(j == 0)
        def _():
            dq_scr[...] = jnp.zeros(dq_scr.shape, F32)

        v_ones = ((lane & (HEAD_DIM - 1)) < 3).astype(BF16)
        for a in range(2):
            km_scr[a] = jnp.where(halves[a], k_ref[...], ka_ref[...])
            vm_scr[a] = jnp.where(halves[a], v_ref[...], v_ones)
        dk_scr[...] = jnp.zeros(dk_scr.shape, F32)
        dv_scr[...] = jnp.zeros(dv_scr.shape, F32)

        def update(i, on_diagonal):
            rows = pl.ds(pl.multiple_of(i * tq, tq), tq)
            qs = q_ref[rows, :] * scale
            do2, qaug_t, doaug_t = do_ref[rows, :], qa_ref[rows, :], da_ref[rows, :]
            if on_diagonal:
                keep = (lax.broadcasted_iota(jnp.int32, (tq, tq), 0)
                        >= lax.broadcasted_iota(jnp.int32, (tq, tq), 1))
            qas = [jnp.where(halves[a], qs, qaug_t) for a in range(2)]
            logits = [lax.dot_general(qas[a], km_scr[a], NT, preferred_element_type=F32) for a in range(2)]
            dps = [lax.dot_general(jnp.where(halves[a], do2, doaug_t), vm_scr[a], NT, preferred_element_type=F32)
                   for a in range(2)]
            dv = None
            for a in range(2):
                s = jnp.where(keep, logits[a], NEG_INF) if on_diagonal else logits[a]
                p = jnp.exp(s)
                dsb = (p * dps[a]).astype(BF16)
                do0 = jnp.where(halves[a], do2, jnp.zeros_like(do2))
                dv_a = lax.dot_general(p.astype(BF16), do0, TN_DIMS, preferred_element_type=F32)
                dv = dv_a if dv is None else dv + dv_a
                dk_scr[a] += lax.dot_general(dsb, qas[a], TN_DIMS, preferred_element_type=F32)
                dq_scr[a, rows, :] += jnp.dot(dsb, km_scr[a], preferred_element_type=F32)
            dv_scr[...] += dv

        def below_diagonal(n, carry):
            update(j + 1 + 2 * n, False)
            update(j + 2 + 2 * n, False)
            return carry

        update(j, True)
        below = nq - 1 - j
        lax.fori_loop(0, below // 2, below_diagonal, 0)

        @pl.when(below % 2 == 1)
        def _():
            update(nq - 1, False)


        dk_ref[...] = jnp.where(halves[0], dk_scr[0], dk_scr[1]).astype(BF16)
        dkx_ref[...] = jnp.where(halves[0], dk_scr[1], dk_scr[0])
        dv_ref[...] = dv_scr[...].astype(BF16)

        @pl.when(j == nq - 1)
        def _():
            row_lane, row_halves = _head_halves(S)
            dq_ref[...] = (jnp.where(row_halves[0], dq_scr[0], dq_scr[1]) * scale).astype(BF16)
            dqx_ref[...] = jnp.where(row_halves[0], dq_scr[1], dq_scr[0])

    tile = lambda col: pl.BlockSpec((tq, LANES), lambda p, j: (j, col + p))
    whole = lambda col: pl.BlockSpec((S, LANES), lambda p, j: (0, col + p))
    return pl.pallas_call(
        body, name="attn_bwd", grid=(P, nq),
        in_specs=[whole(qc), tile(kc), tile(vc), whole(P), whole(0), tile(0), whole(0)],
        out_specs=[whole(0), tile(0), tile(0), whole(0), tile(0)],
        out_shape=[jax.ShapeDtypeStruct((S, W), BF16),
                   jax.ShapeDtypeStruct((S, W), BF16),
                   jax.ShapeDtypeStruct((S, W), BF16),
                   jax.ShapeDtypeStruct((S, W), F32),
                   jax.ShapeDtypeStruct((S, W), F32)],
        scratch_shapes=[pltpu.VMEM((2, tq, LANES), BF16),
                        pltpu.VMEM((2, tq, LANES), BF16),
                        pltpu.VMEM((2, tq, LANES), F32),
                        pltpu.VMEM((tq, LANES), F32),
                        pltpu.VMEM((2, S, LANES), F32)],
        compiler_params=_params("parallel", "arbitrary"),
    )(proj, proj, proj, da, qaug, kaug, doaug)


def _fgate_bwd(dqx, dkx, z, bias, n_heads):
    S = z.shape[0]
    tb = min(TB, S)
    nb = S // tb

    def body(dqx_ref, dkx_ref, z_ref, b_ref, dz_ref, db_ref):
        tri = (lax.broadcasted_iota(jnp.int32, (tb, tb), 1)
               >= lax.broadcasted_iota(jnp.int32, (tb, tb), 0)).astype(F32)
        n_cols = dqx_ref.shape[1]
        col = lax.broadcasted_iota(jnp.int32, (n_cols, LANES), 0)
        head = lax.broadcasted_iota(jnp.int32, (n_cols, LANES), 1)
        home = (head >> 1) * LANES + jnp.where((head & 1) == 0, HEAD_DIM, 0)
        is_head = head < n_heads
        pick_rows = jnp.logical_and(is_head, col == home + SLOT_C).astype(F32)
        pick_cols = jnp.logical_and(is_head, col == home + SLOT_ONE).astype(F32)

        local = []
        for b in range(nb):
            rows = slice(b * tb, (b + 1) * tb)
            dc = (jnp.dot(dqx_ref[rows, :], pick_rows, preferred_element_type=F32, precision=lax.Precision.HIGHEST)
                  - jnp.dot(dkx_ref[rows, :], pick_cols, preferred_element_type=F32, precision=lax.Precision.HIGHEST))
            local.append(jnp.dot(tri, dc, preferred_element_type=F32, precision=lax.Precision.HIGHEST))
        carry = jnp.zeros((1, LANES), F32)
        db = jnp.zeros((1, LANES), F32)
        for b in reversed(range(nb)):
            rows = slice(b * tb, (b + 1) * tb)
            rc = local[b] + carry
            carry = rc[0:1, :]
            dz = rc * _sigmoid(-(z_ref[rows, :] + b_ref[...]))
            dz_ref[rows, :] = dz.astype(BF16)
            db = db + jnp.sum(dz, axis=0, keepdims=True)
        db_ref[...] = db

    return pl.pallas_call(
        body, name="fgate_bwd",
        out_shape=[jax.ShapeDtypeStruct((S, LANES), BF16),
                   jax.ShapeDtypeStruct((1, LANES), F32)],
        compiler_params=pltpu.CompilerParams(vmem_limit_bytes=VMEM_LIMIT),
    )(dqx, dkx, z, bias)


def _pool_bwd(proj, da, pool_w, pool_scale):
    S = proj.shape[0]
    G = len(POOL_WINDOWS)

    def body(u_ref, dy_ref, w_ref, s_ref, du_ref, dw_ref, ds_ref, pad_ref):
        g = pl.program_id(0)
        for gi, w in enumerate(POOL_WINDOWS):
            @pl.when(g == gi)
            def _():
                d, cnt = _window_mean_minus_self(u_ref[...].astype(F32), pad_ref, w, S)
                db = d.astype(BF16)
                wb = w_ref[0].astype(BF16)
                yraw = jnp.dot(db, wb, preferred_element_type=F32)
                dy = dy_ref[...].astype(F32)
                ds_ref[...] = jnp.sum(dy * yraw, axis=0, keepdims=True)
                dzb = (dy * s_ref[...]).astype(BF16)
                dw_ref[0] = lax.dot_general(db, dzb, TN_DIMS, preferred_element_type=F32)
                dd = lax.dot_general(dzb, wb, NT, preferred_element_type=F32)
                pad_ref[0:S, :] = dd / cnt
                pad_ref[S:S + MAX_WINDOW, :] = jnp.zeros((MAX_WINDOW, LANES), F32)
                acc = -dd
                for j in range(w):
                    acc = acc + pad_ref[j:j + S, :]
                du_ref[...] = acc.astype(BF16)

    return pl.pallas_call(
        body, name="pool_bwd", grid=(G,),
        in_specs=[pl.BlockSpec((S, LANES), lambda g: (0, g)),
                  pl.BlockSpec((S, LANES), lambda g: (0, g)),
                  pl.BlockSpec((1, LANES, LANES), lambda g: (g, 0, 0)),
                  pl.BlockSpec((1, LANES), lambda g: (0, g))],
        out_specs=[pl.BlockSpec((S, LANES), lambda g: (0, g)),
                   pl.BlockSpec((1, LANES, LANES), lambda g: (g, 0, 0)),
                   pl.BlockSpec((1, LANES), lambda g: (0, g))],
        out_shape=[jax.ShapeDtypeStruct((S, G * LANES), BF16),
                   jax.ShapeDtypeStruct((G, LANES, LANES), F32),
                   jax.ShapeDtypeStruct((1, G * LANES), F32)],
        scratch_shapes=[pltpu.VMEM((S + MAX_WINDOW, LANES), F32)],
        compiler_params=_params("arbitrary"),
    )(proj, da, pool_w, pool_scale)


def _inproj_bwd_dx(sections, dzf, wmain, wf, x, gam, g, after=()):
    S, D = x.shape
    N = wmain.shape[1]
    W = D // 2
    n_sec = len(sections)
    tm = min(TM // 2, S)

    def body(dz_ref, w_ref, wf_ref, x_ref, gam_ref, g_ref, *rest):
        sec_refs = rest[:n_sec]
        dx_ref, dg_ref = rest[-2:]

        @pl.when(pl.program_id(0) == 0)
        def _():
            dg_ref[...] = jnp.zeros(dg_ref.shape, F32)

        dh = lax.dot_general(dz_ref[...], wf_ref[...], NT, preferred_element_type=F32)
        for n, ref in enumerate(sec_refs):
            dh = dh + lax.dot_general(ref[...], w_ref[:, n * W:(n + 1) * W], NT, preferred_element_type=F32)
        xf = x_ref[...]
        r = lax.rsqrt(jnp.mean(xf * xf, axis=-1, keepdims=True) + RMS_EPS)
        xhat = xf * r
        dg_ref[...] += jnp.sum(dh * xhat, axis=0, keepdims=True)
        dxhat = dh * gam_ref[...]
        dx_ref[...] = g_ref[...] + r * (dxhat - xhat * jnp.mean(dxhat * xhat, axis=-1, keepdims=True))

    sec_specs, sec_arrays = _section_specs(sections, tm, W)
    return pl.pallas_call(
        body, name="inproj_bwd_dx", grid=(S // tm,),
        in_specs=[pl.BlockSpec((tm, LANES), lambda i: (i, 0)),
                  pl.BlockSpec((D, N), lambda i: (0, 0)),
                  pl.BlockSpec((D, LANES), lambda i: (0, 0)),
                  pl.BlockSpec((tm, D), lambda i: (i, 0)),
                  pl.BlockSpec((1, D), lambda i: (0, 0)),
                  pl.BlockSpec((tm, D), lambda i: (i, 0))] + sec_specs + _after_specs(after),
        out_specs=[pl.BlockSpec((tm, D), lambda i: (i, 0)),
                   pl.BlockSpec((1, D), lambda i: (0, 0))],
        out_shape=[jax.ShapeDtypeStruct((S, D), F32),
                   jax.ShapeDtypeStruct((1, D), F32)],
        compiler_params=_params("arbitrary"),
    )(dzf, wmain, wf, x, gam, g, *sec_arrays, *after)


def _adamw(w, m, v, gsets, name, rows):
    A, R, C = w.shape
    tr = min(rows, R)
    c1 = 1.0 / (1.0 - ADAM_B1 ** ADAM_STEP)
    c2 = 1.0 / (1.0 - ADAM_B2 ** ADAM_STEP)
    counts = [len(gs) for gs in gsets]

    def body(w_ref, m_ref, v_ref, *rest):
        g_ref, d_ref, nm_ref, nv_ref = rest[-4:]
        at = 0
        for a in range(A):
            part_refs = rest[at:at + counts[a]]
            at += counts[a]

            @pl.when(pl.program_id(0) == a)
            def _():
                g = None
                for ref in part_refs:
                    for s in range(ref.shape[0]):
                        term = ref[s].astype(F32)
                        g = term if g is None else g + term
                nm = ADAM_B1 * m_ref[0] + (1.0 - ADAM_B1) * g
                nv = ADAM_B2 * v_ref[0] + (1.0 - ADAM_B2) * (g * g)
                g_ref[0] = g
                nm_ref[0] = nm
                nv_ref[0] = nv
                d_ref[0] = -ADAM_LR * ((nm * c1) / (jnp.sqrt(nv * c2) + ADAM_EPS) + ADAM_WD * w_ref[0])

    spec = pl.BlockSpec((1, tr, C), lambda a, r: (a, r, 0))
    part_specs = [pl.BlockSpec((part.shape[0], tr, C), lambda a, r, l=l: (0, jnp.where(a == l, r, 0), 0))
                  for l, gs in enumerate(gsets) for part in gs]
    shape = jax.ShapeDtypeStruct((A, R, C), F32)
    return pl.pallas_call(
        body, name=name, grid=(A, R // tr),
        in_specs=[spec, spec, spec] + part_specs,
        out_specs=[spec, spec, spec, spec],
        out_shape=[shape, shape, shape, shape],
        compiler_params=_params("arbitrary", "arbitrary"),
    )(w, m, v, *[part for gs in gsets for part in gs])


def _position():
    return lax.axis_index("x"), lax.axis_index("y"), lax.axis_index("c")


def _index(dev):
    return 4 * dev[0] + 2 * dev[1] + dev[2]


def _all_gather(arrs, slots, out_shapes, name):
    n_arr = len(arrs)

    def body(*refs):
        ins, outs = refs[:n_arr], refs[n_arr:2 * n_arr]
        send_sems, recv_sems, local_sems = refs[2 * n_arr:]
        x, y, c = _position()
        me, sibling = (x, y, c), (x, y, 1 - c)
        chips = [(1 - x, y), (x, 1 - y), (1 - x, 1 - y)]

        def copy(a, k, block, to, src=None):
            part = slots[a](outs[a], _index(block))
            return pltpu.make_async_remote_copy(
                src_ref=part if src is None else src, dst_ref=part,
                send_sem=send_sems.at[a, k], recv_sem=recv_sems.at[a, k],
                device_id=to, device_id_type=MESH)

        mine = [pltpu.make_async_copy(ins[a], slots[a](outs[a], _index(me)), local_sems.at[a])
                for a in range(n_arr)]
        for cp in mine:
            cp.start()
        first = []
        for a in range(n_arr):
            first.append(copy(a, 0, me, sibling, src=ins[a]))
            first += [copy(a, 1 + j, me, (*chip, c), src=ins[a]) for j, chip in enumerate(chips)]
        for cp in first:
            cp.start()
        passed = []
        for j, chip in enumerate(chips):
            for a in range(n_arr):
                copy(a, 1 + j, (*chip, c), me).wait_recv()
                fwd = copy(a, 4 + j, (*chip, c), sibling)
                fwd.start()
                passed.append(fwd)
        for a in range(n_arr):
            copy(a, 0, sibling, me).wait_recv()
            for j, chip in enumerate(chips):
                copy(a, 4 + j, (*chip, 1 - c), me).wait_recv()
        for cp in first + passed:
            cp.wait_send()
        for cp in mine:
            cp.wait()

    any_spec = pl.BlockSpec(memory_space=pl.ANY)
    return pl.pallas_call(
        body, name=name,
        in_specs=[any_spec] * n_arr, out_specs=[any_spec] * n_arr, out_shape=out_shapes,
        scratch_shapes=[pltpu.SemaphoreType.DMA((n_arr, 7)), pltpu.SemaphoreType.DMA((n_arr, 7)),
                        pltpu.SemaphoreType.DMA((n_arr,))],
    )(*arrs)


def _split_copies(srcs, lands, send_sems, recv_sems, kinds):
    x, y, c = _position()
    me = _index((x, y, c))
    copies = []
    for a, (src_part, land_part) in enumerate(kinds):
        for k in range(1, N_DEV):
            peer = (x ^ ((k >> 2) & 1), y ^ ((k >> 1) & 1), c ^ (k & 1))
            copies.append(pltpu.make_async_remote_copy(
                src_ref=src_part(srcs[a], _index(peer)), dst_ref=land_part(lands[a], me, k),
                send_sem=send_sems[a].at[k - 1], recv_sem=recv_sems[a].at[k - 1],
                device_id=peer, device_id_type=MESH))
    return copies


def _split_start(srcs, lands, kinds, name, after=()):
    n = len(srcs)

    def body(*refs):
        src_refs, land_refs = refs[:n], refs[n:2 * n]
        outs = refs[2 * n + len(after):]
        send_sems, recv_sems = outs[:n], outs[n:2 * n]
        token = outs[-1]
        for cp in _split_copies(src_refs, land_refs, send_sems, recv_sems, kinds):
            cp.start()
        token[...] = jnp.zeros(token.shape, token.dtype)

    hbm = pl.BlockSpec(memory_space=pltpu.HBM)
    sem = pl.BlockSpec(memory_space=pltpu.SEMAPHORE)
    operands = [pltpu.with_memory_space_constraint(t, pltpu.HBM) for t in (*srcs, *lands)]
    out = pl.pallas_call(
        body, name=name,
        in_specs=[hbm] * (2 * n) + _after_specs(after),
        out_specs=[sem] * (2 * n) + [hbm] * (2 * n) + [pl.BlockSpec(memory_space=pltpu.VMEM)],
        out_shape=[pltpu.SemaphoreType.DMA((N_DEV - 1,))] * (2 * n)
        + [pltpu.HBM(t.shape, t.dtype) for t in operands] + [jax.ShapeDtypeStruct((8, LANES), F32)],
        input_output_aliases={i: 2 * n + i for i in range(2 * n)},
        compiler_params=pltpu.CompilerParams(has_side_effects=pltpu.SideEffectType.DATAFLOW_SIDE_EFFECTING),
    )(*operands, *after)
    return [(out[a], out[n + a], out[2 * n + a], out[3 * n + a]) for a in range(n)], out[-1]


def _split_wait(started, kinds, after, name):
    n = len(started)
    sems = [t[0] for t in started] + [t[1] for t in started]
    srcs = [t[2] for t in started]
    lands = [t[3] for t in started]

    def body(*refs):
        src_refs, land_refs = refs[:n], refs[n:2 * n]
        send_sems, recv_sems = refs[2 * n:3 * n], refs[3 * n:4 * n]
        for cp in _split_copies(src_refs, land_refs, send_sems, recv_sems, kinds):
            cp.wait_send()
            cp.wait_recv()

    hbm = pl.BlockSpec(memory_space=pltpu.HBM)
    sem = pl.BlockSpec(memory_space=pltpu.SEMAPHORE)
    out = pl.pallas_call(
        body, name=name,
        in_specs=[hbm] * (2 * n) + [sem] * (2 * n) + _after_specs(after),
        out_specs=[hbm] * (2 * n),
        out_shape=[pltpu.HBM(t.shape, t.dtype) for t in (*srcs, *lands)],
        input_output_aliases={i: i for i in range(2 * n)},
        compiler_params=pltpu.CompilerParams(has_side_effects=pltpu.SideEffectType.DATAFLOW_SIDE_EFFECTING),
    )(*srcs, *lands, *sems, *after)
    return out[n:]


def _as_rows(p):
    if p.size % LANES == 0:
        rows = p.reshape(-1, LANES)
    else:
        rows = p.reshape(-1, p.shape[-1])
        rows = jnp.pad(rows, ((0, 0), (0, LANES - rows.shape[1])))
    return jnp.pad(rows, ((0, -rows.shape[0] % 8), (0, 0)))


def _pack(parts):
    return jnp.concatenate([_as_rows(p) for p in parts])[None]


def _unpack(packed, like):
    out, at = [], 0
    for p in like:
        whole = p.size % LANES == 0
        n = p.size // LANES if whole else p.size // p.shape[-1]
        rows = packed[0, at:at + n]
        out.append((rows if whole else rows[:, :p.shape[-1]]).reshape(p.shape))
        at += n + (-n % 8)
    return out


def _local_step(x, target, norm_g, forget_bias, pool_w, pool_scale, final_g, weights_in, weights_out, on_grads,
                first_after=()):
    L = norm_g.shape[0]
    S, D = x.shape
    W = D // 2
    H = W // HEAD_DIM
    bias = jnp.pad(forget_bias, ((0, 0), (0, LANES - H)))

    saved = []
    after = tuple(first_after)
    for l in range(L):
        wmain, wf = weights_in(l, x)
        proj, h, z = _inproj_fwd(x, norm_g[l:l + 1], wmain, wf, after)
        after = ()
        qaug, kaug = _fgate_fwd(z, bias[l:l + 1], H)
        ypool = _pool_fwd(proj, pool_w[l], pool_scale[l:l + 1])
        o, qaug_b = _attn_fwd(proj, qaug, kaug)
        wout = weights_out(l, o)
        x_new, mixed = _outproj_fwd(ypool, o, proj, x, wout)
        saved.append((x, proj, h, z, qaug_b, kaug, ypool, o, mixed, wmain, wf, wout))
        x = x_new

    g, loss, d_final_g = _loss_head(x, final_g.reshape(1, D), target)

    d_norm_g, d_bias, d_pool_w, d_pool_scale = [], [], [], []
    for l in reversed(range(L)):
        x_in, proj, h, z, qaug_b, kaug, ypool, o, mixed, wmain, wf, wout = saved[l]
        d_wout, da, dgate, doaug = _outproj_bwd(g, wout, mixed, ypool, o, proj)
        dq, dk, dv, dqx, dkx = _attn_bwd(proj, da, qaug_b, kaug, doaug)
        dzf, db = _fgate_bwd(dqx, dkx, z, bias[l:l + 1], H)
        dpu, dpw, dps = _pool_bwd(proj, da, pool_w[l], pool_scale[l:l + 1])
        dproj = [(dpu, 0), (dgate, 0), (dq, 0), (dk, 0), (dv, 0), (dgate, 1)]
        d_wmain, d_wf = _inproj_bwd_dw(h, dproj, dzf)
        after = tuple(on_grads(l, d_wmain, d_wf[:, :H], d_wout))
        g, dgam = _inproj_bwd_dx(dproj, dzf, wmain, wf, x_in, norm_g[l:l + 1], g, after)
        d_norm_g.append(dgam[0])
        d_bias.append(db[0, :H])
        d_pool_w.append(dpw)
        d_pool_scale.append(dps[0])

    stack = lambda parts: jnp.stack(parts[::-1])
    grads = dict(norm_g=stack(d_norm_g), forget_bias=stack(d_bias), pool_w=stack(d_pool_w),
                 pool_scale=stack(d_pool_scale), final_g=d_final_g[0])
    return loss[0, 0], g, grads


def kernel(x, norm_g, w_in, forget_bias, pool_w, pool_scale, w_out, final_g, loss_target, m_norm_g, m_w_in, m_forget_bias, m_pool_w, m_pool_scale, m_w_out, m_final_g, v_norm_g, v_w_in, v_forget_bias, v_pool_w, v_pool_scale, v_w_out, v_final_g):
    L, D, cols = w_in.shape
    rows_out = w_out.shape[1]
    W = D // 2
    H = W // HEAD_DIM
    me = _index(_position())

    win_b, wout_b = w_in.astype(BF16), w_out.astype(BF16)
    gather_in = (lambda ref, peer: ref, lambda ref, mine, k: ref.at[mine])
    gather_out = (lambda ref, peer: ref, lambda ref, mine, k: ref.at[pl.ds(mine * rows_out, rows_out), :])

    (first_in,) = _all_gather([win_b[0]], [lambda ref, n: ref.at[n]],
                              [jax.ShapeDtypeStruct((N_DEV, D, cols), BF16)], "gather_first")
    rest_srcs = [wout_b[0]] + [w[l] for l in range(1, L) for w in (win_b, wout_b)]
    rest_lands = [jnp.tile(wout_b[0], (N_DEV, 1))]
    for l in range(1, L):
        rest_lands += [jnp.broadcast_to(win_b[l][None], (N_DEV, D, cols)), jnp.tile(wout_b[l], (N_DEV, 1))]
    rest_kinds = [gather_out] + [gather_in, gather_out] * (L - 1)
    rest, rest_token = _split_start(rest_srcs, rest_lands, rest_kinds, "gather_start_rest", (first_in,))

    def weights_in(l, x_in):
        if l == 0:
            win_all = first_in
        else:
            (win_all,) = _split_wait([rest[2 * l - 1]], [gather_in], (x_in,), f"gather_wait_in_{l}")
        w_full = jnp.transpose(win_all, (1, 0, 2)).reshape(D, N_DEV * cols)
        return w_full[:, :6 * W], jnp.pad(w_full[:, 6 * W:], ((0, 0), (0, LANES - H)))

    def weights_out(l, o):
        (wout_full,) = _split_wait([rest[2 * l]], [gather_out], (o,), f"gather_wait_out_{l}")
        return wout_full

    exchange_kinds = [(lambda ref, peer: ref.at[peer], lambda ref, mine, k: ref.at[k - 1]),
                      (lambda ref, peer: ref.at[pl.ds(peer * rows_out, rows_out), :],
                       lambda ref, mine, k: ref.at[k - 1])]
    exchanges, own_parts = {}, {}

    def on_grads(l, d_wmain, d_wf, d_wout):
        dw_in = jnp.concatenate([d_wmain, d_wf], axis=1)
        own_parts[l] = (lax.dynamic_slice_in_dim(dw_in, me * cols, cols, 1)[None],
                        lax.dynamic_slice_in_dim(d_wout, me * rows_out, rows_out, 0)[None])
        srcs = [jnp.transpose(dw_in.astype(BF16).reshape(D, N_DEV, cols), (1, 0, 2)), d_wout.astype(BF16)]
        lands = [lax.empty((N_DEV - 1, D, cols), BF16), lax.empty((N_DEV - 1, rows_out, D), BF16)]
        exchanges[l], token = _split_start(srcs, lands, exchange_kinds, f"exchange_start_{l}")
        return (token,)

    loss, dx, grads = _local_step(x[0], loss_target[0], norm_g, forget_bias, pool_w, pool_scale, final_g,
                                  weights_in, weights_out, on_grads, (rest_token,))
    loss = lax.psum(loss, ("x", "y", "c"))

    names = ["norm_g", "forget_bias", "pool_w", "pool_scale", "final_g"]
    small_w = [norm_g, forget_bias, pool_w, pool_scale, final_g]
    small_m = [m_norm_g, m_forget_bias, m_pool_w, m_pool_scale, m_final_g]
    small_v = [v_norm_g, v_forget_bias, v_pool_w, v_pool_scale, v_final_g]
    gs = _pack([grads[n] for n in names])
    rows = gs.shape[1]
    (gs_all,) = _all_gather([gs], [lambda ref, n: ref.at[n]],
                            [jax.ShapeDtypeStruct((N_DEV, 1, rows, LANES), F32)], "gather_small_grads")
    packed = _adamw(_pack(small_w), _pack(small_m), _pack(small_v),
                    [[gs_all.reshape(N_DEV, rows, LANES)]], "adamw_small", rows)
    g_s, d_s, nm_s, nv_s = [_unpack(p, small_w) for p in packed]

    gin_sets, gout_sets = [], []
    for l in range(L):
        got_in, got_out = _split_wait(exchanges[l], exchange_kinds, (dx, packed[1]), f"exchange_wait_{l}")
        gin_sets.append([own_parts[l][0], got_in])
        gout_sets.append([own_parts[l][1], got_out])
    g_w_in, d_w_in, nm_w_in, nv_w_in = _adamw(w_in, m_w_in, v_w_in, gin_sets, "adamw_w_in", 256)
    g_w_out, d_w_out, nm_w_out, nv_w_out = _adamw(w_out, m_w_out, v_w_out, gout_sets, "adamw_w_out", 128)

    def order(big_in, big_out, small):
        return (small[0], big_in, small[1], small[2], small[3], big_out, small[4])

    return (loss, dx[None], *order(g_w_in, g_w_out, g_s), *order(d_w_in, d_w_out, d_s),
            *order(nm_w_in, nm_w_out, nm_s), *order(nv_w_in, nv_w_out, nv_s))
```

```python
import math

import jax
import jax.numpy as jnp
from jax import lax
from jax.experimental import pallas as pl
from jax.experimental.pallas import tpu as pltpu

F32 = jnp.float32
BF16 = jnp.bfloat16
MESH = pl.DeviceIdType.MESH

RMS_EPS = 1e-6
NEG_INF = -1e30
HEAD_DIM = 64
POOL_WINDOWS = (2, 4, 8, 16)
MAX_WINDOW = 16
LANES = 128
N_DEV = 8

ADAM_LR = 0.001
ADAM_B1 = 0.9
ADAM_B2 = 0.999
ADAM_EPS = 1e-08
ADAM_WD = 0.01
ADAM_STEP = 10

TM = 512
TN = 512
TQ = 512
TB = 256
VMEM_LIMIT = 56 * 1024 * 1024

NT = (((1,), (1,)), ((), ()))
TN_DIMS = (((0,), (0,)), ((), ()))

SLOT_C, SLOT_ONE, SLOT_LSE = 0, 3, 6


def _params(*sem):
    return pltpu.CompilerParams(dimension_semantics=sem, vmem_limit_bytes=VMEM_LIMIT)


def _sigmoid(x):
    return 1.0 / (1.0 + jnp.exp(-x))


def _split3(x):
    hi = x.astype(BF16).astype(F32)
    rest = x - hi
    mid = rest.astype(BF16).astype(F32)
    return hi, mid, rest - mid


def _after_specs(after):
    return [pl.BlockSpec(memory_space=pl.ANY)] * len(after)


def _slot_width(cols):
    return LANES * (-(-(cols + (N_DEV - 1) * (cols % LANES)) // LANES))


def _inproj_fwd(x, gam, slots, after=()):
    S, D = x.shape
    n_dev, _, sw = slots.shape
    stride = sw - LANES
    width = stride * n_dev + LANES
    N = width - LANES
    tm, tn = min(TM, S), TN

    def body(x_ref, g_ref, s_ref, *rest):
        proj_ref, h_ref, z_ref, w_ref = rest[-4:]

        @pl.when(pl.program_id(0) == 0)
        def _():
            for n in range(n_dev):
                base = stride * n
                first = s_ref[n, :, 0:LANES]
                if n > 0:
                    first = first + s_ref[n - 1, :, stride:sw]
                w_ref[:, base:base + LANES] = first
                w_ref[:, base + LANES:base + stride] = s_ref[n, :, LANES:stride]
            w_ref[:, stride * n_dev:width] = s_ref[n_dev - 1, :, stride:sw]

        xf = x_ref[...]
        r = lax.rsqrt(jnp.mean(xf * xf, axis=-1, keepdims=True) + RMS_EPS)
        h = ((xf * r) * g_ref[...]).astype(BF16)
        h_ref[...] = h
        z_ref[...] = jnp.dot(h, w_ref[:, N:width], preferred_element_type=F32)
        for n in range(N // tn):
            cols = slice(n * tn, (n + 1) * tn)
            proj_ref[:, cols] = jnp.dot(h, w_ref[:, cols], preferred_element_type=F32).astype(BF16)

    return pl.pallas_call(
        body, name="inproj_fwd", grid=(S // tm,),
        in_specs=[pl.BlockSpec((tm, D), lambda i: (i, 0)),
                  pl.BlockSpec((1, D), lambda i: (0, 0)),
                  pl.BlockSpec((n_dev, D, sw), lambda i: (0, 0, 0))] + _after_specs(after),
        out_specs=[pl.BlockSpec((tm, N), lambda i: (i, 0)),
                   pl.BlockSpec((tm, D), lambda i: (i, 0)),
                   pl.BlockSpec((tm, LANES), lambda i: (i, 0)),
                   pl.BlockSpec((D, width), lambda i: (0, 0))],
        out_shape=[jax.ShapeDtypeStruct((S, N), BF16),
                   jax.ShapeDtypeStruct((S, D), BF16),
                   jax.ShapeDtypeStruct((S, LANES), F32),
                   jax.ShapeDtypeStruct((D, width), BF16)],
        compiler_params=_params("arbitrary"),
    )(x, gam, slots, *after)


def _fgate_fwd(z, bias, n_heads):
    S = z.shape[0]
    tb = min(TB, S)
    P = n_heads // 2

    def body(z_ref, b_ref, qaug_ref, kaug_ref):
        lane = lax.broadcasted_iota(jnp.int32, (tb, LANES), 1)
        tri = (lax.broadcasted_iota(jnp.int32, (tb, tb), 0)
               >= lax.broadcasted_iota(jnp.int32, (tb, tb), 1)).astype(F32)
        head = lax.broadcasted_iota(jnp.int32, (LANES, P * LANES), 0)
        col = lax.broadcasted_iota(jnp.int32, (LANES, P * LANES), 1)
        home = (head >> 1) * LANES + jnp.where((head & 1) == 0, HEAD_DIM, 0)
        is_head = head < n_heads
        place_q = [jnp.logical_and(is_head, col == home + SLOT_C + n).astype(BF16) for n in range(3)]
        place_k = [jnp.logical_and(is_head, col == home + SLOT_ONE + n).astype(BF16) for n in range(3)]
        slot = lax.broadcasted_iota(jnp.int32, (tb, P * LANES), 1) & (HEAD_DIM - 1)
        q_ones = jnp.logical_and(slot >= SLOT_ONE, slot < SLOT_ONE + 3).astype(F32)
        k_ones = jnp.logical_or(slot < SLOT_C + 3,
                                jnp.logical_and(slot >= SLOT_LSE, slot < SLOT_LSE + 3)).astype(F32)

        local = []
        for b in range(S // tb):
            zz = z_ref[b * tb:(b + 1) * tb, :] + b_ref[...]
            lf = jnp.minimum(zz, 0.0) - jnp.log(1.0 + jnp.exp(-jnp.abs(zz)))
            lf = jnp.where(lane < n_heads, lf, 0.0)
            local.append(jnp.dot(tri, lf, preferred_element_type=F32, precision=lax.Precision.HIGHEST))
        carry = jnp.zeros((1, LANES), F32)
        for b, part_sum in enumerate(local):
            c = part_sum + carry
            carry = c[tb - 1:tb, :]
            qa, ka = q_ones, k_ones
            for n, part in enumerate(_split3(c)):
                qa = qa + jnp.dot(part.astype(BF16), place_q[n], preferred_element_type=F32)
                ka = ka - jnp.dot(part.astype(BF16), place_k[n], preferred_element_type=F32)
            qaug_ref[b * tb:(b + 1) * tb, :] = qa.astype(BF16)
            kaug_ref[b * tb:(b + 1) * tb, :] = ka.astype(BF16)

    return pl.pallas_call(
        body, name="fgate_fwd",
        out_shape=[jax.ShapeDtypeStruct((S, P * LANES), BF16),
                   jax.ShapeDtypeStruct((S, P * LANES), BF16)],
        compiler_params=pltpu.CompilerParams(vmem_limit_bytes=VMEM_LIMIT),
    )(z, bias)


def _window_mean_minus_self(u, pad_ref, w, S):
    pad_ref[0:MAX_WINDOW, :] = jnp.zeros((MAX_WINDOW, LANES), F32)
    pad_ref[MAX_WINDOW:MAX_WINDOW + S, :] = u
    acc = u
    for j in range(1, w):
        acc = acc + pad_ref[MAX_WINDOW - j:MAX_WINDOW - j + S, :]
    t = lax.broadcasted_iota(jnp.int32, (S, LANES), 0)
    cnt = jnp.minimum(t + 1, w).astype(F32)
    return acc / cnt - u, cnt


def _pool_fwd(proj, pool_w, pool_scale):
    S = proj.shape[0]
    G = len(POOL_WINDOWS)

    def body(u_ref, w_ref, s_ref, y_ref, pad_ref):
        g = pl.program_id(0)
        for gi, w in enumerate(POOL_WINDOWS):
            @pl.when(g == gi)
            def _():
                d, _ = _window_mean_minus_self(u_ref[...].astype(F32), pad_ref, w, S)
                y = jnp.dot(d.astype(BF16), w_ref[0].astype(BF16), preferred_element_type=F32)
                y_ref[...] = (y * s_ref[...]).astype(BF16)

    return pl.pallas_call(
        body, name="pool_fwd", grid=(G,),
        in_specs=[pl.BlockSpec((S, LANES), lambda g: (0, g)),
                  pl.BlockSpec((1, LANES, LANES), lambda g: (g, 0, 0)),
                  pl.BlockSpec((1, LANES), lambda g: (0, g))],
        out_specs=pl.BlockSpec((S, LANES), lambda g: (0, g)),
        out_shape=jax.ShapeDtypeStruct((S, G * LANES), BF16),
        scratch_shapes=[pltpu.VMEM((S + MAX_WINDOW, LANES), F32)],
        compiler_params=_params("arbitrary"),
    )(proj, pool_w, pool_scale)


def _head_halves(rows):
    lane = lax.broadcasted_iota(jnp.int32, (rows, LANES), 1)
    return lane, (lane < HEAD_DIM, lane >= HEAD_DIM)


def _attn_fwd(proj, qaug, kaug):
    S = proj.shape[0]
    W = proj.shape[1] // 6
    P = W // LANES
    tq = min(TQ, S)
    nq = S // tq
    qc, kc, vc = 2 * P, 3 * P, 4 * P
    scale = 1.0 / math.sqrt(HEAD_DIM)

    def body(q_ref, k_ref, v_ref, qa_ref, ka_ref, o_ref, qb_ref, qm_scr, m_scr, acc_scr):
        i = pl.program_id(1)
        lane, halves = _head_halves(tq)
        v_ones = ((lane & (HEAD_DIM - 1)) < 3).astype(BF16)
        qs = q_ref[...] * scale
        qm_scr[0] = jnp.where(halves[0], qs, qa_ref[...])
        qm_scr[1] = jnp.where(halves[1], qs, qa_ref[...])
        m_scr[...] = jnp.full(m_scr.shape, NEG_INF, F32)
        acc_scr[...] = jnp.zeros(acc_scr.shape, F32)

        def update(j, on_diagonal):
            keys = pl.ds(pl.multiple_of(j * tq, tq), tq)
            k2, v2, kaug_t = k_ref[keys, :], v_ref[keys, :], ka_ref[keys, :]
            if on_diagonal:
                keep = (lax.broadcasted_iota(jnp.int32, (tq, tq), 0)
                        >= lax.broadcasted_iota(jnp.int32, (tq, tq), 1))
            logits = [lax.dot_general(qm_scr[a], jnp.where(halves[a], k2, kaug_t), NT, preferred_element_type=F32)
                      for a in range(2)]
            for a in range(2):
                s = jnp.where(keep, logits[a], NEG_INF) if on_diagonal else logits[a]
                va = jnp.where(halves[a], v2, v_ones)
                m_prev = m_scr[a]
                m_new = jnp.maximum(m_prev, jnp.max(s, axis=1, keepdims=True))
                p = jnp.exp(s - jnp.tile(m_new, (1, tq // LANES)))
                acc_scr[a] = jnp.exp(m_prev - m_new) * acc_scr[a] + jnp.dot(p.astype(BF16), va,
                                                                              preferred_element_type=F32)
                m_scr[a] = m_new

        def below_diagonal(jj, carry):
            update(2 * jj, False)
            update(2 * jj + 1, False)
            return carry

        lax.fori_loop(0, i // 2, below_diagonal, 0)

        @pl.when(i % 2 == 1)
        def _():
            update(i - 1, False)

        update(i, True)
        acc_a, acc_b = acc_scr[0], acc_scr[1]
        l_a, l_b = acc_a[:, HEAD_DIM:HEAD_DIM + 1], acc_b[:, 0:1]
        o_ref[...] = jnp.where(halves[0], acc_a / l_a, acc_b / l_b).astype(BF16)
        lse = jnp.where(halves[0], m_scr[1] + jnp.log(l_b), m_scr[0] + jnp.log(l_a))
        slot = lane & (HEAD_DIM - 1)
        aug = qa_ref[...].astype(F32)
        for n, part in enumerate(_split3(lse)):
            aug = jnp.where(slot == SLOT_LSE + n, -part, aug)
        qb_ref[...] = aug.astype(BF16)

    tile = lambda col: pl.BlockSpec((tq, LANES), lambda p, i: (i, col + p))
    whole = lambda col: pl.BlockSpec((S, LANES), lambda p, i: (0, col + p))
    return pl.pallas_call(
        body, name="attn_fwd", grid=(P, nq),
        in_specs=[tile(qc), whole(kc), whole(vc), tile(0), whole(0)],
        out_specs=[tile(0), tile(0)],
        out_shape=[jax.ShapeDtypeStruct((S, W), BF16), jax.ShapeDtypeStruct((S, W), BF16)],
        scratch_shapes=[pltpu.VMEM((2, tq, LANES), BF16),
                        pltpu.VMEM((2, tq, LANES), F32),
                        pltpu.VMEM((2, tq, LANES), F32)],
        compiler_params=_params("parallel", "arbitrary"),
    )(proj, proj, proj, qaug, kaug)


def _outproj_fwd(ypool, o, proj, x, wout):
    S, D = x.shape
    W = D // 2
    tm, tn = min(TM, S), TN

    def body(y_ref, o_ref, pg_ref, ag_ref, x_ref, w_ref, xn_ref, mix_ref):
        pg, ag = pg_ref[...].astype(F32), ag_ref[...].astype(F32)
        mix_ref[:, 0:W] = (y_ref[...].astype(F32) * (pg * _sigmoid(pg))).astype(BF16)
        mix_ref[:, W:D] = (o_ref[...].astype(F32) * (ag * _sigmoid(ag))).astype(BF16)
        for n in range(D // tn):
            cols = slice(n * tn, (n + 1) * tn)
            xn_ref[:, cols] = x_ref[:, cols] + jnp.dot(mix_ref[...], w_ref[:, cols], preferred_element_type=F32)

    return pl.pallas_call(
        body, name="outproj_fwd", grid=(S // tm,),
        in_specs=[pl.BlockSpec((tm, W), lambda i: (i, 0)),
                  pl.BlockSpec((tm, W), lambda i: (i, 0)),
                  pl.BlockSpec((tm, W), lambda i: (i, 1)),
                  pl.BlockSpec((tm, W), lambda i: (i, 5)),
                  pl.BlockSpec((tm, D), lambda i: (i, 0)),
                  pl.BlockSpec((D, D), lambda i: (0, 0))],
        out_specs=[pl.BlockSpec((tm, D), lambda i: (i, 0)),
                   pl.BlockSpec((tm, D), lambda i: (i, 0))],
        out_shape=[jax.ShapeDtypeStruct((S, D), F32),
                   jax.ShapeDtypeStruct((S, D), BF16)],
        compiler_params=_params("parallel"),
    )(ypool, o, proj, proj, x, wout)


def _loss_head(x, gam, target):
    S, D = x.shape
    tm = min(TM, S)

    def body(x_ref, g_ref, t_ref, dx_ref, loss_ref, dg_ref):
        @pl.when(pl.program_id(0) == 0)
        def _():
            loss_ref[...] = jnp.zeros(loss_ref.shape, F32)
            dg_ref[...] = jnp.zeros(dg_ref.shape, F32)

        xf, gam_v = x_ref[...], g_ref[...]
        r = lax.rsqrt(jnp.mean(xf * xf, axis=-1, keepdims=True) + RMS_EPS)
        xhat = xf * r
        err = xhat * gam_v - t_ref[...]
        part = jnp.sum(jnp.sum(err * err, axis=-1, keepdims=True), axis=0, keepdims=True)
        loss_ref[...] += part * (0.5 / D)
        dy = err * (1.0 / D)
        dg_ref[...] += jnp.sum(dy * xhat, axis=0, keepdims=True)
        dxhat = dy * gam_v
        dx_ref[...] = r * (dxhat - xhat * jnp.mean(dxhat * xhat, axis=-1, keepdims=True))

    return pl.pallas_call(
        body, name="loss_head", grid=(S // tm,),
        in_specs=[pl.BlockSpec((tm, D), lambda i: (i, 0)),
                  pl.BlockSpec((1, D), lambda i: (0, 0)),
                  pl.BlockSpec((tm, D), lambda i: (i, 0))],
        out_specs=[pl.BlockSpec((tm, D), lambda i: (i, 0)),
                   pl.BlockSpec((8, LANES), lambda i: (0, 0)),
                   pl.BlockSpec((1, D), lambda i: (0, 0))],
        out_shape=[jax.ShapeDtypeStruct((S, D), F32),
                   jax.ShapeDtypeStruct((8, LANES), F32),
                   jax.ShapeDtypeStruct((1, D), F32)],
        compiler_params=_params("arbitrary"),
    )(x, gam, target)


def _outproj_bwd(g, wout, mixed, ypool, o, proj):
    S, D = g.shape
    W = D // 2
    tm = min(TM, S)

    def body(g_ref, w_ref, mix_ref, y_ref, o_ref, pg_ref, ag_ref, dw_ref, da_ref, dgate_ref, doaug_ref):
        @pl.when(pl.program_id(0) == 0)
        def _():
            dw_ref[...] = jnp.zeros(dw_ref.shape, F32)

        gb = g_ref[...].astype(BF16)
        dw_ref[...] += lax.dot_general(mix_ref[...], gb, TN_DIMS, preferred_element_type=F32)
        for half, (val_ref, gate_ref) in enumerate(((y_ref, pg_ref), (o_ref, ag_ref))):
            cols = slice(half * W, (half + 1) * W)
            dmix = lax.dot_general(gb, w_ref[cols, :], NT, preferred_element_type=F32)
            gt = gate_ref[...].astype(F32)
            sg = _sigmoid(gt)
            da_ref[:, cols] = (dmix * (gt * sg)).astype(BF16)
            dgate_ref[:, cols] = (dmix * val_ref[...].astype(F32) * (sg * (1.0 + gt * (1.0 - sg)))).astype(BF16)

        lane, halves = _head_halves(tm)
        slot = lane & (HEAD_DIM - 1)
        for p in range(W // LANES):
            cols = slice(p * LANES, (p + 1) * LANES)
            prod = da_ref[:, W + p * LANES:W + (p + 1) * LANES].astype(F32) * o_ref[:, cols].astype(F32)
            d_a = jnp.sum(jnp.where(halves[0], prod, 0.0), axis=1, keepdims=True)
            d_b = jnp.sum(jnp.where(halves[1], prod, 0.0), axis=1, keepdims=True)
            aug = jnp.zeros((tm, LANES), F32)
            for n, part in enumerate(_split3(jnp.where(halves[0], d_b, d_a))):
                aug = jnp.where(slot == SLOT_C + n, -part, aug)
            doaug_ref[:, cols] = aug.astype(BF16)

    rows = lambda width, col: pl.BlockSpec((tm, width), lambda i: (i, col))
    return pl.pallas_call(
        body, name="outproj_bwd", grid=(S // tm,),
        in_specs=[rows(D, 0), pl.BlockSpec((D, D), lambda i: (0, 0)), rows(D, 0), rows(W, 0), rows(W, 0),
                  rows(W, 1), rows(W, 5)],
        out_specs=[pl.BlockSpec((D, D), lambda i: (0, 0)), rows(D, 0), rows(D, 0), rows(W, 0)],
        out_shape=[jax.ShapeDtypeStruct((D, D), F32),
                   jax.ShapeDtypeStruct((S, D), BF16),
                   jax.ShapeDtypeStruct((S, D), BF16),
                   jax.ShapeDtypeStruct((S, W), BF16)],
        compiler_params=_params("arbitrary"),
    )(g, wout, mixed, ypool, o, proj, proj)


def _section_specs(sections, rows, width):
    specs = [pl.BlockSpec((rows, width), lambda k, c=c: (k, c)) for _, c in sections]
    return specs, [a for a, _ in sections]


def _inproj_bwd_dw(h, sections, dzf):
    S, D = h.shape
    W = D // 2
    n_sec = len(sections)
    ts = min(TM, S)

    def body(h_ref, dz_ref, *rest):
        sec_refs, (dw_ref, dwf_ref) = rest[:n_sec], rest[n_sec:]

        @pl.when(pl.program_id(0) == 0)
        def _():
            dw_ref[...] = jnp.zeros(dw_ref.shape, F32)
            dwf_ref[...] = jnp.zeros(dwf_ref.shape, F32)

        ht = h_ref[...].T
        dwf_ref[...] += jnp.dot(ht, dz_ref[...], preferred_element_type=F32)
        for n, ref in enumerate(sec_refs):
            dw_ref[:, n * W:(n + 1) * W] += jnp.dot(ht, ref[...], preferred_element_type=F32)

    sec_specs, sec_arrays = _section_specs(sections, ts, W)
    return pl.pallas_call(
        body, name="inproj_bwd_dw", grid=(S // ts,),
        in_specs=[pl.BlockSpec((ts, D), lambda k: (k, 0)),
                  pl.BlockSpec((ts, LANES), lambda k: (k, 0))] + sec_specs,
        out_specs=[pl.BlockSpec((D, n_sec * W), lambda k: (0, 0)),
                   pl.BlockSpec((D, LANES), lambda k: (0, 0))],
        out_shape=[jax.ShapeDtypeStruct((D, n_sec * W), F32),
                   jax.ShapeDtypeStruct((D, LANES), F32)],
        compiler_params=_params("arbitrary"),
    )(h, dzf, *sec_arrays)


def _attn_bwd(proj, da, qaug, kaug, doaug):
    S = proj.shape[0]
    W = proj.shape[1] // 6
    P = W // LANES
    tq = min(TQ, S)
    nq = S // tq
    qc, kc, vc = 2 * P, 3 * P, 4 * P
    scale = 1.0 / math.sqrt(HEAD_DIM)

    def body(q_ref, k_ref, v_ref, do_ref, qa_ref, ka_ref, da_ref,
             dq_ref, dk_ref, dv_ref, dqx_ref, dkx_ref, km_scr, vm_scr, dk_scr, dv_scr, dq_scr):
        j = pl.program_id(1)
        lane, halves = _head_halves(tq)

        @pl.when(j == 0)
        def _():
            dq_scr[...] = jnp.zeros(dq_scr.shape, F32)

        v_ones = ((lane & (HEAD_DIM - 1)) < 3).astype(BF16)
        for a in range(2):
            km_scr[a] = jnp.where(halves[a], k_ref[...], ka_ref[...])
            vm_scr[a] = jnp.where(halves[a], v_ref[...], v_ones)
        dk_scr[...] = jnp.zeros(dk_scr.shape, F32)
        dv_scr[...] = jnp.zeros(dv_scr.shape, F32)

        def update(i, on_diagonal):
            rows = pl.ds(pl.multiple_of(i * tq, tq), tq)
            qs = q_ref[rows, :] * scale
            do2, qaug_t, doaug_t = do_ref[rows, :], qa_ref[rows, :], da_ref[rows, :]
            if on_diagonal:
                keep = (lax.broadcasted_iota(jnp.int32, (tq, tq), 0)
                        >= lax.broadcasted_iota(jnp.int32, (tq, tq), 1))
            qas = [jnp.where(halves[a], qs, qaug_t) for a in range(2)]
            logits = [lax.dot_general(qas[a], km_scr[a], NT, preferred_element_type=F32) for a in range(2)]
            dps = [lax.dot_general(jnp.where(halves[a], do2, doaug_t), vm_scr[a], NT, preferred_element_type=F32)
                   for a in range(2)]
            dv = None
            for a in range(2):
                s = jnp.where(keep, logits[a], NEG_INF) if on_diagonal else logits[a]
                p = jnp.exp(s)
                dsb = (p * dps[a]).astype(BF16)
                do0 = jnp.where(halves[a], do2, jnp.zeros_like(do2))
                dv_a = lax.dot_general(p.astype(BF16), do0, TN_DIMS, preferred_element_type=F32)
                dv = dv_a if dv is None else dv + dv_a
                dk_scr[a] += lax.dot_general(dsb, qas[a], TN_DIMS, preferred_element_type=F32)
                dq_scr[a, rows, :] += jnp.dot(dsb, km_scr[a], preferred_element_type=F32)
            dv_scr[...] += dv

        def below_diagonal(n, carry):
            update(j + 1 + 2 * n, False)
            update(j + 2 + 2 * n, False)
            return carry

        update(j, True)
        below = nq - 1 - j
        lax.fori_loop(0, below // 2, below_diagonal, 0)

        @pl.when(below % 2 == 1)
        def _():
            update(nq - 1, False)


        dk_ref[...] = jnp.where(halves[0], dk_scr[0], dk_scr[1]).astype(BF16)
        dkx_ref[...] = jnp.where(halves[0], dk_scr[1], dk_scr[0])
        dv_ref[...] = dv_scr[...].astype(BF16)

        @pl.when(j == nq - 1)
        def _():
            row_lane, row_halves = _head_halves(S)
            dq_ref[...] = (jnp.where(row_halves[0], dq_scr[0], dq_scr[1]) * scale).astype(BF16)
            dqx_ref[...] = jnp.where(row_halves[0], dq_scr[1], dq_scr[0])

    tile = lambda col: pl.BlockSpec((tq, LANES), lambda p, j: (j, col + p))
    whole = lambda col: pl.BlockSpec((S, LANES), lambda p, j: (0, col + p))
    return pl.pallas_call(
        body, name="attn_bwd", grid=(P, nq),
        in_specs=[whole(qc), tile(kc), tile(vc), whole(P), whole(0), tile(0), whole(0)],
        out_specs=[whole(0), tile(0), tile(0), whole(0), tile(0)],
        out_shape=[jax.ShapeDtypeStruct((S, W), BF16),
                   jax.ShapeDtypeStruct((S, W), BF16),
                   jax.ShapeDtypeStruct((S, W), BF16),
                   jax.ShapeDtypeStruct((S, W), F32),
                   jax.ShapeDtypeStruct((S, W), F32)],
        scratch_shapes=[pltpu.VMEM((2, tq, LANES), BF16),
                        pltpu.VMEM((2, tq, LANES), BF16),
                        pltpu.VMEM((2, tq, LANES), F32),
                        pltpu.VMEM((tq, LANES), F32),
                        pltpu.VMEM((2, S, LANES), F32)],
        compiler_params=_params("parallel", "arbitrary"),
    )(proj, proj, proj, da, qaug, kaug, doaug)


def _fgate_bwd(dqx, dkx, z, bias, n_heads):
    S = z.shape[0]
    tb = min(TB, S)
    nb = S // tb

    def body(dqx_ref, dkx_ref, z_ref, b_ref, dz_ref, db_ref):
        tri = (lax.broadcasted_iota(jnp.int32, (tb, tb), 1)
               >= lax.broadcasted_iota(jnp.int32, (tb, tb), 0)).astype(F32)
        n_cols = dqx_ref.shape[1]
        col = lax.broadcasted_iota(jnp.int32, (n_cols, LANES), 0)
        head = lax.broadcasted_iota(jnp.int32, (n_cols, LANES), 1)
        home = (head >> 1) * LANES + jnp.where((head & 1) == 0, HEAD_DIM, 0)
        is_head = head < n_heads
        pick_rows = jnp.logical_and(is_head, col == home + SLOT_C).astype(F32)
        pick_cols = jnp.logical_and(is_head, col == home + SLOT_ONE).astype(F32)

        local = []
        for b in range(nb):
            rows = slice(b * tb, (b + 1) * tb)
            dc = (jnp.dot(dqx_ref[rows, :], pick_rows, preferred_element_type=F32, precision=lax.Precision.HIGHEST)
                  - jnp.dot(dkx_ref[rows, :], pick_cols, preferred_element_type=F32, precision=lax.Precision.HIGHEST))
            local.append(jnp.dot(tri, dc, preferred_element_type=F32, precision=lax.Precision.HIGHEST))
        carry = jnp.zeros((1, LANES), F32)
        db = jnp.zeros((1, LANES), F32)
        for b in reversed(range(nb)):
            rows = slice(b * tb, (b + 1) * tb)
            rc = local[b] + carry
            carry = rc[0:1, :]
            dz = rc * _sigmoid(-(z_ref[rows, :] + b_ref[...]))
            dz_ref[rows, :] = dz.astype(BF16)
            db = db + jnp.sum(dz, axis=0, keepdims=True)
        db_ref[...] = db

    return pl.pallas_call(
        body, name="fgate_bwd",
        out_shape=[jax.ShapeDtypeStruct((S, LANES), BF16),
                   jax.ShapeDtypeStruct((1, LANES), F32)],
        compiler_params=pltpu.CompilerParams(vmem_limit_bytes=VMEM_LIMIT),
    )(dqx, dkx, z, bias)


def _pool_bwd(proj, da, pool_w, pool_scale):
    S = proj.shape[0]
    G = len(POOL_WINDOWS)

    def body(u_ref, dy_ref, w_ref, s_ref, du_ref, dw_ref, ds_ref, pad_ref):
        g = pl.program_id(0)
        for gi, w in enumerate(POOL_WINDOWS):
            @pl.when(g == gi)
            def _():
                d, cnt = _window_mean_minus_self(u_ref[...].astype(F32), pad_ref, w, S)
                db = d.astype(BF16)
                wb = w_ref[0].astype(BF16)
                yraw = jnp.dot(db, wb, preferred_element_type=F32)
                dy = dy_ref[...].astype(F32)
                ds_ref[...] = jnp.sum(dy * yraw, axis=0, keepdims=True)
                dzb = (dy * s_ref[...]).astype(BF16)
                dw_ref[0] = lax.dot_general(db, dzb, TN_DIMS, preferred_element_type=F32)
                dd = lax.dot_general(dzb, wb, NT, preferred_element_type=F32)
                pad_ref[0:S, :] = dd / cnt
                pad_ref[S:S + MAX_WINDOW, :] = jnp.zeros((MAX_WINDOW, LANES), F32)
                acc = -dd
                for j in range(w):
                    acc = acc + pad_ref[j:j + S, :]
                du_ref[...] = acc.astype(BF16)

    return pl.pallas_call(
        body, name="pool_bwd", grid=(G,),
        in_specs=[pl.BlockSpec((S, LANES), lambda g: (0, g)),
                  pl.BlockSpec((S, LANES), lambda g: (0, g)),
                  pl.BlockSpec((1, LANES, LANES), lambda g: (g, 0, 0)),
                  pl.BlockSpec((1, LANES), lambda g: (0, g))],
        out_specs=[pl.BlockSpec((S, LANES), lambda g: (0, g)),
                   pl.BlockSpec((1, LANES, LANES), lambda g: (g, 0, 0)),
                   pl.BlockSpec((1, LANES), lambda g: (0, g))],
        out_shape=[jax.ShapeDtypeStruct((S, G * LANES), BF16),
                   jax.ShapeDtypeStruct((G, LANES, LANES), F32),
                   jax.ShapeDtypeStruct((1, G * LANES), F32)],
        scratch_shapes=[pltpu.VMEM((S + MAX_WINDOW, LANES), F32)],
        compiler_params=_params("arbitrary"),
    )(proj, da, pool_w, pool_scale)


def _inproj_bwd_dx(sections, dzf, w, x, gam, g, after=()):
    S, D = x.shape
    N = w.shape[1] - LANES
    W = D // 2
    n_sec = len(sections)
    tm = min(TM // 2, S)

    def body(dz_ref, w_ref, wf_ref, x_ref, gam_ref, g_ref, *rest):
        sec_refs = rest[:n_sec]
        dx_ref, dg_ref = rest[-2:]

        @pl.when(pl.program_id(0) == 0)
        def _():
            dg_ref[...] = jnp.zeros(dg_ref.shape, F32)

        dh = lax.dot_general(dz_ref[...], wf_ref[...], NT, preferred_element_type=F32)
        for n, ref in enumerate(sec_refs):
            dh = dh + lax.dot_general(ref[...], w_ref[:, n * W:(n + 1) * W], NT, preferred_element_type=F32)
        xf = x_ref[...]
        r = lax.rsqrt(jnp.mean(xf * xf, axis=-1, keepdims=True) + RMS_EPS)
        xhat = xf * r
        dg_ref[...] += jnp.sum(dh * xhat, axis=0, keepdims=True)
        dxhat = dh * gam_ref[...]
        dx_ref[...] = g_ref[...] + r * (dxhat - xhat * jnp.mean(dxhat * xhat, axis=-1, keepdims=True))

    sec_specs, sec_arrays = _section_specs(sections, tm, W)
    return pl.pallas_call(
        body, name="inproj_bwd_dx", grid=(S // tm,),
        in_specs=[pl.BlockSpec((tm, LANES), lambda i: (i, 0)),
                  pl.BlockSpec((D, N), lambda i: (0, 0)),
                  pl.BlockSpec((D, LANES), lambda i: (0, N // LANES)),
                  pl.BlockSpec((tm, D), lambda i: (i, 0)),
                  pl.BlockSpec((1, D), lambda i: (0, 0)),
                  pl.BlockSpec((tm, D), lambda i: (i, 0))] + sec_specs + _after_specs(after),
        out_specs=[pl.BlockSpec((tm, D), lambda i: (i, 0)),
                   pl.BlockSpec((1, D), lambda i: (0, 0))],
        out_shape=[jax.ShapeDtypeStruct((S, D), F32),
                   jax.ShapeDtypeStruct((1, D), F32)],
        compiler_params=_params("arbitrary"),
    )(dzf, w, w, x, gam, g, *sec_arrays, *after)


def _adamw(w, m, v, gsets, name, rows):
    A, R, C = w.shape
    tr = min(rows, R)
    c1 = 1.0 / (1.0 - ADAM_B1 ** ADAM_STEP)
    c2 = 1.0 / (1.0 - ADAM_B2 ** ADAM_STEP)
    counts = [len(gs) for gs in gsets]

    def body(w_ref, m_ref, v_ref, *rest):
        g_ref, d_ref, nm_ref, nv_ref = rest[-4:]
        at = 0
        for a in range(A):
            part_refs = rest[at:at + counts[a]]
            at += counts[a]

            @pl.when(pl.program_id(0) == a)
            def _():
                g = None
                for ref in part_refs:
                    for s in range(ref.shape[0]):
                        term = ref[s].astype(F32)
                        g = term if g is None else g + term
                nm = ADAM_B1 * m_ref[0] + (1.0 - ADAM_B1) * g
                nv = ADAM_B2 * v_ref[0] + (1.0 - ADAM_B2) * (g * g)
                g_ref[0] = g
                nm_ref[0] = nm
                nv_ref[0] = nv
                d_ref[0] = -ADAM_LR * ((nm * c1) / (jnp.sqrt(nv * c2) + ADAM_EPS) + ADAM_WD * w_ref[0])

    spec = pl.BlockSpec((1, tr, C), lambda a, r: (a, r, 0))
    part_specs = [pl.BlockSpec((part.shape[0], tr, C), lambda a, r, l=l: (0, jnp.where(a == l, r, 0), 0))
                  for l, gs in enumerate(gsets) for part in gs]
    shape = jax.ShapeDtypeStruct((A, R, C), F32)
    return pl.pallas_call(
        body, name=name, grid=(A, R // tr),
        in_specs=[spec, spec, spec] + part_specs,
        out_specs=[spec, spec, spec, spec],
        out_shape=[shape, shape, shape, shape],
        compiler_params=_params("arbitrary", "arbitrary"),
    )(w, m, v, *[part for gs in gsets for part in gs])


def _position():
    return lax.axis_index("x"), lax.axis_index("y"), lax.axis_index("c")


def _index(dev):
    return 4 * dev[0] + 2 * dev[1] + dev[2]


def _all_gather(arrs, slots, out_shapes, name):
    n_arr = len(arrs)

    def body(*refs):
        ins, outs = refs[:n_arr], refs[n_arr:2 * n_arr]
        send_sems, recv_sems, local_sems = refs[2 * n_arr:]
        x, y, c = _position()
        me, sibling = (x, y, c), (x, y, 1 - c)
        chips = [(1 - x, y), (x, 1 - y), (1 - x, 1 - y)]

        def copy(a, k, block, to, src=None):
            part = slots[a](outs[a], _index(block))
            return pltpu.make_async_remote_copy(
                src_ref=part if src is None else src, dst_ref=part,
                send_sem=send_sems.at[a, k], recv_sem=recv_sems.at[a, k],
                device_id=to, device_id_type=MESH)

        mine = [pltpu.make_async_copy(ins[a], slots[a](outs[a], _index(me)), local_sems.at[a])
                for a in range(n_arr)]
        for cp in mine:
            cp.start()
        first = []
        for a in range(n_arr):
            first.append(copy(a, 0, me, sibling, src=ins[a]))
            first += [copy(a, 1 + j, me, (*chip, c), src=ins[a]) for j, chip in enumerate(chips)]
        for cp in first:
            cp.start()
        passed = []
        for j, chip in enumerate(chips):
            for a in range(n_arr):
                copy(a, 1 + j, (*chip, c), me).wait_recv()
                fwd = copy(a, 4 + j, (*chip, c), sibling)
                fwd.start()
                passed.append(fwd)
        for a in range(n_arr):
            copy(a, 0, sibling, me).wait_recv()
            for j, chip in enumerate(chips):
                copy(a, 4 + j, (*chip, 1 - c), me).wait_recv()
        for cp in first + passed:
            cp.wait_send()
        for cp in mine:
            cp.wait()

    any_spec = pl.BlockSpec(memory_space=pl.ANY)
    return pl.pallas_call(
        body, name=name,
        in_specs=[any_spec] * n_arr, out_specs=[any_spec] * n_arr, out_shape=out_shapes,
        scratch_shapes=[pltpu.SemaphoreType.DMA((n_arr, 7)), pltpu.SemaphoreType.DMA((n_arr, 7)),
                        pltpu.SemaphoreType.DMA((n_arr,))],
    )(*arrs)


def _split_copies(srcs, lands, send_sems, recv_sems, kinds):
    x, y, c = _position()
    me = _index((x, y, c))
    copies = []
    for a, (src_part, land_part) in enumerate(kinds):
        for k in range(1, N_DEV):
            peer = (x ^ ((k >> 2) & 1), y ^ ((k >> 1) & 1), c ^ (k & 1))
            copies.append(pltpu.make_async_remote_copy(
                src_ref=src_part(srcs[a], _index(peer)), dst_ref=land_part(lands[a], me, k),
                send_sem=send_sems[a].at[k - 1], recv_sem=recv_sems[a].at[k - 1],
                device_id=peer, device_id_type=MESH))
    return copies


def _split_start(srcs, lands, kinds, name, after=()):
    n = len(srcs)

    def body(*refs):
        src_refs, land_refs = refs[:n], refs[n:2 * n]
        outs = refs[2 * n + len(after):]
        send_sems, recv_sems = outs[:n], outs[n:2 * n]
        token = outs[-1]
        for cp in _split_copies(src_refs, land_refs, send_sems, recv_sems, kinds):
            cp.start()
        token[...] = jnp.zeros(token.shape, token.dtype)

    hbm = pl.BlockSpec(memory_space=pltpu.HBM)
    sem = pl.BlockSpec(memory_space=pltpu.SEMAPHORE)
    operands = [pltpu.with_memory_space_constraint(t, pltpu.HBM) for t in (*srcs, *lands)]
    out = pl.pallas_call(
        body, name=name,
        in_specs=[hbm] * (2 * n) + _after_specs(after),
        out_specs=[sem] * (2 * n) + [hbm] * (2 * n) + [pl.BlockSpec(memory_space=pltpu.VMEM)],
        out_shape=[pltpu.SemaphoreType.DMA((N_DEV - 1,))] * (2 * n)
        + [pltpu.HBM(t.shape, t.dtype) for t in operands] + [jax.ShapeDtypeStruct((8, LANES), F32)],
        input_output_aliases={i: 2 * n + i for i in range(2 * n)},
        compiler_params=pltpu.CompilerParams(has_side_effects=pltpu.SideEffectType.DATAFLOW_SIDE_EFFECTING),
    )(*operands, *after)
    return [(out[a], out[n + a], out[2 * n + a], out[3 * n + a]) for a in range(n)], out[-1]


def _split_wait(started, kinds, after, name):
    n = len(started)
    sems = [t[0] for t in started] + [t[1] for t in started]
    srcs = [t[2] for t in started]
    lands = [t[3] for t in started]

    def body(*refs):
        src_refs, land_refs = refs[:n], refs[n:2 * n]
        send_sems, recv_sems = refs[2 * n:3 * n], refs[3 * n:4 * n]
        for cp in _split_copies(src_refs, land_refs, send_sems, recv_sems, kinds):
            cp.wait_send()
            cp.wait_recv()

    hbm = pl.BlockSpec(memory_space=pltpu.HBM)
    sem = pl.BlockSpec(memory_space=pltpu.SEMAPHORE)
    out = pl.pallas_call(
        body, name=name,
        in_specs=[hbm] * (2 * n) + [sem] * (2 * n) + _after_specs(after),
        out_specs=[hbm] * (2 * n),
        out_shape=[pltpu.HBM(t.shape, t.dtype) for t in (*srcs, *lands)],
        input_output_aliases={i: i for i in range(2 * n)},
        compiler_params=pltpu.CompilerParams(has_side_effects=pltpu.SideEffectType.DATAFLOW_SIDE_EFFECTING),
    )(*srcs, *lands, *sems, *after)
    return out[n:]


def _as_rows(p):
    if p.size % LANES == 0:
        rows = p.reshape(-1, LANES)
    else:
        rows = p.reshape(-1, p.shape[-1])
        rows = jnp.pad(rows, ((0, 0), (0, LANES - rows.shape[1])))
    return jnp.pad(rows, ((0, -rows.shape[0] % 8), (0, 0)))


def _pack(parts):
    return jnp.concatenate([_as_rows(p) for p in parts])[None]


def _unpack(packed, like):
    out, at = [], 0
    for p in like:
        whole = p.size % LANES == 0
        n = p.size // LANES if whole else p.size // p.shape[-1]
        rows = packed[0, at:at + n]
        out.append((rows if whole else rows[:, :p.shape[-1]]).reshape(p.shape))
        at += n + (-n % 8)
    return out


def _local_step(x, target, norm_g, forget_bias, pool_w, pool_scale, final_g, weights_in, weights_out, on_grads,
                first_after=()):
    L = norm_g.shape[0]
    S, D = x.shape
    W = D // 2
    H = W // HEAD_DIM
    bias = jnp.pad(forget_bias, ((0, 0), (0, LANES - H)))

    saved = []
    after = tuple(first_after)
    for l in range(L):
        proj, h, z, w = _inproj_fwd(x, norm_g[l:l + 1], weights_in(l, x), after)
        after = ()
        qaug, kaug = _fgate_fwd(z, bias[l:l + 1], H)
        ypool = _pool_fwd(proj, pool_w[l], pool_scale[l:l + 1])
        o, qaug_b = _attn_fwd(proj, qaug, kaug)
        wout = weights_out(l, o)
        x_new, mixed = _outproj_fwd(ypool, o, proj, x, wout)
        saved.append((x, proj, h, z, qaug_b, kaug, ypool, o, mixed, w, wout))
        x = x_new

    g, loss, d_final_g = _loss_head(x, final_g.reshape(1, D), target)

    d_norm_g, d_bias, d_pool_w, d_pool_scale = [], [], [], []
    for l in reversed(range(L)):
        x_in, proj, h, z, qaug_b, kaug, ypool, o, mixed, w, wout = saved[l]
        d_wout, da, dgate, doaug = _outproj_bwd(g, wout, mixed, ypool, o, proj)
        dq, dk, dv, dqx, dkx = _attn_bwd(proj, da, qaug_b, kaug, doaug)
        dzf, db = _fgate_bwd(dqx, dkx, z, bias[l:l + 1], H)
        dpu, dpw, dps = _pool_bwd(proj, da, pool_w[l], pool_scale[l:l + 1])
        dproj = [(dpu, 0), (dgate, 0), (dq, 0), (dk, 0), (dv, 0), (dgate, 1)]
        d_wmain, d_wf = _inproj_bwd_dw(h, dproj, dzf)
        after = tuple(on_grads(l, d_wmain, d_wf[:, :H], d_wout))
        g, dgam = _inproj_bwd_dx(dproj, dzf, w, x_in, norm_g[l:l + 1], g, after)
        d_norm_g.append(dgam[0])
        d_bias.append(db[0, :H])
        d_pool_w.append(dpw)
        d_pool_scale.append(dps[0])

    stack = lambda parts: jnp.stack(parts[::-1])
    grads = dict(norm_g=stack(d_norm_g), forget_bias=stack(d_bias), pool_w=stack(d_pool_w),
                 pool_scale=stack(d_pool_scale), final_g=d_final_g[0])
    return loss[0, 0], g, grads


def kernel(x, norm_g, w_in, forget_bias, pool_w, pool_scale, w_out, final_g, loss_target, m_norm_g, m_w_in, m_forget_bias, m_pool_w, m_pool_scale, m_w_out, m_final_g, v_norm_g, v_w_in, v_forget_bias, v_pool_w, v_pool_scale, v_w_out, v_final_g):
    L, D, cols = w_in.shape
    rows_out = w_out.shape[1]
    W = D // 2
    H = W // HEAD_DIM
    me = _index(_position())

    slot = _slot_width(cols)
    wout_b = w_out.astype(BF16)
    win_b = [lax.dynamic_update_slice(jnp.zeros((D, slot), BF16), w_in[l].astype(BF16),
                                      (0, me * (cols % LANES))) for l in range(L)]
    gather_in = (lambda ref, peer: ref, lambda ref, mine, k: ref.at[mine])
    gather_out = (lambda ref, peer: ref, lambda ref, mine, k: ref.at[pl.ds(mine * rows_out, rows_out), :])

    (first_in,) = _all_gather([win_b[0]], [lambda ref, n: ref.at[n]],
                              [jax.ShapeDtypeStruct((N_DEV, D, slot), BF16)], "gather_first")
    rest_srcs = [wout_b[0]] + [w[l] for l in range(1, L) for w in (win_b, wout_b)]
    rest_lands = [jnp.tile(wout_b[0], (N_DEV, 1))]
    for l in range(1, L):
        rest_lands += [jnp.broadcast_to(win_b[l][None], (N_DEV, D, slot)), jnp.tile(wout_b[l], (N_DEV, 1))]
    rest_kinds = [gather_out] + [gather_in, gather_out] * (L - 1)
    rest, rest_token = _split_start(rest_srcs, rest_lands, rest_kinds, "gather_start_rest", (first_in,))

    def weights_in(l, x_in):
        if l == 0:
            return first_in
        (win_all,) = _split_wait([rest[2 * l - 1]], [gather_in], (x_in,), f"gather_wait_in_{l}")
        return win_all

    def weights_out(l, o):
        (wout_full,) = _split_wait([rest[2 * l]], [gather_out], (o,), f"gather_wait_out_{l}")
        return wout_full

    exchange_kinds = [(lambda ref, peer: ref.at[peer], lambda ref, mine, k: ref.at[k - 1]),
                      (lambda ref, peer: ref.at[pl.ds(peer * rows_out, rows_out), :],
                       lambda ref, mine, k: ref.at[k - 1])]
    exchanges, own_parts = {}, {}

    def on_grads(l, d_wmain, d_wf, d_wout):
        dw_in = jnp.concatenate([d_wmain, d_wf], axis=1)
        own_parts[l] = (lax.dynamic_slice_in_dim(dw_in, me * cols, cols, 1)[None],
                        lax.dynamic_slice_in_dim(d_wout, me * rows_out, rows_out, 0)[None])
        srcs = [jnp.transpose(dw_in.astype(BF16).reshape(D, N_DEV, cols), (1, 0, 2)), d_wout.astype(BF16)]
        lands = [lax.empty((N_DEV - 1, D, cols), BF16), lax.empty((N_DEV - 1, rows_out, D), BF16)]
        exchanges[l], token = _split_start(srcs, lands, exchange_kinds, f"exchange_start_{l}")
        return (token,)

    loss, dx, grads = _local_step(x[0], loss_target[0], norm_g, forget_bias, pool_w, pool_scale, final_g,
                                  weights_in, weights_out, on_grads, (rest_token,))
    loss = lax.psum(loss, ("x", "y", "c"))

    names = ["norm_g", "forget_bias", "pool_w", "pool_scale", "final_g"]
    small_w = [norm_g, forget_bias, pool_w, pool_scale, final_g]
    small_m = [m_norm_g, m_forget_bias, m_pool_w, m_pool_scale, m_final_g]
    small_v = [v_norm_g, v_forget_bias, v_pool_w, v_pool_scale, v_final_g]
    gs = _pack([grads[n] for n in names])
    rows = gs.shape[1]
    (gs_all,) = _all_gather([gs], [lambda ref, n: ref.at[n]],
                            [jax.ShapeDtypeStruct((N_DEV, 1, rows, LANES), F32)], "gather_small_grads")
    packed = _adamw(_pack(small_w), _pack(small_m), _pack(small_v),
                    [[gs_all.reshape(N_DEV, rows, LANES)]], "adamw_small", rows)
    g_s, d_s, nm_s, nv_s = [_unpack(p, small_w) for p in packed]

    gin_sets, gout_sets = [], []
    for l in range(L):
        got_in, got_out = _split_wait(exchanges[l], exchange_kinds, (dx, packed[1]), f"exchange_wait_{l}")
        gin_sets.append([own_parts[l][0], got_in])
        gout_sets.append([own_parts[l][1], got_out])
    g_w_in, d_w_in, nm_w_in, nv_w_in = _adamw(w_in, m_w_in, v_w_in, gin_sets, "adamw_w_in", 256)
    g_w_out, d_w_out, nm_w_out, nv_w_out = _adamw(w_out, m_w_out, v_w_out, gout_sets, "adamw_w_out", 128)

    def order(big_in, big_out, small):
        return (small[0], big_in, small[1], small[2], small[3], big_out, small[4])

    return (loss, dx[None], *order(g_w_in, g_w_out, g_s), *order(d_w_in, d_w_out, d_s),
            *order(nm_w_in, nm_w_out, nm_s), *order(nv_w_in, nv_w_out, nv_s))
```

```python
import math

import jax
import jax.numpy as jnp
from jax import lax
from jax.experimental import pallas as pl
from jax.experimental.pallas import tpu as pltpu

F32 = jnp.float32
BF16 = jnp.bfloat16
MESH = pl.DeviceIdType.MESH

RMS_EPS = 1e-6
NEG_INF = -1e30
HEAD_DIM = 64
POOL_WINDOWS = (2, 4, 8, 16)
MAX_WINDOW = 16
LANES = 128
N_DEV = 8

ADAM_LR = 0.001
ADAM_B1 = 0.9
ADAM_B2 = 0.999
ADAM_EPS = 1e-08
ADAM_WD = 0.01
ADAM_STEP = 10

TM = 512
TN = 512
TQ = 512
TB = 256
VMEM_LIMIT = 56 * 1024 * 1024

NT = (((1,), (1,)), ((), ()))
TN_DIMS = (((0,), (0,)), ((), ()))

SLOT_C, SLOT_ONE, SLOT_LSE = 0, 3, 6


def _params(*sem):
    return pltpu.CompilerParams(dimension_semantics=sem, vmem_limit_bytes=VMEM_LIMIT)


def _sigmoid(x):
    return 1.0 / (1.0 + jnp.exp(-x))


def _split3(x):
    hi = x.astype(BF16).astype(F32)
    rest = x - hi
    mid = rest.astype(BF16).astype(F32)
    return hi, mid, rest - mid


def _after_specs(after):
    return [pl.BlockSpec(memory_space=pl.ANY)] * len(after)


def _slot_width(cols):
    return LANES * (-(-(cols + (N_DEV - 1) * (cols % LANES)) // LANES))


def _inproj_fwd(x, gam, slots, after=()):
    S, D = x.shape
    n_dev, _, sw = slots.shape
    stride = sw - LANES
    width = stride * n_dev + LANES
    N = width - LANES
    tm, tn = min(TM, S), TN

    def body(x_ref, g_ref, s_ref, *rest):
        proj_ref, h_ref, z_ref, w_ref = rest[-4:]

        @pl.when(pl.program_id(0) == 0)
        def _():
            for n in range(n_dev):
                base = stride * n
                first = s_ref[n, :, 0:LANES]
                if n > 0:
                    first = first + s_ref[n - 1, :, stride:sw]
                w_ref[:, base:base + LANES] = first
                w_ref[:, base + LANES:base + stride] = s_ref[n, :, LANES:stride]
            w_ref[:, stride * n_dev:width] = s_ref[n_dev - 1, :, stride:sw]

        xf = x_ref[...]
        r = lax.rsqrt(jnp.mean(xf * xf, axis=-1, keepdims=True) + RMS_EPS)
        h = ((xf * r) * g_ref[...]).astype(BF16)
        h_ref[...] = h
        z_ref[...] = jnp.dot(h, w_ref[:, N:width], preferred_element_type=F32)
        for n in range(N // tn):
            cols = slice(n * tn, (n + 1) * tn)
            proj_ref[:, cols] = jnp.dot(h, w_ref[:, cols], preferred_element_type=F32).astype(BF16)

    return pl.pallas_call(
        body, name="inproj_fwd", grid=(S // tm,),
        in_specs=[pl.BlockSpec((tm, D), lambda i: (i, 0)),
                  pl.BlockSpec((1, D), lambda i: (0, 0)),
                  pl.BlockSpec((n_dev, D, sw), lambda i: (0, 0, 0))] + _after_specs(after),
        out_specs=[pl.BlockSpec((tm, N), lambda i: (i, 0)),
                   pl.BlockSpec((tm, D), lambda i: (i, 0)),
                   pl.BlockSpec((tm, LANES), lambda i: (i, 0)),
                   pl.BlockSpec((D, width), lambda i: (0, 0))],
        out_shape=[jax.ShapeDtypeStruct((S, N), BF16),
                   jax.ShapeDtypeStruct((S, D), BF16),
                   jax.ShapeDtypeStruct((S, LANES), F32),
                   jax.ShapeDtypeStruct((D, width), BF16)],
        compiler_params=_params("arbitrary"),
    )(x, gam, slots, *after)


def _fgate_fwd(z, bias, n_heads):
    S = z.shape[0]
    tb = min(TB, S)
    P = n_heads // 2

    def body(z_ref, b_ref, qaug_ref, kaug_ref):
        lane = lax.broadcasted_iota(jnp.int32, (tb, LANES), 1)
        tri = (lax.broadcasted_iota(jnp.int32, (tb, tb), 0)
               >= lax.broadcasted_iota(jnp.int32, (tb, tb), 1)).astype(F32)
        head = lax.broadcasted_iota(jnp.int32, (LANES, P * LANES), 0)
        col = lax.broadcasted_iota(jnp.int32, (LANES, P * LANES), 1)
        home = (head >> 1) * LANES + jnp.where((head & 1) == 0, HEAD_DIM, 0)
        is_head = head < n_heads
        place_q = [jnp.logical_and(is_head, col == home + SLOT_C + n).astype(BF16) for n in range(3)]
        place_k = [jnp.logical_and(is_head, col == home + SLOT_ONE + n).astype(BF16) for n in range(3)]
        slot = lax.broadcasted_iota(jnp.int32, (tb, P * LANES), 1) & (HEAD_DIM - 1)
        q_ones = jnp.logical_and(slot >= SLOT_ONE, slot < SLOT_ONE + 3).astype(F32)
        k_ones = jnp.logical_or(slot < SLOT_C + 3,
                                jnp.logical_and(slot >= SLOT_LSE, slot < SLOT_LSE + 3)).astype(F32)

        local = []
        for b in range(S // tb):
            zz = z_ref[b * tb:(b + 1) * tb, :] + b_ref[...]
            lf = jnp.minimum(zz, 0.0) - jnp.log(1.0 + jnp.exp(-jnp.abs(zz)))
            lf = jnp.where(lane < n_heads, lf, 0.0)
            local.append(jnp.dot(tri, lf, preferred_element_type=F32, precision=lax.Precision.HIGHEST))
        carry = jnp.zeros((1, LANES), F32)
        for b, part_sum in enumerate(local):
            c = part_sum + carry
            carry = c[tb - 1:tb, :]
            qa, ka = q_ones, k_ones
            for n, part in enumerate(_split3(c)):
                qa = qa + jnp.dot(part.astype(BF16), place_q[n], preferred_element_type=F32)
                ka = ka - jnp.dot(part.astype(BF16), place_k[n], preferred_element_type=F32)
            qaug_ref[b * tb:(b + 1) * tb, :] = qa.astype(BF16)
            kaug_ref[b * tb:(b + 1) * tb, :] = ka.astype(BF16)

    return pl.pallas_call(
        body, name="fgate_fwd",
        out_shape=[jax.ShapeDtypeStruct((S, P * LANES), BF16),
                   jax.ShapeDtypeStruct((S, P * LANES), BF16)],
        compiler_params=pltpu.CompilerParams(vmem_limit_bytes=VMEM_LIMIT),
    )(z, bias)


def _window_mean_minus_self(u, pad_ref, w, S):
    pad_ref[0:MAX_WINDOW, :] = jnp.zeros((MAX_WINDOW, LANES), F32)
    pad_ref[MAX_WINDOW:MAX_WINDOW + S, :] = u
    acc = u
    for j in range(1, w):
        acc = acc + pad_ref[MAX_WINDOW - j:MAX_WINDOW - j + S, :]
    t = lax.broadcasted_iota(jnp.int32, (S, LANES), 0)
    cnt = jnp.minimum(t + 1, w).astype(F32)
    return acc / cnt - u, cnt


def _pool_fwd(proj, pool_w, pool_scale):
    S = proj.shape[0]
    G = len(POOL_WINDOWS)

    def body(u_ref, w_ref, s_ref, y_ref, pad_ref):
        g = pl.program_id(0)
        for gi, w in enumerate(POOL_WINDOWS):
            @pl.when(g == gi)
            def _():
                d, _ = _window_mean_minus_self(u_ref[...].astype(F32), pad_ref, w, S)
                y = jnp.dot(d.astype(BF16), w_ref[0].astype(BF16), preferred_element_type=F32)
                y_ref[...] = (y * s_ref[...]).astype(BF16)

    return pl.pallas_call(
        body, name="pool_fwd", grid=(G,),
        in_specs=[pl.BlockSpec((S, LANES), lambda g: (0, g)),
                  pl.BlockSpec((1, LANES, LANES), lambda g: (g, 0, 0)),
                  pl.BlockSpec((1, LANES), lambda g: (0, g))],
        out_specs=pl.BlockSpec((S, LANES), lambda g: (0, g)),
        out_shape=jax.ShapeDtypeStruct((S, G * LANES), BF16),
        scratch_shapes=[pltpu.VMEM((S + MAX_WINDOW, LANES), F32)],
        compiler_params=_params("arbitrary"),
    )(proj, pool_w, pool_scale)


def _head_halves(rows):
    lane = lax.broadcasted_iota(jnp.int32, (rows, LANES), 1)
    return lane, (lane < HEAD_DIM, lane >= HEAD_DIM)


def _attn_fwd(proj, qaug, kaug):
    S = proj.shape[0]
    W = proj.shape[1] // 6
    P = W // LANES
    tq = min(TQ, S)
    nq = S // tq
    qc, kc, vc = 2 * P, 3 * P, 4 * P
    scale = 1.0 / math.sqrt(HEAD_DIM)

    def body(q_ref, k_ref, v_ref, qa_ref, ka_ref, o_ref, qb_ref, qm_scr, m_scr, acc_scr):
        i = pl.program_id(1)
        lane, halves = _head_halves(tq)
        v_ones = ((lane & (HEAD_DIM - 1)) < 3).astype(BF16)
        qs = q_ref[...] * scale
        qm_scr[0] = jnp.where(halves[0], qs, qa_ref[...])
        qm_scr[1] = jnp.where(halves[1], qs, qa_ref[...])
        m_scr[...] = jnp.full(m_scr.shape, NEG_INF, F32)
        acc_scr[...] = jnp.zeros(acc_scr.shape, F32)

        def update(j, on_diagonal):
            keys = pl.ds(pl.multiple_of(j * tq, tq), tq)
            k2, v2, kaug_t = k_ref[keys, :], v_ref[keys, :], ka_ref[keys, :]
            if on_diagonal:
                keep = (lax.broadcasted_iota(jnp.int32, (tq, tq), 0)
                        >= lax.broadcasted_iota(jnp.int32, (tq, tq), 1))
            logits = [lax.dot_general(qm_scr[a], jnp.where(halves[a], k2, kaug_t), NT, preferred_element_type=F32)
                      for a in range(2)]
            for a in range(2):
                s = jnp.where(keep, logits[a], NEG_INF) if on_diagonal else logits[a]
                va = jnp.where(halves[a], v2, v_ones)
                m_prev = m_scr[a]
                m_new = jnp.maximum(m_prev, jnp.max(s, axis=1, keepdims=True))
                p = jnp.exp(s - jnp.tile(m_new, (1, tq // LANES)))
                acc_scr[a] = jnp.exp(m_prev - m_new) * acc_scr[a] + jnp.dot(p.astype(BF16), va,
                                                                              preferred_element_type=F32)
                m_scr[a] = m_new

        def below_diagonal(jj, carry):
            update(2 * jj, False)
            update(2 * jj + 1, False)
            return carry

        lax.fori_loop(0, i // 2, below_diagonal, 0)

        @pl.when(i % 2 == 1)
        def _():
            update(i - 1, False)

        update(i, True)
        acc_a, acc_b = acc_scr[0], acc_scr[1]
        l_a, l_b = acc_a[:, HEAD_DIM:HEAD_DIM + 1], acc_b[:, 0:1]
        o_ref[...] = jnp.where(halves[0], acc_a / l_a, acc_b / l_b).astype(BF16)
        lse = jnp.where(halves[0], m_scr[1] + jnp.log(l_b), m_scr[0] + jnp.log(l_a))
        slot = lane & (HEAD_DIM - 1)
        aug = qa_ref[...].astype(F32)
        for n, part in enumerate(_split3(lse)):
            aug = jnp.where(slot == SLOT_LSE + n, -part, aug)
        qb_ref[...] = aug.astype(BF16)

    tile = lambda col: pl.BlockSpec((tq, LANES), lambda p, i: (i, col + p))
    whole = lambda col: pl.BlockSpec((S, LANES), lambda p, i: (0, col + p))
    return pl.pallas_call(
        body, name="attn_fwd", grid=(P, nq),
        in_specs=[tile(qc), whole(kc), whole(vc), tile(0), whole(0)],
        out_specs=[tile(0), tile(0)],
        out_shape=[jax.ShapeDtypeStruct((S, W), BF16), jax.ShapeDtypeStruct((S, W), BF16)],
        scratch_shapes=[pltpu.VMEM((2, tq, LANES), BF16),
                        pltpu.VMEM((2, tq, LANES), F32),
                        pltpu.VMEM((2, tq, LANES), F32)],
        compiler_params=_params("parallel", "arbitrary"),
    )(proj, proj, proj, qaug, kaug)


def _outproj_fwd(ypool, o, proj, x, wout):
    S, D = x.shape
    W = D // 2
    tm, tn = min(TM, S), TN

    def body(y_ref, o_ref, pg_ref, ag_ref, x_ref, w_ref, xn_ref, mix_ref):
        pg, ag = pg_ref[...].astype(F32), ag_ref[...].astype(F32)
        mix_ref[:, 0:W] = (y_ref[...].astype(F32) * (pg * _sigmoid(pg))).astype(BF16)
        mix_ref[:, W:D] = (o_ref[...].astype(F32) * (ag * _sigmoid(ag))).astype(BF16)
        for n in range(D // tn):
            cols = slice(n * tn, (n + 1) * tn)
            xn_ref[:, cols] = x_ref[:, cols] + jnp.dot(mix_ref[...], w_ref[:, cols], preferred_element_type=F32)

    return pl.pallas_call(
        body, name="outproj_fwd", grid=(S // tm,),
        in_specs=[pl.BlockSpec((tm, W), lambda i: (i, 0)),
                  pl.BlockSpec((tm, W), lambda i: (i, 0)),
                  pl.BlockSpec((tm, W), lambda i: (i, 1)),
                  pl.BlockSpec((tm, W), lambda i: (i, 5)),
                  pl.BlockSpec((tm, D), lambda i: (i, 0)),
                  pl.BlockSpec((D, D), lambda i: (0, 0))],
        out_specs=[pl.BlockSpec((tm, D), lambda i: (i, 0)),
                   pl.BlockSpec((tm, D), lambda i: (i, 0))],
        out_shape=[jax.ShapeDtypeStruct((S, D), F32),
                   jax.ShapeDtypeStruct((S, D), BF16)],
        compiler_params=_params("parallel"),
    )(ypool, o, proj, proj, x, wout)


def _loss_head(x, gam, target):
    S, D = x.shape
    tm = min(TM, S)

    def body(x_ref, g_ref, t_ref, dx_ref, loss_ref, dg_ref):
        @pl.when(pl.program_id(0) == 0)
        def _():
            loss_ref[...] = jnp.zeros(loss_ref.shape, F32)
            dg_ref[...] = jnp.zeros(dg_ref.shape, F32)

        xf, gam_v = x_ref[...], g_ref[...]
        r = lax.rsqrt(jnp.mean(xf * xf, axis=-1, keepdims=True) + RMS_EPS)
        xhat = xf * r
        err = xhat * gam_v - t_ref[...]
        part = jnp.sum(jnp.sum(err * err, axis=-1, keepdims=True), axis=0, keepdims=True)
        loss_ref[...] += part * (0.5 / D)
        dy = err * (1.0 / D)
        dg_ref[...] += jnp.sum(dy * xhat, axis=0, keepdims=True)
        dxhat = dy * gam_v
        dx_ref[...] = r * (dxhat - xhat * jnp.mean(dxhat * xhat, axis=-1, keepdims=True))

    return pl.pallas_call(
        body, name="loss_head", grid=(S // tm,),
        in_specs=[pl.BlockSpec((tm, D), lambda i: (i, 0)),
                  pl.BlockSpec((1, D), lambda i: (0, 0)),
                  pl.BlockSpec((tm, D), lambda i: (i, 0))],
        out_specs=[pl.BlockSpec((tm, D), lambda i: (i, 0)),
                   pl.BlockSpec((8, LANES), lambda i: (0, 0)),
                   pl.BlockSpec((1, D), lambda i: (0, 0))],
        out_shape=[jax.ShapeDtypeStruct((S, D), F32),
                   jax.ShapeDtypeStruct((8, LANES), F32),
                   jax.ShapeDtypeStruct((1, D), F32)],
        compiler_params=_params("arbitrary"),
    )(x, gam, target)


def _outproj_bwd(g, wout, mixed, ypool, o, proj):
    S, D = g.shape
    W = D // 2
    tm = min(TM, S)

    def body(g_ref, w_ref, mix_ref, y_ref, o_ref, pg_ref, ag_ref, dw_ref, da_ref, dgate_ref, doaug_ref):
        @pl.when(pl.program_id(0) == 0)
        def _():
            dw_ref[...] = jnp.zeros(dw_ref.shape, F32)

        gb = g_ref[...].astype(BF16)
        dw_ref[...] += lax.dot_general(mix_ref[...], gb, TN_DIMS, preferred_element_type=F32)
        for half, (val_ref, gate_ref) in enumerate(((y_ref, pg_ref), (o_ref, ag_ref))):
            cols = slice(half * W, (half + 1) * W)
            dmix = lax.dot_general(gb, w_ref[cols, :], NT, preferred_element_type=F32)
            gt = gate_ref[...].astype(F32)
            sg = _sigmoid(gt)
            da_ref[:, cols] = (dmix * (gt * sg)).astype(BF16)
            dgate_ref[:, cols] = (dmix * val_ref[...].astype(F32) * (sg * (1.0 + gt * (1.0 - sg)))).astype(BF16)

        lane, halves = _head_halves(tm)
        slot = lane & (HEAD_DIM - 1)
        for p in range(W // LANES):
            cols = slice(p * LANES, (p + 1) * LANES)
            prod = da_ref[:, W + p * LANES:W + (p + 1) * LANES].astype(F32) * o_ref[:, cols].astype(F32)
            d_a = jnp.sum(jnp.where(halves[0], prod, 0.0), axis=1, keepdims=True)
            d_b = jnp.sum(jnp.where(halves[1], prod, 0.0), axis=1, keepdims=True)
            aug = jnp.zeros((tm, LANES), F32)
            for n, part in enumerate(_split3(jnp.where(halves[0], d_b, d_a))):
                aug = jnp.where(slot == SLOT_C + n, -part, aug)
            doaug_ref[:, cols] = aug.astype(BF16)

    rows = lambda width, col: pl.BlockSpec((tm, width), lambda i: (i, col))
    return pl.pallas_call(
        body, name="outproj_bwd", grid=(S // tm,),
        in_specs=[rows(D, 0), pl.BlockSpec((D, D), lambda i: (0, 0)), rows(D, 0), rows(W, 0), rows(W, 0),
                  rows(W, 1), rows(W, 5)],
        out_specs=[pl.BlockSpec((D, D), lambda i: (0, 0)), rows(D, 0), rows(D, 0), rows(W, 0)],
        out_shape=[jax.ShapeDtypeStruct((D, D), F32),
                   jax.ShapeDtypeStruct((S, D), BF16),
                   jax.ShapeDtypeStruct((S, D), BF16),
                   jax.ShapeDtypeStruct((S, W), BF16)],
        compiler_params=_params("arbitrary"),
    )(g, wout, mixed, ypool, o, proj, proj)


def _section_specs(sections, rows, width):
    specs = [pl.BlockSpec((rows, width), lambda k, c=c: (k, c)) for _, c in sections]
    return specs, [a for a, _ in sections]


def _inproj_bwd_dw(h, sections, dzf):
    S, D = h.shape
    W = D // 2
    n_sec = len(sections)
    N = n_sec * W
    ts = min(TM, S)
    n_steps = S // ts

    def body(h_ref, dz_ref, *rest):
        sec_refs, (dw_ref, dwb_ref) = rest[:n_sec], rest[n_sec:]

        @pl.when(pl.program_id(0) == 0)
        def _():
            dw_ref[...] = jnp.zeros(dw_ref.shape, F32)

        ht = h_ref[...].T
        dw_ref[:, N:N + LANES] += jnp.dot(ht, dz_ref[...], preferred_element_type=F32)
        for n, ref in enumerate(sec_refs):
            dw_ref[:, n * W:(n + 1) * W] += jnp.dot(ht, ref[...], preferred_element_type=F32)

        @pl.when(pl.program_id(0) == n_steps - 1)
        def _():
            dwb_ref[...] = dw_ref[...].astype(BF16)

    sec_specs, sec_arrays = _section_specs(sections, ts, W)
    whole = pl.BlockSpec((D, N + LANES), lambda k: (0, 0))
    return pl.pallas_call(
        body, name="inproj_bwd_dw", grid=(n_steps,),
        in_specs=[pl.BlockSpec((ts, D), lambda k: (k, 0)),
                  pl.BlockSpec((ts, LANES), lambda k: (k, 0))] + sec_specs,
        out_specs=[whole, whole],
        out_shape=[jax.ShapeDtypeStruct((D, N + LANES), F32),
                   jax.ShapeDtypeStruct((D, N + LANES), BF16)],
        compiler_params=_params("arbitrary"),
    )(h, dzf, *sec_arrays)


def _attn_bwd(proj, da, qaug, kaug, doaug):
    S = proj.shape[0]
    W = proj.shape[1] // 6
    P = W // LANES
    tq = min(TQ, S)
    nq = S // tq
    qc, kc, vc = 2 * P, 3 * P, 4 * P
    scale = 1.0 / math.sqrt(HEAD_DIM)

    def body(q_ref, k_ref, v_ref, do_ref, qa_ref, ka_ref, da_ref,
             dq_ref, dk_ref, dv_ref, dqx_ref, dkx_ref, km_scr, vm_scr, dk_scr, dv_scr, dq_scr):
        j = pl.program_id(1)
        lane, halves = _head_halves(tq)

        @pl.when(j == 0)
        def _():
            dq_scr[...] = jnp.zeros(dq_scr.shape, F32)

        v_ones = ((lane & (HEAD_DIM - 1)) < 3).astype(BF16)
        for a in range(2):
            km_scr[a] = jnp.where(halves[a], k_ref[...], ka_ref[...])
            vm_scr[a] = jnp.where(halves[a], v_ref[...], v_ones)
        dk_scr[...] = jnp.zeros(dk_scr.shape, F32)
        dv_scr[...] = jnp.zeros(dv_scr.shape, F32)

        def update(i, on_diagonal):
            rows = pl.ds(pl.multiple_of(i * tq, tq), tq)
            qs = q_ref[rows, :] * scale
            do2, qaug_t, doaug_t = do_ref[rows, :], qa_ref[rows, :], da_ref[rows, :]
            if on_diagonal:
                keep = (lax.broadcasted_iota(jnp.int32, (tq, tq), 0)
                        >= lax.broadcasted_iota(jnp.int32, (tq, tq), 1))
            qas = [jnp.where(halves[a], qs, qaug_t) for a in range(2)]
            logits = [lax.dot_general(qas[a], km_scr[a], NT, preferred_element_type=F32) for a in range(2)]
            dps = [lax.dot_general(jnp.where(halves[a], do2, doaug_t), vm_scr[a], NT, preferred_element_type=F32)
                   for a in range(2)]
            dv = None
            for a in range(2):
                s = jnp.where(keep, logits[a], NEG_INF) if on_diagonal else logits[a]
                p = jnp.exp(s)
                dsb = (p * dps[a]).astype(BF16)
                do0 = jnp.where(halves[a], do2, jnp.zeros_like(do2))
                dv_a = lax.dot_general(p.astype(BF16), do0, TN_DIMS, preferred_element_type=F32)
                dv = dv_a if dv is None else dv + dv_a
                dk_scr[a] += lax.dot_general(dsb, qas[a], TN_DIMS, preferred_element_type=F32)
                dq_scr[a, rows, :] += jnp.dot(dsb, km_scr[a], preferred_element_type=F32)
            dv_scr[...] += dv

        def below_diagonal(n, carry):
            update(j + 1 + 2 * n, False)
            update(j + 2 + 2 * n, False)
            return carry

        update(j, True)
        below = nq - 1 - j
        lax.fori_loop(0, below // 2, below_diagonal, 0)

        @pl.when(below % 2 == 1)
        def _():
            update(nq - 1, False)


        dk_ref[...] = jnp.where(halves[0], dk_scr[0], dk_scr[1]).astype(BF16)
        dkx_ref[...] = jnp.where(halves[0], dk_scr[1], dk_scr[0])
        dv_ref[...] = dv_scr[...].astype(BF16)

        @pl.when(j == nq - 1)
        def _():
            row_lane, row_halves = _head_halves(S)
            dq_ref[...] = (jnp.where(row_halves[0], dq_scr[0], dq_scr[1]) * scale).astype(BF16)
            dqx_ref[...] = jnp.where(row_halves[0], dq_scr[1], dq_scr[0])

    tile = lambda col: pl.BlockSpec((tq, LANES), lambda p, j: (j, col + p))
    whole = lambda col: pl.BlockSpec((S, LANES), lambda p, j: (0, col + p))
    return pl.pallas_call(
        body, name="attn_bwd", grid=(P, nq),
        in_specs=[whole(qc), tile(kc), tile(vc), whole(P), whole(0), tile(0), whole(0)],
        out_specs=[whole(0), tile(0), tile(0), whole(0), tile(0)],
        out_shape=[jax.ShapeDtypeStruct((S, W), BF16),
                   jax.ShapeDtypeStruct((S, W), BF16),
                   jax.ShapeDtypeStruct((S, W), BF16),
                   jax.ShapeDtypeStruct((S, W), F32),
                   jax.ShapeDtypeStruct((S, W), F32)],
        scratch_shapes=[pltpu.VMEM((2, tq, LANES), BF16),
                        pltpu.VMEM((2, tq, LANES), BF16),
                        pltpu.VMEM((2, tq, LANES), F32),
                        pltpu.VMEM((tq, LANES), F32),
                        pltpu.VMEM((2, S, LANES), F32)],
        compiler_params=_params("parallel", "arbitrary"),
    )(proj, proj, proj, da, qaug, kaug, doaug)


def _fgate_bwd(dqx, dkx, z, bias, n_heads):
    S = z.shape[0]
    tb = min(TB, S)
    nb = S // tb

    def body(dqx_ref, dkx_ref, z_ref, b_ref, dz_ref, db_ref):
        tri = (lax.broadcasted_iota(jnp.int32, (tb, tb), 1)
               >= lax.broadcasted_iota(jnp.int32, (tb, tb), 0)).astype(F32)
        n_cols = dqx_ref.shape[1]
        col = lax.broadcasted_iota(jnp.int32, (n_cols, LANES), 0)
        head = lax.broadcasted_iota(jnp.int32, (n_cols, LANES), 1)
        home = (head >> 1) * LANES + jnp.where((head & 1) == 0, HEAD_DIM, 0)
        is_head = head < n_heads
        pick_rows = jnp.logical_and(is_head, col == home + SLOT_C).astype(F32)
        pick_cols = jnp.logical_and(is_head, col == home + SLOT_ONE).astype(F32)

        local = []
        for b in range(nb):
            rows = slice(b * tb, (b + 1) * tb)
            dc = (jnp.dot(dqx_ref[rows, :], pick_rows, preferred_element_type=F32, precision=lax.Precision.HIGHEST)
                  - jnp.dot(dkx_ref[rows, :], pick_cols, preferred_element_type=F32, precision=lax.Precision.HIGHEST))
            local.append(jnp.dot(tri, dc, preferred_element_type=F32, precision=lax.Precision.HIGHEST))
        carry = jnp.zeros((1, LANES), F32)
        db = jnp.zeros((1, LANES), F32)
        for b in reversed(range(nb)):
            rows = slice(b * tb, (b + 1) * tb)
            rc = local[b] + carry
            carry = rc[0:1, :]
            dz = rc * _sigmoid(-(z_ref[rows, :] + b_ref[...]))
            dz_ref[rows, :] = dz.astype(BF16)
            db = db + jnp.sum(dz, axis=0, keepdims=True)
        db_ref[...] = db

    return pl.pallas_call(
        body, name="fgate_bwd",
        out_shape=[jax.ShapeDtypeStruct((S, LANES), BF16),
                   jax.ShapeDtypeStruct((1, LANES), F32)],
        compiler_params=pltpu.CompilerParams(vmem_limit_bytes=VMEM_LIMIT),
    )(dqx, dkx, z, bias)


def _pool_bwd(proj, da, pool_w, pool_scale):
    S = proj.shape[0]
    G = len(POOL_WINDOWS)

    def body(u_ref, dy_ref, w_ref, s_ref, du_ref, dw_ref, ds_ref, pad_ref):
        g = pl.program_id(0)
        for gi, w in enumerate(POOL_WINDOWS):
            @pl.when(g == gi)
            def _():
                d, cnt = _window_mean_minus_self(u_ref[...].astype(F32), pad_ref, w, S)
                db = d.astype(BF16)
                wb = w_ref[0].astype(BF16)
                yraw = jnp.dot(db, wb, preferred_element_type=F32)
                dy = dy_ref[...].astype(F32)
                ds_ref[...] = jnp.sum(dy * yraw, axis=0, keepdims=True)
                dzb = (dy * s_ref[...]).astype(BF16)
                dw_ref[0] = lax.dot_general(db, dzb, TN_DIMS, preferred_element_type=F32)
                dd = lax.dot_general(dzb, wb, NT, preferred_element_type=F32)
                pad_ref[0:S, :] = dd / cnt
                pad_ref[S:S + MAX_WINDOW, :] = jnp.zeros((MAX_WINDOW, LANES), F32)
                acc = -dd
                for j in range(w):
                    acc = acc + pad_ref[j:j + S, :]
                du_ref[...] = acc.astype(BF16)

    return pl.pallas_call(
        body, name="pool_bwd", grid=(G,),
        in_specs=[pl.BlockSpec((S, LANES), lambda g: (0, g)),
                  pl.BlockSpec((S, LANES), lambda g: (0, g)),
                  pl.BlockSpec((1, LANES, LANES), lambda g: (g, 0, 0)),
                  pl.BlockSpec((1, LANES), lambda g: (0, g))],
        out_specs=[pl.BlockSpec((S, LANES), lambda g: (0, g)),
                   pl.BlockSpec((1, LANES, LANES), lambda g: (g, 0, 0)),
                   pl.BlockSpec((1, LANES), lambda g: (0, g))],
        out_shape=[jax.ShapeDtypeStruct((S, G * LANES), BF16),
                   jax.ShapeDtypeStruct((G, LANES, LANES), F32),
                   jax.ShapeDtypeStruct((1, G * LANES), F32)],
        scratch_shapes=[pltpu.VMEM((S + MAX_WINDOW, LANES), F32)],
        compiler_params=_params("arbitrary"),
    )(proj, da, pool_w, pool_scale)


def _inproj_bwd_dx(sections, dzf, w, x, gam, g, after=()):
    S, D = x.shape
    N = w.shape[1] - LANES
    W = D // 2
    n_sec = len(sections)
    tm = min(TM // 2, S)

    def body(dz_ref, w_ref, wf_ref, x_ref, gam_ref, g_ref, *rest):
        sec_refs = rest[:n_sec]
        dx_ref, dg_ref = rest[-2:]

        @pl.when(pl.program_id(0) == 0)
        def _():
            dg_ref[...] = jnp.zeros(dg_ref.shape, F32)

        dh = lax.dot_general(dz_ref[...], wf_ref[...], NT, preferred_element_type=F32)
        for n, ref in enumerate(sec_refs):
            dh = dh + lax.dot_general(ref[...], w_ref[:, n * W:(n + 1) * W], NT, preferred_element_type=F32)
        xf = x_ref[...]
        r = lax.rsqrt(jnp.mean(xf * xf, axis=-1, keepdims=True) + RMS_EPS)
        xhat = xf * r
        dg_ref[...] += jnp.sum(dh * xhat, axis=0, keepdims=True)
        dxhat = dh * gam_ref[...]
        dx_ref[...] = g_ref[...] + r * (dxhat - xhat * jnp.mean(dxhat * xhat, axis=-1, keepdims=True))

    sec_specs, sec_arrays = _section_specs(sections, tm, W)
    return pl.pallas_call(
        body, name="inproj_bwd_dx", grid=(S // tm,),
        in_specs=[pl.BlockSpec((tm, LANES), lambda i: (i, 0)),
                  pl.BlockSpec((D, N), lambda i: (0, 0)),
                  pl.BlockSpec((D, LANES), lambda i: (0, N // LANES)),
                  pl.BlockSpec((tm, D), lambda i: (i, 0)),
                  pl.BlockSpec((1, D), lambda i: (0, 0)),
                  pl.BlockSpec((tm, D), lambda i: (i, 0))] + sec_specs + _after_specs(after),
        out_specs=[pl.BlockSpec((tm, D), lambda i: (i, 0)),
                   pl.BlockSpec((1, D), lambda i: (0, 0))],
        out_shape=[jax.ShapeDtypeStruct((S, D), F32),
                   jax.ShapeDtypeStruct((1, D), F32)],
        compiler_params=_params("arbitrary"),
    )(dzf, w, w, x, gam, g, *sec_arrays, *after)


def _adamw(w, m, v, gsets, name, rows, shifted=False):
    A, R, C = w.shape
    tr = min(rows, R)
    c1 = 1.0 / (1.0 - ADAM_B1 ** ADAM_STEP)
    c2 = 1.0 / (1.0 - ADAM_B2 ** ADAM_STEP)
    counts = [len(gs) for gs in gsets]

    def body(w_ref, m_ref, v_ref, *rest):
        g_ref, d_ref, nm_ref, nv_ref = rest[-4:]
        at = 0
        for a in range(A):
            part_refs = rest[at:at + counts[a]]
            at += counts[a]

            @pl.when(pl.program_id(0) == a)
            def _():
                g = None
                for ref in part_refs:
                    for s in range(ref.shape[0]):
                        term = ref[s].astype(F32)
                        g = term if g is None else g + term
                if shifted:
                    lanes = g.shape[1]
                    g = pltpu.roll(g, (lanes - _index(_position()) * (C % LANES)) % lanes, axis=1)[:, :C]
                nm = ADAM_B1 * m_ref[0] + (1.0 - ADAM_B1) * g
                nv = ADAM_B2 * v_ref[0] + (1.0 - ADAM_B2) * (g * g)
                g_ref[0] = g
                nm_ref[0] = nm
                nv_ref[0] = nv
                d_ref[0] = -ADAM_LR * ((nm * c1) / (jnp.sqrt(nv * c2) + ADAM_EPS) + ADAM_WD * w_ref[0])

    spec = pl.BlockSpec((1, tr, C), lambda a, r: (a, r, 0))
    part_specs = [pl.BlockSpec((part.shape[0], tr, part.shape[2]), lambda a, r, l=l: (0, jnp.where(a == l, r, 0), 0))
                  for l, gs in enumerate(gsets) for part in gs]
    shape = jax.ShapeDtypeStruct((A, R, C), F32)
    return pl.pallas_call(
        body, name=name, grid=(A, R // tr),
        in_specs=[spec, spec, spec] + part_specs,
        out_specs=[spec, spec, spec, spec],
        out_shape=[shape, shape, shape, shape],
        compiler_params=_params("arbitrary", "arbitrary"),
    )(w, m, v, *[part for gs in gsets for part in gs])


def _position():
    return lax.axis_index("x"), lax.axis_index("y"), lax.axis_index("c")


def _index(dev):
    return 4 * dev[0] + 2 * dev[1] + dev[2]


def _all_gather(arrs, slots, out_shapes, name):
    n_arr = len(arrs)

    def body(*refs):
        ins, outs = refs[:n_arr], refs[n_arr:2 * n_arr]
        send_sems, recv_sems, local_sems = refs[2 * n_arr:]
        x, y, c = _position()
        me, sibling = (x, y, c), (x, y, 1 - c)
        chips = [(1 - x, y), (x, 1 - y), (1 - x, 1 - y)]

        def copy(a, k, block, to, src=None):
            part = slots[a](outs[a], _index(block))
            return pltpu.make_async_remote_copy(
                src_ref=part if src is None else src, dst_ref=part,
                send_sem=send_sems.at[a, k], recv_sem=recv_sems.at[a, k],
                device_id=to, device_id_type=MESH)

        mine = [pltpu.make_async_copy(ins[a], slots[a](outs[a], _index(me)), local_sems.at[a])
                for a in range(n_arr)]
        for cp in mine:
            cp.start()
        first = []
        for a in range(n_arr):
            first.append(copy(a, 0, me, sibling, src=ins[a]))
            first += [copy(a, 1 + j, me, (*chip, c), src=ins[a]) for j, chip in enumerate(chips)]
        for cp in first:
            cp.start()
        passed = []
        for j, chip in enumerate(chips):
            for a in range(n_arr):
                copy(a, 1 + j, (*chip, c), me).wait_recv()
                fwd = copy(a, 4 + j, (*chip, c), sibling)
                fwd.start()
                passed.append(fwd)
        for a in range(n_arr):
            copy(a, 0, sibling, me).wait_recv()
            for j, chip in enumerate(chips):
                copy(a, 4 + j, (*chip, 1 - c), me).wait_recv()
        for cp in first + passed:
            cp.wait_send()
        for cp in mine:
            cp.wait()

    any_spec = pl.BlockSpec(memory_space=pl.ANY)
    return pl.pallas_call(
        body, name=name,
        in_specs=[any_spec] * n_arr, out_specs=[any_spec] * n_arr, out_shape=out_shapes,
        scratch_shapes=[pltpu.SemaphoreType.DMA((n_arr, 7)), pltpu.SemaphoreType.DMA((n_arr, 7)),
                        pltpu.SemaphoreType.DMA((n_arr,))],
    )(*arrs)


def _split_copies(srcs, lands, send_sems, recv_sems, kinds):
    x, y, c = _position()
    me = _index((x, y, c))
    copies = []
    for a, (src_part, land_part) in enumerate(kinds):
        for k in range(1, N_DEV):
            peer = (x ^ ((k >> 2) & 1), y ^ ((k >> 1) & 1), c ^ (k & 1))
            copies.append(pltpu.make_async_remote_copy(
                src_ref=src_part(srcs[a], _index(peer)), dst_ref=land_part(lands[a], me, k),
                send_sem=send_sems[a].at[k - 1], recv_sem=recv_sems[a].at[k - 1],
                device_id=peer, device_id_type=MESH))
    return copies


def _split_start(srcs, lands, kinds, name, after=()):
    n = len(srcs)

    def body(*refs):
        src_refs, land_refs = refs[:n], refs[n:2 * n]
        outs = refs[2 * n + len(after):]
        send_sems, recv_sems = outs[:n], outs[n:2 * n]
        token = outs[-1]
        for cp in _split_copies(src_refs, land_refs, send_sems, recv_sems, kinds):
            cp.start()
        token[...] = jnp.zeros(token.shape, token.dtype)

    hbm = pl.BlockSpec(memory_space=pltpu.HBM)
    sem = pl.BlockSpec(memory_space=pltpu.SEMAPHORE)
    operands = [pltpu.with_memory_space_constraint(t, pltpu.HBM) for t in (*srcs, *lands)]
    out = pl.pallas_call(
        body, name=name,
        in_specs=[hbm] * (2 * n) + _after_specs(after),
        out_specs=[sem] * (2 * n) + [hbm] * (2 * n) + [pl.BlockSpec(memory_space=pltpu.VMEM)],
        out_shape=[pltpu.SemaphoreType.DMA((N_DEV - 1,))] * (2 * n)
        + [pltpu.HBM(t.shape, t.dtype) for t in operands] + [jax.ShapeDtypeStruct((8, LANES), F32)],
        input_output_aliases={i: 2 * n + i for i in range(2 * n)},
        compiler_params=pltpu.CompilerParams(has_side_effects=pltpu.SideEffectType.DATAFLOW_SIDE_EFFECTING),
    )(*operands, *after)
    return [(out[a], out[n + a], out[2 * n + a], out[3 * n + a]) for a in range(n)], out[-1]


def _split_wait(started, kinds, after, name):
    n = len(started)
    sems = [t[0] for t in started] + [t[1] for t in started]
    srcs = [t[2] for t in started]
    lands = [t[3] for t in started]

    def body(*refs):
        src_refs, land_refs = refs[:n], refs[n:2 * n]
        send_sems, recv_sems = refs[2 * n:3 * n], refs[3 * n:4 * n]
        for cp in _split_copies(src_refs, land_refs, send_sems, recv_sems, kinds):
            cp.wait_send()
            cp.wait_recv()

    hbm = pl.BlockSpec(memory_space=pltpu.HBM)
    sem = pl.BlockSpec(memory_space=pltpu.SEMAPHORE)
    out = pl.pallas_call(
        body, name=name,
        in_specs=[hbm] * (2 * n) + [sem] * (2 * n) + _after_specs(after),
        out_specs=[hbm] * (2 * n),
        out_shape=[pltpu.HBM(t.shape, t.dtype) for t in (*srcs, *lands)],
        input_output_aliases={i: i for i in range(2 * n)},
        compiler_params=pltpu.CompilerParams(has_side_effects=pltpu.SideEffectType.DATAFLOW_SIDE_EFFECTING),
    )(*srcs, *lands, *sems, *after)
    return out[n:]


def _as_rows(p):
    if p.size % LANES == 0:
        rows = p.reshape(-1, LANES)
    else:
        rows = p.reshape(-1, p.shape[-1])
        rows = jnp.pad(rows, ((0, 0), (0, LANES - rows.shape[1])))
    return jnp.pad(rows, ((0, -rows.shape[0] % 8), (0, 0)))


def _pack(parts):
    return jnp.concatenate([_as_rows(p) for p in parts])[None]


def _unpack(packed, like):
    out, at = [], 0
    for p in like:
        whole = p.size % LANES == 0
        n = p.size // LANES if whole else p.size // p.shape[-1]
        rows = packed[0, at:at + n]
        out.append((rows if whole else rows[:, :p.shape[-1]]).reshape(p.shape))
        at += n + (-n % 8)
    return out


def _local_step(x, target, norm_g, forget_bias, pool_w, pool_scale, final_g, weights_in, weights_out, on_grads,
                first_after=()):
    L = norm_g.shape[0]
    S, D = x.shape
    W = D // 2
    H = W // HEAD_DIM
    bias = jnp.pad(forget_bias, ((0, 0), (0, LANES - H)))

    saved = []
    after = tuple(first_after)
    for l in range(L):
        proj, h, z, w = _inproj_fwd(x, norm_g[l:l + 1], weights_in(l, x), after)
        after = ()
        qaug, kaug = _fgate_fwd(z, bias[l:l + 1], H)
        ypool = _pool_fwd(proj, pool_w[l], pool_scale[l:l + 1])
        o, qaug_b = _attn_fwd(proj, qaug, kaug)
        wout = weights_out(l, o)
        x_new, mixed = _outproj_fwd(ypool, o, proj, x, wout)
        saved.append((x, proj, h, z, qaug_b, kaug, ypool, o, mixed, w, wout))
        x = x_new

    g, loss, d_final_g = _loss_head(x, final_g.reshape(1, D), target)

    d_norm_g, d_bias, d_pool_w, d_pool_scale = [], [], [], []
    for l in reversed(range(L)):
        x_in, proj, h, z, qaug_b, kaug, ypool, o, mixed, w, wout = saved[l]
        d_wout, da, dgate, doaug = _outproj_bwd(g, wout, mixed, ypool, o, proj)
        dq, dk, dv, dqx, dkx = _attn_bwd(proj, da, qaug_b, kaug, doaug)
        dzf, db = _fgate_bwd(dqx, dkx, z, bias[l:l + 1], H)
        dpu, dpw, dps = _pool_bwd(proj, da, pool_w[l], pool_scale[l:l + 1])
        dproj = [(dpu, 0), (dgate, 0), (dq, 0), (dk, 0), (dv, 0), (dgate, 1)]
        d_w, d_w_bf16 = _inproj_bwd_dw(h, dproj, dzf)
        after = tuple(on_grads(l, d_w, d_w_bf16, d_wout))
        g, dgam = _inproj_bwd_dx(dproj, dzf, w, x_in, norm_g[l:l + 1], g, after)
        d_norm_g.append(dgam[0])
        d_bias.append(db[0, :H])
        d_pool_w.append(dpw)
        d_pool_scale.append(dps[0])

    stack = lambda parts: jnp.stack(parts[::-1])
    grads = dict(norm_g=stack(d_norm_g), forget_bias=stack(d_bias), pool_w=stack(d_pool_w),
                 pool_scale=stack(d_pool_scale), final_g=d_final_g[0])
    return loss[0, 0], g, grads


def kernel(x, norm_g, w_in, forget_bias, pool_w, pool_scale, w_out, final_g, loss_target, m_norm_g, m_w_in, m_forget_bias, m_pool_w, m_pool_scale, m_w_out, m_final_g, v_norm_g, v_w_in, v_forget_bias, v_pool_w, v_pool_scale, v_w_out, v_final_g):
    L, D, cols = w_in.shape
    rows_out = w_out.shape[1]
    W = D // 2
    H = W // HEAD_DIM
    me = _index(_position())

    slot = _slot_width(cols)
    wout_b = w_out.astype(BF16)
    win_b = [lax.dynamic_update_slice(jnp.zeros((D, slot), BF16), w_in[l].astype(BF16),
                                      (0, me * (cols % LANES))) for l in range(L)]
    gather_in = (lambda ref, peer: ref, lambda ref, mine, k: ref.at[mine])
    gather_out = (lambda ref, peer: ref, lambda ref, mine, k: ref.at[pl.ds(mine * rows_out, rows_out), :])

    (first_in,) = _all_gather([win_b[0]], [lambda ref, n: ref.at[n]],
                              [jax.ShapeDtypeStruct((N_DEV, D, slot), BF16)], "gather_first")
    rest_srcs = [wout_b[0]] + [w[l] for l in range(1, L) for w in (win_b, wout_b)]
    rest_lands = [jnp.tile(wout_b[0], (N_DEV, 1))]
    for l in range(1, L):
        rest_lands += [jnp.broadcast_to(win_b[l][None], (N_DEV, D, slot)), jnp.tile(wout_b[l], (N_DEV, 1))]
    rest_kinds = [gather_out] + [gather_in, gather_out] * (L - 1)
    rest, rest_token = _split_start(rest_srcs, rest_lands, rest_kinds, "gather_start_rest", (first_in,))

    def weights_in(l, x_in):
        if l == 0:
            return first_in
        (win_all,) = _split_wait([rest[2 * l - 1]], [gather_in], (x_in,), f"gather_wait_in_{l}")
        return win_all

    def weights_out(l, o):
        (wout_full,) = _split_wait([rest[2 * l]], [gather_out], (o,), f"gather_wait_out_{l}")
        return wout_full

    stride = slot - LANES
    exchange_kinds = [(lambda ref, peer: ref.at[:, pl.ds(pl.multiple_of(peer * stride, LANES), slot)],
                       lambda ref, mine, k: ref.at[k - 1]),
                      (lambda ref, peer: ref.at[pl.ds(peer * rows_out, rows_out), :],
                       lambda ref, mine, k: ref.at[k - 1])]
    exchanges, own_parts = {}, {}

    def on_grads(l, dw, dw_bf16, d_wout):
        own_parts[l] = (lax.dynamic_slice_in_dim(dw, me * stride, slot, 1)[None],
                        lax.dynamic_slice_in_dim(d_wout, me * rows_out, rows_out, 0)[None])
        lands = [lax.empty((N_DEV - 1, D, slot), BF16), lax.empty((N_DEV - 1, rows_out, D), BF16)]
        exchanges[l], token = _split_start([dw_bf16, d_wout.astype(BF16)], lands, exchange_kinds,
                                           f"exchange_start_{l}")
        return (token,)

    loss, dx, grads = _local_step(x[0], loss_target[0], norm_g, forget_bias, pool_w, pool_scale, final_g,
                                  weights_in, weights_out, on_grads, (rest_token,))
    loss = lax.psum(loss, ("x", "y", "c"))

    names = ["norm_g", "forget_bias", "pool_w", "pool_scale", "final_g"]
    small_w = [norm_g, forget_bias, pool_w, pool_scale, final_g]
    small_m = [m_norm_g, m_forget_bias, m_pool_w, m_pool_scale, m_final_g]
    small_v = [v_norm_g, v_forget_bias, v_pool_w, v_pool_scale, v_final_g]
    gs = _pack([grads[n] for n in names])
    rows = gs.shape[1]
    (gs_all,) = _all_gather([gs], [lambda ref, n: ref.at[n]],
                            [jax.ShapeDtypeStruct((N_DEV, 1, rows, LANES), F32)], "gather_small_grads")
    packed = _adamw(_pack(small_w), _pack(small_m), _pack(small_v),
                    [[gs_all.reshape(N_DEV, rows, LANES)]], "adamw_small", rows)
    g_s, d_s, nm_s, nv_s = [_unpack(p, small_w) for p in packed]

    gin_sets, gout_sets = [], []
    for l in range(L):
        got_in, got_out = _split_wait(exchanges[l], exchange_kinds, (dx, packed[1]), f"exchange_wait_{l}")
        gin_sets.append([own_parts[l][0], got_in])
        gout_sets.append([own_parts[l][1], got_out])
    g_w_in, d_w_in, nm_w_in, nv_w_in = _adamw(w_in, m_w_in, v_w_in, gin_sets, "adamw_w_in", 256, shifted=True)
    g_w_out, d_w_out, nm_w_out, nv_w_out = _adamw(w_out, m_w_out, v_w_out, gout_sets, "adamw_w_out", 128)

    def order(big_in, big_out, small):
        return (small[0], big_in, small[1], small[2], small[3], big_out, small[4])

    return (loss, dx[None], *order(g_w_in, g_w_out, g_s), *order(d_w_in, d_w_out, d_s),
            *order(nm_w_in, nm_w_out, nm_s), *order(nv_w_in, nv_w_out, nv_s))
```

```python
import math

import jax
import jax.numpy as jnp
from jax import lax
from jax.experimental import pallas as pl
from jax.experimental.pallas import tpu as pltpu

F32 = jnp.float32
BF16 = jnp.bfloat16
MESH = pl.DeviceIdType.MESH

RMS_EPS = 1e-6
NEG_INF = -1e30
HEAD_DIM = 64
POOL_WINDOWS = (2, 4, 8, 16)
MAX_WINDOW = 16
LANES = 128
N_DEV = 8

ADAM_LR = 0.001
ADAM_B1 = 0.9
ADAM_B2 = 0.999
ADAM_EPS = 1e-08
ADAM_WD = 0.01
ADAM_STEP = 10

TM = 512
TN = 512
TQ = 512
TB = 256
VMEM_LIMIT = 56 * 1024 * 1024

NT = (((1,), (1,)), ((), ()))
TN_DIMS = (((0,), (0,)), ((), ()))

SLOT_C, SLOT_ONE, SLOT_LSE = 0, 3, 6


def _params(*sem):
    return pltpu.CompilerParams(dimension_semantics=sem, vmem_limit_bytes=VMEM_LIMIT)


def _sigmoid(x):
    return 1.0 / (1.0 + jnp.exp(-x))


def _split3(x):
    hi = x.astype(BF16).astype(F32)
    rest = x - hi
    mid = rest.astype(BF16).astype(F32)
    return hi, mid, rest - mid


def _after_specs(after):
    return [pl.BlockSpec(memory_space=pl.ANY)] * len(after)


def _slot_width(cols):
    return LANES * (-(-(cols + (N_DEV - 1) * (cols % LANES)) // LANES))


def _inproj_fwd(x, gam, slots, after=()):
    S, D = x.shape
    n_dev, _, sw = slots.shape
    stride = sw - LANES
    width = stride * n_dev + LANES
    N = width - LANES
    tm, tn = min(TM, S), TN

    def body(x_ref, g_ref, s_ref, *rest):
        proj_ref, h_ref, z_ref, w_ref = rest[-4:]

        @pl.when(pl.program_id(0) == 0)
        def _():
            for n in range(n_dev):
                base = stride * n
                first = s_ref[n, :, 0:LANES]
                if n > 0:
                    first = first + s_ref[n - 1, :, stride:sw]
                w_ref[:, base:base + LANES] = first
                w_ref[:, base + LANES:base + stride] = s_ref[n, :, LANES:stride]
            w_ref[:, stride * n_dev:width] = s_ref[n_dev - 1, :, stride:sw]

        xf = x_ref[...]
        r = lax.rsqrt(jnp.mean(xf * xf, axis=-1, keepdims=True) + RMS_EPS)
        h = ((xf * r) * g_ref[...]).astype(BF16)
        h_ref[...] = h
        z_ref[...] = jnp.dot(h, w_ref[:, N:width], preferred_element_type=F32)
        for n in range(N // tn):
            cols = slice(n * tn, (n + 1) * tn)
            proj_ref[:, cols] = jnp.dot(h, w_ref[:, cols], preferred_element_type=F32).astype(BF16)

    return pl.pallas_call(
        body, name="inproj_fwd", grid=(S // tm,),
        in_specs=[pl.BlockSpec((tm, D), lambda i: (i, 0)),
                  pl.BlockSpec((1, D), lambda i: (0, 0)),
                  pl.BlockSpec((n_dev, D, sw), lambda i: (0, 0, 0))] + _after_specs(after),
        out_specs=[pl.BlockSpec((tm, N), lambda i: (i, 0)),
                   pl.BlockSpec((tm, D), lambda i: (i, 0)),
                   pl.BlockSpec((tm, LANES), lambda i: (i, 0)),
                   pl.BlockSpec((D, width), lambda i: (0, 0))],
        out_shape=[jax.ShapeDtypeStruct((S, N), BF16),
                   jax.ShapeDtypeStruct((S, D), BF16),
                   jax.ShapeDtypeStruct((S, LANES), F32),
                   jax.ShapeDtypeStruct((D, width), BF16)],
        compiler_params=_params("arbitrary"),
    )(x, gam, slots, *after)


def _fgate_fwd(z, bias, n_heads):
    S = z.shape[0]
    tb = min(TB, S)
    P = n_heads // 2

    def body(z_ref, b_ref, qaug_ref, kaug_ref):
        lane = lax.broadcasted_iota(jnp.int32, (tb, LANES), 1)
        tri = (lax.broadcasted_iota(jnp.int32, (tb, tb), 0)
               >= lax.broadcasted_iota(jnp.int32, (tb, tb), 1)).astype(F32)
        head = lax.broadcasted_iota(jnp.int32, (LANES, P * LANES), 0)
        col = lax.broadcasted_iota(jnp.int32, (LANES, P * LANES), 1)
        home = (head >> 1) * LANES + jnp.where((head & 1) == 0, HEAD_DIM, 0)
        is_head = head < n_heads
        place_q = [jnp.logical_and(is_head, col == home + SLOT_C + n).astype(BF16) for n in range(3)]
        place_k = [jnp.logical_and(is_head, col == home + SLOT_ONE + n).astype(BF16) for n in range(3)]
        slot = lax.broadcasted_iota(jnp.int32, (tb, P * LANES), 1) & (HEAD_DIM - 1)
        q_ones = jnp.logical_and(slot >= SLOT_ONE, slot < SLOT_ONE + 3).astype(F32)
        k_ones = jnp.logical_or(slot < SLOT_C + 3,
                                jnp.logical_and(slot >= SLOT_LSE, slot < SLOT_LSE + 3)).astype(F32)

        local = []
        for b in range(S // tb):
            zz = z_ref[b * tb:(b + 1) * tb, :] + b_ref[...]
            lf = jnp.minimum(zz, 0.0) - jnp.log(1.0 + jnp.exp(-jnp.abs(zz)))
            lf = jnp.where(lane < n_heads, lf, 0.0)
            local.append(jnp.dot(tri, lf, preferred_element_type=F32, precision=lax.Precision.HIGHEST))
        carry = jnp.zeros((1, LANES), F32)
        for b, part_sum in enumerate(local):
            c = part_sum + carry
            carry = c[tb - 1:tb, :]
            qa, ka = q_ones, k_ones
            for n, part in enumerate(_split3(c)):
                qa = qa + jnp.dot(part.astype(BF16), place_q[n], preferred_element_type=F32)
                ka = ka - jnp.dot(part.astype(BF16), place_k[n], preferred_element_type=F32)
            qaug_ref[b * tb:(b + 1) * tb, :] = qa.astype(BF16)
            kaug_ref[b * tb:(b + 1) * tb, :] = ka.astype(BF16)

    return pl.pallas_call(
        body, name="fgate_fwd",
        out_shape=[jax.ShapeDtypeStruct((S, P * LANES), BF16),
                   jax.ShapeDtypeStruct((S, P * LANES), BF16)],
        compiler_params=pltpu.CompilerParams(vmem_limit_bytes=VMEM_LIMIT),
    )(z, bias)


def _window_mean_minus_self(u, pad_ref, w, S):
    pad_ref[0:MAX_WINDOW, :] = jnp.zeros((MAX_WINDOW, LANES), F32)
    pad_ref[MAX_WINDOW:MAX_WINDOW + S, :] = u
    acc = u
    for j in range(1, w):
        acc = acc + pad_ref[MAX_WINDOW - j:MAX_WINDOW - j + S, :]
    t = lax.broadcasted_iota(jnp.int32, (S, LANES), 0)
    cnt = jnp.minimum(t + 1, w).astype(F32)
    return acc / cnt - u, cnt


def _pool_fwd(proj, pool_w, pool_scale):
    S = proj.shape[0]
    G = len(POOL_WINDOWS)

    def body(u_ref, w_ref, s_ref, y_ref, pad_ref):
        g = pl.program_id(0)
        for gi, w in enumerate(POOL_WINDOWS):
            @pl.when(g == gi)
            def _():
                d, _ = _window_mean_minus_self(u_ref[...].astype(F32), pad_ref, w, S)
                y = jnp.dot(d.astype(BF16), w_ref[0].astype(BF16), preferred_element_type=F32)
                y_ref[...] = (y * s_ref[...]).astype(BF16)

    return pl.pallas_call(
        body, name="pool_fwd", grid=(G,),
        in_specs=[pl.BlockSpec((S, LANES), lambda g: (0, g)),
                  pl.BlockSpec((1, LANES, LANES), lambda g: (g, 0, 0)),
                  pl.BlockSpec((1, LANES), lambda g: (0, g))],
        out_specs=pl.BlockSpec((S, LANES), lambda g: (0, g)),
        out_shape=jax.ShapeDtypeStruct((S, G * LANES), BF16),
        scratch_shapes=[pltpu.VMEM((S + MAX_WINDOW, LANES), F32)],
        compiler_params=_params("arbitrary"),
    )(proj, pool_w, pool_scale)


def _head_halves(rows):
    lane = lax.broadcasted_iota(jnp.int32, (rows, LANES), 1)
    return lane, (lane < HEAD_DIM, lane >= HEAD_DIM)


def _attn_fwd(proj, qaug, kaug):
    S = proj.shape[0]
    W = proj.shape[1] // 6
    P = W // LANES
    tq = min(TQ, S)
    nq = S // tq
    qc, kc, vc = 2 * P, 3 * P, 4 * P
    scale = 1.0 / math.sqrt(HEAD_DIM)

    def body(q_ref, k_ref, v_ref, qa_ref, ka_ref, o_ref, qb_ref, qm_scr, m_scr, acc_scr):
        i = pl.program_id(1)
        lane, halves = _head_halves(tq)
        v_ones = ((lane & (HEAD_DIM - 1)) < 3).astype(BF16)
        qs = q_ref[...] * scale
        qm_scr[0] = jnp.where(halves[0], qs, qa_ref[...])
        qm_scr[1] = jnp.where(halves[1], qs, qa_ref[...])
        m_scr[...] = jnp.full(m_scr.shape, NEG_INF, F32)
        acc_scr[...] = jnp.zeros(acc_scr.shape, F32)

        def update(j, on_diagonal):
            keys = pl.ds(pl.multiple_of(j * tq, tq), tq)
            k2, v2, kaug_t = k_ref[keys, :], v_ref[keys, :], ka_ref[keys, :]
            if on_diagonal:
                keep = (lax.broadcasted_iota(jnp.int32, (tq, tq), 0)
                        >= lax.broadcasted_iota(jnp.int32, (tq, tq), 1))
            logits = [lax.dot_general(qm_scr[a], jnp.where(halves[a], k2, kaug_t), NT, preferred_element_type=F32)
                      for a in range(2)]
            for a in range(2):
                s = jnp.where(keep, logits[a], NEG_INF) if on_diagonal else logits[a]
                va = jnp.where(halves[a], v2, v_ones)
                m_prev = m_scr[a]
                m_new = jnp.maximum(m_prev, jnp.max(s, axis=1, keepdims=True))
                p = jnp.exp(s - jnp.tile(m_new, (1, tq // LANES)))
                acc_scr[a] = jnp.exp(m_prev - m_new) * acc_scr[a] + jnp.dot(p.astype(BF16), va,
                                                                              preferred_element_type=F32)
                m_scr[a] = m_new

        def below_diagonal(jj, carry):
            update(2 * jj, False)
            update(2 * jj + 1, False)
            return carry

        lax.fori_loop(0, i // 2, below_diagonal, 0)

        @pl.when(i % 2 == 1)
        def _():
            update(i - 1, False)

        update(i, True)
        acc_a, acc_b = acc_scr[0], acc_scr[1]
        l_a, l_b = acc_a[:, HEAD_DIM:HEAD_DIM + 1], acc_b[:, 0:1]
        o_ref[...] = jnp.where(halves[0], acc_a / l_a, acc_b / l_b).astype(BF16)
        lse = jnp.where(halves[0], m_scr[1] + jnp.log(l_b), m_scr[0] + jnp.log(l_a))
        slot = lane & (HEAD_DIM - 1)
        aug = qa_ref[...].astype(F32)
        for n, part in enumerate(_split3(lse)):
            aug = jnp.where(slot == SLOT_LSE + n, -part, aug)
        qb_ref[...] = aug.astype(BF16)

    tile = lambda col: pl.BlockSpec((tq, LANES), lambda p, i: (i, col + p))
    whole = lambda col: pl.BlockSpec((S, LANES), lambda p, i: (0, col + p))
    return pl.pallas_call(
        body, name="attn_fwd", grid=(P, nq),
        in_specs=[tile(qc), whole(kc), whole(vc), tile(0), whole(0)],
        out_specs=[tile(0), tile(0)],
        out_shape=[jax.ShapeDtypeStruct((S, W), BF16), jax.ShapeDtypeStruct((S, W), BF16)],
        scratch_shapes=[pltpu.VMEM((2, tq, LANES), BF16),
                        pltpu.VMEM((2, tq, LANES), F32),
                        pltpu.VMEM((2, tq, LANES), F32)],
        compiler_params=_params("parallel", "arbitrary"),
    )(proj, proj, proj, qaug, kaug)


def _outproj_fwd(ypool, o, proj, x, wout):
    S, D = x.shape
    W = D // 2
    tm, tn = min(TM, S), TN

    def body(y_ref, o_ref, pg_ref, ag_ref, x_ref, w_ref, xn_ref, mix_ref):
        pg, ag = pg_ref[...].astype(F32), ag_ref[...].astype(F32)
        mix_ref[:, 0:W] = (y_ref[...].astype(F32) * (pg * _sigmoid(pg))).astype(BF16)
        mix_ref[:, W:D] = (o_ref[...].astype(F32) * (ag * _sigmoid(ag))).astype(BF16)
        for n in range(D // tn):
            cols = slice(n * tn, (n + 1) * tn)
            xn_ref[:, cols] = x_ref[:, cols] + jnp.dot(mix_ref[...], w_ref[:, cols], preferred_element_type=F32)

    return pl.pallas_call(
        body, name="outproj_fwd", grid=(S // tm,),
        in_specs=[pl.BlockSpec((tm, W), lambda i: (i, 0)),
                  pl.BlockSpec((tm, W), lambda i: (i, 0)),
                  pl.BlockSpec((tm, W), lambda i: (i, 1)),
                  pl.BlockSpec((tm, W), lambda i: (i, 5)),
                  pl.BlockSpec((tm, D), lambda i: (i, 0)),
                  pl.BlockSpec((D, D), lambda i: (0, 0))],
        out_specs=[pl.BlockSpec((tm, D), lambda i: (i, 0)),
                   pl.BlockSpec((tm, D), lambda i: (i, 0))],
        out_shape=[jax.ShapeDtypeStruct((S, D), F32),
                   jax.ShapeDtypeStruct((S, D), BF16)],
        compiler_params=_params("parallel"),
    )(ypool, o, proj, proj, x, wout)


def _loss_head(x, gam, target):
    S, D = x.shape
    tm = min(TM, S)

    def body(x_ref, g_ref, t_ref, dx_ref, loss_ref, dg_ref):
        @pl.when(pl.program_id(0) == 0)
        def _():
            loss_ref[...] = jnp.zeros(loss_ref.shape, F32)
            dg_ref[...] = jnp.zeros(dg_ref.shape, F32)

        xf, gam_v = x_ref[...], g_ref[...]
        r = lax.rsqrt(jnp.mean(xf * xf, axis=-1, keepdims=True) + RMS_EPS)
        xhat = xf * r
        err = xhat * gam_v - t_ref[...]
        part = jnp.sum(jnp.sum(err * err, axis=-1, keepdims=True), axis=0, keepdims=True)
        loss_ref[...] += part * (0.5 / D)
        dy = err * (1.0 / D)
        dg_ref[...] += jnp.sum(dy * xhat, axis=0, keepdims=True)
        dxhat = dy * gam_v
        dx_ref[...] = r * (dxhat - xhat * jnp.mean(dxhat * xhat, axis=-1, keepdims=True))

    return pl.pallas_call(
        body, name="loss_head", grid=(S // tm,),
        in_specs=[pl.BlockSpec((tm, D), lambda i: (i, 0)),
                  pl.BlockSpec((1, D), lambda i: (0, 0)),
                  pl.BlockSpec((tm, D), lambda i: (i, 0))],
        out_specs=[pl.BlockSpec((tm, D), lambda i: (i, 0)),
                   pl.BlockSpec((8, LANES), lambda i: (0, 0)),
                   pl.BlockSpec((1, D), lambda i: (0, 0))],
        out_shape=[jax.ShapeDtypeStruct((S, D), F32),
                   jax.ShapeDtypeStruct((8, LANES), F32),
                   jax.ShapeDtypeStruct((1, D), F32)],
        compiler_params=_params("arbitrary"),
    )(x, gam, target)


def _outproj_bwd(g, wout, mixed, ypool, o, proj):
    S, D = g.shape
    W = D // 2
    tm = min(TM, S)

    def body(g_ref, w_ref, mix_ref, y_ref, o_ref, pg_ref, ag_ref, dw_ref, da_ref, dgate_ref, doaug_ref):
        @pl.when(pl.program_id(0) == 0)
        def _():
            dw_ref[...] = jnp.zeros(dw_ref.shape, F32)

        gb = g_ref[...].astype(BF16)
        dw_ref[...] += lax.dot_general(mix_ref[...], gb, TN_DIMS, preferred_element_type=F32)
        for half, (val_ref, gate_ref) in enumerate(((y_ref, pg_ref), (o_ref, ag_ref))):
            cols = slice(half * W, (half + 1) * W)
            dmix = lax.dot_general(gb, w_ref[cols, :], NT, preferred_element_type=F32)
            gt = gate_ref[...].astype(F32)
            sg = _sigmoid(gt)
            da_ref[:, cols] = (dmix * (gt * sg)).astype(BF16)
            dgate_ref[:, cols] = (dmix * val_ref[...].astype(F32) * (sg * (1.0 + gt * (1.0 - sg)))).astype(BF16)

        lane, halves = _head_halves(tm)
        slot = lane & (HEAD_DIM - 1)
        for p in range(W // LANES):
            cols = slice(p * LANES, (p + 1) * LANES)
            prod = da_ref[:, W + p * LANES:W + (p + 1) * LANES].astype(F32) * o_ref[:, cols].astype(F32)
            d_a = jnp.sum(jnp.where(halves[0], prod, 0.0), axis=1, keepdims=True)
            d_b = jnp.sum(jnp.where(halves[1], prod, 0.0), axis=1, keepdims=True)
            aug = jnp.zeros((tm, LANES), F32)
            for n, part in enumerate(_split3(jnp.where(halves[0], d_b, d_a))):
                aug = jnp.where(slot == SLOT_C + n, -part, aug)
            doaug_ref[:, cols] = aug.astype(BF16)

    rows = lambda width, col: pl.BlockSpec((tm, width), lambda i: (i, col))
    return pl.pallas_call(
        body, name="outproj_bwd", grid=(S // tm,),
        in_specs=[rows(D, 0), pl.BlockSpec((D, D), lambda i: (0, 0)), rows(D, 0), rows(W, 0), rows(W, 0),
                  rows(W, 1), rows(W, 5)],
        out_specs=[pl.BlockSpec((D, D), lambda i: (0, 0)), rows(D, 0), rows(D, 0), rows(W, 0)],
        out_shape=[jax.ShapeDtypeStruct((D, D), F32),
                   jax.ShapeDtypeStruct((S, D), BF16),
                   jax.ShapeDtypeStruct((S, D), BF16),
                   jax.ShapeDtypeStruct((S, W), BF16)],
        compiler_params=_params("arbitrary"),
    )(g, wout, mixed, ypool, o, proj, proj)


def _section_specs(sections, rows, width):
    specs = [pl.BlockSpec((rows, width), lambda k, c=c: (k, c)) for _, c in sections]
    return specs, [a for a, _ in sections]


def _inproj_bwd_dw(h, sections, dzf):
    S, D = h.shape
    W = D // 2
    n_sec = len(sections)
    N = n_sec * W
    ts = min(TM, S)
    n_steps = S // ts

    def body(h_ref, dz_ref, *rest):
        sec_refs, (dw_ref, dwb_ref) = rest[:n_sec], rest[n_sec:]

        @pl.when(pl.program_id(0) == 0)
        def _():
            dw_ref[...] = jnp.zeros(dw_ref.shape, F32)

        ht = h_ref[...].T
        dw_ref[:, N:N + LANES] += jnp.dot(ht, dz_ref[...], preferred_element_type=F32)
        for n, ref in enumerate(sec_refs):
            dw_ref[:, n * W:(n + 1) * W] += jnp.dot(ht, ref[...], preferred_element_type=F32)

        @pl.when(pl.program_id(0) == n_steps - 1)
        def _():
            dwb_ref[...] = dw_ref[...].astype(BF16)

    sec_specs, sec_arrays = _section_specs(sections, ts, W)
    whole = pl.BlockSpec((D, N + LANES), lambda k: (0, 0))
    return pl.pallas_call(
        body, name="inproj_bwd_dw", grid=(n_steps,),
        in_specs=[pl.BlockSpec((ts, D), lambda k: (k, 0)),
                  pl.BlockSpec((ts, LANES), lambda k: (k, 0))] + sec_specs,
        out_specs=[whole, whole],
        out_shape=[jax.ShapeDtypeStruct((D, N + LANES), F32),
                   jax.ShapeDtypeStruct((D, N + LANES), BF16)],
        compiler_params=_params("arbitrary"),
    )(h, dzf, *sec_arrays)


def _attn_bwd(proj, da, qaug, kaug, doaug):
    S = proj.shape[0]
    W = proj.shape[1] // 6
    P = W // LANES
    tq = min(TQ, S)
    nq = S // tq
    qc, kc, vc = 2 * P, 3 * P, 4 * P
    scale = 1.0 / math.sqrt(HEAD_DIM)

    def body(q_ref, k_ref, v_ref, do_ref, qa_ref, ka_ref, da_ref,
             dq_ref, dk_ref, dv_ref, dqx_ref, dkx_ref, km_scr, vm_scr, dk_scr, dv_scr, dq_scr):
        j = pl.program_id(1)
        lane, halves = _head_halves(tq)

        @pl.when(j == 0)
        def _():
            dq_scr[...] = jnp.zeros(dq_scr.shape, F32)

        v_ones = ((lane & (HEAD_DIM - 1)) < 3).astype(BF16)
        for a in range(2):
            km_scr[a] = jnp.where(halves[a], k_ref[...], ka_ref[...])
            vm_scr[a] = jnp.where(halves[a], v_ref[...], v_ones)
        dk_scr[...] = jnp.zeros(dk_scr.shape, F32)
        dv_scr[...] = jnp.zeros(dv_scr.shape, F32)

        def update(i, on_diagonal):
            rows = pl.ds(pl.multiple_of(i * tq, tq), tq)
            qs = q_ref[rows, :] * scale
            do2, qaug_t, doaug_t = do_ref[rows, :], qa_ref[rows, :], da_ref[rows, :]
            if on_diagonal:
                keep = (lax.broadcasted_iota(jnp.int32, (tq, tq), 0)
                        >= lax.broadcasted_iota(jnp.int32, (tq, tq), 1))
            qas = [jnp.where(halves[a], qs, qaug_t) for a in range(2)]
            logits = [lax.dot_general(qas[a], km_scr[a], NT, preferred_element_type=F32) for a in range(2)]
            dps = [lax.dot_general(jnp.where(halves[a], do2, doaug_t), vm_scr[a], NT, preferred_element_type=F32)
                   for a in range(2)]
            dv = None
            for a in range(2):
                s = jnp.where(keep, logits[a], NEG_INF) if on_diagonal else logits[a]
                p = jnp.exp(s)
                dsb = (p * dps[a]).astype(BF16)
                do0 = jnp.where(halves[a], do2, jnp.zeros_like(do2))
                dv_a = lax.dot_general(p.astype(BF16), do0, TN_DIMS, preferred_element_type=F32)
                dv = dv_a if dv is None else dv + dv_a
                dk_scr[a] += lax.dot_general(dsb, qas[a], TN_DIMS, preferred_element_type=F32)
                dq_scr[a, rows, :] += jnp.dot(dsb, km_scr[a], preferred_element_type=F32)
            dv_scr[...] += dv

        def below_diagonal(n, carry):
            update(j + 1 + 2 * n, False)
            update(j + 2 + 2 * n, False)
            return carry

        update(j, True)
        below = nq - 1 - j
        lax.fori_loop(0, below // 2, below_diagonal, 0)

        @pl.when(below % 2 == 1)
        def _():
            update(nq - 1, False)


        dk_ref[...] = jnp.where(halves[0], dk_scr[0], dk_scr[1]).astype(BF16)
        dkx_ref[...] = jnp.where(halves[0], dk_scr[1], dk_scr[0])
        dv_ref[...] = dv_scr[...].astype(BF16)

        @pl.when(j == nq - 1)
        def _():
            row_lane, row_halves = _head_halves(S)
            dq_ref[...] = (jnp.where(row_halves[0], dq_scr[0], dq_scr[1]) * scale).astype(BF16)
            dqx_ref[...] = jnp.where(row_halves[0], dq_scr[1], dq_scr[0])

    tile = lambda col: pl.BlockSpec((tq, LANES), lambda p, j: (j, col + p))
    whole = lambda col: pl.BlockSpec((S, LANES), lambda p, j: (0, col + p))
    return pl.pallas_call(
        body, name="attn_bwd", grid=(P, nq),
        in_specs=[whole(qc), tile(kc), tile(vc), whole(P), whole(0), tile(0), whole(0)],
        out_specs=[whole(0), tile(0), tile(0), whole(0), tile(0)],
        out_shape=[jax.ShapeDtypeStruct((S, W), BF16),
                   jax.ShapeDtypeStruct((S, W), BF16),
                   jax.ShapeDtypeStruct((S, W), BF16),
                   jax.ShapeDtypeStruct((S, W), F32),
                   jax.ShapeDtypeStruct((S, W), F32)],
        scratch_shapes=[pltpu.VMEM((2, tq, LANES), BF16),
                        pltpu.VMEM((2, tq, LANES), BF16),
                        pltpu.VMEM((2, tq, LANES), F32),
                        pltpu.VMEM((tq, LANES), F32),
                        pltpu.VMEM((2, S, LANES), F32)],
        compiler_params=_params("parallel", "arbitrary"),
    )(proj, proj, proj, da, qaug, kaug, doaug)


def _fgate_bwd(dqx, dkx, z, bias, n_heads):
    S = z.shape[0]
    tb = min(TB, S)
    nb = S // tb

    def body(dqx_ref, dkx_ref, z_ref, b_ref, dz_ref, db_ref):
        tri = (lax.broadcasted_iota(jnp.int32, (tb, tb), 1)
               >= lax.broadcasted_iota(jnp.int32, (tb, tb), 0)).astype(F32)
        n_cols = dqx_ref.shape[1]
        col = lax.broadcasted_iota(jnp.int32, (n_cols, LANES), 0)
        head = lax.broadcasted_iota(jnp.int32, (n_cols, LANES), 1)
        home = (head >> 1) * LANES + jnp.where((head & 1) == 0, HEAD_DIM, 0)
        is_head = head < n_heads
        pick_rows = jnp.logical_and(is_head, col == home + SLOT_C).astype(F32)
        pick_cols = jnp.logical_and(is_head, col == home + SLOT_ONE).astype(F32)

        local = []
        for b in range(nb):
            rows = slice(b * tb, (b + 1) * tb)
            dc = (jnp.dot(dqx_ref[rows, :], pick_rows, preferred_element_type=F32, precision=lax.Precision.HIGHEST)
                  - jnp.dot(dkx_ref[rows, :], pick_cols, preferred_element_type=F32, precision=lax.Precision.HIGHEST))
            local.append(jnp.dot(tri, dc, preferred_element_type=F32, precision=lax.Precision.HIGHEST))
        carry = jnp.zeros((1, LANES), F32)
        db = jnp.zeros((1, LANES), F32)
        for b in reversed(range(nb)):
            rows = slice(b * tb, (b + 1) * tb)
            rc = local[b] + carry
            carry = rc[0:1, :]
            dz = rc * _sigmoid(-(z_ref[rows, :] + b_ref[...]))
            dz_ref[rows, :] = dz.astype(BF16)
            db = db + jnp.sum(dz, axis=0, keepdims=True)
        db_ref[...] = db

    return pl.pallas_call(
        body, name="fgate_bwd",
        out_shape=[jax.ShapeDtypeStruct((S, LANES), BF16),
                   jax.ShapeDtypeStruct((1, LANES), F32)],
        compiler_params=pltpu.CompilerParams(vmem_limit_bytes=VMEM_LIMIT),
    )(dqx, dkx, z, bias)


def _pool_bwd(proj, da, pool_w, pool_scale):
    S = proj.shape[0]
    G = len(POOL_WINDOWS)

    def body(u_ref, dy_ref, w_ref, s_ref, du_ref, dw_ref, ds_ref, pad_ref):
        g = pl.program_id(0)
        for gi, w in enumerate(POOL_WINDOWS):
            @pl.when(g == gi)
            def _():
                d, cnt = _window_mean_minus_self(u_ref[...].astype(F32), pad_ref, w, S)
                db = d.astype(BF16)
                wb = w_ref[0].astype(BF16)
                yraw = jnp.dot(db, wb, preferred_element_type=F32)
                dy = dy_ref[...].astype(F32)
                ds_ref[...] = jnp.sum(dy * yraw, axis=0, keepdims=True)
                dzb = (dy * s_ref[...]).astype(BF16)
                dw_ref[0] = lax.dot_general(db, dzb, TN_DIMS, preferred_element_type=F32)
                dd = lax.dot_general(dzb, wb, NT, preferred_element_type=F32)
                pad_ref[0:S, :] = dd / cnt
                pad_ref[S:S + MAX_WINDOW, :] = jnp.zeros((MAX_WINDOW, LANES), F32)
                acc = -dd
                for j in range(w):
                    acc = acc + pad_ref[j:j + S, :]
                du_ref[...] = acc.astype(BF16)

    return pl.pallas_call(
        body, name="pool_bwd", grid=(G,),
        in_specs=[pl.BlockSpec((S, LANES), lambda g: (0, g)),
                  pl.BlockSpec((S, LANES), lambda g: (0, g)),
                  pl.BlockSpec((1, LANES, LANES), lambda g: (g, 0, 0)),
                  pl.BlockSpec((1, LANES), lambda g: (0, g))],
        out_specs=[pl.BlockSpec((S, LANES), lambda g: (0, g)),
                   pl.BlockSpec((1, LANES, LANES), lambda g: (g, 0, 0)),
                   pl.BlockSpec((1, LANES), lambda g: (0, g))],
        out_shape=[jax.ShapeDtypeStruct((S, G * LANES), BF16),
                   jax.ShapeDtypeStruct((G, LANES, LANES), F32),
                   jax.ShapeDtypeStruct((1, G * LANES), F32)],
        scratch_shapes=[pltpu.VMEM((S + MAX_WINDOW, LANES), F32)],
        compiler_params=_params("arbitrary"),
    )(proj, da, pool_w, pool_scale)


def _inproj_bwd_dx(sections, dzf, w, x, gam, g, after=()):
    S, D = x.shape
    N = w.shape[1] - LANES
    W = D // 2
    n_sec = len(sections)
    tm = min(TM // 2, S)

    def body(dz_ref, w_ref, wf_ref, x_ref, gam_ref, g_ref, *rest):
        sec_refs = rest[:n_sec]
        dx_ref, dg_ref = rest[-2:]

        @pl.when(pl.program_id(0) == 0)
        def _():
            dg_ref[...] = jnp.zeros(dg_ref.shape, F32)

        dh = lax.dot_general(dz_ref[...], wf_ref[...], NT, preferred_element_type=F32)
        for n, ref in enumerate(sec_refs):
            dh = dh + lax.dot_general(ref[...], w_ref[:, n * W:(n + 1) * W], NT, preferred_element_type=F32)
        xf = x_ref[...]
        r = lax.rsqrt(jnp.mean(xf * xf, axis=-1, keepdims=True) + RMS_EPS)
        xhat = xf * r
        dg_ref[...] += jnp.sum(dh * xhat, axis=0, keepdims=True)
        dxhat = dh * gam_ref[...]
        dx_ref[...] = g_ref[...] + r * (dxhat - xhat * jnp.mean(dxhat * xhat, axis=-1, keepdims=True))

    sec_specs, sec_arrays = _section_specs(sections, tm, W)
    return pl.pallas_call(
        body, name="inproj_bwd_dx", grid=(S // tm,),
        in_specs=[pl.BlockSpec((tm, LANES), lambda i: (i, 0)),
                  pl.BlockSpec((D, N), lambda i: (0, 0)),
                  pl.BlockSpec((D, LANES), lambda i: (0, N // LANES)),
                  pl.BlockSpec((tm, D), lambda i: (i, 0)),
                  pl.BlockSpec((1, D), lambda i: (0, 0)),
                  pl.BlockSpec((tm, D), lambda i: (i, 0))] + sec_specs + _after_specs(after),
        out_specs=[pl.BlockSpec((tm, D), lambda i: (i, 0)),
                   pl.BlockSpec((1, D), lambda i: (0, 0))],
        out_shape=[jax.ShapeDtypeStruct((S, D), F32),
                   jax.ShapeDtypeStruct((1, D), F32)],
        compiler_params=_params("arbitrary"),
    )(dzf, w, w, x, gam, g, *sec_arrays, *after)


def _adamw(w, m, v, gsets, name, rows, shifted=False):
    A, R, C = w.shape
    tr = min(rows, R)
    c1 = 1.0 / (1.0 - ADAM_B1 ** ADAM_STEP)
    c2 = 1.0 / (1.0 - ADAM_B2 ** ADAM_STEP)
    counts = [len(gs) for gs in gsets]

    def body(w_ref, m_ref, v_ref, *rest):
        g_ref, d_ref, nm_ref, nv_ref = rest[-4:]
        at = 0
        for a in range(A):
            part_refs = rest[at:at + counts[a]]
            at += counts[a]

            @pl.when(pl.program_id(0) == a)
            def _():
                g = None
                for ref in part_refs:
                    for s in range(ref.shape[0]):
                        term = ref[s].astype(F32)
                        g = term if g is None else g + term
                if shifted:
                    lanes = g.shape[1]
                    g = pltpu.roll(g, (lanes - _index(_position()) * (C % LANES)) % lanes, axis=1)[:, :C]
                nm = ADAM_B1 * m_ref[0] + (1.0 - ADAM_B1) * g
                nv = ADAM_B2 * v_ref[0] + (1.0 - ADAM_B2) * (g * g)
                g_ref[0] = g
                nm_ref[0] = nm
                nv_ref[0] = nv
                d_ref[0] = -ADAM_LR * ((nm * c1) / (jnp.sqrt(nv * c2) + ADAM_EPS) + ADAM_WD * w_ref[0])

    spec = pl.BlockSpec((1, tr, C), lambda a, r: (a, r, 0))
    part_specs = [pl.BlockSpec((part.shape[0], tr, part.shape[2]), lambda a, r, l=l: (0, jnp.where(a == l, r, 0), 0))
                  for l, gs in enumerate(gsets) for part in gs]
    shape = jax.ShapeDtypeStruct((A, R, C), F32)
    return pl.pallas_call(
        body, name=name, grid=(A, R // tr),
        in_specs=[spec, spec, spec] + part_specs,
        out_specs=[spec, spec, spec, spec],
        out_shape=[shape, shape, shape, shape],
        compiler_params=_params("arbitrary", "arbitrary"),
    )(w, m, v, *[part for gs in gsets for part in gs])


def _position():
    return lax.axis_index("x"), lax.axis_index("y"), lax.axis_index("c")


def _index(dev):
    return 4 * dev[0] + 2 * dev[1] + dev[2]


def _all_gather(arrs, slots, out_shapes, name):
    n_arr = len(arrs)

    def body(*refs):
        ins, outs = refs[:n_arr], refs[n_arr:2 * n_arr]
        send_sems, recv_sems, local_sems = refs[2 * n_arr:]
        x, y, c = _position()
        me, sibling = (x, y, c), (x, y, 1 - c)
        chips = [(1 - x, y), (x, 1 - y), (1 - x, 1 - y)]

        def copy(a, k, block, to, src=None):
            part = slots[a](outs[a], _index(block))
            return pltpu.make_async_remote_copy(
                src_ref=part if src is None else src, dst_ref=part,
                send_sem=send_sems.at[a, k], recv_sem=recv_sems.at[a, k],
                device_id=to, device_id_type=MESH)

        mine = [pltpu.make_async_copy(ins[a], slots[a](outs[a], _index(me)), local_sems.at[a])
                for a in range(n_arr)]
        for cp in mine:
            cp.start()
        first = []
        for a in range(n_arr):
            first.append(copy(a, 0, me, sibling, src=ins[a]))
            first += [copy(a, 1 + j, me, (*chip, c), src=ins[a]) for j, chip in enumerate(chips)]
        for cp in first:
            cp.start()
        passed = []
        for j, chip in enumerate(chips):
            for a in range(n_arr):
                copy(a, 1 + j, (*chip, c), me).wait_recv()
                fwd = copy(a, 4 + j, (*chip, c), sibling)
                fwd.start()
                passed.append(fwd)
        for a in range(n_arr):
            copy(a, 0, sibling, me).wait_recv()
            for j, chip in enumerate(chips):
                copy(a, 4 + j, (*chip, 1 - c), me).wait_recv()
        for cp in first + passed:
            cp.wait_send()
        for cp in mine:
            cp.wait()

    any_spec = pl.BlockSpec(memory_space=pl.ANY)
    return pl.pallas_call(
        body, name=name,
        in_specs=[any_spec] * n_arr, out_specs=[any_spec] * n_arr, out_shape=out_shapes,
        scratch_shapes=[pltpu.SemaphoreType.DMA((n_arr, 7)), pltpu.SemaphoreType.DMA((n_arr, 7)),
                        pltpu.SemaphoreType.DMA((n_arr,))],
    )(*arrs)


def _split_copies(srcs, lands, send_sems, recv_sems, kinds):
    x, y, c = _position()
    me = _index((x, y, c))
    copies = []
    for a, (src_part, land_part) in enumerate(kinds):
        for k in range(1, N_DEV):
            peer = (x ^ ((k >> 2) & 1), y ^ ((k >> 1) & 1), c ^ (k & 1))
            copies.append(pltpu.make_async_remote_copy(
                src_ref=src_part(srcs[a], _index(peer)), dst_ref=land_part(lands[a], me, k),
                send_sem=send_sems[a].at[k - 1], recv_sem=recv_sems[a].at[k - 1],
                device_id=peer, device_id_type=MESH))
    return copies


def _split_start(srcs, lands, kinds, name, after=()):
    n = len(srcs)

    def body(*refs):
        src_refs, land_refs = refs[:n], refs[n:2 * n]
        outs = refs[2 * n + len(after):]
        send_sems, recv_sems = outs[:n], outs[n:2 * n]
        token = outs[-1]
        for cp in _split_copies(src_refs, land_refs, send_sems, recv_sems, kinds):
            cp.start()
        token[...] = jnp.zeros(token.shape, token.dtype)

    hbm = pl.BlockSpec(memory_space=pltpu.HBM)
    sem = pl.BlockSpec(memory_space=pltpu.SEMAPHORE)
    operands = [pltpu.with_memory_space_constraint(t, pltpu.HBM) for t in (*srcs, *lands)]
    out = pl.pallas_call(
        body, name=name,
        in_specs=[hbm] * (2 * n) + _after_specs(after),
        out_specs=[sem] * (2 * n) + [hbm] * (2 * n) + [pl.BlockSpec(memory_space=pltpu.VMEM)],
        out_shape=[pltpu.SemaphoreType.DMA((N_DEV - 1,))] * (2 * n)
        + [pltpu.HBM(t.shape, t.dtype) for t in operands] + [jax.ShapeDtypeStruct((8, LANES), F32)],
        input_output_aliases={i: 2 * n + i for i in range(2 * n)},
        compiler_params=pltpu.CompilerParams(has_side_effects=pltpu.SideEffectType.DATAFLOW_SIDE_EFFECTING),
    )(*operands, *after)
    return [(out[a], out[n + a], out[2 * n + a], out[3 * n + a]) for a in range(n)], out[-1]


def _split_wait(started, kinds, after, name):
    n = len(started)
    sems = [t[0] for t in started] + [t[1] for t in started]
    srcs = [t[2] for t in started]
    lands = [t[3] for t in started]

    def body(*refs):
        src_refs, land_refs = refs[:n], refs[n:2 * n]
        send_sems, recv_sems = refs[2 * n:3 * n], refs[3 * n:4 * n]
        for cp in _split_copies(src_refs, land_refs, send_sems, recv_sems, kinds):
            cp.wait_send()
            cp.wait_recv()

    hbm = pl.BlockSpec(memory_space=pltpu.HBM)
    sem = pl.BlockSpec(memory_space=pltpu.SEMAPHORE)
    out = pl.pallas_call(
        body, name=name,
        in_specs=[hbm] * (2 * n) + [sem] * (2 * n) + _after_specs(after),
        out_specs=[hbm] * (2 * n),
        out_shape=[pltpu.HBM(t.shape, t.dtype) for t in (*srcs, *lands)],
        input_output_aliases={i: i for i in range(2 * n)},
        compiler_params=pltpu.CompilerParams(has_side_effects=pltpu.SideEffectType.DATAFLOW_SIDE_EFFECTING),
    )(*srcs, *lands, *sems, *after)
    return out[n:]


def _as_rows(p):
    if p.size % LANES == 0:
        rows = p.reshape(-1, LANES)
    else:
        rows = p.reshape(-1, p.shape[-1])
        rows = jnp.pad(rows, ((0, 0), (0, LANES - rows.shape[1])))
    return jnp.pad(rows, ((0, -rows.shape[0] % 8), (0, 0)))


def _pack(parts):
    return jnp.concatenate([_as_rows(p) for p in parts])[None]


def _unpack(packed, like):
    out, at = [], 0
    for p in like:
        whole = p.size % LANES == 0
        n = p.size // LANES if whole else p.size // p.shape[-1]
        rows = packed[0, at:at + n]
        out.append((rows if whole else rows[:, :p.shape[-1]]).reshape(p.shape))
        at += n + (-n % 8)
    return out


def _local_step(x, target, norm_g, forget_bias, pool_w, pool_scale, final_g, weights_in, weights_out, on_grads,
                first_after=()):
    L = norm_g.shape[0]
    S, D = x.shape
    W = D // 2
    H = W // HEAD_DIM
    bias = jnp.pad(forget_bias, ((0, 0), (0, LANES - H)))

    saved = []
    after = tuple(first_after)
    for l in range(L):
        proj, h, z, w = _inproj_fwd(x, norm_g[l:l + 1], weights_in(l, x), after)
        after = ()
        qaug, kaug = _fgate_fwd(z, bias[l:l + 1], H)
        ypool = _pool_fwd(proj, pool_w[l], pool_scale[l:l + 1])
        o, qaug_b = _attn_fwd(proj, qaug, kaug)
        wout = weights_out(l, o)
        x_new, mixed = _outproj_fwd(ypool, o, proj, x, wout)
        saved.append((x, proj, h, z, qaug_b, kaug, ypool, o, mixed, w, wout))
        x = x_new

    g, loss, d_final_g = _loss_head(x, final_g.reshape(1, D), target)

    small = None
    for l in reversed(range(L)):
        x_in, proj, h, z, qaug_b, kaug, ypool, o, mixed, w, wout = saved[l]
        d_wout, da, dgate, doaug = _outproj_bwd(g, wout, mixed, ypool, o, proj)
        dq, dk, dv, dqx, dkx = _attn_bwd(proj, da, qaug_b, kaug, doaug)
        dzf, db = _fgate_bwd(dqx, dkx, z, bias[l:l + 1], H)
        dpu, dpw, dps = _pool_bwd(proj, da, pool_w[l], pool_scale[l:l + 1])
        dproj = [(dpu, 0), (dgate, 0), (dq, 0), (dk, 0), (dv, 0), (dgate, 1)]
        d_w, d_w_bf16 = _inproj_bwd_dw(h, dproj, dzf)
        after = tuple(on_grads(l, d_w, d_w_bf16, d_wout, small))
        g, dgam = _inproj_bwd_dx(dproj, dzf, w, x_in, norm_g[l:l + 1], g, after)
        small = (dgam[0], db[0, :H], dpw, dps[0])
    return loss[0, 0], g, small, d_final_g[0]


def kernel(x, norm_g, w_in, forget_bias, pool_w, pool_scale, w_out, final_g, loss_target, m_norm_g, m_w_in, m_forget_bias, m_pool_w, m_pool_scale, m_w_out, m_final_g, v_norm_g, v_w_in, v_forget_bias, v_pool_w, v_pool_scale, v_w_out, v_final_g):
    L, D, cols = w_in.shape
    rows_out = w_out.shape[1]
    W = D // 2
    H = W // HEAD_DIM
    me = _index(_position())

    slot = _slot_width(cols)
    wout_b = w_out.astype(BF16)
    win_b = [lax.dynamic_update_slice(jnp.zeros((D, slot), BF16), w_in[l].astype(BF16),
                                      (0, me * (cols % LANES))) for l in range(L)]
    gather_in = (lambda ref, peer: ref, lambda ref, mine, k: ref.at[mine])
    gather_out = (lambda ref, peer: ref, lambda ref, mine, k: ref.at[pl.ds(mine * rows_out, rows_out), :])

    (first_in,) = _all_gather([win_b[0]], [lambda ref, n: ref.at[n]],
                              [jax.ShapeDtypeStruct((N_DEV, D, slot), BF16)], "gather_first")
    rest_srcs = [wout_b[0]] + [w[l] for l in range(1, L) for w in (win_b, wout_b)]
    rest_lands = [jnp.tile(wout_b[0], (N_DEV, 1))]
    for l in range(1, L):
        rest_lands += [jnp.broadcast_to(win_b[l][None], (N_DEV, D, slot)), jnp.tile(wout_b[l], (N_DEV, 1))]
    rest_kinds = [gather_out] + [gather_in, gather_out] * (L - 1)
    rest, rest_token = _split_start(rest_srcs, rest_lands, rest_kinds, "gather_start_rest", (first_in,))

    def weights_in(l, x_in):
        if l == 0:
            return first_in
        (win_all,) = _split_wait([rest[2 * l - 1]], [gather_in], (x_in,), f"gather_wait_in_{l}")
        return win_all

    def weights_out(l, o):
        (wout_full,) = _split_wait([rest[2 * l]], [gather_out], (o,), f"gather_wait_out_{l}")
        return wout_full

    stride = slot - LANES
    exchange_kinds = [(lambda ref, peer: ref.at[:, pl.ds(pl.multiple_of(peer * stride, LANES), slot)],
                       lambda ref, mine, k: ref.at[k - 1]),
                      (lambda ref, peer: ref.at[pl.ds(peer * rows_out, rows_out), :],
                       lambda ref, mine, k: ref.at[k - 1])]
    zero_g = jnp.zeros_like(final_g)

    def small_pack(l, norm_g_l, bias_l, pool_w_l, pool_scale_l, final):
        return _pack([norm_g_l, bias_l, pool_w_l, pool_scale_l, final if l == 0 else zero_g])[0]

    exchanges, own_parts = {}, {}

    def on_grads(l, dw, dw_bf16, d_wout, small):
        own_parts[l] = (lax.dynamic_slice_in_dim(dw, me * stride, slot, 1)[None],
                        lax.dynamic_slice_in_dim(d_wout, me * rows_out, rows_out, 0)[None])
        srcs = [dw_bf16, d_wout.astype(BF16)]
        lands = [lax.empty((N_DEV - 1, D, slot), BF16), lax.empty((N_DEV - 1, rows_out, D), BF16)]
        kinds = list(exchange_kinds)
        if small is not None:
            packed_small = small_pack(l + 1, *small, None)
            srcs.append(packed_small)
            lands.append(jnp.broadcast_to(packed_small[None], (N_DEV, *packed_small.shape)))
            kinds.append(gather_in)
        started, token = _split_start(srcs, lands, kinds, f"exchange_start_{l}")
        exchanges[l] = (started, kinds)
        return (token,)

    loss, dx, small_first, d_final_g = _local_step(
        x[0], loss_target[0], norm_g, forget_bias, pool_w, pool_scale, final_g,
        weights_in, weights_out, on_grads, (rest_token,))
    loss = lax.psum(loss, ("x", "y", "c"))
    packed_first = small_pack(0, *small_first, d_final_g)
    first_started, first_token = _split_start(
        [packed_first], [jnp.broadcast_to(packed_first[None], (N_DEV, *packed_first.shape))], [gather_in],
        "small_start")

    gin_sets, gout_sets, small_sets = [], [], [None] * L
    for l in range(L):
        started, kinds = exchanges[l]
        got = _split_wait(started, kinds, (dx, first_token), f"exchange_wait_{l}")
        gin_sets.append([own_parts[l][0], got[0]])
        gout_sets.append([own_parts[l][1], got[1]])
        if len(got) > 2:
            small_sets[l + 1] = [got[2]]
    g_w_in, d_w_in, nm_w_in, nv_w_in = _adamw(w_in, m_w_in, v_w_in, gin_sets, "adamw_w_in", 256, shifted=True)
    g_w_out, d_w_out, nm_w_out, nv_w_out = _adamw(w_out, m_w_out, v_w_out, gout_sets, "adamw_w_out", 128)
    small_sets[0] = _split_wait(first_started, [gather_in], (d_w_out,), "small_wait")

    def small_stack(norm_g_, bias_, pool_w_, pool_scale_, final):
        return jnp.stack([small_pack(l, norm_g_[l], bias_[l], pool_w_[l], pool_scale_[l], final) for l in range(L)])

    packed = _adamw(small_stack(norm_g, forget_bias, pool_w, pool_scale, final_g),
                    small_stack(m_norm_g, m_forget_bias, m_pool_w, m_pool_scale, m_final_g),
                    small_stack(v_norm_g, v_forget_bias, v_pool_w, v_pool_scale, v_final_g),
                    small_sets, "adamw_small", packed_first.shape[0])

    def small_unpack(p):
        like = [norm_g[0], forget_bias[0], pool_w[0], pool_scale[0], final_g]
        layers = [_unpack(p[l:l + 1], like) for l in range(L)]
        return [jnp.stack([layers[l][n] for l in range(L)]) for n in range(4)] + [layers[0][4]]

    g_s, d_s, nm_s, nv_s = [small_unpack(p) for p in packed]

    def order(big_in, big_out, small):
        return (small[0], big_in, small[1], small[2], small[3], big_out, small[4])

    return (loss, dx[None], *order(g_w_in, g_w_out, g_s), *order(d_w_in, d_w_out, d_s),
            *order(nm_w_in, nm_w_out, nm_s), *order(nv_w_in, nv_w_out, nv_s))
```

```python
import math

import jax
import jax.numpy as jnp
from jax import lax
from jax.experimental import pallas as pl
from jax.experimental.pallas import tpu as pltpu

F32 = jnp.float32
BF16 = jnp.bfloat16
MESH = pl.DeviceIdType.MESH

RMS_EPS = 1e-6
NEG_INF = -1e30
HEAD_DIM = 64
POOL_WINDOWS = (2, 4, 8, 16)
MAX_WINDOW = 16
LANES = 128
N_DEV = 8

ADAM_LR = 0.001
ADAM_B1 = 0.9
ADAM_B2 = 0.999
ADAM_EPS = 1e-08
ADAM_WD = 0.01
ADAM_STEP = 10

TM = 512
TN = 512
TQ = 512
TB = 256
VMEM_LIMIT = 56 * 1024 * 1024

NT = (((1,), (1,)), ((), ()))
TN_DIMS = (((0,), (0,)), ((), ()))

SLOT_C, SLOT_ONE, SLOT_LSE = 0, 3, 6


def _params(*sem):
    return pltpu.CompilerParams(dimension_semantics=sem, vmem_limit_bytes=VMEM_LIMIT)


def _sigmoid(x):
    return 1.0 / (1.0 + jnp.exp(-x))


def _split3(x):
    hi = x.astype(BF16).astype(F32)
    rest = x - hi
    mid = rest.astype(BF16).astype(F32)
    return hi, mid, rest - mid


def _after_specs(after):
    return [pl.BlockSpec(memory_space=pl.ANY)] * len(after)


def _slot_width(cols):
    return LANES * (-(-(cols + (N_DEV - 1) * (cols % LANES)) // LANES))


def _inproj_fwd(x, gam, slots, after=()):
    S, D = x.shape
    n_dev, _, sw = slots.shape
    stride = sw - LANES
    width = stride * n_dev + LANES
    N = width - LANES
    tm, tn = min(TM, S), TN

    def body(x_ref, g_ref, s_ref, *rest):
        proj_ref, h_ref, z_ref, w_ref = rest[-4:]

        @pl.when(pl.program_id(0) == 0)
        def _():
            for n in range(n_dev):
                base = stride * n
                first = s_ref[n, :, 0:LANES]
                if n > 0:
                    first = first + s_ref[n - 1, :, stride:sw]
                w_ref[:, base:base + LANES] = first
                w_ref[:, base + LANES:base + stride] = s_ref[n, :, LANES:stride]
            w_ref[:, stride * n_dev:width] = s_ref[n_dev - 1, :, stride:sw]

        xf = x_ref[...]
        r = lax.rsqrt(jnp.mean(xf * xf, axis=-1, keepdims=True) + RMS_EPS)
        h = ((xf * r) * g_ref[...]).astype(BF16)
        h_ref[...] = h
        z_ref[...] = jnp.dot(h, w_ref[:, N:width], preferred_element_type=F32)
        for n in range(N // tn):
            cols = slice(n * tn, (n + 1) * tn)
            proj_ref[:, cols] = jnp.dot(h, w_ref[:, cols], preferred_element_type=F32).astype(BF16)

    return pl.pallas_call(
        body, name="inproj_fwd", grid=(S // tm,),
        in_specs=[pl.BlockSpec((tm, D), lambda i: (i, 0)),
                  pl.BlockSpec((1, D), lambda i: (0, 0)),
                  pl.BlockSpec((n_dev, D, sw), lambda i: (0, 0, 0))] + _after_specs(after),
        out_specs=[pl.BlockSpec((tm, N), lambda i: (i, 0)),
                   pl.BlockSpec((tm, D), lambda i: (i, 0)),
                   pl.BlockSpec((tm, LANES), lambda i: (i, 0)),
                   pl.BlockSpec((D, width), lambda i: (0, 0))],
        out_shape=[jax.ShapeDtypeStruct((S, N), BF16),
                   jax.ShapeDtypeStruct((S, D), BF16),
                   jax.ShapeDtypeStruct((S, LANES), F32),
                   jax.ShapeDtypeStruct((D, width), BF16)],
        compiler_params=_params("arbitrary"),
    )(x, gam, slots, *after)


def _fgate_fwd(z, bias, n_heads):
    S = z.shape[0]
    tb = min(TB, S)
    P = n_heads // 2

    def body(z_ref, b_ref, qaug_ref, kaug_ref):
        lane = lax.broadcasted_iota(jnp.int32, (tb, LANES), 1)
        tri = (lax.broadcasted_iota(jnp.int32, (tb, tb), 0)
               >= lax.broadcasted_iota(jnp.int32, (tb, tb), 1)).astype(F32)
        head = lax.broadcasted_iota(jnp.int32, (LANES, P * LANES), 0)
        col = lax.broadcasted_iota(jnp.int32, (LANES, P * LANES), 1)
        home = (head >> 1) * LANES + jnp.where((head & 1) == 0, HEAD_DIM, 0)
        is_head = head < n_heads
        place_q = [jnp.logical_and(is_head, col == home + SLOT_C + n).astype(BF16) for n in range(3)]
        place_k = [jnp.logical_and(is_head, col == home + SLOT_ONE + n).astype(BF16) for n in range(3)]
        slot = lax.broadcasted_iota(jnp.int32, (tb, P * LANES), 1) & (HEAD_DIM - 1)
        q_ones = jnp.logical_and(slot >= SLOT_ONE, slot < SLOT_ONE + 3).astype(F32)
        k_ones = jnp.logical_or(slot < SLOT_C + 3,
                                jnp.logical_and(slot >= SLOT_LSE, slot < SLOT_LSE + 3)).astype(F32)

        local = []
        for b in range(S // tb):
            zz = z_ref[b * tb:(b + 1) * tb, :] + b_ref[...]
            lf = jnp.minimum(zz, 0.0) - jnp.log(1.0 + jnp.exp(-jnp.abs(zz)))
            lf = jnp.where(lane < n_heads, lf, 0.0)
            local.append(jnp.dot(tri, lf, preferred_element_type=F32, precision=lax.Precision.HIGHEST))
        carry = jnp.zeros((1, LANES), F32)
        for b, part_sum in enumerate(local):
            c = part_sum + carry
            carry = c[tb - 1:tb, :]
            qa, ka = q_ones, k_ones
            for n, part in enumerate(_split3(c)):
                qa = qa + jnp.dot(part.astype(BF16), place_q[n], preferred_element_type=F32)
                ka = ka - jnp.dot(part.astype(BF16), place_k[n], preferred_element_type=F32)
            qaug_ref[b * tb:(b + 1) * tb, :] = qa.astype(BF16)
            kaug_ref[b * tb:(b + 1) * tb, :] = ka.astype(BF16)

    return pl.pallas_call(
        body, name="fgate_fwd",
        out_shape=[jax.ShapeDtypeStruct((S, P * LANES), BF16),
                   jax.ShapeDtypeStruct((S, P * LANES), BF16)],
        compiler_params=pltpu.CompilerParams(vmem_limit_bytes=VMEM_LIMIT),
    )(z, bias)


def _window_mean_minus_self(u, pad_ref, w, S):
    pad_ref[0:MAX_WINDOW, :] = jnp.zeros((MAX_WINDOW, LANES), F32)
    pad_ref[MAX_WINDOW:MAX_WINDOW + S, :] = u
    acc = u
    for j in range(1, w):
        acc = acc + pad_ref[MAX_WINDOW - j:MAX_WINDOW - j + S, :]
    t = lax.broadcasted_iota(jnp.int32, (S, LANES), 0)
    cnt = jnp.minimum(t + 1, w).astype(F32)
    return acc / cnt - u, cnt


def _pool_fwd(proj, pool_w, pool_scale):
    S = proj.shape[0]
    G = len(POOL_WINDOWS)

    def body(u_ref, w_ref, s_ref, y_ref, pad_ref):
        g = pl.program_id(0)
        for gi, w in enumerate(POOL_WINDOWS):
            @pl.when(g == gi)
            def _():
                d, _ = _window_mean_minus_self(u_ref[...].astype(F32), pad_ref, w, S)
                y = jnp.dot(d.astype(BF16), w_ref[0].astype(BF16), preferred_element_type=F32)
                y_ref[...] = (y * s_ref[...]).astype(BF16)

    return pl.pallas_call(
        body, name="pool_fwd", grid=(G,),
        in_specs=[pl.BlockSpec((S, LANES), lambda g: (0, g)),
                  pl.BlockSpec((1, LANES, LANES), lambda g: (g, 0, 0)),
                  pl.BlockSpec((1, LANES), lambda g: (0, g))],
        out_specs=pl.BlockSpec((S, LANES), lambda g: (0, g)),
        out_shape=jax.ShapeDtypeStruct((S, G * LANES), BF16),
        scratch_shapes=[pltpu.VMEM((S + MAX_WINDOW, LANES), F32)],
        compiler_params=_params("arbitrary"),
    )(proj, pool_w, pool_scale)


def _head_halves(rows):
    lane = lax.broadcasted_iota(jnp.int32, (rows, LANES), 1)
    return lane, (lane < HEAD_DIM, lane >= HEAD_DIM)


def _attn_fwd(proj, qaug, kaug):
    S = proj.shape[0]
    W = proj.shape[1] // 6
    P = W // LANES
    tq = min(TQ, S)
    nq = S // tq
    qc, kc, vc = 2 * P, 3 * P, 4 * P
    scale = 1.0 / math.sqrt(HEAD_DIM)

    def body(q_ref, k_ref, v_ref, qa_ref, ka_ref, o_ref, qb_ref, qm_scr, m_scr, acc_scr):
        i = pl.program_id(1)
        lane, halves = _head_halves(tq)
        v_ones = ((lane & (HEAD_DIM - 1)) < 3).astype(BF16)
        qs = q_ref[...] * scale
        qm_scr[0] = jnp.where(halves[0], qs, qa_ref[...])
        qm_scr[1] = jnp.where(halves[1], qs, qa_ref[...])
        m_scr[...] = jnp.full(m_scr.shape, NEG_INF, F32)
        acc_scr[...] = jnp.zeros(acc_scr.shape, F32)

        def update(j, on_diagonal):
            keys = pl.ds(pl.multiple_of(j * tq, tq), tq)
            k2, v2, kaug_t = k_ref[keys, :], v_ref[keys, :], ka_ref[keys, :]
            if on_diagonal:
                keep = (lax.broadcasted_iota(jnp.int32, (tq, tq), 0)
                        >= lax.broadcasted_iota(jnp.int32, (tq, tq), 1))
            logits = [lax.dot_general(qm_scr[a], jnp.where(halves[a], k2, kaug_t), NT, preferred_element_type=F32)
                      for a in range(2)]
            for a in range(2):
                s = jnp.where(keep, logits[a], NEG_INF) if on_diagonal else logits[a]
                va = jnp.where(halves[a], v2, v_ones)
                m_prev = m_scr[a]
                m_new = jnp.maximum(m_prev, jnp.max(s, axis=1, keepdims=True))
                p = jnp.exp(s - jnp.tile(m_new, (1, tq // LANES)))
                acc_scr[a] = jnp.exp(m_prev - m_new) * acc_scr[a] + jnp.dot(p.astype(BF16), va,
                                                                              preferred_element_type=F32)
                m_scr[a] = m_new

        def below_diagonal(jj, carry):
            update(2 * jj, False)
            update(2 * jj + 1, False)
            return carry

        lax.fori_loop(0, i // 2, below_diagonal, 0)

        @pl.when(i % 2 == 1)
        def _():
            update(i - 1, False)

        update(i, True)
        acc_a, acc_b = acc_scr[0], acc_scr[1]
        l_a, l_b = acc_a[:, HEAD_DIM:HEAD_DIM + 1], acc_b[:, 0:1]
        o_ref[...] = jnp.where(halves[0], acc_a / l_a, acc_b / l_b).astype(BF16)
        lse = jnp.where(halves[0], m_scr[1] + jnp.log(l_b), m_scr[0] + jnp.log(l_a))
        slot = lane & (HEAD_DIM - 1)
        aug = qa_ref[...].astype(F32)
        for n, part in enumerate(_split3(lse)):
            aug = jnp.where(slot == SLOT_LSE + n, -part, aug)
        qb_ref[...] = aug.astype(BF16)

    tile = lambda col: pl.BlockSpec((tq, LANES), lambda p, i: (i, col + p))
    whole = lambda col: pl.BlockSpec((S, LANES), lambda p, i: (0, col + p))
    return pl.pallas_call(
        body, name="attn_fwd", grid=(P, nq),
        in_specs=[tile(qc), whole(kc), whole(vc), tile(0), whole(0)],
        out_specs=[tile(0), tile(0)],
        out_shape=[jax.ShapeDtypeStruct((S, W), BF16), jax.ShapeDtypeStruct((S, W), BF16)],
        scratch_shapes=[pltpu.VMEM((2, tq, LANES), BF16),
                        pltpu.VMEM((2, tq, LANES), F32),
                        pltpu.VMEM((2, tq, LANES), F32)],
        compiler_params=_params("parallel", "arbitrary"),
    )(proj, proj, proj, qaug, kaug)


def _outproj_fwd(ypool, o, proj, x, wout):
    S, D = x.shape
    W = D // 2
    tm, tn = min(TM, S), TN

    def body(y_ref, o_ref, pg_ref, ag_ref, x_ref, w_ref, xn_ref, mix_ref):
        pg, ag = pg_ref[...].astype(F32), ag_ref[...].astype(F32)
        mix_ref[:, 0:W] = (y_ref[...].astype(F32) * (pg * _sigmoid(pg))).astype(BF16)
        mix_ref[:, W:D] = (o_ref[...].astype(F32) * (ag * _sigmoid(ag))).astype(BF16)
        for n in range(D // tn):
            cols = slice(n * tn, (n + 1) * tn)
            xn_ref[:, cols] = x_ref[:, cols] + jnp.dot(mix_ref[...], w_ref[:, cols], preferred_element_type=F32)

    return pl.pallas_call(
        body, name="outproj_fwd", grid=(S // tm,),
        in_specs=[pl.BlockSpec((tm, W), lambda i: (i, 0)),
                  pl.BlockSpec((tm, W), lambda i: (i, 0)),
                  pl.BlockSpec((tm, W), lambda i: (i, 1)),
                  pl.BlockSpec((tm, W), lambda i: (i, 5)),
                  pl.BlockSpec((tm, D), lambda i: (i, 0)),
                  pl.BlockSpec((D, D), lambda i: (0, 0))],
        out_specs=[pl.BlockSpec((tm, D), lambda i: (i, 0)),
                   pl.BlockSpec((tm, D), lambda i: (i, 0))],
        out_shape=[jax.ShapeDtypeStruct((S, D), F32),
                   jax.ShapeDtypeStruct((S, D), BF16)],
        compiler_params=_params("parallel"),
    )(ypool, o, proj, proj, x, wout)


def _loss_head(x, gam, target):
    S, D = x.shape
    tm = min(TM, S)

    def body(x_ref, g_ref, t_ref, dx_ref, loss_ref, dg_ref):
        @pl.when(pl.program_id(0) == 0)
        def _():
            loss_ref[...] = jnp.zeros(loss_ref.shape, F32)
            dg_ref[...] = jnp.zeros(dg_ref.shape, F32)

        xf, gam_v = x_ref[...], g_ref[...]
        r = lax.rsqrt(jnp.mean(xf * xf, axis=-1, keepdims=True) + RMS_EPS)
        xhat = xf * r
        err = xhat * gam_v - t_ref[...]
        part = jnp.sum(jnp.sum(err * err, axis=-1, keepdims=True), axis=0, keepdims=True)
        loss_ref[...] += part * (0.5 / D)
        dy = err * (1.0 / D)
        dg_ref[...] += jnp.sum(dy * xhat, axis=0, keepdims=True)
        dxhat = dy * gam_v
        dx_ref[...] = r * (dxhat - xhat * jnp.mean(dxhat * xhat, axis=-1, keepdims=True))

    return pl.pallas_call(
        body, name="loss_head", grid=(S // tm,),
        in_specs=[pl.BlockSpec((tm, D), lambda i: (i, 0)),
                  pl.BlockSpec((1, D), lambda i: (0, 0)),
                  pl.BlockSpec((tm, D), lambda i: (i, 0))],
        out_specs=[pl.BlockSpec((tm, D), lambda i: (i, 0)),
                   pl.BlockSpec((8, LANES), lambda i: (0, 0)),
                   pl.BlockSpec((1, D), lambda i: (0, 0))],
        out_shape=[jax.ShapeDtypeStruct((S, D), F32),
                   jax.ShapeDtypeStruct((8, LANES), F32),
                   jax.ShapeDtypeStruct((1, D), F32)],
        compiler_params=_params("arbitrary"),
    )(x, gam, target)


def _outproj_bwd(g, wout, mixed, ypool, o, proj):
    S, D = g.shape
    W = D // 2
    tm = min(TM, S)

    def body(g_ref, w_ref, mix_ref, y_ref, o_ref, pg_ref, ag_ref, dw_ref, da_ref, dgate_ref, doaug_ref):
        @pl.when(pl.program_id(0) == 0)
        def _():
            dw_ref[...] = jnp.zeros(dw_ref.shape, F32)

        gb = g_ref[...].astype(BF16)
        dw_ref[...] += lax.dot_general(mix_ref[...], gb, TN_DIMS, preferred_element_type=F32)
        for half, (val_ref, gate_ref) in enumerate(((y_ref, pg_ref), (o_ref, ag_ref))):
            cols = slice(half * W, (half + 1) * W)
            dmix = lax.dot_general(gb, w_ref[cols, :], NT, preferred_element_type=F32)
            gt = gate_ref[...].astype(F32)
            sg = _sigmoid(gt)
            da_ref[:, cols] = (dmix * (gt * sg)).astype(BF16)
            dgate_ref[:, cols] = (dmix * val_ref[...].astype(F32) * (sg * (1.0 + gt * (1.0 - sg)))).astype(BF16)

        lane, halves = _head_halves(tm)
        slot = lane & (HEAD_DIM - 1)
        for p in range(W // LANES):
            cols = slice(p * LANES, (p + 1) * LANES)
            prod = da_ref[:, W + p * LANES:W + (p + 1) * LANES].astype(F32) * o_ref[:, cols].astype(F32)
            d_a = jnp.sum(jnp.where(halves[0], prod, 0.0), axis=1, keepdims=True)
            d_b = jnp.sum(jnp.where(halves[1], prod, 0.0), axis=1, keepdims=True)
            aug = jnp.zeros((tm, LANES), F32)
            for n, part in enumerate(_split3(jnp.where(halves[0], d_b, d_a))):
                aug = jnp.where(slot == SLOT_C + n, -part, aug)
            doaug_ref[:, cols] = aug.astype(BF16)

    rows = lambda width, col: pl.BlockSpec((tm, width), lambda i: (i, col))
    return pl.pallas_call(
        body, name="outproj_bwd", grid=(S // tm,),
        in_specs=[rows(D, 0), pl.BlockSpec((D, D), lambda i: (0, 0)), rows(D, 0), rows(W, 0), rows(W, 0),
                  rows(W, 1), rows(W, 5)],
        out_specs=[pl.BlockSpec((D, D), lambda i: (0, 0)), rows(D, 0), rows(D, 0), rows(W, 0)],
        out_shape=[jax.ShapeDtypeStruct((D, D), F32),
                   jax.ShapeDtypeStruct((S, D), BF16),
                   jax.ShapeDtypeStruct((S, D), BF16),
                   jax.ShapeDtypeStruct((S, W), BF16)],
        compiler_params=_params("arbitrary"),
    )(g, wout, mixed, ypool, o, proj, proj)


def _section_specs(sections, rows, width):
    specs = [pl.BlockSpec((rows, width), lambda k, c=c: (k, c)) for _, c in sections]
    return specs, [a for a, _ in sections]


def _inproj_bwd_dw(h, sections, dzf):
    S, D = h.shape
    W = D // 2
    n_sec = len(sections)
    N = n_sec * W
    ts = min(TM, S)
    n_steps = S // ts

    def body(h_ref, dz_ref, *rest):
        sec_refs, (dw_ref, dwb_ref) = rest[:n_sec], rest[n_sec:]

        @pl.when(pl.program_id(0) == 0)
        def _():
            dw_ref[...] = jnp.zeros(dw_ref.shape, F32)

        ht = h_ref[...].T
        dw_ref[:, N:N + LANES] += jnp.dot(ht, dz_ref[...], preferred_element_type=F32)
        for n, ref in enumerate(sec_refs):
            dw_ref[:, n * W:(n + 1) * W] += jnp.dot(ht, ref[...], preferred_element_type=F32)

        @pl.when(pl.program_id(0) == n_steps - 1)
        def _():
            dwb_ref[...] = dw_ref[...].astype(BF16)

    sec_specs, sec_arrays = _section_specs(sections, ts, W)
    whole = pl.BlockSpec((D, N + LANES), lambda k: (0, 0))
    return pl.pallas_call(
        body, name="inproj_bwd_dw", grid=(n_steps,),
        in_specs=[pl.BlockSpec((ts, D), lambda k: (k, 0)),
                  pl.BlockSpec((ts, LANES), lambda k: (k, 0))] + sec_specs,
        out_specs=[whole, whole],
        out_shape=[jax.ShapeDtypeStruct((D, N + LANES), F32),
                   jax.ShapeDtypeStruct((D, N + LANES), BF16)],
        compiler_params=_params("arbitrary"),
    )(h, dzf, *sec_arrays)


def _attn_bwd(proj, da, qaug, kaug, doaug):
    S = proj.shape[0]
    W = proj.shape[1] // 6
    P = W // LANES
    tq = min(TQ, S)
    nq = S // tq
    qc, kc, vc = 2 * P, 3 * P, 4 * P
    scale = 1.0 / math.sqrt(HEAD_DIM)

    def body(q_ref, k_ref, v_ref, do_ref, qa_ref, ka_ref, da_ref,
             dq_ref, dk_ref, dv_ref, dqx_ref, dkx_ref, km_scr, vm_scr, dk_scr, dv_scr, dq_scr):
        j = pl.program_id(1)
        lane, halves = _head_halves(tq)

        @pl.when(j == 0)
        def _():
            dq_scr[...] = jnp.zeros(dq_scr.shape, F32)

        v_ones = ((lane & (HEAD_DIM - 1)) < 3).astype(BF16)
        for a in range(2):
            km_scr[a] = jnp.where(halves[a], k_ref[...], ka_ref[...])
            vm_scr[a] = jnp.where(halves[a], v_ref[...], v_ones)
        dk_scr[...] = jnp.zeros(dk_scr.shape, F32)
        dv_scr[...] = jnp.zeros(dv_scr.shape, F32)

        def update(i, on_diagonal):
            rows = pl.ds(pl.multiple_of(i * tq, tq), tq)
            qs = q_ref[rows, :] * scale
            do2, qaug_t, doaug_t = do_ref[rows, :], qa_ref[rows, :], da_ref[rows, :]
            if on_diagonal:
                keep = (lax.broadcasted_iota(jnp.int32, (tq, tq), 0)
                        >= lax.broadcasted_iota(jnp.int32, (tq, tq), 1))
            qas = [jnp.where(halves[a], qs, qaug_t) for a in range(2)]
            logits = [lax.dot_general(qas[a], km_scr[a], NT, preferred_element_type=F32) for a in range(2)]
            dps = [lax.dot_general(jnp.where(halves[a], do2, doaug_t), vm_scr[a], NT, preferred_element_type=F32)
                   for a in range(2)]
            dv = None
            for a in range(2):
                s = jnp.where(keep, logits[a], NEG_INF) if on_diagonal else logits[a]
                p = jnp.exp(s)
                dsb = (p * dps[a]).astype(BF16)
                do0 = jnp.where(halves[a], do2, jnp.zeros_like(do2))
                dv_a = lax.dot_general(p.astype(BF16), do0, TN_DIMS, preferred_element_type=F32)
                dv = dv_a if dv is None else dv + dv_a
                dk_scr[a] += lax.dot_general(dsb, qas[a], TN_DIMS, preferred_element_type=F32)
                dq_scr[a, rows, :] += jnp.dot(dsb, km_scr[a], preferred_element_type=F32)
            dv_scr[...] += dv

        def below_diagonal(n, carry):
            update(j + 1 + 2 * n, False)
            update(j + 2 + 2 * n, False)
            return carry

        update(j, True)
        below = nq - 1 - j
        lax.fori_loop(0, below // 2, below_diagonal, 0)

        @pl.when(below % 2 == 1)
        def _():
            update(nq - 1, False)


        dk_ref[...] = jnp.where(halves[0], dk_scr[0], dk_scr[1]).astype(BF16)
        dkx_ref[...] = jnp.where(halves[0], dk_scr[1], dk_scr[0])
        dv_ref[...] = dv_scr[...].astype(BF16)

        @pl.when(j == nq - 1)
        def _():
            row_lane, row_halves = _head_halves(S)
            dq_ref[...] = (jnp.where(row_halves[0], dq_scr[0], dq_scr[1]) * scale).astype(BF16)
            dqx_ref[...] = jnp.where(row_halves[0], dq_scr[1], dq_scr[0])

    tile = lambda col: pl.BlockSpec((tq, LANES), lambda p, j: (j, col + p))
    whole = lambda col: pl.BlockSpec((S, LANES), lambda p, j: (0, col + p))
    return pl.pallas_call(
        body, name="attn_bwd", grid=(P, nq),
        in_specs=[whole(qc), tile(kc), tile(vc), whole(P), whole(0), tile(0), whole(0)],
        out_specs=[whole(0), tile(0), tile(0), whole(0), tile(0)],
        out_shape=[jax.ShapeDtypeStruct((S, W), BF16),
                   jax.ShapeDtypeStruct((S, W), BF16),
                   jax.ShapeDtypeStruct((S, W), BF16),
                   jax.ShapeDtypeStruct((S, W), F32),
                   jax.ShapeDtypeStruct((S, W), F32)],
        scratch_shapes=[pltpu.VMEM((2, tq, LANES), BF16),
                        pltpu.VMEM((2, tq, LANES), BF16),
                        pltpu.VMEM((2, tq, LANES), F32),
                        pltpu.VMEM((tq, LANES), F32),
                        pltpu.VMEM((2, S, LANES), F32)],
        compiler_params=_params("parallel", "arbitrary"),
    )(proj, proj, proj, da, qaug, kaug, doaug)


def _fgate_bwd(dqx, dkx, z, bias, n_heads):
    S = z.shape[0]
    tb = min(TB, S)
    nb = S // tb

    def body(dqx_ref, dkx_ref, z_ref, b_ref, dz_ref, db_ref):
        tri = (lax.broadcasted_iota(jnp.int32, (tb, tb), 1)
               >= lax.broadcasted_iota(jnp.int32, (tb, tb), 0)).astype(F32)
        n_cols = dqx_ref.shape[1]
        col = lax.broadcasted_iota(jnp.int32, (n_cols, LANES), 0)
        head = lax.broadcasted_iota(jnp.int32, (n_cols, LANES), 1)
        home = (head >> 1) * LANES + jnp.where((head & 1) == 0, HEAD_DIM, 0)
        is_head = head < n_heads
        pick_rows = jnp.logical_and(is_head, col == home + SLOT_C).astype(F32)
        pick_cols = jnp.logical_and(is_head, col == home + SLOT_ONE).astype(F32)

        local = []
        for b in range(nb):
            rows = slice(b * tb, (b + 1) * tb)
            dc = (jnp.dot(dqx_ref[rows, :], pick_rows, preferred_element_type=F32, precision=lax.Precision.HIGHEST)
                  - jnp.dot(dkx_ref[rows, :], pick_cols, preferred_element_type=F32, precision=lax.Precision.HIGHEST))
            local.append(jnp.dot(tri, dc, preferred_element_type=F32, precision=lax.Precision.HIGHEST))
        carry = jnp.zeros((1, LANES), F32)
        db = jnp.zeros((1, LANES), F32)
        for b in reversed(range(nb)):
            rows = slice(b * tb, (b + 1) * tb)
            rc = local[b] + carry
            carry = rc[0:1, :]
            dz = rc * _sigmoid(-(z_ref[rows, :] + b_ref[...]))
            dz_ref[rows, :] = dz.astype(BF16)
            db = db + jnp.sum(dz, axis=0, keepdims=True)
        db_ref[...] = db

    return pl.pallas_call(
        body, name="fgate_bwd",
        out_shape=[jax.ShapeDtypeStruct((S, LANES), BF16),
                   jax.ShapeDtypeStruct((1, LANES), F32)],
        compiler_params=pltpu.CompilerParams(vmem_limit_bytes=VMEM_LIMIT),
    )(dqx, dkx, z, bias)


def _pool_bwd(proj, da, pool_w, pool_scale):
    S = proj.shape[0]
    G = len(POOL_WINDOWS)

    def body(u_ref, dy_ref, w_ref, s_ref, du_ref, dw_ref, ds_ref, pad_ref):
        g = pl.program_id(0)
        for gi, w in enumerate(POOL_WINDOWS):
            @pl.when(g == gi)
            def _():
                d, cnt = _window_mean_minus_self(u_ref[...].astype(F32), pad_ref, w, S)
                db = d.astype(BF16)
                wb = w_ref[0].astype(BF16)
                yraw = jnp.dot(db, wb, preferred_element_type=F32)
                dy = dy_ref[...].astype(F32)
                ds_ref[...] = jnp.sum(dy * yraw, axis=0, keepdims=True)
                dzb = (dy * s_ref[...]).astype(BF16)
                dw_ref[0] = lax.dot_general(db, dzb, TN_DIMS, preferred_element_type=F32)
                dd = lax.dot_general(dzb, wb, NT, preferred_element_type=F32)
                pad_ref[0:S, :] = dd / cnt
                pad_ref[S:S + MAX_WINDOW, :] = jnp.zeros((MAX_WINDOW, LANES), F32)
                acc = -dd
                for j in range(w):
                    acc = acc + pad_ref[j:j + S, :]
                du_ref[...] = acc.astype(BF16)

    return pl.pallas_call(
        body, name="pool_bwd", grid=(G,),
        in_specs=[pl.BlockSpec((S, LANES), lambda g: (0, g)),
                  pl.BlockSpec((S, LANES), lambda g: (0, g)),
                  pl.BlockSpec((1, LANES, LANES), lambda g: (g, 0, 0)),
                  pl.BlockSpec((1, LANES), lambda g: (0, g))],
        out_specs=[pl.BlockSpec((S, LANES), lambda g: (0, g)),
                   pl.BlockSpec((1, LANES, LANES), lambda g: (g, 0, 0)),
                   pl.BlockSpec((1, LANES), lambda g: (0, g))],
        out_shape=[jax.ShapeDtypeStruct((S, G * LANES), BF16),
                   jax.ShapeDtypeStruct((G, LANES, LANES), F32),
                   jax.ShapeDtypeStruct((1, G * LANES), F32)],
        scratch_shapes=[pltpu.VMEM((S + MAX_WINDOW, LANES), F32)],
        compiler_params=_params("arbitrary"),
    )(proj, da, pool_w, pool_scale)


def _inproj_bwd_dx(sections, dzf, w, x, gam, g, after=()):
    S, D = x.shape
    N = w.shape[1] - LANES
    W = D // 2
    n_sec = len(sections)
    tm = min(TM // 2, S)

    def body(dz_ref, w_ref, wf_ref, x_ref, gam_ref, g_ref, *rest):
        sec_refs = rest[:n_sec]
        dx_ref, dg_ref = rest[-2:]

        @pl.when(pl.program_id(0) == 0)
        def _():
            dg_ref[...] = jnp.zeros(dg_ref.shape, F32)

        dh = lax.dot_general(dz_ref[...], wf_ref[...], NT, preferred_element_type=F32)
        for n, ref in enumerate(sec_refs):
            dh = dh + lax.dot_general(ref[...], w_ref[:, n * W:(n + 1) * W], NT, preferred_element_type=F32)
        xf = x_ref[...]
        r = lax.rsqrt(jnp.mean(xf * xf, axis=-1, keepdims=True) + RMS_EPS)
        xhat = xf * r
        dg_ref[...] += jnp.sum(dh * xhat, axis=0, keepdims=True)
        dxhat = dh * gam_ref[...]
        dx_ref[...] = g_ref[...] + r * (dxhat - xhat * jnp.mean(dxhat * xhat, axis=-1, keepdims=True))

    sec_specs, sec_arrays = _section_specs(sections, tm, W)
    return pl.pallas_call(
        body, name="inproj_bwd_dx", grid=(S // tm,),
        in_specs=[pl.BlockSpec((tm, LANES), lambda i: (i, 0)),
                  pl.BlockSpec((D, N), lambda i: (0, 0)),
                  pl.BlockSpec((D, LANES), lambda i: (0, N // LANES)),
                  pl.BlockSpec((tm, D), lambda i: (i, 0)),
                  pl.BlockSpec((1, D), lambda i: (0, 0)),
                  pl.BlockSpec((tm, D), lambda i: (i, 0))] + sec_specs + _after_specs(after),
        out_specs=[pl.BlockSpec((tm, D), lambda i: (i, 0)),
                   pl.BlockSpec((1, D), lambda i: (0, 0))],
        out_shape=[jax.ShapeDtypeStruct((S, D), F32),
                   jax.ShapeDtypeStruct((1, D), F32)],
        compiler_params=_params("arbitrary"),
    )(dzf, w, w, x, gam, g, *sec_arrays, *after)


def _adamw(w, m, v, gsets, name, rows, shifted=False, first=0, into=None):
    A, R, C = w.shape
    n_sets = len(gsets)
    tr = min(rows, R)
    c1 = 1.0 / (1.0 - ADAM_B1 ** ADAM_STEP)
    c2 = 1.0 / (1.0 - ADAM_B2 ** ADAM_STEP)
    counts = [len(gs) for gs in gsets]

    def body(w_ref, m_ref, v_ref, *rest):
        g_ref, d_ref, nm_ref, nv_ref = rest[-4:]
        at = 0
        for a in range(n_sets):
            part_refs = rest[at:at + counts[a]]
            at += counts[a]

            @pl.when(pl.program_id(0) == a)
            def _():
                g = None
                for ref in part_refs:
                    for s in range(ref.shape[0]):
                        term = ref[s].astype(F32)
                        g = term if g is None else g + term
                if shifted:
                    lanes = g.shape[1]
                    g = pltpu.roll(g, (lanes - _index(_position()) * (C % LANES)) % lanes, axis=1)[:, :C]
                nm = ADAM_B1 * m_ref[0] + (1.0 - ADAM_B1) * g
                nv = ADAM_B2 * v_ref[0] + (1.0 - ADAM_B2) * (g * g)
                g_ref[0] = g
                nm_ref[0] = nm
                nv_ref[0] = nv
                d_ref[0] = -ADAM_LR * ((nm * c1) / (jnp.sqrt(nv * c2) + ADAM_EPS) + ADAM_WD * w_ref[0])

    spec = pl.BlockSpec((1, tr, C), lambda a, r: (first + a, r, 0))
    part_specs = [pl.BlockSpec((part.shape[0], tr, part.shape[2]), lambda a, r, l=l: (0, jnp.where(a == l, r, 0), 0))
                  for l, gs in enumerate(gsets) for part in gs]
    parts = [part for gs in gsets for part in gs]
    shape = jax.ShapeDtypeStruct((A, R, C), F32)
    earlier = () if into is None else tuple(into)
    return pl.pallas_call(
        body, name=name, grid=(n_sets, R // tr),
        in_specs=[spec, spec, spec] + part_specs + _after_specs(earlier),
        out_specs=[spec, spec, spec, spec],
        out_shape=[shape, shape, shape, shape],
        input_output_aliases={3 + len(parts) + n: n for n in range(len(earlier))},
        compiler_params=_params("arbitrary", "arbitrary"),
    )(w, m, v, *parts, *earlier)


def _position():
    return lax.axis_index("x"), lax.axis_index("y"), lax.axis_index("c")


def _index(dev):
    return 4 * dev[0] + 2 * dev[1] + dev[2]


def _all_gather(arrs, slots, out_shapes, name):
    n_arr = len(arrs)

    def body(*refs):
        ins, outs = refs[:n_arr], refs[n_arr:2 * n_arr]
        send_sems, recv_sems, local_sems = refs[2 * n_arr:]
        x, y, c = _position()
        me, sibling = (x, y, c), (x, y, 1 - c)
        chips = [(1 - x, y), (x, 1 - y), (1 - x, 1 - y)]

        def copy(a, k, block, to, src=None):
            part = slots[a](outs[a], _index(block))
            return pltpu.make_async_remote_copy(
                src_ref=part if src is None else src, dst_ref=part,
                send_sem=send_sems.at[a, k], recv_sem=recv_sems.at[a, k],
                device_id=to, device_id_type=MESH)

        mine = [pltpu.make_async_copy(ins[a], slots[a](outs[a], _index(me)), local_sems.at[a])
                for a in range(n_arr)]
        for cp in mine:
            cp.start()
        first = []
        for a in range(n_arr):
            first.append(copy(a, 0, me, sibling, src=ins[a]))
            first += [copy(a, 1 + j, me, (*chip, c), src=ins[a]) for j, chip in enumerate(chips)]
        for cp in first:
            cp.start()
        passed = []
        for j, chip in enumerate(chips):
            for a in range(n_arr):
                copy(a, 1 + j, (*chip, c), me).wait_recv()
                fwd = copy(a, 4 + j, (*chip, c), sibling)
                fwd.start()
                passed.append(fwd)
        for a in range(n_arr):
            copy(a, 0, sibling, me).wait_recv()
            for j, chip in enumerate(chips):
                copy(a, 4 + j, (*chip, 1 - c), me).wait_recv()
        for cp in first + passed:
            cp.wait_send()
        for cp in mine:
            cp.wait()

    any_spec = pl.BlockSpec(memory_space=pl.ANY)
    return pl.pallas_call(
        body, name=name,
        in_specs=[any_spec] * n_arr, out_specs=[any_spec] * n_arr, out_shape=out_shapes,
        scratch_shapes=[pltpu.SemaphoreType.DMA((n_arr, 7)), pltpu.SemaphoreType.DMA((n_arr, 7)),
                        pltpu.SemaphoreType.DMA((n_arr,))],
    )(*arrs)


def _split_copies(srcs, lands, send_sems, recv_sems, kinds):
    x, y, c = _position()
    me = _index((x, y, c))
    copies = []
    for a, (src_part, land_part) in enumerate(kinds):
        for k in range(1, N_DEV):
            peer = (x ^ ((k >> 2) & 1), y ^ ((k >> 1) & 1), c ^ (k & 1))
            copies.append(pltpu.make_async_remote_copy(
                src_ref=src_part(srcs[a], _index(peer)), dst_ref=land_part(lands[a], me, k),
                send_sem=send_sems[a].at[k - 1], recv_sem=recv_sems[a].at[k - 1],
                device_id=peer, device_id_type=MESH))
    return copies


def _split_start(srcs, lands, kinds, name, after=()):
    n = len(srcs)

    def body(*refs):
        src_refs, land_refs = refs[:n], refs[n:2 * n]
        outs = refs[2 * n + len(after):]
        send_sems, recv_sems = outs[:n], outs[n:2 * n]
        token = outs[-1]
        for cp in _split_copies(src_refs, land_refs, send_sems, recv_sems, kinds):
            cp.start()
        token[...] = jnp.zeros(token.shape, token.dtype)

    hbm = pl.BlockSpec(memory_space=pltpu.HBM)
    sem = pl.BlockSpec(memory_space=pltpu.SEMAPHORE)
    operands = [pltpu.with_memory_space_constraint(t, pltpu.HBM) for t in (*srcs, *lands)]
    out = pl.pallas_call(
        body, name=name,
        in_specs=[hbm] * (2 * n) + _after_specs(after),
        out_specs=[sem] * (2 * n) + [hbm] * (2 * n) + [pl.BlockSpec(memory_space=pltpu.VMEM)],
        out_shape=[pltpu.SemaphoreType.DMA((N_DEV - 1,))] * (2 * n)
        + [pltpu.HBM(t.shape, t.dtype) for t in operands] + [jax.ShapeDtypeStruct((8, LANES), F32)],
        input_output_aliases={i: 2 * n + i for i in range(2 * n)},
        compiler_params=pltpu.CompilerParams(has_side_effects=pltpu.SideEffectType.DATAFLOW_SIDE_EFFECTING),
    )(*operands, *after)
    return [(out[a], out[n + a], out[2 * n + a], out[3 * n + a]) for a in range(n)], out[-1]


def _split_wait(started, kinds, after, name):
    n = len(started)
    sems = [t[0] for t in started] + [t[1] for t in started]
    srcs = [t[2] for t in started]
    lands = [t[3] for t in started]

    def body(*refs):
        src_refs, land_refs = refs[:n], refs[n:2 * n]
        send_sems, recv_sems = refs[2 * n:3 * n], refs[3 * n:4 * n]
        for cp in _split_copies(src_refs, land_refs, send_sems, recv_sems, kinds):
            cp.wait_send()
            cp.wait_recv()

    hbm = pl.BlockSpec(memory_space=pltpu.HBM)
    sem = pl.BlockSpec(memory_space=pltpu.SEMAPHORE)
    out = pl.pallas_call(
        body, name=name,
        in_specs=[hbm] * (2 * n) + [sem] * (2 * n) + _after_specs(after),
        out_specs=[hbm] * (2 * n),
        out_shape=[pltpu.HBM(t.shape, t.dtype) for t in (*srcs, *lands)],
        input_output_aliases={i: i for i in range(2 * n)},
        compiler_params=pltpu.CompilerParams(has_side_effects=pltpu.SideEffectType.DATAFLOW_SIDE_EFFECTING),
    )(*srcs, *lands, *sems, *after)
    return out[n:]


def _as_rows(p):
    if p.size % LANES == 0:
        rows = p.reshape(-1, LANES)
    else:
        rows = p.reshape(-1, p.shape[-1])
        rows = jnp.pad(rows, ((0, 0), (0, LANES - rows.shape[1])))
    return jnp.pad(rows, ((0, -rows.shape[0] % 8), (0, 0)))


def _pack(parts):
    return jnp.concatenate([_as_rows(p) for p in parts])[None]


def _unpack(packed, like):
    out, at = [], 0
    for p in like:
        whole = p.size % LANES == 0
        n = p.size // LANES if whole else p.size // p.shape[-1]
        rows = packed[0, at:at + n]
        out.append((rows if whole else rows[:, :p.shape[-1]]).reshape(p.shape))
        at += n + (-n % 8)
    return out


def _local_step(x, target, norm_g, forget_bias, pool_w, pool_scale, final_g, weights_in, weights_out, on_grads,
                first_after=()):
    L = norm_g.shape[0]
    S, D = x.shape
    W = D // 2
    H = W // HEAD_DIM
    bias = jnp.pad(forget_bias, ((0, 0), (0, LANES - H)))

    saved = []
    after = tuple(first_after)
    for l in range(L):
        proj, h, z, w = _inproj_fwd(x, norm_g[l:l + 1], weights_in(l, x), after)
        after = ()
        qaug, kaug = _fgate_fwd(z, bias[l:l + 1], H)
        ypool = _pool_fwd(proj, pool_w[l], pool_scale[l:l + 1])
        o, qaug_b = _attn_fwd(proj, qaug, kaug)
        wout = weights_out(l, o)
        x_new, mixed = _outproj_fwd(ypool, o, proj, x, wout)
        saved.append((x, proj, h, z, qaug_b, kaug, ypool, o, mixed, w, wout))
        x = x_new

    g, loss, d_final_g = _loss_head(x, final_g.reshape(1, D), target)

    small = None
    for l in reversed(range(L)):
        x_in, proj, h, z, qaug_b, kaug, ypool, o, mixed, w, wout = saved[l]
        d_wout, da, dgate, doaug = _outproj_bwd(g, wout, mixed, ypool, o, proj)
        dq, dk, dv, dqx, dkx = _attn_bwd(proj, da, qaug_b, kaug, doaug)
        dzf, db = _fgate_bwd(dqx, dkx, z, bias[l:l + 1], H)
        dpu, dpw, dps = _pool_bwd(proj, da, pool_w[l], pool_scale[l:l + 1])
        dproj = [(dpu, 0), (dgate, 0), (dq, 0), (dk, 0), (dv, 0), (dgate, 1)]
        d_w, d_w_bf16 = _inproj_bwd_dw(h, dproj, dzf)
        after = tuple(on_grads(l, d_w, d_w_bf16, d_wout, small))
        g, dgam = _inproj_bwd_dx(dproj, dzf, w, x_in, norm_g[l:l + 1], g, after)
        small = (dgam[0], db[0, :H], dpw, dps[0])
    return loss[0, 0], g, small, d_final_g[0]


def kernel(x, norm_g, w_in, forget_bias, pool_w, pool_scale, w_out, final_g, loss_target, m_norm_g, m_w_in, m_forget_bias, m_pool_w, m_pool_scale, m_w_out, m_final_g, v_norm_g, v_w_in, v_forget_bias, v_pool_w, v_pool_scale, v_w_out, v_final_g):
    L, D, cols = w_in.shape
    rows_out = w_out.shape[1]
    W = D // 2
    H = W // HEAD_DIM
    me = _index(_position())

    slot = _slot_width(cols)
    wout_b = w_out.astype(BF16)
    win_b = [lax.dynamic_update_slice(jnp.zeros((D, slot), BF16), w_in[l].astype(BF16),
                                      (0, me * (cols % LANES))) for l in range(L)]
    gather_in = (lambda ref, peer: ref, lambda ref, mine, k: ref.at[mine])
    gather_out = (lambda ref, peer: ref, lambda ref, mine, k: ref.at[pl.ds(mine * rows_out, rows_out), :])

    (first_in,) = _all_gather([win_b[0]], [lambda ref, n: ref.at[n]],
                              [jax.ShapeDtypeStruct((N_DEV, D, slot), BF16)], "gather_first")
    rest_srcs = [wout_b[0]] + [w[l] for l in range(1, L) for w in (win_b, wout_b)]
    rest_lands = [jnp.tile(wout_b[0], (N_DEV, 1))]
    for l in range(1, L):
        rest_lands += [jnp.broadcast_to(win_b[l][None], (N_DEV, D, slot)), jnp.tile(wout_b[l], (N_DEV, 1))]
    rest_kinds = [gather_out] + [gather_in, gather_out] * (L - 1)
    rest, rest_token = _split_start(rest_srcs, rest_lands, rest_kinds, "gather_start_rest", (first_in,))

    def weights_in(l, x_in):
        if l == 0:
            return first_in
        (win_all,) = _split_wait([rest[2 * l - 1]], [gather_in], (x_in,), f"gather_wait_in_{l}")
        return win_all

    def weights_out(l, o):
        (wout_full,) = _split_wait([rest[2 * l]], [gather_out], (o,), f"gather_wait_out_{l}")
        return wout_full

    stride = slot - LANES
    exchange_kinds = [(lambda ref, peer: ref.at[:, pl.ds(pl.multiple_of(peer * stride, LANES), slot)],
                       lambda ref, mine, k: ref.at[k - 1]),
                      (lambda ref, peer: ref.at[pl.ds(peer * rows_out, rows_out), :],
                       lambda ref, mine, k: ref.at[k - 1])]
    zero_g = jnp.zeros_like(final_g)

    def small_pack(l, norm_g_l, bias_l, pool_w_l, pool_scale_l, final):
        return _pack([norm_g_l, bias_l, pool_w_l, pool_scale_l, final if l == 0 else zero_g])[0]

    exchanges, own_parts = {}, {}

    def on_grads(l, dw, dw_bf16, d_wout, small):
        own_parts[l] = (lax.dynamic_slice_in_dim(dw, me * stride, slot, 1)[None],
                        lax.dynamic_slice_in_dim(d_wout, me * rows_out, rows_out, 0)[None])
        srcs = [dw_bf16, d_wout.astype(BF16)]
        lands = [lax.empty((N_DEV - 1, D, slot), BF16), lax.empty((N_DEV - 1, rows_out, D), BF16)]
        kinds = list(exchange_kinds)
        if small is not None:
            packed_small = small_pack(l + 1, *small, None)
            srcs.append(packed_small)
            lands.append(jnp.broadcast_to(packed_small[None], (N_DEV, *packed_small.shape)))
            kinds.append(gather_in)
        started, token = _split_start(srcs, lands, kinds, f"exchange_start_{l}")
        exchanges[l] = (started, kinds)
        return (token,)

    loss, dx, small_first, d_final_g = _local_step(
        x[0], loss_target[0], norm_g, forget_bias, pool_w, pool_scale, final_g,
        weights_in, weights_out, on_grads, (rest_token,))
    loss = lax.psum(loss, ("x", "y", "c"))
    packed_first = small_pack(0, *small_first, d_final_g)
    first_started, first_token = _split_start(
        [packed_first], [jnp.broadcast_to(packed_first[None], (N_DEV, *packed_first.shape))], [gather_in],
        "small_start", (w_in, m_w_in, v_w_in, *own_parts[0]))

    gin_sets, gout_sets, small_sets = [None] * L, [None] * L, [None] * L

    def wait_for(l, after):
        started, kinds = exchanges[l]
        got = _split_wait(started, kinds, after, f"exchange_wait_{l}")
        gin_sets[l] = [own_parts[l][0], got[0]]
        gout_sets[l] = [own_parts[l][1], got[1]]
        if len(got) > 2:
            small_sets[l + 1] = [got[2]]

    for l in range(1, L):
        wait_for(l, (dx, first_token))
    rest_in = _adamw(w_in, m_w_in, v_w_in, gin_sets[1:], "adamw_w_in_rest", 256, shifted=True, first=1)
    rest_out = _adamw(w_out, m_w_out, v_w_out, gout_sets[1:], "adamw_w_out_rest", 128, first=1)
    wait_for(0, (rest_in[1], rest_out[1]))
    g_w_in, d_w_in, nm_w_in, nv_w_in = _adamw(w_in, m_w_in, v_w_in, gin_sets[:1], "adamw_w_in_first", 256,
                                              shifted=True, into=rest_in)
    g_w_out, d_w_out, nm_w_out, nv_w_out = _adamw(w_out, m_w_out, v_w_out, gout_sets[:1], "adamw_w_out_first", 128,
                                                  into=rest_out)
    small_sets[0] = _split_wait(first_started, [gather_in], (d_w_out,), "small_wait")

    def small_stack(norm_g_, bias_, pool_w_, pool_scale_, final):
        return jnp.stack([small_pack(l, norm_g_[l], bias_[l], pool_w_[l], pool_scale_[l], final) for l in range(L)])

    packed = _adamw(small_stack(norm_g, forget_bias, pool_w, pool_scale, final_g),
                    small_stack(m_norm_g, m_forget_bias, m_pool_w, m_pool_scale, m_final_g),
                    small_stack(v_norm_g, v_forget_bias, v_pool_w, v_pool_scale, v_final_g),
                    small_sets, "adamw_small", packed_first.shape[0])

    def small_unpack(p):
        like = [norm_g[0], forget_bias[0], pool_w[0], pool_scale[0], final_g]
        layers = [_unpack(p[l:l + 1], like) for l in range(L)]
        return [jnp.stack([layers[l][n] for l in range(L)]) for n in range(4)] + [layers[0][4]]

    g_s, d_s, nm_s, nv_s = [small_unpack(p) for p in packed]

    def order(big_in, big_out, small):
        return (small[0], big_in, small[1], small[2], small[3], big_out, small[4])

    return (loss, dx[None], *order(g_w_in, g_w_out, g_s), *order(d_w_in, d_w_out, d_s),
            *order(nm_w_in, nm_w_out, nm_s), *order(nv_w_in, nv_w_out, nv_s))
```

```python
import math

import jax
import jax.numpy as jnp
from jax import lax
from jax.experimental import pallas as pl
from jax.experimental.pallas import tpu as pltpu

F32 = jnp.float32
BF16 = jnp.bfloat16
MESH = pl.DeviceIdType.MESH

RMS_EPS = 1e-6
NEG_INF = -1e30
HEAD_DIM = 64
POOL_WINDOWS = (2, 4, 8, 16)
MAX_WINDOW = 16
LANES = 128
N_DEV = 8

ADAM_LR = 0.001
ADAM_B1 = 0.9
ADAM_B2 = 0.999
ADAM_EPS = 1e-08
ADAM_WD = 0.01
ADAM_STEP = 10

TM = 512
TN = 512
TQ = 512
TB = 256
VMEM_LIMIT = 56 * 1024 * 1024

NT = (((1,), (1,)), ((), ()))
TN_DIMS = (((0,), (0,)), ((), ()))

SLOT_C, SLOT_ONE, SLOT_LSE = 0, 3, 6


def _params(*sem):
    return pltpu.CompilerParams(dimension_semantics=sem, vmem_limit_bytes=VMEM_LIMIT)


def _sigmoid(x):
    return 1.0 / (1.0 + jnp.exp(-x))


def _split3(x):
    hi = x.astype(BF16).astype(F32)
    rest = x - hi
    mid = rest.astype(BF16).astype(F32)
    return hi, mid, rest - mid


def _after_specs(after):
    return [pl.BlockSpec(memory_space=pl.ANY)] * len(after)


def _slot_width(cols):
    return LANES * (-(-(cols + (N_DEV - 1) * (cols % LANES)) // LANES))


def _inproj_fwd(x, gam, slots, after=()):
    S, D = x.shape
    n_dev, _, sw = slots.shape
    stride = sw - LANES
    width = stride * n_dev + LANES
    N = width - LANES
    tm, tn = min(TM, S), TN

    def body(x_ref, g_ref, s_ref, *rest):
        proj_ref, h_ref, z_ref, w_ref = rest[-4:]

        @pl.when(pl.program_id(0) == 0)
        def _():
            for n in range(n_dev):
                base = stride * n
                first = s_ref[n, :, 0:LANES]
                if n > 0:
                    first = first + s_ref[n - 1, :, stride:sw]
                w_ref[:, base:base + LANES] = first
                w_ref[:, base + LANES:base + stride] = s_ref[n, :, LANES:stride]
            w_ref[:, stride * n_dev:width] = s_ref[n_dev - 1, :, stride:sw]

        xf = x_ref[...]
        r = lax.rsqrt(jnp.mean(xf * xf, axis=-1, keepdims=True) + RMS_EPS)
        h = ((xf * r) * g_ref[...]).astype(BF16)
        h_ref[...] = h
        z_ref[...] = jnp.dot(h, w_ref[:, N:width], preferred_element_type=F32)
        for n in range(N // tn):
            cols = slice(n * tn, (n + 1) * tn)
            proj_ref[:, cols] = jnp.dot(h, w_ref[:, cols], preferred_element_type=F32).astype(BF16)

    return pl.pallas_call(
        body, name="inproj_fwd", grid=(S // tm,),
        in_specs=[pl.BlockSpec((tm, D), lambda i: (i, 0)),
                  pl.BlockSpec((1, D), lambda i: (0, 0)),
                  pl.BlockSpec((n_dev, D, sw), lambda i: (0, 0, 0))] + _after_specs(after),
        out_specs=[pl.BlockSpec((tm, N), lambda i: (i, 0)),
                   pl.BlockSpec((tm, D), lambda i: (i, 0)),
                   pl.BlockSpec((tm, LANES), lambda i: (i, 0)),
                   pl.BlockSpec((D, width), lambda i: (0, 0))],
        out_shape=[jax.ShapeDtypeStruct((S, N), BF16),
                   jax.ShapeDtypeStruct((S, D), BF16),
                   jax.ShapeDtypeStruct((S, LANES), F32),
                   jax.ShapeDtypeStruct((D, width), BF16)],
        compiler_params=_params("arbitrary"),
    )(x, gam, slots, *after)


def _fgate_fwd(z, bias, n_heads):
    S = z.shape[0]
    tb = min(TB, S)
    P = n_heads // 2

    def body(z_ref, b_ref, qaug_ref, kaug_ref):
        lane = lax.broadcasted_iota(jnp.int32, (tb, LANES), 1)
        tri = (lax.broadcasted_iota(jnp.int32, (tb, tb), 0)
               >= lax.broadcasted_iota(jnp.int32, (tb, tb), 1)).astype(F32)
        head = lax.broadcasted_iota(jnp.int32, (LANES, P * LANES), 0)
        col = lax.broadcasted_iota(jnp.int32, (LANES, P * LANES), 1)
        home = (head >> 1) * LANES + jnp.where((head & 1) == 0, HEAD_DIM, 0)
        is_head = head < n_heads
        place_q = [jnp.logical_and(is_head, col == home + SLOT_C + n).astype(BF16) for n in range(3)]
        place_k = [jnp.logical_and(is_head, col == home + SLOT_ONE + n).astype(BF16) for n in range(3)]
        slot = lax.broadcasted_iota(jnp.int32, (tb, P * LANES), 1) & (HEAD_DIM - 1)
        q_ones = jnp.logical_and(slot >= SLOT_ONE, slot < SLOT_ONE + 3).astype(F32)
        k_ones = jnp.logical_or(slot < SLOT_C + 3,
                                jnp.logical_and(slot >= SLOT_LSE, slot < SLOT_LSE + 3)).astype(F32)

        local = []
        for b in range(S // tb):
            zz = z_ref[b * tb:(b + 1) * tb, :] + b_ref[...]
            lf = jnp.minimum(zz, 0.0) - jnp.log(1.0 + jnp.exp(-jnp.abs(zz)))
            lf = jnp.where(lane < n_heads, lf, 0.0)
            local.append(jnp.dot(tri, lf, preferred_element_type=F32, precision=lax.Precision.HIGHEST))
        carry = jnp.zeros((1, LANES), F32)
        for b, part_sum in enumerate(local):
            c = part_sum + carry
            carry = c[tb - 1:tb, :]
            qa, ka = q_ones, k_ones
            for n, part in enumerate(_split3(c)):
                qa = qa + jnp.dot(part.astype(BF16), place_q[n], preferred_element_type=F32)
                ka = ka - jnp.dot(part.astype(BF16), place_k[n], preferred_element_type=F32)
            qaug_ref[b * tb:(b + 1) * tb, :] = qa.astype(BF16)
            kaug_ref[b * tb:(b + 1) * tb, :] = ka.astype(BF16)

    return pl.pallas_call(
        body, name="fgate_fwd",
        out_shape=[jax.ShapeDtypeStruct((S, P * LANES), BF16),
                   jax.ShapeDtypeStruct((S, P * LANES), BF16)],
        compiler_params=pltpu.CompilerParams(vmem_limit_bytes=VMEM_LIMIT),
    )(z, bias)


def _window_mean_minus_self(u, pad_ref, w, S):
    pad_ref[0:MAX_WINDOW, :] = jnp.zeros((MAX_WINDOW, LANES), F32)
    pad_ref[MAX_WINDOW:MAX_WINDOW + S, :] = u
    acc = u
    for j in range(1, w):
        acc = acc + pad_ref[MAX_WINDOW - j:MAX_WINDOW - j + S, :]
    t = lax.broadcasted_iota(jnp.int32, (S, LANES), 0)
    cnt = jnp.minimum(t + 1, w).astype(F32)
    return acc / cnt - u, cnt


def _pool_fwd(proj, pool_w, pool_scale):
    S = proj.shape[0]
    G = len(POOL_WINDOWS)

    def body(u_ref, w_ref, s_ref, y_ref, pad_ref):
        g = pl.program_id(0)
        for gi, w in enumerate(POOL_WINDOWS):
            @pl.when(g == gi)
            def _():
                d, _ = _window_mean_minus_self(u_ref[...].astype(F32), pad_ref, w, S)
                y = jnp.dot(d.astype(BF16), w_ref[0].astype(BF16), preferred_element_type=F32)
                y_ref[...] = (y * s_ref[...]).astype(BF16)

    return pl.pallas_call(
        body, name="pool_fwd", grid=(G,),
        in_specs=[pl.BlockSpec((S, LANES), lambda g: (0, g)),
                  pl.BlockSpec((1, LANES, LANES), lambda g: (g, 0, 0)),
                  pl.BlockSpec((1, LANES), lambda g: (0, g))],
        out_specs=pl.BlockSpec((S, LANES), lambda g: (0, g)),
        out_shape=jax.ShapeDtypeStruct((S, G * LANES), BF16),
        scratch_shapes=[pltpu.VMEM((S + MAX_WINDOW, LANES), F32)],
        compiler_params=_params("arbitrary"),
    )(proj, pool_w, pool_scale)


def _head_halves(rows):
    lane = lax.broadcasted_iota(jnp.int32, (rows, LANES), 1)
    return lane, (lane < HEAD_DIM, lane >= HEAD_DIM)


def _attn_fwd(proj, qaug, kaug):
    S = proj.shape[0]
    W = proj.shape[1] // 6
    P = W // LANES
    tq = min(TQ, S)
    nq = S // tq
    qc, kc, vc = 2 * P, 3 * P, 4 * P
    scale = 1.0 / math.sqrt(HEAD_DIM)

    def body(q_ref, k_ref, v_ref, qa_ref, ka_ref, o_ref, qb_ref, qm_scr, m_scr, acc_scr):
        i = pl.program_id(1)
        lane, halves = _head_halves(tq)
        v_ones = ((lane & (HEAD_DIM - 1)) < 3).astype(BF16)
        qs = q_ref[...] * scale
        qm_scr[0] = jnp.where(halves[0], qs, qa_ref[...])
        qm_scr[1] = jnp.where(halves[1], qs, qa_ref[...])
        m_scr[...] = jnp.full(m_scr.shape, NEG_INF, F32)
        acc_scr[...] = jnp.zeros(acc_scr.shape, F32)

        def update(j, on_diagonal):
            keys = pl.ds(pl.multiple_of(j * tq, tq), tq)
            k2, v2, kaug_t = k_ref[keys, :], v_ref[keys, :], ka_ref[keys, :]
            if on_diagonal:
                keep = (lax.broadcasted_iota(jnp.int32, (tq, tq), 0)
                        >= lax.broadcasted_iota(jnp.int32, (tq, tq), 1))
            logits = [lax.dot_general(qm_scr[a], jnp.where(halves[a], k2, kaug_t), NT, preferred_element_type=F32)
                      for a in range(2)]
            for a in range(2):
                s = jnp.where(keep, logits[a], NEG_INF) if on_diagonal else logits[a]
                va = jnp.where(halves[a], v2, v_ones)
                m_prev = m_scr[a]
                m_new = jnp.maximum(m_prev, jnp.max(s, axis=1, keepdims=True))
                p = jnp.exp(s - jnp.tile(m_new, (1, tq // LANES)))
                acc_scr[a] = jnp.exp(m_prev - m_new) * acc_scr[a] + jnp.dot(p.astype(BF16), va,
                                                                              preferred_element_type=F32)
                m_scr[a] = m_new

        def below_diagonal(jj, carry):
            update(2 * jj, False)
            update(2 * jj + 1, False)
            return carry

        lax.fori_loop(0, i // 2, below_diagonal, 0)

        @pl.when(i % 2 == 1)
        def _():
            update(i - 1, False)

        update(i, True)
        acc_a, acc_b = acc_scr[0], acc_scr[1]
        l_a, l_b = acc_a[:, HEAD_DIM:HEAD_DIM + 1], acc_b[:, 0:1]
        o_ref[...] = jnp.where(halves[0], acc_a / l_a, acc_b / l_b).astype(BF16)
        lse = jnp.where(halves[0], m_scr[1] + jnp.log(l_b), m_scr[0] + jnp.log(l_a))
        slot = lane & (HEAD_DIM - 1)
        aug = qa_ref[...].astype(F32)
        for n, part in enumerate(_split3(lse)):
            aug = jnp.where(slot == SLOT_LSE + n, -part, aug)
        qb_ref[...] = aug.astype(BF16)

    tile = lambda col: pl.BlockSpec((tq, LANES), lambda p, i: (i, col + p))
    whole = lambda col: pl.BlockSpec((S, LANES), lambda p, i: (0, col + p))
    return pl.pallas_call(
        body, name="attn_fwd", grid=(P, nq),
        in_specs=[tile(qc), whole(kc), whole(vc), tile(0), whole(0)],
        out_specs=[tile(0), tile(0)],
        out_shape=[jax.ShapeDtypeStruct((S, W), BF16), jax.ShapeDtypeStruct((S, W), BF16)],
        scratch_shapes=[pltpu.VMEM((2, tq, LANES), BF16),
                        pltpu.VMEM((2, tq, LANES), F32),
                        pltpu.VMEM((2, tq, LANES), F32)],
        compiler_params=_params("parallel", "arbitrary"),
    )(proj, proj, proj, qaug, kaug)


def _outproj_fwd(ypool, o, proj, x, wout):
    S, D = x.shape
    W = D // 2
    tm, tn = min(TM, S), TN

    def body(y_ref, o_ref, pg_ref, ag_ref, x_ref, w_ref, xn_ref, mix_ref):
        pg, ag = pg_ref[...].astype(F32), ag_ref[...].astype(F32)
        mix_ref[:, 0:W] = (y_ref[...].astype(F32) * (pg * _sigmoid(pg))).astype(BF16)
        mix_ref[:, W:D] = (o_ref[...].astype(F32) * (ag * _sigmoid(ag))).astype(BF16)
        for n in range(D // tn):
            cols = slice(n * tn, (n + 1) * tn)
            xn_ref[:, cols] = x_ref[:, cols] + jnp.dot(mix_ref[...], w_ref[:, cols], preferred_element_type=F32)

    return pl.pallas_call(
        body, name="outproj_fwd", grid=(S // tm,),
        in_specs=[pl.BlockSpec((tm, W), lambda i: (i, 0)),
                  pl.BlockSpec((tm, W), lambda i: (i, 0)),
                  pl.BlockSpec((tm, W), lambda i: (i, 1)),
                  pl.BlockSpec((tm, W), lambda i: (i, 5)),
                  pl.BlockSpec((tm, D), lambda i: (i, 0)),
                  pl.BlockSpec((D, D), lambda i: (0, 0))],
        out_specs=[pl.BlockSpec((tm, D), lambda i: (i, 0)),
                   pl.BlockSpec((tm, D), lambda i: (i, 0))],
        out_shape=[jax.ShapeDtypeStruct((S, D), F32),
                   jax.ShapeDtypeStruct((S, D), BF16)],
        compiler_params=_params("parallel"),
    )(ypool, o, proj, proj, x, wout)


def _loss_head(x, gam, target):
    S, D = x.shape
    tm = min(TM, S)

    def body(x_ref, g_ref, t_ref, dx_ref, loss_ref, dg_ref):
        @pl.when(pl.program_id(0) == 0)
        def _():
            loss_ref[...] = jnp.zeros(loss_ref.shape, F32)
            dg_ref[...] = jnp.zeros(dg_ref.shape, F32)

        xf, gam_v = x_ref[...], g_ref[...]
        r = lax.rsqrt(jnp.mean(xf * xf, axis=-1, keepdims=True) + RMS_EPS)
        xhat = xf * r
        err = xhat * gam_v - t_ref[...]
        part = jnp.sum(jnp.sum(err * err, axis=-1, keepdims=True), axis=0, keepdims=True)
        loss_ref[...] += part * (0.5 / D)
        dy = err * (1.0 / D)
        dg_ref[...] += jnp.sum(dy * xhat, axis=0, keepdims=True)
        dxhat = dy * gam_v
        dx_ref[...] = r * (dxhat - xhat * jnp.mean(dxhat * xhat, axis=-1, keepdims=True))

    return pl.pallas_call(
        body, name="loss_head", grid=(S // tm,),
        in_specs=[pl.BlockSpec((tm, D), lambda i: (i, 0)),
                  pl.BlockSpec((1, D), lambda i: (0, 0)),
                  pl.BlockSpec((tm, D), lambda i: (i, 0))],
        out_specs=[pl.BlockSpec((tm, D), lambda i: (i, 0)),
                   pl.BlockSpec((8, LANES), lambda i: (0, 0)),
                   pl.BlockSpec((1, D), lambda i: (0, 0))],
        out_shape=[jax.ShapeDtypeStruct((S, D), F32),
                   jax.ShapeDtypeStruct((8, LANES), F32),
                   jax.ShapeDtypeStruct((1, D), F32)],
        compiler_params=_params("arbitrary"),
    )(x, gam, target)


def _outproj_bwd(g, wout, mixed, ypool, o, proj):
    S, D = g.shape
    W = D // 2
    tm = min(TM, S)

    def body(g_ref, w_ref, mix_ref, y_ref, o_ref, pg_ref, ag_ref, dw_ref, da_ref, dgate_ref, doaug_ref):
        @pl.when(pl.program_id(0) == 0)
        def _():
            dw_ref[...] = jnp.zeros(dw_ref.shape, F32)

        gb = g_ref[...].astype(BF16)
        dw_ref[...] += lax.dot_general(mix_ref[...], gb, TN_DIMS, preferred_element_type=F32)
        for half, (val_ref, gate_ref) in enumerate(((y_ref, pg_ref), (o_ref, ag_ref))):
            cols = slice(half * W, (half + 1) * W)
            dmix = lax.dot_general(gb, w_ref[cols, :], NT, preferred_element_type=F32)
            gt = gate_ref[...].astype(F32)
            sg = _sigmoid(gt)
            da_ref[:, cols] = (dmix * (gt * sg)).astype(BF16)
            dgate_ref[:, cols] = (dmix * val_ref[...].astype(F32) * (sg * (1.0 + gt * (1.0 - sg)))).astype(BF16)

        lane, halves = _head_halves(tm)
        slot = lane & (HEAD_DIM - 1)
        for p in range(W // LANES):
            cols = slice(p * LANES, (p + 1) * LANES)
            prod = da_ref[:, W + p * LANES:W + (p + 1) * LANES].astype(F32) * o_ref[:, cols].astype(F32)
            d_a = jnp.sum(jnp.where(halves[0], prod, 0.0), axis=1, keepdims=True)
            d_b = jnp.sum(jnp.where(halves[1], prod, 0.0), axis=1, keepdims=True)
            aug = jnp.zeros((tm, LANES), F32)
            for n, part in enumerate(_split3(jnp.where(halves[0], d_b, d_a))):
                aug = jnp.where(slot == SLOT_C + n, -part, aug)
            doaug_ref[:, cols] = aug.astype(BF16)

    rows = lambda width, col: pl.BlockSpec((tm, width), lambda i: (i, col))
    return pl.pallas_call(
        body, name="outproj_bwd", grid=(S // tm,),
        in_specs=[rows(D, 0), pl.BlockSpec((D, D), lambda i: (0, 0)), rows(D, 0), rows(W, 0), rows(W, 0),
                  rows(W, 1), rows(W, 5)],
        out_specs=[pl.BlockSpec((D, D), lambda i: (0, 0)), rows(D, 0), rows(D, 0), rows(W, 0)],
        out_shape=[jax.ShapeDtypeStruct((D, D), F32),
                   jax.ShapeDtypeStruct((S, D), BF16),
                   jax.ShapeDtypeStruct((S, D), BF16),
                   jax.ShapeDtypeStruct((S, W), BF16)],
        compiler_params=_params("arbitrary"),
    )(g, wout, mixed, ypool, o, proj, proj)


def _section_specs(sections, rows, width):
    specs = [pl.BlockSpec((rows, width), lambda k, c=c: (k, c)) for _, c in sections]
    return specs, [a for a, _ in sections]


def _inproj_bwd_dw(h, sections, dzf):
    S, D = h.shape
    W = D // 2
    n_sec = len(sections)
    N = n_sec * W
    ts = min(TM, S)
    n_steps = S // ts

    def body(h_ref, dz_ref, *rest):
        sec_refs, (dw_ref, dwb_ref) = rest[:n_sec], rest[n_sec:]

        @pl.when(pl.program_id(0) == 0)
        def _():
            dw_ref[...] = jnp.zeros(dw_ref.shape, F32)

        ht = h_ref[...].T
        dw_ref[:, N:N + LANES] += jnp.dot(ht, dz_ref[...], preferred_element_type=F32)
        for n, ref in enumerate(sec_refs):
            dw_ref[:, n * W:(n + 1) * W] += jnp.dot(ht, ref[...], preferred_element_type=F32)

        @pl.when(pl.program_id(0) == n_steps - 1)
        def _():
            dwb_ref[...] = dw_ref[...].astype(BF16)

    sec_specs, sec_arrays = _section_specs(sections, ts, W)
    whole = pl.BlockSpec((D, N + LANES), lambda k: (0, 0))
    return pl.pallas_call(
        body, name="inproj_bwd_dw", grid=(n_steps,),
        in_specs=[pl.BlockSpec((ts, D), lambda k: (k, 0)),
                  pl.BlockSpec((ts, LANES), lambda k: (k, 0))] + sec_specs,
        out_specs=[whole, whole],
        out_shape=[jax.ShapeDtypeStruct((D, N + LANES), F32),
                   jax.ShapeDtypeStruct((D, N + LANES), BF16)],
        compiler_params=_params("arbitrary"),
    )(h, dzf, *sec_arrays)


def _attn_bwd(proj, da, qaug, kaug, doaug):
    S = proj.shape[0]
    W = proj.shape[1] // 6
    P = W // LANES
    tq = min(TQ, S)
    nq = S // tq
    qc, kc, vc = 2 * P, 3 * P, 4 * P
    scale = 1.0 / math.sqrt(HEAD_DIM)

    def body(q_ref, k_ref, v_ref, do_ref, qa_ref, ka_ref, da_ref,
             dq_ref, dk_ref, dv_ref, dqx_ref, dkx_ref, km_scr, vm_scr, dk_scr, dv_scr, dq_scr):
        j = pl.program_id(1)
        lane, halves = _head_halves(tq)

        @pl.when(j == 0)
        def _():
            dq_scr[...] = jnp.zeros(dq_scr.shape, F32)

        v_ones = ((lane & (HEAD_DIM - 1)) < 3).astype(BF16)
        for a in range(2):
            km_scr[a] = jnp.where(halves[a], k_ref[...], ka_ref[...])
            vm_scr[a] = jnp.where(halves[a], v_ref[...], v_ones)
        dk_scr[...] = jnp.zeros(dk_scr.shape, F32)
        dv_scr[...] = jnp.zeros(dv_scr.shape, F32)

        def update(i, on_diagonal):
            rows = pl.ds(pl.multiple_of(i * tq, tq), tq)
            qs = q_ref[rows, :] * scale
            do2, qaug_t, doaug_t = do_ref[rows, :], qa_ref[rows, :], da_ref[rows, :]
            if on_diagonal:
                keep = (lax.broadcasted_iota(jnp.int32, (tq, tq), 0)
                        >= lax.broadcasted_iota(jnp.int32, (tq, tq), 1))
            qas = [jnp.where(halves[a], qs, qaug_t) for a in range(2)]
            logits = [lax.dot_general(qas[a], km_scr[a], NT, preferred_element_type=F32) for a in range(2)]
            dps = [lax.dot_general(jnp.where(halves[a], do2, doaug_t), vm_scr[a], NT, preferred_element_type=F32)
                   for a in range(2)]
            dv = None
            for a in range(2):
                s = jnp.where(keep, logits[a], NEG_INF) if on_diagonal else logits[a]
                p = jnp.exp(s)
                dsb = (p * dps[a]).astype(BF16)
                do0 = jnp.where(halves[a], do2, jnp.zeros_like(do2))
                dv_a = lax.dot_general(p.astype(BF16), do0, TN_DIMS, preferred_element_type=F32)
                dv = dv_a if dv is None else dv + dv_a
                dk_scr[a] += lax.dot_general(dsb, qas[a], TN_DIMS, preferred_element_type=F32)
                dq_scr[a, rows, :] += jnp.dot(dsb, km_scr[a], preferred_element_type=F32)
            dv_scr[...] += dv

        def below_diagonal(n, carry):
            update(j + 1 + 2 * n, False)
            update(j + 2 + 2 * n, False)
            return carry

        update(j, True)
        below = nq - 1 - j
        lax.fori_loop(0, below // 2, below_diagonal, 0)

        @pl.when(below % 2 == 1)
        def _():
            update(nq - 1, False)


        dk_ref[...] = jnp.where(halves[0], dk_scr[0], dk_scr[1]).astype(BF16)
        dkx_ref[...] = jnp.where(halves[0], dk_scr[1], dk_scr[0])
        dv_ref[...] = dv_scr[...].astype(BF16)

        @pl.when(j == nq - 1)
        def _():
            row_lane, row_halves = _head_halves(S)
            dq_ref[...] = (jnp.where(row_halves[0], dq_scr[0], dq_scr[1]) * scale).astype(BF16)
            dqx_ref[...] = jnp.where(row_halves[0], dq_scr[1], dq_scr[0])

    tile = lambda col: pl.BlockSpec((tq, LANES), lambda p, j: (j, col + p))
    whole = lambda col: pl.BlockSpec((S, LANES), lambda p, j: (0, col + p))
    return pl.pallas_call(
        body, name="attn_bwd", grid=(P, nq),
        in_specs=[whole(qc), tile(kc), tile(vc), whole(P), whole(0), tile(0), whole(0)],
        out_specs=[whole(0), tile(0), tile(0), whole(0), tile(0)],
        out_shape=[jax.ShapeDtypeStruct((S, W), BF16),
                   jax.ShapeDtypeStruct((S, W), BF16),
                   jax.ShapeDtypeStruct((S, W), BF16),
                   jax.ShapeDtypeStruct((S, W), F32),
                   jax.ShapeDtypeStruct((S, W), F32)],
        scratch_shapes=[pltpu.VMEM((2, tq, LANES), BF16),
                        pltpu.VMEM((2, tq, LANES), BF16),
                        pltpu.VMEM((2, tq, LANES), F32),
                        pltpu.VMEM((tq, LANES), F32),
                        pltpu.VMEM((2, S, LANES), F32)],
        compiler_params=_params("parallel", "arbitrary"),
    )(proj, proj, proj, da, qaug, kaug, doaug)


def _fgate_bwd(dqx, dkx, z, bias, n_heads):
    S = z.shape[0]
    tb = min(TB, S)
    nb = S // tb

    def body(dqx_ref, dkx_ref, z_ref, b_ref, dz_ref, db_ref):
        tri = (lax.broadcasted_iota(jnp.int32, (tb, tb), 1)
               >= lax.broadcasted_iota(jnp.int32, (tb, tb), 0)).astype(F32)
        lane = lax.broadcasted_iota(jnp.int32, (tb, LANES), 1)

        local = []
        for b in range(nb):
            rows = slice(b * tb, (b + 1) * tb)
            dc = jnp.zeros((tb, LANES), F32)
            for p in range(n_heads // 2):
                row_sums = dqx_ref[rows, p * LANES:(p + 1) * LANES]
                col_sums = dkx_ref[rows, p * LANES:(p + 1) * LANES]
                for h, at in ((2 * p, HEAD_DIM), (2 * p + 1, 0)):
                    one = row_sums[:, at + SLOT_C:at + SLOT_C + 1] - col_sums[:, at + SLOT_ONE:at + SLOT_ONE + 1]
                    dc = jnp.where(lane == h, jnp.broadcast_to(one, (tb, LANES)), dc)
            local.append(jnp.dot(tri, dc, preferred_element_type=F32, precision=lax.Precision.HIGHEST))
        carry = jnp.zeros((1, LANES), F32)
        db = jnp.zeros((1, LANES), F32)
        for b in reversed(range(nb)):
            rows = slice(b * tb, (b + 1) * tb)
            rc = local[b] + carry
            carry = rc[0:1, :]
            dz = rc * _sigmoid(-(z_ref[rows, :] + b_ref[...]))
            dz_ref[rows, :] = dz.astype(BF16)
            db = db + jnp.sum(dz, axis=0, keepdims=True)
        db_ref[...] = db

    return pl.pallas_call(
        body, name="fgate_bwd",
        out_shape=[jax.ShapeDtypeStruct((S, LANES), BF16),
                   jax.ShapeDtypeStruct((1, LANES), F32)],
        compiler_params=pltpu.CompilerParams(vmem_limit_bytes=VMEM_LIMIT),
    )(dqx, dkx, z, bias)


def _pool_bwd(proj, da, pool_w, pool_scale):
    S = proj.shape[0]
    G = len(POOL_WINDOWS)

    def body(u_ref, dy_ref, w_ref, s_ref, du_ref, dw_ref, ds_ref, pad_ref):
        g = pl.program_id(0)
        for gi, w in enumerate(POOL_WINDOWS):
            @pl.when(g == gi)
            def _():
                d, cnt = _window_mean_minus_self(u_ref[...].astype(F32), pad_ref, w, S)
                db = d.astype(BF16)
                wb = w_ref[0].astype(BF16)
                yraw = jnp.dot(db, wb, preferred_element_type=F32)
                dy = dy_ref[...].astype(F32)
                ds_ref[...] = jnp.sum(dy * yraw, axis=0, keepdims=True)
                dzb = (dy * s_ref[...]).astype(BF16)
                dw_ref[0] = lax.dot_general(db, dzb, TN_DIMS, preferred_element_type=F32)
                dd = lax.dot_general(dzb, wb, NT, preferred_element_type=F32)
                pad_ref[0:S, :] = dd / cnt
                pad_ref[S:S + MAX_WINDOW, :] = jnp.zeros((MAX_WINDOW, LANES), F32)
                acc = -dd
                for j in range(w):
                    acc = acc + pad_ref[j:j + S, :]
                du_ref[...] = acc.astype(BF16)

    return pl.pallas_call(
        body, name="pool_bwd", grid=(G,),
        in_specs=[pl.BlockSpec((S, LANES), lambda g: (0, g)),
                  pl.BlockSpec((S, LANES), lambda g: (0, g)),
                  pl.BlockSpec((1, LANES, LANES), lambda g: (g, 0, 0)),
                  pl.BlockSpec((1, LANES), lambda g: (0, g))],
        out_specs=[pl.BlockSpec((S, LANES), lambda g: (0, g)),
                   pl.BlockSpec((1, LANES, LANES), lambda g: (g, 0, 0)),
                   pl.BlockSpec((1, LANES), lambda g: (0, g))],
        out_shape=[jax.ShapeDtypeStruct((S, G * LANES), BF16),
                   jax.ShapeDtypeStruct((G, LANES, LANES), F32),
                   jax.ShapeDtypeStruct((1, G * LANES), F32)],
        scratch_shapes=[pltpu.VMEM((S + MAX_WINDOW, LANES), F32)],
        compiler_params=_params("arbitrary"),
    )(proj, da, pool_w, pool_scale)


def _inproj_bwd_dx(sections, dzf, w, x, gam, g, after=()):
    S, D = x.shape
    N = w.shape[1] - LANES
    W = D // 2
    n_sec = len(sections)
    tm = min(TM // 2, S)

    def body(dz_ref, w_ref, wf_ref, x_ref, gam_ref, g_ref, *rest):
        sec_refs = rest[:n_sec]
        dx_ref, dg_ref = rest[-2:]

        @pl.when(pl.program_id(0) == 0)
        def _():
            dg_ref[...] = jnp.zeros(dg_ref.shape, F32)

        dh = lax.dot_general(dz_ref[...], wf_ref[...], NT, preferred_element_type=F32)
        for n, ref in enumerate(sec_refs):
            dh = dh + lax.dot_general(ref[...], w_ref[:, n * W:(n + 1) * W], NT, preferred_element_type=F32)
        xf = x_ref[...]
        r = lax.rsqrt(jnp.mean(xf * xf, axis=-1, keepdims=True) + RMS_EPS)
        xhat = xf * r
        dg_ref[...] += jnp.sum(dh * xhat, axis=0, keepdims=True)
        dxhat = dh * gam_ref[...]
        dx_ref[...] = g_ref[...] + r * (dxhat - xhat * jnp.mean(dxhat * xhat, axis=-1, keepdims=True))

    sec_specs, sec_arrays = _section_specs(sections, tm, W)
    return pl.pallas_call(
        body, name="inproj_bwd_dx", grid=(S // tm,),
        in_specs=[pl.BlockSpec((tm, LANES), lambda i: (i, 0)),
                  pl.BlockSpec((D, N), lambda i: (0, 0)),
                  pl.BlockSpec((D, LANES), lambda i: (0, N // LANES)),
                  pl.BlockSpec((tm, D), lambda i: (i, 0)),
                  pl.BlockSpec((1, D), lambda i: (0, 0)),
                  pl.BlockSpec((tm, D), lambda i: (i, 0))] + sec_specs + _after_specs(after),
        out_specs=[pl.BlockSpec((tm, D), lambda i: (i, 0)),
                   pl.BlockSpec((1, D), lambda i: (0, 0))],
        out_shape=[jax.ShapeDtypeStruct((S, D), F32),
                   jax.ShapeDtypeStruct((1, D), F32)],
        compiler_params=_params("arbitrary"),
    )(dzf, w, w, x, gam, g, *sec_arrays, *after)


def _adamw(w, m, v, gsets, name, rows, shifted=False, first=0, into=None):
    A, R, C = w.shape
    n_sets = len(gsets)
    tr = min(rows, R)
    c1 = 1.0 / (1.0 - ADAM_B1 ** ADAM_STEP)
    c2 = 1.0 / (1.0 - ADAM_B2 ** ADAM_STEP)
    counts = [len(gs) for gs in gsets]

    def body(w_ref, m_ref, v_ref, *rest):
        g_ref, d_ref, nm_ref, nv_ref = rest[-4:]
        at = 0
        for a in range(n_sets):
            part_refs = rest[at:at + counts[a]]
            at += counts[a]

            @pl.when(pl.program_id(0) == a)
            def _():
                g = None
                for ref in part_refs:
                    for s in range(ref.shape[0]):
                        term = ref[s].astype(F32)
                        g = term if g is None else g + term
                if shifted:
                    lanes = g.shape[1]
                    g = pltpu.roll(g, (lanes - _index(_position()) * (C % LANES)) % lanes, axis=1)[:, :C]
                nm = ADAM_B1 * m_ref[0] + (1.0 - ADAM_B1) * g
                nv = ADAM_B2 * v_ref[0] + (1.0 - ADAM_B2) * (g * g)
                g_ref[0] = g
                nm_ref[0] = nm
                nv_ref[0] = nv
                d_ref[0] = -ADAM_LR * ((nm * c1) / (jnp.sqrt(nv * c2) + ADAM_EPS) + ADAM_WD * w_ref[0])

    spec = pl.BlockSpec((1, tr, C), lambda a, r: (first + a, r, 0))
    part_specs = [pl.BlockSpec((part.shape[0], tr, part.shape[2]), lambda a, r, l=l: (0, jnp.where(a == l, r, 0), 0))
                  for l, gs in enumerate(gsets) for part in gs]
    parts = [part for gs in gsets for part in gs]
    shape = jax.ShapeDtypeStruct((A, R, C), F32)
    earlier = () if into is None else tuple(into)
    return pl.pallas_call(
        body, name=name, grid=(n_sets, R // tr),
        in_specs=[spec, spec, spec] + part_specs + _after_specs(earlier),
        out_specs=[spec, spec, spec, spec],
        out_shape=[shape, shape, shape, shape],
        input_output_aliases={3 + len(parts) + n: n for n in range(len(earlier))},
        compiler_params=_params("arbitrary", "arbitrary"),
    )(w, m, v, *parts, *earlier)


def _position():
    return lax.axis_index("x"), lax.axis_index("y"), lax.axis_index("c")


def _index(dev):
    return 4 * dev[0] + 2 * dev[1] + dev[2]


def _all_gather(arrs, slots, out_shapes, name):
    n_arr = len(arrs)

    def body(*refs):
        ins, outs = refs[:n_arr], refs[n_arr:2 * n_arr]
        send_sems, recv_sems, local_sems = refs[2 * n_arr:]
        x, y, c = _position()
        me, sibling = (x, y, c), (x, y, 1 - c)
        chips = [(1 - x, y), (x, 1 - y), (1 - x, 1 - y)]

        def copy(a, k, block, to, src=None):
            part = slots[a](outs[a], _index(block))
            return pltpu.make_async_remote_copy(
                src_ref=part if src is None else src, dst_ref=part,
                send_sem=send_sems.at[a, k], recv_sem=recv_sems.at[a, k],
                device_id=to, device_id_type=MESH)

        mine = [pltpu.make_async_copy(ins[a], slots[a](outs[a], _index(me)), local_sems.at[a])
                for a in range(n_arr)]
        for cp in mine:
            cp.start()
        first = []
        for a in range(n_arr):
            first.append(copy(a, 0, me, sibling, src=ins[a]))
            first += [copy(a, 1 + j, me, (*chip, c), src=ins[a]) for j, chip in enumerate(chips)]
        for cp in first:
            cp.start()
        passed = []
        for j, chip in enumerate(chips):
            for a in range(n_arr):
                copy(a, 1 + j, (*chip, c), me).wait_recv()
                fwd = copy(a, 4 + j, (*chip, c), sibling)
                fwd.start()
                passed.append(fwd)
        for a in range(n_arr):
            copy(a, 0, sibling, me).wait_recv()
            for j, chip in enumerate(chips):
                copy(a, 4 + j, (*chip, 1 - c), me).wait_recv()
        for cp in first + passed:
            cp.wait_send()
        for cp in mine:
            cp.wait()

    any_spec = pl.BlockSpec(memory_space=pl.ANY)
    return pl.pallas_call(
        body, name=name,
        in_specs=[any_spec] * n_arr, out_specs=[any_spec] * n_arr, out_shape=out_shapes,
        scratch_shapes=[pltpu.SemaphoreType.DMA((n_arr, 7)), pltpu.SemaphoreType.DMA((n_arr, 7)),
                        pltpu.SemaphoreType.DMA((n_arr,))],
    )(*arrs)


def _split_copies(srcs, lands, send_sems, recv_sems, kinds):
    x, y, c = _position()
    me = _index((x, y, c))
    copies = []
    for a, (src_part, land_part) in enumerate(kinds):
        for k in range(1, N_DEV):
            peer = (x ^ ((k >> 2) & 1), y ^ ((k >> 1) & 1), c ^ (k & 1))
            copies.append(pltpu.make_async_remote_copy(
                src_ref=src_part(srcs[a], _index(peer)), dst_ref=land_part(lands[a], me, k),
                send_sem=send_sems[a].at[k - 1], recv_sem=recv_sems[a].at[k - 1],
                device_id=peer, device_id_type=MESH))
    return copies


def _split_start(srcs, lands, kinds, name, after=()):
    n = len(srcs)

    def body(*refs):
        src_refs, land_refs = refs[:n], refs[n:2 * n]
        outs = refs[2 * n + len(after):]
        send_sems, recv_sems = outs[:n], outs[n:2 * n]
        token = outs[-1]
        for cp in _split_copies(src_refs, land_refs, send_sems, recv_sems, kinds):
            cp.start()
        token[...] = jnp.zeros(token.shape, token.dtype)

    hbm = pl.BlockSpec(memory_space=pltpu.HBM)
    sem = pl.BlockSpec(memory_space=pltpu.SEMAPHORE)
    operands = [pltpu.with_memory_space_constraint(t, pltpu.HBM) for t in (*srcs, *lands)]
    out = pl.pallas_call(
        body, name=name,
        in_specs=[hbm] * (2 * n) + _after_specs(after),
        out_specs=[sem] * (2 * n) + [hbm] * (2 * n) + [pl.BlockSpec(memory_space=pltpu.VMEM)],
        out_shape=[pltpu.SemaphoreType.DMA((N_DEV - 1,))] * (2 * n)
        + [pltpu.HBM(t.shape, t.dtype) for t in operands] + [jax.ShapeDtypeStruct((8, LANES), F32)],
        input_output_aliases={i: 2 * n + i for i in range(2 * n)},
        compiler_params=pltpu.CompilerParams(has_side_effects=pltpu.SideEffectType.DATAFLOW_SIDE_EFFECTING),
    )(*operands, *after)
    return [(out[a], out[n + a], out[2 * n + a], out[3 * n + a]) for a in range(n)], out[-1]


def _split_wait(started, kinds, after, name):
    n = len(started)
    sems = [t[0] for t in started] + [t[1] for t in started]
    srcs = [t[2] for t in started]
    lands = [t[3] for t in started]

    def body(*refs):
        src_refs, land_refs = refs[:n], refs[n:2 * n]
        send_sems, recv_sems = refs[2 * n:3 * n], refs[3 * n:4 * n]
        for cp in _split_copies(src_refs, land_refs, send_sems, recv_sems, kinds):
            cp.wait_send()
            cp.wait_recv()

    hbm = pl.BlockSpec(memory_space=pltpu.HBM)
    sem = pl.BlockSpec(memory_space=pltpu.SEMAPHORE)
    out = pl.pallas_call(
        body, name=name,
        in_specs=[hbm] * (2 * n) + [sem] * (2 * n) + _after_specs(after),
        out_specs=[hbm] * (2 * n),
        out_shape=[pltpu.HBM(t.shape, t.dtype) for t in (*srcs, *lands)],
        input_output_aliases={i: i for i in range(2 * n)},
        compiler_params=pltpu.CompilerParams(has_side_effects=pltpu.SideEffectType.DATAFLOW_SIDE_EFFECTING),
    )(*srcs, *lands, *sems, *after)
    return out[n:]


def _as_rows(p):
    if p.size % LANES == 0:
        rows = p.reshape(-1, LANES)
    else:
        rows = p.reshape(-1, p.shape[-1])
        rows = jnp.pad(rows, ((0, 0), (0, LANES - rows.shape[1])))
    return jnp.pad(rows, ((0, -rows.shape[0] % 8), (0, 0)))


def _pack(parts):
    return jnp.concatenate([_as_rows(p) for p in parts])[None]


def _unpack(packed, like):
    out, at = [], 0
    for p in like:
        whole = p.size % LANES == 0
        n = p.size // LANES if whole else p.size // p.shape[-1]
        rows = packed[0, at:at + n]
        out.append((rows if whole else rows[:, :p.shape[-1]]).reshape(p.shape))
        at += n + (-n % 8)
    return out


def _local_step(x, target, norm_g, forget_bias, pool_w, pool_scale, final_g, weights_in, weights_out, on_grads,
                first_after=()):
    L = norm_g.shape[0]
    S, D = x.shape
    W = D // 2
    H = W // HEAD_DIM
    bias = jnp.pad(forget_bias, ((0, 0), (0, LANES - H)))

    saved = []
    after = tuple(first_after)
    for l in range(L):
        proj, h, z, w = _inproj_fwd(x, norm_g[l:l + 1], weights_in(l, x), after)
        after = ()
        qaug, kaug = _fgate_fwd(z, bias[l:l + 1], H)
        ypool = _pool_fwd(proj, pool_w[l], pool_scale[l:l + 1])
        o, qaug_b = _attn_fwd(proj, qaug, kaug)
        wout = weights_out(l, o)
        x_new, mixed = _outproj_fwd(ypool, o, proj, x, wout)
        saved.append((x, proj, h, z, qaug_b, kaug, ypool, o, mixed, w, wout))
        x = x_new

    g, loss, d_final_g = _loss_head(x, final_g.reshape(1, D), target)

    small = None
    for l in reversed(range(L)):
        x_in, proj, h, z, qaug_b, kaug, ypool, o, mixed, w, wout = saved[l]
        d_wout, da, dgate, doaug = _outproj_bwd(g, wout, mixed, ypool, o, proj)
        dq, dk, dv, dqx, dkx = _attn_bwd(proj, da, qaug_b, kaug, doaug)
        dzf, db = _fgate_bwd(dqx, dkx, z, bias[l:l + 1], H)
        dpu, dpw, dps = _pool_bwd(proj, da, pool_w[l], pool_scale[l:l + 1])
        dproj = [(dpu, 0), (dgate, 0), (dq, 0), (dk, 0), (dv, 0), (dgate, 1)]
        d_w, d_w_bf16 = _inproj_bwd_dw(h, dproj, dzf)
        after = tuple(on_grads(l, d_w, d_w_bf16, d_wout, small))
        g, dgam = _inproj_bwd_dx(dproj, dzf, w, x_in, norm_g[l:l + 1], g, after)
        small = (dgam[0], db[0, :H], dpw, dps[0])
    return loss[0, 0], g, small, d_final_g[0]


def kernel(x, norm_g, w_in, forget_bias, pool_w, pool_scale, w_out, final_g, loss_target, m_norm_g, m_w_in, m_forget_bias, m_pool_w, m_pool_scale, m_w_out, m_final_g, v_norm_g, v_w_in, v_forget_bias, v_pool_w, v_pool_scale, v_w_out, v_final_g):
    L, D, cols = w_in.shape
    rows_out = w_out.shape[1]
    W = D // 2
    H = W // HEAD_DIM
    me = _index(_position())

    slot = _slot_width(cols)
    wout_b = w_out.astype(BF16)
    win_b = [lax.dynamic_update_slice(jnp.zeros((D, slot), BF16), w_in[l].astype(BF16),
                                      (0, me * (cols % LANES))) for l in range(L)]
    gather_in = (lambda ref, peer: ref, lambda ref, mine, k: ref.at[mine])
    gather_out = (lambda ref, peer: ref, lambda ref, mine, k: ref.at[pl.ds(mine * rows_out, rows_out), :])

    (first_in,) = _all_gather([win_b[0]], [lambda ref, n: ref.at[n]],
                              [jax.ShapeDtypeStruct((N_DEV, D, slot), BF16)], "gather_first")
    rest_srcs = [wout_b[0]] + [w[l] for l in range(1, L) for w in (win_b, wout_b)]
    rest_lands = [jnp.tile(wout_b[0], (N_DEV, 1))]
    for l in range(1, L):
        rest_lands += [jnp.broadcast_to(win_b[l][None], (N_DEV, D, slot)), jnp.tile(wout_b[l], (N_DEV, 1))]
    rest_kinds = [gather_out] + [gather_in, gather_out] * (L - 1)
    rest, rest_token = _split_start(rest_srcs, rest_lands, rest_kinds, "gather_start_rest", (first_in,))

    def weights_in(l, x_in):
        if l == 0:
            return first_in
        (win_all,) = _split_wait([rest[2 * l - 1]], [gather_in], (x_in,), f"gather_wait_in_{l}")
        return win_all

    def weights_out(l, o):
        (wout_full,) = _split_wait([rest[2 * l]], [gather_out], (o,), f"gather_wait_out_{l}")
        return wout_full

    stride = slot - LANES
    exchange_kinds = [(lambda ref, peer: ref.at[:, pl.ds(pl.multiple_of(peer * stride, LANES), slot)],
                       lambda ref, mine, k: ref.at[k - 1]),
                      (lambda ref, peer: ref.at[pl.ds(peer * rows_out, rows_out), :],
                       lambda ref, mine, k: ref.at[k - 1])]
    zero_g = jnp.zeros_like(final_g)

    def small_pack(l, norm_g_l, bias_l, pool_w_l, pool_scale_l, final):
        return _pack([norm_g_l, bias_l, pool_w_l, pool_scale_l, final if l == 0 else zero_g])[0]

    exchanges, own_parts = {}, {}

    def on_grads(l, dw, dw_bf16, d_wout, small):
        own_parts[l] = (lax.dynamic_slice_in_dim(dw, me * stride, slot, 1)[None],
                        lax.dynamic_slice_in_dim(d_wout, me * rows_out, rows_out, 0)[None])
        srcs = [dw_bf16, d_wout.astype(BF16)]
        lands = [lax.empty((N_DEV - 1, D, slot), BF16), lax.empty((N_DEV - 1, rows_out, D), BF16)]
        kinds = list(exchange_kinds)
        if small is not None:
            packed_small = small_pack(l + 1, *small, None)
            srcs.append(packed_small)
            lands.append(jnp.broadcast_to(packed_small[None], (N_DEV, *packed_small.shape)))
            kinds.append(gather_in)
        started, token = _split_start(srcs, lands, kinds, f"exchange_start_{l}")
        exchanges[l] = (started, kinds)
        return (token,)

    loss, dx, small_first, d_final_g = _local_step(
        x[0], loss_target[0], norm_g, forget_bias, pool_w, pool_scale, final_g,
        weights_in, weights_out, on_grads, (rest_token,))
    loss = lax.psum(loss, ("x", "y", "c"))
    packed_first = small_pack(0, *small_first, d_final_g)
    first_started, first_token = _split_start(
        [packed_first], [jnp.broadcast_to(packed_first[None], (N_DEV, *packed_first.shape))], [gather_in],
        "small_start", (w_in, m_w_in, v_w_in, *own_parts[0]))

    gin_sets, gout_sets, small_sets = [None] * L, [None] * L, [None] * L

    def wait_for(l, after):
        started, kinds = exchanges[l]
        got = _split_wait(started, kinds, after, f"exchange_wait_{l}")
        gin_sets[l] = [own_parts[l][0], got[0]]
        gout_sets[l] = [own_parts[l][1], got[1]]
        if len(got) > 2:
            small_sets[l + 1] = [got[2]]

    for l in range(1, L):
        wait_for(l, (dx, first_token))
    rest_in = _adamw(w_in, m_w_in, v_w_in, gin_sets[1:], "adamw_w_in_rest", 256, shifted=True, first=1)
    rest_out = _adamw(w_out, m_w_out, v_w_out, gout_sets[1:], "adamw_w_out_rest", 128, first=1)
    wait_for(0, (rest_in[1], rest_out[1]))
    g_w_in, d_w_in, nm_w_in, nv_w_in = _adamw(w_in, m_w_in, v_w_in, gin_sets[:1], "adamw_w_in_first", 256,
                                              shifted=True, into=rest_in)
    g_w_out, d_w_out, nm_w_out, nv_w_out = _adamw(w_out, m_w_out, v_w_out, gout_sets[:1], "adamw_w_out_first", 128,
                                                  into=rest_out)
    small_sets[0] = _split_wait(first_started, [gather_in], (d_w_in, d_w_out), "small_wait")

    def small_stack(norm_g_, bias_, pool_w_, pool_scale_, final):
        return jnp.stack([small_pack(l, norm_g_[l], bias_[l], pool_w_[l], pool_scale_[l], final) for l in range(L)])

    packed = _adamw(small_stack(norm_g, forget_bias, pool_w, pool_scale, final_g),
                    small_stack(m_norm_g, m_forget_bias, m_pool_w, m_pool_scale, m_final_g),
                    small_stack(v_norm_g, v_forget_bias, v_pool_w, v_pool_scale, v_final_g),
                    small_sets, "adamw_small", packed_first.shape[0])

    def small_unpack(p):
        like = [norm_g[0], forget_bias[0], pool_w[0], pool_scale[0], final_g]
        layers = [_unpack(p[l:l + 1], like) for l in range(L)]
        return [jnp.stack([layers[l][n] for l in range(L)]) for n in range(4)] + [layers[0][4]]

    g_s, d_s, nm_s, nv_s = [small_unpack(p) for p in packed]

    def order(big_in, big_out, small):
        return (small[0], big_in, small[1], small[2], small[3], big_out, small[4])

    return (loss, dx[None], *order(g_w_in, g_w_out, g_s), *order(d_w_in, d_w_out, d_s),
            *order(nm_w_in, nm_w_out, nm_s), *order(nv_w_in, nv_w_out, nv_s))
```

```python
import math

import jax
import jax.numpy as jnp
from jax import lax
from jax.experimental import pallas as pl
from jax.experimental.pallas import tpu as pltpu

F32 = jnp.float32
BF16 = jnp.bfloat16
MESH = pl.DeviceIdType.MESH

RMS_EPS = 1e-6
NEG_INF = -1e30
HEAD_DIM = 64
POOL_WINDOWS = (2, 4, 8, 16)
MAX_WINDOW = 16
LANES = 128
N_DEV = 8

ADAM_LR = 0.001
ADAM_B1 = 0.9
ADAM_B2 = 0.999
ADAM_EPS = 1e-08
ADAM_WD = 0.01
ADAM_STEP = 10

TM = 512
TN = 512
TQ = 512
TB = 256
VMEM_LIMIT = 56 * 1024 * 1024

NT = (((1,), (1,)), ((), ()))
TN_DIMS = (((0,), (0,)), ((), ()))

SLOT_C, SLOT_ONE, SLOT_LSE = 0, 3, 6


def _params(*sem):
    return pltpu.CompilerParams(dimension_semantics=sem, vmem_limit_bytes=VMEM_LIMIT)


def _sigmoid(x):
    return 1.0 / (1.0 + jnp.exp(-x))


def _split3(x):
    hi = x.astype(BF16).astype(F32)
    rest = x - hi
    mid = rest.astype(BF16).astype(F32)
    return hi, mid, rest - mid


def _after_specs(after):
    return [pl.BlockSpec(memory_space=pl.ANY)] * len(after)


def _slot_width(cols):
    return LANES * (-(-(cols + (N_DEV - 1) * (cols % LANES)) // LANES))


def _inproj_fwd(x, gam, slots, after=()):
    S, D = x.shape
    n_dev, _, sw = slots.shape
    stride = sw - LANES
    width = stride * n_dev + LANES
    N = width - LANES
    tm, tn = min(TM, S), TN

    def body(x_ref, g_ref, s_ref, *rest):
        proj_ref, h_ref, z_ref, w_ref = rest[-4:]

        @pl.when(pl.program_id(0) == 0)
        def _():
            for n in range(n_dev):
                base = stride * n
                first = s_ref[n, :, 0:LANES]
                if n > 0:
                    first = first + s_ref[n - 1, :, stride:sw]
                w_ref[:, base:base + LANES] = first
                w_ref[:, base + LANES:base + stride] = s_ref[n, :, LANES:stride]
            w_ref[:, stride * n_dev:width] = s_ref[n_dev - 1, :, stride:sw]

        xf = x_ref[...]
        r = lax.rsqrt(jnp.mean(xf * xf, axis=-1, keepdims=True) + RMS_EPS)
        h = ((xf * r) * g_ref[...]).astype(BF16)
        h_ref[...] = h
        z_ref[...] = jnp.dot(h, w_ref[:, N:width], preferred_element_type=F32)
        for n in range(N // tn):
            cols = slice(n * tn, (n + 1) * tn)
            proj_ref[:, cols] = jnp.dot(h, w_ref[:, cols], preferred_element_type=F32).astype(BF16)

    return pl.pallas_call(
        body, name="inproj_fwd", grid=(S // tm,),
        in_specs=[pl.BlockSpec((tm, D), lambda i: (i, 0)),
                  pl.BlockSpec((1, D), lambda i: (0, 0)),
                  pl.BlockSpec((n_dev, D, sw), lambda i: (0, 0, 0))] + _after_specs(after),
        out_specs=[pl.BlockSpec((tm, N), lambda i: (i, 0)),
                   pl.BlockSpec((tm, D), lambda i: (i, 0)),
                   pl.BlockSpec((tm, LANES), lambda i: (i, 0)),
                   pl.BlockSpec((D, width), lambda i: (0, 0))],
        out_shape=[jax.ShapeDtypeStruct((S, N), BF16),
                   jax.ShapeDtypeStruct((S, D), BF16),
                   jax.ShapeDtypeStruct((S, LANES), F32),
                   jax.ShapeDtypeStruct((D, width), BF16)],
        compiler_params=_params("arbitrary"),
    )(x, gam, slots, *after)


def _fgate_fwd(z, bias, n_heads):
    S = z.shape[0]
    tb = min(TB, S)
    P = n_heads // 2

    def body(z_ref, b_ref, qaug_ref, kaug_ref):
        lane = lax.broadcasted_iota(jnp.int32, (tb, LANES), 1)
        tri = (lax.broadcasted_iota(jnp.int32, (tb, tb), 0)
               >= lax.broadcasted_iota(jnp.int32, (tb, tb), 1)).astype(F32)
        head = lax.broadcasted_iota(jnp.int32, (LANES, P * LANES), 0)
        col = lax.broadcasted_iota(jnp.int32, (LANES, P * LANES), 1)
        home = (head >> 1) * LANES + jnp.where((head & 1) == 0, HEAD_DIM, 0)
        is_head = head < n_heads
        place_q = [jnp.logical_and(is_head, col == home + SLOT_C + n).astype(BF16) for n in range(3)]
        place_k = [jnp.logical_and(is_head, col == home + SLOT_ONE + n).astype(BF16) for n in range(3)]
        slot = lax.broadcasted_iota(jnp.int32, (tb, P * LANES), 1) & (HEAD_DIM - 1)
        q_ones = jnp.logical_and(slot >= SLOT_ONE, slot < SLOT_ONE + 3).astype(F32)
        k_ones = jnp.logical_or(slot < SLOT_C + 3,
                                jnp.logical_and(slot >= SLOT_LSE, slot < SLOT_LSE + 3)).astype(F32)

        local = []
        for b in range(S // tb):
            zz = z_ref[b * tb:(b + 1) * tb, :] + b_ref[...]
            lf = jnp.minimum(zz, 0.0) - jnp.log(1.0 + jnp.exp(-jnp.abs(zz)))
            lf = jnp.where(lane < n_heads, lf, 0.0)
            local.append(jnp.dot(tri, lf, preferred_element_type=F32, precision=lax.Precision.HIGHEST))
        carry = jnp.zeros((1, LANES), F32)
        for b, part_sum in enumerate(local):
            c = part_sum + carry
            carry = c[tb - 1:tb, :]
            qa, ka = q_ones, k_ones
            for n, part in enumerate(_split3(c)):
                qa = qa + jnp.dot(part.astype(BF16), place_q[n], preferred_element_type=F32)
                ka = ka - jnp.dot(part.astype(BF16), place_k[n], preferred_element_type=F32)
            qaug_ref[b * tb:(b + 1) * tb, :] = qa.astype(BF16)
            kaug_ref[b * tb:(b + 1) * tb, :] = ka.astype(BF16)

    return pl.pallas_call(
        body, name="fgate_fwd",
        out_shape=[jax.ShapeDtypeStruct((S, P * LANES), BF16),
                   jax.ShapeDtypeStruct((S, P * LANES), BF16)],
        compiler_params=pltpu.CompilerParams(vmem_limit_bytes=VMEM_LIMIT),
    )(z, bias)


def _window_mean_minus_self(u, pad_ref, w, S):
    pad_ref[0:MAX_WINDOW, :] = jnp.zeros((MAX_WINDOW, LANES), F32)
    pad_ref[MAX_WINDOW:MAX_WINDOW + S, :] = u
    acc = u
    for j in range(1, w):
        acc = acc + pad_ref[MAX_WINDOW - j:MAX_WINDOW - j + S, :]
    t = lax.broadcasted_iota(jnp.int32, (S, LANES), 0)
    cnt = jnp.minimum(t + 1, w).astype(F32)
    return acc / cnt - u, cnt


def _pool_fwd(proj, pool_w, pool_scale):
    S = proj.shape[0]
    G = len(POOL_WINDOWS)

    def body(u_ref, w_ref, s_ref, y_ref, pad_ref):
        g = pl.program_id(0)
        for gi, w in enumerate(POOL_WINDOWS):
            @pl.when(g == gi)
            def _():
                d, _ = _window_mean_minus_self(u_ref[...].astype(F32), pad_ref, w, S)
                y = jnp.dot(d.astype(BF16), w_ref[0].astype(BF16), preferred_element_type=F32)
                y_ref[...] = (y * s_ref[...]).astype(BF16)

    return pl.pallas_call(
        body, name="pool_fwd", grid=(G,),
        in_specs=[pl.BlockSpec((S, LANES), lambda g: (0, g)),
                  pl.BlockSpec((1, LANES, LANES), lambda g: (g, 0, 0)),
                  pl.BlockSpec((1, LANES), lambda g: (0, g))],
        out_specs=pl.BlockSpec((S, LANES), lambda g: (0, g)),
        out_shape=jax.ShapeDtypeStruct((S, G * LANES), BF16),
        scratch_shapes=[pltpu.VMEM((S + MAX_WINDOW, LANES), F32)],
        compiler_params=_params("arbitrary"),
    )(proj, pool_w, pool_scale)


def _head_halves(rows):
    lane = lax.broadcasted_iota(jnp.int32, (rows, LANES), 1)
    return lane, (lane < HEAD_DIM, lane >= HEAD_DIM)


def _attn_fwd(proj, qaug, kaug):
    S = proj.shape[0]
    W = proj.shape[1] // 6
    P = W // LANES
    tq = min(TQ, S)
    nq = S // tq
    qc, kc, vc = 2 * P, 3 * P, 4 * P
    scale = 1.0 / math.sqrt(HEAD_DIM)

    def body(q_ref, k_ref, v_ref, qa_ref, ka_ref, o_ref, qb_ref, qm_scr, m_scr, acc_scr):
        i = pl.program_id(1)
        lane, halves = _head_halves(tq)
        v_ones = ((lane & (HEAD_DIM - 1)) < 3).astype(BF16)
        qs = q_ref[...] * scale
        qm_scr[0] = jnp.where(halves[0], qs, qa_ref[...])
        qm_scr[1] = jnp.where(halves[1], qs, qa_ref[...])
        m_scr[...] = jnp.full(m_scr.shape, NEG_INF, F32)
        acc_scr[...] = jnp.zeros(acc_scr.shape, F32)

        def update(j, on_diagonal):
            keys = pl.ds(pl.multiple_of(j * tq, tq), tq)
            k2, v2, kaug_t = k_ref[keys, :], v_ref[keys, :], ka_ref[keys, :]
            if on_diagonal:
                keep = (lax.broadcasted_iota(jnp.int32, (tq, tq), 0)
                        >= lax.broadcasted_iota(jnp.int32, (tq, tq), 1))
            logits = [lax.dot_general(qm_scr[a], jnp.where(halves[a], k2, kaug_t), NT, preferred_element_type=F32)
                      for a in range(2)]
            for a in range(2):
                s = jnp.where(keep, logits[a], NEG_INF) if on_diagonal else logits[a]
                va = jnp.where(halves[a], v2, v_ones)
                m_prev = m_scr[a]
                m_new = jnp.maximum(m_prev, jnp.max(s, axis=1, keepdims=True))
                p = jnp.exp(s - jnp.tile(m_new, (1, tq // LANES)))
                acc_scr[a] = jnp.exp(m_prev - m_new) * acc_scr[a] + jnp.dot(p.astype(BF16), va,
                                                                              preferred_element_type=F32)
                m_scr[a] = m_new

        def below_diagonal(jj, carry):
            update(2 * jj, False)
            update(2 * jj + 1, False)
            return carry

        lax.fori_loop(0, i // 2, below_diagonal, 0)

        @pl.when(i % 2 == 1)
        def _():
            update(i - 1, False)

        update(i, True)
        acc_a, acc_b = acc_scr[0], acc_scr[1]
        l_a, l_b = acc_a[:, HEAD_DIM:HEAD_DIM + 1], acc_b[:, 0:1]
        o_ref[...] = jnp.where(halves[0], acc_a / l_a, acc_b / l_b).astype(BF16)
        lse = jnp.where(halves[0], m_scr[1] + jnp.log(l_b), m_scr[0] + jnp.log(l_a))
        slot = lane & (HEAD_DIM - 1)
        aug = qa_ref[...].astype(F32)
        for n, part in enumerate(_split3(lse)):
            aug = jnp.where(slot == SLOT_LSE + n, -part, aug)
        qb_ref[...] = aug.astype(BF16)

    tile = lambda col: pl.BlockSpec((tq, LANES), lambda p, i: (i, col + p))
    whole = lambda col: pl.BlockSpec((S, LANES), lambda p, i: (0, col + p))
    return pl.pallas_call(
        body, name="attn_fwd", grid=(P, nq),
        in_specs=[tile(qc), whole(kc), whole(vc), tile(0), whole(0)],
        out_specs=[tile(0), tile(0)],
        out_shape=[jax.ShapeDtypeStruct((S, W), BF16), jax.ShapeDtypeStruct((S, W), BF16)],
        scratch_shapes=[pltpu.VMEM((2, tq, LANES), BF16),
                        pltpu.VMEM((2, tq, LANES), F32),
                        pltpu.VMEM((2, tq, LANES), F32)],
        compiler_params=_params("parallel", "arbitrary"),
    )(proj, proj, proj, qaug, kaug)


def _outproj_fwd(ypool, o, proj, x, wout):
    S, D = x.shape
    W = D // 2
    tm, tn = min(TM, S), TN

    def body(y_ref, o_ref, pg_ref, ag_ref, x_ref, w_ref, xn_ref, mix_ref):
        pg, ag = pg_ref[...].astype(F32), ag_ref[...].astype(F32)
        mix_ref[:, 0:W] = (y_ref[...].astype(F32) * (pg * _sigmoid(pg))).astype(BF16)
        mix_ref[:, W:D] = (o_ref[...].astype(F32) * (ag * _sigmoid(ag))).astype(BF16)
        for n in range(D // tn):
            cols = slice(n * tn, (n + 1) * tn)
            xn_ref[:, cols] = x_ref[:, cols] + jnp.dot(mix_ref[...], w_ref[:, cols], preferred_element_type=F32)

    return pl.pallas_call(
        body, name="outproj_fwd", grid=(S // tm,),
        in_specs=[pl.BlockSpec((tm, W), lambda i: (i, 0)),
                  pl.BlockSpec((tm, W), lambda i: (i, 0)),
                  pl.BlockSpec((tm, W), lambda i: (i, 1)),
                  pl.BlockSpec((tm, W), lambda i: (i, 5)),
                  pl.BlockSpec((tm, D), lambda i: (i, 0)),
                  pl.BlockSpec((D, D), lambda i: (0, 0))],
        out_specs=[pl.BlockSpec((tm, D), lambda i: (i, 0)),
                   pl.BlockSpec((tm, D), lambda i: (i, 0))],
        out_shape=[jax.ShapeDtypeStruct((S, D), F32),
                   jax.ShapeDtypeStruct((S, D), BF16)],
        compiler_params=_params("parallel"),
    )(ypool, o, proj, proj, x, wout)


def _loss_head(x, gam, target):
    S, D = x.shape
    tm = min(TM, S)

    def body(x_ref, g_ref, t_ref, dx_ref, loss_ref, dg_ref):
        @pl.when(pl.program_id(0) == 0)
        def _():
            loss_ref[...] = jnp.zeros(loss_ref.shape, F32)
            dg_ref[...] = jnp.zeros(dg_ref.shape, F32)

        xf, gam_v = x_ref[...], g_ref[...]
        r = lax.rsqrt(jnp.mean(xf * xf, axis=-1, keepdims=True) + RMS_EPS)
        xhat = xf * r
        err = xhat * gam_v - t_ref[...]
        part = jnp.sum(jnp.sum(err * err, axis=-1, keepdims=True), axis=0, keepdims=True)
        loss_ref[...] += part * (0.5 / D)
        dy = err * (1.0 / D)
        dg_ref[...] += jnp.sum(dy * xhat, axis=0, keepdims=True)
        dxhat = dy * gam_v
        dx_ref[...] = r * (dxhat - xhat * jnp.mean(dxhat * xhat, axis=-1, keepdims=True))

    return pl.pallas_call(
        body, name="loss_head", grid=(S // tm,),
        in_specs=[pl.BlockSpec((tm, D), lambda i: (i, 0)),
                  pl.BlockSpec((1, D), lambda i: (0, 0)),
                  pl.BlockSpec((tm, D), lambda i: (i, 0))],
        out_specs=[pl.BlockSpec((tm, D), lambda i: (i, 0)),
                   pl.BlockSpec((8, LANES), lambda i: (0, 0)),
                   pl.BlockSpec((1, D), lambda i: (0, 0))],
        out_shape=[jax.ShapeDtypeStruct((S, D), F32),
                   jax.ShapeDtypeStruct((8, LANES), F32),
                   jax.ShapeDtypeStruct((1, D), F32)],
        compiler_params=_params("arbitrary"),
    )(x, gam, target)


def _outproj_bwd(g, wout, mixed, ypool, o, proj):
    S, D = g.shape
    W = D // 2
    tm = min(TM, S)

    def body(g_ref, w_ref, mix_ref, y_ref, o_ref, pg_ref, ag_ref, dw_ref, dwb_ref, da_ref, dgate_ref, doaug_ref):
        @pl.when(pl.program_id(0) == 0)
        def _():
            dw_ref[...] = jnp.zeros(dw_ref.shape, F32)

        gb = g_ref[...].astype(BF16)
        dw_ref[...] += lax.dot_general(mix_ref[...], gb, TN_DIMS, preferred_element_type=F32)
        for half, (val_ref, gate_ref) in enumerate(((y_ref, pg_ref), (o_ref, ag_ref))):
            cols = slice(half * W, (half + 1) * W)
            dmix = lax.dot_general(gb, w_ref[cols, :], NT, preferred_element_type=F32)
            gt = gate_ref[...].astype(F32)
            sg = _sigmoid(gt)
            da_ref[:, cols] = (dmix * (gt * sg)).astype(BF16)
            dgate_ref[:, cols] = (dmix * val_ref[...].astype(F32) * (sg * (1.0 + gt * (1.0 - sg)))).astype(BF16)

        lane, halves = _head_halves(tm)
        slot = lane & (HEAD_DIM - 1)
        for p in range(W // LANES):
            cols = slice(p * LANES, (p + 1) * LANES)
            prod = da_ref[:, W + p * LANES:W + (p + 1) * LANES].astype(F32) * o_ref[:, cols].astype(F32)
            d_a = jnp.sum(jnp.where(halves[0], prod, 0.0), axis=1, keepdims=True)
            d_b = jnp.sum(jnp.where(halves[1], prod, 0.0), axis=1, keepdims=True)
            aug = jnp.zeros((tm, LANES), F32)
            for n, part in enumerate(_split3(jnp.where(halves[0], d_b, d_a))):
                aug = jnp.where(slot == SLOT_C + n, -part, aug)
            doaug_ref[:, cols] = aug.astype(BF16)

        @pl.when(pl.program_id(0) == S // tm - 1)
        def _():
            dwb_ref[...] = dw_ref[...].astype(BF16)

    rows = lambda width, col: pl.BlockSpec((tm, width), lambda i: (i, col))
    whole = pl.BlockSpec((D, D), lambda i: (0, 0))
    return pl.pallas_call(
        body, name="outproj_bwd", grid=(S // tm,),
        in_specs=[rows(D, 0), whole, rows(D, 0), rows(W, 0), rows(W, 0), rows(W, 1), rows(W, 5)],
        out_specs=[whole, whole, rows(D, 0), rows(D, 0), rows(W, 0)],
        out_shape=[jax.ShapeDtypeStruct((D, D), F32),
                   jax.ShapeDtypeStruct((D, D), BF16),
                   jax.ShapeDtypeStruct((S, D), BF16),
                   jax.ShapeDtypeStruct((S, D), BF16),
                   jax.ShapeDtypeStruct((S, W), BF16)],
        compiler_params=_params("arbitrary"),
    )(g, wout, mixed, ypool, o, proj, proj)


def _section_specs(sections, rows, width):
    specs = [pl.BlockSpec((rows, width), lambda k, c=c: (k, c)) for _, c in sections]
    return specs, [a for a, _ in sections]


def _inproj_bwd_dw(h, sections, dzf):
    S, D = h.shape
    W = D // 2
    n_sec = len(sections)
    N = n_sec * W
    ts = min(TM, S)
    n_steps = S // ts

    def body(h_ref, dz_ref, *rest):
        sec_refs, (dw_ref, dwb_ref) = rest[:n_sec], rest[n_sec:]

        @pl.when(pl.program_id(0) == 0)
        def _():
            dw_ref[...] = jnp.zeros(dw_ref.shape, F32)

        ht = h_ref[...].T
        dw_ref[:, N:N + LANES] += jnp.dot(ht, dz_ref[...], preferred_element_type=F32)
        for n, ref in enumerate(sec_refs):
            dw_ref[:, n * W:(n + 1) * W] += jnp.dot(ht, ref[...], preferred_element_type=F32)

        @pl.when(pl.program_id(0) == n_steps - 1)
        def _():
            dwb_ref[...] = dw_ref[...].astype(BF16)

    sec_specs, sec_arrays = _section_specs(sections, ts, W)
    whole = pl.BlockSpec((D, N + LANES), lambda k: (0, 0))
    return pl.pallas_call(
        body, name="inproj_bwd_dw", grid=(n_steps,),
        in_specs=[pl.BlockSpec((ts, D), lambda k: (k, 0)),
                  pl.BlockSpec((ts, LANES), lambda k: (k, 0))] + sec_specs,
        out_specs=[whole, whole],
        out_shape=[jax.ShapeDtypeStruct((D, N + LANES), F32),
                   jax.ShapeDtypeStruct((D, N + LANES), BF16)],
        compiler_params=_params("arbitrary"),
    )(h, dzf, *sec_arrays)


def _attn_bwd(proj, da, qaug, kaug, doaug):
    S = proj.shape[0]
    W = proj.shape[1] // 6
    P = W // LANES
    tq = min(TQ, S)
    nq = S // tq
    qc, kc, vc = 2 * P, 3 * P, 4 * P
    scale = 1.0 / math.sqrt(HEAD_DIM)

    def body(q_ref, k_ref, v_ref, do_ref, qa_ref, ka_ref, da_ref,
             dq_ref, dk_ref, dv_ref, dqx_ref, dkx_ref, km_scr, vm_scr, dk_scr, dv_scr, dq_scr):
        j = pl.program_id(1)
        lane, halves = _head_halves(tq)

        @pl.when(j == 0)
        def _():
            dq_scr[...] = jnp.zeros(dq_scr.shape, F32)

        v_ones = ((lane & (HEAD_DIM - 1)) < 3).astype(BF16)
        for a in range(2):
            km_scr[a] = jnp.where(halves[a], k_ref[...], ka_ref[...])
            vm_scr[a] = jnp.where(halves[a], v_ref[...], v_ones)
        dk_scr[...] = jnp.zeros(dk_scr.shape, F32)
        dv_scr[...] = jnp.zeros(dv_scr.shape, F32)

        def update(i, on_diagonal):
            rows = pl.ds(pl.multiple_of(i * tq, tq), tq)
            qs = q_ref[rows, :] * scale
            do2, qaug_t, doaug_t = do_ref[rows, :], qa_ref[rows, :], da_ref[rows, :]
            if on_diagonal:
                keep = (lax.broadcasted_iota(jnp.int32, (tq, tq), 0)
                        >= lax.broadcasted_iota(jnp.int32, (tq, tq), 1))
            qas = [jnp.where(halves[a], qs, qaug_t) for a in range(2)]
            logits = [lax.dot_general(qas[a], km_scr[a], NT, preferred_element_type=F32) for a in range(2)]
            dps = [lax.dot_general(jnp.where(halves[a], do2, doaug_t), vm_scr[a], NT, preferred_element_type=F32)
                   for a in range(2)]
            dv = None
            for a in range(2):
                s = jnp.where(keep, logits[a], NEG_INF) if on_diagonal else logits[a]
                p = jnp.exp(s)
                dsb = (p * dps[a]).astype(BF16)
                do0 = jnp.where(halves[a], do2, jnp.zeros_like(do2))
                dv_a = lax.dot_general(p.astype(BF16), do0, TN_DIMS, preferred_element_type=F32)
                dv = dv_a if dv is None else dv + dv_a
                dk_scr[a] += lax.dot_general(dsb, qas[a], TN_DIMS, preferred_element_type=F32)
                dq_scr[a, rows, :] += jnp.dot(dsb, km_scr[a], preferred_element_type=F32)
            dv_scr[...] += dv

        def below_diagonal(n, carry):
            update(j + 1 + 2 * n, False)
            update(j + 2 + 2 * n, False)
            return carry

        update(j, True)
        below = nq - 1 - j
        lax.fori_loop(0, below // 2, below_diagonal, 0)

        @pl.when(below % 2 == 1)
        def _():
            update(nq - 1, False)


        dk_ref[...] = jnp.where(halves[0], dk_scr[0], dk_scr[1]).astype(BF16)
        dkx_ref[...] = jnp.where(halves[0], dk_scr[1], dk_scr[0])
        dv_ref[...] = dv_scr[...].astype(BF16)

        @pl.when(j == nq - 1)
        def _():
            row_lane, row_halves = _head_halves(S)
            dq_ref[...] = (jnp.where(row_halves[0], dq_scr[0], dq_scr[1]) * scale).astype(BF16)
            dqx_ref[...] = jnp.where(row_halves[0], dq_scr[1], dq_scr[0])

    tile = lambda col: pl.BlockSpec((tq, LANES), lambda p, j: (j, col + p))
    whole = lambda col: pl.BlockSpec((S, LANES), lambda p, j: (0, col + p))
    return pl.pallas_call(
        body, name="attn_bwd", grid=(P, nq),
        in_specs=[whole(qc), tile(kc), tile(vc), whole(P), whole(0), tile(0), whole(0)],
        out_specs=[whole(0), tile(0), tile(0), whole(0), tile(0)],
        out_shape=[jax.ShapeDtypeStruct((S, W), BF16),
                   jax.ShapeDtypeStruct((S, W), BF16),
                   jax.ShapeDtypeStruct((S, W), BF16),
                   jax.ShapeDtypeStruct((S, W), F32),
                   jax.ShapeDtypeStruct((S, W), F32)],
        scratch_shapes=[pltpu.VMEM((2, tq, LANES), BF16),
                        pltpu.VMEM((2, tq, LANES), BF16),
                        pltpu.VMEM((2, tq, LANES), F32),
                        pltpu.VMEM((tq, LANES), F32),
                        pltpu.VMEM((2, S, LANES), F32)],
        compiler_params=_params("parallel", "arbitrary"),
    )(proj, proj, proj, da, qaug, kaug, doaug)


def _fgate_bwd(dqx, dkx, z, bias, n_heads):
    S = z.shape[0]
    tb = min(TB, S)
    nb = S // tb

    def body(dqx_ref, dkx_ref, z_ref, b_ref, dz_ref, db_ref):
        tri = (lax.broadcasted_iota(jnp.int32, (tb, tb), 1)
               >= lax.broadcasted_iota(jnp.int32, (tb, tb), 0)).astype(F32)
        lane = lax.broadcasted_iota(jnp.int32, (tb, LANES), 1)

        local = []
        for b in range(nb):
            rows = slice(b * tb, (b + 1) * tb)
            dc = jnp.zeros((tb, LANES), F32)
            for p in range(n_heads // 2):
                row_sums = dqx_ref[rows, p * LANES:(p + 1) * LANES]
                col_sums = dkx_ref[rows, p * LANES:(p + 1) * LANES]
                for h, at in ((2 * p, HEAD_DIM), (2 * p + 1, 0)):
                    one = row_sums[:, at + SLOT_C:at + SLOT_C + 1] - col_sums[:, at + SLOT_ONE:at + SLOT_ONE + 1]
                    dc = jnp.where(lane == h, jnp.broadcast_to(one, (tb, LANES)), dc)
            local.append(jnp.dot(tri, dc, preferred_element_type=F32, precision=lax.Precision.HIGHEST))
        carry = jnp.zeros((1, LANES), F32)
        db = jnp.zeros((1, LANES), F32)
        for b in reversed(range(nb)):
            rows = slice(b * tb, (b + 1) * tb)
            rc = local[b] + carry
            carry = rc[0:1, :]
            dz = rc * _sigmoid(-(z_ref[rows, :] + b_ref[...]))
            dz_ref[rows, :] = dz.astype(BF16)
            db = db + jnp.sum(dz, axis=0, keepdims=True)
        db_ref[...] = db

    return pl.pallas_call(
        body, name="fgate_bwd",
        out_shape=[jax.ShapeDtypeStruct((S, LANES), BF16),
                   jax.ShapeDtypeStruct((1, LANES), F32)],
        compiler_params=pltpu.CompilerParams(vmem_limit_bytes=VMEM_LIMIT),
    )(dqx, dkx, z, bias)


def _pool_bwd(proj, da, pool_w, pool_scale):
    S = proj.shape[0]
    G = len(POOL_WINDOWS)

    def body(u_ref, dy_ref, w_ref, s_ref, du_ref, dw_ref, ds_ref, pad_ref):
        g = pl.program_id(0)
        for gi, w in enumerate(POOL_WINDOWS):
            @pl.when(g == gi)
            def _():
                d, cnt = _window_mean_minus_self(u_ref[...].astype(F32), pad_ref, w, S)
                db = d.astype(BF16)
                wb = w_ref[0].astype(BF16)
                yraw = jnp.dot(db, wb, preferred_element_type=F32)
                dy = dy_ref[...].astype(F32)
                ds_ref[...] = jnp.sum(dy * yraw, axis=0, keepdims=True)
                dzb = (dy * s_ref[...]).astype(BF16)
                dw_ref[0] = lax.dot_general(db, dzb, TN_DIMS, preferred_element_type=F32)
                dd = lax.dot_general(dzb, wb, NT, preferred_element_type=F32)
                pad_ref[0:S, :] = dd / cnt
                pad_ref[S:S + MAX_WINDOW, :] = jnp.zeros((MAX_WINDOW, LANES), F32)
                acc = -dd
                for j in range(w):
                    acc = acc + pad_ref[j:j + S, :]
                du_ref[...] = acc.astype(BF16)

    return pl.pallas_call(
        body, name="pool_bwd", grid=(G,),
        in_specs=[pl.BlockSpec((S, LANES), lambda g: (0, g)),
                  pl.BlockSpec((S, LANES), lambda g: (0, g)),
                  pl.BlockSpec((1, LANES, LANES), lambda g: (g, 0, 0)),
                  pl.BlockSpec((1, LANES), lambda g: (0, g))],
        out_specs=[pl.BlockSpec((S, LANES), lambda g: (0, g)),
                   pl.BlockSpec((1, LANES, LANES), lambda g: (g, 0, 0)),
                   pl.BlockSpec((1, LANES), lambda g: (0, g))],
        out_shape=[jax.ShapeDtypeStruct((S, G * LANES), BF16),
                   jax.ShapeDtypeStruct((G, LANES, LANES), F32),
                   jax.ShapeDtypeStruct((1, G * LANES), F32)],
        scratch_shapes=[pltpu.VMEM((S + MAX_WINDOW, LANES), F32)],
        compiler_params=_params("arbitrary"),
    )(proj, da, pool_w, pool_scale)


def _inproj_bwd_dx(sections, dzf, w, x, gam, g, after=()):
    S, D = x.shape
    N = w.shape[1] - LANES
    W = D // 2
    n_sec = len(sections)
    tm = min(TM // 2, S)

    def body(dz_ref, w_ref, wf_ref, x_ref, gam_ref, g_ref, *rest):
        sec_refs = rest[:n_sec]
        dx_ref, dg_ref = rest[-2:]

        @pl.when(pl.program_id(0) == 0)
        def _():
            dg_ref[...] = jnp.zeros(dg_ref.shape, F32)

        dh = lax.dot_general(dz_ref[...], wf_ref[...], NT, preferred_element_type=F32)
        for n, ref in enumerate(sec_refs):
            dh = dh + lax.dot_general(ref[...], w_ref[:, n * W:(n + 1) * W], NT, preferred_element_type=F32)
        xf = x_ref[...]
        r = lax.rsqrt(jnp.mean(xf * xf, axis=-1, keepdims=True) + RMS_EPS)
        xhat = xf * r
        dg_ref[...] += jnp.sum(dh * xhat, axis=0, keepdims=True)
        dxhat = dh * gam_ref[...]
        dx_ref[...] = g_ref[...] + r * (dxhat - xhat * jnp.mean(dxhat * xhat, axis=-1, keepdims=True))

    sec_specs, sec_arrays = _section_specs(sections, tm, W)
    return pl.pallas_call(
        body, name="inproj_bwd_dx", grid=(S // tm,),
        in_specs=[pl.BlockSpec((tm, LANES), lambda i: (i, 0)),
                  pl.BlockSpec((D, N), lambda i: (0, 0)),
                  pl.BlockSpec((D, LANES), lambda i: (0, N // LANES)),
                  pl.BlockSpec((tm, D), lambda i: (i, 0)),
                  pl.BlockSpec((1, D), lambda i: (0, 0)),
                  pl.BlockSpec((tm, D), lambda i: (i, 0))] + sec_specs + _after_specs(after),
        out_specs=[pl.BlockSpec((tm, D), lambda i: (i, 0)),
                   pl.BlockSpec((1, D), lambda i: (0, 0))],
        out_shape=[jax.ShapeDtypeStruct((S, D), F32),
                   jax.ShapeDtypeStruct((1, D), F32)],
        compiler_params=_params("arbitrary"),
    )(dzf, w, w, x, gam, g, *sec_arrays, *after)


def _adamw(w, m, v, gsets, name, rows, shifted=False, first=0, into=None):
    A, R, C = w.shape
    n_sets = len(gsets)
    tr = min(rows, R)
    c1 = 1.0 / (1.0 - ADAM_B1 ** ADAM_STEP)
    c2 = 1.0 / (1.0 - ADAM_B2 ** ADAM_STEP)
    counts = [len(gs) for gs in gsets]

    def body(w_ref, m_ref, v_ref, *rest):
        g_ref, d_ref, nm_ref, nv_ref = rest[-4:]
        at = 0
        for a in range(n_sets):
            part_refs = rest[at:at + counts[a]]
            at += counts[a]

            @pl.when(pl.program_id(0) == a)
            def _():
                g = None
                for ref in part_refs:
                    for s in range(ref.shape[0]):
                        term = ref[s].astype(F32)
                        g = term if g is None else g + term
                if shifted:
                    lanes = g.shape[1]
                    g = pltpu.roll(g, (lanes - _index(_position()) * (C % LANES)) % lanes, axis=1)[:, :C]
                nm = ADAM_B1 * m_ref[0] + (1.0 - ADAM_B1) * g
                nv = ADAM_B2 * v_ref[0] + (1.0 - ADAM_B2) * (g * g)
                g_ref[0] = g
                nm_ref[0] = nm
                nv_ref[0] = nv
                d_ref[0] = -ADAM_LR * ((nm * c1) / (jnp.sqrt(nv * c2) + ADAM_EPS) + ADAM_WD * w_ref[0])

    spec = pl.BlockSpec((1, tr, C), lambda a, r: (first + a, r, 0))
    part_specs = [pl.BlockSpec((part.shape[0], tr, part.shape[2]), lambda a, r, l=l: (0, jnp.where(a == l, r, 0), 0))
                  for l, gs in enumerate(gsets) for part in gs]
    parts = [part for gs in gsets for part in gs]
    shape = jax.ShapeDtypeStruct((A, R, C), F32)
    earlier = () if into is None else tuple(into)
    return pl.pallas_call(
        body, name=name, grid=(n_sets, R // tr),
        in_specs=[spec, spec, spec] + part_specs + _after_specs(earlier),
        out_specs=[spec, spec, spec, spec],
        out_shape=[shape, shape, shape, shape],
        input_output_aliases={3 + len(parts) + n: n for n in range(len(earlier))},
        compiler_params=_params("arbitrary", "arbitrary"),
    )(w, m, v, *parts, *earlier)


def _position():
    return lax.axis_index("x"), lax.axis_index("y"), lax.axis_index("c")


def _index(dev):
    return 4 * dev[0] + 2 * dev[1] + dev[2]


def _all_gather(arrs, slots, out_shapes, name):
    n_arr = len(arrs)

    def body(*refs):
        ins, outs = refs[:n_arr], refs[n_arr:2 * n_arr]
        send_sems, recv_sems, local_sems = refs[2 * n_arr:]
        x, y, c = _position()
        me, sibling = (x, y, c), (x, y, 1 - c)
        chips = [(1 - x, y), (x, 1 - y), (1 - x, 1 - y)]

        def copy(a, k, block, to, src=None):
            part = slots[a](outs[a], _index(block))
            return pltpu.make_async_remote_copy(
                src_ref=part if src is None else src, dst_ref=part,
                send_sem=send_sems.at[a, k], recv_sem=recv_sems.at[a, k],
                device_id=to, device_id_type=MESH)

        mine = [pltpu.make_async_copy(ins[a], slots[a](outs[a], _index(me)), local_sems.at[a])
                for a in range(n_arr)]
        for cp in mine:
            cp.start()
        first = []
        for a in range(n_arr):
            first.append(copy(a, 0, me, sibling, src=ins[a]))
            first += [copy(a, 1 + j, me, (*chip, c), src=ins[a]) for j, chip in enumerate(chips)]
        for cp in first:
            cp.start()
        passed = []
        for j, chip in enumerate(chips):
            for a in range(n_arr):
                copy(a, 1 + j, (*chip, c), me).wait_recv()
                fwd = copy(a, 4 + j, (*chip, c), sibling)
                fwd.start()
                passed.append(fwd)
        for a in range(n_arr):
            copy(a, 0, sibling, me).wait_recv()
            for j, chip in enumerate(chips):
                copy(a, 4 + j, (*chip, 1 - c), me).wait_recv()
        for cp in first + passed:
            cp.wait_send()
        for cp in mine:
            cp.wait()

    any_spec = pl.BlockSpec(memory_space=pl.ANY)
    return pl.pallas_call(
        body, name=name,
        in_specs=[any_spec] * n_arr, out_specs=[any_spec] * n_arr, out_shape=out_shapes,
        scratch_shapes=[pltpu.SemaphoreType.DMA((n_arr, 7)), pltpu.SemaphoreType.DMA((n_arr, 7)),
                        pltpu.SemaphoreType.DMA((n_arr,))],
    )(*arrs)


def _split_copies(srcs, lands, send_sems, recv_sems, kinds):
    x, y, c = _position()
    me = _index((x, y, c))
    copies = []
    for a, (src_part, land_part) in enumerate(kinds):
        for k in range(1, N_DEV):
            peer = (x ^ ((k >> 2) & 1), y ^ ((k >> 1) & 1), c ^ (k & 1))
            copies.append(pltpu.make_async_remote_copy(
                src_ref=src_part(srcs[a], _index(peer)), dst_ref=land_part(lands[a], me, k),
                send_sem=send_sems[a].at[k - 1], recv_sem=recv_sems[a].at[k - 1],
                device_id=peer, device_id_type=MESH))
    return copies


def _split_start(srcs, lands, kinds, name, after=()):
    n = len(srcs)

    def body(*refs):
        src_refs, land_refs = refs[:n], refs[n:2 * n]
        outs = refs[2 * n + len(after):]
        send_sems, recv_sems = outs[:n], outs[n:2 * n]
        token = outs[-1]
        for cp in _split_copies(src_refs, land_refs, send_sems, recv_sems, kinds):
            cp.start()
        token[...] = jnp.zeros(token.shape, token.dtype)

    hbm = pl.BlockSpec(memory_space=pltpu.HBM)
    sem = pl.BlockSpec(memory_space=pltpu.SEMAPHORE)
    operands = [pltpu.with_memory_space_constraint(t, pltpu.HBM) for t in (*srcs, *lands)]
    out = pl.pallas_call(
        body, name=name,
        in_specs=[hbm] * (2 * n) + _after_specs(after),
        out_specs=[sem] * (2 * n) + [hbm] * (2 * n) + [pl.BlockSpec(memory_space=pltpu.VMEM)],
        out_shape=[pltpu.SemaphoreType.DMA((N_DEV - 1,))] * (2 * n)
        + [pltpu.HBM(t.shape, t.dtype) for t in operands] + [jax.ShapeDtypeStruct((8, LANES), F32)],
        input_output_aliases={i: 2 * n + i for i in range(2 * n)},
        compiler_params=pltpu.CompilerParams(has_side_effects=pltpu.SideEffectType.DATAFLOW_SIDE_EFFECTING),
    )(*operands, *after)
    return [(out[a], out[n + a], out[2 * n + a], out[3 * n + a]) for a in range(n)], out[-1]


def _split_wait(started, kinds, after, name):
    n = len(started)
    sems = [t[0] for t in started] + [t[1] for t in started]
    srcs = [t[2] for t in started]
    lands = [t[3] for t in started]

    def body(*refs):
        src_refs, land_refs = refs[:n], refs[n:2 * n]
        send_sems, recv_sems = refs[2 * n:3 * n], refs[3 * n:4 * n]
        for cp in _split_copies(src_refs, land_refs, send_sems, recv_sems, kinds):
            cp.wait_send()
            cp.wait_recv()

    hbm = pl.BlockSpec(memory_space=pltpu.HBM)
    sem = pl.BlockSpec(memory_space=pltpu.SEMAPHORE)
    out = pl.pallas_call(
        body, name=name,
        in_specs=[hbm] * (2 * n) + [sem] * (2 * n) + _after_specs(after),
        out_specs=[hbm] * (2 * n),
        out_shape=[pltpu.HBM(t.shape, t.dtype) for t in (*srcs, *lands)],
        input_output_aliases={i: i for i in range(2 * n)},
        compiler_params=pltpu.CompilerParams(has_side_effects=pltpu.SideEffectType.DATAFLOW_SIDE_EFFECTING),
    )(*srcs, *lands, *sems, *after)
    return out[n:]


def _as_rows(p):
    if p.size % LANES == 0:
        rows = p.reshape(-1, LANES)
    else:
        rows = p.reshape(-1, p.shape[-1])
        rows = jnp.pad(rows, ((0, 0), (0, LANES - rows.shape[1])))
    return jnp.pad(rows, ((0, -rows.shape[0] % 8), (0, 0)))


def _pack(parts):
    return jnp.concatenate([_as_rows(p) for p in parts])[None]


def _unpack(packed, like):
    out, at = [], 0
    for p in like:
        whole = p.size % LANES == 0
        n = p.size // LANES if whole else p.size // p.shape[-1]
        rows = packed[0, at:at + n]
        out.append((rows if whole else rows[:, :p.shape[-1]]).reshape(p.shape))
        at += n + (-n % 8)
    return out


def _local_step(x, target, norm_g, forget_bias, pool_w, pool_scale, final_g, weights_in, weights_out, on_grads,
                first_after=()):
    L = norm_g.shape[0]
    S, D = x.shape
    W = D // 2
    H = W // HEAD_DIM
    bias = jnp.pad(forget_bias, ((0, 0), (0, LANES - H)))

    saved = []
    after = tuple(first_after)
    for l in range(L):
        proj, h, z, w = _inproj_fwd(x, norm_g[l:l + 1], weights_in(l, x), after)
        after = ()
        qaug, kaug = _fgate_fwd(z, bias[l:l + 1], H)
        ypool = _pool_fwd(proj, pool_w[l], pool_scale[l:l + 1])
        o, qaug_b = _attn_fwd(proj, qaug, kaug)
        wout = weights_out(l, o)
        x_new, mixed = _outproj_fwd(ypool, o, proj, x, wout)
        saved.append((x, proj, h, z, qaug_b, kaug, ypool, o, mixed, w, wout))
        x = x_new

    g, loss, d_final_g = _loss_head(x, final_g.reshape(1, D), target)

    small = None
    for l in reversed(range(L)):
        x_in, proj, h, z, qaug_b, kaug, ypool, o, mixed, w, wout = saved[l]
        d_wout, d_wout_bf16, da, dgate, doaug = _outproj_bwd(g, wout, mixed, ypool, o, proj)
        dq, dk, dv, dqx, dkx = _attn_bwd(proj, da, qaug_b, kaug, doaug)
        dzf, db = _fgate_bwd(dqx, dkx, z, bias[l:l + 1], H)
        dpu, dpw, dps = _pool_bwd(proj, da, pool_w[l], pool_scale[l:l + 1])
        dproj = [(dpu, 0), (dgate, 0), (dq, 0), (dk, 0), (dv, 0), (dgate, 1)]
        d_w, d_w_bf16 = _inproj_bwd_dw(h, dproj, dzf)
        after = tuple(on_grads(l, d_w, d_w_bf16, d_wout, d_wout_bf16, small))
        g, dgam = _inproj_bwd_dx(dproj, dzf, w, x_in, norm_g[l:l + 1], g, after)
        small = (dgam[0], db[0, :H], dpw, dps[0])
    return loss[0, 0], g, small, d_final_g[0]


def kernel(x, norm_g, w_in, forget_bias, pool_w, pool_scale, w_out, final_g, loss_target, m_norm_g, m_w_in, m_forget_bias, m_pool_w, m_pool_scale, m_w_out, m_final_g, v_norm_g, v_w_in, v_forget_bias, v_pool_w, v_pool_scale, v_w_out, v_final_g):
    L, D, cols = w_in.shape
    rows_out = w_out.shape[1]
    W = D // 2
    H = W // HEAD_DIM
    me = _index(_position())

    slot = _slot_width(cols)
    wout_b = w_out.astype(BF16)
    win_b = lax.dynamic_update_slice(jnp.zeros((L, D, slot), BF16), w_in.astype(BF16),
                                     (0, 0, me * (cols % LANES)))
    gather_in = (lambda ref, peer: ref, lambda ref, mine, k: ref.at[mine])
    gather_out = (lambda ref, peer: ref, lambda ref, mine, k: ref.at[pl.ds(mine * rows_out, rows_out), :])

    def landing(block, n_slots):
        zone = lax.empty((n_slots * block.shape[0], *block.shape[1:]), block.dtype)
        return lax.dynamic_update_slice(zone, block, (me * block.shape[0],) + (0,) * (block.ndim - 1))

    (first_in,) = _all_gather([win_b[0]], [lambda ref, n: ref.at[n]],
                              [jax.ShapeDtypeStruct((N_DEV, D, slot), BF16)], "gather_first")
    rest_srcs = [wout_b[0]] + [w[l] for l in range(1, L) for w in (win_b, wout_b)]
    rest_lands = [landing(wout_b[0], N_DEV)]
    for l in range(1, L):
        rest_lands += [landing(win_b[l][None], N_DEV), landing(wout_b[l], N_DEV)]
    rest_kinds = [gather_out] + [gather_in, gather_out] * (L - 1)
    rest, rest_token = _split_start(rest_srcs, rest_lands, rest_kinds, "gather_start_rest", (first_in,))

    def weights_in(l, x_in):
        if l == 0:
            return first_in
        (win_all,) = _split_wait([rest[2 * l - 1]], [gather_in], (x_in,), f"gather_wait_in_{l}")
        return win_all

    def weights_out(l, o):
        (wout_full,) = _split_wait([rest[2 * l]], [gather_out], (o,), f"gather_wait_out_{l}")
        return wout_full

    stride = slot - LANES
    exchange_kinds = [(lambda ref, peer: ref.at[:, pl.ds(pl.multiple_of(peer * stride, LANES), slot)],
                       lambda ref, mine, k: ref.at[k - 1]),
                      (lambda ref, peer: ref.at[pl.ds(peer * rows_out, rows_out), :],
                       lambda ref, mine, k: ref.at[k - 1])]
    zero_g = jnp.zeros_like(final_g)
    zero_loss = jnp.zeros((LANES,), F32)

    def small_pack(l, norm_g_l, bias_l, pool_w_l, pool_scale_l, final, loss_row=None):
        return _pack([norm_g_l, bias_l, pool_w_l, pool_scale_l, final if l == 0 else zero_g,
                      zero_loss if loss_row is None else loss_row])[0]

    exchanges, own_parts = {}, {}

    def on_grads(l, dw, dw_bf16, d_wout, d_wout_bf16, small):
        own_parts[l] = (lax.dynamic_slice_in_dim(dw, me * stride, slot, 1)[None],
                        lax.dynamic_slice_in_dim(d_wout, me * rows_out, rows_out, 0)[None])
        srcs = [dw_bf16, d_wout_bf16]
        lands = [lax.empty((N_DEV - 1, D, slot), BF16), lax.empty((N_DEV - 1, rows_out, D), BF16)]
        kinds = list(exchange_kinds)
        if small is not None:
            packed_small = small_pack(l + 1, *small, None)
            srcs.append(packed_small)
            lands.append(landing(packed_small[None], N_DEV))
            kinds.append(gather_in)
        started, token = _split_start(srcs, lands, kinds, f"exchange_start_{l}")
        exchanges[l] = (started, kinds)
        return (token,)

    loss, dx, small_first, d_final_g = _local_step(
        x[0], loss_target[0], norm_g, forget_bias, pool_w, pool_scale, final_g,
        weights_in, weights_out, on_grads, (rest_token,))
    packed_first = small_pack(0, *small_first, d_final_g, jnp.full((LANES,), loss, F32))
    first_started, first_token = _split_start(
        [packed_first], [landing(packed_first[None], N_DEV)], [gather_in],
        "small_start", (w_in, m_w_in, v_w_in, *own_parts[0]))

    gin_sets, gout_sets, small_sets = [None] * L, [None] * L, [None] * L

    def wait_for(l, after):
        started, kinds = exchanges[l]
        got = _split_wait(started, kinds, after, f"exchange_wait_{l}")
        gin_sets[l] = [own_parts[l][0], got[0]]
        gout_sets[l] = [own_parts[l][1], got[1]]
        if len(got) > 2:
            small_sets[l + 1] = [got[2]]

    for l in range(1, L):
        wait_for(l, (dx, first_token))
    rest_in = _adamw(w_in, m_w_in, v_w_in, gin_sets[1:], "adamw_w_in_rest", 256, shifted=True, first=1)
    rest_out = _adamw(w_out, m_w_out, v_w_out, gout_sets[1:], "adamw_w_out_rest", 128, first=1)
    wait_for(0, (rest_in[1], rest_out[1]))
    g_w_in, d_w_in, nm_w_in, nv_w_in = _adamw(w_in, m_w_in, v_w_in, gin_sets[:1], "adamw_w_in_first", 256,
                                              shifted=True, into=rest_in)
    g_w_out, d_w_out, nm_w_out, nv_w_out = _adamw(w_out, m_w_out, v_w_out, gout_sets[:1], "adamw_w_out_first", 128,
                                                  into=rest_out)
    small_sets[0] = _split_wait(first_started, [gather_in], (d_w_in, d_w_out), "small_wait")
    loss = jnp.sum(small_sets[0][0][:, packed_first.shape[0] - 8, 0])

    def small_stack(norm_g_, bias_, pool_w_, pool_scale_, final):
        return jnp.stack([small_pack(l, norm_g_[l], bias_[l], pool_w_[l], pool_scale_[l], final) for l in range(L)])

    packed = _adamw(small_stack(norm_g, forget_bias, pool_w, pool_scale, final_g),
                    small_stack(m_norm_g, m_forget_bias, m_pool_w, m_pool_scale, m_final_g),
                    small_stack(v_norm_g, v_forget_bias, v_pool_w, v_pool_scale, v_final_g),
                    small_sets, "adamw_small", packed_first.shape[0])

    def small_unpack(p):
        like = [norm_g[0], forget_bias[0], pool_w[0], pool_scale[0], final_g]
        layers = [_unpack(p[l:l + 1], like) for l in range(L)]
        return [jnp.stack([layers[l][n] for l in range(L)]) for n in range(4)] + [layers[0][4]]

    g_s, d_s, nm_s, nv_s = [small_unpack(p) for p in packed]

    def order(big_in, big_out, small):
        return (small[0], big_in, small[1], small[2], small[3], big_out, small[4])

    return (loss, dx[None], *order(g_w_in, g_w_out, g_s), *order(d_w_in, d_w_out, d_s),
            *order(nm_w_in, nm_w_out, nm_s), *order(nv_w_in, nv_w_out, nv_s))
```

```python
import math

import jax
import jax.numpy as jnp
from jax import lax
from jax.experimental import pallas as pl
from jax.experimental.pallas import tpu as pltpu

F32 = jnp.float32
BF16 = jnp.bfloat16
MESH = pl.DeviceIdType.MESH

RMS_EPS = 1e-6
NEG_INF = -1e30
HEAD_DIM = 64
POOL_WINDOWS = (2, 4, 8, 16)
MAX_WINDOW = 16
LANES = 128
N_DEV = 8

ADAM_LR = 0.001
ADAM_B1 = 0.9
ADAM_B2 = 0.999
ADAM_EPS = 1e-08
ADAM_WD = 0.01
ADAM_STEP = 10

TM = 512
TN = 512
TQ = 512
TB = 256
VMEM_LIMIT = 56 * 1024 * 1024

NT = (((1,), (1,)), ((), ()))
TN_DIMS = (((0,), (0,)), ((), ()))

SLOT_C, SLOT_ONE, SLOT_LSE = 0, 3, 6


def _params(*sem):
    return pltpu.CompilerParams(dimension_semantics=sem, vmem_limit_bytes=VMEM_LIMIT)


def _sigmoid(x):
    return 1.0 / (1.0 + jnp.exp(-x))


def _split3(x):
    hi = x.astype(BF16).astype(F32)
    rest = x - hi
    mid = rest.astype(BF16).astype(F32)
    return hi, mid, rest - mid


def _after_specs(after):
    return [pl.BlockSpec(memory_space=pl.ANY)] * len(after)


def _slot_width(cols):
    return LANES * (-(-(cols + (N_DEV - 1) * (cols % LANES)) // LANES))


def _shift_slots(w_in):
    L, D, cols = w_in.shape
    slot = _slot_width(cols)
    tr = min(TM // 2, D)

    def body(w_ref, o_ref, pad_scr):
        pad_scr[...] = jnp.zeros(pad_scr.shape, F32)
        pad_scr[:, 0:cols] = w_ref[0]
        o_ref[0] = pltpu.roll(pad_scr[...], _index(_position()) * (cols % LANES), axis=1).astype(BF16)

    return pl.pallas_call(
        body, name="shift_slots", grid=(L, D // tr),
        in_specs=[pl.BlockSpec((1, tr, cols), lambda l, r: (l, r, 0))],
        out_specs=pl.BlockSpec((1, tr, slot), lambda l, r: (l, r, 0)),
        out_shape=jax.ShapeDtypeStruct((L, D, slot), BF16),
        scratch_shapes=[pltpu.VMEM((tr, slot), F32)],
        compiler_params=_params("parallel", "parallel"),
    )(w_in)


def _inproj_fwd(x, gam, slots, after=()):
    S, D = x.shape
    n_dev, _, sw = slots.shape
    stride = sw - LANES
    width = stride * n_dev + LANES
    N = width - LANES
    tm, tn = min(TM, S), TN

    def body(x_ref, g_ref, s_ref, *rest):
        proj_ref, h_ref, z_ref, w_ref = rest[-4:]

        @pl.when(pl.program_id(0) == 0)
        def _():
            for n in range(n_dev):
                base = stride * n
                first = s_ref[n, :, 0:LANES]
                if n > 0:
                    first = first + s_ref[n - 1, :, stride:sw]
                w_ref[:, base:base + LANES] = first
                w_ref[:, base + LANES:base + stride] = s_ref[n, :, LANES:stride]
            w_ref[:, stride * n_dev:width] = s_ref[n_dev - 1, :, stride:sw]

        xf = x_ref[...]
        r = lax.rsqrt(jnp.mean(xf * xf, axis=-1, keepdims=True) + RMS_EPS)
        h = ((xf * r) * g_ref[...]).astype(BF16)
        h_ref[...] = h
        z_ref[...] = jnp.dot(h, w_ref[:, N:width], preferred_element_type=F32)
        for n in range(N // tn):
            cols = slice(n * tn, (n + 1) * tn)
            proj_ref[:, cols] = jnp.dot(h, w_ref[:, cols], preferred_element_type=F32).astype(BF16)

    return pl.pallas_call(
        body, name="inproj_fwd", grid=(S // tm,),
        in_specs=[pl.BlockSpec((tm, D), lambda i: (i, 0)),
                  pl.BlockSpec((1, D), lambda i: (0, 0)),
                  pl.BlockSpec((n_dev, D, sw), lambda i: (0, 0, 0))] + _after_specs(after),
        out_specs=[pl.BlockSpec((tm, N), lambda i: (i, 0)),
                   pl.BlockSpec((tm, D), lambda i: (i, 0)),
                   pl.BlockSpec((tm, LANES), lambda i: (i, 0)),
                   pl.BlockSpec((D, width), lambda i: (0, 0))],
        out_shape=[jax.ShapeDtypeStruct((S, N), BF16),
                   jax.ShapeDtypeStruct((S, D), BF16),
                   jax.ShapeDtypeStruct((S, LANES), F32),
                   jax.ShapeDtypeStruct((D, width), BF16)],
        compiler_params=_params("arbitrary"),
    )(x, gam, slots, *after)


def _fgate_fwd(z, bias, n_heads):
    S = z.shape[0]
    tb = min(TB, S)
    P = n_heads // 2

    def body(z_ref, b_ref, qaug_ref, kaug_ref):
        lane = lax.broadcasted_iota(jnp.int32, (tb, LANES), 1)
        tri = (lax.broadcasted_iota(jnp.int32, (tb, tb), 0)
               >= lax.broadcasted_iota(jnp.int32, (tb, tb), 1)).astype(F32)
        head = lax.broadcasted_iota(jnp.int32, (LANES, P * LANES), 0)
        col = lax.broadcasted_iota(jnp.int32, (LANES, P * LANES), 1)
        home = (head >> 1) * LANES + jnp.where((head & 1) == 0, HEAD_DIM, 0)
        is_head = head < n_heads
        place_q = [jnp.logical_and(is_head, col == home + SLOT_C + n).astype(BF16) for n in range(3)]
        place_k = [jnp.logical_and(is_head, col == home + SLOT_ONE + n).astype(BF16) for n in range(3)]
        slot = lax.broadcasted_iota(jnp.int32, (tb, P * LANES), 1) & (HEAD_DIM - 1)
        q_ones = jnp.logical_and(slot >= SLOT_ONE, slot < SLOT_ONE + 3).astype(F32)
        k_ones = jnp.logical_or(slot < SLOT_C + 3,
                                jnp.logical_and(slot >= SLOT_LSE, slot < SLOT_LSE + 3)).astype(F32)

        local = []
        for b in range(S // tb):
            zz = z_ref[b * tb:(b + 1) * tb, :] + b_ref[...]
            lf = jnp.minimum(zz, 0.0) - jnp.log(1.0 + jnp.exp(-jnp.abs(zz)))
            lf = jnp.where(lane < n_heads, lf, 0.0)
            local.append(jnp.dot(tri, lf, preferred_element_type=F32, precision=lax.Precision.HIGHEST))
        carry = jnp.zeros((1, LANES), F32)
        for b, part_sum in enumerate(local):
            c = part_sum + carry
            carry = c[tb - 1:tb, :]
            qa, ka = q_ones, k_ones
            for n, part in enumerate(_split3(c)):
                qa = qa + jnp.dot(part.astype(BF16), place_q[n], preferred_element_type=F32)
                ka = ka - jnp.dot(part.astype(BF16), place_k[n], preferred_element_type=F32)
            qaug_ref[b * tb:(b + 1) * tb, :] = qa.astype(BF16)
            kaug_ref[b * tb:(b + 1) * tb, :] = ka.astype(BF16)

    return pl.pallas_call(
        body, name="fgate_fwd",
        out_shape=[jax.ShapeDtypeStruct((S, P * LANES), BF16),
                   jax.ShapeDtypeStruct((S, P * LANES), BF16)],
        compiler_params=pltpu.CompilerParams(vmem_limit_bytes=VMEM_LIMIT),
    )(z, bias)


def _window_mean_minus_self(u, pad_ref, w, S):
    pad_ref[0:MAX_WINDOW, :] = jnp.zeros((MAX_WINDOW, LANES), F32)
    pad_ref[MAX_WINDOW:MAX_WINDOW + S, :] = u
    acc = u
    for j in range(1, w):
        acc = acc + pad_ref[MAX_WINDOW - j:MAX_WINDOW - j + S, :]
    t = lax.broadcasted_iota(jnp.int32, (S, LANES), 0)
    cnt = jnp.minimum(t + 1, w).astype(F32)
    return acc / cnt - u, cnt


def _pool_fwd(proj, pool_w, pool_scale):
    S = proj.shape[0]
    G = len(POOL_WINDOWS)

    def body(u_ref, w_ref, s_ref, y_ref, pad_ref):
        g = pl.program_id(0)
        for gi, w in enumerate(POOL_WINDOWS):
            @pl.when(g == gi)
            def _():
                d, _ = _window_mean_minus_self(u_ref[...].astype(F32), pad_ref, w, S)
                y = jnp.dot(d.astype(BF16), w_ref[0].astype(BF16), preferred_element_type=F32)
                y_ref[...] = (y * s_ref[...]).astype(BF16)

    return pl.pallas_call(
        body, name="pool_fwd", grid=(G,),
        in_specs=[pl.BlockSpec((S, LANES), lambda g: (0, g)),
                  pl.BlockSpec((1, LANES, LANES), lambda g: (g, 0, 0)),
                  pl.BlockSpec((1, LANES), lambda g: (0, g))],
        out_specs=pl.BlockSpec((S, LANES), lambda g: (0, g)),
        out_shape=jax.ShapeDtypeStruct((S, G * LANES), BF16),
        scratch_shapes=[pltpu.VMEM((S + MAX_WINDOW, LANES), F32)],
        compiler_params=_params("arbitrary"),
    )(proj, pool_w, pool_scale)


def _head_halves(rows):
    lane = lax.broadcasted_iota(jnp.int32, (rows, LANES), 1)
    return lane, (lane < HEAD_DIM, lane >= HEAD_DIM)


def _attn_fwd(proj, qaug, kaug):
    S = proj.shape[0]
    W = proj.shape[1] // 6
    P = W // LANES
    tq = min(TQ, S)
    nq = S // tq
    qc, kc, vc = 2 * P, 3 * P, 4 * P
    scale = 1.0 / math.sqrt(HEAD_DIM)

    def body(q_ref, k_ref, v_ref, qa_ref, ka_ref, o_ref, qb_ref, qm_scr, m_scr, acc_scr):
        i = pl.program_id(1)
        lane, halves = _head_halves(tq)
        v_ones = ((lane & (HEAD_DIM - 1)) < 3).astype(BF16)
        qs = q_ref[...] * scale
        qm_scr[0] = jnp.where(halves[0], qs, qa_ref[...])
        qm_scr[1] = jnp.where(halves[1], qs, qa_ref[...])
        m_scr[...] = jnp.full(m_scr.shape, NEG_INF, F32)
        acc_scr[...] = jnp.zeros(acc_scr.shape, F32)

        def update(j, on_diagonal):
            keys = pl.ds(pl.multiple_of(j * tq, tq), tq)
            k2, v2, kaug_t = k_ref[keys, :], v_ref[keys, :], ka_ref[keys, :]
            if on_diagonal:
                keep = (lax.broadcasted_iota(jnp.int32, (tq, tq), 0)
                        >= lax.broadcasted_iota(jnp.int32, (tq, tq), 1))
            logits = [lax.dot_general(qm_scr[a], jnp.where(halves[a], k2, kaug_t), NT, preferred_element_type=F32)
                      for a in range(2)]
            for a in range(2):
                s = jnp.where(keep, logits[a], NEG_INF) if on_diagonal else logits[a]
                va = jnp.where(halves[a], v2, v_ones)
                m_prev = m_scr[a]
                m_new = jnp.maximum(m_prev, jnp.max(s, axis=1, keepdims=True))
                p = jnp.exp(s - jnp.tile(m_new, (1, tq // LANES)))
                acc_scr[a] = jnp.exp(m_prev - m_new) * acc_scr[a] + jnp.dot(p.astype(BF16), va,
                                                                              preferred_element_type=F32)
                m_scr[a] = m_new

        def below_diagonal(jj, carry):
            update(2 * jj, False)
            update(2 * jj + 1, False)
            return carry

        lax.fori_loop(0, i // 2, below_diagonal, 0)

        @pl.when(i % 2 == 1)
        def _():
            update(i - 1, False)

        update(i, True)
        acc_a, acc_b = acc_scr[0], acc_scr[1]
        l_a, l_b = acc_a[:, HEAD_DIM:HEAD_DIM + 1], acc_b[:, 0:1]
        o_ref[...] = jnp.where(halves[0], acc_a / l_a, acc_b / l_b).astype(BF16)
        lse = jnp.where(halves[0], m_scr[1] + jnp.log(l_b), m_scr[0] + jnp.log(l_a))
        slot = lane & (HEAD_DIM - 1)
        aug = qa_ref[...].astype(F32)
        for n, part in enumerate(_split3(lse)):
            aug = jnp.where(slot == SLOT_LSE + n, -part, aug)
        qb_ref[...] = aug.astype(BF16)

    tile = lambda col: pl.BlockSpec((tq, LANES), lambda p, i: (i, col + p))
    whole = lambda col: pl.BlockSpec((S, LANES), lambda p, i: (0, col + p))
    return pl.pallas_call(
        body, name="attn_fwd", grid=(P, nq),
        in_specs=[tile(qc), whole(kc), whole(vc), tile(0), whole(0)],
        out_specs=[tile(0), tile(0)],
        out_shape=[jax.ShapeDtypeStruct((S, W), BF16), jax.ShapeDtypeStruct((S, W), BF16)],
        scratch_shapes=[pltpu.VMEM((2, tq, LANES), BF16),
                        pltpu.VMEM((2, tq, LANES), F32),
                        pltpu.VMEM((2, tq, LANES), F32)],
        compiler_params=_params("parallel", "arbitrary"),
    )(proj, proj, proj, qaug, kaug)


def _outproj_fwd(ypool, o, proj, x, wout):
    S, D = x.shape
    W = D // 2
    tm, tn = min(TM, S), TN

    def body(y_ref, o_ref, pg_ref, ag_ref, x_ref, w_ref, xn_ref, mix_ref):
        pg, ag = pg_ref[...].astype(F32), ag_ref[...].astype(F32)
        mix_ref[:, 0:W] = (y_ref[...].astype(F32) * (pg * _sigmoid(pg))).astype(BF16)
        mix_ref[:, W:D] = (o_ref[...].astype(F32) * (ag * _sigmoid(ag))).astype(BF16)
        for n in range(D // tn):
            cols = slice(n * tn, (n + 1) * tn)
            xn_ref[:, cols] = x_ref[:, cols] + jnp.dot(mix_ref[...], w_ref[:, cols], preferred_element_type=F32)

    return pl.pallas_call(
        body, name="outproj_fwd", grid=(S // tm,),
        in_specs=[pl.BlockSpec((tm, W), lambda i: (i, 0)),
                  pl.BlockSpec((tm, W), lambda i: (i, 0)),
                  pl.BlockSpec((tm, W), lambda i: (i, 1)),
                  pl.BlockSpec((tm, W), lambda i: (i, 5)),
                  pl.BlockSpec((tm, D), lambda i: (i, 0)),
                  pl.BlockSpec((D, D), lambda i: (0, 0))],
        out_specs=[pl.BlockSpec((tm, D), lambda i: (i, 0)),
                   pl.BlockSpec((tm, D), lambda i: (i, 0))],
        out_shape=[jax.ShapeDtypeStruct((S, D), F32),
                   jax.ShapeDtypeStruct((S, D), BF16)],
        compiler_params=_params("parallel"),
    )(ypool, o, proj, proj, x, wout)


def _loss_head(x, gam, target):
    S, D = x.shape
    tm = min(TM, S)

    def body(x_ref, g_ref, t_ref, dx_ref, loss_ref, dg_ref):
        @pl.when(pl.program_id(0) == 0)
        def _():
            loss_ref[...] = jnp.zeros(loss_ref.shape, F32)
            dg_ref[...] = jnp.zeros(dg_ref.shape, F32)

        xf, gam_v = x_ref[...], g_ref[...]
        r = lax.rsqrt(jnp.mean(xf * xf, axis=-1, keepdims=True) + RMS_EPS)
        xhat = xf * r
        err = xhat * gam_v - t_ref[...]
        part = jnp.sum(jnp.sum(err * err, axis=-1, keepdims=True), axis=0, keepdims=True)
        loss_ref[...] += part * (0.5 / D)
        dy = err * (1.0 / D)
        dg_ref[...] += jnp.sum(dy * xhat, axis=0, keepdims=True)
        dxhat = dy * gam_v
        dx_ref[...] = r * (dxhat - xhat * jnp.mean(dxhat * xhat, axis=-1, keepdims=True))

    return pl.pallas_call(
        body, name="loss_head", grid=(S // tm,),
        in_specs=[pl.BlockSpec((tm, D), lambda i: (i, 0)),
                  pl.BlockSpec((1, D), lambda i: (0, 0)),
                  pl.BlockSpec((tm, D), lambda i: (i, 0))],
        out_specs=[pl.BlockSpec((tm, D), lambda i: (i, 0)),
                   pl.BlockSpec((8, LANES), lambda i: (0, 0)),
                   pl.BlockSpec((1, D), lambda i: (0, 0))],
        out_shape=[jax.ShapeDtypeStruct((S, D), F32),
                   jax.ShapeDtypeStruct((8, LANES), F32),
                   jax.ShapeDtypeStruct((1, D), F32)],
        compiler_params=_params("arbitrary"),
    )(x, gam, target)


def _outproj_bwd(g, wout, mixed, ypool, o, proj):
    S, D = g.shape
    W = D // 2
    tm = min(TM, S)

    def body(g_ref, w_ref, mix_ref, y_ref, o_ref, pg_ref, ag_ref, dw_ref, dwb_ref, da_ref, dgate_ref, doaug_ref):
        @pl.when(pl.program_id(0) == 0)
        def _():
            dw_ref[...] = jnp.zeros(dw_ref.shape, F32)

        gb = g_ref[...].astype(BF16)
        dw_ref[...] += lax.dot_general(mix_ref[...], gb, TN_DIMS, preferred_element_type=F32)
        for half, (val_ref, gate_ref) in enumerate(((y_ref, pg_ref), (o_ref, ag_ref))):
            cols = slice(half * W, (half + 1) * W)
            dmix = lax.dot_general(gb, w_ref[cols, :], NT, preferred_element_type=F32)
            gt = gate_ref[...].astype(F32)
            sg = _sigmoid(gt)
            da_ref[:, cols] = (dmix * (gt * sg)).astype(BF16)
            dgate_ref[:, cols] = (dmix * val_ref[...].astype(F32) * (sg * (1.0 + gt * (1.0 - sg)))).astype(BF16)

        lane, halves = _head_halves(tm)
        slot = lane & (HEAD_DIM - 1)
        for p in range(W // LANES):
            cols = slice(p * LANES, (p + 1) * LANES)
            prod = da_ref[:, W + p * LANES:W + (p + 1) * LANES].astype(F32) * o_ref[:, cols].astype(F32)
            d_a = jnp.sum(jnp.where(halves[0], prod, 0.0), axis=1, keepdims=True)
            d_b = jnp.sum(jnp.where(halves[1], prod, 0.0), axis=1, keepdims=True)
            aug = jnp.zeros((tm, LANES), F32)
            for n, part in enumerate(_split3(jnp.where(halves[0], d_b, d_a))):
                aug = jnp.where(slot == SLOT_C + n, -part, aug)
            doaug_ref[:, cols] = aug.astype(BF16)

        @pl.when(pl.program_id(0) == S // tm - 1)
        def _():
            dwb_ref[...] = dw_ref[...].astype(BF16)

    rows = lambda width, col: pl.BlockSpec((tm, width), lambda i: (i, col))
    whole = pl.BlockSpec((D, D), lambda i: (0, 0))
    return pl.pallas_call(
        body, name="outproj_bwd", grid=(S // tm,),
        in_specs=[rows(D, 0), whole, rows(D, 0), rows(W, 0), rows(W, 0), rows(W, 1), rows(W, 5)],
        out_specs=[whole, whole, rows(D, 0), rows(D, 0), rows(W, 0)],
        out_shape=[jax.ShapeDtypeStruct((D, D), F32),
                   jax.ShapeDtypeStruct((D, D), BF16),
                   jax.ShapeDtypeStruct((S, D), BF16),
                   jax.ShapeDtypeStruct((S, D), BF16),
                   jax.ShapeDtypeStruct((S, W), BF16)],
        compiler_params=_params("arbitrary"),
    )(g, wout, mixed, ypool, o, proj, proj)


def _section_specs(sections, rows, width):
    specs = [pl.BlockSpec((rows, width), lambda k, c=c: (k, c)) for _, c in sections]
    return specs, [a for a, _ in sections]


def _inproj_bwd_dw(h, sections, dzf):
    S, D = h.shape
    W = D // 2
    n_sec = len(sections)
    N = n_sec * W
    ts = min(TM, S)
    n_steps = S // ts

    def body(h_ref, dz_ref, *rest):
        sec_refs, (dw_ref, dwb_ref) = rest[:n_sec], rest[n_sec:]

        @pl.when(pl.program_id(0) == 0)
        def _():
            dw_ref[...] = jnp.zeros(dw_ref.shape, F32)

        ht = h_ref[...].T
        dw_ref[:, N:N + LANES] += jnp.dot(ht, dz_ref[...], preferred_element_type=F32)
        for n, ref in enumerate(sec_refs):
            dw_ref[:, n * W:(n + 1) * W] += jnp.dot(ht, ref[...], preferred_element_type=F32)

        @pl.when(pl.program_id(0) == n_steps - 1)
        def _():
            dwb_ref[...] = dw_ref[...].astype(BF16)

    sec_specs, sec_arrays = _section_specs(sections, ts, W)
    whole = pl.BlockSpec((D, N + LANES), lambda k: (0, 0))
    return pl.pallas_call(
        body, name="inproj_bwd_dw", grid=(n_steps,),
        in_specs=[pl.BlockSpec((ts, D), lambda k: (k, 0)),
                  pl.BlockSpec((ts, LANES), lambda k: (k, 0))] + sec_specs,
        out_specs=[whole, whole],
        out_shape=[jax.ShapeDtypeStruct((D, N + LANES), F32),
                   jax.ShapeDtypeStruct((D, N + LANES), BF16)],
        compiler_params=_params("arbitrary"),
    )(h, dzf, *sec_arrays)


def _attn_bwd(proj, da, qaug, kaug, doaug):
    S = proj.shape[0]
    W = proj.shape[1] // 6
    P = W // LANES
    tq = min(TQ, S)
    nq = S // tq
    qc, kc, vc = 2 * P, 3 * P, 4 * P
    scale = 1.0 / math.sqrt(HEAD_DIM)

    def body(q_ref, k_ref, v_ref, do_ref, qa_ref, ka_ref, da_ref,
             dq_ref, dk_ref, dv_ref, dqx_ref, dkx_ref, km_scr, vm_scr, dk_scr, dv_scr, dq_scr):
        j = pl.program_id(1)
        lane, halves = _head_halves(tq)

        @pl.when(j == 0)
        def _():
            dq_scr[...] = jnp.zeros(dq_scr.shape, F32)

        v_ones = ((lane & (HEAD_DIM - 1)) < 3).astype(BF16)
        for a in range(2):
            km_scr[a] = jnp.where(halves[a], k_ref[...], ka_ref[...])
            vm_scr[a] = jnp.where(halves[a], v_ref[...], v_ones)
        dk_scr[...] = jnp.zeros(dk_scr.shape, F32)
        dv_scr[...] = jnp.zeros(dv_scr.shape, F32)

        def update(i, on_diagonal):
            rows = pl.ds(pl.multiple_of(i * tq, tq), tq)
            qs = q_ref[rows, :] * scale
            do2, qaug_t, doaug_t = do_ref[rows, :], qa_ref[rows, :], da_ref[rows, :]
            if on_diagonal:
                keep = (lax.broadcasted_iota(jnp.int32, (tq, tq), 0)
                        >= lax.broadcasted_iota(jnp.int32, (tq, tq), 1))
            qas = [jnp.where(halves[a], qs, qaug_t) for a in range(2)]
            logits = [lax.dot_general(qas[a], km_scr[a], NT, preferred_element_type=F32) for a in range(2)]
            dps = [lax.dot_general(jnp.where(halves[a], do2, doaug_t), vm_scr[a], NT, preferred_element_type=F32)
                   for a in range(2)]
            dv = None
            for a in range(2):
                s = jnp.where(keep, logits[a], NEG_INF) if on_diagonal else logits[a]
                p = jnp.exp(s)
                dsb = (p * dps[a]).astype(BF16)
                do0 = jnp.where(halves[a], do2, jnp.zeros_like(do2))
                dv_a = lax.dot_general(p.astype(BF16), do0, TN_DIMS, preferred_element_type=F32)
                dv = dv_a if dv is None else dv + dv_a
                dk_scr[a] += lax.dot_general(dsb, qas[a], TN_DIMS, preferred_element_type=F32)
                dq_scr[a, rows, :] += jnp.dot(dsb, km_scr[a], preferred_element_type=F32)
            dv_scr[...] += dv

        def below_diagonal(n, carry):
            update(j + 1 + 2 * n, False)
            update(j + 2 + 2 * n, False)
            return carry

        update(j, True)
        below = nq - 1 - j
        lax.fori_loop(0, below // 2, below_diagonal, 0)

        @pl.when(below % 2 == 1)
        def _():
            update(nq - 1, False)


        dk_ref[...] = jnp.where(halves[0], dk_scr[0], dk_scr[1]).astype(BF16)
        dkx_ref[...] = jnp.where(halves[0], dk_scr[1], dk_scr[0])
        dv_ref[...] = dv_scr[...].astype(BF16)

        @pl.when(j == nq - 1)
        def _():
            row_lane, row_halves = _head_halves(S)
            dq_ref[...] = (jnp.where(row_halves[0], dq_scr[0], dq_scr[1]) * scale).astype(BF16)
            dqx_ref[...] = jnp.where(row_halves[0], dq_scr[1], dq_scr[0])

    tile = lambda col: pl.BlockSpec((tq, LANES), lambda p, j: (j, col + p))
    whole = lambda col: pl.BlockSpec((S, LANES), lambda p, j: (0, col + p))
    return pl.pallas_call(
        body, name="attn_bwd", grid=(P, nq),
        in_specs=[whole(qc), tile(kc), tile(vc), whole(P), whole(0), tile(0), whole(0)],
        out_specs=[whole(0), tile(0), tile(0), whole(0), tile(0)],
        out_shape=[jax.ShapeDtypeStruct((S, W), BF16),
                   jax.ShapeDtypeStruct((S, W), BF16),
                   jax.ShapeDtypeStruct((S, W), BF16),
                   jax.ShapeDtypeStruct((S, W), F32),
                   jax.ShapeDtypeStruct((S, W), F32)],
        scratch_shapes=[pltpu.VMEM((2, tq, LANES), BF16),
                        pltpu.VMEM((2, tq, LANES), BF16),
                        pltpu.VMEM((2, tq, LANES), F32),
                        pltpu.VMEM((tq, LANES), F32),
                        pltpu.VMEM((2, S, LANES), F32)],
        compiler_params=_params("parallel", "arbitrary"),
    )(proj, proj, proj, da, qaug, kaug, doaug)


def _fgate_bwd(dqx, dkx, z, bias, n_heads):
    S = z.shape[0]
    tb = min(TB, S)
    nb = S // tb

    def body(dqx_ref, dkx_ref, z_ref, b_ref, dz_ref, db_ref):
        tri = (lax.broadcasted_iota(jnp.int32, (tb, tb), 1)
               >= lax.broadcasted_iota(jnp.int32, (tb, tb), 0)).astype(F32)
        lane = lax.broadcasted_iota(jnp.int32, (tb, LANES), 1)

        local = []
        for b in range(nb):
            rows = slice(b * tb, (b + 1) * tb)
            dc = jnp.zeros((tb, LANES), F32)
            for p in range(n_heads // 2):
                row_sums = dqx_ref[rows, p * LANES:(p + 1) * LANES]
                col_sums = dkx_ref[rows, p * LANES:(p + 1) * LANES]
                for h, at in ((2 * p, HEAD_DIM), (2 * p + 1, 0)):
                    one = row_sums[:, at + SLOT_C:at + SLOT_C + 1] - col_sums[:, at + SLOT_ONE:at + SLOT_ONE + 1]
                    dc = jnp.where(lane == h, jnp.broadcast_to(one, (tb, LANES)), dc)
            local.append(jnp.dot(tri, dc, preferred_element_type=F32, precision=lax.Precision.HIGHEST))
        carry = jnp.zeros((1, LANES), F32)
        db = jnp.zeros((1, LANES), F32)
        for b in reversed(range(nb)):
            rows = slice(b * tb, (b + 1) * tb)
            rc = local[b] + carry
            carry = rc[0:1, :]
            dz = rc * _sigmoid(-(z_ref[rows, :] + b_ref[...]))
            dz_ref[rows, :] = dz.astype(BF16)
            db = db + jnp.sum(dz, axis=0, keepdims=True)
        db_ref[...] = db

    return pl.pallas_call(
        body, name="fgate_bwd",
        out_shape=[jax.ShapeDtypeStruct((S, LANES), BF16),
                   jax.ShapeDtypeStruct((1, LANES), F32)],
        compiler_params=pltpu.CompilerParams(vmem_limit_bytes=VMEM_LIMIT),
    )(dqx, dkx, z, bias)


def _pool_bwd(proj, da, pool_w, pool_scale):
    S = proj.shape[0]
    G = len(POOL_WINDOWS)

    def body(u_ref, dy_ref, w_ref, s_ref, du_ref, dw_ref, ds_ref, pad_ref):
        g = pl.program_id(0)
        for gi, w in enumerate(POOL_WINDOWS):
            @pl.when(g == gi)
            def _():
                d, cnt = _window_mean_minus_self(u_ref[...].astype(F32), pad_ref, w, S)
                db = d.astype(BF16)
                wb = w_ref[0].astype(BF16)
                yraw = jnp.dot(db, wb, preferred_element_type=F32)
                dy = dy_ref[...].astype(F32)
                ds_ref[...] = jnp.sum(dy * yraw, axis=0, keepdims=True)
                dzb = (dy * s_ref[...]).astype(BF16)
                dw_ref[0] = lax.dot_general(db, dzb, TN_DIMS, preferred_element_type=F32)
                dd = lax.dot_general(dzb, wb, NT, preferred_element_type=F32)
                pad_ref[0:S, :] = dd / cnt
                pad_ref[S:S + MAX_WINDOW, :] = jnp.zeros((MAX_WINDOW, LANES), F32)
                acc = -dd
                for j in range(w):
                    acc = acc + pad_ref[j:j + S, :]
                du_ref[...] = acc.astype(BF16)

    return pl.pallas_call(
        body, name="pool_bwd", grid=(G,),
        in_specs=[pl.BlockSpec((S, LANES), lambda g: (0, g)),
                  pl.BlockSpec((S, LANES), lambda g: (0, g)),
                  pl.BlockSpec((1, LANES, LANES), lambda g: (g, 0, 0)),
                  pl.BlockSpec((1, LANES), lambda g: (0, g))],
        out_specs=[pl.BlockSpec((S, LANES), lambda g: (0, g)),
                   pl.BlockSpec((1, LANES, LANES), lambda g: (g, 0, 0)),
                   pl.BlockSpec((1, LANES), lambda g: (0, g))],
        out_shape=[jax.ShapeDtypeStruct((S, G * LANES), BF16),
                   jax.ShapeDtypeStruct((G, LANES, LANES), F32),
                   jax.ShapeDtypeStruct((1, G * LANES), F32)],
        scratch_shapes=[pltpu.VMEM((S + MAX_WINDOW, LANES), F32)],
        compiler_params=_params("arbitrary"),
    )(proj, da, pool_w, pool_scale)


def _inproj_bwd_dx(sections, dzf, w, x, gam, g, after=()):
    S, D = x.shape
    N = w.shape[1] - LANES
    W = D // 2
    n_sec = len(sections)
    tm = min(TM // 2, S)

    def body(dz_ref, w_ref, wf_ref, x_ref, gam_ref, g_ref, *rest):
        sec_refs = rest[:n_sec]
        dx_ref, dg_ref = rest[-2:]

        @pl.when(pl.program_id(0) == 0)
        def _():
            dg_ref[...] = jnp.zeros(dg_ref.shape, F32)

        dh = lax.dot_general(dz_ref[...], wf_ref[...], NT, preferred_element_type=F32)
        for n, ref in enumerate(sec_refs):
            dh = dh + lax.dot_general(ref[...], w_ref[:, n * W:(n + 1) * W], NT, preferred_element_type=F32)
        xf = x_ref[...]
        r = lax.rsqrt(jnp.mean(xf * xf, axis=-1, keepdims=True) + RMS_EPS)
        xhat = xf * r
        dg_ref[...] += jnp.sum(dh * xhat, axis=0, keepdims=True)
        dxhat = dh * gam_ref[...]
        dx_ref[...] = g_ref[...] + r * (dxhat - xhat * jnp.mean(dxhat * xhat, axis=-1, keepdims=True))

    sec_specs, sec_arrays = _section_specs(sections, tm, W)
    return pl.pallas_call(
        body, name="inproj_bwd_dx", grid=(S // tm,),
        in_specs=[pl.BlockSpec((tm, LANES), lambda i: (i, 0)),
                  pl.BlockSpec((D, N), lambda i: (0, 0)),
                  pl.BlockSpec((D, LANES), lambda i: (0, N // LANES)),
                  pl.BlockSpec((tm, D), lambda i: (i, 0)),
                  pl.BlockSpec((1, D), lambda i: (0, 0)),
                  pl.BlockSpec((tm, D), lambda i: (i, 0))] + sec_specs + _after_specs(after),
        out_specs=[pl.BlockSpec((tm, D), lambda i: (i, 0)),
                   pl.BlockSpec((1, D), lambda i: (0, 0))],
        out_shape=[jax.ShapeDtypeStruct((S, D), F32),
                   jax.ShapeDtypeStruct((1, D), F32)],
        compiler_params=_params("arbitrary"),
    )(dzf, w, w, x, gam, g, *sec_arrays, *after)


def _adamw(w, m, v, gsets, name, rows, shifted=False, first=0, into=None):
    A, R, C = w.shape
    n_sets = len(gsets)
    tr = min(rows, R)
    c1 = 1.0 / (1.0 - ADAM_B1 ** ADAM_STEP)
    c2 = 1.0 / (1.0 - ADAM_B2 ** ADAM_STEP)
    counts = [len(gs) for gs in gsets]

    def body(w_ref, m_ref, v_ref, *rest):
        g_ref, d_ref, nm_ref, nv_ref = rest[-4:]
        at = 0
        for a in range(n_sets):
            part_refs = rest[at:at + counts[a]]
            at += counts[a]

            @pl.when(pl.program_id(0) == a)
            def _():
                g = None
                for ref in part_refs:
                    for s in range(ref.shape[0]):
                        term = ref[s].astype(F32)
                        g = term if g is None else g + term
                if shifted:
                    lanes = g.shape[1]
                    g = pltpu.roll(g, (lanes - _index(_position()) * (C % LANES)) % lanes, axis=1)[:, :C]
                nm = ADAM_B1 * m_ref[0] + (1.0 - ADAM_B1) * g
                nv = ADAM_B2 * v_ref[0] + (1.0 - ADAM_B2) * (g * g)
                g_ref[0] = g
                nm_ref[0] = nm
                nv_ref[0] = nv
                d_ref[0] = -ADAM_LR * ((nm * c1) / (jnp.sqrt(nv * c2) + ADAM_EPS) + ADAM_WD * w_ref[0])

    spec = pl.BlockSpec((1, tr, C), lambda a, r: (first + a, r, 0))
    part_specs = [pl.BlockSpec((part.shape[0], tr, part.shape[2]), lambda a, r, l=l: (0, jnp.where(a == l, r, 0), 0))
                  for l, gs in enumerate(gsets) for part in gs]
    parts = [part for gs in gsets for part in gs]
    shape = jax.ShapeDtypeStruct((A, R, C), F32)
    earlier = () if into is None else tuple(into)
    return pl.pallas_call(
        body, name=name, grid=(n_sets, R // tr),
        in_specs=[spec, spec, spec] + part_specs + _after_specs(earlier),
        out_specs=[spec, spec, spec, spec],
        out_shape=[shape, shape, shape, shape],
        input_output_aliases={3 + len(parts) + n: n for n in range(len(earlier))},
        compiler_params=_params("arbitrary", "arbitrary"),
    )(w, m, v, *parts, *earlier)


def _position():
    return lax.axis_index("x"), lax.axis_index("y"), lax.axis_index("c")


def _index(dev):
    return 4 * dev[0] + 2 * dev[1] + dev[2]


def _all_gather(arrs, slots, out_shapes, name):
    n_arr = len(arrs)

    def body(*refs):
        ins, outs = refs[:n_arr], refs[n_arr:2 * n_arr]
        send_sems, recv_sems, local_sems = refs[2 * n_arr:]
        x, y, c = _position()
        me, sibling = (x, y, c), (x, y, 1 - c)
        chips = [(1 - x, y), (x, 1 - y), (1 - x, 1 - y)]

        def copy(a, k, block, to, src=None):
            part = slots[a](outs[a], _index(block))
            return pltpu.make_async_remote_copy(
                src_ref=part if src is None else src, dst_ref=part,
                send_sem=send_sems.at[a, k], recv_sem=recv_sems.at[a, k],
                device_id=to, device_id_type=MESH)

        mine = [pltpu.make_async_copy(ins[a], slots[a](outs[a], _index(me)), local_sems.at[a])
                for a in range(n_arr)]
        for cp in mine:
            cp.start()
        first = []
        for a in range(n_arr):
            first.append(copy(a, 0, me, sibling, src=ins[a]))
            first += [copy(a, 1 + j, me, (*chip, c), src=ins[a]) for j, chip in enumerate(chips)]
        for cp in first:
            cp.start()
        passed = []
        for j, chip in enumerate(chips):
            for a in range(n_arr):
                copy(a, 1 + j, (*chip, c), me).wait_recv()
                fwd = copy(a, 4 + j, (*chip, c), sibling)
                fwd.start()
                passed.append(fwd)
        for a in range(n_arr):
            copy(a, 0, sibling, me).wait_recv()
            for j, chip in enumerate(chips):
                copy(a, 4 + j, (*chip, 1 - c), me).wait_recv()
        for cp in first + passed:
            cp.wait_send()
        for cp in mine:
            cp.wait()

    any_spec = pl.BlockSpec(memory_space=pl.ANY)
    return pl.pallas_call(
        body, name=name,
        in_specs=[any_spec] * n_arr, out_specs=[any_spec] * n_arr, out_shape=out_shapes,
        scratch_shapes=[pltpu.SemaphoreType.DMA((n_arr, 7)), pltpu.SemaphoreType.DMA((n_arr, 7)),
                        pltpu.SemaphoreType.DMA((n_arr,))],
    )(*arrs)


def _split_copies(srcs, lands, send_sems, recv_sems, kinds):
    x, y, c = _position()
    me = _index((x, y, c))
    copies = []
    for a, (src_part, land_part) in enumerate(kinds):
        for k in range(1, N_DEV):
            peer = (x ^ ((k >> 2) & 1), y ^ ((k >> 1) & 1), c ^ (k & 1))
            copies.append(pltpu.make_async_remote_copy(
                src_ref=src_part(srcs[a], _index(peer)), dst_ref=land_part(lands[a], me, k),
                send_sem=send_sems[a].at[k - 1], recv_sem=recv_sems[a].at[k - 1],
                device_id=peer, device_id_type=MESH))
    return copies


def _split_start(srcs, lands, kinds, name, after=()):
    n = len(srcs)

    def body(*refs):
        src_refs, land_refs = refs[:n], refs[n:2 * n]
        outs = refs[2 * n + len(after):]
        send_sems, recv_sems = outs[:n], outs[n:2 * n]
        token = outs[-1]
        for cp in _split_copies(src_refs, land_refs, send_sems, recv_sems, kinds):
            cp.start()
        token[...] = jnp.zeros(token.shape, token.dtype)

    hbm = pl.BlockSpec(memory_space=pltpu.HBM)
    sem = pl.BlockSpec(memory_space=pltpu.SEMAPHORE)
    operands = [pltpu.with_memory_space_constraint(t, pltpu.HBM) for t in (*srcs, *lands)]
    out = pl.pallas_call(
        body, name=name,
        in_specs=[hbm] * (2 * n) + _after_specs(after),
        out_specs=[sem] * (2 * n) + [hbm] * (2 * n) + [pl.BlockSpec(memory_space=pltpu.VMEM)],
        out_shape=[pltpu.SemaphoreType.DMA((N_DEV - 1,))] * (2 * n)
        + [pltpu.HBM(t.shape, t.dtype) for t in operands] + [jax.ShapeDtypeStruct((8, LANES), F32)],
        input_output_aliases={i: 2 * n + i for i in range(2 * n)},
        compiler_params=pltpu.CompilerParams(has_side_effects=pltpu.SideEffectType.DATAFLOW_SIDE_EFFECTING),
    )(*operands, *after)
    return [(out[a], out[n + a], out[2 * n + a], out[3 * n + a]) for a in range(n)], out[-1]


def _split_wait(started, kinds, after, name):
    n = len(started)
    sems = [t[0] for t in started] + [t[1] for t in started]
    srcs = [t[2] for t in started]
    lands = [t[3] for t in started]

    def body(*refs):
        src_refs, land_refs = refs[:n], refs[n:2 * n]
        send_sems, recv_sems = refs[2 * n:3 * n], refs[3 * n:4 * n]
        for cp in _split_copies(src_refs, land_refs, send_sems, recv_sems, kinds):
            cp.wait_send()
            cp.wait_recv()

    hbm = pl.BlockSpec(memory_space=pltpu.HBM)
    sem = pl.BlockSpec(memory_space=pltpu.SEMAPHORE)
    out = pl.pallas_call(
        body, name=name,
        in_specs=[hbm] * (2 * n) + [sem] * (2 * n) + _after_specs(after),
        out_specs=[hbm] * (2 * n),
        out_shape=[pltpu.HBM(t.shape, t.dtype) for t in (*srcs, *lands)],
        input_output_aliases={i: i for i in range(2 * n)},
        compiler_params=pltpu.CompilerParams(has_side_effects=pltpu.SideEffectType.DATAFLOW_SIDE_EFFECTING),
    )(*srcs, *lands, *sems, *after)
    return out[n:]


def _as_rows(p):
    if p.size % LANES == 0:
        rows = p.reshape(-1, LANES)
    else:
        rows = p.reshape(-1, p.shape[-1])
        rows = jnp.pad(rows, ((0, 0), (0, LANES - rows.shape[1])))
    return jnp.pad(rows, ((0, -rows.shape[0] % 8), (0, 0)))


def _pack(parts):
    return jnp.concatenate([_as_rows(p) for p in parts])[None]


def _unpack(packed, like):
    out, at = [], 0
    for p in like:
        whole = p.size % LANES == 0
        n = p.size // LANES if whole else p.size // p.shape[-1]
        rows = packed[0, at:at + n]
        out.append((rows if whole else rows[:, :p.shape[-1]]).reshape(p.shape))
        at += n + (-n % 8)
    return out


def _local_step(x, target, norm_g, forget_bias, pool_w, pool_scale, final_g, weights_in, weights_out, on_grads,
                first_after=()):
    L = norm_g.shape[0]
    S, D = x.shape
    W = D // 2
    H = W // HEAD_DIM
    bias = jnp.pad(forget_bias, ((0, 0), (0, LANES - H)))

    saved = []
    after = tuple(first_after)
    for l in range(L):
        proj, h, z, w = _inproj_fwd(x, norm_g[l:l + 1], weights_in(l, x), after)
        after = ()
        qaug, kaug = _fgate_fwd(z, bias[l:l + 1], H)
        ypool = _pool_fwd(proj, pool_w[l], pool_scale[l:l + 1])
        o, qaug_b = _attn_fwd(proj, qaug, kaug)
        wout = weights_out(l, o)
        x_new, mixed = _outproj_fwd(ypool, o, proj, x, wout)
        saved.append((x, proj, h, z, qaug_b, kaug, ypool, o, mixed, w, wout))
        x = x_new

    g, loss, d_final_g = _loss_head(x, final_g.reshape(1, D), target)

    small = None
    for l in reversed(range(L)):
        x_in, proj, h, z, qaug_b, kaug, ypool, o, mixed, w, wout = saved[l]
        d_wout, d_wout_bf16, da, dgate, doaug = _outproj_bwd(g, wout, mixed, ypool, o, proj)
        dq, dk, dv, dqx, dkx = _attn_bwd(proj, da, qaug_b, kaug, doaug)
        dzf, db = _fgate_bwd(dqx, dkx, z, bias[l:l + 1], H)
        dpu, dpw, dps = _pool_bwd(proj, da, pool_w[l], pool_scale[l:l + 1])
        dproj = [(dpu, 0), (dgate, 0), (dq, 0), (dk, 0), (dv, 0), (dgate, 1)]
        d_w, d_w_bf16 = _inproj_bwd_dw(h, dproj, dzf)
        after = tuple(on_grads(l, d_w, d_w_bf16, d_wout, d_wout_bf16, small))
        g, dgam = _inproj_bwd_dx(dproj, dzf, w, x_in, norm_g[l:l + 1], g, after)
        small = (dgam[0], db[0, :H], dpw, dps[0])
    return loss[0, 0], g, small, d_final_g[0]


def kernel(x, norm_g, w_in, forget_bias, pool_w, pool_scale, w_out, final_g, loss_target, m_norm_g, m_w_in, m_forget_bias, m_pool_w, m_pool_scale, m_w_out, m_final_g, v_norm_g, v_w_in, v_forget_bias, v_pool_w, v_pool_scale, v_w_out, v_final_g):
    L, D, cols = w_in.shape
    rows_out = w_out.shape[1]
    W = D // 2
    H = W // HEAD_DIM
    me = _index(_position())

    slot = _slot_width(cols)
    wout_b = w_out.astype(BF16)
    win_b = _shift_slots(w_in)
    gather_in = (lambda ref, peer: ref, lambda ref, mine, k: ref.at[mine])
    gather_out = (lambda ref, peer: ref, lambda ref, mine, k: ref.at[pl.ds(mine * rows_out, rows_out), :])

    def landing(block, n_slots):
        zone = lax.empty((n_slots * block.shape[0], *block.shape[1:]), block.dtype)
        return lax.dynamic_update_slice(zone, block, (me * block.shape[0],) + (0,) * (block.ndim - 1))

    (first_in,) = _all_gather([win_b[0]], [lambda ref, n: ref.at[n]],
                              [jax.ShapeDtypeStruct((N_DEV, D, slot), BF16)], "gather_first")
    rest_srcs = [wout_b[0]] + [w[l] for l in range(1, L) for w in (win_b, wout_b)]
    rest_lands = [landing(wout_b[0], N_DEV)]
    for l in range(1, L):
        rest_lands += [landing(win_b[l][None], N_DEV), landing(wout_b[l], N_DEV)]
    rest_kinds = [gather_out] + [gather_in, gather_out] * (L - 1)
    rest, rest_token = _split_start(rest_srcs, rest_lands, rest_kinds, "gather_start_rest", (first_in,))

    def weights_in(l, x_in):
        if l == 0:
            return first_in
        (win_all,) = _split_wait([rest[2 * l - 1]], [gather_in], (x_in,), f"gather_wait_in_{l}")
        return win_all

    def weights_out(l, o):
        (wout_full,) = _split_wait([rest[2 * l]], [gather_out], (o,), f"gather_wait_out_{l}")
        return wout_full

    stride = slot - LANES
    exchange_kinds = [(lambda ref, peer: ref.at[:, pl.ds(pl.multiple_of(peer * stride, LANES), slot)],
                       lambda ref, mine, k: ref.at[k - 1]),
                      (lambda ref, peer: ref.at[pl.ds(peer * rows_out, rows_out), :],
                       lambda ref, mine, k: ref.at[k - 1])]
    zero_g = jnp.zeros_like(final_g)
    zero_loss = jnp.zeros((LANES,), F32)

    def small_pack(l, norm_g_l, bias_l, pool_w_l, pool_scale_l, final, loss_row=None):
        return _pack([norm_g_l, bias_l, pool_w_l, pool_scale_l, final if l == 0 else zero_g,
                      zero_loss if loss_row is None else loss_row])[0]

    exchanges, own_parts = {}, {}

    def on_grads(l, dw, dw_bf16, d_wout, d_wout_bf16, small):
        own_parts[l] = (lax.dynamic_slice_in_dim(dw, me * stride, slot, 1)[None],
                        lax.dynamic_slice_in_dim(d_wout, me * rows_out, rows_out, 0)[None])
        srcs = [dw_bf16, d_wout_bf16]
        lands = [lax.empty((N_DEV - 1, D, slot), BF16), lax.empty((N_DEV - 1, rows_out, D), BF16)]
        kinds = list(exchange_kinds)
        if small is not None:
            packed_small = small_pack(l + 1, *small, None)
            srcs.append(packed_small)
            lands.append(landing(packed_small[None], N_DEV))
            kinds.append(gather_in)
        started, token = _split_start(srcs, lands, kinds, f"exchange_start_{l}")
        exchanges[l] = (started, kinds)
        return (token,)

    loss, dx, small_first, d_final_g = _local_step(
        x[0], loss_target[0], norm_g, forget_bias, pool_w, pool_scale, final_g,
        weights_in, weights_out, on_grads, (rest_token,))
    packed_first = small_pack(0, *small_first, d_final_g, jnp.full((LANES,), loss, F32))
    first_started, first_token = _split_start(
        [packed_first], [landing(packed_first[None], N_DEV)], [gather_in],
        "small_start", (w_in, m_w_in, v_w_in, *own_parts[0]))

    gin_sets, gout_sets, small_sets = [None] * L, [None] * L, [None] * L

    def wait_for(l, after):
        started, kinds = exchanges[l]
        got = _split_wait(started, kinds, after, f"exchange_wait_{l}")
        gin_sets[l] = [own_parts[l][0], got[0]]
        gout_sets[l] = [own_parts[l][1], got[1]]
        if len(got) > 2:
            small_sets[l + 1] = [got[2]]

    for l in range(1, L):
        wait_for(l, (dx, first_token))
    rest_in = _adamw(w_in, m_w_in, v_w_in, gin_sets[1:], "adamw_w_in_rest", 256, shifted=True, first=1)
    rest_out = _adamw(w_out, m_w_out, v_w_out, gout_sets[1:], "adamw_w_out_rest", 128, first=1)
    wait_for(0, (rest_in[1], rest_out[1]))
    g_w_in, d_w_in, nm_w_in, nv_w_in = _adamw(w_in, m_w_in, v_w_in, gin_sets[:1], "adamw_w_in_first", 256,
                                              shifted=True, into=rest_in)
    g_w_out, d_w_out, nm_w_out, nv_w_out = _adamw(w_out, m_w_out, v_w_out, gout_sets[:1], "adamw_w_out_first", 128,
                                                  into=rest_out)
    small_sets[0] = _split_wait(first_started, [gather_in], (d_w_in, d_w_out), "small_wait")
    loss = jnp.sum(small_sets[0][0][:, packed_first.shape[0] - 8, 0])

    def small_stack(norm_g_, bias_, pool_w_, pool_scale_, final):
        return jnp.stack([small_pack(l, norm_g_[l], bias_[l], pool_w_[l], pool_scale_[l], final) for l in range(L)])

    packed = _adamw(small_stack(norm_g, forget_bias, pool_w, pool_scale, final_g),
                    small_stack(m_norm_g, m_forget_bias, m_pool_w, m_pool_scale, m_final_g),
                    small_stack(v_norm_g, v_forget_bias, v_pool_w, v_pool_scale, v_final_g),
                    small_sets, "adamw_small", packed_first.shape[0])

    def small_unpack(p):
        like = [norm_g[0], forget_bias[0], pool_w[0], pool_scale[0], final_g]
        layers = [_unpack(p[l:l + 1], like) for l in range(L)]
        return [jnp.stack([layers[l][n] for l in range(L)]) for n in range(4)] + [layers[0][4]]

    g_s, d_s, nm_s, nv_s = [small_unpack(p) for p in packed]

    def order(big_in, big_out, small):
        return (small[0], big_in, small[1], small[2], small[3], big_out, small[4])

    return (loss, dx[None], *order(g_w_in, g_w_out, g_s), *order(d_w_in, d_w_out, d_s),
            *order(nm_w_in, nm_w_out, nm_s), *order(nv_w_in, nv_w_out, nv_s))
```

```python
import math

import jax
import jax.numpy as jnp
from jax import lax
from jax.experimental import pallas as pl
from jax.experimental.pallas import tpu as pltpu

F32 = jnp.float32
BF16 = jnp.bfloat16
MESH = pl.DeviceIdType.MESH

RMS_EPS = 1e-6
NEG_INF = -1e30
HEAD_DIM = 64
POOL_WINDOWS = (2, 4, 8, 16)
MAX_WINDOW = 16
LANES = 128
N_DEV = 8

ADAM_LR = 0.001
ADAM_B1 = 0.9
ADAM_B2 = 0.999
ADAM_EPS = 1e-08
ADAM_WD = 0.01
ADAM_STEP = 10

TM = 512
TN = 512
TQ = 512
TB = 256
VMEM_LIMIT = 56 * 1024 * 1024

NT = (((1,), (1,)), ((), ()))
TN_DIMS = (((0,), (0,)), ((), ()))

SLOT_C, SLOT_ONE, SLOT_LSE = 0, 3, 6


def _params(*sem):
    return pltpu.CompilerParams(dimension_semantics=sem, vmem_limit_bytes=VMEM_LIMIT)


def _sigmoid(x):
    return 1.0 / (1.0 + jnp.exp(-x))


def _split3(x):
    hi = x.astype(BF16).astype(F32)
    rest = x - hi
    mid = rest.astype(BF16).astype(F32)
    return hi, mid, rest - mid


def _after_specs(after):
    return [pl.BlockSpec(memory_space=pl.ANY)] * len(after)


def _slot_width(cols):
    return LANES * (-(-(cols + (N_DEV - 1) * (cols % LANES)) // LANES))


def _shift_slots(w_in):
    L, D, cols = w_in.shape
    slot = _slot_width(cols)
    tr = min(TM // 2, D)

    def body(w_ref, o_ref, pad_scr):
        pad_scr[...] = jnp.zeros(pad_scr.shape, F32)
        pad_scr[:, 0:cols] = w_ref[0]
        o_ref[0] = pltpu.roll(pad_scr[...], _index(_position()) * (cols % LANES), axis=1).astype(BF16)

    return pl.pallas_call(
        body, name="shift_slots", grid=(L, D // tr),
        in_specs=[pl.BlockSpec((1, tr, cols), lambda l, r: (l, r, 0))],
        out_specs=pl.BlockSpec((1, tr, slot), lambda l, r: (l, r, 0)),
        out_shape=jax.ShapeDtypeStruct((L, D, slot), BF16),
        scratch_shapes=[pltpu.VMEM((tr, slot), F32)],
        compiler_params=_params("parallel", "parallel"),
    )(w_in)


def _inproj_fwd(x, gam, slots, after=()):
    S, D = x.shape
    n_dev, _, sw = slots.shape
    stride = sw - LANES
    width = stride * n_dev + LANES
    N = width - LANES
    tm, tn = min(TM, S), TN

    def body(x_ref, g_ref, s_ref, *rest):
        proj_ref, h_ref, z_ref, w_ref = rest[-4:]

        @pl.when(pl.program_id(0) == 0)
        def _():
            for n in range(n_dev):
                base = stride * n
                first = s_ref[n, :, 0:LANES]
                if n > 0:
                    first = first + s_ref[n - 1, :, stride:sw]
                w_ref[:, base:base + LANES] = first
                w_ref[:, base + LANES:base + stride] = s_ref[n, :, LANES:stride]
            w_ref[:, stride * n_dev:width] = s_ref[n_dev - 1, :, stride:sw]

        xf = x_ref[...]
        r = lax.rsqrt(jnp.mean(xf * xf, axis=-1, keepdims=True) + RMS_EPS)
        h = ((xf * r) * g_ref[...]).astype(BF16)
        h_ref[...] = h
        z_ref[...] = jnp.dot(h, w_ref[:, N:width], preferred_element_type=F32)
        for n in range(N // tn):
            cols = slice(n * tn, (n + 1) * tn)
            proj_ref[:, cols] = jnp.dot(h, w_ref[:, cols], preferred_element_type=F32).astype(BF16)

    return pl.pallas_call(
        body, name="inproj_fwd", grid=(S // tm,),
        in_specs=[pl.BlockSpec((tm, D), lambda i: (i, 0)),
                  pl.BlockSpec((1, D), lambda i: (0, 0)),
                  pl.BlockSpec((n_dev, D, sw), lambda i: (0, 0, 0))] + _after_specs(after),
        out_specs=[pl.BlockSpec((tm, N), lambda i: (i, 0)),
                   pl.BlockSpec((tm, D), lambda i: (i, 0)),
                   pl.BlockSpec((tm, LANES), lambda i: (i, 0)),
                   pl.BlockSpec((D, width), lambda i: (0, 0))],
        out_shape=[jax.ShapeDtypeStruct((S, N), BF16),
                   jax.ShapeDtypeStruct((S, D), BF16),
                   jax.ShapeDtypeStruct((S, LANES), F32),
                   jax.ShapeDtypeStruct((D, width), BF16)],
        compiler_params=_params("arbitrary"),
    )(x, gam, slots, *after)


def _fgate_fwd(z, bias, n_heads):
    S = z.shape[0]
    tb = min(TB, S)
    P = n_heads // 2

    def body(z_ref, b_ref, qaug_ref, kaug_ref):
        lane = lax.broadcasted_iota(jnp.int32, (tb, LANES), 1)
        tri = (lax.broadcasted_iota(jnp.int32, (tb, tb), 0)
               >= lax.broadcasted_iota(jnp.int32, (tb, tb), 1)).astype(F32)
        head = lax.broadcasted_iota(jnp.int32, (LANES, P * LANES), 0)
        col = lax.broadcasted_iota(jnp.int32, (LANES, P * LANES), 1)
        home = (head >> 1) * LANES + jnp.where((head & 1) == 0, HEAD_DIM, 0)
        is_head = head < n_heads
        place_q = [jnp.logical_and(is_head, col == home + SLOT_C + n).astype(BF16) for n in range(3)]
        place_k = [jnp.logical_and(is_head, col == home + SLOT_ONE + n).astype(BF16) for n in range(3)]
        slot = lax.broadcasted_iota(jnp.int32, (tb, P * LANES), 1) & (HEAD_DIM - 1)
        q_ones = jnp.logical_and(slot >= SLOT_ONE, slot < SLOT_ONE + 3).astype(F32)
        k_ones = jnp.logical_or(slot < SLOT_C + 3,
                                jnp.logical_and(slot >= SLOT_LSE, slot < SLOT_LSE + 3)).astype(F32)

        local = []
        for b in range(S // tb):
            zz = z_ref[b * tb:(b + 1) * tb, :] + b_ref[...]
            lf = jnp.minimum(zz, 0.0) - jnp.log(1.0 + jnp.exp(-jnp.abs(zz)))
            lf = jnp.where(lane < n_heads, lf, 0.0)
            local.append(jnp.dot(tri, lf, preferred_element_type=F32, precision=lax.Precision.HIGHEST))
        carry = jnp.zeros((1, LANES), F32)
        for b, part_sum in enumerate(local):
            c = part_sum + carry
            carry = c[tb - 1:tb, :]
            qa, ka = q_ones, k_ones
            for n, part in enumerate(_split3(c)):
                qa = qa + jnp.dot(part.astype(BF16), place_q[n], preferred_element_type=F32)
                ka = ka - jnp.dot(part.astype(BF16), place_k[n], preferred_element_type=F32)
            qaug_ref[b * tb:(b + 1) * tb, :] = qa.astype(BF16)
            kaug_ref[b * tb:(b + 1) * tb, :] = ka.astype(BF16)

    return pl.pallas_call(
        body, name="fgate_fwd",
        out_shape=[jax.ShapeDtypeStruct((S, P * LANES), BF16),
                   jax.ShapeDtypeStruct((S, P * LANES), BF16)],
        compiler_params=pltpu.CompilerParams(vmem_limit_bytes=VMEM_LIMIT),
    )(z, bias)


def _window_mean_minus_self(u, pad_ref, w, S):
    pad_ref[0:MAX_WINDOW, :] = jnp.zeros((MAX_WINDOW, LANES), F32)
    pad_ref[MAX_WINDOW:MAX_WINDOW + S, :] = u
    acc = u
    for j in range(1, w):
        acc = acc + pad_ref[MAX_WINDOW - j:MAX_WINDOW - j + S, :]
    t = lax.broadcasted_iota(jnp.int32, (S, LANES), 0)
    cnt = jnp.minimum(t + 1, w).astype(F32)
    return acc / cnt - u, cnt


def _pool_fwd(proj, pool_w, pool_scale):
    S = proj.shape[0]
    G = len(POOL_WINDOWS)

    def body(u_ref, w_ref, s_ref, y_ref, pad_ref):
        g = pl.program_id(0)
        for gi, w in enumerate(POOL_WINDOWS):
            @pl.when(g == gi)
            def _():
                d, _ = _window_mean_minus_self(u_ref[...].astype(F32), pad_ref, w, S)
                y = jnp.dot(d.astype(BF16), w_ref[0].astype(BF16), preferred_element_type=F32)
                y_ref[...] = (y * s_ref[...]).astype(BF16)

    return pl.pallas_call(
        body, name="pool_fwd", grid=(G,),
        in_specs=[pl.BlockSpec((S, LANES), lambda g: (0, g)),
                  pl.BlockSpec((1, LANES, LANES), lambda g: (g, 0, 0)),
                  pl.BlockSpec((1, LANES), lambda g: (0, g))],
        out_specs=pl.BlockSpec((S, LANES), lambda g: (0, g)),
        out_shape=jax.ShapeDtypeStruct((S, G * LANES), BF16),
        scratch_shapes=[pltpu.VMEM((S + MAX_WINDOW, LANES), F32)],
        compiler_params=_params("arbitrary"),
    )(proj, pool_w, pool_scale)


def _head_halves(rows):
    lane = lax.broadcasted_iota(jnp.int32, (rows, LANES), 1)
    return lane, (lane < HEAD_DIM, lane >= HEAD_DIM)


def _attn_fwd(proj, qaug, kaug):
    S = proj.shape[0]
    W = proj.shape[1] // 6
    P = W // LANES
    tq = min(TQ, S)
    nq = S // tq
    qc, kc, vc = 2 * P, 3 * P, 4 * P
    scale = 1.0 / math.sqrt(HEAD_DIM)

    def body(q_ref, k_ref, v_ref, qa_ref, ka_ref, o_ref, qb_ref, qm_scr, m_scr, acc_scr):
        i = pl.program_id(1)
        lane, halves = _head_halves(tq)
        v_ones = ((lane & (HEAD_DIM - 1)) < 3).astype(BF16)
        qs = q_ref[...] * scale
        qm_scr[0] = jnp.where(halves[0], qs, qa_ref[...])
        qm_scr[1] = jnp.where(halves[1], qs, qa_ref[...])
        m_scr[...] = jnp.full(m_scr.shape, NEG_INF, F32)
        acc_scr[...] = jnp.zeros(acc_scr.shape, F32)

        def update(j, on_diagonal):
            keys = pl.ds(pl.multiple_of(j * tq, tq), tq)
            k2, v2, kaug_t = k_ref[keys, :], v_ref[keys, :], ka_ref[keys, :]
            if on_diagonal:
                keep = (lax.broadcasted_iota(jnp.int32, (tq, tq), 0)
                        >= lax.broadcasted_iota(jnp.int32, (tq, tq), 1))
            logits = [lax.dot_general(qm_scr[a], jnp.where(halves[a], k2, kaug_t), NT, preferred_element_type=F32)
                      for a in range(2)]
            for a in range(2):
                s = jnp.where(keep, logits[a], NEG_INF) if on_diagonal else logits[a]
                va = jnp.where(halves[a], v2, v_ones)
                m_prev = m_scr[a]
                m_new = jnp.maximum(m_prev, jnp.max(s, axis=1, keepdims=True))
                p = jnp.exp(s - jnp.tile(m_new, (1, tq // LANES)))
                acc_scr[a] = jnp.exp(m_prev - m_new) * acc_scr[a] + jnp.dot(p.astype(BF16), va,
                                                                              preferred_element_type=F32)
                m_scr[a] = m_new

        def below_diagonal(jj, carry):
            update(2 * jj, False)
            update(2 * jj + 1, False)
            return carry

        lax.fori_loop(0, i // 2, below_diagonal, 0)

        @pl.when(i % 2 == 1)
        def _():
            update(i - 1, False)

        update(i, True)
        acc_a, acc_b = acc_scr[0], acc_scr[1]
        l_a, l_b = acc_a[:, HEAD_DIM:HEAD_DIM + 1], acc_b[:, 0:1]
        o_ref[...] = jnp.where(halves[0], acc_a / l_a, acc_b / l_b).astype(BF16)
        lse = jnp.where(halves[0], m_scr[1] + jnp.log(l_b), m_scr[0] + jnp.log(l_a))
        slot = lane & (HEAD_DIM - 1)
        aug = qa_ref[...].astype(F32)
        for n, part in enumerate(_split3(lse)):
            aug = jnp.where(slot == SLOT_LSE + n, -part, aug)
        qb_ref[...] = aug.astype(BF16)

    tile = lambda col: pl.BlockSpec((tq, LANES), lambda p, i: (i, col + p))
    whole = lambda col: pl.BlockSpec((S, LANES), lambda p, i: (0, col + p))
    return pl.pallas_call(
        body, name="attn_fwd", grid=(P, nq),
        in_specs=[tile(qc), whole(kc), whole(vc), tile(0), whole(0)],
        out_specs=[tile(0), tile(0)],
        out_shape=[jax.ShapeDtypeStruct((S, W), BF16), jax.ShapeDtypeStruct((S, W), BF16)],
        scratch_shapes=[pltpu.VMEM((2, tq, LANES), BF16),
                        pltpu.VMEM((2, tq, LANES), F32),
                        pltpu.VMEM((2, tq, LANES), F32)],
        compiler_params=_params("parallel", "arbitrary"),
    )(proj, proj, proj, qaug, kaug)


def _outproj_fwd(ypool, o, proj, x, wout):
    S, D = x.shape
    W = D // 2
    tm, tn = min(TM, S), TN

    def body(y_ref, o_ref, pg_ref, ag_ref, x_ref, w_ref, xn_ref, mix_ref):
        pg, ag = pg_ref[...].astype(F32), ag_ref[...].astype(F32)
        mix_ref[:, 0:W] = (y_ref[...].astype(F32) * (pg * _sigmoid(pg))).astype(BF16)
        mix_ref[:, W:D] = (o_ref[...].astype(F32) * (ag * _sigmoid(ag))).astype(BF16)
        for n in range(D // tn):
            cols = slice(n * tn, (n + 1) * tn)
            xn_ref[:, cols] = x_ref[:, cols] + jnp.dot(mix_ref[...], w_ref[:, cols], preferred_element_type=F32)

    return pl.pallas_call(
        body, name="outproj_fwd", grid=(S // tm,),
        in_specs=[pl.BlockSpec((tm, W), lambda i: (i, 0)),
                  pl.BlockSpec((tm, W), lambda i: (i, 0)),
                  pl.BlockSpec((tm, W), lambda i: (i, 1)),
                  pl.BlockSpec((tm, W), lambda i: (i, 5)),
                  pl.BlockSpec((tm, D), lambda i: (i, 0)),
                  pl.BlockSpec((D, D), lambda i: (0, 0))],
        out_specs=[pl.BlockSpec((tm, D), lambda i: (i, 0)),
                   pl.BlockSpec((tm, D), lambda i: (i, 0))],
        out_shape=[jax.ShapeDtypeStruct((S, D), F32),
                   jax.ShapeDtypeStruct((S, D), BF16)],
        compiler_params=_params("parallel"),
    )(ypool, o, proj, proj, x, wout)


def _loss_head(x, gam, target):
    S, D = x.shape
    tm = min(TM, S)

    def body(x_ref, g_ref, t_ref, dx_ref, loss_ref, dg_ref):
        @pl.when(pl.program_id(0) == 0)
        def _():
            loss_ref[...] = jnp.zeros(loss_ref.shape, F32)
            dg_ref[...] = jnp.zeros(dg_ref.shape, F32)

        xf, gam_v = x_ref[...], g_ref[...]
        r = lax.rsqrt(jnp.mean(xf * xf, axis=-1, keepdims=True) + RMS_EPS)
        xhat = xf * r
        err = xhat * gam_v - t_ref[...]
        part = jnp.sum(jnp.sum(err * err, axis=-1, keepdims=True), axis=0, keepdims=True)
        loss_ref[...] += part * (0.5 / D)
        dy = err * (1.0 / D)
        dg_ref[...] += jnp.sum(dy * xhat, axis=0, keepdims=True)
        dxhat = dy * gam_v
        dx_ref[...] = r * (dxhat - xhat * jnp.mean(dxhat * xhat, axis=-1, keepdims=True))

    return pl.pallas_call(
        body, name="loss_head", grid=(S // tm,),
        in_specs=[pl.BlockSpec((tm, D), lambda i: (i, 0)),
                  pl.BlockSpec((1, D), lambda i: (0, 0)),
                  pl.BlockSpec((tm, D), lambda i: (i, 0))],
        out_specs=[pl.BlockSpec((tm, D), lambda i: (i, 0)),
                   pl.BlockSpec((8, LANES), lambda i: (0, 0)),
                   pl.BlockSpec((1, D), lambda i: (0, 0))],
        out_shape=[jax.ShapeDtypeStruct((S, D), F32),
                   jax.ShapeDtypeStruct((8, LANES), F32),
                   jax.ShapeDtypeStruct((1, D), F32)],
        compiler_params=_params("arbitrary"),
    )(x, gam, target)


def _outproj_bwd(g, wout, mixed, ypool, o, proj):
    S, D = g.shape
    W = D // 2
    tm = min(TM, S)

    def body(g_ref, w_ref, mix_ref, y_ref, o_ref, pg_ref, ag_ref, dw_ref, dwb_ref, da_ref, dgate_ref, doaug_ref):
        @pl.when(pl.program_id(0) == 0)
        def _():
            dw_ref[...] = jnp.zeros(dw_ref.shape, F32)

        gb = g_ref[...].astype(BF16)
        dmixes = [lax.dot_general(gb, w_ref[half * W:(half + 1) * W, :], NT, preferred_element_type=F32)
                  for half in range(2)]
        dw_ref[...] += lax.dot_general(mix_ref[...], gb, TN_DIMS, preferred_element_type=F32)
        d_o = None
        for half, (val_ref, gate_ref) in enumerate(((y_ref, pg_ref), (o_ref, ag_ref))):
            cols = slice(half * W, (half + 1) * W)
            gt = gate_ref[...].astype(F32)
            sg = _sigmoid(gt)
            d_o = (dmixes[half] * (gt * sg)).astype(BF16)
            da_ref[:, cols] = d_o
            dgate_ref[:, cols] = (dmixes[half] * val_ref[...].astype(F32)
                                  * (sg * (1.0 + gt * (1.0 - sg)))).astype(BF16)

        lane, halves = _head_halves(tm)
        slot = lane & (HEAD_DIM - 1)
        for p in range(W // LANES):
            cols = slice(p * LANES, (p + 1) * LANES)
            prod = d_o[:, cols].astype(F32) * o_ref[:, cols].astype(F32)
            d_a = jnp.sum(jnp.where(halves[0], prod, 0.0), axis=1, keepdims=True)
            d_b = jnp.sum(jnp.where(halves[1], prod, 0.0), axis=1, keepdims=True)
            aug = jnp.zeros((tm, LANES), F32)
            for n, part in enumerate(_split3(jnp.where(halves[0], d_b, d_a))):
                aug = jnp.where(slot == SLOT_C + n, -part, aug)
            doaug_ref[:, cols] = aug.astype(BF16)

        @pl.when(pl.program_id(0) == S // tm - 1)
        def _():
            dwb_ref[...] = dw_ref[...].astype(BF16)

    rows = lambda width, col: pl.BlockSpec((tm, width), lambda i: (i, col))
    whole = pl.BlockSpec((D, D), lambda i: (0, 0))
    return pl.pallas_call(
        body, name="outproj_bwd", grid=(S // tm,),
        in_specs=[rows(D, 0), whole, rows(D, 0), rows(W, 0), rows(W, 0), rows(W, 1), rows(W, 5)],
        out_specs=[whole, whole, rows(D, 0), rows(D, 0), rows(W, 0)],
        out_shape=[jax.ShapeDtypeStruct((D, D), F32),
                   jax.ShapeDtypeStruct((D, D), BF16),
                   jax.ShapeDtypeStruct((S, D), BF16),
                   jax.ShapeDtypeStruct((S, D), BF16),
                   jax.ShapeDtypeStruct((S, W), BF16)],
        compiler_params=_params("arbitrary"),
    )(g, wout, mixed, ypool, o, proj, proj)


def _section_specs(sections, rows, width):
    specs = [pl.BlockSpec((rows, width), lambda k, c=c: (k, c)) for _, c in sections]
    return specs, [a for a, _ in sections]


def _inproj_bwd_dw(h, sections, dzf):
    S, D = h.shape
    W = D // 2
    n_sec = len(sections)
    N = n_sec * W
    ts = min(TM, S)
    n_steps = S // ts

    def body(h_ref, dz_ref, *rest):
        sec_refs, (dw_ref, dwb_ref) = rest[:n_sec], rest[n_sec:]

        @pl.when(pl.program_id(0) == 0)
        def _():
            dw_ref[...] = jnp.zeros(dw_ref.shape, F32)

        ht = h_ref[...].T
        dw_ref[:, N:N + LANES] += jnp.dot(ht, dz_ref[...], preferred_element_type=F32)
        for n, ref in enumerate(sec_refs):
            dw_ref[:, n * W:(n + 1) * W] += jnp.dot(ht, ref[...], preferred_element_type=F32)

        @pl.when(pl.program_id(0) == n_steps - 1)
        def _():
            dwb_ref[...] = dw_ref[...].astype(BF16)

    sec_specs, sec_arrays = _section_specs(sections, ts, W)
    whole = pl.BlockSpec((D, N + LANES), lambda k: (0, 0))
    return pl.pallas_call(
        body, name="inproj_bwd_dw", grid=(n_steps,),
        in_specs=[pl.BlockSpec((ts, D), lambda k: (k, 0)),
                  pl.BlockSpec((ts, LANES), lambda k: (k, 0))] + sec_specs,
        out_specs=[whole, whole],
        out_shape=[jax.ShapeDtypeStruct((D, N + LANES), F32),
                   jax.ShapeDtypeStruct((D, N + LANES), BF16)],
        compiler_params=_params("arbitrary"),
    )(h, dzf, *sec_arrays)


def _attn_bwd(proj, da, qaug, kaug, doaug):
    S = proj.shape[0]
    W = proj.shape[1] // 6
    P = W // LANES
    tq = min(TQ, S)
    nq = S // tq
    qc, kc, vc = 2 * P, 3 * P, 4 * P
    scale = 1.0 / math.sqrt(HEAD_DIM)

    def body(q_ref, k_ref, v_ref, do_ref, qa_ref, ka_ref, da_ref,
             dq_ref, dk_ref, dv_ref, dqx_ref, dkx_ref, km_scr, vm_scr, dk_scr, dv_scr, dq_scr):
        j = pl.program_id(1)
        lane, halves = _head_halves(tq)

        @pl.when(j == 0)
        def _():
            dq_scr[...] = jnp.zeros(dq_scr.shape, F32)

        v_ones = ((lane & (HEAD_DIM - 1)) < 3).astype(BF16)
        for a in range(2):
            km_scr[a] = jnp.where(halves[a], k_ref[...], ka_ref[...])
            vm_scr[a] = jnp.where(halves[a], v_ref[...], v_ones)
        dk_scr[...] = jnp.zeros(dk_scr.shape, F32)
        dv_scr[...] = jnp.zeros(dv_scr.shape, F32)

        def update(i, on_diagonal):
            rows = pl.ds(pl.multiple_of(i * tq, tq), tq)
            qs = q_ref[rows, :] * scale
            do2, qaug_t, doaug_t = do_ref[rows, :], qa_ref[rows, :], da_ref[rows, :]
            if on_diagonal:
                keep = (lax.broadcasted_iota(jnp.int32, (tq, tq), 0)
                        >= lax.broadcasted_iota(jnp.int32, (tq, tq), 1))
            qas = [jnp.where(halves[a], qs, qaug_t) for a in range(2)]
            logits = [lax.dot_general(qas[a], km_scr[a], NT, preferred_element_type=F32) for a in range(2)]
            dps = [lax.dot_general(jnp.where(halves[a], do2, doaug_t), vm_scr[a], NT, preferred_element_type=F32)
                   for a in range(2)]
            dv = None
            for a in range(2):
                s = jnp.where(keep, logits[a], NEG_INF) if on_diagonal else logits[a]
                p = jnp.exp(s)
                dsb = (p * dps[a]).astype(BF16)
                do0 = jnp.where(halves[a], do2, jnp.zeros_like(do2))
                dv_a = lax.dot_general(p.astype(BF16), do0, TN_DIMS, preferred_element_type=F32)
                dv = dv_a if dv is None else dv + dv_a
                dk_scr[a] += lax.dot_general(dsb, qas[a], TN_DIMS, preferred_element_type=F32)
                dq_scr[a, rows, :] += jnp.dot(dsb, km_scr[a], preferred_element_type=F32)
            dv_scr[...] += dv

        def below_diagonal(n, carry):
            update(j + 1 + 2 * n, False)
            update(j + 2 + 2 * n, False)
            return carry

        update(j, True)
        below = nq - 1 - j
        lax.fori_loop(0, below // 2, below_diagonal, 0)

        @pl.when(below % 2 == 1)
        def _():
            update(nq - 1, False)


        dk_ref[...] = jnp.where(halves[0], dk_scr[0], dk_scr[1]).astype(BF16)
        dkx_ref[...] = jnp.where(halves[0], dk_scr[1], dk_scr[0])
        dv_ref[...] = dv_scr[...].astype(BF16)

        @pl.when(j == nq - 1)
        def _():
            row_lane, row_halves = _head_halves(S)
            dq_ref[...] = (jnp.where(row_halves[0], dq_scr[0], dq_scr[1]) * scale).astype(BF16)
            dqx_ref[...] = jnp.where(row_halves[0], dq_scr[1], dq_scr[0])

    tile = lambda col: pl.BlockSpec((tq, LANES), lambda p, j: (j, col + p))
    whole = lambda col: pl.BlockSpec((S, LANES), lambda p, j: (0, col + p))
    return pl.pallas_call(
        body, name="attn_bwd", grid=(P, nq),
        in_specs=[whole(qc), tile(kc), tile(vc), whole(P), whole(0), tile(0), whole(0)],
        out_specs=[whole(0), tile(0), tile(0), whole(0), tile(0)],
        out_shape=[jax.ShapeDtypeStruct((S, W), BF16),
                   jax.ShapeDtypeStruct((S, W), BF16),
                   jax.ShapeDtypeStruct((S, W), BF16),
                   jax.ShapeDtypeStruct((S, W), F32),
                   jax.ShapeDtypeStruct((S, W), F32)],
        scratch_shapes=[pltpu.VMEM((2, tq, LANES), BF16),
                        pltpu.VMEM((2, tq, LANES), BF16),
                        pltpu.VMEM((2, tq, LANES), F32),
                        pltpu.VMEM((tq, LANES), F32),
                        pltpu.VMEM((2, S, LANES), F32)],
        compiler_params=_params("parallel", "arbitrary"),
    )(proj, proj, proj, da, qaug, kaug, doaug)


def _fgate_bwd(dqx, dkx, z, bias, n_heads):
    S = z.shape[0]
    tb = min(TB, S)
    nb = S // tb

    def body(dqx_ref, dkx_ref, z_ref, b_ref, dz_ref, db_ref):
        tri = (lax.broadcasted_iota(jnp.int32, (tb, tb), 1)
               >= lax.broadcasted_iota(jnp.int32, (tb, tb), 0)).astype(F32)
        lane = lax.broadcasted_iota(jnp.int32, (tb, LANES), 1)

        local = []
        for b in range(nb):
            rows = slice(b * tb, (b + 1) * tb)
            dc = jnp.zeros((tb, LANES), F32)
            for p in range(n_heads // 2):
                row_sums = dqx_ref[rows, p * LANES:(p + 1) * LANES]
                col_sums = dkx_ref[rows, p * LANES:(p + 1) * LANES]
                for h, at in ((2 * p, HEAD_DIM), (2 * p + 1, 0)):
                    one = row_sums[:, at + SLOT_C:at + SLOT_C + 1] - col_sums[:, at + SLOT_ONE:at + SLOT_ONE + 1]
                    dc = jnp.where(lane == h, jnp.broadcast_to(one, (tb, LANES)), dc)
            local.append(jnp.dot(tri, dc, preferred_element_type=F32, precision=lax.Precision.HIGHEST))
        carry = jnp.zeros((1, LANES), F32)
        db = jnp.zeros((1, LANES), F32)
        for b in reversed(range(nb)):
            rows = slice(b * tb, (b + 1) * tb)
            rc = local[b] + carry
            carry = rc[0:1, :]
            dz = rc * _sigmoid(-(z_ref[rows, :] + b_ref[...]))
            dz_ref[rows, :] = dz.astype(BF16)
            db = db + jnp.sum(dz, axis=0, keepdims=True)
        db_ref[...] = db

    return pl.pallas_call(
        body, name="fgate_bwd",
        out_shape=[jax.ShapeDtypeStruct((S, LANES), BF16),
                   jax.ShapeDtypeStruct((1, LANES), F32)],
        compiler_params=pltpu.CompilerParams(vmem_limit_bytes=VMEM_LIMIT),
    )(dqx, dkx, z, bias)


def _pool_bwd(proj, da, pool_w, pool_scale):
    S = proj.shape[0]
    G = len(POOL_WINDOWS)

    def body(u_ref, dy_ref, w_ref, s_ref, du_ref, dw_ref, ds_ref, pad_ref):
        g = pl.program_id(0)
        for gi, w in enumerate(POOL_WINDOWS):
            @pl.when(g == gi)
            def _():
                d, cnt = _window_mean_minus_self(u_ref[...].astype(F32), pad_ref, w, S)
                db = d.astype(BF16)
                wb = w_ref[0].astype(BF16)
                yraw = jnp.dot(db, wb, preferred_element_type=F32)
                dy = dy_ref[...].astype(F32)
                ds_ref[...] = jnp.sum(dy * yraw, axis=0, keepdims=True)
                dzb = (dy * s_ref[...]).astype(BF16)
                dw_ref[0] = lax.dot_general(db, dzb, TN_DIMS, preferred_element_type=F32)
                dd = lax.dot_general(dzb, wb, NT, preferred_element_type=F32)
                pad_ref[0:S, :] = dd / cnt
                pad_ref[S:S + MAX_WINDOW, :] = jnp.zeros((MAX_WINDOW, LANES), F32)
                acc = -dd
                for j in range(w):
                    acc = acc + pad_ref[j:j + S, :]
                du_ref[...] = acc.astype(BF16)

    return pl.pallas_call(
        body, name="pool_bwd", grid=(G,),
        in_specs=[pl.BlockSpec((S, LANES), lambda g: (0, g)),
                  pl.BlockSpec((S, LANES), lambda g: (0, g)),
                  pl.BlockSpec((1, LANES, LANES), lambda g: (g, 0, 0)),
                  pl.BlockSpec((1, LANES), lambda g: (0, g))],
        out_specs=[pl.BlockSpec((S, LANES), lambda g: (0, g)),
                   pl.BlockSpec((1, LANES, LANES), lambda g: (g, 0, 0)),
                   pl.BlockSpec((1, LANES), lambda g: (0, g))],
        out_shape=[jax.ShapeDtypeStruct((S, G * LANES), BF16),
                   jax.ShapeDtypeStruct((G, LANES, LANES), F32),
                   jax.ShapeDtypeStruct((1, G * LANES), F32)],
        scratch_shapes=[pltpu.VMEM((S + MAX_WINDOW, LANES), F32)],
        compiler_params=_params("arbitrary"),
    )(proj, da, pool_w, pool_scale)


def _inproj_bwd_dx(sections, dzf, w, x, gam, g, after=()):
    S, D = x.shape
    N = w.shape[1] - LANES
    W = D // 2
    n_sec = len(sections)
    tm = min(TM, S)

    def body(dz_ref, w_ref, wf_ref, x_ref, gam_ref, g_ref, *rest):
        sec_refs = rest[:n_sec]
        dx_ref, dg_ref = rest[-2:]

        @pl.when(pl.program_id(0) == 0)
        def _():
            dg_ref[...] = jnp.zeros(dg_ref.shape, F32)

        dh = lax.dot_general(dz_ref[...], wf_ref[...], NT, preferred_element_type=F32)
        for n, ref in enumerate(sec_refs):
            dh = dh + lax.dot_general(ref[...], w_ref[:, n * W:(n + 1) * W], NT, preferred_element_type=F32)
        xf = x_ref[...]
        r = lax.rsqrt(jnp.mean(xf * xf, axis=-1, keepdims=True) + RMS_EPS)
        xhat = xf * r
        dg_ref[...] += jnp.sum(dh * xhat, axis=0, keepdims=True)
        dxhat = dh * gam_ref[...]
        dx_ref[...] = g_ref[...] + r * (dxhat - xhat * jnp.mean(dxhat * xhat, axis=-1, keepdims=True))

    sec_specs, sec_arrays = _section_specs(sections, tm, W)
    return pl.pallas_call(
        body, name="inproj_bwd_dx", grid=(S // tm,),
        in_specs=[pl.BlockSpec((tm, LANES), lambda i: (i, 0)),
                  pl.BlockSpec((D, N), lambda i: (0, 0)),
                  pl.BlockSpec((D, LANES), lambda i: (0, N // LANES)),
                  pl.BlockSpec((tm, D), lambda i: (i, 0)),
                  pl.BlockSpec((1, D), lambda i: (0, 0)),
                  pl.BlockSpec((tm, D), lambda i: (i, 0))] + sec_specs + _after_specs(after),
        out_specs=[pl.BlockSpec((tm, D), lambda i: (i, 0)),
                   pl.BlockSpec((1, D), lambda i: (0, 0))],
        out_shape=[jax.ShapeDtypeStruct((S, D), F32),
                   jax.ShapeDtypeStruct((1, D), F32)],
        compiler_params=_params("arbitrary"),
    )(dzf, w, w, x, gam, g, *sec_arrays, *after)


def _adamw(w, m, v, gsets, name, rows, shifted=False, first=0, into=None):
    A, R, C = w.shape
    n_sets = len(gsets)
    tr = min(rows, R)
    c1 = 1.0 / (1.0 - ADAM_B1 ** ADAM_STEP)
    c2 = 1.0 / (1.0 - ADAM_B2 ** ADAM_STEP)
    counts = [len(gs) for gs in gsets]

    def body(w_ref, m_ref, v_ref, *rest):
        g_ref, d_ref, nm_ref, nv_ref = rest[-4:]
        at = 0
        for a in range(n_sets):
            part_refs = rest[at:at + counts[a]]
            at += counts[a]

            @pl.when(pl.program_id(0) == a)
            def _():
                g = None
                for ref in part_refs:
                    for s in range(ref.shape[0]):
                        term = ref[s].astype(F32)
                        g = term if g is None else g + term
                if shifted:
                    lanes = g.shape[1]
                    g = pltpu.roll(g, (lanes - _index(_position()) * (C % LANES)) % lanes, axis=1)[:, :C]
                nm = ADAM_B1 * m_ref[0] + (1.0 - ADAM_B1) * g
                nv = ADAM_B2 * v_ref[0] + (1.0 - ADAM_B2) * (g * g)
                g_ref[0] = g
                nm_ref[0] = nm
                nv_ref[0] = nv
                d_ref[0] = -ADAM_LR * ((nm * c1) / (jnp.sqrt(nv * c2) + ADAM_EPS) + ADAM_WD * w_ref[0])

    spec = pl.BlockSpec((1, tr, C), lambda a, r: (first + a, r, 0))
    part_specs = [pl.BlockSpec((part.shape[0], tr, part.shape[2]), lambda a, r, l=l: (0, jnp.where(a == l, r, 0), 0))
                  for l, gs in enumerate(gsets) for part in gs]
    parts = [part for gs in gsets for part in gs]
    shape = jax.ShapeDtypeStruct((A, R, C), F32)
    earlier = () if into is None else tuple(into)
    return pl.pallas_call(
        body, name=name, grid=(n_sets, R // tr),
        in_specs=[spec, spec, spec] + part_specs + _after_specs(earlier),
        out_specs=[spec, spec, spec, spec],
        out_shape=[shape, shape, shape, shape],
        input_output_aliases={3 + len(parts) + n: n for n in range(len(earlier))},
        compiler_params=_params("arbitrary", "arbitrary"),
    )(w, m, v, *parts, *earlier)


def _position():
    return lax.axis_index("x"), lax.axis_index("y"), lax.axis_index("c")


def _index(dev):
    return 4 * dev[0] + 2 * dev[1] + dev[2]


def _all_gather(arrs, slots, out_shapes, name):
    n_arr = len(arrs)

    def body(*refs):
        ins, outs = refs[:n_arr], refs[n_arr:2 * n_arr]
        send_sems, recv_sems, local_sems = refs[2 * n_arr:]
        x, y, c = _position()
        me, sibling = (x, y, c), (x, y, 1 - c)
        chips = [(1 - x, y), (x, 1 - y), (1 - x, 1 - y)]

        def copy(a, k, block, to, src=None):
            part = slots[a](outs[a], _index(block))
            return pltpu.make_async_remote_copy(
                src_ref=part if src is None else src, dst_ref=part,
                send_sem=send_sems.at[a, k], recv_sem=recv_sems.at[a, k],
                device_id=to, device_id_type=MESH)

        mine = [pltpu.make_async_copy(ins[a], slots[a](outs[a], _index(me)), local_sems.at[a])
                for a in range(n_arr)]
        for cp in mine:
            cp.start()
        first = []
        for a in range(n_arr):
            first.append(copy(a, 0, me, sibling, src=ins[a]))
            first += [copy(a, 1 + j, me, (*chip, c), src=ins[a]) for j, chip in enumerate(chips)]
        for cp in first:
            cp.start()
        passed = []
        for j, chip in enumerate(chips):
            for a in range(n_arr):
                copy(a, 1 + j, (*chip, c), me).wait_recv()
                fwd = copy(a, 4 + j, (*chip, c), sibling)
                fwd.start()
                passed.append(fwd)
        for a in range(n_arr):
            copy(a, 0, sibling, me).wait_recv()
            for j, chip in enumerate(chips):
                copy(a, 4 + j, (*chip, 1 - c), me).wait_recv()
        for cp in first + passed:
            cp.wait_send()
        for cp in mine:
            cp.wait()

    any_spec = pl.BlockSpec(memory_space=pl.ANY)
    return pl.pallas_call(
        body, name=name,
        in_specs=[any_spec] * n_arr, out_specs=[any_spec] * n_arr, out_shape=out_shapes,
        scratch_shapes=[pltpu.SemaphoreType.DMA((n_arr, 7)), pltpu.SemaphoreType.DMA((n_arr, 7)),
                        pltpu.SemaphoreType.DMA((n_arr,))],
    )(*arrs)


def _split_copies(srcs, lands, send_sems, recv_sems, kinds):
    x, y, c = _position()
    me = _index((x, y, c))
    copies = []
    for a, (src_part, land_part) in enumerate(kinds):
        for k in range(1, N_DEV):
            peer = (x ^ ((k >> 2) & 1), y ^ ((k >> 1) & 1), c ^ (k & 1))
            copies.append(pltpu.make_async_remote_copy(
                src_ref=src_part(srcs[a], _index(peer)), dst_ref=land_part(lands[a], me, k),
                send_sem=send_sems[a].at[k - 1], recv_sem=recv_sems[a].at[k - 1],
                device_id=peer, device_id_type=MESH))
    return copies


def _split_start(srcs, lands, kinds, name, after=()):
    n = len(srcs)

    def body(*refs):
        src_refs, land_refs = refs[:n], refs[n:2 * n]
        outs = refs[2 * n + len(after):]
        send_sems, recv_sems = outs[:n], outs[n:2 * n]
        token = outs[-1]
        for cp in _split_copies(src_refs, land_refs, send_sems, recv_sems, kinds):
            cp.start()
        token[...] = jnp.zeros(token.shape, token.dtype)

    hbm = pl.BlockSpec(memory_space=pltpu.HBM)
    sem = pl.BlockSpec(memory_space=pltpu.SEMAPHORE)
    operands = [pltpu.with_memory_space_constraint(t, pltpu.HBM) for t in (*srcs, *lands)]
    out = pl.pallas_call(
        body, name=name,
        in_specs=[hbm] * (2 * n) + _after_specs(after),
        out_specs=[sem] * (2 * n) + [hbm] * (2 * n) + [pl.BlockSpec(memory_space=pltpu.VMEM)],
        out_shape=[pltpu.SemaphoreType.DMA((N_DEV - 1,))] * (2 * n)
        + [pltpu.HBM(t.shape, t.dtype) for t in operands] + [jax.ShapeDtypeStruct((8, LANES), F32)],
        input_output_aliases={i: 2 * n + i for i in range(2 * n)},
        compiler_params=pltpu.CompilerParams(has_side_effects=pltpu.SideEffectType.DATAFLOW_SIDE_EFFECTING),
    )(*operands, *after)
    return [(out[a], out[n + a], out[2 * n + a], out[3 * n + a]) for a in range(n)], out[-1]


def _split_wait(started, kinds, after, name):
    n = len(started)
    sems = [t[0] for t in started] + [t[1] for t in started]
    srcs = [t[2] for t in started]
    lands = [t[3] for t in started]

    def body(*refs):
        src_refs, land_refs = refs[:n], refs[n:2 * n]
        send_sems, recv_sems = refs[2 * n:3 * n], refs[3 * n:4 * n]
        for cp in _split_copies(src_refs, land_refs, send_sems, recv_sems, kinds):
            cp.wait_send()
            cp.wait_recv()

    hbm = pl.BlockSpec(memory_space=pltpu.HBM)
    sem = pl.BlockSpec(memory_space=pltpu.SEMAPHORE)
    out = pl.pallas_call(
        body, name=name,
        in_specs=[hbm] * (2 * n) + [sem] * (2 * n) + _after_specs(after),
        out_specs=[hbm] * (2 * n),
        out_shape=[pltpu.HBM(t.shape, t.dtype) for t in (*srcs, *lands)],
        input_output_aliases={i: i for i in range(2 * n)},
        compiler_params=pltpu.CompilerParams(has_side_effects=pltpu.SideEffectType.DATAFLOW_SIDE_EFFECTING),
    )(*srcs, *lands, *sems, *after)
    return out[n:]


def _as_rows(p):
    if p.size % LANES == 0:
        rows = p.reshape(-1, LANES)
    else:
        rows = p.reshape(-1, p.shape[-1])
        rows = jnp.pad(rows, ((0, 0), (0, LANES - rows.shape[1])))
    return jnp.pad(rows, ((0, -rows.shape[0] % 8), (0, 0)))


def _pack(parts):
    return jnp.concatenate([_as_rows(p) for p in parts])[None]


def _unpack(packed, like):
    out, at = [], 0
    for p in like:
        whole = p.size % LANES == 0
        n = p.size // LANES if whole else p.size // p.shape[-1]
        rows = packed[0, at:at + n]
        out.append((rows if whole else rows[:, :p.shape[-1]]).reshape(p.shape))
        at += n + (-n % 8)
    return out


def _local_step(x, target, norm_g, forget_bias, pool_w, pool_scale, final_g, weights_in, weights_out, on_grads,
                first_after=()):
    L = norm_g.shape[0]
    S, D = x.shape
    W = D // 2
    H = W // HEAD_DIM
    bias = jnp.pad(forget_bias, ((0, 0), (0, LANES - H)))

    saved = []
    after = tuple(first_after)
    for l in range(L):
        proj, h, z, w = _inproj_fwd(x, norm_g[l:l + 1], weights_in(l, x), after)
        after = ()
        qaug, kaug = _fgate_fwd(z, bias[l:l + 1], H)
        ypool = _pool_fwd(proj, pool_w[l], pool_scale[l:l + 1])
        o, qaug_b = _attn_fwd(proj, qaug, kaug)
        wout = weights_out(l, o)
        x_new, mixed = _outproj_fwd(ypool, o, proj, x, wout)
        saved.append((x, proj, h, z, qaug_b, kaug, ypool, o, mixed, w, wout))
        x = x_new

    g, loss, d_final_g = _loss_head(x, final_g.reshape(1, D), target)

    small = None
    for l in reversed(range(L)):
        x_in, proj, h, z, qaug_b, kaug, ypool, o, mixed, w, wout = saved[l]
        d_wout, d_wout_bf16, da, dgate, doaug = _outproj_bwd(g, wout, mixed, ypool, o, proj)
        dq, dk, dv, dqx, dkx = _attn_bwd(proj, da, qaug_b, kaug, doaug)
        dzf, db = _fgate_bwd(dqx, dkx, z, bias[l:l + 1], H)
        dpu, dpw, dps = _pool_bwd(proj, da, pool_w[l], pool_scale[l:l + 1])
        dproj = [(dpu, 0), (dgate, 0), (dq, 0), (dk, 0), (dv, 0), (dgate, 1)]
        d_w, d_w_bf16 = _inproj_bwd_dw(h, dproj, dzf)
        after = tuple(on_grads(l, d_w, d_w_bf16, d_wout, d_wout_bf16, small))
        g, dgam = _inproj_bwd_dx(dproj, dzf, w, x_in, norm_g[l:l + 1], g, after)
        small = (dgam[0], db[0, :H], dpw, dps[0])
    return loss[0, 0], g, small, d_final_g[0]


def kernel(x, norm_g, w_in, forget_bias, pool_w, pool_scale, w_out, final_g, loss_target, m_norm_g, m_w_in, m_forget_bias, m_pool_w, m_pool_scale, m_w_out, m_final_g, v_norm_g, v_w_in, v_forget_bias, v_pool_w, v_pool_scale, v_w_out, v_final_g):
    L, D, cols = w_in.shape
    rows_out = w_out.shape[1]
    W = D // 2
    H = W // HEAD_DIM
    me = _index(_position())

    slot = _slot_width(cols)
    wout_b = w_out.astype(BF16)
    win_b = _shift_slots(w_in)
    gather_in = (lambda ref, peer: ref, lambda ref, mine, k: ref.at[mine])
    gather_out = (lambda ref, peer: ref, lambda ref, mine, k: ref.at[pl.ds(mine * rows_out, rows_out), :])

    def landing(block, n_slots):
        zone = lax.empty((n_slots * block.shape[0], *block.shape[1:]), block.dtype)
        return lax.dynamic_update_slice(zone, block, (me * block.shape[0],) + (0,) * (block.ndim - 1))

    (first_in,) = _all_gather([win_b[0]], [lambda ref, n: ref.at[n]],
                              [jax.ShapeDtypeStruct((N_DEV, D, slot), BF16)], "gather_first")
    rest_srcs = [wout_b[0]] + [w[l] for l in range(1, L) for w in (win_b, wout_b)]
    rest_lands = [landing(wout_b[0], N_DEV)]
    for l in range(1, L):
        rest_lands += [landing(win_b[l][None], N_DEV), landing(wout_b[l], N_DEV)]
    rest_kinds = [gather_out] + [gather_in, gather_out] * (L - 1)
    rest, rest_token = _split_start(rest_srcs, rest_lands, rest_kinds, "gather_start_rest", (first_in,))

    def weights_in(l, x_in):
        if l == 0:
            return first_in
        (win_all,) = _split_wait([rest[2 * l - 1]], [gather_in], (x_in,), f"gather_wait_in_{l}")
        return win_all

    def weights_out(l, o):
        (wout_full,) = _split_wait([rest[2 * l]], [gather_out], (o,), f"gather_wait_out_{l}")
        return wout_full

    stride = slot - LANES
    exchange_kinds = [(lambda ref, peer: ref.at[:, pl.ds(pl.multiple_of(peer * stride, LANES), slot)],
                       lambda ref, mine, k: ref.at[k - 1]),
                      (lambda ref, peer: ref.at[pl.ds(peer * rows_out, rows_out), :],
                       lambda ref, mine, k: ref.at[k - 1])]
    zero_g = jnp.zeros_like(final_g)
    zero_loss = jnp.zeros((LANES,), F32)

    def small_pack(l, norm_g_l, bias_l, pool_w_l, pool_scale_l, final, loss_row=None):
        return _pack([norm_g_l, bias_l, pool_w_l, pool_scale_l, final if l == 0 else zero_g,
                      zero_loss if loss_row is None else loss_row])[0]

    exchanges, own_parts = {}, {}

    def on_grads(l, dw, dw_bf16, d_wout, d_wout_bf16, small):
        own_parts[l] = (lax.dynamic_slice_in_dim(dw, me * stride, slot, 1)[None],
                        lax.dynamic_slice_in_dim(d_wout, me * rows_out, rows_out, 0)[None])
        srcs = [dw_bf16, d_wout_bf16]
        lands = [lax.empty((N_DEV - 1, D, slot), BF16), lax.empty((N_DEV - 1, rows_out, D), BF16)]
        kinds = list(exchange_kinds)
        if small is not None:
            packed_small = small_pack(l + 1, *small, None)
            srcs.append(packed_small)
            lands.append(landing(packed_small[None], N_DEV))
            kinds.append(gather_in)
        started, token = _split_start(srcs, lands, kinds, f"exchange_start_{l}")
        exchanges[l] = (started, kinds)
        return (token,)

    loss, dx, small_first, d_final_g = _local_step(
        x[0], loss_target[0], norm_g, forget_bias, pool_w, pool_scale, final_g,
        weights_in, weights_out, on_grads, (rest_token,))
    packed_first = small_pack(0, *small_first, d_final_g, jnp.full((LANES,), loss, F32))
    first_started, first_token = _split_start(
        [packed_first], [landing(packed_first[None], N_DEV)], [gather_in],
        "small_start", (w_in, m_w_in, v_w_in, *own_parts[0]))

    gin_sets, gout_sets, small_sets = [None] * L, [None] * L, [None] * L

    def wait_for(l, after):
        started, kinds = exchanges[l]
        got = _split_wait(started, kinds, after, f"exchange_wait_{l}")
        gin_sets[l] = [own_parts[l][0], got[0]]
        gout_sets[l] = [own_parts[l][1], got[1]]
        if len(got) > 2:
            small_sets[l + 1] = [got[2]]

    for l in range(1, L):
        wait_for(l, (dx, first_token))
    rest_in = _adamw(w_in, m_w_in, v_w_in, gin_sets[1:], "adamw_w_in_rest", 256, shifted=True, first=1)
    rest_out = _adamw(w_out, m_w_out, v_w_out, gout_sets[1:], "adamw_w_out_rest", 128, first=1)
    wait_for(0, (rest_in[1], rest_out[1]))
    g_w_in, d_w_in, nm_w_in, nv_w_in = _adamw(w_in, m_w_in, v_w_in, gin_sets[:1], "adamw_w_in_first", 256,
                                              shifted=True, into=rest_in)
    g_w_out, d_w_out, nm_w_out, nv_w_out = _adamw(w_out, m_w_out, v_w_out, gout_sets[:1], "adamw_w_out_first", 128,
                                                  into=rest_out)
    small_sets[0] = _split_wait(first_started, [gather_in], (d_w_in, d_w_out), "small_wait")
    loss = jnp.sum(small_sets[0][0][:, packed_first.shape[0] - 8, 0])

    def small_stack(norm_g_, bias_, pool_w_, pool_scale_, final):
        return jnp.stack([small_pack(l, norm_g_[l], bias_[l], pool_w_[l], pool_scale_[l], final) for l in range(L)])

    packed = _adamw(small_stack(norm_g, forget_bias, pool_w, pool_scale, final_g),
                    small_stack(m_norm_g, m_forget_bias, m_pool_w, m_pool_scale, m_final_g),
                    small_stack(v_norm_g, v_forget_bias, v_pool_w, v_pool_scale, v_final_g),
                    small_sets, "adamw_small", packed_first.shape[0])

    def small_unpack(p):
        like = [norm_g[0], forget_bias[0], pool_w[0], pool_scale[0], final_g]
        layers = [_unpack(p[l:l + 1], like) for l in range(L)]
        return [jnp.stack([layers[l][n] for l in range(L)]) for n in range(4)] + [layers[0][4]]

    g_s, d_s, nm_s, nv_s = [small_unpack(p) for p in packed]

    def order(big_in, big_out, small):
        return (small[0], big_in, small[1], small[2], small[3], big_out, small[4])

    return (loss, dx[None], *order(g_w_in, g_w_out, g_s), *order(d_w_in, d_w_out, d_s),
            *order(nm_w_in, nm_w_out, nm_s), *order(nv_w_in, nv_w_out, nv_s))
```

```python
import math

import jax
import jax.numpy as jnp
from jax import lax
from jax.experimental import pallas as pl
from jax.experimental.pallas import tpu as pltpu

F32 = jnp.float32
BF16 = jnp.bfloat16
MESH = pl.DeviceIdType.MESH

RMS_EPS = 1e-6
NEG_INF = -1e30
HEAD_DIM = 64
POOL_WINDOWS = (2, 4, 8, 16)
MAX_WINDOW = 16
LANES = 128
N_DEV = 8

ADAM_LR = 0.001
ADAM_B1 = 0.9
ADAM_B2 = 0.999
ADAM_EPS = 1e-08
ADAM_WD = 0.01
ADAM_STEP = 10

TM = 512
TN = 512
TQ = 512
TB = 256
VMEM_LIMIT = 56 * 1024 * 1024

NT = (((1,), (1,)), ((), ()))
TN_DIMS = (((0,), (0,)), ((), ()))

SLOT_C, SLOT_ONE, SLOT_LSE = 0, 3, 6


def _params(*sem):
    return pltpu.CompilerParams(dimension_semantics=sem, vmem_limit_bytes=VMEM_LIMIT)


def _sigmoid(x):
    return 1.0 / (1.0 + jnp.exp(-x))


def _split3(x):
    hi = x.astype(BF16).astype(F32)
    rest = x - hi
    mid = rest.astype(BF16).astype(F32)
    return hi, mid, rest - mid


def _after_specs(after):
    return [pl.BlockSpec(memory_space=pl.ANY)] * len(after)


def _slot_width(cols):
    return LANES * (-(-(cols + (N_DEV - 1) * (cols % LANES)) // LANES))


def _shift_slots(w_in):
    L, D, cols = w_in.shape
    slot = _slot_width(cols)
    tr = min(TM // 2, D)

    def body(w_ref, o_ref, pad_scr):
        pad_scr[...] = jnp.zeros(pad_scr.shape, F32)
        pad_scr[:, 0:cols] = w_ref[0]
        o_ref[0] = pltpu.roll(pad_scr[...], _index(_position()) * (cols % LANES), axis=1).astype(BF16)

    return pl.pallas_call(
        body, name="shift_slots", grid=(L, D // tr),
        in_specs=[pl.BlockSpec((1, tr, cols), lambda l, r: (l, r, 0))],
        out_specs=pl.BlockSpec((1, tr, slot), lambda l, r: (l, r, 0)),
        out_shape=jax.ShapeDtypeStruct((L, D, slot), BF16),
        scratch_shapes=[pltpu.VMEM((tr, slot), F32)],
        compiler_params=_params("parallel", "parallel"),
    )(w_in)


def _inproj_fwd(x, gam, slots, after=()):
    S, D = x.shape
    n_dev, _, sw = slots.shape
    stride = sw - LANES
    width = stride * n_dev + LANES
    N = width - LANES
    tm, tn = min(TM, S), TN

    def body(x_ref, g_ref, s_ref, *rest):
        proj_ref, h_ref, z_ref, w_ref = rest[-4:]

        @pl.when(pl.program_id(0) == 0)
        def _():
            for n in range(n_dev):
                base = stride * n
                first = s_ref[n, :, 0:LANES]
                if n > 0:
                    first = first + s_ref[n - 1, :, stride:sw]
                w_ref[:, base:base + LANES] = first
                w_ref[:, base + LANES:base + stride] = s_ref[n, :, LANES:stride]
            w_ref[:, stride * n_dev:width] = s_ref[n_dev - 1, :, stride:sw]

        xf = x_ref[...]
        r = lax.rsqrt(jnp.mean(xf * xf, axis=-1, keepdims=True) + RMS_EPS)
        h = ((xf * r) * g_ref[...]).astype(BF16)
        h_ref[...] = h
        z_ref[...] = jnp.dot(h, w_ref[:, N:width], preferred_element_type=F32)
        for n in range(N // tn):
            cols = slice(n * tn, (n + 1) * tn)
            proj_ref[:, cols] = jnp.dot(h, w_ref[:, cols], preferred_element_type=F32).astype(BF16)

    return pl.pallas_call(
        body, name="inproj_fwd", grid=(S // tm,),
        in_specs=[pl.BlockSpec((tm, D), lambda i: (i, 0)),
                  pl.BlockSpec((1, D), lambda i: (0, 0)),
                  pl.BlockSpec((n_dev, D, sw), lambda i: (0, 0, 0))] + _after_specs(after),
        out_specs=[pl.BlockSpec((tm, N), lambda i: (i, 0)),
                   pl.BlockSpec((tm, D), lambda i: (i, 0)),
                   pl.BlockSpec((tm, LANES), lambda i: (i, 0)),
                   pl.BlockSpec((D, width), lambda i: (0, 0))],
        out_shape=[jax.ShapeDtypeStruct((S, N), BF16),
                   jax.ShapeDtypeStruct((S, D), BF16),
                   jax.ShapeDtypeStruct((S, LANES), F32),
                   jax.ShapeDtypeStruct((D, width), BF16)],
        compiler_params=_params("arbitrary"),
    )(x, gam, slots, *after)


def _fgate_fwd(z, bias, n_heads):
    S = z.shape[0]
    tb = min(TB, S)
    P = n_heads // 2
    assert n_heads <= 8, "the three parts of c are packed eight lanes apart"

    def body(z_ref, b_ref, qaug_ref, kaug_ref):
        lane = lax.broadcasted_iota(jnp.int32, (tb, LANES), 1)
        tri = (lax.broadcasted_iota(jnp.int32, (tb, tb), 0)
               >= lax.broadcasted_iota(jnp.int32, (tb, tb), 1)).astype(F32)
        row = lax.broadcasted_iota(jnp.int32, (LANES, P * LANES), 0)
        col = lax.broadcasted_iota(jnp.int32, (LANES, P * LANES), 1)
        head, part_n = row & 7, row >> 3
        home = (head >> 1) * LANES + jnp.where((head & 1) == 0, HEAD_DIM, 0)
        is_part = jnp.logical_and(head < n_heads, part_n < 3)
        place_q = jnp.logical_and(is_part, col == home + SLOT_C + part_n).astype(BF16)
        place_k = jnp.logical_and(is_part, col == home + SLOT_ONE + part_n).astype(BF16)
        slot = lax.broadcasted_iota(jnp.int32, (tb, P * LANES), 1) & (HEAD_DIM - 1)
        q_ones = jnp.logical_and(slot >= SLOT_ONE, slot < SLOT_ONE + 3).astype(F32)
        k_ones = jnp.logical_or(slot < SLOT_C + 3,
                                jnp.logical_and(slot >= SLOT_LSE, slot < SLOT_LSE + 3)).astype(F32)

        local = []
        for b in range(S // tb):
            zz = z_ref[b * tb:(b + 1) * tb, :] + b_ref[...]
            lf = jnp.minimum(zz, 0.0) - jnp.log(1.0 + jnp.exp(-jnp.abs(zz)))
            lf = jnp.where(lane < n_heads, lf, 0.0)
            local.append(jnp.dot(tri, lf, preferred_element_type=F32, precision=lax.Precision.HIGHEST))
        carry = jnp.zeros((1, LANES), F32)
        for b, part_sum in enumerate(local):
            c = part_sum + carry
            carry = c[tb - 1:tb, :]
            hi, mid, lo = _split3(c)
            packed = (hi + pltpu.roll(mid, 8, axis=1) + pltpu.roll(lo, 16, axis=1)).astype(BF16)
            qaug_ref[b * tb:(b + 1) * tb, :] = (
                q_ones + jnp.dot(packed, place_q, preferred_element_type=F32)).astype(BF16)
            kaug_ref[b * tb:(b + 1) * tb, :] = (
                k_ones - jnp.dot(packed, place_k, preferred_element_type=F32)).astype(BF16)

    return pl.pallas_call(
        body, name="fgate_fwd",
        out_shape=[jax.ShapeDtypeStruct((S, P * LANES), BF16),
                   jax.ShapeDtypeStruct((S, P * LANES), BF16)],
        compiler_params=pltpu.CompilerParams(vmem_limit_bytes=VMEM_LIMIT),
    )(z, bias)


def _window_mean_minus_self(u, pad_ref, w, S):
    pad_ref[0:MAX_WINDOW, :] = jnp.zeros((MAX_WINDOW, LANES), F32)
    pad_ref[MAX_WINDOW:MAX_WINDOW + S, :] = u
    acc = u
    for j in range(1, w):
        acc = acc + pad_ref[MAX_WINDOW - j:MAX_WINDOW - j + S, :]
    t = lax.broadcasted_iota(jnp.int32, (S, LANES), 0)
    cnt = jnp.minimum(t + 1, w).astype(F32)
    return acc / cnt - u, cnt


def _pool_fwd(proj, pool_w, pool_scale):
    S = proj.shape[0]
    G = len(POOL_WINDOWS)

    def body(u_ref, w_ref, s_ref, y_ref, pad_ref):
        g = pl.program_id(0)
        for gi, w in enumerate(POOL_WINDOWS):
            @pl.when(g == gi)
            def _():
                d, _ = _window_mean_minus_self(u_ref[...].astype(F32), pad_ref, w, S)
                y = jnp.dot(d.astype(BF16), w_ref[0].astype(BF16), preferred_element_type=F32)
                y_ref[...] = (y * s_ref[...]).astype(BF16)

    return pl.pallas_call(
        body, name="pool_fwd", grid=(G,),
        in_specs=[pl.BlockSpec((S, LANES), lambda g: (0, g)),
                  pl.BlockSpec((1, LANES, LANES), lambda g: (g, 0, 0)),
                  pl.BlockSpec((1, LANES), lambda g: (0, g))],
        out_specs=pl.BlockSpec((S, LANES), lambda g: (0, g)),
        out_shape=jax.ShapeDtypeStruct((S, G * LANES), BF16),
        scratch_shapes=[pltpu.VMEM((S + MAX_WINDOW, LANES), F32)],
        compiler_params=_params("arbitrary"),
    )(proj, pool_w, pool_scale)


def _head_halves(rows):
    lane = lax.broadcasted_iota(jnp.int32, (rows, LANES), 1)
    return lane, (lane < HEAD_DIM, lane >= HEAD_DIM)


def _attn_fwd(proj, qaug, kaug):
    S = proj.shape[0]
    W = proj.shape[1] // 6
    P = W // LANES
    tq = min(TQ, S)
    nq = S // tq
    qc, kc, vc = 2 * P, 3 * P, 4 * P
    scale = 1.0 / math.sqrt(HEAD_DIM)

    def body(q_ref, k_ref, v_ref, qa_ref, ka_ref, o_ref, qb_ref, qm_scr, m_scr, acc_scr):
        i = pl.program_id(1)
        lane, halves = _head_halves(tq)
        v_ones = ((lane & (HEAD_DIM - 1)) < 3).astype(BF16)
        qs = q_ref[...] * scale
        qm_scr[0] = jnp.where(halves[0], qs, qa_ref[...])
        qm_scr[1] = jnp.where(halves[1], qs, qa_ref[...])
        m_scr[...] = jnp.full(m_scr.shape, NEG_INF, F32)
        acc_scr[...] = jnp.zeros(acc_scr.shape, F32)

        def update(j, on_diagonal):
            keys = pl.ds(pl.multiple_of(j * tq, tq), tq)
            k2, v2, kaug_t = k_ref[keys, :], v_ref[keys, :], ka_ref[keys, :]
            if on_diagonal:
                keep = (lax.broadcasted_iota(jnp.int32, (tq, tq), 0)
                        >= lax.broadcasted_iota(jnp.int32, (tq, tq), 1))
            logits = [lax.dot_general(qm_scr[a], jnp.where(halves[a], k2, kaug_t), NT, preferred_element_type=F32)
                      for a in range(2)]
            for a in range(2):
                s = jnp.where(keep, logits[a], NEG_INF) if on_diagonal else logits[a]
                va = jnp.where(halves[a], v2, v_ones)
                m_prev = m_scr[a]
                m_new = jnp.maximum(m_prev, jnp.max(s, axis=1, keepdims=True))
                p = jnp.exp(s - jnp.tile(m_new, (1, tq // LANES)))
                acc_scr[a] = jnp.exp(m_prev - m_new) * acc_scr[a] + jnp.dot(p.astype(BF16), va,
                                                                              preferred_element_type=F32)
                m_scr[a] = m_new

        def below_diagonal(jj, carry):
            update(2 * jj, False)
            update(2 * jj + 1, False)
            return carry

        lax.fori_loop(0, i // 2, below_diagonal, 0)

        @pl.when(i % 2 == 1)
        def _():
            update(i - 1, False)

        update(i, True)
        acc_a, acc_b = acc_scr[0], acc_scr[1]
        l_a, l_b = acc_a[:, HEAD_DIM:HEAD_DIM + 1], acc_b[:, 0:1]
        o_ref[...] = jnp.where(halves[0], acc_a / l_a, acc_b / l_b).astype(BF16)
        lse = jnp.where(halves[0], m_scr[1] + jnp.log(l_b), m_scr[0] + jnp.log(l_a))
        slot = lane & (HEAD_DIM - 1)
        aug = qa_ref[...].astype(F32)
        for n, part in enumerate(_split3(lse)):
            aug = jnp.where(slot == SLOT_LSE + n, -part, aug)
        qb_ref[...] = aug.astype(BF16)

    tile = lambda col: pl.BlockSpec((tq, LANES), lambda p, i: (i, col + p))
    whole = lambda col: pl.BlockSpec((S, LANES), lambda p, i: (0, col + p))
    return pl.pallas_call(
        body, name="attn_fwd", grid=(P, nq),
        in_specs=[tile(qc), whole(kc), whole(vc), tile(0), whole(0)],
        out_specs=[tile(0), tile(0)],
        out_shape=[jax.ShapeDtypeStruct((S, W), BF16), jax.ShapeDtypeStruct((S, W), BF16)],
        scratch_shapes=[pltpu.VMEM((2, tq, LANES), BF16),
                        pltpu.VMEM((2, tq, LANES), F32),
                        pltpu.VMEM((2, tq, LANES), F32)],
        compiler_params=_params("parallel", "arbitrary"),
    )(proj, proj, proj, qaug, kaug)


def _outproj_fwd(ypool, o, proj, x, wout):
    S, D = x.shape
    W = D // 2
    tm, tn = min(TM, S), TN

    def body(y_ref, o_ref, pg_ref, ag_ref, x_ref, w_ref, xn_ref, mix_ref):
        pg, ag = pg_ref[...].astype(F32), ag_ref[...].astype(F32)
        mix_ref[:, 0:W] = (y_ref[...].astype(F32) * (pg * _sigmoid(pg))).astype(BF16)
        mix_ref[:, W:D] = (o_ref[...].astype(F32) * (ag * _sigmoid(ag))).astype(BF16)
        for n in range(D // tn):
            cols = slice(n * tn, (n + 1) * tn)
            xn_ref[:, cols] = x_ref[:, cols] + jnp.dot(mix_ref[...], w_ref[:, cols], preferred_element_type=F32)

    return pl.pallas_call(
        body, name="outproj_fwd", grid=(S // tm,),
        in_specs=[pl.BlockSpec((tm, W), lambda i: (i, 0)),
                  pl.BlockSpec((tm, W), lambda i: (i, 0)),
                  pl.BlockSpec((tm, W), lambda i: (i, 1)),
                  pl.BlockSpec((tm, W), lambda i: (i, 5)),
                  pl.BlockSpec((tm, D), lambda i: (i, 0)),
                  pl.BlockSpec((D, D), lambda i: (0, 0))],
        out_specs=[pl.BlockSpec((tm, D), lambda i: (i, 0)),
                   pl.BlockSpec((tm, D), lambda i: (i, 0))],
        out_shape=[jax.ShapeDtypeStruct((S, D), F32),
                   jax.ShapeDtypeStruct((S, D), BF16)],
        compiler_params=_params("parallel"),
    )(ypool, o, proj, proj, x, wout)


def _loss_head(x, gam, target):
    S, D = x.shape
    tm = min(TM, S)

    def body(x_ref, g_ref, t_ref, dx_ref, loss_ref, dg_ref):
        @pl.when(pl.program_id(0) == 0)
        def _():
            loss_ref[...] = jnp.zeros(loss_ref.shape, F32)
            dg_ref[...] = jnp.zeros(dg_ref.shape, F32)

        xf, gam_v = x_ref[...], g_ref[...]
        r = lax.rsqrt(jnp.mean(xf * xf, axis=-1, keepdims=True) + RMS_EPS)
        xhat = xf * r
        err = xhat * gam_v - t_ref[...]
        part = jnp.sum(jnp.sum(err * err, axis=-1, keepdims=True), axis=0, keepdims=True)
        loss_ref[...] += part * (0.5 / D)
        dy = err * (1.0 / D)
        dg_ref[...] += jnp.sum(dy * xhat, axis=0, keepdims=True)
        dxhat = dy * gam_v
        dx_ref[...] = r * (dxhat - xhat * jnp.mean(dxhat * xhat, axis=-1, keepdims=True))

    return pl.pallas_call(
        body, name="loss_head", grid=(S // tm,),
        in_specs=[pl.BlockSpec((tm, D), lambda i: (i, 0)),
                  pl.BlockSpec((1, D), lambda i: (0, 0)),
                  pl.BlockSpec((tm, D), lambda i: (i, 0))],
        out_specs=[pl.BlockSpec((tm, D), lambda i: (i, 0)),
                   pl.BlockSpec((8, LANES), lambda i: (0, 0)),
                   pl.BlockSpec((1, D), lambda i: (0, 0))],
        out_shape=[jax.ShapeDtypeStruct((S, D), F32),
                   jax.ShapeDtypeStruct((8, LANES), F32),
                   jax.ShapeDtypeStruct((1, D), F32)],
        compiler_params=_params("arbitrary"),
    )(x, gam, target)


def _outproj_bwd(g, wout, mixed, ypool, o, proj):
    S, D = g.shape
    W = D // 2
    tm = min(TM, S)

    def body(g_ref, w_ref, mix_ref, y_ref, o_ref, pg_ref, ag_ref, dw_ref, dwb_ref, da_ref, dgate_ref, doaug_ref):
        @pl.when(pl.program_id(0) == 0)
        def _():
            dw_ref[...] = jnp.zeros(dw_ref.shape, F32)

        gb = g_ref[...].astype(BF16)
        dmixes = [lax.dot_general(gb, w_ref[half * W:(half + 1) * W, :], NT, preferred_element_type=F32)
                  for half in range(2)]
        dw_ref[...] += lax.dot_general(mix_ref[...], gb, TN_DIMS, preferred_element_type=F32)
        d_o = None
        for half, (val_ref, gate_ref) in enumerate(((y_ref, pg_ref), (o_ref, ag_ref))):
            cols = slice(half * W, (half + 1) * W)
            gt = gate_ref[...].astype(F32)
            sg = _sigmoid(gt)
            d_o = (dmixes[half] * (gt * sg)).astype(BF16)
            da_ref[:, cols] = d_o
            dgate_ref[:, cols] = (dmixes[half] * val_ref[...].astype(F32)
                                  * (sg * (1.0 + gt * (1.0 - sg)))).astype(BF16)

        lane, halves = _head_halves(tm)
        slot = lane & (HEAD_DIM - 1)
        for p in range(W // LANES):
            cols = slice(p * LANES, (p + 1) * LANES)
            prod = d_o[:, cols].astype(F32) * o_ref[:, cols].astype(F32)
            d_a = jnp.sum(jnp.where(halves[0], prod, 0.0), axis=1, keepdims=True)
            d_b = jnp.sum(jnp.where(halves[1], prod, 0.0), axis=1, keepdims=True)
            aug = jnp.zeros((tm, LANES), F32)
            for n, part in enumerate(_split3(jnp.where(halves[0], d_b, d_a))):
                aug = jnp.where(slot == SLOT_C + n, -part, aug)
            doaug_ref[:, cols] = aug.astype(BF16)

        @pl.when(pl.program_id(0) == S // tm - 1)
        def _():
            dwb_ref[...] = dw_ref[...].astype(BF16)

    rows = lambda width, col: pl.BlockSpec((tm, width), lambda i: (i, col))
    whole = pl.BlockSpec((D, D), lambda i: (0, 0))
    return pl.pallas_call(
        body, name="outproj_bwd", grid=(S // tm,),
        in_specs=[rows(D, 0), whole, rows(D, 0), rows(W, 0), rows(W, 0), rows(W, 1), rows(W, 5)],
        out_specs=[whole, whole, rows(D, 0), rows(D, 0), rows(W, 0)],
        out_shape=[jax.ShapeDtypeStruct((D, D), F32),
                   jax.ShapeDtypeStruct((D, D), BF16),
                   jax.ShapeDtypeStruct((S, D), BF16),
                   jax.ShapeDtypeStruct((S, D), BF16),
                   jax.ShapeDtypeStruct((S, W), BF16)],
        compiler_params=_params("arbitrary"),
    )(g, wout, mixed, ypool, o, proj, proj)


def _section_specs(sections, rows, width):
    specs = [pl.BlockSpec((rows, width), lambda k, c=c: (k, c)) for _, c in sections]
    return specs, [a for a, _ in sections]


def _inproj_bwd_dw(h, sections, dzf):
    S, D = h.shape
    W = D // 2
    n_sec = len(sections)
    N = n_sec * W
    ts = min(TM, S)
    n_steps = S // ts

    def body(h_ref, dz_ref, *rest):
        sec_refs, (dw_ref, dwb_ref) = rest[:n_sec], rest[n_sec:]

        @pl.when(pl.program_id(0) == 0)
        def _():
            dw_ref[...] = jnp.zeros(dw_ref.shape, F32)

        ht = h_ref[...].T
        dw_ref[:, N:N + LANES] += jnp.dot(ht, dz_ref[...], preferred_element_type=F32)
        for n, ref in enumerate(sec_refs):
            dw_ref[:, n * W:(n + 1) * W] += jnp.dot(ht, ref[...], preferred_element_type=F32)

        @pl.when(pl.program_id(0) == n_steps - 1)
        def _():
            dwb_ref[...] = dw_ref[...].astype(BF16)

    sec_specs, sec_arrays = _section_specs(sections, ts, W)
    whole = pl.BlockSpec((D, N + LANES), lambda k: (0, 0))
    return pl.pallas_call(
        body, name="inproj_bwd_dw", grid=(n_steps,),
        in_specs=[pl.BlockSpec((ts, D), lambda k: (k, 0)),
                  pl.BlockSpec((ts, LANES), lambda k: (k, 0))] + sec_specs,
        out_specs=[whole, whole],
        out_shape=[jax.ShapeDtypeStruct((D, N + LANES), F32),
                   jax.ShapeDtypeStruct((D, N + LANES), BF16)],
        compiler_params=_params("arbitrary"),
    )(h, dzf, *sec_arrays)


def _attn_bwd(proj, da, qaug, kaug, doaug):
    S = proj.shape[0]
    W = proj.shape[1] // 6
    P = W // LANES
    tq = min(TQ, S)
    nq = S // tq
    qc, kc, vc = 2 * P, 3 * P, 4 * P
    scale = 1.0 / math.sqrt(HEAD_DIM)

    def body(q_ref, k_ref, v_ref, do_ref, qa_ref, ka_ref, da_ref,
             dq_ref, dk_ref, dv_ref, drows_ref, dcols_ref, km_scr, vm_scr, dk_scr, dv_scr, dq_scr):
        pair, j = pl.program_id(0), pl.program_id(1)
        lane, halves = _head_halves(tq)

        @pl.when(jnp.logical_and(pair == 0, j == 0))
        def _():
            drows_ref[...] = jnp.zeros(drows_ref.shape, F32)
            dcols_ref[...] = jnp.zeros(dcols_ref.shape, F32)

        @pl.when(j == 0)
        def _():
            dq_scr[...] = jnp.zeros(dq_scr.shape, F32)

        v_ones = ((lane & (HEAD_DIM - 1)) < 3).astype(BF16)
        for a in range(2):
            km_scr[a] = jnp.where(halves[a], k_ref[...], ka_ref[...])
            vm_scr[a] = jnp.where(halves[a], v_ref[...], v_ones)
        dk_scr[...] = jnp.zeros(dk_scr.shape, F32)
        dv_scr[...] = jnp.zeros(dv_scr.shape, F32)

        def update(i, on_diagonal):
            rows = pl.ds(pl.multiple_of(i * tq, tq), tq)
            qs = q_ref[rows, :] * scale
            do2, qaug_t, doaug_t = do_ref[rows, :], qa_ref[rows, :], da_ref[rows, :]
            if on_diagonal:
                keep = (lax.broadcasted_iota(jnp.int32, (tq, tq), 0)
                        >= lax.broadcasted_iota(jnp.int32, (tq, tq), 1))
            qas = [jnp.where(halves[a], qs, qaug_t) for a in range(2)]
            logits = [lax.dot_general(qas[a], km_scr[a], NT, preferred_element_type=F32) for a in range(2)]
            dps = [lax.dot_general(jnp.where(halves[a], do2, doaug_t), vm_scr[a], NT, preferred_element_type=F32)
                   for a in range(2)]
            dv = None
            for a in range(2):
                s = jnp.where(keep, logits[a], NEG_INF) if on_diagonal else logits[a]
                p = jnp.exp(s)
                dsb = (p * dps[a]).astype(BF16)
                do0 = jnp.where(halves[a], do2, jnp.zeros_like(do2))
                dv_a = lax.dot_general(p.astype(BF16), do0, TN_DIMS, preferred_element_type=F32)
                dv = dv_a if dv is None else dv + dv_a
                dk_scr[a] += lax.dot_general(dsb, qas[a], TN_DIMS, preferred_element_type=F32)
                dq_scr[a, rows, :] += jnp.dot(dsb, km_scr[a], preferred_element_type=F32)
            dv_scr[...] += dv

        def below_diagonal(n, carry):
            update(j + 1 + 2 * n, False)
            update(j + 2 + 2 * n, False)
            return carry

        update(j, True)
        below = nq - 1 - j
        lax.fori_loop(0, below // 2, below_diagonal, 0)

        @pl.when(below % 2 == 1)
        def _():
            update(nq - 1, False)


        def to_head_lanes(old, first, second):
            at = lax.broadcasted_iota(jnp.int32, old.shape, 1) - 2 * pair
            return jnp.where(at == 0, first, jnp.where(at == 1, second, old))

        dk_ref[...] = jnp.where(halves[0], dk_scr[0], dk_scr[1]).astype(BF16)
        dv_ref[...] = dv_scr[...].astype(BF16)
        keys = pl.ds(pl.multiple_of(j * tq, tq), tq)
        ones_a, ones_b = HEAD_DIM + SLOT_ONE, SLOT_ONE
        dcols_ref[keys, :] = to_head_lanes(dcols_ref[keys, :], dk_scr[0][:, ones_a:ones_a + 1],
                                           dk_scr[1][:, ones_b:ones_b + 1])

        @pl.when(j == nq - 1)
        def _():
            row_lane, row_halves = _head_halves(S)
            dq_ref[...] = (jnp.where(row_halves[0], dq_scr[0], dq_scr[1]) * scale).astype(BF16)
            c_a, c_b = HEAD_DIM + SLOT_C, SLOT_C
            drows_ref[...] = to_head_lanes(drows_ref[...], dq_scr[0][:, c_a:c_a + 1], dq_scr[1][:, c_b:c_b + 1])

    tile = lambda col: pl.BlockSpec((tq, LANES), lambda p, j: (j, col + p))
    whole = lambda col: pl.BlockSpec((S, LANES), lambda p, j: (0, col + p))
    shared = pl.BlockSpec((S, LANES), lambda p, j: (0, 0))
    return pl.pallas_call(
        body, name="attn_bwd", grid=(P, nq),
        in_specs=[whole(qc), tile(kc), tile(vc), whole(P), whole(0), tile(0), whole(0)],
        out_specs=[whole(0), tile(0), tile(0), shared, shared],
        out_shape=[jax.ShapeDtypeStruct((S, W), BF16),
                   jax.ShapeDtypeStruct((S, W), BF16),
                   jax.ShapeDtypeStruct((S, W), BF16),
                   jax.ShapeDtypeStruct((S, LANES), F32),
                   jax.ShapeDtypeStruct((S, LANES), F32)],
        scratch_shapes=[pltpu.VMEM((2, tq, LANES), BF16),
                        pltpu.VMEM((2, tq, LANES), BF16),
                        pltpu.VMEM((2, tq, LANES), F32),
                        pltpu.VMEM((tq, LANES), F32),
                        pltpu.VMEM((2, S, LANES), F32)],
        compiler_params=_params("arbitrary", "arbitrary"),
    )(proj, proj, proj, da, qaug, kaug, doaug)


def _fgate_bwd(drows, dcols, z, bias):
    S = z.shape[0]
    tb = min(TB, S)
    nb = S // tb

    def body(drows_ref, dcols_ref, z_ref, b_ref, dz_ref, db_ref):
        tri = (lax.broadcasted_iota(jnp.int32, (tb, tb), 1)
               >= lax.broadcasted_iota(jnp.int32, (tb, tb), 0)).astype(F32)

        local = []
        for b in range(nb):
            rows = slice(b * tb, (b + 1) * tb)
            local.append(jnp.dot(tri, drows_ref[rows, :] - dcols_ref[rows, :], preferred_element_type=F32,
                                 precision=lax.Precision.HIGHEST))
        carry = jnp.zeros((1, LANES), F32)
        db = jnp.zeros((1, LANES), F32)
        for b in reversed(range(nb)):
            rows = slice(b * tb, (b + 1) * tb)
            rc = local[b] + carry
            carry = rc[0:1, :]
            dz = rc * _sigmoid(-(z_ref[rows, :] + b_ref[...]))
            dz_ref[rows, :] = dz.astype(BF16)
            db = db + jnp.sum(dz, axis=0, keepdims=True)
        db_ref[...] = db

    return pl.pallas_call(
        body, name="fgate_bwd",
        out_shape=[jax.ShapeDtypeStruct((S, LANES), BF16),
                   jax.ShapeDtypeStruct((1, LANES), F32)],
        compiler_params=pltpu.CompilerParams(vmem_limit_bytes=VMEM_LIMIT),
    )(drows, dcols, z, bias)


def _pool_bwd(proj, da, pool_w, pool_scale):
    S = proj.shape[0]
    G = len(POOL_WINDOWS)

    def body(u_ref, dy_ref, w_ref, s_ref, du_ref, dw_ref, ds_ref, pad_ref):
        g = pl.program_id(0)
        for gi, w in enumerate(POOL_WINDOWS):
            @pl.when(g == gi)
            def _():
                d, cnt = _window_mean_minus_self(u_ref[...].astype(F32), pad_ref, w, S)
                db = d.astype(BF16)
                wb = w_ref[0].astype(BF16)
                yraw = jnp.dot(db, wb, preferred_element_type=F32)
                dy = dy_ref[...].astype(F32)
                ds_ref[...] = jnp.sum(dy * yraw, axis=0, keepdims=True)
                dzb = (dy * s_ref[...]).astype(BF16)
                dw_ref[0] = lax.dot_general(db, dzb, TN_DIMS, preferred_element_type=F32)
                dd = lax.dot_general(dzb, wb, NT, preferred_element_type=F32)
                pad_ref[0:S, :] = dd / cnt
                pad_ref[S:S + MAX_WINDOW, :] = jnp.zeros((MAX_WINDOW, LANES), F32)
                acc = -dd
                for j in range(w):
                    acc = acc + pad_ref[j:j + S, :]
                du_ref[...] = acc.astype(BF16)

    return pl.pallas_call(
        body, name="pool_bwd", grid=(G,),
        in_specs=[pl.BlockSpec((S, LANES), lambda g: (0, g)),
                  pl.BlockSpec((S, LANES), lambda g: (0, g)),
                  pl.BlockSpec((1, LANES, LANES), lambda g: (g, 0, 0)),
                  pl.BlockSpec((1, LANES), lambda g: (0, g))],
        out_specs=[pl.BlockSpec((S, LANES), lambda g: (0, g)),
                   pl.BlockSpec((1, LANES, LANES), lambda g: (g, 0, 0)),
                   pl.BlockSpec((1, LANES), lambda g: (0, g))],
        out_shape=[jax.ShapeDtypeStruct((S, G * LANES), BF16),
                   jax.ShapeDtypeStruct((G, LANES, LANES), F32),
                   jax.ShapeDtypeStruct((1, G * LANES), F32)],
        scratch_shapes=[pltpu.VMEM((S + MAX_WINDOW, LANES), F32)],
        compiler_params=_params("arbitrary"),
    )(proj, da, pool_w, pool_scale)


def _inproj_bwd_dx(sections, dzf, w, x, gam, g, after=()):
    S, D = x.shape
    N = w.shape[1] - LANES
    W = D // 2
    n_sec = len(sections)
    tm = min(TM, S)

    def body(dz_ref, w_ref, wf_ref, x_ref, gam_ref, g_ref, *rest):
        sec_refs = rest[:n_sec]
        dx_ref, dg_ref = rest[-2:]

        @pl.when(pl.program_id(0) == 0)
        def _():
            dg_ref[...] = jnp.zeros(dg_ref.shape, F32)

        dh = lax.dot_general(dz_ref[...], wf_ref[...], NT, preferred_element_type=F32)
        for n, ref in enumerate(sec_refs):
            dh = dh + lax.dot_general(ref[...], w_ref[:, n * W:(n + 1) * W], NT, preferred_element_type=F32)
        xf = x_ref[...]
        r = lax.rsqrt(jnp.mean(xf * xf, axis=-1, keepdims=True) + RMS_EPS)
        xhat = xf * r
        dg_ref[...] += jnp.sum(dh * xhat, axis=0, keepdims=True)
        dxhat = dh * gam_ref[...]
        dx_ref[...] = g_ref[...] + r * (dxhat - xhat * jnp.mean(dxhat * xhat, axis=-1, keepdims=True))

    sec_specs, sec_arrays = _section_specs(sections, tm, W)
    return pl.pallas_call(
        body, name="inproj_bwd_dx", grid=(S // tm,),
        in_specs=[pl.BlockSpec((tm, LANES), lambda i: (i, 0)),
                  pl.BlockSpec((D, N), lambda i: (0, 0)),
                  pl.BlockSpec((D, LANES), lambda i: (0, N // LANES)),
                  pl.BlockSpec((tm, D), lambda i: (i, 0)),
                  pl.BlockSpec((1, D), lambda i: (0, 0)),
                  pl.BlockSpec((tm, D), lambda i: (i, 0))] + sec_specs + _after_specs(after),
        out_specs=[pl.BlockSpec((tm, D), lambda i: (i, 0)),
                   pl.BlockSpec((1, D), lambda i: (0, 0))],
        out_shape=[jax.ShapeDtypeStruct((S, D), F32),
                   jax.ShapeDtypeStruct((1, D), F32)],
        compiler_params=_params("arbitrary"),
    )(dzf, w, w, x, gam, g, *sec_arrays, *after)


def _adamw(w, m, v, gsets, name, rows, shifted=False, first=0, into=None):
    A, R, C = w.shape
    n_sets = len(gsets)
    tr = min(rows, R)
    c1 = 1.0 / (1.0 - ADAM_B1 ** ADAM_STEP)
    c2 = 1.0 / (1.0 - ADAM_B2 ** ADAM_STEP)
    counts = [len(gs) for gs in gsets]

    def body(w_ref, m_ref, v_ref, *rest):
        g_ref, d_ref, nm_ref, nv_ref = rest[-4:]
        at = 0
        for a in range(n_sets):
            part_refs = rest[at:at + counts[a]]
            at += counts[a]

            @pl.when(pl.program_id(0) == a)
            def _():
                g = None
                for ref in part_refs:
                    for s in range(ref.shape[0]):
                        term = ref[s].astype(F32)
                        g = term if g is None else g + term
                if shifted:
                    lanes = g.shape[1]
                    g = pltpu.roll(g, (lanes - _index(_position()) * (C % LANES)) % lanes, axis=1)[:, :C]
                nm = ADAM_B1 * m_ref[0] + (1.0 - ADAM_B1) * g
                nv = ADAM_B2 * v_ref[0] + (1.0 - ADAM_B2) * (g * g)
                g_ref[0] = g
                nm_ref[0] = nm
                nv_ref[0] = nv
                d_ref[0] = -ADAM_LR * ((nm * c1) / (jnp.sqrt(nv * c2) + ADAM_EPS) + ADAM_WD * w_ref[0])

    spec = pl.BlockSpec((1, tr, C), lambda a, r: (first + a, r, 0))
    part_specs = [pl.BlockSpec((part.shape[0], tr, part.shape[2]), lambda a, r, l=l: (0, jnp.where(a == l, r, 0), 0))
                  for l, gs in enumerate(gsets) for part in gs]
    parts = [part for gs in gsets for part in gs]
    shape = jax.ShapeDtypeStruct((A, R, C), F32)
    earlier = () if into is None else tuple(into)
    return pl.pallas_call(
        body, name=name, grid=(n_sets, R // tr),
        in_specs=[spec, spec, spec] + part_specs + _after_specs(earlier),
        out_specs=[spec, spec, spec, spec],
        out_shape=[shape, shape, shape, shape],
        input_output_aliases={3 + len(parts) + n: n for n in range(len(earlier))},
        compiler_params=_params("arbitrary", "arbitrary"),
    )(w, m, v, *parts, *earlier)


def _position():
    return lax.axis_index("x"), lax.axis_index("y"), lax.axis_index("c")


def _index(dev):
    return 4 * dev[0] + 2 * dev[1] + dev[2]


def _all_gather(arrs, slots, out_shapes, name):
    n_arr = len(arrs)

    def body(*refs):
        ins, outs = refs[:n_arr], refs[n_arr:2 * n_arr]
        send_sems, recv_sems, local_sems = refs[2 * n_arr:]
        x, y, c = _position()
        me, sibling = (x, y, c), (x, y, 1 - c)
        chips = [(1 - x, y), (x, 1 - y), (1 - x, 1 - y)]

        def copy(a, k, block, to, src=None):
            part = slots[a](outs[a], _index(block))
            return pltpu.make_async_remote_copy(
                src_ref=part if src is None else src, dst_ref=part,
                send_sem=send_sems.at[a, k], recv_sem=recv_sems.at[a, k],
                device_id=to, device_id_type=MESH)

        mine = [pltpu.make_async_copy(ins[a], slots[a](outs[a], _index(me)), local_sems.at[a])
                for a in range(n_arr)]
        for cp in mine:
            cp.start()
        first = []
        for a in range(n_arr):
            first.append(copy(a, 0, me, sibling, src=ins[a]))
            first += [copy(a, 1 + j, me, (*chip, c), src=ins[a]) for j, chip in enumerate(chips)]
        for cp in first:
            cp.start()
        passed = []
        for j, chip in enumerate(chips):
            for a in range(n_arr):
                copy(a, 1 + j, (*chip, c), me).wait_recv()
                fwd = copy(a, 4 + j, (*chip, c), sibling)
                fwd.start()
                passed.append(fwd)
        for a in range(n_arr):
            copy(a, 0, sibling, me).wait_recv()
            for j, chip in enumerate(chips):
                copy(a, 4 + j, (*chip, 1 - c), me).wait_recv()
        for cp in first + passed:
            cp.wait_send()
        for cp in mine:
            cp.wait()

    any_spec = pl.BlockSpec(memory_space=pl.ANY)
    return pl.pallas_call(
        body, name=name,
        in_specs=[any_spec] * n_arr, out_specs=[any_spec] * n_arr, out_shape=out_shapes,
        scratch_shapes=[pltpu.SemaphoreType.DMA((n_arr, 7)), pltpu.SemaphoreType.DMA((n_arr, 7)),
                        pltpu.SemaphoreType.DMA((n_arr,))],
    )(*arrs)


def _split_copies(srcs, lands, send_sems, recv_sems, kinds):
    x, y, c = _position()
    me = _index((x, y, c))
    copies = []
    for a, (src_part, land_part) in enumerate(kinds):
        for k in range(1, N_DEV):
            peer = (x ^ ((k >> 2) & 1), y ^ ((k >> 1) & 1), c ^ (k & 1))
            copies.append(pltpu.make_async_remote_copy(
                src_ref=src_part(srcs[a], _index(peer)), dst_ref=land_part(lands[a], me, k),
                send_sem=send_sems[a].at[k - 1], recv_sem=recv_sems[a].at[k - 1],
                device_id=peer, device_id_type=MESH))
    return copies


def _split_start(srcs, lands, kinds, name, after=()):
    n = len(srcs)

    def body(*refs):
        src_refs, land_refs = refs[:n], refs[n:2 * n]
        outs = refs[2 * n + len(after):]
        send_sems, recv_sems = outs[:n], outs[n:2 * n]
        token = outs[-1]
        for cp in _split_copies(src_refs, land_refs, send_sems, recv_sems, kinds):
            cp.start()
        token[...] = jnp.zeros(token.shape, token.dtype)

    hbm = pl.BlockSpec(memory_space=pltpu.HBM)
    sem = pl.BlockSpec(memory_space=pltpu.SEMAPHORE)
    operands = [pltpu.with_memory_space_constraint(t, pltpu.HBM) for t in (*srcs, *lands)]
    out = pl.pallas_call(
        body, name=name,
        in_specs=[hbm] * (2 * n) + _after_specs(after),
        out_specs=[sem] * (2 * n) + [hbm] * (2 * n) + [pl.BlockSpec(memory_space=pltpu.VMEM)],
        out_shape=[pltpu.SemaphoreType.DMA((N_DEV - 1,))] * (2 * n)
        + [pltpu.HBM(t.shape, t.dtype) for t in operands] + [jax.ShapeDtypeStruct((8, LANES), F32)],
        input_output_aliases={i: 2 * n + i for i in range(2 * n)},
        compiler_params=pltpu.CompilerParams(has_side_effects=pltpu.SideEffectType.DATAFLOW_SIDE_EFFECTING),
    )(*operands, *after)
    return [(out[a], out[n + a], out[2 * n + a], out[3 * n + a]) for a in range(n)], out[-1]


def _split_wait(started, kinds, after, name):
    n = len(started)
    sems = [t[0] for t in started] + [t[1] for t in started]
    srcs = [t[2] for t in started]
    lands = [t[3] for t in started]

    def body(*refs):
        src_refs, land_refs = refs[:n], refs[n:2 * n]
        send_sems, recv_sems = refs[2 * n:3 * n], refs[3 * n:4 * n]
        for cp in _split_copies(src_refs, land_refs, send_sems, recv_sems, kinds):
            cp.wait_send()
            cp.wait_recv()

    hbm = pl.BlockSpec(memory_space=pltpu.HBM)
    sem = pl.BlockSpec(memory_space=pltpu.SEMAPHORE)
    out = pl.pallas_call(
        body, name=name,
        in_specs=[hbm] * (2 * n) + [sem] * (2 * n) + _after_specs(after),
        out_specs=[hbm] * (2 * n),
        out_shape=[pltpu.HBM(t.shape, t.dtype) for t in (*srcs, *lands)],
        input_output_aliases={i: i for i in range(2 * n)},
        compiler_params=pltpu.CompilerParams(has_side_effects=pltpu.SideEffectType.DATAFLOW_SIDE_EFFECTING),
    )(*srcs, *lands, *sems, *after)
    return out[n:]


def _as_rows(p):
    if p.size % LANES == 0:
        rows = p.reshape(-1, LANES)
    else:
        rows = p.reshape(-1, p.shape[-1])
        rows = jnp.pad(rows, ((0, 0), (0, LANES - rows.shape[1])))
    return jnp.pad(rows, ((0, -rows.shape[0] % 8), (0, 0)))


def _pack(parts):
    return jnp.concatenate([_as_rows(p) for p in parts])[None]


def _unpack(packed, like):
    out, at = [], 0
    for p in like:
        whole = p.size % LANES == 0
        n = p.size // LANES if whole else p.size // p.shape[-1]
        rows = packed[0, at:at + n]
        out.append((rows if whole else rows[:, :p.shape[-1]]).reshape(p.shape))
        at += n + (-n % 8)
    return out


def _local_step(x, target, norm_g, forget_bias, pool_w, pool_scale, final_g, weights_in, weights_out, on_grads,
                first_after=()):
    L = norm_g.shape[0]
    S, D = x.shape
    W = D // 2
    H = W // HEAD_DIM
    bias = jnp.pad(forget_bias, ((0, 0), (0, LANES - H)))

    saved = []
    after = tuple(first_after)
    for l in range(L):
        proj, h, z, w = _inproj_fwd(x, norm_g[l:l + 1], weights_in(l, x), after)
        after = ()
        qaug, kaug = _fgate_fwd(z, bias[l:l + 1], H)
        ypool = _pool_fwd(proj, pool_w[l], pool_scale[l:l + 1])
        o, qaug_b = _attn_fwd(proj, qaug, kaug)
        wout = weights_out(l, o)
        x_new, mixed = _outproj_fwd(ypool, o, proj, x, wout)
        saved.append((x, proj, h, z, qaug_b, kaug, ypool, o, mixed, w, wout))
        x = x_new

    g, loss, d_final_g = _loss_head(x, final_g.reshape(1, D), target)

    small = None
    for l in reversed(range(L)):
        x_in, proj, h, z, qaug_b, kaug, ypool, o, mixed, w, wout = saved[l]
        d_wout, d_wout_bf16, da, dgate, doaug = _outproj_bwd(g, wout, mixed, ypool, o, proj)
        dq, dk, dv, drows, dcols = _attn_bwd(proj, da, qaug_b, kaug, doaug)
        dzf, db = _fgate_bwd(drows, dcols, z, bias[l:l + 1])
        dpu, dpw, dps = _pool_bwd(proj, da, pool_w[l], pool_scale[l:l + 1])
        dproj = [(dpu, 0), (dgate, 0), (dq, 0), (dk, 0), (dv, 0), (dgate, 1)]
        d_w, d_w_bf16 = _inproj_bwd_dw(h, dproj, dzf)
        after = tuple(on_grads(l, d_w, d_w_bf16, d_wout, d_wout_bf16, small))
        g, dgam = _inproj_bwd_dx(dproj, dzf, w, x_in, norm_g[l:l + 1], g, after)
        small = (dgam[0], db[0, :H], dpw, dps[0])
    return loss[0, 0], g, small, d_final_g[0]


def kernel(x, norm_g, w_in, forget_bias, pool_w, pool_scale, w_out, final_g, loss_target, m_norm_g, m_w_in, m_forget_bias, m_pool_w, m_pool_scale, m_w_out, m_final_g, v_norm_g, v_w_in, v_forget_bias, v_pool_w, v_pool_scale, v_w_out, v_final_g):
    L, D, cols = w_in.shape
    rows_out = w_out.shape[1]
    me = _index(_position())
    slot = _slot_width(cols)
    wout_b = w_out.astype(BF16)
    win_b = _shift_slots(w_in)
    gather_in = (lambda ref, peer: ref, lambda ref, mine, k: ref.at[mine])
    gather_out = (lambda ref, peer: ref, lambda ref, mine, k: ref.at[pl.ds(mine * rows_out, rows_out), :])

    def landing(block, n_slots):
        zone = lax.empty((n_slots * block.shape[0], *block.shape[1:]), block.dtype)
        return lax.dynamic_update_slice(zone, block, (me * block.shape[0],) + (0,) * (block.ndim - 1))

    (first_in,) = _all_gather([win_b[0]], [lambda ref, n: ref.at[n]],
                              [jax.ShapeDtypeStruct((N_DEV, D, slot), BF16)], "gather_first")
    rest_srcs = [wout_b[0]] + [w[l] for l in range(1, L) for w in (win_b, wout_b)]
    rest_lands = [landing(wout_b[0], N_DEV)]
    for l in range(1, L):
        rest_lands += [landing(win_b[l][None], N_DEV), landing(wout_b[l], N_DEV)]
    rest_kinds = [gather_out] + [gather_in, gather_out] * (L - 1)
    rest, rest_token = _split_start(rest_srcs, rest_lands, rest_kinds, "gather_start_rest", (first_in,))

    def weights_in(l, x_in):
        if l == 0:
            return first_in
        (win_all,) = _split_wait([rest[2 * l - 1]], [gather_in], (x_in,), f"gather_wait_in_{l}")
        return win_all

    def weights_out(l, o):
        (wout_full,) = _split_wait([rest[2 * l]], [gather_out], (o,), f"gather_wait_out_{l}")
        return wout_full

    stride = slot - LANES
    exchange_kinds = [(lambda ref, peer: ref.at[:, pl.ds(pl.multiple_of(peer * stride, LANES), slot)],
                       lambda ref, mine, k: ref.at[k - 1]),
                      (lambda ref, peer: ref.at[pl.ds(peer * rows_out, rows_out), :],
                       lambda ref, mine, k: ref.at[k - 1])]
    zero_g = jnp.zeros_like(final_g)
    zero_loss = jnp.zeros((LANES,), F32)

    def small_pack(l, norm_g_l, bias_l, pool_w_l, pool_scale_l, final, loss_row=None):
        return _pack([norm_g_l, bias_l, pool_w_l, pool_scale_l, final if l == 0 else zero_g,
                      zero_loss if loss_row is None else loss_row])[0]

    exchanges, own_parts = {}, {}

    def on_grads(l, dw, dw_bf16, d_wout, d_wout_bf16, small):
        own_parts[l] = (lax.dynamic_slice_in_dim(dw, me * stride, slot, 1)[None],
                        lax.dynamic_slice_in_dim(d_wout, me * rows_out, rows_out, 0)[None])
        srcs = [dw_bf16, d_wout_bf16]
        lands = [lax.empty((N_DEV - 1, D, slot), BF16), lax.empty((N_DEV - 1, rows_out, D), BF16)]
        kinds = list(exchange_kinds)
        if small is not None:
            packed_small = small_pack(l + 1, *small, None)
            srcs.append(packed_small)
            lands.append(landing(packed_small[None], N_DEV))
            kinds.append(gather_in)
        started, token = _split_start(srcs, lands, kinds, f"exchange_start_{l}")
        exchanges[l] = (started, kinds)
        return (token,)

    loss, dx, small_first, d_final_g = _local_step(
        x[0], loss_target[0], norm_g, forget_bias, pool_w, pool_scale, final_g,
        weights_in, weights_out, on_grads, (rest_token,))
    packed_first = small_pack(0, *small_first, d_final_g, jnp.full((LANES,), loss, F32))
    first_started, first_token = _split_start(
        [packed_first], [landing(packed_first[None], N_DEV)], [gather_in],
        "small_start", (w_in, m_w_in, v_w_in, *own_parts[0]))

    gin_sets, gout_sets, small_sets = [None] * L, [None] * L, [None] * L

    def wait_for(l, after):
        started, kinds = exchanges[l]
        got = _split_wait(started, kinds, after, f"exchange_wait_{l}")
        gin_sets[l] = [own_parts[l][0], got[0]]
        gout_sets[l] = [own_parts[l][1], got[1]]
        if len(got) > 2:
            small_sets[l + 1] = [got[2]]

    for l in range(1, L):
        wait_for(l, (dx, first_token))
    rest_in = _adamw(w_in, m_w_in, v_w_in, gin_sets[1:], "adamw_w_in_rest", TM // 2, shifted=True, first=1)
    rest_out = _adamw(w_out, m_w_out, v_w_out, gout_sets[1:], "adamw_w_out_rest", rows_out, first=1)
    wait_for(0, (rest_in[1], rest_out[1]))
    g_w_in, d_w_in, nm_w_in, nv_w_in = _adamw(w_in, m_w_in, v_w_in, gin_sets[:1], "adamw_w_in_first", TM // 2,
                                              shifted=True, into=rest_in)
    g_w_out, d_w_out, nm_w_out, nv_w_out = _adamw(w_out, m_w_out, v_w_out, gout_sets[:1], "adamw_w_out_first",
                                                  rows_out, into=rest_out)
    small_sets[0] = _split_wait(first_started, [gather_in], (d_w_in, d_w_out), "small_wait")
    loss = jnp.sum(small_sets[0][0][:, packed_first.shape[0] - 8, 0])

    def small_stack(norm_g_, bias_, pool_w_, pool_scale_, final):
        return jnp.stack([small_pack(l, norm_g_[l], bias_[l], pool_w_[l], pool_scale_[l], final) for l in range(L)])

    packed = _adamw(small_stack(norm_g, forget_bias, pool_w, pool_scale, final_g),
                    small_stack(m_norm_g, m_forget_bias, m_pool_w, m_pool_scale, m_final_g),
                    small_stack(v_norm_g, v_forget_bias, v_pool_w, v_pool_scale, v_final_g),
                    small_sets, "adamw_small", packed_first.shape[0])

    def small_unpack(p):
        like = [norm_g[0], forget_bias[0], pool_w[0], pool_scale[0], final_g]
        layers = [_unpack(p[l:l + 1], like) for l in range(L)]
        return [jnp.stack([layers[l][n] for l in range(L)]) for n in range(4)] + [layers[0][4]]

    g_s, d_s, nm_s, nv_s = [small_unpack(p) for p in packed]

    def order(big_in, big_out, small):
        return (small[0], big_in, small[1], small[2], small[3], big_out, small[4])

    return (loss, dx[None], *order(g_w_in, g_w_out, g_s), *order(d_w_in, d_w_out, d_s),
            *order(nm_w_in, nm_w_out, nm_s), *order(nv_w_in, nv_w_out, nv_s))
```

```python
import math

import jax
import jax.numpy as jnp
from jax import lax
from jax.experimental import pallas as pl
from jax.experimental.pallas import tpu as pltpu

F32 = jnp.float32
BF16 = jnp.bfloat16
MESH = pl.DeviceIdType.MESH

RMS_EPS = 1e-6
NEG_INF = -1e30
HEAD_DIM = 64
POOL_WINDOWS = (2, 4, 8, 16)
MAX_WINDOW = 16
LANES = 128
N_DEV = 8

ADAM_LR = 0.001
ADAM_B1 = 0.9
ADAM_B2 = 0.999
ADAM_EPS = 1e-08
ADAM_WD = 0.01
ADAM_STEP = 10

TM = 512
TN = 512
TQ = 512
TB = 256
VMEM_LIMIT = 56 * 1024 * 1024

NT = (((1,), (1,)), ((), ()))
TN_DIMS = (((0,), (0,)), ((), ()))

SLOT_C, SLOT_ONE, SLOT_LSE = 0, 3, 6


def _params(*sem):
    return pltpu.CompilerParams(dimension_semantics=sem, vmem_limit_bytes=VMEM_LIMIT)


def _sigmoid(x):
    return 1.0 / (1.0 + jnp.exp(-x))


def _split3(x):
    hi = x.astype(BF16).astype(F32)
    rest = x - hi
    mid = rest.astype(BF16).astype(F32)
    return hi, mid, rest - mid


def _after_specs(after):
    return [pl.BlockSpec(memory_space=pl.ANY)] * len(after)


def _slot_width(cols):
    return LANES * (-(-(cols + (N_DEV - 1) * (cols % LANES)) // LANES))


def _shift_slots(w_in):
    L, D, cols = w_in.shape
    slot = _slot_width(cols)
    tr = min(TM // 2, D)

    def body(w_ref, o_ref, pad_scr):
        pad_scr[...] = jnp.zeros(pad_scr.shape, F32)
        pad_scr[:, 0:cols] = w_ref[0]
        o_ref[0] = pltpu.roll(pad_scr[...], _index(_position()) * (cols % LANES), axis=1).astype(BF16)

    return pl.pallas_call(
        body, name="shift_slots", grid=(L, D // tr),
        in_specs=[pl.BlockSpec((1, tr, cols), lambda l, r: (l, r, 0))],
        out_specs=pl.BlockSpec((1, tr, slot), lambda l, r: (l, r, 0)),
        out_shape=jax.ShapeDtypeStruct((L, D, slot), BF16),
        scratch_shapes=[pltpu.VMEM((tr, slot), F32)],
        compiler_params=_params("parallel", "parallel"),
    )(w_in)


def _inproj_fwd(x, gam, slots, after=()):
    S, D = x.shape
    n_dev, _, sw = slots.shape
    stride = sw - LANES
    width = stride * n_dev + LANES
    N = width - LANES
    tm, tn = min(TM, S), TN

    def body(x_ref, g_ref, s_ref, *rest):
        proj_ref, h_ref, z_ref, w_ref = rest[-4:]

        @pl.when(pl.program_id(0) == 0)
        def _():
            for n in range(n_dev):
                base = stride * n
                first = s_ref[n, :, 0:LANES]
                if n > 0:
                    first = first + s_ref[n - 1, :, stride:sw]
                w_ref[:, base:base + LANES] = first
                w_ref[:, base + LANES:base + stride] = s_ref[n, :, LANES:stride]
            w_ref[:, stride * n_dev:width] = s_ref[n_dev - 1, :, stride:sw]

        xf = x_ref[...]
        r = lax.rsqrt(jnp.mean(xf * xf, axis=-1, keepdims=True) + RMS_EPS)
        h = ((xf * r) * g_ref[...]).astype(BF16)
        h_ref[...] = h
        z_ref[...] = jnp.dot(h, w_ref[:, N:width], preferred_element_type=F32)
        for n in range(N // tn):
            cols = slice(n * tn, (n + 1) * tn)
            proj_ref[:, cols] = jnp.dot(h, w_ref[:, cols], preferred_element_type=F32).astype(BF16)

    return pl.pallas_call(
        body, name="inproj_fwd", grid=(S // tm,),
        in_specs=[pl.BlockSpec((tm, D), lambda i: (i, 0)),
                  pl.BlockSpec((1, D), lambda i: (0, 0)),
                  pl.BlockSpec((n_dev, D, sw), lambda i: (0, 0, 0))] + _after_specs(after),
        out_specs=[pl.BlockSpec((tm, N), lambda i: (i, 0)),
                   pl.BlockSpec((tm, D), lambda i: (i, 0)),
                   pl.BlockSpec((tm, LANES), lambda i: (i, 0)),
                   pl.BlockSpec((D, width), lambda i: (0, 0))],
        out_shape=[jax.ShapeDtypeStruct((S, N), BF16),
                   jax.ShapeDtypeStruct((S, D), BF16),
                   jax.ShapeDtypeStruct((S, LANES), F32),
                   jax.ShapeDtypeStruct((D, width), BF16)],
        compiler_params=_params("arbitrary"),
    )(x, gam, slots, *after)


def _fgate_fwd(z, bias, n_heads):
    S = z.shape[0]
    tb = min(TB, S)
    P = n_heads // 2
    assert n_heads <= 8, "the three parts of c are packed eight lanes apart"

    def body(z_ref, b_ref, qaug_ref, kaug_ref):
        lane = lax.broadcasted_iota(jnp.int32, (tb, LANES), 1)
        tri = (lax.broadcasted_iota(jnp.int32, (tb, tb), 0)
               >= lax.broadcasted_iota(jnp.int32, (tb, tb), 1)).astype(F32)
        row = lax.broadcasted_iota(jnp.int32, (LANES, P * LANES), 0)
        col = lax.broadcasted_iota(jnp.int32, (LANES, P * LANES), 1)
        head, part_n = row & 7, row >> 3
        home = (head >> 1) * LANES + jnp.where((head & 1) == 0, HEAD_DIM, 0)
        is_part = jnp.logical_and(head < n_heads, part_n < 3)
        place_q = jnp.logical_and(is_part, col == home + SLOT_C + part_n).astype(BF16)
        place_k = jnp.logical_and(is_part, col == home + SLOT_ONE + part_n).astype(BF16)
        slot = lax.broadcasted_iota(jnp.int32, (tb, P * LANES), 1) & (HEAD_DIM - 1)
        q_ones = jnp.logical_and(slot >= SLOT_ONE, slot < SLOT_ONE + 3).astype(F32)
        k_ones = jnp.logical_or(slot < SLOT_C + 3,
                                jnp.logical_and(slot >= SLOT_LSE, slot < SLOT_LSE + 3)).astype(F32)

        local = []
        for b in range(S // tb):
            zz = z_ref[b * tb:(b + 1) * tb, :] + b_ref[...]
            lf = jnp.minimum(zz, 0.0) - jnp.log(1.0 + jnp.exp(-jnp.abs(zz)))
            lf = jnp.where(lane < n_heads, lf, 0.0)
            local.append(jnp.dot(tri, lf, preferred_element_type=F32, precision=lax.Precision.HIGHEST))
        carry = jnp.zeros((1, LANES), F32)
        for b, part_sum in enumerate(local):
            c = part_sum + carry
            carry = c[tb - 1:tb, :]
            hi, mid, lo = _split3(c)
            packed = (hi + pltpu.roll(mid, 8, axis=1) + pltpu.roll(lo, 16, axis=1)).astype(BF16)
            qaug_ref[b * tb:(b + 1) * tb, :] = (
                q_ones + jnp.dot(packed, place_q, preferred_element_type=F32)).astype(BF16)
            kaug_ref[b * tb:(b + 1) * tb, :] = (
                k_ones - jnp.dot(packed, place_k, preferred_element_type=F32)).astype(BF16)

    return pl.pallas_call(
        body, name="fgate_fwd",
        out_shape=[jax.ShapeDtypeStruct((S, P * LANES), BF16),
                   jax.ShapeDtypeStruct((S, P * LANES), BF16)],
        compiler_params=pltpu.CompilerParams(vmem_limit_bytes=VMEM_LIMIT),
    )(z, bias)


def _window_mean_minus_self(u, pad_ref, w, S):
    pad_ref[0:MAX_WINDOW, :] = jnp.zeros((MAX_WINDOW, LANES), F32)
    pad_ref[MAX_WINDOW:MAX_WINDOW + S, :] = u
    acc = u
    for j in range(1, w):
        acc = acc + pad_ref[MAX_WINDOW - j:MAX_WINDOW - j + S, :]
    t = lax.broadcasted_iota(jnp.int32, (S, LANES), 0)
    cnt = jnp.minimum(t + 1, w).astype(F32)
    return acc / cnt - u, cnt


def _pool_fwd(proj, pool_w, pool_scale):
    S = proj.shape[0]
    G = len(POOL_WINDOWS)

    def body(u_ref, w_ref, s_ref, y_ref, pad_ref):
        g = pl.program_id(0)
        for gi, w in enumerate(POOL_WINDOWS):
            @pl.when(g == gi)
            def _():
                d, _ = _window_mean_minus_self(u_ref[...].astype(F32), pad_ref, w, S)
                y = jnp.dot(d.astype(BF16), w_ref[0].astype(BF16), preferred_element_type=F32)
                y_ref[...] = (y * s_ref[...]).astype(BF16)

    return pl.pallas_call(
        body, name="pool_fwd", grid=(G,),
        in_specs=[pl.BlockSpec((S, LANES), lambda g: (0, g)),
                  pl.BlockSpec((1, LANES, LANES), lambda g: (g, 0, 0)),
                  pl.BlockSpec((1, LANES), lambda g: (0, g))],
        out_specs=pl.BlockSpec((S, LANES), lambda g: (0, g)),
        out_shape=jax.ShapeDtypeStruct((S, G * LANES), BF16),
        scratch_shapes=[pltpu.VMEM((S + MAX_WINDOW, LANES), F32)],
        compiler_params=_params("arbitrary"),
    )(proj, pool_w, pool_scale)


def _head_halves(rows):
    lane = lax.broadcasted_iota(jnp.int32, (rows, LANES), 1)
    return lane, (lane < HEAD_DIM, lane >= HEAD_DIM)


def _attn_fwd(proj, qaug, kaug):
    S = proj.shape[0]
    W = proj.shape[1] // 6
    P = W // LANES
    tq = min(TQ, S)
    nq = S // tq
    qc, kc, vc = 2 * P, 3 * P, 4 * P
    scale = 1.0 / math.sqrt(HEAD_DIM)

    def body(q_ref, k_ref, v_ref, qa_ref, ka_ref, o_ref, qb_ref, qm_scr, m_scr, acc_scr):
        i = pl.program_id(1)
        lane, halves = _head_halves(tq)
        v_ones = ((lane & (HEAD_DIM - 1)) < 3).astype(BF16)
        qs = q_ref[...] * scale
        qm_scr[0] = jnp.where(halves[0], qs, qa_ref[...])
        qm_scr[1] = jnp.where(halves[1], qs, qa_ref[...])
        m_scr[...] = jnp.full(m_scr.shape, NEG_INF, F32)
        acc_scr[...] = jnp.zeros(acc_scr.shape, F32)

        def update(j, on_diagonal):
            keys = pl.ds(pl.multiple_of(j * tq, tq), tq)
            k2, v2, kaug_t = k_ref[keys, :], v_ref[keys, :], ka_ref[keys, :]
            if on_diagonal:
                keep = (lax.broadcasted_iota(jnp.int32, (tq, tq), 0)
                        >= lax.broadcasted_iota(jnp.int32, (tq, tq), 1))
            logits = [lax.dot_general(qm_scr[a], jnp.where(halves[a], k2, kaug_t), NT, preferred_element_type=F32)
                      for a in range(2)]
            for a in range(2):
                s = jnp.where(keep, logits[a], NEG_INF) if on_diagonal else logits[a]
                va = jnp.where(halves[a], v2, v_ones)
                m_prev = m_scr[a]
                m_new = jnp.maximum(m_prev, jnp.max(s, axis=1, keepdims=True))
                p = jnp.exp(s - jnp.tile(m_new, (1, tq // LANES)))
                acc_scr[a] = jnp.exp(m_prev - m_new) * acc_scr[a] + jnp.dot(p.astype(BF16), va,
                                                                              preferred_element_type=F32)
                m_scr[a] = m_new

        def below_diagonal(jj, carry):
            update(2 * jj, False)
            update(2 * jj + 1, False)
            return carry

        lax.fori_loop(0, i // 2, below_diagonal, 0)

        @pl.when(i % 2 == 1)
        def _():
            update(i - 1, False)

        update(i, True)
        acc_a, acc_b = acc_scr[0], acc_scr[1]
        l_a, l_b = acc_a[:, HEAD_DIM:HEAD_DIM + 1], acc_b[:, 0:1]
        o_ref[...] = jnp.where(halves[0], acc_a / l_a, acc_b / l_b).astype(BF16)
        lse = jnp.where(halves[0], m_scr[1] + jnp.log(l_b), m_scr[0] + jnp.log(l_a))
        slot = lane & (HEAD_DIM - 1)
        aug = qa_ref[...].astype(F32)
        for n, part in enumerate(_split3(lse)):
            aug = jnp.where(slot == SLOT_LSE + n, -part, aug)
        qb_ref[...] = aug.astype(BF16)

    tile = lambda col: pl.BlockSpec((tq, LANES), lambda p, i: (i, col + p))
    whole = lambda col: pl.BlockSpec((S, LANES), lambda p, i: (0, col + p))
    return pl.pallas_call(
        body, name="attn_fwd", grid=(P, nq),
        in_specs=[tile(qc), whole(kc), whole(vc), tile(0), whole(0)],
        out_specs=[tile(0), tile(0)],
        out_shape=[jax.ShapeDtypeStruct((S, W), BF16), jax.ShapeDtypeStruct((S, W), BF16)],
        scratch_shapes=[pltpu.VMEM((2, tq, LANES), BF16),
                        pltpu.VMEM((2, tq, LANES), F32),
                        pltpu.VMEM((2, tq, LANES), F32)],
        compiler_params=_params("parallel", "arbitrary"),
    )(proj, proj, proj, qaug, kaug)


def _outproj_fwd(ypool, o, proj, x, wout):
    S, D = x.shape
    W = D // 2
    tm, tn = min(TM, S), TN

    def body(y_ref, o_ref, pg_ref, ag_ref, x_ref, w_ref, xn_ref, mix_ref):
        pg, ag = pg_ref[...].astype(F32), ag_ref[...].astype(F32)
        mix_ref[:, 0:W] = (y_ref[...].astype(F32) * (pg * _sigmoid(pg))).astype(BF16)
        mix_ref[:, W:D] = (o_ref[...].astype(F32) * (ag * _sigmoid(ag))).astype(BF16)
        for n in range(D // tn):
            cols = slice(n * tn, (n + 1) * tn)
            xn_ref[:, cols] = x_ref[:, cols] + jnp.dot(mix_ref[...], w_ref[:, cols], preferred_element_type=F32)

    return pl.pallas_call(
        body, name="outproj_fwd", grid=(S // tm,),
        in_specs=[pl.BlockSpec((tm, W), lambda i: (i, 0)),
                  pl.BlockSpec((tm, W), lambda i: (i, 0)),
                  pl.BlockSpec((tm, W), lambda i: (i, 1)),
                  pl.BlockSpec((tm, W), lambda i: (i, 5)),
                  pl.BlockSpec((tm, D), lambda i: (i, 0)),
                  pl.BlockSpec((D, D), lambda i: (0, 0))],
        out_specs=[pl.BlockSpec((tm, D), lambda i: (i, 0)),
                   pl.BlockSpec((tm, D), lambda i: (i, 0))],
        out_shape=[jax.ShapeDtypeStruct((S, D), F32),
                   jax.ShapeDtypeStruct((S, D), BF16)],
        compiler_params=_params("parallel"),
    )(ypool, o, proj, proj, x, wout)


def _loss_head(x, gam, target):
    S, D = x.shape
    tm = min(TM, S)

    def body(x_ref, g_ref, t_ref, dx_ref, loss_ref, dg_ref):
        @pl.when(pl.program_id(0) == 0)
        def _():
            loss_ref[...] = jnp.zeros(loss_ref.shape, F32)
            dg_ref[...] = jnp.zeros(dg_ref.shape, F32)

        xf, gam_v = x_ref[...], g_ref[...]
        r = lax.rsqrt(jnp.mean(xf * xf, axis=-1, keepdims=True) + RMS_EPS)
        xhat = xf * r
        err = xhat * gam_v - t_ref[...]
        part = jnp.sum(jnp.sum(err * err, axis=-1, keepdims=True), axis=0, keepdims=True)
        loss_ref[...] += part * (0.5 / D)
        dy = err * (1.0 / D)
        dg_ref[...] += jnp.sum(dy * xhat, axis=0, keepdims=True)
        dxhat = dy * gam_v
        dx_ref[...] = r * (dxhat - xhat * jnp.mean(dxhat * xhat, axis=-1, keepdims=True))

    return pl.pallas_call(
        body, name="loss_head", grid=(S // tm,),
        in_specs=[pl.BlockSpec((tm, D), lambda i: (i, 0)),
                  pl.BlockSpec((1, D), lambda i: (0, 0)),
                  pl.BlockSpec((tm, D), lambda i: (i, 0))],
        out_specs=[pl.BlockSpec((tm, D), lambda i: (i, 0)),
                   pl.BlockSpec((8, LANES), lambda i: (0, 0)),
                   pl.BlockSpec((1, D), lambda i: (0, 0))],
        out_shape=[jax.ShapeDtypeStruct((S, D), F32),
                   jax.ShapeDtypeStruct((8, LANES), F32),
                   jax.ShapeDtypeStruct((1, D), F32)],
        compiler_params=_params("arbitrary"),
    )(x, gam, target)


def _outproj_bwd(g, wout, mixed, ypool, o, proj):
    S, D = g.shape
    W = D // 2
    tm = min(TM, S)

    def body(g_ref, w_ref, mix_ref, y_ref, o_ref, pg_ref, ag_ref, dw_ref, dwb_ref, da_ref, dgate_ref, doaug_ref):
        @pl.when(pl.program_id(0) == 0)
        def _():
            dw_ref[...] = jnp.zeros(dw_ref.shape, F32)

        gb = g_ref[...].astype(BF16)
        dmixes = [lax.dot_general(gb, w_ref[half * W:(half + 1) * W, :], NT, preferred_element_type=F32)
                  for half in range(2)]
        dw_ref[...] += lax.dot_general(mix_ref[...], gb, TN_DIMS, preferred_element_type=F32)
        d_o = None
        for half, (val_ref, gate_ref) in enumerate(((y_ref, pg_ref), (o_ref, ag_ref))):
            cols = slice(half * W, (half + 1) * W)
            gt = gate_ref[...].astype(F32)
            sg = _sigmoid(gt)
            d_o = (dmixes[half] * (gt * sg)).astype(BF16)
            da_ref[:, cols] = d_o
            dgate_ref[:, cols] = (dmixes[half] * val_ref[...].astype(F32)
                                  * (sg * (1.0 + gt * (1.0 - sg)))).astype(BF16)

        lane, halves = _head_halves(tm)
        slot = lane & (HEAD_DIM - 1)
        for p in range(W // LANES):
            cols = slice(p * LANES, (p + 1) * LANES)
            prod = d_o[:, cols].astype(F32) * o_ref[:, cols].astype(F32)
            d_a = jnp.sum(jnp.where(halves[0], prod, 0.0), axis=1, keepdims=True)
            d_b = jnp.sum(jnp.where(halves[1], prod, 0.0), axis=1, keepdims=True)
            aug = jnp.zeros((tm, LANES), F32)
            for n, part in enumerate(_split3(jnp.where(halves[0], d_b, d_a))):
                aug = jnp.where(slot == SLOT_C + n, -part, aug)
            doaug_ref[:, cols] = aug.astype(BF16)

        @pl.when(pl.program_id(0) == S // tm - 1)
        def _():
            dwb_ref[...] = dw_ref[...].astype(BF16)

    rows = lambda width, col: pl.BlockSpec((tm, width), lambda i: (i, col))
    whole = pl.BlockSpec((D, D), lambda i: (0, 0))
    return pl.pallas_call(
        body, name="outproj_bwd", grid=(S // tm,),
        in_specs=[rows(D, 0), whole, rows(D, 0), rows(W, 0), rows(W, 0), rows(W, 1), rows(W, 5)],
        out_specs=[whole, whole, rows(D, 0), rows(D, 0), rows(W, 0)],
        out_shape=[jax.ShapeDtypeStruct((D, D), F32),
                   jax.ShapeDtypeStruct((D, D), BF16),
                   jax.ShapeDtypeStruct((S, D), BF16),
                   jax.ShapeDtypeStruct((S, D), BF16),
                   jax.ShapeDtypeStruct((S, W), BF16)],
        compiler_params=_params("arbitrary"),
    )(g, wout, mixed, ypool, o, proj, proj)


def _section_specs(sections, rows, width):
    specs = [pl.BlockSpec((rows, width), lambda k, c=c: (k, c)) for _, c in sections]
    return specs, [a for a, _ in sections]


def _inproj_bwd_dw(h, sections, dzf):
    S, D = h.shape
    W = D // 2
    n_sec = len(sections)
    N = n_sec * W
    ts = min(2 * TM, S)
    n_steps = S // ts

    def body(h_ref, dz_ref, *rest):
        sec_refs, (dw_ref, dwb_ref) = rest[:n_sec], rest[n_sec:]

        @pl.when(pl.program_id(0) == 0)
        def _():
            dw_ref[...] = jnp.zeros(dw_ref.shape, F32)

        ht = h_ref[...].T
        dw_ref[:, N:N + LANES] += jnp.dot(ht, dz_ref[...], preferred_element_type=F32)
        for n, ref in enumerate(sec_refs):
            dw_ref[:, n * W:(n + 1) * W] += jnp.dot(ht, ref[...], preferred_element_type=F32)

        @pl.when(pl.program_id(0) == n_steps - 1)
        def _():
            dwb_ref[...] = dw_ref[...].astype(BF16)

    sec_specs, sec_arrays = _section_specs(sections, ts, W)
    whole = pl.BlockSpec((D, N + LANES), lambda k: (0, 0), pipeline_mode=pl.Buffered(1))
    return pl.pallas_call(
        body, name="inproj_bwd_dw", grid=(n_steps,),
        in_specs=[pl.BlockSpec((ts, D), lambda k: (k, 0)),
                  pl.BlockSpec((ts, LANES), lambda k: (k, 0))] + sec_specs,
        out_specs=[whole, whole],
        out_shape=[jax.ShapeDtypeStruct((D, N + LANES), F32),
                   jax.ShapeDtypeStruct((D, N + LANES), BF16)],
        compiler_params=_params("arbitrary"),
    )(h, dzf, *sec_arrays)


def _attn_bwd(proj, da, qaug, kaug, doaug):
    S = proj.shape[0]
    W = proj.shape[1] // 6
    P = W // LANES
    tq = min(TQ, S)
    nq = S // tq
    qc, kc, vc = 2 * P, 3 * P, 4 * P
    scale = 1.0 / math.sqrt(HEAD_DIM)

    def body(q_ref, k_ref, v_ref, do_ref, qa_ref, ka_ref, da_ref,
             dq_ref, dk_ref, dv_ref, drows_ref, dcols_ref, km_scr, vm_scr, dk_scr, dv_scr, dq_scr):
        pair, j = pl.program_id(0), pl.program_id(1)
        lane, halves = _head_halves(tq)

        @pl.when(jnp.logical_and(pair == 0, j == 0))
        def _():
            drows_ref[...] = jnp.zeros(drows_ref.shape, F32)
            dcols_ref[...] = jnp.zeros(dcols_ref.shape, F32)

        @pl.when(j == 0)
        def _():
            dq_scr[...] = jnp.zeros(dq_scr.shape, F32)

        v_ones = ((lane & (HEAD_DIM - 1)) < 3).astype(BF16)
        for a in range(2):
            km_scr[a] = jnp.where(halves[a], k_ref[...], ka_ref[...])
            vm_scr[a] = jnp.where(halves[a], v_ref[...], v_ones)
        dk_scr[...] = jnp.zeros(dk_scr.shape, F32)
        dv_scr[...] = jnp.zeros(dv_scr.shape, F32)

        def update(i, on_diagonal):
            rows = pl.ds(pl.multiple_of(i * tq, tq), tq)
            qs = q_ref[rows, :] * scale
            do2, qaug_t, doaug_t = do_ref[rows, :], qa_ref[rows, :], da_ref[rows, :]
            if on_diagonal:
                keep = (lax.broadcasted_iota(jnp.int32, (tq, tq), 0)
                        >= lax.broadcasted_iota(jnp.int32, (tq, tq), 1))
            qas = [jnp.where(halves[a], qs, qaug_t) for a in range(2)]
            logits = [lax.dot_general(qas[a], km_scr[a], NT, preferred_element_type=F32) for a in range(2)]
            dps = [lax.dot_general(jnp.where(halves[a], do2, doaug_t), vm_scr[a], NT, preferred_element_type=F32)
                   for a in range(2)]
            dv = None
            for a in range(2):
                s = jnp.where(keep, logits[a], NEG_INF) if on_diagonal else logits[a]
                p = jnp.exp(s)
                dsb = (p * dps[a]).astype(BF16)
                do0 = jnp.where(halves[a], do2, jnp.zeros_like(do2))
                dv_a = lax.dot_general(p.astype(BF16), do0, TN_DIMS, preferred_element_type=F32)
                dv = dv_a if dv is None else dv + dv_a
                dk_scr[a] += lax.dot_general(dsb, qas[a], TN_DIMS, preferred_element_type=F32)
                dq_scr[a, rows, :] += jnp.dot(dsb, km_scr[a], preferred_element_type=F32)
            dv_scr[...] += dv

        def below_diagonal(n, carry):
            update(j + 1 + 2 * n, False)
            update(j + 2 + 2 * n, False)
            return carry

        update(j, True)
        below = nq - 1 - j
        lax.fori_loop(0, below // 2, below_diagonal, 0)

        @pl.when(below % 2 == 1)
        def _():
            update(nq - 1, False)


        def to_head_lanes(old, first, second):
            at = lax.broadcasted_iota(jnp.int32, old.shape, 1) - 2 * pair
            return jnp.where(at == 0, first, jnp.where(at == 1, second, old))

        dk_ref[...] = jnp.where(halves[0], dk_scr[0], dk_scr[1]).astype(BF16)
        dv_ref[...] = dv_scr[...].astype(BF16)
        keys = pl.ds(pl.multiple_of(j * tq, tq), tq)
        ones_a, ones_b = HEAD_DIM + SLOT_ONE, SLOT_ONE
        dcols_ref[keys, :] = to_head_lanes(dcols_ref[keys, :], dk_scr[0][:, ones_a:ones_a + 1],
                                           dk_scr[1][:, ones_b:ones_b + 1])

        @pl.when(j == nq - 1)
        def _():
            row_lane, row_halves = _head_halves(S)
            dq_ref[...] = (jnp.where(row_halves[0], dq_scr[0], dq_scr[1]) * scale).astype(BF16)
            c_a, c_b = HEAD_DIM + SLOT_C, SLOT_C
            drows_ref[...] = to_head_lanes(drows_ref[...], dq_scr[0][:, c_a:c_a + 1], dq_scr[1][:, c_b:c_b + 1])

    tile = lambda col: pl.BlockSpec((tq, LANES), lambda p, j: (j, col + p))
    whole = lambda col: pl.BlockSpec((S, LANES), lambda p, j: (0, col + p))
    shared = pl.BlockSpec((S, LANES), lambda p, j: (0, 0))
    return pl.pallas_call(
        body, name="attn_bwd", grid=(P, nq),
        in_specs=[whole(qc), tile(kc), tile(vc), whole(P), whole(0), tile(0), whole(0)],
        out_specs=[whole(0), tile(0), tile(0), shared, shared],
        out_shape=[jax.ShapeDtypeStruct((S, W), BF16),
                   jax.ShapeDtypeStruct((S, W), BF16),
                   jax.ShapeDtypeStruct((S, W), BF16),
                   jax.ShapeDtypeStruct((S, LANES), F32),
                   jax.ShapeDtypeStruct((S, LANES), F32)],
        scratch_shapes=[pltpu.VMEM((2, tq, LANES), BF16),
                        pltpu.VMEM((2, tq, LANES), BF16),
                        pltpu.VMEM((2, tq, LANES), F32),
                        pltpu.VMEM((tq, LANES), F32),
                        pltpu.VMEM((2, S, LANES), F32)],
        compiler_params=_params("arbitrary", "arbitrary"),
    )(proj, proj, proj, da, qaug, kaug, doaug)


def _fgate_bwd(drows, dcols, z, bias):
    S = z.shape[0]
    tb = min(TB, S)
    nb = S // tb

    def body(drows_ref, dcols_ref, z_ref, b_ref, dz_ref, db_ref):
        tri = (lax.broadcasted_iota(jnp.int32, (tb, tb), 1)
               >= lax.broadcasted_iota(jnp.int32, (tb, tb), 0)).astype(F32)

        local = []
        for b in range(nb):
            rows = slice(b * tb, (b + 1) * tb)
            local.append(jnp.dot(tri, drows_ref[rows, :] - dcols_ref[rows, :], preferred_element_type=F32,
                                 precision=lax.Precision.HIGHEST))
        carry = jnp.zeros((1, LANES), F32)
        db = jnp.zeros((1, LANES), F32)
        for b in reversed(range(nb)):
            rows = slice(b * tb, (b + 1) * tb)
            rc = local[b] + carry
            carry = rc[0:1, :]
            dz = rc * _sigmoid(-(z_ref[rows, :] + b_ref[...]))
            dz_ref[rows, :] = dz.astype(BF16)
            db = db + jnp.sum(dz, axis=0, keepdims=True)
        db_ref[...] = db

    return pl.pallas_call(
        body, name="fgate_bwd",
        out_shape=[jax.ShapeDtypeStruct((S, LANES), BF16),
                   jax.ShapeDtypeStruct((1, LANES), F32)],
        compiler_params=pltpu.CompilerParams(vmem_limit_bytes=VMEM_LIMIT),
    )(drows, dcols, z, bias)


def _pool_bwd(proj, da, pool_w, pool_scale):
    S = proj.shape[0]
    G = len(POOL_WINDOWS)

    def body(u_ref, dy_ref, w_ref, s_ref, du_ref, dw_ref, ds_ref, pad_ref):
        g = pl.program_id(0)
        for gi, w in enumerate(POOL_WINDOWS):
            @pl.when(g == gi)
            def _():
                d, cnt = _window_mean_minus_self(u_ref[...].astype(F32), pad_ref, w, S)
                db = d.astype(BF16)
                wb = w_ref[0].astype(BF16)
                yraw = jnp.dot(db, wb, preferred_element_type=F32)
                dy = dy_ref[...].astype(F32)
                ds_ref[...] = jnp.sum(dy * yraw, axis=0, keepdims=True)
                dzb = (dy * s_ref[...]).astype(BF16)
                dw_ref[0] = lax.dot_general(db, dzb, TN_DIMS, preferred_element_type=F32)
                dd = lax.dot_general(dzb, wb, NT, preferred_element_type=F32)
                pad_ref[0:S, :] = dd / cnt
                pad_ref[S:S + MAX_WINDOW, :] = jnp.zeros((MAX_WINDOW, LANES), F32)
                acc = -dd
                for j in range(w):
                    acc = acc + pad_ref[j:j + S, :]
                du_ref[...] = acc.astype(BF16)

    return pl.pallas_call(
        body, name="pool_bwd", grid=(G,),
        in_specs=[pl.BlockSpec((S, LANES), lambda g: (0, g)),
                  pl.BlockSpec((S, LANES), lambda g: (0, g)),
                  pl.BlockSpec((1, LANES, LANES), lambda g: (g, 0, 0)),
                  pl.BlockSpec((1, LANES), lambda g: (0, g))],
        out_specs=[pl.BlockSpec((S, LANES), lambda g: (0, g)),
                   pl.BlockSpec((1, LANES, LANES), lambda g: (g, 0, 0)),
                   pl.BlockSpec((1, LANES), lambda g: (0, g))],
        out_shape=[jax.ShapeDtypeStruct((S, G * LANES), BF16),
                   jax.ShapeDtypeStruct((G, LANES, LANES), F32),
                   jax.ShapeDtypeStruct((1, G * LANES), F32)],
        scratch_shapes=[pltpu.VMEM((S + MAX_WINDOW, LANES), F32)],
        compiler_params=_params("arbitrary"),
    )(proj, da, pool_w, pool_scale)


def _inproj_bwd_dx(sections, dzf, w, x, gam, g, after=()):
    S, D = x.shape
    N = w.shape[1] - LANES
    W = D // 2
    n_sec = len(sections)
    tm = min(TM, S)

    def body(dz_ref, w_ref, wf_ref, x_ref, gam_ref, g_ref, *rest):
        sec_refs = rest[:n_sec]
        dx_ref, dg_ref = rest[-2:]

        @pl.when(pl.program_id(0) == 0)
        def _():
            dg_ref[...] = jnp.zeros(dg_ref.shape, F32)

        dh = lax.dot_general(dz_ref[...], wf_ref[...], NT, preferred_element_type=F32)
        for n, ref in enumerate(sec_refs):
            dh = dh + lax.dot_general(ref[...], w_ref[:, n * W:(n + 1) * W], NT, preferred_element_type=F32)
        xf = x_ref[...]
        r = lax.rsqrt(jnp.mean(xf * xf, axis=-1, keepdims=True) + RMS_EPS)
        xhat = xf * r
        dg_ref[...] += jnp.sum(dh * xhat, axis=0, keepdims=True)
        dxhat = dh * gam_ref[...]
        dx_ref[...] = g_ref[...] + r * (dxhat - xhat * jnp.mean(dxhat * xhat, axis=-1, keepdims=True))

    sec_specs, sec_arrays = _section_specs(sections, tm, W)
    return pl.pallas_call(
        body, name="inproj_bwd_dx", grid=(S // tm,),
        in_specs=[pl.BlockSpec((tm, LANES), lambda i: (i, 0)),
                  pl.BlockSpec((D, N), lambda i: (0, 0)),
                  pl.BlockSpec((D, LANES), lambda i: (0, N // LANES)),
                  pl.BlockSpec((tm, D), lambda i: (i, 0)),
                  pl.BlockSpec((1, D), lambda i: (0, 0)),
                  pl.BlockSpec((tm, D), lambda i: (i, 0))] + sec_specs + _after_specs(after),
        out_specs=[pl.BlockSpec((tm, D), lambda i: (i, 0)),
                   pl.BlockSpec((1, D), lambda i: (0, 0))],
        out_shape=[jax.ShapeDtypeStruct((S, D), F32),
                   jax.ShapeDtypeStruct((1, D), F32)],
        compiler_params=_params("arbitrary"),
    )(dzf, w, w, x, gam, g, *sec_arrays, *after)


def _adamw(w, m, v, gsets, name, rows, shifted=False, first=0, into=None):
    A, R, C = w.shape
    n_sets = len(gsets)
    tr = min(rows, R)
    c1 = 1.0 / (1.0 - ADAM_B1 ** ADAM_STEP)
    c2 = 1.0 / (1.0 - ADAM_B2 ** ADAM_STEP)
    counts = [len(gs) for gs in gsets]

    def body(w_ref, m_ref, v_ref, *rest):
        g_ref, d_ref, nm_ref, nv_ref = rest[-4:]
        at = 0
        for a in range(n_sets):
            part_refs = rest[at:at + counts[a]]
            at += counts[a]

            @pl.when(pl.program_id(0) == a)
            def _():
                g = None
                for ref in part_refs:
                    for s in range(ref.shape[0]):
                        term = ref[s].astype(F32)
                        g = term if g is None else g + term
                if shifted:
                    lanes = g.shape[1]
                    g = pltpu.roll(g, (lanes - _index(_position()) * (C % LANES)) % lanes, axis=1)[:, :C]
                nm = ADAM_B1 * m_ref[0] + (1.0 - ADAM_B1) * g
                nv = ADAM_B2 * v_ref[0] + (1.0 - ADAM_B2) * (g * g)
                g_ref[0] = g
                nm_ref[0] = nm
                nv_ref[0] = nv
                d_ref[0] = -ADAM_LR * ((nm * c1) / (jnp.sqrt(nv * c2) + ADAM_EPS) + ADAM_WD * w_ref[0])

    spec = pl.BlockSpec((1, tr, C), lambda a, r: (first + a, r, 0))
    part_specs = [pl.BlockSpec((part.shape[0], tr, part.shape[2]), lambda a, r, l=l: (0, jnp.where(a == l, r, 0), 0))
                  for l, gs in enumerate(gsets) for part in gs]
    parts = [part for gs in gsets for part in gs]
    shape = jax.ShapeDtypeStruct((A, R, C), F32)
    earlier = () if into is None else tuple(into)
    return pl.pallas_call(
        body, name=name, grid=(n_sets, R // tr),
        in_specs=[spec, spec, spec] + part_specs + _after_specs(earlier),
        out_specs=[spec, spec, spec, spec],
        out_shape=[shape, shape, shape, shape],
        input_output_aliases={3 + len(parts) + n: n for n in range(len(earlier))},
        compiler_params=_params("arbitrary", "arbitrary"),
    )(w, m, v, *parts, *earlier)


def _position():
    return lax.axis_index("x"), lax.axis_index("y"), lax.axis_index("c")


def _index(dev):
    return 4 * dev[0] + 2 * dev[1] + dev[2]


def _all_gather(arrs, slots, out_shapes, name):
    n_arr = len(arrs)

    def body(*refs):
        ins, outs = refs[:n_arr], refs[n_arr:2 * n_arr]
        send_sems, recv_sems, local_sems = refs[2 * n_arr:]
        x, y, c = _position()
        me, sibling = (x, y, c), (x, y, 1 - c)
        chips = [(1 - x, y), (x, 1 - y), (1 - x, 1 - y)]

        def copy(a, k, block, to, src=None):
            part = slots[a](outs[a], _index(block))
            return pltpu.make_async_remote_copy(
                src_ref=part if src is None else src, dst_ref=part,
                send_sem=send_sems.at[a, k], recv_sem=recv_sems.at[a, k],
                device_id=to, device_id_type=MESH)

        mine = [pltpu.make_async_copy(ins[a], slots[a](outs[a], _index(me)), local_sems.at[a])
                for a in range(n_arr)]
        for cp in mine:
            cp.start()
        first = []
        for a in range(n_arr):
            first.append(copy(a, 0, me, sibling, src=ins[a]))
            first += [copy(a, 1 + j, me, (*chip, c), src=ins[a]) for j, chip in enumerate(chips)]
        for cp in first:
            cp.start()
        passed = []
        for j, chip in enumerate(chips):
            for a in range(n_arr):
                copy(a, 1 + j, (*chip, c), me).wait_recv()
                fwd = copy(a, 4 + j, (*chip, c), sibling)
                fwd.start()
                passed.append(fwd)
        for a in range(n_arr):
            copy(a, 0, sibling, me).wait_recv()
            for j, chip in enumerate(chips):
                copy(a, 4 + j, (*chip, 1 - c), me).wait_recv()
        for cp in first + passed:
            cp.wait_send()
        for cp in mine:
            cp.wait()

    any_spec = pl.BlockSpec(memory_space=pl.ANY)
    return pl.pallas_call(
        body, name=name,
        in_specs=[any_spec] * n_arr, out_specs=[any_spec] * n_arr, out_shape=out_shapes,
        scratch_shapes=[pltpu.SemaphoreType.DMA((n_arr, 7)), pltpu.SemaphoreType.DMA((n_arr, 7)),
                        pltpu.SemaphoreType.DMA((n_arr,))],
    )(*arrs)


def _split_copies(srcs, lands, send_sems, recv_sems, kinds):
    x, y, c = _position()
    me = _index((x, y, c))
    copies = []
    for a, (src_part, land_part) in enumerate(kinds):
        for k in range(1, N_DEV):
            peer = (x ^ ((k >> 2) & 1), y ^ ((k >> 1) & 1), c ^ (k & 1))
            copies.append(pltpu.make_async_remote_copy(
                src_ref=src_part(srcs[a], _index(peer)), dst_ref=land_part(lands[a], me, k),
                send_sem=send_sems[a].at[k - 1], recv_sem=recv_sems[a].at[k - 1],
                device_id=peer, device_id_type=MESH))
    return copies


def _split_start(srcs, lands, kinds, name, after=()):
    n = len(srcs)

    def body(*refs):
        src_refs, land_refs = refs[:n], refs[n:2 * n]
        outs = refs[2 * n + len(after):]
        send_sems, recv_sems = outs[:n], outs[n:2 * n]
        token = outs[-1]
        for cp in _split_copies(src_refs, land_refs, send_sems, recv_sems, kinds):
            cp.start()
        token[...] = jnp.zeros(token.shape, token.dtype)

    hbm = pl.BlockSpec(memory_space=pltpu.HBM)
    sem = pl.BlockSpec(memory_space=pltpu.SEMAPHORE)
    operands = [pltpu.with_memory_space_constraint(t, pltpu.HBM) for t in (*srcs, *lands)]
    out = pl.pallas_call(
        body, name=name,
        in_specs=[hbm] * (2 * n) + _after_specs(after),
        out_specs=[sem] * (2 * n) + [hbm] * (2 * n) + [pl.BlockSpec(memory_space=pltpu.VMEM)],
        out_shape=[pltpu.SemaphoreType.DMA((N_DEV - 1,))] * (2 * n)
        + [pltpu.HBM(t.shape, t.dtype) for t in operands] + [jax.ShapeDtypeStruct((8, LANES), F32)],
        input_output_aliases={i: 2 * n + i for i in range(2 * n)},
        compiler_params=pltpu.CompilerParams(has_side_effects=pltpu.SideEffectType.DATAFLOW_SIDE_EFFECTING),
    )(*operands, *after)
    return [(out[a], out[n + a], out[2 * n + a], out[3 * n + a]) for a in range(n)], out[-1]


def _split_wait(started, kinds, after, name):
    n = len(started)
    sems = [t[0] for t in started] + [t[1] for t in started]
    srcs = [t[2] for t in started]
    lands = [t[3] for t in started]

    def body(*refs):
        src_refs, land_refs = refs[:n], refs[n:2 * n]
        send_sems, recv_sems = refs[2 * n:3 * n], refs[3 * n:4 * n]
        for cp in _split_copies(src_refs, land_refs, send_sems, recv_sems, kinds):
            cp.wait_send()
            cp.wait_recv()

    hbm = pl.BlockSpec(memory_space=pltpu.HBM)
    sem = pl.BlockSpec(memory_space=pltpu.SEMAPHORE)
    out = pl.pallas_call(
        body, name=name,
        in_specs=[hbm] * (2 * n) + [sem] * (2 * n) + _after_specs(after),
        out_specs=[hbm] * (2 * n),
        out_shape=[pltpu.HBM(t.shape, t.dtype) for t in (*srcs, *lands)],
        input_output_aliases={i: i for i in range(2 * n)},
        compiler_params=pltpu.CompilerParams(has_side_effects=pltpu.SideEffectType.DATAFLOW_SIDE_EFFECTING),
    )(*srcs, *lands, *sems, *after)
    return out[n:]


def _as_rows(p):
    if p.size % LANES == 0:
        rows = p.reshape(-1, LANES)
    else:
        rows = p.reshape(-1, p.shape[-1])
        rows = jnp.pad(rows, ((0, 0), (0, LANES - rows.shape[1])))
    return jnp.pad(rows, ((0, -rows.shape[0] % 8), (0, 0)))


def _pack(parts):
    return jnp.concatenate([_as_rows(p) for p in parts])[None]


def _unpack(packed, like):
    out, at = [], 0
    for p in like:
        whole = p.size % LANES == 0
        n = p.size // LANES if whole else p.size // p.shape[-1]
        rows = packed[0, at:at + n]
        out.append((rows if whole else rows[:, :p.shape[-1]]).reshape(p.shape))
        at += n + (-n % 8)
    return out


def _local_step(x, target, norm_g, forget_bias, pool_w, pool_scale, final_g, weights_in, weights_out, on_grads,
                first_after=()):
    L = norm_g.shape[0]
    S, D = x.shape
    W = D // 2
    H = W // HEAD_DIM
    bias = jnp.pad(forget_bias, ((0, 0), (0, LANES - H)))

    saved = []
    after = tuple(first_after)
    for l in range(L):
        proj, h, z, w = _inproj_fwd(x, norm_g[l:l + 1], weights_in(l, x), after)
        after = ()
        qaug, kaug = _fgate_fwd(z, bias[l:l + 1], H)
        ypool = _pool_fwd(proj, pool_w[l], pool_scale[l:l + 1])
        o, qaug_b = _attn_fwd(proj, qaug, kaug)
        wout = weights_out(l, o)
        x_new, mixed = _outproj_fwd(ypool, o, proj, x, wout)
        saved.append((x, proj, h, z, qaug_b, kaug, ypool, o, mixed, w, wout))
        x = x_new

    g, loss, d_final_g = _loss_head(x, final_g.reshape(1, D), target)

    small = None
    for l in reversed(range(L)):
        x_in, proj, h, z, qaug_b, kaug, ypool, o, mixed, w, wout = saved[l]
        d_wout, d_wout_bf16, da, dgate, doaug = _outproj_bwd(g, wout, mixed, ypool, o, proj)
        dq, dk, dv, drows, dcols = _attn_bwd(proj, da, qaug_b, kaug, doaug)
        dzf, db = _fgate_bwd(drows, dcols, z, bias[l:l + 1])
        dpu, dpw, dps = _pool_bwd(proj, da, pool_w[l], pool_scale[l:l + 1])
        dproj = [(dpu, 0), (dgate, 0), (dq, 0), (dk, 0), (dv, 0), (dgate, 1)]
        d_w, d_w_bf16 = _inproj_bwd_dw(h, dproj, dzf)
        after = tuple(on_grads(l, d_w, d_w_bf16, d_wout, d_wout_bf16, small))
        g, dgam = _inproj_bwd_dx(dproj, dzf, w, x_in, norm_g[l:l + 1], g, after)
        small = (dgam[0], db[0, :H], dpw, dps[0])
    return loss[0, 0], g, small, d_final_g[0]


def kernel(x, norm_g, w_in, forget_bias, pool_w, pool_scale, w_out, final_g, loss_target, m_norm_g, m_w_in, m_forget_bias, m_pool_w, m_pool_scale, m_w_out, m_final_g, v_norm_g, v_w_in, v_forget_bias, v_pool_w, v_pool_scale, v_w_out, v_final_g):
    L, D, cols = w_in.shape
    rows_out = w_out.shape[1]
    me = _index(_position())
    slot = _slot_width(cols)
    wout_b = w_out.astype(BF16)
    win_b = _shift_slots(w_in)
    gather_in = (lambda ref, peer: ref, lambda ref, mine, k: ref.at[mine])
    gather_out = (lambda ref, peer: ref, lambda ref, mine, k: ref.at[pl.ds(mine * rows_out, rows_out), :])

    def landing(block, n_slots):
        zone = lax.empty((n_slots * block.shape[0], *block.shape[1:]), block.dtype)
        return lax.dynamic_update_slice(zone, block, (me * block.shape[0],) + (0,) * (block.ndim - 1))

    (first_in,) = _all_gather([win_b[0]], [lambda ref, n: ref.at[n]],
                              [jax.ShapeDtypeStruct((N_DEV, D, slot), BF16)], "gather_first")
    rest_srcs = [wout_b[0]] + [w[l] for l in range(1, L) for w in (win_b, wout_b)]
    rest_lands = [landing(wout_b[0], N_DEV)]
    for l in range(1, L):
        rest_lands += [landing(win_b[l][None], N_DEV), landing(wout_b[l], N_DEV)]
    rest_kinds = [gather_out] + [gather_in, gather_out] * (L - 1)
    rest, rest_token = _split_start(rest_srcs, rest_lands, rest_kinds, "gather_start_rest", (first_in,))

    def weights_in(l, x_in):
        if l == 0:
            return first_in
        (win_all,) = _split_wait([rest[2 * l - 1]], [gather_in], (x_in,), f"gather_wait_in_{l}")
        return win_all

    def weights_out(l, o):
        (wout_full,) = _split_wait([rest[2 * l]], [gather_out], (o,), f"gather_wait_out_{l}")
        return wout_full

    stride = slot - LANES
    exchange_kinds = [(lambda ref, peer: ref.at[:, pl.ds(pl.multiple_of(peer * stride, LANES), slot)],
                       lambda ref, mine, k: ref.at[k - 1]),
                      (lambda ref, peer: ref.at[pl.ds(peer * rows_out, rows_out), :],
                       lambda ref, mine, k: ref.at[k - 1])]
    zero_g = jnp.zeros_like(final_g)
    zero_loss = jnp.zeros((LANES,), F32)

    def small_pack(l, norm_g_l, bias_l, pool_w_l, pool_scale_l, final, loss_row=None):
        return _pack([norm_g_l, bias_l, pool_w_l, pool_scale_l, final if l == 0 else zero_g,
                      zero_loss if loss_row is None else loss_row])[0]

    exchanges, own_parts = {}, {}

    def on_grads(l, dw, dw_bf16, d_wout, d_wout_bf16, small):
        own_parts[l] = (lax.dynamic_slice_in_dim(dw, me * stride, slot, 1)[None],
                        lax.dynamic_slice_in_dim(d_wout, me * rows_out, rows_out, 0)[None])
        srcs = [dw_bf16, d_wout_bf16]
        lands = [lax.empty((N_DEV - 1, D, slot), BF16), lax.empty((N_DEV - 1, rows_out, D), BF16)]
        kinds = list(exchange_kinds)
        if small is not None:
            packed_small = small_pack(l + 1, *small, None)
            srcs.append(packed_small)
            lands.append(landing(packed_small[None], N_DEV))
            kinds.append(gather_in)
        started, token = _split_start(srcs, lands, kinds, f"exchange_start_{l}")
        exchanges[l] = (started, kinds)
        return (token,)

    loss, dx, small_first, d_final_g = _local_step(
        x[0], loss_target[0], norm_g, forget_bias, pool_w, pool_scale, final_g,
        weights_in, weights_out, on_grads, (rest_token,))
    packed_first = small_pack(0, *small_first, d_final_g, jnp.full((LANES,), loss, F32))
    first_started, first_token = _split_start(
        [packed_first], [landing(packed_first[None], N_DEV)], [gather_in],
        "small_start", (w_in, m_w_in, v_w_in, *own_parts[0]))

    gin_sets, gout_sets, small_sets = [None] * L, [None] * L, [None] * L

    def wait_for(l, after):
        started, kinds = exchanges[l]
        got = _split_wait(started, kinds, after, f"exchange_wait_{l}")
        gin_sets[l] = [own_parts[l][0], got[0]]
        gout_sets[l] = [own_parts[l][1], got[1]]
        if len(got) > 2:
            small_sets[l + 1] = [got[2]]

    for l in range(1, L):
        wait_for(l, (dx, first_token))
    rest_in = _adamw(w_in, m_w_in, v_w_in, gin_sets[1:], "adamw_w_in_rest", TM // 2, shifted=True, first=1)
    rest_out = _adamw(w_out, m_w_out, v_w_out, gout_sets[1:], "adamw_w_out_rest", rows_out, first=1)
    wait_for(0, (rest_in[1], rest_out[1]))
    g_w_in, d_w_in, nm_w_in, nv_w_in = _adamw(w_in, m_w_in, v_w_in, gin_sets[:1], "adamw_w_in_first", TM // 2,
                                              shifted=True, into=rest_in)
    g_w_out, d_w_out, nm_w_out, nv_w_out = _adamw(w_out, m_w_out, v_w_out, gout_sets[:1], "adamw_w_out_first",
                                                  rows_out, into=rest_out)
    small_sets[0] = _split_wait(first_started, [gather_in], (d_w_in, d_w_out), "small_wait")
    loss = jnp.sum(small_sets[0][0][:, packed_first.shape[0] - 8, 0])

    def small_stack(norm_g_, bias_, pool_w_, pool_scale_, final):
        return jnp.stack([small_pack(l, norm_g_[l], bias_[l], pool_w_[l], pool_scale_[l], final) for l in range(L)])

    packed = _adamw(small_stack(norm_g, forget_bias, pool_w, pool_scale, final_g),
                    small_stack(m_norm_g, m_forget_bias, m_pool_w, m_pool_scale, m_final_g),
                    small_stack(v_norm_g, v_forget_bias, v_pool_w, v_pool_scale, v_final_g),
                    small_sets, "adamw_small", packed_first.shape[0])

    def small_unpack(p):
        like = [norm_g[0], forget_bias[0], pool_w[0], pool_scale[0], final_g]
        layers = [_unpack(p[l:l + 1], like) for l in range(L)]
        return [jnp.stack([layers[l][n] for l in range(L)]) for n in range(4)] + [layers[0][4]]

    g_s, d_s, nm_s, nv_s = [small_unpack(p) for p in packed]

    def order(big_in, big_out, small):
        return (small[0], big_in, small[1], small[2], small[3], big_out, small[4])

    return (loss, dx[None], *order(g_w_in, g_w_out, g_s), *order(d_w_in, d_w_out, d_s),
            *order(nm_w_in, nm_w_out, nm_s), *order(nv_w_in, nv_w_out, nv_s))
```

```python
import math

import jax
import jax.numpy as jnp
from jax import lax
from jax.experimental import pallas as pl
from jax.experimental.pallas import tpu as pltpu

F32 = jnp.float32
BF16 = jnp.bfloat16
MESH = pl.DeviceIdType.MESH

RMS_EPS = 1e-6
NEG_INF = -1e30
HEAD_DIM = 64
POOL_WINDOWS = (2, 4, 8, 16)
MAX_WINDOW = 16
LANES = 128
N_DEV = 8

ADAM_LR = 0.001
ADAM_B1 = 0.9
ADAM_B2 = 0.999
ADAM_EPS = 1e-08
ADAM_WD = 0.01
ADAM_STEP = 10

TM = 512
TN = 512
TQ = 512
TB = 256
VMEM_LIMIT = 56 * 1024 * 1024

NT = (((1,), (1,)), ((), ()))
TN_DIMS = (((0,), (0,)), ((), ()))

SLOT_C, SLOT_ONE, SLOT_LSE = 0, 3, 6


def _params(*sem):
    return pltpu.CompilerParams(dimension_semantics=sem, vmem_limit_bytes=VMEM_LIMIT)


def _sigmoid(x):
    return 1.0 / (1.0 + jnp.exp(-x))


def _split3(x):
    hi = x.astype(BF16).astype(F32)
    rest = x - hi
    mid = rest.astype(BF16).astype(F32)
    return hi, mid, rest - mid


def _after_specs(after):
    return [pl.BlockSpec(memory_space=pl.ANY)] * len(after)


def _slot_width(cols):
    return LANES * (-(-(cols + (N_DEV - 1) * (cols % LANES)) // LANES))


def _shift_slots(w_in):
    L, D, cols = w_in.shape
    slot = _slot_width(cols)
    tr = min(TM // 2, D)

    def body(w_ref, o_ref, pad_scr):
        pad_scr[...] = jnp.zeros(pad_scr.shape, F32)
        pad_scr[:, 0:cols] = w_ref[0]
        o_ref[0] = pltpu.roll(pad_scr[...], _index(_position()) * (cols % LANES), axis=1).astype(BF16)

    return pl.pallas_call(
        body, name="shift_slots", grid=(L, D // tr),
        in_specs=[pl.BlockSpec((1, tr, cols), lambda l, r: (l, r, 0))],
        out_specs=pl.BlockSpec((1, tr, slot), lambda l, r: (l, r, 0)),
        out_shape=jax.ShapeDtypeStruct((L, D, slot), BF16),
        scratch_shapes=[pltpu.VMEM((tr, slot), F32)],
        compiler_params=_params("parallel", "parallel"),
    )(w_in)


def _inproj_fwd(x, gam, slots, after=()):
    S, D = x.shape
    n_dev, _, sw = slots.shape
    stride = sw - LANES
    width = stride * n_dev + LANES
    N = width - LANES
    tm, tn = min(TM, S), TN

    def body(x_ref, g_ref, s_ref, *rest):
        proj_ref, h_ref, z_ref, w_ref = rest[-4:]

        @pl.when(pl.program_id(0) == 0)
        def _():
            for n in range(n_dev):
                base = stride * n
                first = s_ref[n, :, 0:LANES]
                if n > 0:
                    first = first + s_ref[n - 1, :, stride:sw]
                w_ref[:, base:base + LANES] = first
                w_ref[:, base + LANES:base + stride] = s_ref[n, :, LANES:stride]
            w_ref[:, stride * n_dev:width] = s_ref[n_dev - 1, :, stride:sw]

        xf = x_ref[...]
        r = lax.rsqrt(jnp.mean(xf * xf, axis=-1, keepdims=True) + RMS_EPS)
        h = ((xf * r) * g_ref[...]).astype(BF16)
        h_ref[...] = h
        z_ref[...] = jnp.dot(h, w_ref[:, N:width], preferred_element_type=F32)
        for n in range(N // tn):
            cols = slice(n * tn, (n + 1) * tn)
            proj_ref[:, cols] = jnp.dot(h, w_ref[:, cols], preferred_element_type=F32).astype(BF16)

    return pl.pallas_call(
        body, name="inproj_fwd", grid=(S // tm,),
        in_specs=[pl.BlockSpec((tm, D), lambda i: (i, 0)),
                  pl.BlockSpec((1, D), lambda i: (0, 0)),
                  pl.BlockSpec((n_dev, D, sw), lambda i: (0, 0, 0))] + _after_specs(after),
        out_specs=[pl.BlockSpec((tm, N), lambda i: (i, 0)),
                   pl.BlockSpec((tm, D), lambda i: (i, 0)),
                   pl.BlockSpec((tm, LANES), lambda i: (i, 0)),
                   pl.BlockSpec((D, width), lambda i: (0, 0))],
        out_shape=[jax.ShapeDtypeStruct((S, N), BF16),
                   jax.ShapeDtypeStruct((S, D), BF16),
                   jax.ShapeDtypeStruct((S, LANES), F32),
                   jax.ShapeDtypeStruct((D, width), BF16)],
        compiler_params=_params("arbitrary"),
    )(x, gam, slots, *after)


def _fgate_fwd(z, bias, n_heads):
    S = z.shape[0]
    tb = min(TB, S)
    P = n_heads // 2
    assert n_heads <= 8, "the three parts of c are packed eight lanes apart"

    def body(z_ref, b_ref, qaug_ref, kaug_ref):
        lane = lax.broadcasted_iota(jnp.int32, (tb, LANES), 1)
        tri = (lax.broadcasted_iota(jnp.int32, (tb, tb), 0)
               >= lax.broadcasted_iota(jnp.int32, (tb, tb), 1)).astype(F32)
        row = lax.broadcasted_iota(jnp.int32, (LANES, P * LANES), 0)
        col = lax.broadcasted_iota(jnp.int32, (LANES, P * LANES), 1)
        head, part_n = row & 7, row >> 3
        home = (head >> 1) * LANES + jnp.where((head & 1) == 0, HEAD_DIM, 0)
        is_part = jnp.logical_and(head < n_heads, part_n < 3)
        place_q = jnp.logical_and(is_part, col == home + SLOT_C + part_n).astype(BF16)
        place_k = jnp.logical_and(is_part, col == home + SLOT_ONE + part_n).astype(BF16)
        slot = lax.broadcasted_iota(jnp.int32, (tb, P * LANES), 1) & (HEAD_DIM - 1)
        q_ones = jnp.logical_and(slot >= SLOT_ONE, slot < SLOT_ONE + 3).astype(F32)
        k_ones = jnp.logical_or(slot < SLOT_C + 3,
                                jnp.logical_and(slot >= SLOT_LSE, slot < SLOT_LSE + 3)).astype(F32)

        local = []
        for b in range(S // tb):
            zz = z_ref[b * tb:(b + 1) * tb, :] + b_ref[...]
            lf = jnp.minimum(zz, 0.0) - jnp.log(1.0 + jnp.exp(-jnp.abs(zz)))
            lf = jnp.where(lane < n_heads, lf, 0.0)
            local.append(jnp.dot(tri, lf, preferred_element_type=F32, precision=lax.Precision.HIGHEST))
        carry = jnp.zeros((1, LANES), F32)
        for b, part_sum in enumerate(local):
            c = part_sum + carry
            carry = c[tb - 1:tb, :]
            hi, mid, lo = _split3(c)
            packed = (hi + pltpu.roll(mid, 8, axis=1) + pltpu.roll(lo, 16, axis=1)).astype(BF16)
            qaug_ref[b * tb:(b + 1) * tb, :] = (
                q_ones + jnp.dot(packed, place_q, preferred_element_type=F32)).astype(BF16)
            kaug_ref[b * tb:(b + 1) * tb, :] = (
                k_ones - jnp.dot(packed, place_k, preferred_element_type=F32)).astype(BF16)

    return pl.pallas_call(
        body, name="fgate_fwd",
        out_shape=[jax.ShapeDtypeStruct((S, P * LANES), BF16),
                   jax.ShapeDtypeStruct((S, P * LANES), BF16)],
        compiler_params=pltpu.CompilerParams(vmem_limit_bytes=VMEM_LIMIT),
    )(z, bias)


def _window_mean_minus_self(u, pad_ref, w, S):
    pad_ref[0:MAX_WINDOW, :] = jnp.zeros((MAX_WINDOW, LANES), F32)
    pad_ref[MAX_WINDOW:MAX_WINDOW + S, :] = u
    acc = u
    for j in range(1, w):
        acc = acc + pad_ref[MAX_WINDOW - j:MAX_WINDOW - j + S, :]
    t = lax.broadcasted_iota(jnp.int32, (S, LANES), 0)
    cnt = jnp.minimum(t + 1, w).astype(F32)
    return acc / cnt - u, cnt


def _pool_fwd(proj, pool_w, pool_scale):
    S = proj.shape[0]
    G = len(POOL_WINDOWS)

    def body(u_ref, w_ref, s_ref, y_ref, pad_ref):
        g = pl.program_id(0)
        for gi, w in enumerate(POOL_WINDOWS):
            @pl.when(g == gi)
            def _():
                d, _ = _window_mean_minus_self(u_ref[...].astype(F32), pad_ref, w, S)
                y = jnp.dot(d.astype(BF16), w_ref[0].astype(BF16), preferred_element_type=F32)
                y_ref[...] = (y * s_ref[...]).astype(BF16)

    return pl.pallas_call(
        body, name="pool_fwd", grid=(G,),
        in_specs=[pl.BlockSpec((S, LANES), lambda g: (0, g)),
                  pl.BlockSpec((1, LANES, LANES), lambda g: (g, 0, 0)),
                  pl.BlockSpec((1, LANES), lambda g: (0, g))],
        out_specs=pl.BlockSpec((S, LANES), lambda g: (0, g)),
        out_shape=jax.ShapeDtypeStruct((S, G * LANES), BF16),
        scratch_shapes=[pltpu.VMEM((S + MAX_WINDOW, LANES), F32)],
        compiler_params=_params("arbitrary"),
    )(proj, pool_w, pool_scale)


def _head_halves(rows):
    lane = lax.broadcasted_iota(jnp.int32, (rows, LANES), 1)
    return lane, (lane < HEAD_DIM, lane >= HEAD_DIM)


def _attn_fwd(proj, qaug, kaug):
    S = proj.shape[0]
    W = proj.shape[1] // 6
    P = W // LANES
    tq = min(TQ, S)
    nq = S // tq
    qc, kc, vc = 2 * P, 3 * P, 4 * P
    scale = 1.0 / math.sqrt(HEAD_DIM)

    def body(q_ref, k_ref, v_ref, qa_ref, ka_ref, o_ref, qb_ref, qm_scr, m_scr, acc_scr):
        i = pl.program_id(1)
        lane, halves = _head_halves(tq)
        v_ones = ((lane & (HEAD_DIM - 1)) < 3).astype(BF16)
        qs = q_ref[...] * scale
        qm_scr[0] = jnp.where(halves[0], qs, qa_ref[...])
        qm_scr[1] = jnp.where(halves[1], qs, qa_ref[...])
        m_scr[...] = jnp.full(m_scr.shape, NEG_INF, F32)
        acc_scr[...] = jnp.zeros(acc_scr.shape, F32)

        def update(j, on_diagonal):
            keys = pl.ds(pl.multiple_of(j * tq, tq), tq)
            k2, v2, kaug_t = k_ref[keys, :], v_ref[keys, :], ka_ref[keys, :]
            if on_diagonal:
                keep = (lax.broadcasted_iota(jnp.int32, (tq, tq), 0)
                        >= lax.broadcasted_iota(jnp.int32, (tq, tq), 1))
            logits = [lax.dot_general(qm_scr[a], jnp.where(halves[a], k2, kaug_t), NT, preferred_element_type=F32)
                      for a in range(2)]
            for a in range(2):
                s = jnp.where(keep, logits[a], NEG_INF) if on_diagonal else logits[a]
                va = jnp.where(halves[a], v2, v_ones)
                m_prev = m_scr[a]
                m_new = jnp.maximum(m_prev, jnp.max(s, axis=1, keepdims=True))
                p = jnp.exp(s - jnp.tile(m_new, (1, tq // LANES)))
                acc_scr[a] = jnp.exp(m_prev - m_new) * acc_scr[a] + jnp.dot(p.astype(BF16), va,
                                                                              preferred_element_type=F32)
                m_scr[a] = m_new

        def below_diagonal(jj, carry):
            update(2 * jj, False)
            update(2 * jj + 1, False)
            return carry

        lax.fori_loop(0, i // 2, below_diagonal, 0)

        @pl.when(i % 2 == 1)
        def _():
            update(i - 1, False)

        update(i, True)
        acc_a, acc_b = acc_scr[0], acc_scr[1]
        l_a, l_b = acc_a[:, HEAD_DIM:HEAD_DIM + 1], acc_b[:, 0:1]
        o_ref[...] = jnp.where(halves[0], acc_a / l_a, acc_b / l_b).astype(BF16)
        lse = jnp.where(halves[0], m_scr[1] + jnp.log(l_b), m_scr[0] + jnp.log(l_a))
        slot = lane & (HEAD_DIM - 1)
        aug = qa_ref[...].astype(F32)
        for n, part in enumerate(_split3(lse)):
            aug = jnp.where(slot == SLOT_LSE + n, -part, aug)
        qb_ref[...] = aug.astype(BF16)

    tile = lambda col: pl.BlockSpec((tq, LANES), lambda p, i: (i, col + p))
    whole = lambda col: pl.BlockSpec((S, LANES), lambda p, i: (0, col + p))
    return pl.pallas_call(
        body, name="attn_fwd", grid=(P, nq),
        in_specs=[tile(qc), whole(kc), whole(vc), tile(0), whole(0)],
        out_specs=[tile(0), tile(0)],
        out_shape=[jax.ShapeDtypeStruct((S, W), BF16), jax.ShapeDtypeStruct((S, W), BF16)],
        scratch_shapes=[pltpu.VMEM((2, tq, LANES), BF16),
                        pltpu.VMEM((2, tq, LANES), F32),
                        pltpu.VMEM((2, tq, LANES), F32)],
        compiler_params=_params("parallel", "arbitrary"),
    )(proj, proj, proj, qaug, kaug)


def _outproj_fwd(ypool, o, proj, x, wout, head=None):
    S, D = x.shape
    W = D // 2
    tm, tn = min(TM, S), TN

    def body(y_ref, o_ref, pg_ref, ag_ref, x_ref, w_ref, *rest):
        out_ref, mix_ref = rest[-4:-2] if head else rest
        pg, ag = pg_ref[...].astype(F32), ag_ref[...].astype(F32)
        mix_ref[:, 0:W] = (y_ref[...].astype(F32) * (pg * _sigmoid(pg))).astype(BF16)
        mix_ref[:, W:D] = (o_ref[...].astype(F32) * (ag * _sigmoid(ag))).astype(BF16)
        for n in range(D // tn):
            cols = slice(n * tn, (n + 1) * tn)
            out_ref[:, cols] = x_ref[:, cols] + jnp.dot(mix_ref[...], w_ref[:, cols], preferred_element_type=F32)
        if not head:
            return
        gam_ref, t_ref = rest[:2]
        loss_ref, dg_ref = rest[-2:]

        @pl.when(pl.program_id(0) == 0)
        def _():
            loss_ref[...] = jnp.zeros(loss_ref.shape, F32)
            dg_ref[...] = jnp.zeros(dg_ref.shape, F32)

        xf, gam_v = out_ref[...], gam_ref[...]
        r = lax.rsqrt(jnp.mean(xf * xf, axis=-1, keepdims=True) + RMS_EPS)
        xhat = xf * r
        err = xhat * gam_v - t_ref[...]
        part = jnp.sum(jnp.sum(err * err, axis=-1, keepdims=True), axis=0, keepdims=True)
        loss_ref[...] += part * (0.5 / D)
        dy = err * (1.0 / D)
        dg_ref[...] += jnp.sum(dy * xhat, axis=0, keepdims=True)
        dxhat = dy * gam_v
        out_ref[...] = r * (dxhat - xhat * jnp.mean(dxhat * xhat, axis=-1, keepdims=True))

    rows = lambda width, col: pl.BlockSpec((tm, width), lambda i: (i, col))
    in_specs = [rows(W, 0), rows(W, 0), rows(W, 1), rows(W, 5), rows(D, 0), pl.BlockSpec((D, D), lambda i: (0, 0))]
    out_specs = [rows(D, 0), rows(D, 0)]
    out_shape = [jax.ShapeDtypeStruct((S, D), F32), jax.ShapeDtypeStruct((S, D), BF16)]
    if head:
        in_specs += [pl.BlockSpec((1, D), lambda i: (0, 0)), rows(D, 0)]
        out_specs += [pl.BlockSpec((8, LANES), lambda i: (0, 0)), pl.BlockSpec((1, D), lambda i: (0, 0))]
        out_shape += [jax.ShapeDtypeStruct((8, LANES), F32), jax.ShapeDtypeStruct((1, D), F32)]
    return pl.pallas_call(
        body, name="outproj_fwd_loss" if head else "outproj_fwd", grid=(S // tm,),
        in_specs=in_specs, out_specs=out_specs, out_shape=out_shape,
        compiler_params=_params("arbitrary" if head else "parallel"),
    )(ypool, o, proj, proj, x, wout, *(head or ()))


def _outproj_bwd(g, wout, mixed, ypool, o, proj):
    S, D = g.shape
    W = D // 2
    tm = min(TM, S)

    def body(g_ref, w_ref, mix_ref, y_ref, o_ref, pg_ref, ag_ref, dw_ref, dwb_ref, da_ref, dgate_ref, doaug_ref):
        @pl.when(pl.program_id(0) == 0)
        def _():
            dw_ref[...] = jnp.zeros(dw_ref.shape, F32)

        gb = g_ref[...].astype(BF16)
        dmixes = [lax.dot_general(gb, w_ref[half * W:(half + 1) * W, :], NT, preferred_element_type=F32)
                  for half in range(2)]
        dw_ref[...] += lax.dot_general(mix_ref[...], gb, TN_DIMS, preferred_element_type=F32)
        d_o = None
        for half, (val_ref, gate_ref) in enumerate(((y_ref, pg_ref), (o_ref, ag_ref))):
            cols = slice(half * W, (half + 1) * W)
            gt = gate_ref[...].astype(F32)
            sg = _sigmoid(gt)
            d_o = (dmixes[half] * (gt * sg)).astype(BF16)
            da_ref[:, cols] = d_o
            dgate_ref[:, cols] = (dmixes[half] * val_ref[...].astype(F32)
                                  * (sg * (1.0 + gt * (1.0 - sg)))).astype(BF16)

        lane, halves = _head_halves(tm)
        slot = lane & (HEAD_DIM - 1)
        for p in range(W // LANES):
            cols = slice(p * LANES, (p + 1) * LANES)
            prod = d_o[:, cols].astype(F32) * o_ref[:, cols].astype(F32)
            d_a = jnp.sum(jnp.where(halves[0], prod, 0.0), axis=1, keepdims=True)
            d_b = jnp.sum(jnp.where(halves[1], prod, 0.0), axis=1, keepdims=True)
            aug = jnp.zeros((tm, LANES), F32)
            for n, part in enumerate(_split3(jnp.where(halves[0], d_b, d_a))):
                aug = jnp.where(slot == SLOT_C + n, -part, aug)
            doaug_ref[:, cols] = aug.astype(BF16)

        @pl.when(pl.program_id(0) == S // tm - 1)
        def _():
            dwb_ref[...] = dw_ref[...].astype(BF16)

    rows = lambda width, col: pl.BlockSpec((tm, width), lambda i: (i, col))
    whole = pl.BlockSpec((D, D), lambda i: (0, 0))
    return pl.pallas_call(
        body, name="outproj_bwd", grid=(S // tm,),
        in_specs=[rows(D, 0), whole, rows(D, 0), rows(W, 0), rows(W, 0), rows(W, 1), rows(W, 5)],
        out_specs=[whole, whole, rows(D, 0), rows(D, 0), rows(W, 0)],
        out_shape=[jax.ShapeDtypeStruct((D, D), F32),
                   jax.ShapeDtypeStruct((D, D), BF16),
                   jax.ShapeDtypeStruct((S, D), BF16),
                   jax.ShapeDtypeStruct((S, D), BF16),
                   jax.ShapeDtypeStruct((S, W), BF16)],
        compiler_params=_params("arbitrary"),
    )(g, wout, mixed, ypool, o, proj, proj)


def _section_specs(sections, rows, width):
    specs = [pl.BlockSpec((rows, width), lambda k, c=c: (k, c)) for _, c in sections]
    return specs, [a for a, _ in sections]


def _inproj_bwd_dw(h, sections, dzf):
    S, D = h.shape
    W = D // 2
    n_sec = len(sections)
    N = n_sec * W
    ts = min(2 * TM, S)
    n_steps = S // ts

    def body(h_ref, dz_ref, *rest):
        sec_refs, (dw_ref, dwb_ref) = rest[:n_sec], rest[n_sec:]

        @pl.when(pl.program_id(0) == 0)
        def _():
            dw_ref[...] = jnp.zeros(dw_ref.shape, F32)

        ht = h_ref[...].T
        dw_ref[:, N:N + LANES] += jnp.dot(ht, dz_ref[...], preferred_element_type=F32)
        for n, ref in enumerate(sec_refs):
            dw_ref[:, n * W:(n + 1) * W] += jnp.dot(ht, ref[...], preferred_element_type=F32)

        @pl.when(pl.program_id(0) == n_steps - 1)
        def _():
            dwb_ref[...] = dw_ref[...].astype(BF16)

    sec_specs, sec_arrays = _section_specs(sections, ts, W)
    whole = pl.BlockSpec((D, N + LANES), lambda k: (0, 0), pipeline_mode=pl.Buffered(1))
    return pl.pallas_call(
        body, name="inproj_bwd_dw", grid=(n_steps,),
        in_specs=[pl.BlockSpec((ts, D), lambda k: (k, 0)),
                  pl.BlockSpec((ts, LANES), lambda k: (k, 0))] + sec_specs,
        out_specs=[whole, whole],
        out_shape=[jax.ShapeDtypeStruct((D, N + LANES), F32),
                   jax.ShapeDtypeStruct((D, N + LANES), BF16)],
        compiler_params=_params("arbitrary"),
    )(h, dzf, *sec_arrays)


def _attn_bwd(proj, da, qaug, kaug, doaug):
    S = proj.shape[0]
    W = proj.shape[1] // 6
    P = W // LANES
    tq = min(TQ, S)
    nq = S // tq
    qc, kc, vc = 2 * P, 3 * P, 4 * P
    scale = 1.0 / math.sqrt(HEAD_DIM)

    def body(q_ref, k_ref, v_ref, do_ref, qa_ref, ka_ref, da_ref,
             dq_ref, dk_ref, dv_ref, drows_ref, dcols_ref, km_scr, vm_scr, dk_scr, dv_scr, dq_scr):
        pair, j = pl.program_id(0), pl.program_id(1)
        lane, halves = _head_halves(tq)

        @pl.when(jnp.logical_and(pair == 0, j == 0))
        def _():
            drows_ref[...] = jnp.zeros(drows_ref.shape, F32)
            dcols_ref[...] = jnp.zeros(dcols_ref.shape, F32)

        @pl.when(j == 0)
        def _():
            dq_scr[...] = jnp.zeros(dq_scr.shape, F32)

        v_ones = ((lane & (HEAD_DIM - 1)) < 3).astype(BF16)
        for a in range(2):
            km_scr[a] = jnp.where(halves[a], k_ref[...], ka_ref[...])
            vm_scr[a] = jnp.where(halves[a], v_ref[...], v_ones)
        dk_scr[...] = jnp.zeros(dk_scr.shape, F32)
        dv_scr[...] = jnp.zeros(dv_scr.shape, F32)

        def update(i, on_diagonal):
            rows = pl.ds(pl.multiple_of(i * tq, tq), tq)
            qs = q_ref[rows, :] * scale
            do2, qaug_t, doaug_t = do_ref[rows, :], qa_ref[rows, :], da_ref[rows, :]
            if on_diagonal:
                keep = (lax.broadcasted_iota(jnp.int32, (tq, tq), 0)
                        >= lax.broadcasted_iota(jnp.int32, (tq, tq), 1))
            qas = [jnp.where(halves[a], qs, qaug_t) for a in range(2)]
            logits = [lax.dot_general(qas[a], km_scr[a], NT, preferred_element_type=F32) for a in range(2)]
            dps = [lax.dot_general(jnp.where(halves[a], do2, doaug_t), vm_scr[a], NT, preferred_element_type=F32)
                   for a in range(2)]
            dv = None
            for a in range(2):
                s = jnp.where(keep, logits[a], NEG_INF) if on_diagonal else logits[a]
                p = jnp.exp(s)
                dsb = (p * dps[a]).astype(BF16)
                do0 = jnp.where(halves[a], do2, jnp.zeros_like(do2))
                dv_a = lax.dot_general(p.astype(BF16), do0, TN_DIMS, preferred_element_type=F32)
                dv = dv_a if dv is None else dv + dv_a
                dk_scr[a] += lax.dot_general(dsb, qas[a], TN_DIMS, preferred_element_type=F32)
                dq_scr[a, rows, :] += jnp.dot(dsb, km_scr[a], preferred_element_type=F32)
            dv_scr[...] += dv

        def below_diagonal(n, carry):
            update(j + 1 + 2 * n, False)
            update(j + 2 + 2 * n, False)
            return carry

        update(j, True)
        below = nq - 1 - j
        lax.fori_loop(0, below // 2, below_diagonal, 0)

        @pl.when(below % 2 == 1)
        def _():
            update(nq - 1, False)


        def to_head_lanes(old, first, second):
            at = lax.broadcasted_iota(jnp.int32, old.shape, 1) - 2 * pair
            return jnp.where(at == 0, first, jnp.where(at == 1, second, old))

        dk_ref[...] = jnp.where(halves[0], dk_scr[0], dk_scr[1]).astype(BF16)
        dv_ref[...] = dv_scr[...].astype(BF16)
        keys = pl.ds(pl.multiple_of(j * tq, tq), tq)
        ones_a, ones_b = HEAD_DIM + SLOT_ONE, SLOT_ONE
        dcols_ref[keys, :] = to_head_lanes(dcols_ref[keys, :], dk_scr[0][:, ones_a:ones_a + 1],
                                           dk_scr[1][:, ones_b:ones_b + 1])

        @pl.when(j == nq - 1)
        def _():
            row_lane, row_halves = _head_halves(S)
            dq_ref[...] = (jnp.where(row_halves[0], dq_scr[0], dq_scr[1]) * scale).astype(BF16)
            c_a, c_b = HEAD_DIM + SLOT_C, SLOT_C
            drows_ref[...] = to_head_lanes(drows_ref[...], dq_scr[0][:, c_a:c_a + 1], dq_scr[1][:, c_b:c_b + 1])

    tile = lambda col: pl.BlockSpec((tq, LANES), lambda p, j: (j, col + p))
    whole = lambda col: pl.BlockSpec((S, LANES), lambda p, j: (0, col + p))
    shared = pl.BlockSpec((S, LANES), lambda p, j: (0, 0))
    return pl.pallas_call(
        body, name="attn_bwd", grid=(P, nq),
        in_specs=[whole(qc), tile(kc), tile(vc), whole(P), whole(0), tile(0), whole(0)],
        out_specs=[whole(0), tile(0), tile(0), shared, shared],
        out_shape=[jax.ShapeDtypeStruct((S, W), BF16),
                   jax.ShapeDtypeStruct((S, W), BF16),
                   jax.ShapeDtypeStruct((S, W), BF16),
                   jax.ShapeDtypeStruct((S, LANES), F32),
                   jax.ShapeDtypeStruct((S, LANES), F32)],
        scratch_shapes=[pltpu.VMEM((2, tq, LANES), BF16),
                        pltpu.VMEM((2, tq, LANES), BF16),
                        pltpu.VMEM((2, tq, LANES), F32),
                        pltpu.VMEM((tq, LANES), F32),
                        pltpu.VMEM((2, S, LANES), F32)],
        compiler_params=_params("arbitrary", "arbitrary"),
    )(proj, proj, proj, da, qaug, kaug, doaug)


def _fgate_bwd(drows, dcols, z, bias):
    S = z.shape[0]
    tb = min(TB, S)
    nb = S // tb

    def body(drows_ref, dcols_ref, z_ref, b_ref, dz_ref, db_ref):
        tri = (lax.broadcasted_iota(jnp.int32, (tb, tb), 1)
               >= lax.broadcasted_iota(jnp.int32, (tb, tb), 0)).astype(F32)

        local = []
        for b in range(nb):
            rows = slice(b * tb, (b + 1) * tb)
            local.append(jnp.dot(tri, drows_ref[rows, :] - dcols_ref[rows, :], preferred_element_type=F32,
                                 precision=lax.Precision.HIGHEST))
        carry = jnp.zeros((1, LANES), F32)
        db = jnp.zeros((1, LANES), F32)
        for b in reversed(range(nb)):
            rows = slice(b * tb, (b + 1) * tb)
            rc = local[b] + carry
            carry = rc[0:1, :]
            dz = rc * _sigmoid(-(z_ref[rows, :] + b_ref[...]))
            dz_ref[rows, :] = dz.astype(BF16)
            db = db + jnp.sum(dz, axis=0, keepdims=True)
        db_ref[...] = db

    return pl.pallas_call(
        body, name="fgate_bwd",
        out_shape=[jax.ShapeDtypeStruct((S, LANES), BF16),
                   jax.ShapeDtypeStruct((1, LANES), F32)],
        compiler_params=pltpu.CompilerParams(vmem_limit_bytes=VMEM_LIMIT),
    )(drows, dcols, z, bias)


def _pool_bwd(proj, da, pool_w, pool_scale):
    S = proj.shape[0]
    G = len(POOL_WINDOWS)

    def body(u_ref, dy_ref, w_ref, s_ref, du_ref, dw_ref, ds_ref, pad_ref):
        g = pl.program_id(0)
        for gi, w in enumerate(POOL_WINDOWS):
            @pl.when(g == gi)
            def _():
                d, cnt = _window_mean_minus_self(u_ref[...].astype(F32), pad_ref, w, S)
                db = d.astype(BF16)
                wb = w_ref[0].astype(BF16)
                yraw = jnp.dot(db, wb, preferred_element_type=F32)
                dy = dy_ref[...].astype(F32)
                ds_ref[...] = jnp.sum(dy * yraw, axis=0, keepdims=True)
                dzb = (dy * s_ref[...]).astype(BF16)
                dw_ref[0] = lax.dot_general(db, dzb, TN_DIMS, preferred_element_type=F32)
                dd = lax.dot_general(dzb, wb, NT, preferred_element_type=F32)
                pad_ref[0:S, :] = dd / cnt
                pad_ref[S:S + MAX_WINDOW, :] = jnp.zeros((MAX_WINDOW, LANES), F32)
                acc = -dd
                for j in range(w):
                    acc = acc + pad_ref[j:j + S, :]
                du_ref[...] = acc.astype(BF16)

    return pl.pallas_call(
        body, name="pool_bwd", grid=(G,),
        in_specs=[pl.BlockSpec((S, LANES), lambda g: (0, g)),
                  pl.BlockSpec((S, LANES), lambda g: (0, g)),
                  pl.BlockSpec((1, LANES, LANES), lambda g: (g, 0, 0)),
                  pl.BlockSpec((1, LANES), lambda g: (0, g))],
        out_specs=[pl.BlockSpec((S, LANES), lambda g: (0, g)),
                   pl.BlockSpec((1, LANES, LANES), lambda g: (g, 0, 0)),
                   pl.BlockSpec((1, LANES), lambda g: (0, g))],
        out_shape=[jax.ShapeDtypeStruct((S, G * LANES), BF16),
                   jax.ShapeDtypeStruct((G, LANES, LANES), F32),
                   jax.ShapeDtypeStruct((1, G * LANES), F32)],
        scratch_shapes=[pltpu.VMEM((S + MAX_WINDOW, LANES), F32)],
        compiler_params=_params("arbitrary"),
    )(proj, da, pool_w, pool_scale)


def _inproj_bwd_dx(sections, dzf, w, x, gam, g, after=()):
    S, D = x.shape
    N = w.shape[1] - LANES
    W = D // 2
    n_sec = len(sections)
    tm = min(TM, S)

    def body(dz_ref, w_ref, wf_ref, x_ref, gam_ref, g_ref, *rest):
        sec_refs = rest[:n_sec]
        dx_ref, dg_ref = rest[-2:]

        @pl.when(pl.program_id(0) == 0)
        def _():
            dg_ref[...] = jnp.zeros(dg_ref.shape, F32)

        dh = lax.dot_general(dz_ref[...], wf_ref[...], NT, preferred_element_type=F32)
        for n, ref in enumerate(sec_refs):
            dh = dh + lax.dot_general(ref[...], w_ref[:, n * W:(n + 1) * W], NT, preferred_element_type=F32)
        xf = x_ref[...]
        r = lax.rsqrt(jnp.mean(xf * xf, axis=-1, keepdims=True) + RMS_EPS)
        xhat = xf * r
        dg_ref[...] += jnp.sum(dh * xhat, axis=0, keepdims=True)
        dxhat = dh * gam_ref[...]
        dx_ref[...] = g_ref[...] + r * (dxhat - xhat * jnp.mean(dxhat * xhat, axis=-1, keepdims=True))

    sec_specs, sec_arrays = _section_specs(sections, tm, W)
    return pl.pallas_call(
        body, name="inproj_bwd_dx", grid=(S // tm,),
        in_specs=[pl.BlockSpec((tm, LANES), lambda i: (i, 0)),
                  pl.BlockSpec((D, N), lambda i: (0, 0)),
                  pl.BlockSpec((D, LANES), lambda i: (0, N // LANES)),
                  pl.BlockSpec((tm, D), lambda i: (i, 0)),
                  pl.BlockSpec((1, D), lambda i: (0, 0)),
                  pl.BlockSpec((tm, D), lambda i: (i, 0))] + sec_specs + _after_specs(after),
        out_specs=[pl.BlockSpec((tm, D), lambda i: (i, 0)),
                   pl.BlockSpec((1, D), lambda i: (0, 0))],
        out_shape=[jax.ShapeDtypeStruct((S, D), F32),
                   jax.ShapeDtypeStruct((1, D), F32)],
        compiler_params=_params("arbitrary"),
    )(dzf, w, w, x, gam, g, *sec_arrays, *after)


def _adamw(w, m, v, gsets, name, rows, shifted=False, first=0, into=None):
    A, R, C = w.shape
    n_sets = len(gsets)
    tr = min(rows, R)
    c1 = 1.0 / (1.0 - ADAM_B1 ** ADAM_STEP)
    c2 = 1.0 / (1.0 - ADAM_B2 ** ADAM_STEP)
    counts = [len(gs) for gs in gsets]

    def body(w_ref, m_ref, v_ref, *rest):
        g_ref, d_ref, nm_ref, nv_ref = rest[-4:]
        at = 0
        for a in range(n_sets):
            part_refs = rest[at:at + counts[a]]
            at += counts[a]

            @pl.when(pl.program_id(0) == a)
            def _():
                g = None
                for ref in part_refs:
                    for s in range(ref.shape[0]):
                        term = ref[s].astype(F32)
                        g = term if g is None else g + term
                if shifted:
                    lanes = g.shape[1]
                    g = pltpu.roll(g, (lanes - _index(_position()) * (C % LANES)) % lanes, axis=1)[:, :C]
                nm = ADAM_B1 * m_ref[0] + (1.0 - ADAM_B1) * g
                nv = ADAM_B2 * v_ref[0] + (1.0 - ADAM_B2) * (g * g)
                g_ref[0] = g
                nm_ref[0] = nm
                nv_ref[0] = nv
                d_ref[0] = -ADAM_LR * ((nm * c1) / (jnp.sqrt(nv * c2) + ADAM_EPS) + ADAM_WD * w_ref[0])

    spec = pl.BlockSpec((1, tr, C), lambda a, r: (first + a, r, 0))
    part_specs = [pl.BlockSpec((part.shape[0], tr, part.shape[2]), lambda a, r, l=l: (0, jnp.where(a == l, r, 0), 0))
                  for l, gs in enumerate(gsets) for part in gs]
    parts = [part for gs in gsets for part in gs]
    shape = jax.ShapeDtypeStruct((A, R, C), F32)
    earlier = () if into is None else tuple(into)
    return pl.pallas_call(
        body, name=name, grid=(n_sets, R // tr),
        in_specs=[spec, spec, spec] + part_specs + _after_specs(earlier),
        out_specs=[spec, spec, spec, spec],
        out_shape=[shape, shape, shape, shape],
        input_output_aliases={3 + len(parts) + n: n for n in range(len(earlier))},
        compiler_params=_params("arbitrary", "arbitrary"),
    )(w, m, v, *parts, *earlier)


def _position():
    return lax.axis_index("x"), lax.axis_index("y"), lax.axis_index("c")


def _index(dev):
    return 4 * dev[0] + 2 * dev[1] + dev[2]


def _all_gather(arrs, slots, out_shapes, name):
    n_arr = len(arrs)

    def body(*refs):
        ins, outs = refs[:n_arr], refs[n_arr:2 * n_arr]
        send_sems, recv_sems, local_sems = refs[2 * n_arr:]
        x, y, c = _position()
        me, sibling = (x, y, c), (x, y, 1 - c)
        chips = [(1 - x, y), (x, 1 - y), (1 - x, 1 - y)]

        def copy(a, k, block, to, src=None):
            part = slots[a](outs[a], _index(block))
            return pltpu.make_async_remote_copy(
                src_ref=part if src is None else src, dst_ref=part,
                send_sem=send_sems.at[a, k], recv_sem=recv_sems.at[a, k],
                device_id=to, device_id_type=MESH)

        mine = [pltpu.make_async_copy(ins[a], slots[a](outs[a], _index(me)), local_sems.at[a])
                for a in range(n_arr)]
        for cp in mine:
            cp.start()
        first = []
        for a in range(n_arr):
            first.append(copy(a, 0, me, sibling, src=ins[a]))
            first += [copy(a, 1 + j, me, (*chip, c), src=ins[a]) for j, chip in enumerate(chips)]
        for cp in first:
            cp.start()
        passed = []
        for j, chip in enumerate(chips):
            for a in range(n_arr):
                copy(a, 1 + j, (*chip, c), me).wait_recv()
                fwd = copy(a, 4 + j, (*chip, c), sibling)
                fwd.start()
                passed.append(fwd)
        for a in range(n_arr):
            copy(a, 0, sibling, me).wait_recv()
            for j, chip in enumerate(chips):
                copy(a, 4 + j, (*chip, 1 - c), me).wait_recv()
        for cp in first + passed:
            cp.wait_send()
        for cp in mine:
            cp.wait()

    any_spec = pl.BlockSpec(memory_space=pl.ANY)
    return pl.pallas_call(
        body, name=name,
        in_specs=[any_spec] * n_arr, out_specs=[any_spec] * n_arr, out_shape=out_shapes,
        scratch_shapes=[pltpu.SemaphoreType.DMA((n_arr, 7)), pltpu.SemaphoreType.DMA((n_arr, 7)),
                        pltpu.SemaphoreType.DMA((n_arr,))],
    )(*arrs)


def _split_copies(srcs, lands, send_sems, recv_sems, kinds):
    x, y, c = _position()
    me = _index((x, y, c))
    copies = []
    for a, (src_part, land_part) in enumerate(kinds):
        for k in range(1, N_DEV):
            peer = (x ^ ((k >> 2) & 1), y ^ ((k >> 1) & 1), c ^ (k & 1))
            copies.append(pltpu.make_async_remote_copy(
                src_ref=src_part(srcs[a], _index(peer)), dst_ref=land_part(lands[a], me, k),
                send_sem=send_sems[a].at[k - 1], recv_sem=recv_sems[a].at[k - 1],
                device_id=peer, device_id_type=MESH))
    return copies


def _split_start(srcs, lands, kinds, name, after=()):
    n = len(srcs)

    def body(*refs):
        src_refs, land_refs = refs[:n], refs[n:2 * n]
        outs = refs[2 * n + len(after):]
        send_sems, recv_sems = outs[:n], outs[n:2 * n]
        token = outs[-1]
        for cp in _split_copies(src_refs, land_refs, send_sems, recv_sems, kinds):
            cp.start()
        token[...] = jnp.zeros(token.shape, token.dtype)

    hbm = pl.BlockSpec(memory_space=pltpu.HBM)
    sem = pl.BlockSpec(memory_space=pltpu.SEMAPHORE)
    operands = [pltpu.with_memory_space_constraint(t, pltpu.HBM) for t in (*srcs, *lands)]
    out = pl.pallas_call(
        body, name=name,
        in_specs=[hbm] * (2 * n) + _after_specs(after),
        out_specs=[sem] * (2 * n) + [hbm] * (2 * n) + [pl.BlockSpec(memory_space=pltpu.VMEM)],
        out_shape=[pltpu.SemaphoreType.DMA((N_DEV - 1,))] * (2 * n)
        + [pltpu.HBM(t.shape, t.dtype) for t in operands] + [jax.ShapeDtypeStruct((8, LANES), F32)],
        input_output_aliases={i: 2 * n + i for i in range(2 * n)},
        compiler_params=pltpu.CompilerParams(has_side_effects=pltpu.SideEffectType.DATAFLOW_SIDE_EFFECTING),
    )(*operands, *after)
    return [(out[a], out[n + a], out[2 * n + a], out[3 * n + a]) for a in range(n)], out[-1]


def _split_wait(started, kinds, after, name):
    n = len(started)
    sems = [t[0] for t in started] + [t[1] for t in started]
    srcs = [t[2] for t in started]
    lands = [t[3] for t in started]

    def body(*refs):
        src_refs, land_refs = refs[:n], refs[n:2 * n]
        send_sems, recv_sems = refs[2 * n:3 * n], refs[3 * n:4 * n]
        for cp in _split_copies(src_refs, land_refs, send_sems, recv_sems, kinds):
            cp.wait_send()
            cp.wait_recv()

    hbm = pl.BlockSpec(memory_space=pltpu.HBM)
    sem = pl.BlockSpec(memory_space=pltpu.SEMAPHORE)
    out = pl.pallas_call(
        body, name=name,
        in_specs=[hbm] * (2 * n) + [sem] * (2 * n) + _after_specs(after),
        out_specs=[hbm] * (2 * n),
        out_shape=[pltpu.HBM(t.shape, t.dtype) for t in (*srcs, *lands)],
        input_output_aliases={i: i for i in range(2 * n)},
        compiler_params=pltpu.CompilerParams(has_side_effects=pltpu.SideEffectType.DATAFLOW_SIDE_EFFECTING),
    )(*srcs, *lands, *sems, *after)
    return out[n:]


def _as_rows(p):
    if p.size % LANES == 0:
        rows = p.reshape(-1, LANES)
    else:
        rows = p.reshape(-1, p.shape[-1])
        rows = jnp.pad(rows, ((0, 0), (0, LANES - rows.shape[1])))
    return jnp.pad(rows, ((0, -rows.shape[0] % 8), (0, 0)))


def _pack(parts):
    return jnp.concatenate([_as_rows(p) for p in parts])[None]


def _unpack(packed, like):
    out, at = [], 0
    for p in like:
        whole = p.size % LANES == 0
        n = p.size // LANES if whole else p.size // p.shape[-1]
        rows = packed[0, at:at + n]
        out.append((rows if whole else rows[:, :p.shape[-1]]).reshape(p.shape))
        at += n + (-n % 8)
    return out


def _local_step(x, target, norm_g, forget_bias, pool_w, pool_scale, final_g, weights_in, weights_out, on_grads,
                first_after=()):
    L = norm_g.shape[0]
    S, D = x.shape
    W = D // 2
    H = W // HEAD_DIM
    bias = jnp.pad(forget_bias, ((0, 0), (0, LANES - H)))

    saved = []
    after = tuple(first_after)
    for l in range(L):
        proj, h, z, w = _inproj_fwd(x, norm_g[l:l + 1], weights_in(l, x), after)
        after = ()
        qaug, kaug = _fgate_fwd(z, bias[l:l + 1], H)
        ypool = _pool_fwd(proj, pool_w[l], pool_scale[l:l + 1])
        o, qaug_b = _attn_fwd(proj, qaug, kaug)
        wout = weights_out(l, o)
        x_in = x
        if l < L - 1:
            x, mixed = _outproj_fwd(ypool, o, proj, x_in, wout)
        else:
            g, mixed, loss, d_final_g = _outproj_fwd(ypool, o, proj, x_in, wout, (final_g.reshape(1, D), target))
        saved.append((x_in, proj, h, z, qaug_b, kaug, ypool, o, mixed, w, wout))

    small = None
    for l in reversed(range(L)):
        x_in, proj, h, z, qaug_b, kaug, ypool, o, mixed, w, wout = saved[l]
        d_wout, d_wout_bf16, da, dgate, doaug = _outproj_bwd(g, wout, mixed, ypool, o, proj)
        dq, dk, dv, drows, dcols = _attn_bwd(proj, da, qaug_b, kaug, doaug)
        dzf, db = _fgate_bwd(drows, dcols, z, bias[l:l + 1])
        dpu, dpw, dps = _pool_bwd(proj, da, pool_w[l], pool_scale[l:l + 1])
        dproj = [(dpu, 0), (dgate, 0), (dq, 0), (dk, 0), (dv, 0), (dgate, 1)]
        d_w, d_w_bf16 = _inproj_bwd_dw(h, dproj, dzf)
        after = tuple(on_grads(l, d_w, d_w_bf16, d_wout, d_wout_bf16, small))
        g, dgam = _inproj_bwd_dx(dproj, dzf, w, x_in, norm_g[l:l + 1], g, after)
        small = (dgam[0], db[0, :H], dpw, dps[0])
    return loss[0, 0], g, small, d_final_g[0]


def kernel(x, norm_g, w_in, forget_bias, pool_w, pool_scale, w_out, final_g, loss_target, m_norm_g, m_w_in, m_forget_bias, m_pool_w, m_pool_scale, m_w_out, m_final_g, v_norm_g, v_w_in, v_forget_bias, v_pool_w, v_pool_scale, v_w_out, v_final_g):
    L, D, cols = w_in.shape
    rows_out = w_out.shape[1]
    me = _index(_position())
    slot = _slot_width(cols)
    wout_b = w_out.astype(BF16)
    win_b = _shift_slots(w_in)
    gather_in = (lambda ref, peer: ref, lambda ref, mine, k: ref.at[mine])
    gather_out = (lambda ref, peer: ref, lambda ref, mine, k: ref.at[pl.ds(mine * rows_out, rows_out), :])

    def landing(block, n_slots):
        zone = lax.empty((n_slots * block.shape[0], *block.shape[1:]), block.dtype)
        return lax.dynamic_update_slice(zone, block, (me * block.shape[0],) + (0,) * (block.ndim - 1))

    (first_in,) = _all_gather([win_b[0]], [lambda ref, n: ref.at[n]],
                              [jax.ShapeDtypeStruct((N_DEV, D, slot), BF16)], "gather_first")
    rest_srcs = [wout_b[0]] + [w[l] for l in range(1, L) for w in (win_b, wout_b)]
    rest_lands = [landing(wout_b[0], N_DEV)]
    for l in range(1, L):
        rest_lands += [landing(win_b[l][None], N_DEV), landing(wout_b[l], N_DEV)]
    rest_kinds = [gather_out] + [gather_in, gather_out] * (L - 1)
    rest, rest_token = _split_start(rest_srcs, rest_lands, rest_kinds, "gather_start_rest", (first_in,))

    def weights_in(l, x_in):
        if l == 0:
            return first_in
        (win_all,) = _split_wait([rest[2 * l - 1]], [gather_in], (x_in,), f"gather_wait_in_{l}")
        return win_all

    def weights_out(l, o):
        (wout_full,) = _split_wait([rest[2 * l]], [gather_out], (o,), f"gather_wait_out_{l}")
        return wout_full

    stride = slot - LANES
    exchange_kinds = [(lambda ref, peer: ref.at[:, pl.ds(pl.multiple_of(peer * stride, LANES), slot)],
                       lambda ref, mine, k: ref.at[k - 1]),
                      (lambda ref, peer: ref.at[pl.ds(peer * rows_out, rows_out), :],
                       lambda ref, mine, k: ref.at[k - 1])]
    zero_g = jnp.zeros_like(final_g)
    zero_loss = jnp.zeros((LANES,), F32)

    def small_pack(l, norm_g_l, bias_l, pool_w_l, pool_scale_l, final, loss_row=None):
        return _pack([norm_g_l, bias_l, pool_w_l, pool_scale_l, final if l == 0 else zero_g,
                      zero_loss if loss_row is None else loss_row])[0]

    exchanges, own_parts = {}, {}

    def on_grads(l, dw, dw_bf16, d_wout, d_wout_bf16, small):
        own_parts[l] = (lax.dynamic_slice_in_dim(dw, me * stride, slot, 1)[None],
                        lax.dynamic_slice_in_dim(d_wout, me * rows_out, rows_out, 0)[None])
        srcs = [dw_bf16, d_wout_bf16]
        lands = [lax.empty((N_DEV - 1, D, slot), BF16), lax.empty((N_DEV - 1, rows_out, D), BF16)]
        kinds = list(exchange_kinds)
        if small is not None:
            packed_small = small_pack(l + 1, *small, None)
            srcs.append(packed_small)
            lands.append(landing(packed_small[None], N_DEV))
            kinds.append(gather_in)
        started, token = _split_start(srcs, lands, kinds, f"exchange_start_{l}")
        exchanges[l] = (started, kinds)
        return (token,)

    loss, dx, small_first, d_final_g = _local_step(
        x[0], loss_target[0], norm_g, forget_bias, pool_w, pool_scale, final_g,
        weights_in, weights_out, on_grads, (rest_token,))
    packed_first = small_pack(0, *small_first, d_final_g, jnp.full((LANES,), loss, F32))
    first_started, first_token = _split_start(
        [packed_first], [landing(packed_first[None], N_DEV)], [gather_in],
        "small_start", (w_in, m_w_in, v_w_in, *own_parts[0]))

    gin_sets, gout_sets, small_sets = [None] * L, [None] * L, [None] * L

    def wait_for(l, after):
        started, kinds = exchanges[l]
        got = _split_wait(started, kinds, after, f"exchange_wait_{l}")
        gin_sets[l] = [own_parts[l][0], got[0]]
        gout_sets[l] = [own_parts[l][1], got[1]]
        if len(got) > 2:
            small_sets[l + 1] = [got[2]]

    for l in range(1, L):
        wait_for(l, (dx, first_token))
    rest_in = _adamw(w_in, m_w_in, v_w_in, gin_sets[1:], "adamw_w_in_rest", TM // 2, shifted=True, first=1)
    rest_out = _adamw(w_out, m_w_out, v_w_out, gout_sets[1:], "adamw_w_out_rest", rows_out, first=1)
    wait_for(0, (rest_in[1], rest_out[1]))
    g_w_in, d_w_in, nm_w_in, nv_w_in = _adamw(w_in, m_w_in, v_w_in, gin_sets[:1], "adamw_w_in_first", TM // 2,
                                              shifted=True, into=rest_in)
    g_w_out, d_w_out, nm_w_out, nv_w_out = _adamw(w_out, m_w_out, v_w_out, gout_sets[:1], "adamw_w_out_first",
                                                  rows_out, into=rest_out)
    small_sets[0] = _split_wait(first_started, [gather_in], (d_w_in, d_w_out), "small_wait")
    loss = jnp.sum(small_sets[0][0][:, packed_first.shape[0] - 8, 0])

    def small_stack(norm_g_, bias_, pool_w_, pool_scale_, final):
        return jnp.stack([small_pack(l, norm_g_[l], bias_[l], pool_w_[l], pool_scale_[l], final) for l in range(L)])

    packed = _adamw(small_stack(norm_g, forget_bias, pool_w, pool_scale, final_g),
                    small_stack(m_norm_g, m_forget_bias, m_pool_w, m_pool_scale, m_final_g),
                    small_stack(v_norm_g, v_forget_bias, v_pool_w, v_pool_scale, v_final_g),
                    small_sets, "adamw_small", packed_first.shape[0])

    def small_unpack(p):
        like = [norm_g[0], forget_bias[0], pool_w[0], pool_scale[0], final_g]
        layers = [_unpack(p[l:l + 1], like) for l in range(L)]
        return [jnp.stack([layers[l][n] for l in range(L)]) for n in range(4)] + [layers[0][4]]

    g_s, d_s, nm_s, nv_s = [small_unpack(p) for p in packed]

    def order(big_in, big_out, small):
        return (small[0], big_in, small[1], small[2], small[3], big_out, small[4])

    return (loss, dx[None], *order(g_w_in, g_w_out, g_s), *order(d_w_in, d_w_out, d_s),
            *order(nm_w_in, nm_w_out, nm_s), *order(nv_w_in, nv_w_out, nv_s))
```

```python
import math

import jax
import jax.numpy as jnp
from jax import lax
from jax.experimental import pallas as pl
from jax.experimental.pallas import tpu as pltpu

F32 = jnp.float32
BF16 = jnp.bfloat16
MESH = pl.DeviceIdType.MESH

RMS_EPS = 1e-6
NEG_INF = -1e30
HEAD_DIM = 64
POOL_WINDOWS = (2, 4, 8, 16)
MAX_WINDOW = 16
LANES = 128
N_DEV = 8

ADAM_LR = 0.001
ADAM_B1 = 0.9
ADAM_B2 = 0.999
ADAM_EPS = 1e-08
ADAM_WD = 0.01
ADAM_STEP = 10

TM = 512
TN = 512
TQ = 512
TB = 256
VMEM_LIMIT = 56 * 1024 * 1024

NT = (((1,), (1,)), ((), ()))
TN_DIMS = (((0,), (0,)), ((), ()))

SLOT_C, SLOT_ONE, SLOT_LSE = 0, 3, 6


def _params(*sem):
    return pltpu.CompilerParams(dimension_semantics=sem, vmem_limit_bytes=VMEM_LIMIT)


def _sigmoid(x):
    return 1.0 / (1.0 + jnp.exp(-x))


def _split3(x):
    hi = x.astype(BF16).astype(F32)
    rest = x - hi
    mid = rest.astype(BF16).astype(F32)
    return hi, mid, rest - mid


def _after_specs(after):
    return [pl.BlockSpec(memory_space=pl.ANY)] * len(after)


def _slot_width(cols):
    return LANES * (-(-(cols + (N_DEV - 1) * (cols % LANES)) // LANES))


def _shift_slots(w_in):
    L, D, cols = w_in.shape
    slot = _slot_width(cols)
    tr = min(TM // 2, D)

    def body(w_ref, o_ref, pad_scr):
        pad_scr[...] = jnp.zeros(pad_scr.shape, F32)
        pad_scr[:, 0:cols] = w_ref[0]
        o_ref[0] = pltpu.roll(pad_scr[...], _index(_position()) * (cols % LANES), axis=1).astype(BF16)

    return pl.pallas_call(
        body, name="shift_slots", grid=(L, D // tr),
        in_specs=[pl.BlockSpec((1, tr, cols), lambda l, r: (l, r, 0))],
        out_specs=pl.BlockSpec((1, tr, slot), lambda l, r: (l, r, 0)),
        out_shape=jax.ShapeDtypeStruct((L, D, slot), BF16),
        scratch_shapes=[pltpu.VMEM((tr, slot), F32)],
        compiler_params=_params("parallel", "parallel"),
    )(w_in)


def _inproj_fwd(x, gam, slots, after=()):
    S, D = x.shape
    n_dev, _, sw = slots.shape
    stride = sw - LANES
    width = stride * n_dev + LANES
    N = width - LANES
    tm, tn = min(TM, S), TN

    def body(x_ref, g_ref, s_ref, *rest):
        proj_ref, h_ref, z_ref, w_ref = rest[-4:]

        @pl.when(pl.program_id(0) == 0)
        def _():
            for n in range(n_dev):
                base = stride * n
                first = s_ref[n, :, 0:LANES]
                if n > 0:
                    first = first + s_ref[n - 1, :, stride:sw]
                w_ref[:, base:base + LANES] = first
                w_ref[:, base + LANES:base + stride] = s_ref[n, :, LANES:stride]
            w_ref[:, stride * n_dev:width] = s_ref[n_dev - 1, :, stride:sw]

        xf = x_ref[...]
        r = lax.rsqrt(jnp.mean(xf * xf, axis=-1, keepdims=True) + RMS_EPS)
        h = ((xf * r) * g_ref[...]).astype(BF16)
        h_ref[...] = h
        z_ref[...] = jnp.dot(h, w_ref[:, N:width], preferred_element_type=F32)
        for n in range(N // tn):
            cols = slice(n * tn, (n + 1) * tn)
            proj_ref[:, cols] = jnp.dot(h, w_ref[:, cols], preferred_element_type=F32).astype(BF16)

    return pl.pallas_call(
        body, name="inproj_fwd", grid=(S // tm,),
        in_specs=[pl.BlockSpec((tm, D), lambda i: (i, 0)),
                  pl.BlockSpec((1, D), lambda i: (0, 0)),
                  pl.BlockSpec((n_dev, D, sw), lambda i: (0, 0, 0))] + _after_specs(after),
        out_specs=[pl.BlockSpec((tm, N), lambda i: (i, 0)),
                   pl.BlockSpec((tm, D), lambda i: (i, 0)),
                   pl.BlockSpec((tm, LANES), lambda i: (i, 0)),
                   pl.BlockSpec((D, width), lambda i: (0, 0))],
        out_shape=[jax.ShapeDtypeStruct((S, N), BF16),
                   jax.ShapeDtypeStruct((S, D), BF16),
                   jax.ShapeDtypeStruct((S, LANES), F32),
                   jax.ShapeDtypeStruct((D, width), BF16)],
        compiler_params=_params("arbitrary"),
    )(x, gam, slots, *after)


def _fgate_fwd(z, bias, n_heads):
    S = z.shape[0]
    tb = min(TB, S)
    P = n_heads // 2
    assert n_heads <= 8, "the three parts of c are packed eight lanes apart"

    def body(z_ref, b_ref, qaug_ref, kaug_ref):
        lane = lax.broadcasted_iota(jnp.int32, (tb, LANES), 1)
        tri = (lax.broadcasted_iota(jnp.int32, (tb, tb), 0)
               >= lax.broadcasted_iota(jnp.int32, (tb, tb), 1)).astype(F32)
        row = lax.broadcasted_iota(jnp.int32, (LANES, P * LANES), 0)
        col = lax.broadcasted_iota(jnp.int32, (LANES, P * LANES), 1)
        head, part_n = row & 7, row >> 3
        home = (head >> 1) * LANES + jnp.where((head & 1) == 0, HEAD_DIM, 0)
        is_part = jnp.logical_and(head < n_heads, part_n < 3)
        place_q = jnp.logical_and(is_part, col == home + SLOT_C + part_n).astype(BF16)
        place_k = jnp.logical_and(is_part, col == home + SLOT_ONE + part_n).astype(BF16)
        slot = lax.broadcasted_iota(jnp.int32, (tb, P * LANES), 1) & (HEAD_DIM - 1)
        q_ones = jnp.logical_and(slot >= SLOT_ONE, slot < SLOT_ONE + 3).astype(F32)
        k_ones = jnp.logical_or(slot < SLOT_C + 3,
                                jnp.logical_and(slot >= SLOT_LSE, slot < SLOT_LSE + 3)).astype(F32)

        local = []
        for b in range(S // tb):
            zz = z_ref[b * tb:(b + 1) * tb, :] + b_ref[...]
            lf = jnp.minimum(zz, 0.0) - jnp.log(1.0 + jnp.exp(-jnp.abs(zz)))
            lf = jnp.where(lane < n_heads, lf, 0.0)
            local.append(jnp.dot(tri, lf, preferred_element_type=F32, precision=lax.Precision.HIGHEST))
        carry = jnp.zeros((1, LANES), F32)
        for b, part_sum in enumerate(local):
            c = part_sum + carry
            carry = c[tb - 1:tb, :]
            hi, mid, lo = _split3(c)
            packed = (hi + pltpu.roll(mid, 8, axis=1) + pltpu.roll(lo, 16, axis=1)).astype(BF16)
            qaug_ref[b * tb:(b + 1) * tb, :] = (
                q_ones + jnp.dot(packed, place_q, preferred_element_type=F32)).astype(BF16)
            kaug_ref[b * tb:(b + 1) * tb, :] = (
                k_ones - jnp.dot(packed, place_k, preferred_element_type=F32)).astype(BF16)

    return pl.pallas_call(
        body, name="fgate_fwd",
        out_shape=[jax.ShapeDtypeStruct((S, P * LANES), BF16),
                   jax.ShapeDtypeStruct((S, P * LANES), BF16)],
        compiler_params=pltpu.CompilerParams(vmem_limit_bytes=VMEM_LIMIT),
    )(z, bias)


def _window_mean_minus_self(u, pad_ref, w, S):
    pad_ref[0:MAX_WINDOW, :] = jnp.zeros((MAX_WINDOW, LANES), F32)
    pad_ref[MAX_WINDOW:MAX_WINDOW + S, :] = u
    acc = u
    for j in range(1, w):
        acc = acc + pad_ref[MAX_WINDOW - j:MAX_WINDOW - j + S, :]
    t = lax.broadcasted_iota(jnp.int32, (S, LANES), 0)
    cnt = jnp.minimum(t + 1, w).astype(F32)
    return acc / cnt - u, cnt


def _pool_fwd(proj, pool_w, pool_scale):
    S = proj.shape[0]
    G = len(POOL_WINDOWS)

    def body(u_ref, w_ref, s_ref, y_ref, pad_ref):
        g = pl.program_id(0)
        for gi, w in enumerate(POOL_WINDOWS):
            @pl.when(g == gi)
            def _():
                d, _ = _window_mean_minus_self(u_ref[...].astype(F32), pad_ref, w, S)
                y = jnp.dot(d.astype(BF16), w_ref[0].astype(BF16), preferred_element_type=F32)
                y_ref[...] = (y * s_ref[...]).astype(BF16)

    return pl.pallas_call(
        body, name="pool_fwd", grid=(G,),
        in_specs=[pl.BlockSpec((S, LANES), lambda g: (0, g)),
                  pl.BlockSpec((1, LANES, LANES), lambda g: (g, 0, 0)),
                  pl.BlockSpec((1, LANES), lambda g: (0, g))],
        out_specs=pl.BlockSpec((S, LANES), lambda g: (0, g)),
        out_shape=jax.ShapeDtypeStruct((S, G * LANES), BF16),
        scratch_shapes=[pltpu.VMEM((S + MAX_WINDOW, LANES), F32)],
        compiler_params=_params("arbitrary"),
    )(proj, pool_w, pool_scale)


def _head_halves(rows):
    lane = lax.broadcasted_iota(jnp.int32, (rows, LANES), 1)
    return lane, (lane < HEAD_DIM, lane >= HEAD_DIM)


def _attn_fwd(proj, qaug, kaug):
    S = proj.shape[0]
    W = proj.shape[1] // 6
    P = W // LANES
    tk = min(TQ, S // 2)
    tq = 2 * tk
    nq = S // tq
    qc, kc, vc = 2 * P, 3 * P, 4 * P
    scale = 1.0 / math.sqrt(HEAD_DIM)

    def body(q_ref, k_ref, v_ref, qa_ref, ka_ref, o_ref, qb_ref, qm_scr, m_scr, acc_scr):
        i = pl.program_id(1)
        lane, halves = _head_halves(tq)
        key_halves = (halves[0][:tk], halves[1][:tk])
        v_ones = ((lane[:tk] & (HEAD_DIM - 1)) < 3).astype(BF16)
        qs = q_ref[...] * scale
        qm_scr[0] = jnp.where(halves[0], qs, qa_ref[...])
        qm_scr[1] = jnp.where(halves[1], qs, qa_ref[...])
        m_scr[...] = jnp.full(m_scr.shape, NEG_INF, F32)
        acc_scr[...] = jnp.zeros(acc_scr.shape, F32)
        top, bottom, both = slice(0, tk), slice(tk, tq), slice(0, tq)

        def update(rows, j, on_diagonal):
            keys = pl.ds(pl.multiple_of(j * tk, tk), tk)
            k2, v2, kaug_t = k_ref[keys, :], v_ref[keys, :], ka_ref[keys, :]
            if on_diagonal:
                keep = (lax.broadcasted_iota(jnp.int32, (tk, tk), 0)
                        >= lax.broadcasted_iota(jnp.int32, (tk, tk), 1))
            logits = [lax.dot_general(qm_scr[a, rows, :], jnp.where(key_halves[a], k2, kaug_t), NT,
                                      preferred_element_type=F32) for a in range(2)]
            for a in range(2):
                s = jnp.where(keep, logits[a], NEG_INF) if on_diagonal else logits[a]
                va = jnp.where(key_halves[a], v2, v_ones)
                m_prev = m_scr[a, rows, :]
                m_new = jnp.maximum(m_prev, jnp.max(s, axis=1, keepdims=True))
                p = jnp.exp(s - jnp.tile(m_new, (1, tk // LANES)))
                acc_scr[a, rows, :] = (jnp.exp(m_prev - m_new) * acc_scr[a, rows, :]
                                       + jnp.dot(p.astype(BF16), va, preferred_element_type=F32))
                m_scr[a, rows, :] = m_new

        def below_diagonal(jj, carry):
            update(both, 2 * jj, False)
            update(both, 2 * jj + 1, False)
            return carry

        lax.fori_loop(0, i, below_diagonal, 0)
        update(top, 2 * i, True)
        update(bottom, 2 * i, False)
        update(bottom, 2 * i + 1, True)
        acc_a, acc_b = acc_scr[0], acc_scr[1]
        l_a, l_b = acc_a[:, HEAD_DIM:HEAD_DIM + 1], acc_b[:, 0:1]
        o_ref[...] = jnp.where(halves[0], acc_a / l_a, acc_b / l_b).astype(BF16)
        lse = jnp.where(halves[0], m_scr[1] + jnp.log(l_b), m_scr[0] + jnp.log(l_a))
        slot = lane & (HEAD_DIM - 1)
        aug = qa_ref[...].astype(F32)
        for n, part in enumerate(_split3(lse)):
            aug = jnp.where(slot == SLOT_LSE + n, -part, aug)
        qb_ref[...] = aug.astype(BF16)

    tile = lambda col: pl.BlockSpec((tq, LANES), lambda p, i: (i, col + p))
    whole = lambda col: pl.BlockSpec((S, LANES), lambda p, i: (0, col + p))
    return pl.pallas_call(
        body, name="attn_fwd", grid=(P, nq),
        in_specs=[tile(qc), whole(kc), whole(vc), tile(0), whole(0)],
        out_specs=[tile(0), tile(0)],
        out_shape=[jax.ShapeDtypeStruct((S, W), BF16), jax.ShapeDtypeStruct((S, W), BF16)],
        scratch_shapes=[pltpu.VMEM((2, tq, LANES), BF16),
                        pltpu.VMEM((2, tq, LANES), F32),
                        pltpu.VMEM((2, tq, LANES), F32)],
        compiler_params=_params("parallel", "arbitrary"),
    )(proj, proj, proj, qaug, kaug)


def _outproj_fwd(ypool, o, proj, x, wout, head=None):
    S, D = x.shape
    W = D // 2
    tm, tn = min(TM, S), TN

    def body(y_ref, o_ref, pg_ref, ag_ref, x_ref, w_ref, *rest):
        out_ref, mix_ref = rest[-4:-2] if head else rest
        pg, ag = pg_ref[...].astype(F32), ag_ref[...].astype(F32)
        mix_ref[:, 0:W] = (y_ref[...].astype(F32) * (pg * _sigmoid(pg))).astype(BF16)
        mix_ref[:, W:D] = (o_ref[...].astype(F32) * (ag * _sigmoid(ag))).astype(BF16)
        for n in range(D // tn):
            cols = slice(n * tn, (n + 1) * tn)
            out_ref[:, cols] = x_ref[:, cols] + jnp.dot(mix_ref[...], w_ref[:, cols], preferred_element_type=F32)
        if not head:
            return
        gam_ref, t_ref = rest[:2]
        loss_ref, dg_ref = rest[-2:]

        @pl.when(pl.program_id(0) == 0)
        def _():
            loss_ref[...] = jnp.zeros(loss_ref.shape, F32)
            dg_ref[...] = jnp.zeros(dg_ref.shape, F32)

        xf, gam_v = out_ref[...], gam_ref[...]
        r = lax.rsqrt(jnp.mean(xf * xf, axis=-1, keepdims=True) + RMS_EPS)
        xhat = xf * r
        err = xhat * gam_v - t_ref[...]
        part = jnp.sum(jnp.sum(err * err, axis=-1, keepdims=True), axis=0, keepdims=True)
        loss_ref[...] += part * (0.5 / D)
        dy = err * (1.0 / D)
        dg_ref[...] += jnp.sum(dy * xhat, axis=0, keepdims=True)
        dxhat = dy * gam_v
        out_ref[...] = r * (dxhat - xhat * jnp.mean(dxhat * xhat, axis=-1, keepdims=True))

    rows = lambda width, col: pl.BlockSpec((tm, width), lambda i: (i, col))
    in_specs = [rows(W, 0), rows(W, 0), rows(W, 1), rows(W, 5), rows(D, 0), pl.BlockSpec((D, D), lambda i: (0, 0))]
    out_specs = [rows(D, 0), rows(D, 0)]
    out_shape = [jax.ShapeDtypeStruct((S, D), F32), jax.ShapeDtypeStruct((S, D), BF16)]
    if head:
        in_specs += [pl.BlockSpec((1, D), lambda i: (0, 0)), rows(D, 0)]
        out_specs += [pl.BlockSpec((8, LANES), lambda i: (0, 0)), pl.BlockSpec((1, D), lambda i: (0, 0))]
        out_shape += [jax.ShapeDtypeStruct((8, LANES), F32), jax.ShapeDtypeStruct((1, D), F32)]
    return pl.pallas_call(
        body, name="outproj_fwd_loss" if head else "outproj_fwd", grid=(S // tm,),
        in_specs=in_specs, out_specs=out_specs, out_shape=out_shape,
        compiler_params=_params("arbitrary" if head else "parallel"),
    )(ypool, o, proj, proj, x, wout, *(head or ()))


def _outproj_bwd(g, wout, mixed, ypool, o, proj):
    S, D = g.shape
    W = D // 2
    tm = min(TM, S)

    def body(g_ref, w_ref, mix_ref, y_ref, o_ref, pg_ref, ag_ref, dw_ref, dwb_ref, da_ref, dgate_ref, doaug_ref):
        @pl.when(pl.program_id(0) == 0)
        def _():
            dw_ref[...] = jnp.zeros(dw_ref.shape, F32)

        gb = g_ref[...].astype(BF16)
        dmixes = [lax.dot_general(gb, w_ref[half * W:(half + 1) * W, :], NT, preferred_element_type=F32)
                  for half in range(2)]
        dw_ref[...] += lax.dot_general(mix_ref[...], gb, TN_DIMS, preferred_element_type=F32)
        d_o = None
        for half, (val_ref, gate_ref) in enumerate(((y_ref, pg_ref), (o_ref, ag_ref))):
            cols = slice(half * W, (half + 1) * W)
            gt = gate_ref[...].astype(F32)
            sg = _sigmoid(gt)
            d_o = (dmixes[half] * (gt * sg)).astype(BF16)
            da_ref[:, cols] = d_o
            dgate_ref[:, cols] = (dmixes[half] * val_ref[...].astype(F32)
                                  * (sg * (1.0 + gt * (1.0 - sg)))).astype(BF16)

        lane, halves = _head_halves(tm)
        slot = lane & (HEAD_DIM - 1)
        for p in range(W // LANES):
            cols = slice(p * LANES, (p + 1) * LANES)
            prod = d_o[:, cols].astype(F32) * o_ref[:, cols].astype(F32)
            d_a = jnp.sum(jnp.where(halves[0], prod, 0.0), axis=1, keepdims=True)
            d_b = jnp.sum(jnp.where(halves[1], prod, 0.0), axis=1, keepdims=True)
            aug = jnp.zeros((tm, LANES), F32)
            for n, part in enumerate(_split3(jnp.where(halves[0], d_b, d_a))):
                aug = jnp.where(slot == SLOT_C + n, -part, aug)
            doaug_ref[:, cols] = aug.astype(BF16)

        @pl.when(pl.program_id(0) == S // tm - 1)
        def _():
            dwb_ref[...] = dw_ref[...].astype(BF16)

    rows = lambda width, col: pl.BlockSpec((tm, width), lambda i: (i, col))
    whole = pl.BlockSpec((D, D), lambda i: (0, 0))
    return pl.pallas_call(
        body, name="outproj_bwd", grid=(S // tm,),
        in_specs=[rows(D, 0), whole, rows(D, 0), rows(W, 0), rows(W, 0), rows(W, 1), rows(W, 5)],
        out_specs=[whole, whole, rows(D, 0), rows(D, 0), rows(W, 0)],
        out_shape=[jax.ShapeDtypeStruct((D, D), F32),
                   jax.ShapeDtypeStruct((D, D), BF16),
                   jax.ShapeDtypeStruct((S, D), BF16),
                   jax.ShapeDtypeStruct((S, D), BF16),
                   jax.ShapeDtypeStruct((S, W), BF16)],
        compiler_params=_params("arbitrary"),
    )(g, wout, mixed, ypool, o, proj, proj)


def _section_specs(sections, rows, width):
    specs = [pl.BlockSpec((rows, width), lambda k, c=c: (k, c)) for _, c in sections]
    return specs, [a for a, _ in sections]


def _inproj_bwd_dw(h, sections, dzf):
    S, D = h.shape
    W = D // 2
    n_sec = len(sections)
    N = n_sec * W
    ts = min(2 * TM, S)
    n_steps = S // ts

    def body(h_ref, dz_ref, *rest):
        sec_refs, (dw_ref, dwb_ref) = rest[:n_sec], rest[n_sec:]

        @pl.when(pl.program_id(0) == 0)
        def _():
            dw_ref[...] = jnp.zeros(dw_ref.shape, F32)

        ht = h_ref[...].T
        dw_ref[:, N:N + LANES] += jnp.dot(ht, dz_ref[...], preferred_element_type=F32)
        for n, ref in enumerate(sec_refs):
            dw_ref[:, n * W:(n + 1) * W] += jnp.dot(ht, ref[...], preferred_element_type=F32)

        @pl.when(pl.program_id(0) == n_steps - 1)
        def _():
            dwb_ref[...] = dw_ref[...].astype(BF16)

    sec_specs, sec_arrays = _section_specs(sections, ts, W)
    whole = pl.BlockSpec((D, N + LANES), lambda k: (0, 0), pipeline_mode=pl.Buffered(1))
    return pl.pallas_call(
        body, name="inproj_bwd_dw", grid=(n_steps,),
        in_specs=[pl.BlockSpec((ts, D), lambda k: (k, 0)),
                  pl.BlockSpec((ts, LANES), lambda k: (k, 0))] + sec_specs,
        out_specs=[whole, whole],
        out_shape=[jax.ShapeDtypeStruct((D, N + LANES), F32),
                   jax.ShapeDtypeStruct((D, N + LANES), BF16)],
        compiler_params=_params("arbitrary"),
    )(h, dzf, *sec_arrays)


def _attn_bwd(proj, da, qaug, kaug, doaug):
    S = proj.shape[0]
    W = proj.shape[1] // 6
    P = W // LANES
    tq = min(TQ, S)
    nq = S // tq
    qc, kc, vc = 2 * P, 3 * P, 4 * P
    scale = 1.0 / math.sqrt(HEAD_DIM)

    def body(q_ref, k_ref, v_ref, do_ref, qa_ref, ka_ref, da_ref,
             dq_ref, dk_ref, dv_ref, drows_ref, dcols_ref, km_scr, vm_scr, dk_scr, dv_scr, dq_scr):
        pair, j = pl.program_id(0), pl.program_id(1)
        lane, halves = _head_halves(tq)

        @pl.when(jnp.logical_and(pair == 0, j == 0))
        def _():
            drows_ref[...] = jnp.zeros(drows_ref.shape, F32)
            dcols_ref[...] = jnp.zeros(dcols_ref.shape, F32)

        @pl.when(j == 0)
        def _():
            dq_scr[...] = jnp.zeros(dq_scr.shape, F32)

        v_ones = ((lane & (HEAD_DIM - 1)) < 3).astype(BF16)
        for a in range(2):
            km_scr[a] = jnp.where(halves[a], k_ref[...], ka_ref[...])
            vm_scr[a] = jnp.where(halves[a], v_ref[...], v_ones)
        dk_scr[...] = jnp.zeros(dk_scr.shape, F32)
        dv_scr[...] = jnp.zeros(dv_scr.shape, F32)

        def update(i, on_diagonal):
            rows = pl.ds(pl.multiple_of(i * tq, tq), tq)
            qs = q_ref[rows, :] * scale
            do2, qaug_t, doaug_t = do_ref[rows, :], qa_ref[rows, :], da_ref[rows, :]
            if on_diagonal:
                keep = (lax.broadcasted_iota(jnp.int32, (tq, tq), 0)
                        >= lax.broadcasted_iota(jnp.int32, (tq, tq), 1))
            qas = [jnp.where(halves[a], qs, qaug_t) for a in range(2)]
            logits = [lax.dot_general(qas[a], km_scr[a], NT, preferred_element_type=F32) for a in range(2)]
            dps = [lax.dot_general(jnp.where(halves[a], do2, doaug_t), vm_scr[a], NT, preferred_element_type=F32)
                   for a in range(2)]
            dv = None
            for a in range(2):
                s = jnp.where(keep, logits[a], NEG_INF) if on_diagonal else logits[a]
                p = jnp.exp(s)
                dsb = (p * dps[a]).astype(BF16)
                do0 = jnp.where(halves[a], do2, jnp.zeros_like(do2))
                dv_a = lax.dot_general(p.astype(BF16), do0, TN_DIMS, preferred_element_type=F32)
                dv = dv_a if dv is None else dv + dv_a
                dk_scr[a] += lax.dot_general(dsb, qas[a], TN_DIMS, preferred_element_type=F32)
                dq_scr[a, rows, :] += jnp.dot(dsb, km_scr[a], preferred_element_type=F32)
            dv_scr[...] += dv

        def below_diagonal(n, carry):
            update(j + 1 + 2 * n, False)
            update(j + 2 + 2 * n, False)
            return carry

        update(j, True)
        below = nq - 1 - j
        lax.fori_loop(0, below // 2, below_diagonal, 0)

        @pl.when(below % 2 == 1)
        def _():
            update(nq - 1, False)


        def to_head_lanes(old, first, second):
            at = lax.broadcasted_iota(jnp.int32, old.shape, 1) - 2 * pair
            return jnp.where(at == 0, first, jnp.where(at == 1, second, old))

        dk_ref[...] = jnp.where(halves[0], dk_scr[0], dk_scr[1]).astype(BF16)
        dv_ref[...] = dv_scr[...].astype(BF16)
        keys = pl.ds(pl.multiple_of(j * tq, tq), tq)
        ones_a, ones_b = HEAD_DIM + SLOT_ONE, SLOT_ONE
        dcols_ref[keys, :] = to_head_lanes(dcols_ref[keys, :], dk_scr[0][:, ones_a:ones_a + 1],
                                           dk_scr[1][:, ones_b:ones_b + 1])

        @pl.when(j == nq - 1)
        def _():
            row_lane, row_halves = _head_halves(S)
            dq_ref[...] = (jnp.where(row_halves[0], dq_scr[0], dq_scr[1]) * scale).astype(BF16)
            c_a, c_b = HEAD_DIM + SLOT_C, SLOT_C
            drows_ref[...] = to_head_lanes(drows_ref[...], dq_scr[0][:, c_a:c_a + 1], dq_scr[1][:, c_b:c_b + 1])

    tile = lambda col: pl.BlockSpec((tq, LANES), lambda p, j: (j, col + p))
    whole = lambda col: pl.BlockSpec((S, LANES), lambda p, j: (0, col + p))
    shared = pl.BlockSpec((S, LANES), lambda p, j: (0, 0))
    return pl.pallas_call(
        body, name="attn_bwd", grid=(P, nq),
        in_specs=[whole(qc), tile(kc), tile(vc), whole(P), whole(0), tile(0), whole(0)],
        out_specs=[whole(0), tile(0), tile(0), shared, shared],
        out_shape=[jax.ShapeDtypeStruct((S, W), BF16),
                   jax.ShapeDtypeStruct((S, W), BF16),
                   jax.ShapeDtypeStruct((S, W), BF16),
                   jax.ShapeDtypeStruct((S, LANES), F32),
                   jax.ShapeDtypeStruct((S, LANES), F32)],
        scratch_shapes=[pltpu.VMEM((2, tq, LANES), BF16),
                        pltpu.VMEM((2, tq, LANES), BF16),
                        pltpu.VMEM((2, tq, LANES), F32),
                        pltpu.VMEM((tq, LANES), F32),
                        pltpu.VMEM((2, S, LANES), F32)],
        compiler_params=_params("arbitrary", "arbitrary"),
    )(proj, proj, proj, da, qaug, kaug, doaug)


def _fgate_bwd(drows, dcols, z, bias):
    S = z.shape[0]
    tb = min(TB, S)
    nb = S // tb

    def body(drows_ref, dcols_ref, z_ref, b_ref, dz_ref, db_ref):
        tri = (lax.broadcasted_iota(jnp.int32, (tb, tb), 1)
               >= lax.broadcasted_iota(jnp.int32, (tb, tb), 0)).astype(F32)

        local = []
        for b in range(nb):
            rows = slice(b * tb, (b + 1) * tb)
            local.append(jnp.dot(tri, drows_ref[rows, :] - dcols_ref[rows, :], preferred_element_type=F32,
                                 precision=lax.Precision.HIGHEST))
        carry = jnp.zeros((1, LANES), F32)
        db = jnp.zeros((1, LANES), F32)
        for b in reversed(range(nb)):
            rows = slice(b * tb, (b + 1) * tb)
            rc = local[b] + carry
            carry = rc[0:1, :]
            dz = rc * _sigmoid(-(z_ref[rows, :] + b_ref[...]))
            dz_ref[rows, :] = dz.astype(BF16)
            db = db + jnp.sum(dz, axis=0, keepdims=True)
        db_ref[...] = db

    return pl.pallas_call(
        body, name="fgate_bwd",
        out_shape=[jax.ShapeDtypeStruct((S, LANES), BF16),
                   jax.ShapeDtypeStruct((1, LANES), F32)],
        compiler_params=pltpu.CompilerParams(vmem_limit_bytes=VMEM_LIMIT),
    )(drows, dcols, z, bias)


def _pool_bwd(proj, da, pool_w, pool_scale):
    S = proj.shape[0]
    G = len(POOL_WINDOWS)

    def body(u_ref, dy_ref, w_ref, s_ref, du_ref, dw_ref, ds_ref, pad_ref):
        g = pl.program_id(0)
        for gi, w in enumerate(POOL_WINDOWS):
            @pl.when(g == gi)
            def _():
                d, cnt = _window_mean_minus_self(u_ref[...].astype(F32), pad_ref, w, S)
                db = d.astype(BF16)
                wb = w_ref[0].astype(BF16)
                yraw = jnp.dot(db, wb, preferred_element_type=F32)
                dy = dy_ref[...].astype(F32)
                ds_ref[...] = jnp.sum(dy * yraw, axis=0, keepdims=True)
                dzb = (dy * s_ref[...]).astype(BF16)
                dw_ref[0] = lax.dot_general(db, dzb, TN_DIMS, preferred_element_type=F32)
                dd = lax.dot_general(dzb, wb, NT, preferred_element_type=F32)
                pad_ref[0:S, :] = dd / cnt
                pad_ref[S:S + MAX_WINDOW, :] = jnp.zeros((MAX_WINDOW, LANES), F32)
                acc = -dd
                for j in range(w):
                    acc = acc + pad_ref[j:j + S, :]
                du_ref[...] = acc.astype(BF16)

    return pl.pallas_call(
        body, name="pool_bwd", grid=(G,),
        in_specs=[pl.BlockSpec((S, LANES), lambda g: (0, g)),
                  pl.BlockSpec((S, LANES), lambda g: (0, g)),
                  pl.BlockSpec((1, LANES, LANES), lambda g: (g, 0, 0)),
                  pl.BlockSpec((1, LANES), lambda g: (0, g))],
        out_specs=[pl.BlockSpec((S, LANES), lambda g: (0, g)),
                   pl.BlockSpec((1, LANES, LANES), lambda g: (g, 0, 0)),
                   pl.BlockSpec((1, LANES), lambda g: (0, g))],
        out_shape=[jax.ShapeDtypeStruct((S, G * LANES), BF16),
                   jax.ShapeDtypeStruct((G, LANES, LANES), F32),
                   jax.ShapeDtypeStruct((1, G * LANES), F32)],
        scratch_shapes=[pltpu.VMEM((S + MAX_WINDOW, LANES), F32)],
        compiler_params=_params("arbitrary"),
    )(proj, da, pool_w, pool_scale)


def _inproj_bwd_dx(sections, dzf, w, x, gam, g, after=()):
    S, D = x.shape
    N = w.shape[1] - LANES
    W = D // 2
    n_sec = len(sections)
    tm = min(TM, S)

    def body(dz_ref, w_ref, wf_ref, x_ref, gam_ref, g_ref, *rest):
        sec_refs = rest[:n_sec]
        dx_ref, dg_ref = rest[-2:]

        @pl.when(pl.program_id(0) == 0)
        def _():
            dg_ref[...] = jnp.zeros(dg_ref.shape, F32)

        dh = lax.dot_general(dz_ref[...], wf_ref[...], NT, preferred_element_type=F32)
        for n, ref in enumerate(sec_refs):
            dh = dh + lax.dot_general(ref[...], w_ref[:, n * W:(n + 1) * W], NT, preferred_element_type=F32)
        xf = x_ref[...]
        r = lax.rsqrt(jnp.mean(xf * xf, axis=-1, keepdims=True) + RMS_EPS)
        xhat = xf * r
        dg_ref[...] += jnp.sum(dh * xhat, axis=0, keepdims=True)
        dxhat = dh * gam_ref[...]
        dx_ref[...] = g_ref[...] + r * (dxhat - xhat * jnp.mean(dxhat * xhat, axis=-1, keepdims=True))

    sec_specs, sec_arrays = _section_specs(sections, tm, W)
    return pl.pallas_call(
        body, name="inproj_bwd_dx", grid=(S // tm,),
        in_specs=[pl.BlockSpec((tm, LANES), lambda i: (i, 0)),
                  pl.BlockSpec((D, N), lambda i: (0, 0)),
                  pl.BlockSpec((D, LANES), lambda i: (0, N // LANES)),
                  pl.BlockSpec((tm, D), lambda i: (i, 0)),
                  pl.BlockSpec((1, D), lambda i: (0, 0)),
                  pl.BlockSpec((tm, D), lambda i: (i, 0))] + sec_specs + _after_specs(after),
        out_specs=[pl.BlockSpec((tm, D), lambda i: (i, 0)),
                   pl.BlockSpec((1, D), lambda i: (0, 0))],
        out_shape=[jax.ShapeDtypeStruct((S, D), F32),
                   jax.ShapeDtypeStruct((1, D), F32)],
        compiler_params=_params("arbitrary"),
    )(dzf, w, w, x, gam, g, *sec_arrays, *after)


def _adamw(w, m, v, gsets, name, rows, shifted=False, first=0, into=None):
    A, R, C = w.shape
    n_sets = len(gsets)
    tr = min(rows, R)
    c1 = 1.0 / (1.0 - ADAM_B1 ** ADAM_STEP)
    c2 = 1.0 / (1.0 - ADAM_B2 ** ADAM_STEP)
    counts = [len(gs) for gs in gsets]

    def body(w_ref, m_ref, v_ref, *rest):
        g_ref, d_ref, nm_ref, nv_ref = rest[-4:]
        at = 0
        for a in range(n_sets):
            part_refs = rest[at:at + counts[a]]
            at += counts[a]

            @pl.when(pl.program_id(0) == a)
            def _():
                g = None
                for ref in part_refs:
                    for s in range(ref.shape[0]):
                        term = ref[s].astype(F32)
                        g = term if g is None else g + term
                if shifted:
                    lanes = g.shape[1]
                    g = pltpu.roll(g, (lanes - _index(_position()) * (C % LANES)) % lanes, axis=1)[:, :C]
                nm = ADAM_B1 * m_ref[0] + (1.0 - ADAM_B1) * g
                nv = ADAM_B2 * v_ref[0] + (1.0 - ADAM_B2) * (g * g)
                g_ref[0] = g
                nm_ref[0] = nm
                nv_ref[0] = nv
                d_ref[0] = -ADAM_LR * ((nm * c1) / (jnp.sqrt(nv * c2) + ADAM_EPS) + ADAM_WD * w_ref[0])

    spec = pl.BlockSpec((1, tr, C), lambda a, r: (first + a, r, 0))
    part_specs = [pl.BlockSpec((part.shape[0], tr, part.shape[2]), lambda a, r, l=l: (0, jnp.where(a == l, r, 0), 0))
                  for l, gs in enumerate(gsets) for part in gs]
    parts = [part for gs in gsets for part in gs]
    shape = jax.ShapeDtypeStruct((A, R, C), F32)
    earlier = () if into is None else tuple(into)
    return pl.pallas_call(
        body, name=name, grid=(n_sets, R // tr),
        in_specs=[spec, spec, spec] + part_specs + _after_specs(earlier),
        out_specs=[spec, spec, spec, spec],
        out_shape=[shape, shape, shape, shape],
        input_output_aliases={3 + len(parts) + n: n for n in range(len(earlier))},
        compiler_params=_params("arbitrary", "arbitrary"),
    )(w, m, v, *parts, *earlier)


def _position():
    return lax.axis_index("x"), lax.axis_index("y"), lax.axis_index("c")


def _index(dev):
    return 4 * dev[0] + 2 * dev[1] + dev[2]


def _all_gather(arrs, slots, out_shapes, name):
    n_arr = len(arrs)

    def body(*refs):
        ins, outs = refs[:n_arr], refs[n_arr:2 * n_arr]
        send_sems, recv_sems, local_sems = refs[2 * n_arr:]
        x, y, c = _position()
        me, sibling = (x, y, c), (x, y, 1 - c)
        chips = [(1 - x, y), (x, 1 - y), (1 - x, 1 - y)]

        def copy(a, k, block, to, src=None):
            part = slots[a](outs[a], _index(block))
            return pltpu.make_async_remote_copy(
                src_ref=part if src is None else src, dst_ref=part,
                send_sem=send_sems.at[a, k], recv_sem=recv_sems.at[a, k],
                device_id=to, device_id_type=MESH)

        mine = [pltpu.make_async_copy(ins[a], slots[a](outs[a], _index(me)), local_sems.at[a])
                for a in range(n_arr)]
        for cp in mine:
            cp.start()
        first = []
        for a in range(n_arr):
            first.append(copy(a, 0, me, sibling, src=ins[a]))
            first += [copy(a, 1 + j, me, (*chip, c), src=ins[a]) for j, chip in enumerate(chips)]
        for cp in first:
            cp.start()
        passed = []
        for j, chip in enumerate(chips):
            for a in range(n_arr):
                copy(a, 1 + j, (*chip, c), me).wait_recv()
                fwd = copy(a, 4 + j, (*chip, c), sibling)
                fwd.start()
                passed.append(fwd)
        for a in range(n_arr):
            copy(a, 0, sibling, me).wait_recv()
            for j, chip in enumerate(chips):
                copy(a, 4 + j, (*chip, 1 - c), me).wait_recv()
        for cp in first + passed:
            cp.wait_send()
        for cp in mine:
            cp.wait()

    any_spec = pl.BlockSpec(memory_space=pl.ANY)
    return pl.pallas_call(
        body, name=name,
        in_specs=[any_spec] * n_arr, out_specs=[any_spec] * n_arr, out_shape=out_shapes,
        scratch_shapes=[pltpu.SemaphoreType.DMA((n_arr, 7)), pltpu.SemaphoreType.DMA((n_arr, 7)),
                        pltpu.SemaphoreType.DMA((n_arr,))],
    )(*arrs)


def _split_copies(srcs, lands, send_sems, recv_sems, kinds):
    x, y, c = _position()
    me = _index((x, y, c))
    copies = []
    for a, (src_part, land_part) in enumerate(kinds):
        for k in range(1, N_DEV):
            peer = (x ^ ((k >> 2) & 1), y ^ ((k >> 1) & 1), c ^ (k & 1))
            copies.append(pltpu.make_async_remote_copy(
                src_ref=src_part(srcs[a], _index(peer)), dst_ref=land_part(lands[a], me, k),
                send_sem=send_sems[a].at[k - 1], recv_sem=recv_sems[a].at[k - 1],
                device_id=peer, device_id_type=MESH))
    return copies


def _split_start(srcs, lands, kinds, name, after=()):
    n = len(srcs)

    def body(*refs):
        src_refs, land_refs = refs[:n], refs[n:2 * n]
        outs = refs[2 * n + len(after):]
        send_sems, recv_sems = outs[:n], outs[n:2 * n]
        token = outs[-1]
        for cp in _split_copies(src_refs, land_refs, send_sems, recv_sems, kinds):
            cp.start()
        token[...] = jnp.zeros(token.shape, token.dtype)

    hbm = pl.BlockSpec(memory_space=pltpu.HBM)
    sem = pl.BlockSpec(memory_space=pltpu.SEMAPHORE)
    operands = [pltpu.with_memory_space_constraint(t, pltpu.HBM) for t in (*srcs, *lands)]
    out = pl.pallas_call(
        body, name=name,
        in_specs=[hbm] * (2 * n) + _after_specs(after),
        out_specs=[sem] * (2 * n) + [hbm] * (2 * n) + [pl.BlockSpec(memory_space=pltpu.VMEM)],
        out_shape=[pltpu.SemaphoreType.DMA((N_DEV - 1,))] * (2 * n)
        + [pltpu.HBM(t.shape, t.dtype) for t in operands] + [jax.ShapeDtypeStruct((8, LANES), F32)],
        input_output_aliases={i: 2 * n + i for i in range(2 * n)},
        compiler_params=pltpu.CompilerParams(has_side_effects=pltpu.SideEffectType.DATAFLOW_SIDE_EFFECTING),
    )(*operands, *after)
    return [(out[a], out[n + a], out[2 * n + a], out[3 * n + a]) for a in range(n)], out[-1]


def _split_wait(started, kinds, after, name):
    n = len(started)
    sems = [t[0] for t in started] + [t[1] for t in started]
    srcs = [t[2] for t in started]
    lands = [t[3] for t in started]

    def body(*refs):
        src_refs, land_refs = refs[:n], refs[n:2 * n]
        send_sems, recv_sems = refs[2 * n:3 * n], refs[3 * n:4 * n]
        for cp in _split_copies(src_refs, land_refs, send_sems, recv_sems, kinds):
            cp.wait_send()
            cp.wait_recv()

    hbm = pl.BlockSpec(memory_space=pltpu.HBM)
    sem = pl.BlockSpec(memory_space=pltpu.SEMAPHORE)
    out = pl.pallas_call(
        body, name=name,
        in_specs=[hbm] * (2 * n) + [sem] * (2 * n) + _after_specs(after),
        out_specs=[hbm] * (2 * n),
        out_shape=[pltpu.HBM(t.shape, t.dtype) for t in (*srcs, *lands)],
        input_output_aliases={i: i for i in range(2 * n)},
        compiler_params=pltpu.CompilerParams(has_side_effects=pltpu.SideEffectType.DATAFLOW_SIDE_EFFECTING),
    )(*srcs, *lands, *sems, *after)
    return out[n:]


def _as_rows(p):
    if p.size % LANES == 0:
        rows = p.reshape(-1, LANES)
    else:
        rows = p.reshape(-1, p.shape[-1])
        rows = jnp.pad(rows, ((0, 0), (0, LANES - rows.shape[1])))
    return jnp.pad(rows, ((0, -rows.shape[0] % 8), (0, 0)))


def _pack(parts):
    return jnp.concatenate([_as_rows(p) for p in parts])[None]


def _unpack(packed, like):
    out, at = [], 0
    for p in like:
        whole = p.size % LANES == 0
        n = p.size // LANES if whole else p.size // p.shape[-1]
        rows = packed[0, at:at + n]
        out.append((rows if whole else rows[:, :p.shape[-1]]).reshape(p.shape))
        at += n + (-n % 8)
    return out


def _local_step(x, target, norm_g, forget_bias, pool_w, pool_scale, final_g, weights_in, weights_out, on_grads,
                first_after=()):
    L = norm_g.shape[0]
    S, D = x.shape
    W = D // 2
    H = W // HEAD_DIM
    bias = jnp.pad(forget_bias, ((0, 0), (0, LANES - H)))

    saved = []
    after = tuple(first_after)
    for l in range(L):
        proj, h, z, w = _inproj_fwd(x, norm_g[l:l + 1], weights_in(l, x), after)
        after = ()
        qaug, kaug = _fgate_fwd(z, bias[l:l + 1], H)
        ypool = _pool_fwd(proj, pool_w[l], pool_scale[l:l + 1])
        o, qaug_b = _attn_fwd(proj, qaug, kaug)
        wout = weights_out(l, o)
        x_in = x
        if l < L - 1:
            x, mixed = _outproj_fwd(ypool, o, proj, x_in, wout)
        else:
            g, mixed, loss, d_final_g = _outproj_fwd(ypool, o, proj, x_in, wout, (final_g.reshape(1, D), target))
        saved.append((x_in, proj, h, z, qaug_b, kaug, ypool, o, mixed, w, wout))

    small = None
    for l in reversed(range(L)):
        x_in, proj, h, z, qaug_b, kaug, ypool, o, mixed, w, wout = saved[l]
        d_wout, d_wout_bf16, da, dgate, doaug = _outproj_bwd(g, wout, mixed, ypool, o, proj)
        dq, dk, dv, drows, dcols = _attn_bwd(proj, da, qaug_b, kaug, doaug)
        dzf, db = _fgate_bwd(drows, dcols, z, bias[l:l + 1])
        dpu, dpw, dps = _pool_bwd(proj, da, pool_w[l], pool_scale[l:l + 1])
        dproj = [(dpu, 0), (dgate, 0), (dq, 0), (dk, 0), (dv, 0), (dgate, 1)]
        d_w, d_w_bf16 = _inproj_bwd_dw(h, dproj, dzf)
        after = tuple(on_grads(l, d_w, d_w_bf16, d_wout, d_wout_bf16, small))
        g, dgam = _inproj_bwd_dx(dproj, dzf, w, x_in, norm_g[l:l + 1], g, after)
        small = (dgam[0], db[0, :H], dpw, dps[0])
    return loss[0, 0], g, small, d_final_g[0]


def kernel(x, norm_g, w_in, forget_bias, pool_w, pool_scale, w_out, final_g, loss_target, m_norm_g, m_w_in, m_forget_bias, m_pool_w, m_pool_scale, m_w_out, m_final_g, v_norm_g, v_w_in, v_forget_bias, v_pool_w, v_pool_scale, v_w_out, v_final_g):
    L, D, cols = w_in.shape
    rows_out = w_out.shape[1]
    me = _index(_position())
    slot = _slot_width(cols)
    wout_b = w_out.astype(BF16)
    win_b = _shift_slots(w_in)
    gather_in = (lambda ref, peer: ref, lambda ref, mine, k: ref.at[mine])
    gather_out = (lambda ref, peer: ref, lambda ref, mine, k: ref.at[pl.ds(mine * rows_out, rows_out), :])

    def landing(block, n_slots):
        zone = lax.empty((n_slots * block.shape[0], *block.shape[1:]), block.dtype)
        return lax.dynamic_update_slice(zone, block, (me * block.shape[0],) + (0,) * (block.ndim - 1))

    (first_in,) = _all_gather([win_b[0]], [lambda ref, n: ref.at[n]],
                              [jax.ShapeDtypeStruct((N_DEV, D, slot), BF16)], "gather_first")
    rest_srcs = [wout_b[0]] + [w[l] for l in range(1, L) for w in (win_b, wout_b)]
    rest_lands = [landing(wout_b[0], N_DEV)]
    for l in range(1, L):
        rest_lands += [landing(win_b[l][None], N_DEV), landing(wout_b[l], N_DEV)]
    rest_kinds = [gather_out] + [gather_in, gather_out] * (L - 1)
    rest, rest_token = _split_start(rest_srcs, rest_lands, rest_kinds, "gather_start_rest", (first_in,))

    def weights_in(l, x_in):
        if l == 0:
            return first_in
        (win_all,) = _split_wait([rest[2 * l - 1]], [gather_in], (x_in,), f"gather_wait_in_{l}")
        return win_all

    def weights_out(l, o):
        (wout_full,) = _split_wait([rest[2 * l]], [gather_out], (o,), f"gather_wait_out_{l}")
        return wout_full

    stride = slot - LANES
    exchange_kinds = [(lambda ref, peer: ref.at[:, pl.ds(pl.multiple_of(peer * stride, LANES), slot)],
                       lambda ref, mine, k: ref.at[k - 1]),
                      (lambda ref, peer: ref.at[pl.ds(peer * rows_out, rows_out), :],
                       lambda ref, mine, k: ref.at[k - 1])]
    zero_g = jnp.zeros_like(final_g)
    zero_loss = jnp.zeros((LANES,), F32)

    def small_pack(l, norm_g_l, bias_l, pool_w_l, pool_scale_l, final, loss_row=None):
        return _pack([norm_g_l, bias_l, pool_w_l, pool_scale_l, final if l == 0 else zero_g,
                      zero_loss if loss_row is None else loss_row])[0]

    exchanges, own_parts = {}, {}

    def on_grads(l, dw, dw_bf16, d_wout, d_wout_bf16, small):
        own_parts[l] = (lax.dynamic_slice_in_dim(dw, me * stride, slot, 1)[None],
                        lax.dynamic_slice_in_dim(d_wout, me * rows_out, rows_out, 0)[None])
        srcs = [dw_bf16, d_wout_bf16]
        lands = [lax.empty((N_DEV - 1, D, slot), BF16), lax.empty((N_DEV - 1, rows_out, D), BF16)]
        kinds = list(exchange_kinds)
        if small is not None:
            packed_small = small_pack(l + 1, *small, None)
            srcs.append(packed_small)
            lands.append(landing(packed_small[None], N_DEV))
            kinds.append(gather_in)
        started, token = _split_start(srcs, lands, kinds, f"exchange_start_{l}")
        exchanges[l] = (started, kinds)
        return (token,)

    loss, dx, small_first, d_final_g = _local_step(
        x[0], loss_target[0], norm_g, forget_bias, pool_w, pool_scale, final_g,
        weights_in, weights_out, on_grads, (rest_token,))
    packed_first = small_pack(0, *small_first, d_final_g, jnp.full((LANES,), loss, F32))
    first_started, first_token = _split_start(
        [packed_first], [landing(packed_first[None], N_DEV)], [gather_in],
        "small_start", (w_in, m_w_in, v_w_in, *own_parts[0]))

    gin_sets, gout_sets, small_sets = [None] * L, [None] * L, [None] * L

    def wait_for(l, after):
        started, kinds = exchanges[l]
        got = _split_wait(started, kinds, after, f"exchange_wait_{l}")
        gin_sets[l] = [own_parts[l][0], got[0]]
        gout_sets[l] = [own_parts[l][1], got[1]]
        if len(got) > 2:
            small_sets[l + 1] = [got[2]]

    for l in range(1, L):
        wait_for(l, (dx, first_token))
    rest_in = _adamw(w_in, m_w_in, v_w_in, gin_sets[1:], "adamw_w_in_rest", TM // 2, shifted=True, first=1)
    rest_out = _adamw(w_out, m_w_out, v_w_out, gout_sets[1:], "adamw_w_out_rest", rows_out, first=1)
    wait_for(0, (rest_in[1], rest_out[1]))
    g_w_in, d_w_in, nm_w_in, nv_w_in = _adamw(w_in, m_w_in, v_w_in, gin_sets[:1], "adamw_w_in_first", TM // 2,
                                              shifted=True, into=rest_in)
    g_w_out, d_w_out, nm_w_out, nv_w_out = _adamw(w_out, m_w_out, v_w_out, gout_sets[:1], "adamw_w_out_first",
                                                  rows_out, into=rest_out)
    small_sets[0] = _split_wait(first_started, [gather_in], (d_w_in, d_w_out), "small_wait")
    loss = jnp.sum(small_sets[0][0][:, packed_first.shape[0] - 8, 0])

    def small_stack(norm_g_, bias_, pool_w_, pool_scale_, final):
        return jnp.stack([small_pack(l, norm_g_[l], bias_[l], pool_w_[l], pool_scale_[l], final) for l in range(L)])

    packed = _adamw(small_stack(norm_g, forget_bias, pool_w, pool_scale, final_g),
                    small_stack(m_norm_g, m_forget_bias, m_pool_w, m_pool_scale, m_final_g),
                    small_stack(v_norm_g, v_forget_bias, v_pool_w, v_pool_scale, v_final_g),
                    small_sets, "adamw_small", packed_first.shape[0])

    def small_unpack(p):
        like = [norm_g[0], forget_bias[0], pool_w[0], pool_scale[0], final_g]
        layers = [_unpack(p[l:l + 1], like) for l in range(L)]
        return [jnp.stack([layers[l][n] for l in range(L)]) for n in range(4)] + [layers[0][4]]

    g_s, d_s, nm_s, nv_s = [small_unpack(p) for p in packed]

    def order(big_in, big_out, small):
        return (small[0], big_in, small[1], small[2], small[3], big_out, small[4])

    return (loss, dx[None], *order(g_w_in, g_w_out, g_s), *order(d_w_in, d_w_out, d_s),
            *order(nm_w_in, nm_w_out, nm_s), *order(nv_w_in, nv_w_out, nv_s))
```

```python
import math

import jax
import jax.numpy as jnp
from jax import lax
from jax.experimental import pallas as pl
from jax.experimental.pallas import tpu as pltpu

F32 = jnp.float32
BF16 = jnp.bfloat16
MESH = pl.DeviceIdType.MESH

RMS_EPS = 1e-6
NEG_INF = -1e30
HEAD_DIM = 64
POOL_WINDOWS = (2, 4, 8, 16)
MAX_WINDOW = 16
LANES = 128
N_DEV = 8

ADAM_LR = 0.001
ADAM_B1 = 0.9
ADAM_B2 = 0.999
ADAM_EPS = 1e-08
ADAM_WD = 0.01
ADAM_STEP = 10

TM = 512
TN = 512
TQ = 512
TB = 256
VMEM_LIMIT = 56 * 1024 * 1024

NT = (((1,), (1,)), ((), ()))
TN_DIMS = (((0,), (0,)), ((), ()))

SLOT_C, SLOT_ONE, SLOT_LSE = 0, 3, 6


def _params(*sem):
    return pltpu.CompilerParams(dimension_semantics=sem, vmem_limit_bytes=VMEM_LIMIT)


def _sigmoid(x):
    return 1.0 / (1.0 + jnp.exp(-x))


def _split3(x):
    hi = x.astype(BF16).astype(F32)
    rest = x - hi
    mid = rest.astype(BF16).astype(F32)
    return hi, mid, rest - mid


def _after_specs(after):
    return [pl.BlockSpec(memory_space=pl.ANY)] * len(after)


def _slot_width(cols):
    return LANES * (-(-(cols + (N_DEV - 1) * (cols % LANES)) // LANES))


def _shift_slots(w_in):
    L, D, cols = w_in.shape
    slot = _slot_width(cols)
    tr = min(TM // 2, D)

    def body(w_ref, o_ref, pad_scr):
        pad_scr[...] = jnp.zeros(pad_scr.shape, F32)
        pad_scr[:, 0:cols] = w_ref[0]
        o_ref[0] = pltpu.roll(pad_scr[...], _index(_position()) * (cols % LANES), axis=1).astype(BF16)

    return pl.pallas_call(
        body, name="shift_slots", grid=(L, D // tr),
        in_specs=[pl.BlockSpec((1, tr, cols), lambda l, r: (l, r, 0))],
        out_specs=pl.BlockSpec((1, tr, slot), lambda l, r: (l, r, 0)),
        out_shape=jax.ShapeDtypeStruct((L, D, slot), BF16),
        scratch_shapes=[pltpu.VMEM((tr, slot), F32)],
        compiler_params=_params("parallel", "parallel"),
    )(w_in)


def _inproj_fwd(x, gam, slots, after=()):
    S, D = x.shape
    n_dev, _, sw = slots.shape
    stride = sw - LANES
    width = stride * n_dev + LANES
    N = width - LANES
    tm, tn = min(TM, S), TN

    def body(x_ref, g_ref, s_ref, *rest):
        proj_ref, h_ref, z_ref, w_ref = rest[-4:]

        @pl.when(pl.program_id(0) == 0)
        def _():
            for n in range(n_dev):
                base = stride * n
                first = s_ref[n, :, 0:LANES]
                if n > 0:
                    first = first + s_ref[n - 1, :, stride:sw]
                w_ref[:, base:base + LANES] = first
                w_ref[:, base + LANES:base + stride] = s_ref[n, :, LANES:stride]
            w_ref[:, stride * n_dev:width] = s_ref[n_dev - 1, :, stride:sw]

        xf = x_ref[...]
        r = lax.rsqrt(jnp.mean(xf * xf, axis=-1, keepdims=True) + RMS_EPS)
        h = ((xf * r) * g_ref[...]).astype(BF16)
        h_ref[...] = h
        z_ref[...] = jnp.dot(h, w_ref[:, N:width], preferred_element_type=F32)
        for n in range(N // tn):
            cols = slice(n * tn, (n + 1) * tn)
            proj_ref[:, cols] = jnp.dot(h, w_ref[:, cols], preferred_element_type=F32).astype(BF16)

    return pl.pallas_call(
        body, name="inproj_fwd", grid=(S // tm,),
        in_specs=[pl.BlockSpec((tm, D), lambda i: (i, 0)),
                  pl.BlockSpec((1, D), lambda i: (0, 0)),
                  pl.BlockSpec((n_dev, D, sw), lambda i: (0, 0, 0))] + _after_specs(after),
        out_specs=[pl.BlockSpec((tm, N), lambda i: (i, 0)),
                   pl.BlockSpec((tm, D), lambda i: (i, 0)),
                   pl.BlockSpec((tm, LANES), lambda i: (i, 0)),
                   pl.BlockSpec((D, width), lambda i: (0, 0))],
        out_shape=[jax.ShapeDtypeStruct((S, N), BF16),
                   jax.ShapeDtypeStruct((S, D), BF16),
                   jax.ShapeDtypeStruct((S, LANES), F32),
                   jax.ShapeDtypeStruct((D, width), BF16)],
        compiler_params=_params("arbitrary"),
    )(x, gam, slots, *after)


def _fgate_fwd(z, bias, n_heads):
    S = z.shape[0]
    tb = min(TB, S)
    P = n_heads // 2
    assert n_heads <= 8, "the three parts of c are packed eight lanes apart"

    def body(z_ref, b_ref, qaug_ref, kaug_ref):
        lane = lax.broadcasted_iota(jnp.int32, (tb, LANES), 1)
        tri = (lax.broadcasted_iota(jnp.int32, (tb, tb), 0)
               >= lax.broadcasted_iota(jnp.int32, (tb, tb), 1)).astype(F32)
        row = lax.broadcasted_iota(jnp.int32, (LANES, P * LANES), 0)
        col = lax.broadcasted_iota(jnp.int32, (LANES, P * LANES), 1)
        head, part_n = row & 7, row >> 3
        home = (head >> 1) * LANES + jnp.where((head & 1) == 0, HEAD_DIM, 0)
        is_part = jnp.logical_and(head < n_heads, part_n < 3)
        place_q = jnp.logical_and(is_part, col == home + SLOT_C + part_n).astype(BF16)
        place_k = jnp.logical_and(is_part, col == home + SLOT_ONE + part_n).astype(BF16)
        slot = lax.broadcasted_iota(jnp.int32, (tb, P * LANES), 1) & (HEAD_DIM - 1)
        q_ones = jnp.logical_and(slot >= SLOT_ONE, slot < SLOT_ONE + 3).astype(F32)
        k_ones = jnp.logical_or(slot < SLOT_C + 3,
                                jnp.logical_and(slot >= SLOT_LSE, slot < SLOT_LSE + 3)).astype(F32)

        local = []
        for b in range(S // tb):
            zz = z_ref[b * tb:(b + 1) * tb, :] + b_ref[...]
            lf = jnp.minimum(zz, 0.0) - jnp.log(1.0 + jnp.exp(-jnp.abs(zz)))
            lf = jnp.where(lane < n_heads, lf, 0.0)
            local.append(jnp.dot(tri, lf, preferred_element_type=F32, precision=lax.Precision.HIGHEST))
        carry = jnp.zeros((1, LANES), F32)
        for b, part_sum in enumerate(local):
            c = part_sum + carry
            carry = c[tb - 1:tb, :]
            hi, mid, lo = _split3(c)
            packed = (hi + pltpu.roll(mid, 8, axis=1) + pltpu.roll(lo, 16, axis=1)).astype(BF16)
            qaug_ref[b * tb:(b + 1) * tb, :] = (
                q_ones + jnp.dot(packed, place_q, preferred_element_type=F32)).astype(BF16)
            kaug_ref[b * tb:(b + 1) * tb, :] = (
                k_ones - jnp.dot(packed, place_k, preferred_element_type=F32)).astype(BF16)

    return pl.pallas_call(
        body, name="fgate_fwd",
        out_shape=[jax.ShapeDtypeStruct((S, P * LANES), BF16),
                   jax.ShapeDtypeStruct((S, P * LANES), BF16)],
        compiler_params=pltpu.CompilerParams(vmem_limit_bytes=VMEM_LIMIT),
    )(z, bias)


def _window_mean_minus_self(u, pad_ref, w, S):
    pad_ref[0:MAX_WINDOW, :] = jnp.zeros((MAX_WINDOW, LANES), F32)
    pad_ref[MAX_WINDOW:MAX_WINDOW + S, :] = u
    acc = u
    for j in range(1, w):
        acc = acc + pad_ref[MAX_WINDOW - j:MAX_WINDOW - j + S, :]
    t = lax.broadcasted_iota(jnp.int32, (S, LANES), 0)
    cnt = jnp.minimum(t + 1, w).astype(F32)
    return acc / cnt - u, cnt


def _pool_fwd(proj, pool_w, pool_scale):
    S = proj.shape[0]
    G = len(POOL_WINDOWS)

    def body(u_ref, w_ref, s_ref, y_ref, pad_ref):
        g = pl.program_id(0)
        for gi, w in enumerate(POOL_WINDOWS):
            @pl.when(g == gi)
            def _():
                d, _ = _window_mean_minus_self(u_ref[...].astype(F32), pad_ref, w, S)
                y = jnp.dot(d.astype(BF16), w_ref[0].astype(BF16), preferred_element_type=F32)
                y_ref[...] = (y * s_ref[...]).astype(BF16)

    return pl.pallas_call(
        body, name="pool_fwd", grid=(G,),
        in_specs=[pl.BlockSpec((S, LANES), lambda g: (0, g)),
                  pl.BlockSpec((1, LANES, LANES), lambda g: (g, 0, 0)),
                  pl.BlockSpec((1, LANES), lambda g: (0, g))],
        out_specs=pl.BlockSpec((S, LANES), lambda g: (0, g)),
        out_shape=jax.ShapeDtypeStruct((S, G * LANES), BF16),
        scratch_shapes=[pltpu.VMEM((S + MAX_WINDOW, LANES), F32)],
        compiler_params=_params("arbitrary"),
    )(proj, pool_w, pool_scale)


def _head_halves(rows):
    lane = lax.broadcasted_iota(jnp.int32, (rows, LANES), 1)
    return lane, (lane < HEAD_DIM, lane >= HEAD_DIM)


def _attn_fwd(proj, qaug, kaug):
    S = proj.shape[0]
    W = proj.shape[1] // 6
    P = W // LANES
    tk = min(TQ, S // 2)
    tq = 2 * tk
    nq = S // tq
    qc, kc, vc = 2 * P, 3 * P, 4 * P
    scale = 1.0 / math.sqrt(HEAD_DIM)

    def body(q_ref, k_ref, v_ref, qa_ref, ka_ref, o_ref, qb_ref, qm_scr, m_scr, acc_scr):
        i = pl.program_id(1)
        lane, halves = _head_halves(tq)
        key_halves = (halves[0][:tk], halves[1][:tk])
        v_ones = ((lane[:tk] & (HEAD_DIM - 1)) < 3).astype(BF16)
        qs = q_ref[...] * scale
        qm_scr[0] = jnp.where(halves[0], qs, qa_ref[...])
        qm_scr[1] = jnp.where(halves[1], qs, qa_ref[...])
        m_scr[...] = jnp.full(m_scr.shape, NEG_INF, F32)
        acc_scr[...] = jnp.zeros(acc_scr.shape, F32)
        top, bottom, both = slice(0, tk), slice(tk, tq), slice(0, tq)

        def update(rows, j, on_diagonal):
            keys = pl.ds(pl.multiple_of(j * tk, tk), tk)
            k2, v2, kaug_t = k_ref[keys, :], v_ref[keys, :], ka_ref[keys, :]
            if on_diagonal:
                keep = (lax.broadcasted_iota(jnp.int32, (tk, tk), 0)
                        >= lax.broadcasted_iota(jnp.int32, (tk, tk), 1))
            logits = [lax.dot_general(qm_scr[a, rows, :], jnp.where(key_halves[a], k2, kaug_t), NT,
                                      preferred_element_type=F32) for a in range(2)]
            for a in range(2):
                s = jnp.where(keep, logits[a], NEG_INF) if on_diagonal else logits[a]
                va = jnp.where(key_halves[a], v2, v_ones)
                m_prev = m_scr[a, rows, :]
                m_new = jnp.maximum(m_prev, jnp.max(s, axis=1, keepdims=True))
                p = jnp.exp(s - jnp.tile(m_new, (1, tk // LANES)))
                acc_scr[a, rows, :] = (jnp.exp(m_prev - m_new) * acc_scr[a, rows, :]
                                       + jnp.dot(p.astype(BF16), va, preferred_element_type=F32))
                m_scr[a, rows, :] = m_new

        def below_diagonal(jj, carry):
            update(both, 2 * jj, False)
            update(both, 2 * jj + 1, False)
            return carry

        lax.fori_loop(0, i, below_diagonal, 0)
        update(top, 2 * i, True)
        update(bottom, 2 * i, False)
        update(bottom, 2 * i + 1, True)
        acc_a, acc_b = acc_scr[0], acc_scr[1]
        l_a, l_b = acc_a[:, HEAD_DIM:HEAD_DIM + 1], acc_b[:, 0:1]
        o_ref[...] = jnp.where(halves[0], acc_a / l_a, acc_b / l_b).astype(BF16)
        lse = jnp.where(halves[0], m_scr[1] + jnp.log(l_b), m_scr[0] + jnp.log(l_a))
        slot = lane & (HEAD_DIM - 1)
        aug = qa_ref[...].astype(F32)
        for n, part in enumerate(_split3(lse)):
            aug = jnp.where(slot == SLOT_LSE + n, -part, aug)
        qb_ref[...] = aug.astype(BF16)

    tile = lambda col: pl.BlockSpec((tq, LANES), lambda p, i: (i, col + p))
    whole = lambda col: pl.BlockSpec((S, LANES), lambda p, i: (0, col + p))
    return pl.pallas_call(
        body, name="attn_fwd", grid=(P, nq),
        in_specs=[tile(qc), whole(kc), whole(vc), tile(0), whole(0)],
        out_specs=[tile(0), tile(0)],
        out_shape=[jax.ShapeDtypeStruct((S, W), BF16), jax.ShapeDtypeStruct((S, W), BF16)],
        scratch_shapes=[pltpu.VMEM((2, tq, LANES), BF16),
                        pltpu.VMEM((2, tq, LANES), F32),
                        pltpu.VMEM((2, tq, LANES), F32)],
        compiler_params=_params("parallel", "arbitrary"),
    )(proj, proj, proj, qaug, kaug)


def _outproj_fwd(ypool, o, proj, x, wout, head=None):
    S, D = x.shape
    W = D // 2
    tm, tn = min(TM, S), TN

    def body(y_ref, o_ref, pg_ref, ag_ref, x_ref, w_ref, *rest):
        mix_ref = rest[-1]
        out_ref = rest[-4] if head else rest[0]
        pg, ag = pg_ref[...].astype(F32), ag_ref[...].astype(F32)
        mix_ref[:, 0:W] = (y_ref[...].astype(F32) * (pg * _sigmoid(pg))).astype(BF16)
        mix_ref[:, W:D] = (o_ref[...].astype(F32) * (ag * _sigmoid(ag))).astype(BF16)
        for n in range(D // tn):
            cols = slice(n * tn, (n + 1) * tn)
            out_ref[:, cols] = x_ref[:, cols] + jnp.dot(mix_ref[...], w_ref[:, cols], preferred_element_type=F32)
        if not head:
            return
        gam_ref, t_ref = rest[:2]
        loss_ref, dg_ref = rest[-3:-1]

        @pl.when(pl.program_id(0) == 0)
        def _():
            loss_ref[...] = jnp.zeros(loss_ref.shape, F32)
            dg_ref[...] = jnp.zeros(dg_ref.shape, F32)

        xf, gam_v = out_ref[...], gam_ref[...]
        r = lax.rsqrt(jnp.mean(xf * xf, axis=-1, keepdims=True) + RMS_EPS)
        xhat = xf * r
        err = xhat * gam_v - t_ref[...]
        part = jnp.sum(jnp.sum(err * err, axis=-1, keepdims=True), axis=0, keepdims=True)
        loss_ref[...] += part * (0.5 / D)
        dy = err * (1.0 / D)
        dg_ref[...] += jnp.sum(dy * xhat, axis=0, keepdims=True)
        dxhat = dy * gam_v
        out_ref[...] = r * (dxhat - xhat * jnp.mean(dxhat * xhat, axis=-1, keepdims=True))

    rows = lambda width, col: pl.BlockSpec((tm, width), lambda i: (i, col))
    in_specs = [rows(W, 0), rows(W, 0), rows(W, 1), rows(W, 5), rows(D, 0), pl.BlockSpec((D, D), lambda i: (0, 0))]
    out_specs = [rows(D, 0)]
    out_shape = [jax.ShapeDtypeStruct((S, D), F32)]
    if head:
        in_specs += [pl.BlockSpec((1, D), lambda i: (0, 0)), rows(D, 0)]
        out_specs += [pl.BlockSpec((8, LANES), lambda i: (0, 0)), pl.BlockSpec((1, D), lambda i: (0, 0))]
        out_shape += [jax.ShapeDtypeStruct((8, LANES), F32), jax.ShapeDtypeStruct((1, D), F32)]
    return pl.pallas_call(
        body, name="outproj_fwd_loss" if head else "outproj_fwd", grid=(S // tm,),
        in_specs=in_specs, out_specs=out_specs, out_shape=out_shape,
        scratch_shapes=[pltpu.VMEM((tm, D), BF16)],
        compiler_params=_params("arbitrary" if head else "parallel"),
    )(ypool, o, proj, proj, x, wout, *(head or ()))


def _outproj_bwd(g, wout, ypool, o, proj):
    S, D = g.shape
    W = D // 2
    tm = min(TM, S)

    def body(g_ref, w_ref, y_ref, o_ref, pg_ref, ag_ref, dw_ref, dwb_ref, da_ref, dgate_ref, doaug_ref):
        @pl.when(pl.program_id(0) == 0)
        def _():
            dw_ref[...] = jnp.zeros(dw_ref.shape, F32)

        gb = g_ref[...].astype(BF16)
        dmixes = [lax.dot_general(gb, w_ref[half * W:(half + 1) * W, :], NT, preferred_element_type=F32)
                  for half in range(2)]
        d_o = None
        for half, (val_ref, gate_ref) in enumerate(((y_ref, pg_ref), (o_ref, ag_ref))):
            cols = slice(half * W, (half + 1) * W)
            gt = gate_ref[...].astype(F32)
            sg = _sigmoid(gt)
            silu = gt * sg
            val = val_ref[...].astype(F32)
            dw_ref[cols, :] += lax.dot_general((val * silu).astype(BF16), gb, TN_DIMS, preferred_element_type=F32)
            d_o = (dmixes[half] * silu).astype(BF16)
            da_ref[:, cols] = d_o
            dgate_ref[:, cols] = (dmixes[half] * val * (sg * (1.0 + gt * (1.0 - sg)))).astype(BF16)

        lane, halves = _head_halves(tm)
        slot = lane & (HEAD_DIM - 1)
        for p in range(W // LANES):
            cols = slice(p * LANES, (p + 1) * LANES)
            prod = d_o[:, cols].astype(F32) * o_ref[:, cols].astype(F32)
            d_a = jnp.sum(jnp.where(halves[0], prod, 0.0), axis=1, keepdims=True)
            d_b = jnp.sum(jnp.where(halves[1], prod, 0.0), axis=1, keepdims=True)
            aug = jnp.zeros((tm, LANES), F32)
            for n, part in enumerate(_split3(jnp.where(halves[0], d_b, d_a))):
                aug = jnp.where(slot == SLOT_C + n, -part, aug)
            doaug_ref[:, cols] = aug.astype(BF16)

        @pl.when(pl.program_id(0) == S // tm - 1)
        def _():
            dwb_ref[...] = dw_ref[...].astype(BF16)

    rows = lambda width, col: pl.BlockSpec((tm, width), lambda i: (i, col))
    whole = pl.BlockSpec((D, D), lambda i: (0, 0))
    return pl.pallas_call(
        body, name="outproj_bwd", grid=(S // tm,),
        in_specs=[rows(D, 0), whole, rows(W, 0), rows(W, 0), rows(W, 1), rows(W, 5)],
        out_specs=[whole, whole, rows(D, 0), rows(D, 0), rows(W, 0)],
        out_shape=[jax.ShapeDtypeStruct((D, D), F32),
                   jax.ShapeDtypeStruct((D, D), BF16),
                   jax.ShapeDtypeStruct((S, D), BF16),
                   jax.ShapeDtypeStruct((S, D), BF16),
                   jax.ShapeDtypeStruct((S, W), BF16)],
        compiler_params=_params("arbitrary"),
    )(g, wout, ypool, o, proj, proj)


def _section_specs(sections, rows, width):
    specs = [pl.BlockSpec((rows, width), lambda k, c=c: (k, c)) for _, c in sections]
    return specs, [a for a, _ in sections]


def _inproj_bwd_dw(h, sections, dzf):
    S, D = h.shape
    W = D // 2
    n_sec = len(sections)
    N = n_sec * W
    ts = min(2 * TM, S)
    n_steps = S // ts

    def body(h_ref, dz_ref, *rest):
        sec_refs, (dw_ref, dwb_ref) = rest[:n_sec], rest[n_sec:]

        @pl.when(pl.program_id(0) == 0)
        def _():
            dw_ref[...] = jnp.zeros(dw_ref.shape, F32)

        ht = h_ref[...].T
        dw_ref[:, N:N + LANES] += jnp.dot(ht, dz_ref[...], preferred_element_type=F32)
        for n, ref in enumerate(sec_refs):
            dw_ref[:, n * W:(n + 1) * W] += jnp.dot(ht, ref[...], preferred_element_type=F32)

        @pl.when(pl.program_id(0) == n_steps - 1)
        def _():
            dwb_ref[...] = dw_ref[...].astype(BF16)

    sec_specs, sec_arrays = _section_specs(sections, ts, W)
    whole = pl.BlockSpec((D, N + LANES), lambda k: (0, 0), pipeline_mode=pl.Buffered(1))
    return pl.pallas_call(
        body, name="inproj_bwd_dw", grid=(n_steps,),
        in_specs=[pl.BlockSpec((ts, D), lambda k: (k, 0)),
                  pl.BlockSpec((ts, LANES), lambda k: (k, 0))] + sec_specs,
        out_specs=[whole, whole],
        out_shape=[jax.ShapeDtypeStruct((D, N + LANES), F32),
                   jax.ShapeDtypeStruct((D, N + LANES), BF16)],
        compiler_params=_params("arbitrary"),
    )(h, dzf, *sec_arrays)


def _attn_bwd(proj, da, qaug, kaug, doaug):
    S = proj.shape[0]
    W = proj.shape[1] // 6
    P = W // LANES
    tq = min(TQ, S)
    nq = S // tq
    qc, kc, vc = 2 * P, 3 * P, 4 * P
    scale = 1.0 / math.sqrt(HEAD_DIM)

    def body(q_ref, k_ref, v_ref, do_ref, qa_ref, ka_ref, da_ref,
             dq_ref, dk_ref, dv_ref, drows_ref, dcols_ref, km_scr, vm_scr, dk_scr, dv_scr, dq_scr):
        pair, j = pl.program_id(0), pl.program_id(1)
        lane, halves = _head_halves(tq)

        @pl.when(jnp.logical_and(pair == 0, j == 0))
        def _():
            drows_ref[...] = jnp.zeros(drows_ref.shape, F32)
            dcols_ref[...] = jnp.zeros(dcols_ref.shape, F32)

        @pl.when(j == 0)
        def _():
            dq_scr[...] = jnp.zeros(dq_scr.shape, F32)

        v_ones = ((lane & (HEAD_DIM - 1)) < 3).astype(BF16)
        for a in range(2):
            km_scr[a] = jnp.where(halves[a], k_ref[...], ka_ref[...])
            vm_scr[a] = jnp.where(halves[a], v_ref[...], v_ones)
        dk_scr[...] = jnp.zeros(dk_scr.shape, F32)
        dv_scr[...] = jnp.zeros(dv_scr.shape, F32)

        def update(i, on_diagonal):
            rows = pl.ds(pl.multiple_of(i * tq, tq), tq)
            qs = q_ref[rows, :] * scale
            do2, qaug_t, doaug_t = do_ref[rows, :], qa_ref[rows, :], da_ref[rows, :]
            if on_diagonal:
                keep = (lax.broadcasted_iota(jnp.int32, (tq, tq), 0)
                        >= lax.broadcasted_iota(jnp.int32, (tq, tq), 1))
            qas = [jnp.where(halves[a], qs, qaug_t) for a in range(2)]
            logits = [lax.dot_general(qas[a], km_scr[a], NT, preferred_element_type=F32) for a in range(2)]
            dps = [lax.dot_general(jnp.where(halves[a], do2, doaug_t), vm_scr[a], NT, preferred_element_type=F32)
                   for a in range(2)]
            dv = None
            for a in range(2):
                s = jnp.where(keep, logits[a], NEG_INF) if on_diagonal else logits[a]
                p = jnp.exp(s)
                dsb = (p * dps[a]).astype(BF16)
                do0 = jnp.where(halves[a], do2, jnp.zeros_like(do2))
                dv_a = lax.dot_general(p.astype(BF16), do0, TN_DIMS, preferred_element_type=F32)
                dv = dv_a if dv is None else dv + dv_a
                dk_scr[a] += lax.dot_general(dsb, qas[a], TN_DIMS, preferred_element_type=F32)
                dq_scr[a, rows, :] += jnp.dot(dsb, km_scr[a], preferred_element_type=F32)
            dv_scr[...] += dv

        def below_diagonal(n, carry):
            update(j + 1 + 2 * n, False)
            update(j + 2 + 2 * n, False)
            return carry

        update(j, True)
        below = nq - 1 - j
        lax.fori_loop(0, below // 2, below_diagonal, 0)

        @pl.when(below % 2 == 1)
        def _():
            update(nq - 1, False)


        def to_head_lanes(old, first, second):
            at = lax.broadcasted_iota(jnp.int32, old.shape, 1) - 2 * pair
            return jnp.where(at == 0, first, jnp.where(at == 1, second, old))

        dk_ref[...] = jnp.where(halves[0], dk_scr[0], dk_scr[1]).astype(BF16)
        dv_ref[...] = dv_scr[...].astype(BF16)
        keys = pl.ds(pl.multiple_of(j * tq, tq), tq)
        ones_a, ones_b = HEAD_DIM + SLOT_ONE, SLOT_ONE
        dcols_ref[keys, :] = to_head_lanes(dcols_ref[keys, :], dk_scr[0][:, ones_a:ones_a + 1],
                                           dk_scr[1][:, ones_b:ones_b + 1])

        @pl.when(j == nq - 1)
        def _():
            row_lane, row_halves = _head_halves(S)
            dq_ref[...] = (jnp.where(row_halves[0], dq_scr[0], dq_scr[1]) * scale).astype(BF16)
            c_a, c_b = HEAD_DIM + SLOT_C, SLOT_C
            drows_ref[...] = to_head_lanes(drows_ref[...], dq_scr[0][:, c_a:c_a + 1], dq_scr[1][:, c_b:c_b + 1])

    tile = lambda col: pl.BlockSpec((tq, LANES), lambda p, j: (j, col + p))
    whole = lambda col: pl.BlockSpec((S, LANES), lambda p, j: (0, col + p))
    shared = pl.BlockSpec((S, LANES), lambda p, j: (0, 0))
    return pl.pallas_call(
        body, name="attn_bwd", grid=(P, nq),
        in_specs=[whole(qc), tile(kc), tile(vc), whole(P), whole(0), tile(0), whole(0)],
        out_specs=[whole(0), tile(0), tile(0), shared, shared],
        out_shape=[jax.ShapeDtypeStruct((S, W), BF16),
                   jax.ShapeDtypeStruct((S, W), BF16),
                   jax.ShapeDtypeStruct((S, W), BF16),
                   jax.ShapeDtypeStruct((S, LANES), F32),
                   jax.ShapeDtypeStruct((S, LANES), F32)],
        scratch_shapes=[pltpu.VMEM((2, tq, LANES), BF16),
                        pltpu.VMEM((2, tq, LANES), BF16),
                        pltpu.VMEM((2, tq, LANES), F32),
                        pltpu.VMEM((tq, LANES), F32),
                        pltpu.VMEM((2, S, LANES), F32)],
        compiler_params=_params("arbitrary", "arbitrary"),
    )(proj, proj, proj, da, qaug, kaug, doaug)


def _fgate_bwd(drows, dcols, z, bias):
    S = z.shape[0]
    tb = min(TB, S)
    nb = S // tb

    def body(drows_ref, dcols_ref, z_ref, b_ref, dz_ref, db_ref):
        tri = (lax.broadcasted_iota(jnp.int32, (tb, tb), 1)
               >= lax.broadcasted_iota(jnp.int32, (tb, tb), 0)).astype(F32)

        local = []
        for b in range(nb):
            rows = slice(b * tb, (b + 1) * tb)
            local.append(jnp.dot(tri, drows_ref[rows, :] - dcols_ref[rows, :], preferred_element_type=F32,
                                 precision=lax.Precision.HIGHEST))
        carry = jnp.zeros((1, LANES), F32)
        db = jnp.zeros((1, LANES), F32)
        for b in reversed(range(nb)):
            rows = slice(b * tb, (b + 1) * tb)
            rc = local[b] + carry
            carry = rc[0:1, :]
            dz = rc * _sigmoid(-(z_ref[rows, :] + b_ref[...]))
            dz_ref[rows, :] = dz.astype(BF16)
            db = db + jnp.sum(dz, axis=0, keepdims=True)
        db_ref[...] = db

    return pl.pallas_call(
        body, name="fgate_bwd",
        out_shape=[jax.ShapeDtypeStruct((S, LANES), BF16),
                   jax.ShapeDtypeStruct((1, LANES), F32)],
        compiler_params=pltpu.CompilerParams(vmem_limit_bytes=VMEM_LIMIT),
    )(drows, dcols, z, bias)


def _pool_bwd(proj, da, pool_w, pool_scale):
    S = proj.shape[0]
    G = len(POOL_WINDOWS)

    def body(u_ref, dy_ref, w_ref, s_ref, du_ref, dw_ref, ds_ref, pad_ref):
        g = pl.program_id(0)
        for gi, w in enumerate(POOL_WINDOWS):
            @pl.when(g == gi)
            def _():
                d, cnt = _window_mean_minus_self(u_ref[...].astype(F32), pad_ref, w, S)
                db = d.astype(BF16)
                wb = w_ref[0].astype(BF16)
                yraw = jnp.dot(db, wb, preferred_element_type=F32)
                dy = dy_ref[...].astype(F32)
                ds_ref[...] = jnp.sum(dy * yraw, axis=0, keepdims=True)
                dzb = (dy * s_ref[...]).astype(BF16)
                dw_ref[0] = lax.dot_general(db, dzb, TN_DIMS, preferred_element_type=F32)
                dd = lax.dot_general(dzb, wb, NT, preferred_element_type=F32)
                pad_ref[0:S, :] = dd / cnt
                pad_ref[S:S + MAX_WINDOW, :] = jnp.zeros((MAX_WINDOW, LANES), F32)
                acc = -dd
                for j in range(w):
                    acc = acc + pad_ref[j:j + S, :]
                du_ref[...] = acc.astype(BF16)

    return pl.pallas_call(
        body, name="pool_bwd", grid=(G,),
        in_specs=[pl.BlockSpec((S, LANES), lambda g: (0, g)),
                  pl.BlockSpec((S, LANES), lambda g: (0, g)),
                  pl.BlockSpec((1, LANES, LANES), lambda g: (g, 0, 0)),
                  pl.BlockSpec((1, LANES), lambda g: (0, g))],
        out_specs=[pl.BlockSpec((S, LANES), lambda g: (0, g)),
                   pl.BlockSpec((1, LANES, LANES), lambda g: (g, 0, 0)),
                   pl.BlockSpec((1, LANES), lambda g: (0, g))],
        out_shape=[jax.ShapeDtypeStruct((S, G * LANES), BF16),
                   jax.ShapeDtypeStruct((G, LANES, LANES), F32),
                   jax.ShapeDtypeStruct((1, G * LANES), F32)],
        scratch_shapes=[pltpu.VMEM((S + MAX_WINDOW, LANES), F32)],
        compiler_params=_params("arbitrary"),
    )(proj, da, pool_w, pool_scale)


def _inproj_bwd_dx(sections, dzf, w, x, gam, g, after=()):
    S, D = x.shape
    N = w.shape[1] - LANES
    W = D // 2
    n_sec = len(sections)
    tm = min(TM, S)

    def body(dz_ref, w_ref, wf_ref, x_ref, gam_ref, g_ref, *rest):
        sec_refs = rest[:n_sec]
        dx_ref, dg_ref = rest[-2:]

        @pl.when(pl.program_id(0) == 0)
        def _():
            dg_ref[...] = jnp.zeros(dg_ref.shape, F32)

        dh = lax.dot_general(dz_ref[...], wf_ref[...], NT, preferred_element_type=F32)
        for n, ref in enumerate(sec_refs):
            dh = dh + lax.dot_general(ref[...], w_ref[:, n * W:(n + 1) * W], NT, preferred_element_type=F32)
        xf = x_ref[...]
        r = lax.rsqrt(jnp.mean(xf * xf, axis=-1, keepdims=True) + RMS_EPS)
        xhat = xf * r
        dg_ref[...] += jnp.sum(dh * xhat, axis=0, keepdims=True)
        dxhat = dh * gam_ref[...]
        dx_ref[...] = g_ref[...] + r * (dxhat - xhat * jnp.mean(dxhat * xhat, axis=-1, keepdims=True))

    sec_specs, sec_arrays = _section_specs(sections, tm, W)
    return pl.pallas_call(
        body, name="inproj_bwd_dx", grid=(S // tm,),
        in_specs=[pl.BlockSpec((tm, LANES), lambda i: (i, 0)),
                  pl.BlockSpec((D, N), lambda i: (0, 0)),
                  pl.BlockSpec((D, LANES), lambda i: (0, N // LANES)),
                  pl.BlockSpec((tm, D), lambda i: (i, 0)),
                  pl.BlockSpec((1, D), lambda i: (0, 0)),
                  pl.BlockSpec((tm, D), lambda i: (i, 0))] + sec_specs + _after_specs(after),
        out_specs=[pl.BlockSpec((tm, D), lambda i: (i, 0)),
                   pl.BlockSpec((1, D), lambda i: (0, 0))],
        out_shape=[jax.ShapeDtypeStruct((S, D), F32),
                   jax.ShapeDtypeStruct((1, D), F32)],
        compiler_params=_params("arbitrary"),
    )(dzf, w, w, x, gam, g, *sec_arrays, *after)


def _adamw(w, m, v, gsets, name, rows, shifted=False, first=0, into=None):
    A, R, C = w.shape
    n_sets = len(gsets)
    tr = min(rows, R)
    c1 = 1.0 / (1.0 - ADAM_B1 ** ADAM_STEP)
    c2 = 1.0 / (1.0 - ADAM_B2 ** ADAM_STEP)
    counts = [len(gs) for gs in gsets]

    def body(w_ref, m_ref, v_ref, *rest):
        g_ref, d_ref, nm_ref, nv_ref = rest[-4:]
        at = 0
        for a in range(n_sets):
            part_refs = rest[at:at + counts[a]]
            at += counts[a]

            @pl.when(pl.program_id(0) == a)
            def _():
                g = None
                for ref in part_refs:
                    for s in range(ref.shape[0]):
                        term = ref[s].astype(F32)
                        g = term if g is None else g + term
                if shifted:
                    lanes = g.shape[1]
                    g = pltpu.roll(g, (lanes - _index(_position()) * (C % LANES)) % lanes, axis=1)[:, :C]
                nm = ADAM_B1 * m_ref[0] + (1.0 - ADAM_B1) * g
                nv = ADAM_B2 * v_ref[0] + (1.0 - ADAM_B2) * (g * g)
                g_ref[0] = g
                nm_ref[0] = nm
                nv_ref[0] = nv
                d_ref[0] = -ADAM_LR * ((nm * c1) / (jnp.sqrt(nv * c2) + ADAM_EPS) + ADAM_WD * w_ref[0])

    spec = pl.BlockSpec((1, tr, C), lambda a, r: (first + a, r, 0))
    part_specs = [pl.BlockSpec((part.shape[0], tr, part.shape[2]), lambda a, r, l=l: (0, jnp.where(a == l, r, 0), 0))
                  for l, gs in enumerate(gsets) for part in gs]
    parts = [part for gs in gsets for part in gs]
    shape = jax.ShapeDtypeStruct((A, R, C), F32)
    earlier = () if into is None else tuple(into)
    return pl.pallas_call(
        body, name=name, grid=(n_sets, R // tr),
        in_specs=[spec, spec, spec] + part_specs + _after_specs(earlier),
        out_specs=[spec, spec, spec, spec],
        out_shape=[shape, shape, shape, shape],
        input_output_aliases={3 + len(parts) + n: n for n in range(len(earlier))},
        compiler_params=_params("arbitrary", "arbitrary"),
    )(w, m, v, *parts, *earlier)


def _position():
    return lax.axis_index("x"), lax.axis_index("y"), lax.axis_index("c")


def _index(dev):
    return 4 * dev[0] + 2 * dev[1] + dev[2]


def _all_gather(arrs, slots, out_shapes, name):
    n_arr = len(arrs)

    def body(*refs):
        ins, outs = refs[:n_arr], refs[n_arr:2 * n_arr]
        send_sems, recv_sems, local_sems = refs[2 * n_arr:]
        x, y, c = _position()
        me, sibling = (x, y, c), (x, y, 1 - c)
        chips = [(1 - x, y), (x, 1 - y), (1 - x, 1 - y)]

        def copy(a, k, block, to, src=None):
            part = slots[a](outs[a], _index(block))
            return pltpu.make_async_remote_copy(
                src_ref=part if src is None else src, dst_ref=part,
                send_sem=send_sems.at[a, k], recv_sem=recv_sems.at[a, k],
                device_id=to, device_id_type=MESH)

        mine = [pltpu.make_async_copy(ins[a], slots[a](outs[a], _index(me)), local_sems.at[a])
                for a in range(n_arr)]
        for cp in mine:
            cp.start()
        first = []
        for a in range(n_arr):
            first.append(copy(a, 0, me, sibling, src=ins[a]))
            first += [copy(a, 1 + j, me, (*chip, c), src=ins[a]) for j, chip in enumerate(chips)]
        for cp in first:
            cp.start()
        passed = []
        for j, chip in enumerate(chips):
            for a in range(n_arr):
                copy(a, 1 + j, (*chip, c), me).wait_recv()
                fwd = copy(a, 4 + j, (*chip, c), sibling)
                fwd.start()
                passed.append(fwd)
        for a in range(n_arr):
            copy(a, 0, sibling, me).wait_recv()
            for j, chip in enumerate(chips):
                copy(a, 4 + j, (*chip, 1 - c), me).wait_recv()
        for cp in first + passed:
            cp.wait_send()
        for cp in mine:
            cp.wait()

    any_spec = pl.BlockSpec(memory_space=pl.ANY)
    return pl.pallas_call(
        body, name=name,
        in_specs=[any_spec] * n_arr, out_specs=[any_spec] * n_arr, out_shape=out_shapes,
        scratch_shapes=[pltpu.SemaphoreType.DMA((n_arr, 7)), pltpu.SemaphoreType.DMA((n_arr, 7)),
                        pltpu.SemaphoreType.DMA((n_arr,))],
    )(*arrs)


def _split_copies(srcs, lands, send_sems, recv_sems, kinds):
    x, y, c = _position()
    me = _index((x, y, c))
    copies = []
    for a, (src_part, land_part) in enumerate(kinds):
        for k in range(1, N_DEV):
            peer = (x ^ ((k >> 2) & 1), y ^ ((k >> 1) & 1), c ^ (k & 1))
            copies.append(pltpu.make_async_remote_copy(
                src_ref=src_part(srcs[a], _index(peer)), dst_ref=land_part(lands[a], me, k),
                send_sem=send_sems[a].at[k - 1], recv_sem=recv_sems[a].at[k - 1],
                device_id=peer, device_id_type=MESH))
    return copies


def _split_start(srcs, lands, kinds, name, after=()):
    n = len(srcs)

    def body(*refs):
        src_refs, land_refs = refs[:n], refs[n:2 * n]
        outs = refs[2 * n + len(after):]
        send_sems, recv_sems = outs[:n], outs[n:2 * n]
        token = outs[-1]
        for cp in _split_copies(src_refs, land_refs, send_sems, recv_sems, kinds):
            cp.start()
        token[...] = jnp.zeros(token.shape, token.dtype)

    hbm = pl.BlockSpec(memory_space=pltpu.HBM)
    sem = pl.BlockSpec(memory_space=pltpu.SEMAPHORE)
    operands = [pltpu.with_memory_space_constraint(t, pltpu.HBM) for t in (*srcs, *lands)]
    out = pl.pallas_call(
        body, name=name,
        in_specs=[hbm] * (2 * n) + _after_specs(after),
        out_specs=[sem] * (2 * n) + [hbm] * (2 * n) + [pl.BlockSpec(memory_space=pltpu.VMEM)],
        out_shape=[pltpu.SemaphoreType.DMA((N_DEV - 1,))] * (2 * n)
        + [pltpu.HBM(t.shape, t.dtype) for t in operands] + [jax.ShapeDtypeStruct((8, LANES), F32)],
        input_output_aliases={i: 2 * n + i for i in range(2 * n)},
        compiler_params=pltpu.CompilerParams(has_side_effects=pltpu.SideEffectType.DATAFLOW_SIDE_EFFECTING),
    )(*operands, *after)
    return [(out[a], out[n + a], out[2 * n + a], out[3 * n + a]) for a in range(n)], out[-1]


def _split_wait(started, kinds, after, name):
    n = len(started)
    sems = [t[0] for t in started] + [t[1] for t in started]
    srcs = [t[2] for t in started]
    lands = [t[3] for t in started]

    def body(*refs):
        src_refs, land_refs = refs[:n], refs[n:2 * n]
        send_sems, recv_sems = refs[2 * n:3 * n], refs[3 * n:4 * n]
        for cp in _split_copies(src_refs, land_refs, send_sems, recv_sems, kinds):
            cp.wait_send()
            cp.wait_recv()

    hbm = pl.BlockSpec(memory_space=pltpu.HBM)
    sem = pl.BlockSpec(memory_space=pltpu.SEMAPHORE)
    out = pl.pallas_call(
        body, name=name,
        in_specs=[hbm] * (2 * n) + [sem] * (2 * n) + _after_specs(after),
        out_specs=[hbm] * (2 * n),
        out_shape=[pltpu.HBM(t.shape, t.dtype) for t in (*srcs, *lands)],
        input_output_aliases={i: i for i in range(2 * n)},
        compiler_params=pltpu.CompilerParams(has_side_effects=pltpu.SideEffectType.DATAFLOW_SIDE_EFFECTING),
    )(*srcs, *lands, *sems, *after)
    return out[n:]


def _as_rows(p):
    if p.size % LANES == 0:
        rows = p.reshape(-1, LANES)
    else:
        rows = p.reshape(-1, p.shape[-1])
        rows = jnp.pad(rows, ((0, 0), (0, LANES - rows.shape[1])))
    return jnp.pad(rows, ((0, -rows.shape[0] % 8), (0, 0)))


def _pack(parts):
    return jnp.concatenate([_as_rows(p) for p in parts])[None]


def _unpack(packed, like):
    out, at = [], 0
    for p in like:
        whole = p.size % LANES == 0
        n = p.size // LANES if whole else p.size // p.shape[-1]
        rows = packed[0, at:at + n]
        out.append((rows if whole else rows[:, :p.shape[-1]]).reshape(p.shape))
        at += n + (-n % 8)
    return out


def _local_step(x, target, norm_g, forget_bias, pool_w, pool_scale, final_g, weights_in, weights_out, on_grads,
                first_after=()):
    L = norm_g.shape[0]
    S, D = x.shape
    W = D // 2
    H = W // HEAD_DIM
    bias = jnp.pad(forget_bias, ((0, 0), (0, LANES - H)))

    saved = []
    after = tuple(first_after)
    for l in range(L):
        proj, h, z, w = _inproj_fwd(x, norm_g[l:l + 1], weights_in(l, x), after)
        after = ()
        qaug, kaug = _fgate_fwd(z, bias[l:l + 1], H)
        ypool = _pool_fwd(proj, pool_w[l], pool_scale[l:l + 1])
        o, qaug_b = _attn_fwd(proj, qaug, kaug)
        wout = weights_out(l, o)
        x_in = x
        if l < L - 1:
            (x,) = _outproj_fwd(ypool, o, proj, x_in, wout)
        else:
            g, loss, d_final_g = _outproj_fwd(ypool, o, proj, x_in, wout, (final_g.reshape(1, D), target))
        saved.append((x_in, proj, h, z, qaug_b, kaug, ypool, o, w, wout))

    small = None
    for l in reversed(range(L)):
        x_in, proj, h, z, qaug_b, kaug, ypool, o, w, wout = saved[l]
        d_wout, d_wout_bf16, da, dgate, doaug = _outproj_bwd(g, wout, ypool, o, proj)
        dq, dk, dv, drows, dcols = _attn_bwd(proj, da, qaug_b, kaug, doaug)
        dzf, db = _fgate_bwd(drows, dcols, z, bias[l:l + 1])
        dpu, dpw, dps = _pool_bwd(proj, da, pool_w[l], pool_scale[l:l + 1])
        dproj = [(dpu, 0), (dgate, 0), (dq, 0), (dk, 0), (dv, 0), (dgate, 1)]
        d_w, d_w_bf16 = _inproj_bwd_dw(h, dproj, dzf)
        after = tuple(on_grads(l, d_w, d_w_bf16, d_wout, d_wout_bf16, small))
        g, dgam = _inproj_bwd_dx(dproj, dzf, w, x_in, norm_g[l:l + 1], g, after)
        small = (dgam[0], db[0, :H], dpw, dps[0])
    return loss[0, 0], g, small, d_final_g[0]


def kernel(x, norm_g, w_in, forget_bias, pool_w, pool_scale, w_out, final_g, loss_target, m_norm_g, m_w_in, m_forget_bias, m_pool_w, m_pool_scale, m_w_out, m_final_g, v_norm_g, v_w_in, v_forget_bias, v_pool_w, v_pool_scale, v_w_out, v_final_g):
    L, D, cols = w_in.shape
    rows_out = w_out.shape[1]
    me = _index(_position())
    slot = _slot_width(cols)
    wout_b = w_out.astype(BF16)
    win_b = _shift_slots(w_in)
    gather_in = (lambda ref, peer: ref, lambda ref, mine, k: ref.at[mine])
    gather_out = (lambda ref, peer: ref, lambda ref, mine, k: ref.at[pl.ds(mine * rows_out, rows_out), :])

    def landing(block, n_slots):
        zone = lax.empty((n_slots * block.shape[0], *block.shape[1:]), block.dtype)
        return lax.dynamic_update_slice(zone, block, (me * block.shape[0],) + (0,) * (block.ndim - 1))

    (first_in,) = _all_gather([win_b[0]], [lambda ref, n: ref.at[n]],
                              [jax.ShapeDtypeStruct((N_DEV, D, slot), BF16)], "gather_first")
    rest_srcs = [wout_b[0]] + [w[l] for l in range(1, L) for w in (win_b, wout_b)]
    rest_lands = [landing(wout_b[0], N_DEV)]
    for l in range(1, L):
        rest_lands += [landing(win_b[l][None], N_DEV), landing(wout_b[l], N_DEV)]
    rest_kinds = [gather_out] + [gather_in, gather_out] * (L - 1)
    rest, rest_token = _split_start(rest_srcs, rest_lands, rest_kinds, "gather_start_rest", (first_in,))

    def weights_in(l, x_in):
        if l == 0:
            return first_in
        (win_all,) = _split_wait([rest[2 * l - 1]], [gather_in], (x_in,), f"gather_wait_in_{l}")
        return win_all

    def weights_out(l, o):
        (wout_full,) = _split_wait([rest[2 * l]], [gather_out], (o,), f"gather_wait_out_{l}")
        return wout_full

    stride = slot - LANES
    exchange_kinds = [(lambda ref, peer: ref.at[:, pl.ds(pl.multiple_of(peer * stride, LANES), slot)],
                       lambda ref, mine, k: ref.at[k - 1]),
                      (lambda ref, peer: ref.at[pl.ds(peer * rows_out, rows_out), :],
                       lambda ref, mine, k: ref.at[k - 1])]
    zero_g = jnp.zeros_like(final_g)
    zero_loss = jnp.zeros((LANES,), F32)

    def small_pack(l, norm_g_l, bias_l, pool_w_l, pool_scale_l, final, loss_row=None):
        return _pack([norm_g_l, bias_l, pool_w_l, pool_scale_l, final if l == 0 else zero_g,
                      zero_loss if loss_row is None else loss_row])[0]

    exchanges, own_parts = {}, {}

    def on_grads(l, dw, dw_bf16, d_wout, d_wout_bf16, small):
        own_parts[l] = (lax.dynamic_slice_in_dim(dw, me * stride, slot, 1)[None],
                        lax.dynamic_slice_in_dim(d_wout, me * rows_out, rows_out, 0)[None])
        srcs = [dw_bf16, d_wout_bf16]
        lands = [lax.empty((N_DEV - 1, D, slot), BF16), lax.empty((N_DEV - 1, rows_out, D), BF16)]
        kinds = list(exchange_kinds)
        if small is not None:
            packed_small = small_pack(l + 1, *small, None)
            srcs.append(packed_small)
            lands.append(landing(packed_small[None], N_DEV))
            kinds.append(gather_in)
        started, token = _split_start(srcs, lands, kinds, f"exchange_start_{l}")
        exchanges[l] = (started, kinds)
        return (token,)

    loss, dx, small_first, d_final_g = _local_step(
        x[0], loss_target[0], norm_g, forget_bias, pool_w, pool_scale, final_g,
        weights_in, weights_out, on_grads, (rest_token,))
    packed_first = small_pack(0, *small_first, d_final_g, jnp.full((LANES,), loss, F32))
    first_started, first_token = _split_start(
        [packed_first], [landing(packed_first[None], N_DEV)], [gather_in],
        "small_start", (w_in, m_w_in, v_w_in, *own_parts[0]))

    gin_sets, gout_sets, small_sets = [None] * L, [None] * L, [None] * L

    def wait_for(l, after):
        started, kinds = exchanges[l]
        got = _split_wait(started, kinds, after, f"exchange_wait_{l}")
        gin_sets[l] = [own_parts[l][0], got[0]]
        gout_sets[l] = [own_parts[l][1], got[1]]
        if len(got) > 2:
            small_sets[l + 1] = [got[2]]

    for l in range(1, L):
        wait_for(l, (dx, first_token))
    rest_in = _adamw(w_in, m_w_in, v_w_in, gin_sets[1:], "adamw_w_in_rest", TM // 2, shifted=True, first=1)
    rest_out = _adamw(w_out, m_w_out, v_w_out, gout_sets[1:], "adamw_w_out_rest", rows_out, first=1)
    wait_for(0, (rest_in[1], rest_out[1]))
    g_w_in, d_w_in, nm_w_in, nv_w_in = _adamw(w_in, m_w_in, v_w_in, gin_sets[:1], "adamw_w_in_first", TM // 2,
                                              shifted=True, into=rest_in)
    g_w_out, d_w_out, nm_w_out, nv_w_out = _adamw(w_out, m_w_out, v_w_out, gout_sets[:1], "adamw_w_out_first",
                                                  rows_out, into=rest_out)
    small_sets[0] = _split_wait(first_started, [gather_in], (d_w_in, d_w_out), "small_wait")
    loss = jnp.sum(small_sets[0][0][:, packed_first.shape[0] - 8, 0])

    def small_stack(norm_g_, bias_, pool_w_, pool_scale_, final):
        return jnp.stack([small_pack(l, norm_g_[l], bias_[l], pool_w_[l], pool_scale_[l], final) for l in range(L)])

    packed = _adamw(small_stack(norm_g, forget_bias, pool_w, pool_scale, final_g),
                    small_stack(m_norm_g, m_forget_bias, m_pool_w, m_pool_scale, m_final_g),
                    small_stack(v_norm_g, v_forget_bias, v_pool_w, v_pool_scale, v_final_g),
                    small_sets, "adamw_small", packed_first.shape[0])

    def small_unpack(p):
        like = [norm_g[0], forget_bias[0], pool_w[0], pool_scale[0], final_g]
        layers = [_unpack(p[l:l + 1], like) for l in range(L)]
        return [jnp.stack([layers[l][n] for l in range(L)]) for n in range(4)] + [layers[0][4]]

    g_s, d_s, nm_s, nv_s = [small_unpack(p) for p in packed]

    def order(big_in, big_out, small):
        return (small[0], big_in, small[1], small[2], small[3], big_out, small[4])

    return (loss, dx[None], *order(g_w_in, g_w_out, g_s), *order(d_w_in, d_w_out, d_s),
            *order(nm_w_in, nm_w_out, nm_s), *order(nv_w_in, nv_w_out, nv_s))
```

```python
import math

import jax
import jax.numpy as jnp
from jax import lax
from jax.experimental import pallas as pl
from jax.experimental.pallas import tpu as pltpu

F32 = jnp.float32
BF16 = jnp.bfloat16
MESH = pl.DeviceIdType.MESH

RMS_EPS = 1e-6
NEG_INF = -1e30
HEAD_DIM = 64
POOL_WINDOWS = (2, 4, 8, 16)
MAX_WINDOW = 16
LANES = 128
N_DEV = 8

ADAM_LR = 0.001
ADAM_B1 = 0.9
ADAM_B2 = 0.999
ADAM_EPS = 1e-08
ADAM_WD = 0.01
ADAM_STEP = 10

TM = 512
TN = 512
TQ = 512
TB = 256
VMEM_LIMIT = 56 * 1024 * 1024

NT = (((1,), (1,)), ((), ()))
TN_DIMS = (((0,), (0,)), ((), ()))

SLOT_C, SLOT_ONE, SLOT_LSE = 0, 3, 6


def _params(*sem):
    return pltpu.CompilerParams(dimension_semantics=sem, vmem_limit_bytes=VMEM_LIMIT)


def _sigmoid(x):
    return 1.0 / (1.0 + jnp.exp(-x))


def _split3(x):
    hi = x.astype(BF16).astype(F32)
    rest = x - hi
    mid = rest.astype(BF16).astype(F32)
    return hi, mid, rest - mid


def _after_specs(after):
    return [pl.BlockSpec(memory_space=pl.ANY)] * len(after)


def _slot_width(cols):
    return LANES * (-(-(cols + (N_DEV - 1) * (cols % LANES)) // LANES))


def _shift_slots(w_in):
    L, D, cols = w_in.shape
    slot = _slot_width(cols)
    tr = min(TM // 2, D)

    def body(w_ref, o_ref, pad_scr):
        pad_scr[...] = jnp.zeros(pad_scr.shape, F32)
        pad_scr[:, 0:cols] = w_ref[0]
        o_ref[0] = pltpu.roll(pad_scr[...], _index(_position()) * (cols % LANES), axis=1).astype(BF16)

    return pl.pallas_call(
        body, name="shift_slots", grid=(L, D // tr),
        in_specs=[pl.BlockSpec((1, tr, cols), lambda l, r: (l, r, 0))],
        out_specs=pl.BlockSpec((1, tr, slot), lambda l, r: (l, r, 0)),
        out_shape=jax.ShapeDtypeStruct((L, D, slot), BF16),
        scratch_shapes=[pltpu.VMEM((tr, slot), F32)],
        compiler_params=_params("parallel", "parallel"),
    )(w_in)


def _inproj_fwd(x, gam, slots, after=()):
    S, D = x.shape
    n_dev, _, sw = slots.shape
    stride = sw - LANES
    width = stride * n_dev + LANES
    N = width - LANES
    tm, tn = min(TM, S), TN

    def body(x_ref, g_ref, s_ref, *rest):
        proj_ref, h_ref, z_ref, w_ref = rest[-4:]

        @pl.when(pl.program_id(0) == 0)
        def _():
            for n in range(n_dev):
                base = stride * n
                first = s_ref[n, :, 0:LANES]
                if n > 0:
                    first = first + s_ref[n - 1, :, stride:sw]
                w_ref[:, base:base + LANES] = first
                w_ref[:, base + LANES:base + stride] = s_ref[n, :, LANES:stride]
            w_ref[:, stride * n_dev:width] = s_ref[n_dev - 1, :, stride:sw]

        xf = x_ref[...]
        r = lax.rsqrt(jnp.mean(xf * xf, axis=-1, keepdims=True) + RMS_EPS)
        h = ((xf * r) * g_ref[...]).astype(BF16)
        h_ref[...] = h
        z_ref[...] = jnp.dot(h, w_ref[:, N:width], preferred_element_type=F32)
        for n in range(N // tn):
            cols = slice(n * tn, (n + 1) * tn)
            proj_ref[:, cols] = jnp.dot(h, w_ref[:, cols], preferred_element_type=F32).astype(BF16)

    return pl.pallas_call(
        body, name="inproj_fwd", grid=(S // tm,),
        in_specs=[pl.BlockSpec((tm, D), lambda i: (i, 0)),
                  pl.BlockSpec((1, D), lambda i: (0, 0)),
                  pl.BlockSpec((n_dev, D, sw), lambda i: (0, 0, 0))] + _after_specs(after),
        out_specs=[pl.BlockSpec((tm, N), lambda i: (i, 0)),
                   pl.BlockSpec((tm, D), lambda i: (i, 0)),
                   pl.BlockSpec((tm, LANES), lambda i: (i, 0)),
                   pl.BlockSpec((D, width), lambda i: (0, 0))],
        out_shape=[jax.ShapeDtypeStruct((S, N), BF16),
                   jax.ShapeDtypeStruct((S, D), BF16),
                   jax.ShapeDtypeStruct((S, LANES), F32),
                   jax.ShapeDtypeStruct((D, width), BF16)],
        compiler_params=_params("arbitrary"),
    )(x, gam, slots, *after)


def _fgate_fwd(z, bias, n_heads):
    S = z.shape[0]
    tb = min(TB, S)
    P = n_heads // 2
    assert n_heads <= 8, "the three parts of c are packed eight lanes apart"

    def body(z_ref, b_ref, qaug_ref, kaug_ref):
        lane = lax.broadcasted_iota(jnp.int32, (tb, LANES), 1)
        tri = (lax.broadcasted_iota(jnp.int32, (tb, tb), 0)
               >= lax.broadcasted_iota(jnp.int32, (tb, tb), 1)).astype(F32)
        row = lax.broadcasted_iota(jnp.int32, (LANES, P * LANES), 0)
        col = lax.broadcasted_iota(jnp.int32, (LANES, P * LANES), 1)
        head, part_n = row & 7, row >> 3
        home = (head >> 1) * LANES + jnp.where((head & 1) == 0, HEAD_DIM, 0)
        is_part = jnp.logical_and(head < n_heads, part_n < 3)
        place_q = jnp.logical_and(is_part, col == home + SLOT_C + part_n).astype(BF16)
        place_k = jnp.logical_and(is_part, col == home + SLOT_ONE + part_n).astype(BF16)
        slot = lax.broadcasted_iota(jnp.int32, (tb, P * LANES), 1) & (HEAD_DIM - 1)
        q_ones = jnp.logical_and(slot >= SLOT_ONE, slot < SLOT_ONE + 3).astype(F32)
        k_ones = jnp.logical_or(slot < SLOT_C + 3,
                                jnp.logical_and(slot >= SLOT_LSE, slot < SLOT_LSE + 3)).astype(F32)

        local = []
        for b in range(S // tb):
            zz = z_ref[b * tb:(b + 1) * tb, :] + b_ref[...]
            lf = jnp.minimum(zz, 0.0) - jnp.log(1.0 + jnp.exp(-jnp.abs(zz)))
            lf = jnp.where(lane < n_heads, lf, 0.0)
            local.append(jnp.dot(tri, lf, preferred_element_type=F32, precision=lax.Precision.HIGHEST))
        carry = jnp.zeros((1, LANES), F32)
        for b, part_sum in enumerate(local):
            c = part_sum + carry
            carry = c[tb - 1:tb, :]
            hi, mid, lo = _split3(c)
            packed = (hi + pltpu.roll(mid, 8, axis=1) + pltpu.roll(lo, 16, axis=1)).astype(BF16)
            qaug_ref[b * tb:(b + 1) * tb, :] = (
                q_ones + jnp.dot(packed, place_q, preferred_element_type=F32)).astype(BF16)
            kaug_ref[b * tb:(b + 1) * tb, :] = (
                k_ones - jnp.dot(packed, place_k, preferred_element_type=F32)).astype(BF16)

    return pl.pallas_call(
        body, name="fgate_fwd",
        out_shape=[jax.ShapeDtypeStruct((S, P * LANES), BF16),
                   jax.ShapeDtypeStruct((S, P * LANES), BF16)],
        compiler_params=pltpu.CompilerParams(vmem_limit_bytes=VMEM_LIMIT),
    )(z, bias)


def _window_mean_minus_self(u, pad_ref, w, S):
    pad_ref[0:MAX_WINDOW, :] = jnp.zeros((MAX_WINDOW, LANES), F32)
    pad_ref[MAX_WINDOW:MAX_WINDOW + S, :] = u
    acc = u
    for j in range(1, w):
        acc = acc + pad_ref[MAX_WINDOW - j:MAX_WINDOW - j + S, :]
    t = lax.broadcasted_iota(jnp.int32, (S, LANES), 0)
    cnt = jnp.minimum(t + 1, w).astype(F32)
    return acc / cnt - u, cnt


def _pool_fwd(proj, pool_w, pool_scale):
    S = proj.shape[0]
    G = len(POOL_WINDOWS)

    def body(u_ref, w_ref, s_ref, y_ref, pad_ref):
        g = pl.program_id(0)
        for gi, w in enumerate(POOL_WINDOWS):
            @pl.when(g == gi)
            def _():
                d, _ = _window_mean_minus_self(u_ref[...].astype(F32), pad_ref, w, S)
                y = jnp.dot(d.astype(BF16), w_ref[0].astype(BF16), preferred_element_type=F32)
                y_ref[...] = (y * s_ref[...]).astype(BF16)

    return pl.pallas_call(
        body, name="pool_fwd", grid=(G,),
        in_specs=[pl.BlockSpec((S, LANES), lambda g: (0, g)),
                  pl.BlockSpec((1, LANES, LANES), lambda g: (g, 0, 0)),
                  pl.BlockSpec((1, LANES), lambda g: (0, g))],
        out_specs=pl.BlockSpec((S, LANES), lambda g: (0, g)),
        out_shape=jax.ShapeDtypeStruct((S, G * LANES), BF16),
        scratch_shapes=[pltpu.VMEM((S + MAX_WINDOW, LANES), F32)],
        compiler_params=_params("arbitrary"),
    )(proj, pool_w, pool_scale)


def _head_halves(rows):
    lane = lax.broadcasted_iota(jnp.int32, (rows, LANES), 1)
    return lane, (lane < HEAD_DIM, lane >= HEAD_DIM)


def _attn_fwd(proj, qaug, kaug):
    S = proj.shape[0]
    W = proj.shape[1] // 6
    P = W // LANES
    tk = min(TQ, S // 2)
    tq = 2 * tk
    nq = S // tq
    qc, kc, vc = 2 * P, 3 * P, 4 * P
    scale = 1.0 / math.sqrt(HEAD_DIM)

    def body(q_ref, k_ref, v_ref, qa_ref, ka_ref, o_ref, qb_ref, qm_scr, m_scr, acc_scr):
        i = pl.program_id(1)
        lane, halves = _head_halves(tq)
        key_halves = (halves[0][:tk], halves[1][:tk])
        v_ones = ((lane[:tk] & (HEAD_DIM - 1)) < 3).astype(BF16)
        qs = q_ref[...] * scale
        qm_scr[0] = jnp.where(halves[0], qs, qa_ref[...])
        qm_scr[1] = jnp.where(halves[1], qs, qa_ref[...])
        m_scr[...] = jnp.full(m_scr.shape, NEG_INF, F32)
        acc_scr[...] = jnp.zeros(acc_scr.shape, F32)
        top, bottom, both = slice(0, tk), slice(tk, tq), slice(0, tq)

        def update(rows, j, on_diagonal):
            keys = pl.ds(pl.multiple_of(j * tk, tk), tk)
            k2, v2, kaug_t = k_ref[keys, :], v_ref[keys, :], ka_ref[keys, :]
            if on_diagonal:
                keep = (lax.broadcasted_iota(jnp.int32, (tk, tk), 0)
                        >= lax.broadcasted_iota(jnp.int32, (tk, tk), 1))
            logits = [lax.dot_general(qm_scr[a, rows, :], jnp.where(key_halves[a], k2, kaug_t), NT,
                                      preferred_element_type=F32) for a in range(2)]
            for a in range(2):
                s = jnp.where(keep, logits[a], NEG_INF) if on_diagonal else logits[a]
                va = jnp.where(key_halves[a], v2, v_ones)
                m_prev = m_scr[a, rows, :]
                m_new = jnp.maximum(m_prev, jnp.max(s, axis=1, keepdims=True))
                p = jnp.exp(s - jnp.tile(m_new, (1, tk // LANES)))
                acc_scr[a, rows, :] = (jnp.exp(m_prev - m_new) * acc_scr[a, rows, :]
                                       + jnp.dot(p.astype(BF16), va, preferred_element_type=F32))
                m_scr[a, rows, :] = m_new

        def below_diagonal(jj, carry):
            update(both, 2 * jj, False)
            update(both, 2 * jj + 1, False)
            return carry

        lax.fori_loop(0, i, below_diagonal, 0)
        update(top, 2 * i, True)
        update(bottom, 2 * i, False)
        update(bottom, 2 * i + 1, True)
        acc_a, acc_b = acc_scr[0], acc_scr[1]
        l_a, l_b = acc_a[:, HEAD_DIM:HEAD_DIM + 1], acc_b[:, 0:1]
        o_ref[...] = jnp.where(halves[0], acc_a / l_a, acc_b / l_b).astype(BF16)
        lse = jnp.where(halves[0], m_scr[1] + jnp.log(l_b), m_scr[0] + jnp.log(l_a))
        slot = lane & (HEAD_DIM - 1)
        aug = qa_ref[...].astype(F32)
        for n, part in enumerate(_split3(lse)):
            aug = jnp.where(slot == SLOT_LSE + n, -part, aug)
        qb_ref[...] = aug.astype(BF16)

    tile = lambda col: pl.BlockSpec((tq, LANES), lambda p, i: (i, col + p))
    whole = lambda col: pl.BlockSpec((S, LANES), lambda p, i: (0, col + p))
    return pl.pallas_call(
        body, name="attn_fwd", grid=(P, nq),
        in_specs=[tile(qc), whole(kc), whole(vc), tile(0), whole(0)],
        out_specs=[tile(0), tile(0)],
        out_shape=[jax.ShapeDtypeStruct((S, W), BF16), jax.ShapeDtypeStruct((S, W), BF16)],
        scratch_shapes=[pltpu.VMEM((2, tq, LANES), BF16),
                        pltpu.VMEM((2, tq, LANES), F32),
                        pltpu.VMEM((2, tq, LANES), F32)],
        compiler_params=_params("parallel", "arbitrary"),
    )(proj, proj, proj, qaug, kaug)


def _outproj_fwd(ypool, o, proj, x, wout, head=None):
    S, D = x.shape
    W = D // 2
    tm, tn = min(TM, S), TN

    def body(y_ref, o_ref, pg_ref, ag_ref, x_ref, w_ref, *rest):
        mix_ref = rest[-1]
        out_ref = rest[-4] if head else rest[0]
        pg, ag = pg_ref[...].astype(F32), ag_ref[...].astype(F32)
        mix_ref[:, 0:W] = (y_ref[...].astype(F32) * (pg * _sigmoid(pg))).astype(BF16)
        mix_ref[:, W:D] = (o_ref[...].astype(F32) * (ag * _sigmoid(ag))).astype(BF16)
        for n in range(D // tn):
            cols = slice(n * tn, (n + 1) * tn)
            out_ref[:, cols] = x_ref[:, cols] + jnp.dot(mix_ref[...], w_ref[:, cols], preferred_element_type=F32)
        if not head:
            return
        gam_ref, t_ref = rest[:2]
        loss_ref, dg_ref = rest[-3:-1]

        @pl.when(pl.program_id(0) == 0)
        def _():
            loss_ref[...] = jnp.zeros(loss_ref.shape, F32)
            dg_ref[...] = jnp.zeros(dg_ref.shape, F32)

        xf, gam_v = out_ref[...], gam_ref[...]
        r = lax.rsqrt(jnp.mean(xf * xf, axis=-1, keepdims=True) + RMS_EPS)
        xhat = xf * r
        err = xhat * gam_v - t_ref[...]
        part = jnp.sum(jnp.sum(err * err, axis=-1, keepdims=True), axis=0, keepdims=True)
        loss_ref[...] += part * (0.5 / D)
        dy = err * (1.0 / D)
        dg_ref[...] += jnp.sum(dy * xhat, axis=0, keepdims=True)
        dxhat = dy * gam_v
        out_ref[...] = r * (dxhat - xhat * jnp.mean(dxhat * xhat, axis=-1, keepdims=True))

    rows = lambda width, col: pl.BlockSpec((tm, width), lambda i: (i, col))
    in_specs = [rows(W, 0), rows(W, 0), rows(W, 1), rows(W, 5), rows(D, 0), pl.BlockSpec((D, D), lambda i: (0, 0))]
    out_specs = [rows(D, 0)]
    out_shape = [jax.ShapeDtypeStruct((S, D), F32)]
    if head:
        in_specs += [pl.BlockSpec((1, D), lambda i: (0, 0)), rows(D, 0)]
        out_specs += [pl.BlockSpec((8, LANES), lambda i: (0, 0)), pl.BlockSpec((1, D), lambda i: (0, 0))]
        out_shape += [jax.ShapeDtypeStruct((8, LANES), F32), jax.ShapeDtypeStruct((1, D), F32)]
    return pl.pallas_call(
        body, name="outproj_fwd_loss" if head else "outproj_fwd", grid=(S // tm,),
        in_specs=in_specs, out_specs=out_specs, out_shape=out_shape,
        scratch_shapes=[pltpu.VMEM((tm, D), BF16)],
        compiler_params=_params("arbitrary" if head else "parallel"),
    )(ypool, o, proj, proj, x, wout, *(head or ()))


def _outproj_bwd(g, wout, ypool, o, proj, after=()):
    S, D = g.shape
    W = D // 2
    tm = min(TM, S)

    def body(g_ref, w_ref, y_ref, o_ref, pg_ref, ag_ref, *rest):
        dw_ref, dwb_ref, da_ref, dgate_ref, doaug_ref = rest[-5:]

        @pl.when(pl.program_id(0) == 0)
        def _():
            dw_ref[...] = jnp.zeros(dw_ref.shape, F32)

        gb = g_ref[...].astype(BF16)
        dmixes = [lax.dot_general(gb, w_ref[half * W:(half + 1) * W, :], NT, preferred_element_type=F32)
                  for half in range(2)]
        d_o = None
        for half, (val_ref, gate_ref) in enumerate(((y_ref, pg_ref), (o_ref, ag_ref))):
            cols = slice(half * W, (half + 1) * W)
            gt = gate_ref[...].astype(F32)
            sg = _sigmoid(gt)
            silu = gt * sg
            val = val_ref[...].astype(F32)
            dw_ref[cols, :] += lax.dot_general((val * silu).astype(BF16), gb, TN_DIMS, preferred_element_type=F32)
            d_o = (dmixes[half] * silu).astype(BF16)
            da_ref[:, cols] = d_o
            dgate_ref[:, cols] = (dmixes[half] * val * (sg * (1.0 + gt * (1.0 - sg)))).astype(BF16)

        lane, halves = _head_halves(tm)
        slot = lane & (HEAD_DIM - 1)
        for p in range(W // LANES):
            cols = slice(p * LANES, (p + 1) * LANES)
            prod = d_o[:, cols].astype(F32) * o_ref[:, cols].astype(F32)
            d_a = jnp.sum(jnp.where(halves[0], prod, 0.0), axis=1, keepdims=True)
            d_b = jnp.sum(jnp.where(halves[1], prod, 0.0), axis=1, keepdims=True)
            aug = jnp.zeros((tm, LANES), F32)
            for n, part in enumerate(_split3(jnp.where(halves[0], d_b, d_a))):
                aug = jnp.where(slot == SLOT_C + n, -part, aug)
            doaug_ref[:, cols] = aug.astype(BF16)

        @pl.when(pl.program_id(0) == S // tm - 1)
        def _():
            dwb_ref[...] = dw_ref[...].astype(BF16)

    rows = lambda width, col: pl.BlockSpec((tm, width), lambda i: (i, col))
    whole = pl.BlockSpec((D, D), lambda i: (0, 0))
    return pl.pallas_call(
        body, name="outproj_bwd", grid=(S // tm,),
        in_specs=[rows(D, 0), whole, rows(W, 0), rows(W, 0), rows(W, 1), rows(W, 5)] + _after_specs(after),
        out_specs=[whole, whole, rows(D, 0), rows(D, 0), rows(W, 0)],
        out_shape=[jax.ShapeDtypeStruct((D, D), F32),
                   jax.ShapeDtypeStruct((D, D), BF16),
                   jax.ShapeDtypeStruct((S, D), BF16),
                   jax.ShapeDtypeStruct((S, D), BF16),
                   jax.ShapeDtypeStruct((S, W), BF16)],
        compiler_params=_params("arbitrary"),
    )(g, wout, ypool, o, proj, proj, *after)


def _section_specs(sections, rows, width):
    specs = [pl.BlockSpec((rows, width), lambda k, c=c: (k, c)) for _, c in sections]
    return specs, [a for a, _ in sections]


def _inproj_bwd(sections, dzf, h=None, wxg=None, after=()):
    S = dzf.shape[0]
    n_sec = len(sections)
    D = h.shape[1] if h is not None else wxg[1].shape[1]
    W = D // 2
    N = n_sec * W
    ts = min(TM if wxg else 2 * TM, S)
    n_steps = S // ts
    once = pl.Buffered(1)

    def body(dz_ref, *rest):
        sec_refs, rest = rest[:n_sec], rest[n_sec:]
        if h is not None:
            h_ref, rest = rest[0], rest[1:]
        if wxg:
            (w_ref, wf_ref, x_ref, gam_ref, g_ref), rest = rest[:5], rest[5:]
        outs = rest[len(after):]
        step = pl.program_id(0)

        if h is not None:
            dw_ref, dwb_ref = outs[:2]

            @pl.when(step == 0)
            def _():
                dw_ref[...] = jnp.zeros(dw_ref.shape, F32)

            ht = h_ref[...].T
            dw_ref[:, N:N + LANES] += jnp.dot(ht, dz_ref[...], preferred_element_type=F32)
            for n, ref in enumerate(sec_refs):
                dw_ref[:, n * W:(n + 1) * W] += jnp.dot(ht, ref[...], preferred_element_type=F32)

            @pl.when(step == n_steps - 1)
            def _():
                dwb_ref[...] = dw_ref[...].astype(BF16)

        if wxg:
            dx_ref, dg_ref = outs[-2:]

            @pl.when(step == 0)
            def _():
                dg_ref[...] = jnp.zeros(dg_ref.shape, F32)

            dh = lax.dot_general(dz_ref[...], wf_ref[...], NT, preferred_element_type=F32)
            for n, ref in enumerate(sec_refs):
                dh = dh + lax.dot_general(ref[...], w_ref[:, n * W:(n + 1) * W], NT, preferred_element_type=F32)
            xf = x_ref[...]
            r = lax.rsqrt(jnp.mean(xf * xf, axis=-1, keepdims=True) + RMS_EPS)
            xhat = xf * r
            dg_ref[...] += jnp.sum(dh * xhat, axis=0, keepdims=True)
            dxhat = dh * gam_ref[...]
            dx_ref[...] = g_ref[...] + r * (dxhat - xhat * jnp.mean(dxhat * xhat, axis=-1, keepdims=True))

    rows = lambda width: pl.BlockSpec((ts, width), lambda k: (k, 0))
    sec_specs, operands = _section_specs(sections, ts, W)
    in_specs, out_specs, out_shape = [rows(LANES)] + sec_specs, [], []
    if h is not None:
        whole = pl.BlockSpec((D, N + LANES), lambda k: (0, 0), pipeline_mode=once)
        in_specs.append(rows(D))
        operands.append(h)
        out_specs += [whole, whole]
        out_shape += [jax.ShapeDtypeStruct((D, N + LANES), F32), jax.ShapeDtypeStruct((D, N + LANES), BF16)]
    if wxg:
        w, x, gam, g = wxg
        in_specs += [pl.BlockSpec((D, N), lambda k: (0, 0), pipeline_mode=once),
                     pl.BlockSpec((D, LANES), lambda k: (0, N // LANES), pipeline_mode=once),
                     rows(D), pl.BlockSpec((1, D), lambda k: (0, 0)), rows(D)]
        operands += [w, w, x, gam, g]
        out_specs += [rows(D), pl.BlockSpec((1, D), lambda k: (0, 0))]
        out_shape += [jax.ShapeDtypeStruct((S, D), F32), jax.ShapeDtypeStruct((1, D), F32)]
    name = "inproj_bwd" if h is not None and wxg else ("inproj_bwd_dw" if wxg is None else "inproj_bwd_dx")
    return pl.pallas_call(
        body, name=name, grid=(n_steps,),
        in_specs=in_specs + _after_specs(after), out_specs=out_specs, out_shape=out_shape,
        compiler_params=_params("arbitrary"),
    )(dzf, *operands, *after)


def _attn_bwd(proj, da, qaug, kaug, doaug):
    S = proj.shape[0]
    W = proj.shape[1] // 6
    P = W // LANES
    tq = min(TQ, S)
    nq = S // tq
    qc, kc, vc = 2 * P, 3 * P, 4 * P
    scale = 1.0 / math.sqrt(HEAD_DIM)

    def body(q_ref, k_ref, v_ref, do_ref, qa_ref, ka_ref, da_ref,
             dq_ref, dk_ref, dv_ref, drows_ref, dcols_ref, km_scr, vm_scr, dk_scr, dv_scr, dq_scr):
        pair, j = pl.program_id(0), pl.program_id(1)
        lane, halves = _head_halves(tq)

        @pl.when(jnp.logical_and(pair == 0, j == 0))
        def _():
            drows_ref[...] = jnp.zeros(drows_ref.shape, F32)
            dcols_ref[...] = jnp.zeros(dcols_ref.shape, F32)

        @pl.when(j == 0)
        def _():
            dq_scr[...] = jnp.zeros(dq_scr.shape, F32)

        v_ones = ((lane & (HEAD_DIM - 1)) < 3).astype(BF16)
        for a in range(2):
            km_scr[a] = jnp.where(halves[a], k_ref[...], ka_ref[...])
            vm_scr[a] = jnp.where(halves[a], v_ref[...], v_ones)
        dk_scr[...] = jnp.zeros(dk_scr.shape, F32)
        dv_scr[...] = jnp.zeros(dv_scr.shape, F32)

        def update(i, on_diagonal):
            rows = pl.ds(pl.multiple_of(i * tq, tq), tq)
            qs = q_ref[rows, :] * scale
            do2, qaug_t, doaug_t = do_ref[rows, :], qa_ref[rows, :], da_ref[rows, :]
            if on_diagonal:
                keep = (lax.broadcasted_iota(jnp.int32, (tq, tq), 0)
                        >= lax.broadcasted_iota(jnp.int32, (tq, tq), 1))
            qas = [jnp.where(halves[a], qs, qaug_t) for a in range(2)]
            logits = [lax.dot_general(qas[a], km_scr[a], NT, preferred_element_type=F32) for a in range(2)]
            dps = [lax.dot_general(jnp.where(halves[a], do2, doaug_t), vm_scr[a], NT, preferred_element_type=F32)
                   for a in range(2)]
            dv = None
            for a in range(2):
                s = jnp.where(keep, logits[a], NEG_INF) if on_diagonal else logits[a]
                p = jnp.exp(s)
                dsb = (p * dps[a]).astype(BF16)
                do0 = jnp.where(halves[a], do2, jnp.zeros_like(do2))
                dv_a = lax.dot_general(p.astype(BF16), do0, TN_DIMS, preferred_element_type=F32)
                dv = dv_a if dv is None else dv + dv_a
                dk_scr[a] += lax.dot_general(dsb, qas[a], TN_DIMS, preferred_element_type=F32)
                dq_scr[a, rows, :] += jnp.dot(dsb, km_scr[a], preferred_element_type=F32)
            dv_scr[...] += dv

        def below_diagonal(n, carry):
            update(j + 1 + 2 * n, False)
            update(j + 2 + 2 * n, False)
            return carry

        update(j, True)
        below = nq - 1 - j
        lax.fori_loop(0, below // 2, below_diagonal, 0)

        @pl.when(below % 2 == 1)
        def _():
            update(nq - 1, False)


        def to_head_lanes(old, first, second):
            at = lax.broadcasted_iota(jnp.int32, old.shape, 1) - 2 * pair
            return jnp.where(at == 0, first, jnp.where(at == 1, second, old))

        dk_ref[...] = jnp.where(halves[0], dk_scr[0], dk_scr[1]).astype(BF16)
        dv_ref[...] = dv_scr[...].astype(BF16)
        keys = pl.ds(pl.multiple_of(j * tq, tq), tq)
        ones_a, ones_b = HEAD_DIM + SLOT_ONE, SLOT_ONE
        dcols_ref[keys, :] = to_head_lanes(dcols_ref[keys, :], dk_scr[0][:, ones_a:ones_a + 1],
                                           dk_scr[1][:, ones_b:ones_b + 1])

        @pl.when(j == nq - 1)
        def _():
            row_lane, row_halves = _head_halves(S)
            dq_ref[...] = (jnp.where(row_halves[0], dq_scr[0], dq_scr[1]) * scale).astype(BF16)
            c_a, c_b = HEAD_DIM + SLOT_C, SLOT_C
            drows_ref[...] = to_head_lanes(drows_ref[...], dq_scr[0][:, c_a:c_a + 1], dq_scr[1][:, c_b:c_b + 1])

    tile = lambda col: pl.BlockSpec((tq, LANES), lambda p, j: (j, col + p))
    whole = lambda col: pl.BlockSpec((S, LANES), lambda p, j: (0, col + p))
    shared = pl.BlockSpec((S, LANES), lambda p, j: (0, 0))
    return pl.pallas_call(
        body, name="attn_bwd", grid=(P, nq),
        in_specs=[whole(qc), tile(kc), tile(vc), whole(P), whole(0), tile(0), whole(0)],
        out_specs=[whole(0), tile(0), tile(0), shared, shared],
        out_shape=[jax.ShapeDtypeStruct((S, W), BF16),
                   jax.ShapeDtypeStruct((S, W), BF16),
                   jax.ShapeDtypeStruct((S, W), BF16),
                   jax.ShapeDtypeStruct((S, LANES), F32),
                   jax.ShapeDtypeStruct((S, LANES), F32)],
        scratch_shapes=[pltpu.VMEM((2, tq, LANES), BF16),
                        pltpu.VMEM((2, tq, LANES), BF16),
                        pltpu.VMEM((2, tq, LANES), F32),
                        pltpu.VMEM((tq, LANES), F32),
                        pltpu.VMEM((2, S, LANES), F32)],
        compiler_params=_params("arbitrary", "arbitrary"),
    )(proj, proj, proj, da, qaug, kaug, doaug)


def _fgate_bwd(drows, dcols, z, bias):
    S = z.shape[0]
    tb = min(TB, S)
    nb = S // tb

    def body(drows_ref, dcols_ref, z_ref, b_ref, dz_ref, db_ref):
        tri = (lax.broadcasted_iota(jnp.int32, (tb, tb), 1)
               >= lax.broadcasted_iota(jnp.int32, (tb, tb), 0)).astype(F32)

        local = []
        for b in range(nb):
            rows = slice(b * tb, (b + 1) * tb)
            local.append(jnp.dot(tri, drows_ref[rows, :] - dcols_ref[rows, :], preferred_element_type=F32,
                                 precision=lax.Precision.HIGHEST))
        carry = jnp.zeros((1, LANES), F32)
        db = jnp.zeros((1, LANES), F32)
        for b in reversed(range(nb)):
            rows = slice(b * tb, (b + 1) * tb)
            rc = local[b] + carry
            carry = rc[0:1, :]
            dz = rc * _sigmoid(-(z_ref[rows, :] + b_ref[...]))
            dz_ref[rows, :] = dz.astype(BF16)
            db = db + jnp.sum(dz, axis=0, keepdims=True)
        db_ref[...] = db

    return pl.pallas_call(
        body, name="fgate_bwd",
        out_shape=[jax.ShapeDtypeStruct((S, LANES), BF16),
                   jax.ShapeDtypeStruct((1, LANES), F32)],
        compiler_params=pltpu.CompilerParams(vmem_limit_bytes=VMEM_LIMIT),
    )(drows, dcols, z, bias)


def _pool_bwd(proj, da, pool_w, pool_scale):
    S = proj.shape[0]
    G = len(POOL_WINDOWS)

    def body(u_ref, dy_ref, w_ref, s_ref, du_ref, dw_ref, ds_ref, pad_ref):
        g = pl.program_id(0)
        for gi, w in enumerate(POOL_WINDOWS):
            @pl.when(g == gi)
            def _():
                d, cnt = _window_mean_minus_self(u_ref[...].astype(F32), pad_ref, w, S)
                db = d.astype(BF16)
                wb = w_ref[0].astype(BF16)
                yraw = jnp.dot(db, wb, preferred_element_type=F32)
                dy = dy_ref[...].astype(F32)
                ds_ref[...] = jnp.sum(dy * yraw, axis=0, keepdims=True)
                dzb = (dy * s_ref[...]).astype(BF16)
                dw_ref[0] = lax.dot_general(db, dzb, TN_DIMS, preferred_element_type=F32)
                dd = lax.dot_general(dzb, wb, NT, preferred_element_type=F32)
                pad_ref[0:S, :] = dd / cnt
                pad_ref[S:S + MAX_WINDOW, :] = jnp.zeros((MAX_WINDOW, LANES), F32)
                acc = -dd
                for j in range(w):
                    acc = acc + pad_ref[j:j + S, :]
                du_ref[...] = acc.astype(BF16)

    return pl.pallas_call(
        body, name="pool_bwd", grid=(G,),
        in_specs=[pl.BlockSpec((S, LANES), lambda g: (0, g)),
                  pl.BlockSpec((S, LANES), lambda g: (0, g)),
                  pl.BlockSpec((1, LANES, LANES), lambda g: (g, 0, 0)),
                  pl.BlockSpec((1, LANES), lambda g: (0, g))],
        out_specs=[pl.BlockSpec((S, LANES), lambda g: (0, g)),
                   pl.BlockSpec((1, LANES, LANES), lambda g: (g, 0, 0)),
                   pl.BlockSpec((1, LANES), lambda g: (0, g))],
        out_shape=[jax.ShapeDtypeStruct((S, G * LANES), BF16),
                   jax.ShapeDtypeStruct((G, LANES, LANES), F32),
                   jax.ShapeDtypeStruct((1, G * LANES), F32)],
        scratch_shapes=[pltpu.VMEM((S + MAX_WINDOW, LANES), F32)],
        compiler_params=_params("arbitrary"),
    )(proj, da, pool_w, pool_scale)


def _adamw(w, m, v, gsets, name, rows, shifted=False, first=0, into=None):
    A, R, C = w.shape
    n_sets = len(gsets)
    tr = min(rows, R)
    c1 = 1.0 / (1.0 - ADAM_B1 ** ADAM_STEP)
    c2 = 1.0 / (1.0 - ADAM_B2 ** ADAM_STEP)
    counts = [len(gs) for gs in gsets]

    def body(w_ref, m_ref, v_ref, *rest):
        g_ref, d_ref, nm_ref, nv_ref = rest[-4:]
        at = 0
        for a in range(n_sets):
            part_refs = rest[at:at + counts[a]]
            at += counts[a]

            @pl.when(pl.program_id(0) == a)
            def _():
                g = None
                for ref in part_refs:
                    for s in range(ref.shape[0]):
                        term = ref[s].astype(F32)
                        g = term if g is None else g + term
                if shifted:
                    lanes = g.shape[1]
                    g = pltpu.roll(g, (lanes - _index(_position()) * (C % LANES)) % lanes, axis=1)[:, :C]
                nm = ADAM_B1 * m_ref[0] + (1.0 - ADAM_B1) * g
                nv = ADAM_B2 * v_ref[0] + (1.0 - ADAM_B2) * (g * g)
                g_ref[0] = g
                nm_ref[0] = nm
                nv_ref[0] = nv
                d_ref[0] = -ADAM_LR * ((nm * c1) / (jnp.sqrt(nv * c2) + ADAM_EPS) + ADAM_WD * w_ref[0])

    spec = pl.BlockSpec((1, tr, C), lambda a, r: (first + a, r, 0))
    part_specs = [pl.BlockSpec((part.shape[0], tr, part.shape[2]), lambda a, r, l=l: (0, jnp.where(a == l, r, 0), 0))
                  for l, gs in enumerate(gsets) for part in gs]
    parts = [part for gs in gsets for part in gs]
    shape = jax.ShapeDtypeStruct((A, R, C), F32)
    earlier = () if into is None else tuple(into)
    return pl.pallas_call(
        body, name=name, grid=(n_sets, R // tr),
        in_specs=[spec, spec, spec] + part_specs + _after_specs(earlier),
        out_specs=[spec, spec, spec, spec],
        out_shape=[shape, shape, shape, shape],
        input_output_aliases={3 + len(parts) + n: n for n in range(len(earlier))},
        compiler_params=_params("arbitrary", "arbitrary"),
    )(w, m, v, *parts, *earlier)


def _position():
    return lax.axis_index("x"), lax.axis_index("y"), lax.axis_index("c")


def _index(dev):
    return 4 * dev[0] + 2 * dev[1] + dev[2]


def _all_gather(arrs, slots, out_shapes, name):
    n_arr = len(arrs)

    def body(*refs):
        ins, outs = refs[:n_arr], refs[n_arr:2 * n_arr]
        send_sems, recv_sems, local_sems = refs[2 * n_arr:]
        x, y, c = _position()
        me, sibling = (x, y, c), (x, y, 1 - c)
        chips = [(1 - x, y), (x, 1 - y), (1 - x, 1 - y)]

        def copy(a, k, block, to, src=None):
            part = slots[a](outs[a], _index(block))
            return pltpu.make_async_remote_copy(
                src_ref=part if src is None else src, dst_ref=part,
                send_sem=send_sems.at[a, k], recv_sem=recv_sems.at[a, k],
                device_id=to, device_id_type=MESH)

        mine = [pltpu.make_async_copy(ins[a], slots[a](outs[a], _index(me)), local_sems.at[a])
                for a in range(n_arr)]
        for cp in mine:
            cp.start()
        first = []
        for a in range(n_arr):
            first.append(copy(a, 0, me, sibling, src=ins[a]))
            first += [copy(a, 1 + j, me, (*chip, c), src=ins[a]) for j, chip in enumerate(chips)]
        for cp in first:
            cp.start()
        passed = []
        for j, chip in enumerate(chips):
            for a in range(n_arr):
                copy(a, 1 + j, (*chip, c), me).wait_recv()
                fwd = copy(a, 4 + j, (*chip, c), sibling)
                fwd.start()
                passed.append(fwd)
        for a in range(n_arr):
            copy(a, 0, sibling, me).wait_recv()
            for j, chip in enumerate(chips):
                copy(a, 4 + j, (*chip, 1 - c), me).wait_recv()
        for cp in first + passed:
            cp.wait_send()
        for cp in mine:
            cp.wait()

    any_spec = pl.BlockSpec(memory_space=pl.ANY)
    return pl.pallas_call(
        body, name=name,
        in_specs=[any_spec] * n_arr, out_specs=[any_spec] * n_arr, out_shape=out_shapes,
        scratch_shapes=[pltpu.SemaphoreType.DMA((n_arr, 7)), pltpu.SemaphoreType.DMA((n_arr, 7)),
                        pltpu.SemaphoreType.DMA((n_arr,))],
    )(*arrs)


def _split_copies(srcs, lands, send_sems, recv_sems, kinds):
    x, y, c = _position()
    me = _index((x, y, c))
    copies = []
    for a, (src_part, land_part) in enumerate(kinds):
        for k in range(1, N_DEV):
            peer = (x ^ ((k >> 2) & 1), y ^ ((k >> 1) & 1), c ^ (k & 1))
            copies.append(pltpu.make_async_remote_copy(
                src_ref=src_part(srcs[a], _index(peer)), dst_ref=land_part(lands[a], me, k),
                send_sem=send_sems[a].at[k - 1], recv_sem=recv_sems[a].at[k - 1],
                device_id=peer, device_id_type=MESH))
    return copies


def _split_start(srcs, lands, kinds, name, after=()):
    n = len(srcs)

    def body(*refs):
        src_refs, land_refs = refs[:n], refs[n:2 * n]
        outs = refs[2 * n + len(after):]
        send_sems, recv_sems = outs[:n], outs[n:2 * n]
        token = outs[-1]
        for cp in _split_copies(src_refs, land_refs, send_sems, recv_sems, kinds):
            cp.start()
        token[...] = jnp.zeros(token.shape, token.dtype)

    hbm = pl.BlockSpec(memory_space=pltpu.HBM)
    sem = pl.BlockSpec(memory_space=pltpu.SEMAPHORE)
    operands = [pltpu.with_memory_space_constraint(t, pltpu.HBM) for t in (*srcs, *lands)]
    out = pl.pallas_call(
        body, name=name,
        in_specs=[hbm] * (2 * n) + _after_specs(after),
        out_specs=[sem] * (2 * n) + [hbm] * (2 * n) + [pl.BlockSpec(memory_space=pltpu.VMEM)],
        out_shape=[pltpu.SemaphoreType.DMA((N_DEV - 1,))] * (2 * n)
        + [pltpu.HBM(t.shape, t.dtype) for t in operands] + [jax.ShapeDtypeStruct((8, LANES), F32)],
        input_output_aliases={i: 2 * n + i for i in range(2 * n)},
        compiler_params=pltpu.CompilerParams(has_side_effects=pltpu.SideEffectType.DATAFLOW_SIDE_EFFECTING),
    )(*operands, *after)
    return [(out[a], out[n + a], out[2 * n + a], out[3 * n + a]) for a in range(n)], out[-1]


def _split_wait(started, kinds, after, name):
    n = len(started)
    sems = [t[0] for t in started] + [t[1] for t in started]
    srcs = [t[2] for t in started]
    lands = [t[3] for t in started]

    def body(*refs):
        src_refs, land_refs = refs[:n], refs[n:2 * n]
        send_sems, recv_sems = refs[2 * n:3 * n], refs[3 * n:4 * n]
        for cp in _split_copies(src_refs, land_refs, send_sems, recv_sems, kinds):
            cp.wait_send()
            cp.wait_recv()

    hbm = pl.BlockSpec(memory_space=pltpu.HBM)
    sem = pl.BlockSpec(memory_space=pltpu.SEMAPHORE)
    out = pl.pallas_call(
        body, name=name,
        in_specs=[hbm] * (2 * n) + [sem] * (2 * n) + _after_specs(after),
        out_specs=[hbm] * (2 * n),
        out_shape=[pltpu.HBM(t.shape, t.dtype) for t in (*srcs, *lands)],
        input_output_aliases={i: i for i in range(2 * n)},
        compiler_params=pltpu.CompilerParams(has_side_effects=pltpu.SideEffectType.DATAFLOW_SIDE_EFFECTING),
    )(*srcs, *lands, *sems, *after)
    return out[n:]


def _as_rows(p):
    if p.size % LANES == 0:
        rows = p.reshape(-1, LANES)
    else:
        rows = p.reshape(-1, p.shape[-1])
        rows = jnp.pad(rows, ((0, 0), (0, LANES - rows.shape[1])))
    return jnp.pad(rows, ((0, -rows.shape[0] % 8), (0, 0)))


def _pack(parts):
    return jnp.concatenate([_as_rows(p) for p in parts])[None]


def _unpack(packed, like):
    out, at = [], 0
    for p in like:
        whole = p.size % LANES == 0
        n = p.size // LANES if whole else p.size // p.shape[-1]
        rows = packed[0, at:at + n]
        out.append((rows if whole else rows[:, :p.shape[-1]]).reshape(p.shape))
        at += n + (-n % 8)
    return out


def _local_step(x, target, norm_g, forget_bias, pool_w, pool_scale, final_g, weights_in, weights_out, on_grads,
                first_after=()):
    L = norm_g.shape[0]
    S, D = x.shape
    W = D // 2
    H = W // HEAD_DIM
    bias = jnp.pad(forget_bias, ((0, 0), (0, LANES - H)))

    saved = []
    after = tuple(first_after)
    for l in range(L):
        proj, h, z, w = _inproj_fwd(x, norm_g[l:l + 1], weights_in(l, x), after)
        after = ()
        qaug, kaug = _fgate_fwd(z, bias[l:l + 1], H)
        ypool = _pool_fwd(proj, pool_w[l], pool_scale[l:l + 1])
        o, qaug_b = _attn_fwd(proj, qaug, kaug)
        wout = weights_out(l, o)
        x_in = x
        if l < L - 1:
            (x,) = _outproj_fwd(ypool, o, proj, x_in, wout)
        else:
            g, loss, d_final_g = _outproj_fwd(ypool, o, proj, x_in, wout, (final_g.reshape(1, D), target))
        saved.append((x_in, proj, h, z, qaug_b, kaug, ypool, o, w, wout))

    small, after = None, ()
    for l in reversed(range(L)):
        x_in, proj, h, z, qaug_b, kaug, ypool, o, w, wout = saved[l]
        d_wout, d_wout_bf16, da, dgate, doaug = _outproj_bwd(g, wout, ypool, o, proj, after)
        dq, dk, dv, drows, dcols = _attn_bwd(proj, da, qaug_b, kaug, doaug)
        dzf, db = _fgate_bwd(drows, dcols, z, bias[l:l + 1])
        dpu, dpw, dps = _pool_bwd(proj, da, pool_w[l], pool_scale[l:l + 1])
        dproj = [(dpu, 0), (dgate, 0), (dq, 0), (dk, 0), (dv, 0), (dgate, 1)]
        wxg = (w, x_in, norm_g[l:l + 1], g)
        if l > 0:
            d_w, d_w_bf16, g, dgam = _inproj_bwd(dproj, dzf, h, wxg)
            after = tuple(on_grads(l, d_w, d_w_bf16, d_wout, d_wout_bf16, small))
        else:
            d_w, d_w_bf16 = _inproj_bwd(dproj, dzf, h)
            after = tuple(on_grads(l, d_w, d_w_bf16, d_wout, d_wout_bf16, small))
            g, dgam = _inproj_bwd(dproj, dzf, None, wxg, after)
        small = (dgam[0], db[0, :H], dpw, dps[0])
    return loss[0, 0], g, small, d_final_g[0]


def kernel(x, norm_g, w_in, forget_bias, pool_w, pool_scale, w_out, final_g, loss_target, m_norm_g, m_w_in, m_forget_bias, m_pool_w, m_pool_scale, m_w_out, m_final_g, v_norm_g, v_w_in, v_forget_bias, v_pool_w, v_pool_scale, v_w_out, v_final_g):
    L, D, cols = w_in.shape
    rows_out = w_out.shape[1]
    me = _index(_position())
    slot = _slot_width(cols)
    wout_b = w_out.astype(BF16)
    win_b = _shift_slots(w_in)
    gather_in = (lambda ref, peer: ref, lambda ref, mine, k: ref.at[mine])
    gather_out = (lambda ref, peer: ref, lambda ref, mine, k: ref.at[pl.ds(mine * rows_out, rows_out), :])

    def landing(block, n_slots):
        zone = lax.empty((n_slots * block.shape[0], *block.shape[1:]), block.dtype)
        return lax.dynamic_update_slice(zone, block, (me * block.shape[0],) + (0,) * (block.ndim - 1))

    (first_in,) = _all_gather([win_b[0]], [lambda ref, n: ref.at[n]],
                              [jax.ShapeDtypeStruct((N_DEV, D, slot), BF16)], "gather_first")
    rest_srcs = [wout_b[0]] + [w[l] for l in range(1, L) for w in (win_b, wout_b)]
    rest_lands = [landing(wout_b[0], N_DEV)]
    for l in range(1, L):
        rest_lands += [landing(win_b[l][None], N_DEV), landing(wout_b[l], N_DEV)]
    rest_kinds = [gather_out] + [gather_in, gather_out] * (L - 1)
    rest, rest_token = _split_start(rest_srcs, rest_lands, rest_kinds, "gather_start_rest", (first_in,))

    def weights_in(l, x_in):
        if l == 0:
            return first_in
        (win_all,) = _split_wait([rest[2 * l - 1]], [gather_in], (x_in,), f"gather_wait_in_{l}")
        return win_all

    def weights_out(l, o):
        (wout_full,) = _split_wait([rest[2 * l]], [gather_out], (o,), f"gather_wait_out_{l}")
        return wout_full

    stride = slot - LANES
    exchange_kinds = [(lambda ref, peer: ref.at[:, pl.ds(pl.multiple_of(peer * stride, LANES), slot)],
                       lambda ref, mine, k: ref.at[k - 1]),
                      (lambda ref, peer: ref.at[pl.ds(peer * rows_out, rows_out), :],
                       lambda ref, mine, k: ref.at[k - 1])]
    zero_g = jnp.zeros_like(final_g)
    zero_loss = jnp.zeros((LANES,), F32)

    def small_pack(l, norm_g_l, bias_l, pool_w_l, pool_scale_l, final, loss_row=None):
        return _pack([norm_g_l, bias_l, pool_w_l, pool_scale_l, final if l == 0 else zero_g,
                      zero_loss if loss_row is None else loss_row])[0]

    exchanges, own_parts = {}, {}

    def on_grads(l, dw, dw_bf16, d_wout, d_wout_bf16, small):
        own_parts[l] = (lax.dynamic_slice_in_dim(dw, me * stride, slot, 1)[None],
                        lax.dynamic_slice_in_dim(d_wout, me * rows_out, rows_out, 0)[None])
        srcs = [dw_bf16, d_wout_bf16]
        lands = [lax.empty((N_DEV - 1, D, slot), BF16), lax.empty((N_DEV - 1, rows_out, D), BF16)]
        kinds = list(exchange_kinds)
        if small is not None:
            packed_small = small_pack(l + 1, *small, None)
            srcs.append(packed_small)
            lands.append(landing(packed_small[None], N_DEV))
            kinds.append(gather_in)
        started, token = _split_start(srcs, lands, kinds, f"exchange_start_{l}")
        exchanges[l] = (started, kinds)
        return (token,)

    loss, dx, small_first, d_final_g = _local_step(
        x[0], loss_target[0], norm_g, forget_bias, pool_w, pool_scale, final_g,
        weights_in, weights_out, on_grads, (rest_token,))
    packed_first = small_pack(0, *small_first, d_final_g, jnp.full((LANES,), loss, F32))
    first_started, first_token = _split_start(
        [packed_first], [landing(packed_first[None], N_DEV)], [gather_in],
        "small_start", (w_in, m_w_in, v_w_in, *own_parts[0]))

    gin_sets, gout_sets, small_sets = [None] * L, [None] * L, [None] * L

    def wait_for(l, after):
        started, kinds = exchanges[l]
        got = _split_wait(started, kinds, after, f"exchange_wait_{l}")
        gin_sets[l] = [own_parts[l][0], got[0]]
        gout_sets[l] = [own_parts[l][1], got[1]]
        if len(got) > 2:
            small_sets[l + 1] = [got[2]]

    for l in range(1, L):
        wait_for(l, (dx, first_token))
    rest_in = _adamw(w_in, m_w_in, v_w_in, gin_sets[1:], "adamw_w_in_rest", TM // 2, shifted=True, first=1)
    rest_out = _adamw(w_out, m_w_out, v_w_out, gout_sets[1:], "adamw_w_out_rest", rows_out, first=1)
    wait_for(0, (rest_in[1], rest_out[1]))
    g_w_in, d_w_in, nm_w_in, nv_w_in = _adamw(w_in, m_w_in, v_w_in, gin_sets[:1], "adamw_w_in_first", TM // 2,
                                              shifted=True, into=rest_in)
    g_w_out, d_w_out, nm_w_out, nv_w_out = _adamw(w_out, m_w_out, v_w_out, gout_sets[:1], "adamw_w_out_first",
                                                  rows_out, into=rest_out)
    small_sets[0] = _split_wait(first_started, [gather_in], (d_w_in, d_w_out), "small_wait")
    loss = jnp.sum(small_sets[0][0][:, packed_first.shape[0] - 8, 0])

    def small_stack(norm_g_, bias_, pool_w_, pool_scale_, final):
        return jnp.stack([small_pack(l, norm_g_[l], bias_[l], pool_w_[l], pool_scale_[l], final) for l in range(L)])

    packed = _adamw(small_stack(norm_g, forget_bias, pool_w, pool_scale, final_g),
                    small_stack(m_norm_g, m_forget_bias, m_pool_w, m_pool_scale, m_final_g),
                    small_stack(v_norm_g, v_forget_bias, v_pool_w, v_pool_scale, v_final_g),
                    small_sets, "adamw_small", packed_first.shape[0])

    def small_unpack(p):
        like = [norm_g[0], forget_bias[0], pool_w[0], pool_scale[0], final_g]
        layers = [_unpack(p[l:l + 1], like) for l in range(L)]
        return [jnp.stack([layers[l][n] for l in range(L)]) for n in range(4)] + [layers[0][4]]

    g_s, d_s, nm_s, nv_s = [small_unpack(p) for p in packed]

    def order(big_in, big_out, small):
        return (small[0], big_in, small[1], small[2], small[3], big_out, small[4])

    return (loss, dx[None], *order(g_w_in, g_w_out, g_s), *order(d_w_in, d_w_out, d_s),
            *order(nm_w_in, nm_w_out, nm_s), *order(nv_w_in, nv_w_out, nv_s))
```

```python
import math

import jax
import jax.numpy as jnp
from jax import lax
from jax.experimental import pallas as pl
from jax.experimental.pallas import tpu as pltpu

F32 = jnp.float32
BF16 = jnp.bfloat16
MESH = pl.DeviceIdType.MESH

RMS_EPS = 1e-6
NEG_INF = -1e30
HEAD_DIM = 64
POOL_WINDOWS = (2, 4, 8, 16)
MAX_WINDOW = 16
LANES = 128
N_DEV = 8

ADAM_LR = 0.001
ADAM_B1 = 0.9
ADAM_B2 = 0.999
ADAM_EPS = 1e-08
ADAM_WD = 0.01
ADAM_STEP = 10

TM = 512
TN = 512
TQ = 512
TB = 256
VMEM_LIMIT = 56 * 1024 * 1024

NT = (((1,), (1,)), ((), ()))
TN_DIMS = (((0,), (0,)), ((), ()))

SLOT_C, SLOT_ONE, SLOT_LSE = 0, 3, 6


def _params(*sem):
    return pltpu.CompilerParams(dimension_semantics=sem, vmem_limit_bytes=VMEM_LIMIT)


def _sigmoid(x):
    return 1.0 / (1.0 + jnp.exp(-x))


def _split3(x):
    hi = x.astype(BF16).astype(F32)
    rest = x - hi
    mid = rest.astype(BF16).astype(F32)
    return hi, mid, rest - mid


def _after_specs(after):
    return [pl.BlockSpec(memory_space=pl.ANY)] * len(after)


def _slot_width(cols):
    return LANES * (-(-(cols + (N_DEV - 1) * (cols % LANES)) // LANES))


def _shift_slots(w_in):
    L, D, cols = w_in.shape
    slot = _slot_width(cols)
    tr = min(TM // 2, D)

    def body(w_ref, o_ref, pad_scr):
        pad_scr[...] = jnp.zeros(pad_scr.shape, F32)
        pad_scr[:, 0:cols] = w_ref[0]
        o_ref[0] = pltpu.roll(pad_scr[...], _index(_position()) * (cols % LANES), axis=1).astype(BF16)

    return pl.pallas_call(
        body, name="shift_slots", grid=(L, D // tr),
        in_specs=[pl.BlockSpec((1, tr, cols), lambda l, r: (l, r, 0))],
        out_specs=pl.BlockSpec((1, tr, slot), lambda l, r: (l, r, 0)),
        out_shape=jax.ShapeDtypeStruct((L, D, slot), BF16),
        scratch_shapes=[pltpu.VMEM((tr, slot), F32)],
        compiler_params=_params("parallel", "parallel"),
    )(w_in)


def _inproj_tile(xf, g_ref, s_ref, proj_ref, h_ref, z_ref, w_ref):
    n_dev, _, sw = s_ref.shape
    stride = sw - LANES
    width = w_ref.shape[1]
    N = width - LANES

    @pl.when(pl.program_id(0) == 0)
    def _():
        for n in range(n_dev):
            base = stride * n
            first = s_ref[n, :, 0:LANES]
            if n > 0:
                first = first + s_ref[n - 1, :, stride:sw]
            w_ref[:, base:base + LANES] = first
            w_ref[:, base + LANES:base + stride] = s_ref[n, :, LANES:stride]
        w_ref[:, stride * n_dev:width] = s_ref[n_dev - 1, :, stride:sw]

    r = lax.rsqrt(jnp.mean(xf * xf, axis=-1, keepdims=True) + RMS_EPS)
    h = ((xf * r) * g_ref[...]).astype(BF16)
    h_ref[...] = h
    z_ref[...] = jnp.dot(h, w_ref[:, N:width], preferred_element_type=F32)
    for n in range(N // TN):
        cols = slice(n * TN, (n + 1) * TN)
        proj_ref[:, cols] = jnp.dot(h, w_ref[:, cols], preferred_element_type=F32).astype(BF16)


def _inproj_specs(slots, S, D, tm):
    n_dev, _, sw = slots.shape
    width = (sw - LANES) * n_dev + LANES
    N = width - LANES
    once = pl.Buffered(1)
    in_specs = [pl.BlockSpec((1, D), lambda i: (0, 0)),
                pl.BlockSpec((n_dev, D, sw), lambda i: (0, 0, 0), pipeline_mode=once)]
    out_specs = [pl.BlockSpec((tm, N), lambda i: (i, 0)),
                 pl.BlockSpec((tm, D), lambda i: (i, 0)),
                 pl.BlockSpec((tm, LANES), lambda i: (i, 0)),
                 pl.BlockSpec((D, width), lambda i: (0, 0), pipeline_mode=once)]
    out_shape = [jax.ShapeDtypeStruct((S, N), BF16),
                 jax.ShapeDtypeStruct((S, D), BF16),
                 jax.ShapeDtypeStruct((S, LANES), F32),
                 jax.ShapeDtypeStruct((D, width), BF16)]
    return in_specs, out_specs, out_shape


def _inproj_fwd(x, gam, slots, after=()):
    S, D = x.shape
    tm = min(TM, S)

    def body(x_ref, g_ref, s_ref, *rest):
        _inproj_tile(x_ref[...], g_ref, s_ref, *rest[-4:])

    in_specs, out_specs, out_shape = _inproj_specs(slots, S, D, tm)
    return pl.pallas_call(
        body, name="inproj_fwd", grid=(S // tm,),
        in_specs=[pl.BlockSpec((tm, D), lambda i: (i, 0))] + in_specs + _after_specs(after),
        out_specs=out_specs, out_shape=out_shape,
        compiler_params=_params("arbitrary"),
    )(x, gam, slots, *after)


def _fgate_fwd(z, bias, n_heads):
    S = z.shape[0]
    tb = min(TB, S)
    P = n_heads // 2
    assert n_heads <= 8, "the three parts of c are packed eight lanes apart"

    def body(z_ref, b_ref, qaug_ref, kaug_ref):
        lane = lax.broadcasted_iota(jnp.int32, (tb, LANES), 1)
        tri = (lax.broadcasted_iota(jnp.int32, (tb, tb), 0)
               >= lax.broadcasted_iota(jnp.int32, (tb, tb), 1)).astype(F32)
        row = lax.broadcasted_iota(jnp.int32, (LANES, P * LANES), 0)
        col = lax.broadcasted_iota(jnp.int32, (LANES, P * LANES), 1)
        head, part_n = row & 7, row >> 3
        home = (head >> 1) * LANES + jnp.where((head & 1) == 0, HEAD_DIM, 0)
        is_part = jnp.logical_and(head < n_heads, part_n < 3)
        place_q = jnp.logical_and(is_part, col == home + SLOT_C + part_n).astype(BF16)
        place_k = jnp.logical_and(is_part, col == home + SLOT_ONE + part_n).astype(BF16)
        slot = lax.broadcasted_iota(jnp.int32, (tb, P * LANES), 1) & (HEAD_DIM - 1)
        q_ones = jnp.logical_and(slot >= SLOT_ONE, slot < SLOT_ONE + 3).astype(F32)
        k_ones = jnp.logical_or(slot < SLOT_C + 3,
                                jnp.logical_and(slot >= SLOT_LSE, slot < SLOT_LSE + 3)).astype(F32)

        local = []
        for b in range(S // tb):
            zz = z_ref[b * tb:(b + 1) * tb, :] + b_ref[...]
            lf = jnp.minimum(zz, 0.0) - jnp.log(1.0 + jnp.exp(-jnp.abs(zz)))
            lf = jnp.where(lane < n_heads, lf, 0.0)
            local.append(jnp.dot(tri, lf, preferred_element_type=F32, precision=lax.Precision.HIGHEST))
        carry = jnp.zeros((1, LANES), F32)
        for b, part_sum in enumerate(local):
            c = part_sum + carry
            carry = c[tb - 1:tb, :]
            hi, mid, lo = _split3(c)
            packed = (hi + pltpu.roll(mid, 8, axis=1) + pltpu.roll(lo, 16, axis=1)).astype(BF16)
            qaug_ref[b * tb:(b + 1) * tb, :] = (
                q_ones + jnp.dot(packed, place_q, preferred_element_type=F32)).astype(BF16)
            kaug_ref[b * tb:(b + 1) * tb, :] = (
                k_ones - jnp.dot(packed, place_k, preferred_element_type=F32)).astype(BF16)

    return pl.pallas_call(
        body, name="fgate_fwd",
        out_shape=[jax.ShapeDtypeStruct((S, P * LANES), BF16),
                   jax.ShapeDtypeStruct((S, P * LANES), BF16)],
        compiler_params=pltpu.CompilerParams(vmem_limit_bytes=VMEM_LIMIT),
    )(z, bias)


def _window_mean_minus_self(u, pad_ref, w, S):
    pad_ref[0:MAX_WINDOW, :] = jnp.zeros((MAX_WINDOW, LANES), F32)
    pad_ref[MAX_WINDOW:MAX_WINDOW + S, :] = u
    acc = u
    for j in range(1, w):
        acc = acc + pad_ref[MAX_WINDOW - j:MAX_WINDOW - j + S, :]
    t = lax.broadcasted_iota(jnp.int32, (S, LANES), 0)
    cnt = jnp.minimum(t + 1, w).astype(F32)
    return acc / cnt - u, cnt


def _pool_fwd(proj, pool_w, pool_scale):
    S = proj.shape[0]
    G = len(POOL_WINDOWS)

    def body(u_ref, w_ref, s_ref, y_ref, pad_ref):
        g = pl.program_id(0)
        for gi, w in enumerate(POOL_WINDOWS):
            @pl.when(g == gi)
            def _():
                d, _ = _window_mean_minus_self(u_ref[...].astype(F32), pad_ref, w, S)
                y = jnp.dot(d.astype(BF16), w_ref[0].astype(BF16), preferred_element_type=F32)
                y_ref[...] = (y * s_ref[...]).astype(BF16)

    return pl.pallas_call(
        body, name="pool_fwd", grid=(G,),
        in_specs=[pl.BlockSpec((S, LANES), lambda g: (0, g)),
                  pl.BlockSpec((1, LANES, LANES), lambda g: (g, 0, 0)),
                  pl.BlockSpec((1, LANES), lambda g: (0, g))],
        out_specs=pl.BlockSpec((S, LANES), lambda g: (0, g)),
        out_shape=jax.ShapeDtypeStruct((S, G * LANES), BF16),
        scratch_shapes=[pltpu.VMEM((S + MAX_WINDOW, LANES), F32)],
        compiler_params=_params("arbitrary"),
    )(proj, pool_w, pool_scale)


def _head_halves(rows):
    lane = lax.broadcasted_iota(jnp.int32, (rows, LANES), 1)
    return lane, (lane < HEAD_DIM, lane >= HEAD_DIM)


def _attn_fwd(proj, qaug, kaug):
    S = proj.shape[0]
    W = proj.shape[1] // 6
    P = W // LANES
    tk = min(TQ, S // 2)
    tq = 2 * tk
    nq = S // tq
    qc, kc, vc = 2 * P, 3 * P, 4 * P
    scale = 1.0 / math.sqrt(HEAD_DIM)

    def body(q_ref, k_ref, v_ref, qa_ref, ka_ref, o_ref, qb_ref, qm_scr, m_scr, acc_scr):
        i = pl.program_id(1)
        lane, halves = _head_halves(tq)
        key_halves = (halves[0][:tk], halves[1][:tk])
        v_ones = ((lane[:tk] & (HEAD_DIM - 1)) < 3).astype(BF16)
        qs = q_ref[...] * scale
        qm_scr[0] = jnp.where(halves[0], qs, qa_ref[...])
        qm_scr[1] = jnp.where(halves[1], qs, qa_ref[...])
        m_scr[...] = jnp.full(m_scr.shape, NEG_INF, F32)
        acc_scr[...] = jnp.zeros(acc_scr.shape, F32)
        top, bottom, both = slice(0, tk), slice(tk, tq), slice(0, tq)

        def update(rows, j, on_diagonal):
            keys = pl.ds(pl.multiple_of(j * tk, tk), tk)
            k2, v2, kaug_t = k_ref[keys, :], v_ref[keys, :], ka_ref[keys, :]
            if on_diagonal:
                keep = (lax.broadcasted_iota(jnp.int32, (tk, tk), 0)
                        >= lax.broadcasted_iota(jnp.int32, (tk, tk), 1))
            logits = [lax.dot_general(qm_scr[a, rows, :], jnp.where(key_halves[a], k2, kaug_t), NT,
                                      preferred_element_type=F32) for a in range(2)]
            for a in range(2):
                s = jnp.where(keep, logits[a], NEG_INF) if on_diagonal else logits[a]
                va = jnp.where(key_halves[a], v2, v_ones)
                m_prev = m_scr[a, rows, :]
                m_new = jnp.maximum(m_prev, jnp.max(s, axis=1, keepdims=True))
                p = jnp.exp(s - jnp.tile(m_new, (1, tk // LANES)))
                acc_scr[a, rows, :] = (jnp.exp(m_prev - m_new) * acc_scr[a, rows, :]
                                       + jnp.dot(p.astype(BF16), va, preferred_element_type=F32))
                m_scr[a, rows, :] = m_new

        def below_diagonal(jj, carry):
            update(both, 2 * jj, False)
            update(both, 2 * jj + 1, False)
            return carry

        lax.fori_loop(0, i, below_diagonal, 0)
        update(top, 2 * i, True)
        update(bottom, 2 * i, False)
        update(bottom, 2 * i + 1, True)
        acc_a, acc_b = acc_scr[0], acc_scr[1]
        l_a, l_b = acc_a[:, HEAD_DIM:HEAD_DIM + 1], acc_b[:, 0:1]
        o_ref[...] = jnp.where(halves[0], acc_a / l_a, acc_b / l_b).astype(BF16)
        lse = jnp.where(halves[0], m_scr[1] + jnp.log(l_b), m_scr[0] + jnp.log(l_a))
        slot = lane & (HEAD_DIM - 1)
        aug = qa_ref[...].astype(F32)
        for n, part in enumerate(_split3(lse)):
            aug = jnp.where(slot == SLOT_LSE + n, -part, aug)
        qb_ref[...] = aug.astype(BF16)

    tile = lambda col: pl.BlockSpec((tq, LANES), lambda p, i: (i, col + p))
    whole = lambda col: pl.BlockSpec((S, LANES), lambda p, i: (0, col + p))
    return pl.pallas_call(
        body, name="attn_fwd", grid=(P, nq),
        in_specs=[tile(qc), whole(kc), whole(vc), tile(0), whole(0)],
        out_specs=[tile(0), tile(0)],
        out_shape=[jax.ShapeDtypeStruct((S, W), BF16), jax.ShapeDtypeStruct((S, W), BF16)],
        scratch_shapes=[pltpu.VMEM((2, tq, LANES), BF16),
                        pltpu.VMEM((2, tq, LANES), F32),
                        pltpu.VMEM((2, tq, LANES), F32)],
        compiler_params=_params("parallel", "arbitrary"),
    )(proj, proj, proj, qaug, kaug)


def _outproj_fwd(ypool, o, proj, x, wout, nxt=None, head=None):
    S, D = x.shape
    W = D // 2
    tm, tn = min(TM, S), TN

    def body(y_ref, o_ref, pg_ref, ag_ref, x_ref, w_ref, *rest):
        mix_ref = rest[-1]
        out_ref = rest[2]
        pg, ag = pg_ref[...].astype(F32), ag_ref[...].astype(F32)
        mix_ref[:, 0:W] = (y_ref[...].astype(F32) * (pg * _sigmoid(pg))).astype(BF16)
        mix_ref[:, W:D] = (o_ref[...].astype(F32) * (ag * _sigmoid(ag))).astype(BF16)
        for n in range(D // tn):
            cols = slice(n * tn, (n + 1) * tn)
            out_ref[:, cols] = x_ref[:, cols] + jnp.dot(mix_ref[...], w_ref[:, cols], preferred_element_type=F32)
        if nxt:
            _inproj_tile(out_ref[...], rest[0], rest[1], *rest[3:7])
            return
        gam_ref, t_ref = rest[:2]
        loss_ref, dg_ref = rest[-3:-1]

        @pl.when(pl.program_id(0) == 0)
        def _():
            loss_ref[...] = jnp.zeros(loss_ref.shape, F32)
            dg_ref[...] = jnp.zeros(dg_ref.shape, F32)

        xf, gam_v = out_ref[...], gam_ref[...]
        r = lax.rsqrt(jnp.mean(xf * xf, axis=-1, keepdims=True) + RMS_EPS)
        xhat = xf * r
        err = xhat * gam_v - t_ref[...]
        part = jnp.sum(jnp.sum(err * err, axis=-1, keepdims=True), axis=0, keepdims=True)
        loss_ref[...] += part * (0.5 / D)
        dy = err * (1.0 / D)
        dg_ref[...] += jnp.sum(dy * xhat, axis=0, keepdims=True)
        dxhat = dy * gam_v
        out_ref[...] = r * (dxhat - xhat * jnp.mean(dxhat * xhat, axis=-1, keepdims=True))

    assert (nxt is None) != (head is None)
    rows = lambda width, col: pl.BlockSpec((tm, width), lambda i: (i, col))
    in_specs = [rows(W, 0), rows(W, 0), rows(W, 1), rows(W, 5), rows(D, 0),
                pl.BlockSpec((D, D), lambda i: (0, 0), pipeline_mode=pl.Buffered(1))]
    out_specs = [rows(D, 0)]
    out_shape = [jax.ShapeDtypeStruct((S, D), F32)]
    if nxt:
        more_in, more_out, more_shape = _inproj_specs(nxt[1], S, D, tm)
        in_specs += more_in
        out_specs += more_out
        out_shape += more_shape
    else:
        in_specs += [pl.BlockSpec((1, D), lambda i: (0, 0)), rows(D, 0)]
        out_specs += [pl.BlockSpec((8, LANES), lambda i: (0, 0)), pl.BlockSpec((1, D), lambda i: (0, 0))]
        out_shape += [jax.ShapeDtypeStruct((8, LANES), F32), jax.ShapeDtypeStruct((1, D), F32)]
    return pl.pallas_call(
        body, name="outproj_fwd_loss" if head else "outproj_inproj_fwd", grid=(S // tm,),
        in_specs=in_specs, out_specs=out_specs, out_shape=out_shape,
        scratch_shapes=[pltpu.VMEM((tm, D), BF16)],
        compiler_params=_params("arbitrary"),
    )(ypool, o, proj, proj, x, wout, *(nxt or head))


def _outproj_bwd(g, wout, ypool, o, proj, after=()):
    S, D = g.shape
    W = D // 2
    tm = min(TM, S)

    def body(g_ref, w_ref, y_ref, o_ref, pg_ref, ag_ref, *rest):
        dw_ref, dwb_ref, da_ref, dgate_ref, doaug_ref = rest[-5:]

        @pl.when(pl.program_id(0) == 0)
        def _():
            dw_ref[...] = jnp.zeros(dw_ref.shape, F32)

        gb = g_ref[...].astype(BF16)
        dmixes = [lax.dot_general(gb, w_ref[half * W:(half + 1) * W, :], NT, preferred_element_type=F32)
                  for half in range(2)]
        d_o = None
        for half, (val_ref, gate_ref) in enumerate(((y_ref, pg_ref), (o_ref, ag_ref))):
            cols = slice(half * W, (half + 1) * W)
            gt = gate_ref[...].astype(F32)
            sg = _sigmoid(gt)
            silu = gt * sg
            val = val_ref[...].astype(F32)
            dw_ref[cols, :] += lax.dot_general((val * silu).astype(BF16), gb, TN_DIMS, preferred_element_type=F32)
            d_o = (dmixes[half] * silu).astype(BF16)
            da_ref[:, cols] = d_o
            dgate_ref[:, cols] = (dmixes[half] * val * (sg * (1.0 + gt * (1.0 - sg)))).astype(BF16)

        lane, halves = _head_halves(tm)
        slot = lane & (HEAD_DIM - 1)
        for p in range(W // LANES):
            cols = slice(p * LANES, (p + 1) * LANES)
            prod = d_o[:, cols].astype(F32) * o_ref[:, cols].astype(F32)
            d_a = jnp.sum(jnp.where(halves[0], prod, 0.0), axis=1, keepdims=True)
            d_b = jnp.sum(jnp.where(halves[1], prod, 0.0), axis=1, keepdims=True)
            aug = jnp.zeros((tm, LANES), F32)
            for n, part in enumerate(_split3(jnp.where(halves[0], d_b, d_a))):
                aug = jnp.where(slot == SLOT_C + n, -part, aug)
            doaug_ref[:, cols] = aug.astype(BF16)

        @pl.when(pl.program_id(0) == S // tm - 1)
        def _():
            dwb_ref[...] = dw_ref[...].astype(BF16)

    rows = lambda width, col: pl.BlockSpec((tm, width), lambda i: (i, col))
    whole = pl.BlockSpec((D, D), lambda i: (0, 0))
    return pl.pallas_call(
        body, name="outproj_bwd", grid=(S // tm,),
        in_specs=[rows(D, 0), whole, rows(W, 0), rows(W, 0), rows(W, 1), rows(W, 5)] + _after_specs(after),
        out_specs=[whole, whole, rows(D, 0), rows(D, 0), rows(W, 0)],
        out_shape=[jax.ShapeDtypeStruct((D, D), F32),
                   jax.ShapeDtypeStruct((D, D), BF16),
                   jax.ShapeDtypeStruct((S, D), BF16),
                   jax.ShapeDtypeStruct((S, D), BF16),
                   jax.ShapeDtypeStruct((S, W), BF16)],
        compiler_params=_params("arbitrary"),
    )(g, wout, ypool, o, proj, proj, *after)


def _section_specs(sections, rows, width):
    specs = [pl.BlockSpec((rows, width), lambda k, c=c: (k, c)) for _, c in sections]
    return specs, [a for a, _ in sections]


def _inproj_bwd(sections, dzf, h=None, wxg=None, after=()):
    S = dzf.shape[0]
    n_sec = len(sections)
    D = h.shape[1] if h is not None else wxg[1].shape[1]
    W = D // 2
    N = n_sec * W
    ts = min(TM if wxg else 2 * TM, S)
    n_steps = S // ts
    once = pl.Buffered(1)

    def body(dz_ref, *rest):
        sec_refs, rest = rest[:n_sec], rest[n_sec:]
        if h is not None:
            h_ref, rest = rest[0], rest[1:]
        if wxg:
            (w_ref, wf_ref, x_ref, gam_ref, g_ref), rest = rest[:5], rest[5:]
        outs = rest[len(after):]
        step = pl.program_id(0)

        if h is not None:
            dw_ref, dwb_ref = outs[:2]

            @pl.when(step == 0)
            def _():
                dw_ref[...] = jnp.zeros(dw_ref.shape, F32)

            ht = h_ref[...].T
            dw_ref[:, N:N + LANES] += jnp.dot(ht, dz_ref[...], preferred_element_type=F32)
            for n, ref in enumerate(sec_refs):
                dw_ref[:, n * W:(n + 1) * W] += jnp.dot(ht, ref[...], preferred_element_type=F32)

            @pl.when(step == n_steps - 1)
            def _():
                dwb_ref[...] = dw_ref[...].astype(BF16)

        if wxg:
            dx_ref, dg_ref = outs[-2:]

            @pl.when(step == 0)
            def _():
                dg_ref[...] = jnp.zeros(dg_ref.shape, F32)

            dh = lax.dot_general(dz_ref[...], wf_ref[...], NT, preferred_element_type=F32)
            for n, ref in enumerate(sec_refs):
                dh = dh + lax.dot_general(ref[...], w_ref[:, n * W:(n + 1) * W], NT, preferred_element_type=F32)
            xf = x_ref[...]
            r = lax.rsqrt(jnp.mean(xf * xf, axis=-1, keepdims=True) + RMS_EPS)
            xhat = xf * r
            dg_ref[...] += jnp.sum(dh * xhat, axis=0, keepdims=True)
            dxhat = dh * gam_ref[...]
            dx_ref[...] = g_ref[...] + r * (dxhat - xhat * jnp.mean(dxhat * xhat, axis=-1, keepdims=True))

    rows = lambda width: pl.BlockSpec((ts, width), lambda k: (k, 0))
    sec_specs, operands = _section_specs(sections, ts, W)
    in_specs, out_specs, out_shape = [rows(LANES)] + sec_specs, [], []
    if h is not None:
        whole = pl.BlockSpec((D, N + LANES), lambda k: (0, 0), pipeline_mode=once)
        in_specs.append(rows(D))
        operands.append(h)
        out_specs += [whole, whole]
        out_shape += [jax.ShapeDtypeStruct((D, N + LANES), F32), jax.ShapeDtypeStruct((D, N + LANES), BF16)]
    if wxg:
        w, x, gam, g = wxg
        in_specs += [pl.BlockSpec((D, N), lambda k: (0, 0), pipeline_mode=once),
                     pl.BlockSpec((D, LANES), lambda k: (0, N // LANES), pipeline_mode=once),
                     rows(D), pl.BlockSpec((1, D), lambda k: (0, 0)), rows(D)]
        operands += [w, w, x, gam, g]
        out_specs += [rows(D), pl.BlockSpec((1, D), lambda k: (0, 0))]
        out_shape += [jax.ShapeDtypeStruct((S, D), F32), jax.ShapeDtypeStruct((1, D), F32)]
    name = "inproj_bwd" if h is not None and wxg else ("inproj_bwd_dw" if wxg is None else "inproj_bwd_dx")
    return pl.pallas_call(
        body, name=name, grid=(n_steps,),
        in_specs=in_specs + _after_specs(after), out_specs=out_specs, out_shape=out_shape,
        compiler_params=_params("arbitrary"),
    )(dzf, *operands, *after)


def _attn_bwd(proj, da, qaug, kaug, doaug):
    S = proj.shape[0]
    W = proj.shape[1] // 6
    P = W // LANES
    tq = min(TQ, S)
    nq = S // tq
    qc, kc, vc = 2 * P, 3 * P, 4 * P
    scale = 1.0 / math.sqrt(HEAD_DIM)

    def body(q_ref, k_ref, v_ref, do_ref, qa_ref, ka_ref, da_ref,
             dq_ref, dk_ref, dv_ref, drows_ref, dcols_ref, km_scr, vm_scr, dk_scr, dv_scr, dq_scr):
        pair, j = pl.program_id(0), pl.program_id(1)
        lane, halves = _head_halves(tq)

        @pl.when(jnp.logical_and(pair == 0, j == 0))
        def _():
            drows_ref[...] = jnp.zeros(drows_ref.shape, F32)
            dcols_ref[...] = jnp.zeros(dcols_ref.shape, F32)

        @pl.when(j == 0)
        def _():
            dq_scr[...] = jnp.zeros(dq_scr.shape, F32)

        v_ones = ((lane & (HEAD_DIM - 1)) < 3).astype(BF16)
        for a in range(2):
            km_scr[a] = jnp.where(halves[a], k_ref[...], ka_ref[...])
            vm_scr[a] = jnp.where(halves[a], v_ref[...], v_ones)
        dk_scr[...] = jnp.zeros(dk_scr.shape, F32)
        dv_scr[...] = jnp.zeros(dv_scr.shape, F32)

        def update(i, on_diagonal):
            rows = pl.ds(pl.multiple_of(i * tq, tq), tq)
            qs = q_ref[rows, :] * scale
            do2, qaug_t, doaug_t = do_ref[rows, :], qa_ref[rows, :], da_ref[rows, :]
            if on_diagonal:
                keep = (lax.broadcasted_iota(jnp.int32, (tq, tq), 0)
                        >= lax.broadcasted_iota(jnp.int32, (tq, tq), 1))
            qas = [jnp.where(halves[a], qs, qaug_t) for a in range(2)]
            logits = [lax.dot_general(qas[a], km_scr[a], NT, preferred_element_type=F32) for a in range(2)]
            dps = [lax.dot_general(jnp.where(halves[a], do2, doaug_t), vm_scr[a], NT, preferred_element_type=F32)
                   for a in range(2)]
            dv = None
            for a in range(2):
                s = jnp.where(keep, logits[a], NEG_INF) if on_diagonal else logits[a]
                p = jnp.exp(s)
                dsb = (p * dps[a]).astype(BF16)
                do0 = jnp.where(halves[a], do2, jnp.zeros_like(do2))
                dv_a = lax.dot_general(p.astype(BF16), do0, TN_DIMS, preferred_element_type=F32)
                dv = dv_a if dv is None else dv + dv_a
                dk_scr[a] += lax.dot_general(dsb, qas[a], TN_DIMS, preferred_element_type=F32)
                dq_scr[a, rows, :] += jnp.dot(dsb, km_scr[a], preferred_element_type=F32)
            dv_scr[...] += dv

        def below_diagonal(n, carry):
            update(j + 1 + 2 * n, False)
            update(j + 2 + 2 * n, False)
            return carry

        update(j, True)
        below = nq - 1 - j
        lax.fori_loop(0, below // 2, below_diagonal, 0)

        @pl.when(below % 2 == 1)
        def _():
            update(nq - 1, False)


        def to_head_lanes(old, first, second):
            at = lax.broadcasted_iota(jnp.int32, old.shape, 1) - 2 * pair
            return jnp.where(at == 0, first, jnp.where(at == 1, second, old))

        dk_ref[...] = jnp.where(halves[0], dk_scr[0], dk_scr[1]).astype(BF16)
        dv_ref[...] = dv_scr[...].astype(BF16)
        keys = pl.ds(pl.multiple_of(j * tq, tq), tq)
        ones_a, ones_b = HEAD_DIM + SLOT_ONE, SLOT_ONE
        dcols_ref[keys, :] = to_head_lanes(dcols_ref[keys, :], dk_scr[0][:, ones_a:ones_a + 1],
                                           dk_scr[1][:, ones_b:ones_b + 1])

        @pl.when(j == nq - 1)
        def _():
            row_lane, row_halves = _head_halves(S)
            dq_ref[...] = (jnp.where(row_halves[0], dq_scr[0], dq_scr[1]) * scale).astype(BF16)
            c_a, c_b = HEAD_DIM + SLOT_C, SLOT_C
            drows_ref[...] = to_head_lanes(drows_ref[...], dq_scr[0][:, c_a:c_a + 1], dq_scr[1][:, c_b:c_b + 1])

    tile = lambda col: pl.BlockSpec((tq, LANES), lambda p, j: (j, col + p))
    whole = lambda col: pl.BlockSpec((S, LANES), lambda p, j: (0, col + p))
    shared = pl.BlockSpec((S, LANES), lambda p, j: (0, 0))
    return pl.pallas_call(
        body, name="attn_bwd", grid=(P, nq),
        in_specs=[whole(qc), tile(kc), tile(vc), whole(P), whole(0), tile(0), whole(0)],
        out_specs=[whole(0), tile(0), tile(0), shared, shared],
        out_shape=[jax.ShapeDtypeStruct((S, W), BF16),
                   jax.ShapeDtypeStruct((S, W), BF16),
                   jax.ShapeDtypeStruct((S, W), BF16),
                   jax.ShapeDtypeStruct((S, LANES), F32),
                   jax.ShapeDtypeStruct((S, LANES), F32)],
        scratch_shapes=[pltpu.VMEM((2, tq, LANES), BF16),
                        pltpu.VMEM((2, tq, LANES), BF16),
                        pltpu.VMEM((2, tq, LANES), F32),
                        pltpu.VMEM((tq, LANES), F32),
                        pltpu.VMEM((2, S, LANES), F32)],
        compiler_params=_params("arbitrary", "arbitrary"),
    )(proj, proj, proj, da, qaug, kaug, doaug)


def _fgate_bwd(drows, dcols, z, bias):
    S = z.shape[0]
    tb = min(TB, S)
    nb = S // tb

    def body(drows_ref, dcols_ref, z_ref, b_ref, dz_ref, db_ref):
        tri = (lax.broadcasted_iota(jnp.int32, (tb, tb), 1)
               >= lax.broadcasted_iota(jnp.int32, (tb, tb), 0)).astype(F32)

        local = []
        for b in range(nb):
            rows = slice(b * tb, (b + 1) * tb)
            local.append(jnp.dot(tri, drows_ref[rows, :] - dcols_ref[rows, :], preferred_element_type=F32,
                                 precision=lax.Precision.HIGHEST))
        carry = jnp.zeros((1, LANES), F32)
        db = jnp.zeros((1, LANES), F32)
        for b in reversed(range(nb)):
            rows = slice(b * tb, (b + 1) * tb)
            rc = local[b] + carry
            carry = rc[0:1, :]
            dz = rc * _sigmoid(-(z_ref[rows, :] + b_ref[...]))
            dz_ref[rows, :] = dz.astype(BF16)
            db = db + jnp.sum(dz, axis=0, keepdims=True)
        db_ref[...] = db

    return pl.pallas_call(
        body, name="fgate_bwd",
        out_shape=[jax.ShapeDtypeStruct((S, LANES), BF16),
                   jax.ShapeDtypeStruct((1, LANES), F32)],
        compiler_params=pltpu.CompilerParams(vmem_limit_bytes=VMEM_LIMIT),
    )(drows, dcols, z, bias)


def _pool_bwd(proj, da, pool_w, pool_scale):
    S = proj.shape[0]
    G = len(POOL_WINDOWS)

    def body(u_ref, dy_ref, w_ref, s_ref, du_ref, dw_ref, ds_ref, pad_ref):
        g = pl.program_id(0)
        for gi, w in enumerate(POOL_WINDOWS):
            @pl.when(g == gi)
            def _():
                d, cnt = _window_mean_minus_self(u_ref[...].astype(F32), pad_ref, w, S)
                db = d.astype(BF16)
                wb = w_ref[0].astype(BF16)
                yraw = jnp.dot(db, wb, preferred_element_type=F32)
                dy = dy_ref[...].astype(F32)
                ds_ref[...] = jnp.sum(dy * yraw, axis=0, keepdims=True)
                dzb = (dy * s_ref[...]).astype(BF16)
                dw_ref[0] = lax.dot_general(db, dzb, TN_DIMS, preferred_element_type=F32)
                dd = lax.dot_general(dzb, wb, NT, preferred_element_type=F32)
                pad_ref[0:S, :] = dd / cnt
                pad_ref[S:S + MAX_WINDOW, :] = jnp.zeros((MAX_WINDOW, LANES), F32)
                acc = -dd
                for j in range(w):
                    acc = acc + pad_ref[j:j + S, :]
                du_ref[...] = acc.astype(BF16)

    return pl.pallas_call(
        body, name="pool_bwd", grid=(G,),
        in_specs=[pl.BlockSpec((S, LANES), lambda g: (0, g)),
                  pl.BlockSpec((S, LANES), lambda g: (0, g)),
                  pl.BlockSpec((1, LANES, LANES), lambda g: (g, 0, 0)),
                  pl.BlockSpec((1, LANES), lambda g: (0, g))],
        out_specs=[pl.BlockSpec((S, LANES), lambda g: (0, g)),
                   pl.BlockSpec((1, LANES, LANES), lambda g: (g, 0, 0)),
                   pl.BlockSpec((1, LANES), lambda g: (0, g))],
        out_shape=[jax.ShapeDtypeStruct((S, G * LANES), BF16),
                   jax.ShapeDtypeStruct((G, LANES, LANES), F32),
                   jax.ShapeDtypeStruct((1, G * LANES), F32)],
        scratch_shapes=[pltpu.VMEM((S + MAX_WINDOW, LANES), F32)],
        compiler_params=_params("arbitrary"),
    )(proj, da, pool_w, pool_scale)


def _adamw(w, m, v, gsets, name, rows, shifted=False, first=0, into=None):
    A, R, C = w.shape
    n_sets = len(gsets)
    tr = min(rows, R)
    c1 = 1.0 / (1.0 - ADAM_B1 ** ADAM_STEP)
    c2 = 1.0 / (1.0 - ADAM_B2 ** ADAM_STEP)
    counts = [len(gs) for gs in gsets]

    def body(w_ref, m_ref, v_ref, *rest):
        g_ref, d_ref, nm_ref, nv_ref = rest[-4:]
        at = 0
        for a in range(n_sets):
            part_refs = rest[at:at + counts[a]]
            at += counts[a]

            @pl.when(pl.program_id(0) == a)
            def _():
                g = None
                for ref in part_refs:
                    for s in range(ref.shape[0]):
                        term = ref[s].astype(F32)
                        g = term if g is None else g + term
                if shifted:
                    lanes = g.shape[1]
                    g = pltpu.roll(g, (lanes - _index(_position()) * (C % LANES)) % lanes, axis=1)[:, :C]
                nm = ADAM_B1 * m_ref[0] + (1.0 - ADAM_B1) * g
                nv = ADAM_B2 * v_ref[0] + (1.0 - ADAM_B2) * (g * g)
                g_ref[0] = g
                nm_ref[0] = nm
                nv_ref[0] = nv
                d_ref[0] = -ADAM_LR * ((nm * c1) / (jnp.sqrt(nv * c2) + ADAM_EPS) + ADAM_WD * w_ref[0])

    spec = pl.BlockSpec((1, tr, C), lambda a, r: (first + a, r, 0))
    part_specs = [pl.BlockSpec((part.shape[0], tr, part.shape[2]), lambda a, r, l=l: (0, jnp.where(a == l, r, 0), 0))
                  for l, gs in enumerate(gsets) for part in gs]
    parts = [part for gs in gsets for part in gs]
    shape = jax.ShapeDtypeStruct((A, R, C), F32)
    earlier = () if into is None else tuple(into)
    return pl.pallas_call(
        body, name=name, grid=(n_sets, R // tr),
        in_specs=[spec, spec, spec] + part_specs + _after_specs(earlier),
        out_specs=[spec, spec, spec, spec],
        out_shape=[shape, shape, shape, shape],
        input_output_aliases={3 + len(parts) + n: n for n in range(len(earlier))},
        compiler_params=_params("arbitrary", "arbitrary"),
    )(w, m, v, *parts, *earlier)


def _position():
    return lax.axis_index("x"), lax.axis_index("y"), lax.axis_index("c")


def _index(dev):
    return 4 * dev[0] + 2 * dev[1] + dev[2]


def _all_gather(arrs, slots, out_shapes, name):
    n_arr = len(arrs)

    def body(*refs):
        ins, outs = refs[:n_arr], refs[n_arr:2 * n_arr]
        send_sems, recv_sems, local_sems = refs[2 * n_arr:]
        x, y, c = _position()
        me, sibling = (x, y, c), (x, y, 1 - c)
        chips = [(1 - x, y), (x, 1 - y), (1 - x, 1 - y)]

        def copy(a, k, block, to, src=None):
            part = slots[a](outs[a], _index(block))
            return pltpu.make_async_remote_copy(
                src_ref=part if src is None else src, dst_ref=part,
                send_sem=send_sems.at[a, k], recv_sem=recv_sems.at[a, k],
                device_id=to, device_id_type=MESH)

        mine = [pltpu.make_async_copy(ins[a], slots[a](outs[a], _index(me)), local_sems.at[a])
                for a in range(n_arr)]
        for cp in mine:
            cp.start()
        first = []
        for a in range(n_arr):
            first.append(copy(a, 0, me, sibling, src=ins[a]))
            first += [copy(a, 1 + j, me, (*chip, c), src=ins[a]) for j, chip in enumerate(chips)]
        for cp in first:
            cp.start()
        passed = []
        for j, chip in enumerate(chips):
            for a in range(n_arr):
                copy(a, 1 + j, (*chip, c), me).wait_recv()
                fwd = copy(a, 4 + j, (*chip, c), sibling)
                fwd.start()
                passed.append(fwd)
        for a in range(n_arr):
            copy(a, 0, sibling, me).wait_recv()
            for j, chip in enumerate(chips):
                copy(a, 4 + j, (*chip, 1 - c), me).wait_recv()
        for cp in first + passed:
            cp.wait_send()
        for cp in mine:
            cp.wait()

    any_spec = pl.BlockSpec(memory_space=pl.ANY)
    return pl.pallas_call(
        body, name=name,
        in_specs=[any_spec] * n_arr, out_specs=[any_spec] * n_arr, out_shape=out_shapes,
        scratch_shapes=[pltpu.SemaphoreType.DMA((n_arr, 7)), pltpu.SemaphoreType.DMA((n_arr, 7)),
                        pltpu.SemaphoreType.DMA((n_arr,))],
    )(*arrs)


def _split_copies(srcs, lands, send_sems, recv_sems, kinds):
    x, y, c = _position()
    me = _index((x, y, c))
    copies = []
    for a, (src_part, land_part) in enumerate(kinds):
        for k in range(1, N_DEV):
            peer = (x ^ ((k >> 2) & 1), y ^ ((k >> 1) & 1), c ^ (k & 1))
            copies.append(pltpu.make_async_remote_copy(
                src_ref=src_part(srcs[a], _index(peer)), dst_ref=land_part(lands[a], me, k),
                send_sem=send_sems[a].at[k - 1], recv_sem=recv_sems[a].at[k - 1],
                device_id=peer, device_id_type=MESH))
    return copies


def _split_start(srcs, lands, kinds, name, after=()):
    n = len(srcs)

    def body(*refs):
        src_refs, land_refs = refs[:n], refs[n:2 * n]
        outs = refs[2 * n + len(after):]
        send_sems, recv_sems = outs[:n], outs[n:2 * n]
        token = outs[-1]
        for cp in _split_copies(src_refs, land_refs, send_sems, recv_sems, kinds):
            cp.start()
        token[...] = jnp.zeros(token.shape, token.dtype)

    hbm = pl.BlockSpec(memory_space=pltpu.HBM)
    sem = pl.BlockSpec(memory_space=pltpu.SEMAPHORE)
    operands = [pltpu.with_memory_space_constraint(t, pltpu.HBM) for t in (*srcs, *lands)]
    out = pl.pallas_call(
        body, name=name,
        in_specs=[hbm] * (2 * n) + _after_specs(after),
        out_specs=[sem] * (2 * n) + [hbm] * (2 * n) + [pl.BlockSpec(memory_space=pltpu.VMEM)],
        out_shape=[pltpu.SemaphoreType.DMA((N_DEV - 1,))] * (2 * n)
        + [pltpu.HBM(t.shape, t.dtype) for t in operands] + [jax.ShapeDtypeStruct((8, LANES), F32)],
        input_output_aliases={i: 2 * n + i for i in range(2 * n)},
        compiler_params=pltpu.CompilerParams(has_side_effects=pltpu.SideEffectType.DATAFLOW_SIDE_EFFECTING),
    )(*operands, *after)
    return [(out[a], out[n + a], out[2 * n + a], out[3 * n + a]) for a in range(n)], out[-1]


def _split_wait(started, kinds, after, name):
    n = len(started)
    sems = [t[0] for t in started] + [t[1] for t in started]
    srcs = [t[2] for t in started]
    lands = [t[3] for t in started]

    def body(*refs):
        src_refs, land_refs = refs[:n], refs[n:2 * n]
        send_sems, recv_sems = refs[2 * n:3 * n], refs[3 * n:4 * n]
        for cp in _split_copies(src_refs, land_refs, send_sems, recv_sems, kinds):
            cp.wait_send()
            cp.wait_recv()

    hbm = pl.BlockSpec(memory_space=pltpu.HBM)
    sem = pl.BlockSpec(memory_space=pltpu.SEMAPHORE)
    out = pl.pallas_call(
        body, name=name,
        in_specs=[hbm] * (2 * n) + [sem] * (2 * n) + _after_specs(after),
        out_specs=[hbm] * (2 * n),
        out_shape=[pltpu.HBM(t.shape, t.dtype) for t in (*srcs, *lands)],
        input_output_aliases={i: i for i in range(2 * n)},
        compiler_params=pltpu.CompilerParams(has_side_effects=pltpu.SideEffectType.DATAFLOW_SIDE_EFFECTING),
    )(*srcs, *lands, *sems, *after)
    return out[n:]


def _as_rows(p):
    if p.size % LANES == 0:
        rows = p.reshape(-1, LANES)
    else:
        rows = p.reshape(-1, p.shape[-1])
        rows = jnp.pad(rows, ((0, 0), (0, LANES - rows.shape[1])))
    return jnp.pad(rows, ((0, -rows.shape[0] % 8), (0, 0)))


def _pack(parts):
    return jnp.concatenate([_as_rows(p) for p in parts])[None]


def _unpack(packed, like):
    out, at = [], 0
    for p in like:
        whole = p.size % LANES == 0
        n = p.size // LANES if whole else p.size // p.shape[-1]
        rows = packed[0, at:at + n]
        out.append((rows if whole else rows[:, :p.shape[-1]]).reshape(p.shape))
        at += n + (-n % 8)
    return out


def _local_step(x, target, norm_g, forget_bias, pool_w, pool_scale, final_g, weights_in, weights_out, on_grads,
                first_after=()):
    L = norm_g.shape[0]
    S, D = x.shape
    W = D // 2
    H = W // HEAD_DIM
    bias = jnp.pad(forget_bias, ((0, 0), (0, LANES - H)))

    saved = []
    nxt = _inproj_fwd(x, norm_g[0:1], weights_in(0, x), tuple(first_after))
    for l in range(L):
        x_in, (proj, h, z, w) = x, nxt
        qaug, kaug = _fgate_fwd(z, bias[l:l + 1], H)
        ypool = _pool_fwd(proj, pool_w[l], pool_scale[l:l + 1])
        o, qaug_b = _attn_fwd(proj, qaug, kaug)
        wout = weights_out(l, o)
        if l < L - 1:
            x, *nxt = _outproj_fwd(ypool, o, proj, x_in, wout, nxt=(norm_g[l + 1:l + 2], weights_in(l + 1, o)))
        else:
            g, loss, d_final_g = _outproj_fwd(ypool, o, proj, x_in, wout, head=(final_g.reshape(1, D), target))
        saved.append((x_in, proj, h, z, qaug_b, kaug, ypool, o, w, wout))

    small, after = None, ()
    for l in reversed(range(L)):
        x_in, proj, h, z, qaug_b, kaug, ypool, o, w, wout = saved[l]
        d_wout, d_wout_bf16, da, dgate, doaug = _outproj_bwd(g, wout, ypool, o, proj, after)
        dq, dk, dv, drows, dcols = _attn_bwd(proj, da, qaug_b, kaug, doaug)
        dzf, db = _fgate_bwd(drows, dcols, z, bias[l:l + 1])
        dpu, dpw, dps = _pool_bwd(proj, da, pool_w[l], pool_scale[l:l + 1])
        dproj = [(dpu, 0), (dgate, 0), (dq, 0), (dk, 0), (dv, 0), (dgate, 1)]
        wxg = (w, x_in, norm_g[l:l + 1], g)
        if l > 0:
            d_w, d_w_bf16, g, dgam = _inproj_bwd(dproj, dzf, h, wxg)
            after = tuple(on_grads(l, d_w, d_w_bf16, d_wout, d_wout_bf16, small))
        else:
            d_w, d_w_bf16 = _inproj_bwd(dproj, dzf, h)
            after = tuple(on_grads(l, d_w, d_w_bf16, d_wout, d_wout_bf16, small))
            g, dgam = _inproj_bwd(dproj, dzf, None, wxg, after)
        small = (dgam[0], db[0, :H], dpw, dps[0])
    return loss[0, 0], g, small, d_final_g[0]


def kernel(x, norm_g, w_in, forget_bias, pool_w, pool_scale, w_out, final_g, loss_target, m_norm_g, m_w_in, m_forget_bias, m_pool_w, m_pool_scale, m_w_out, m_final_g, v_norm_g, v_w_in, v_forget_bias, v_pool_w, v_pool_scale, v_w_out, v_final_g):
    L, D, cols = w_in.shape
    rows_out = w_out.shape[1]
    me = _index(_position())
    slot = _slot_width(cols)
    wout_b = w_out.astype(BF16)
    win_b = _shift_slots(w_in)
    gather_in = (lambda ref, peer: ref, lambda ref, mine, k: ref.at[mine])
    gather_out = (lambda ref, peer: ref, lambda ref, mine, k: ref.at[pl.ds(mine * rows_out, rows_out), :])

    def landing(block, n_slots):
        zone = lax.empty((n_slots * block.shape[0], *block.shape[1:]), block.dtype)
        return lax.dynamic_update_slice(zone, block, (me * block.shape[0],) + (0,) * (block.ndim - 1))

    (first_in,) = _all_gather([win_b[0]], [lambda ref, n: ref.at[n]],
                              [jax.ShapeDtypeStruct((N_DEV, D, slot), BF16)], "gather_first")
    rest_srcs = [wout_b[0]] + [w[l] for l in range(1, L) for w in (win_b, wout_b)]
    rest_lands = [landing(wout_b[0], N_DEV)]
    for l in range(1, L):
        rest_lands += [landing(win_b[l][None], N_DEV), landing(wout_b[l], N_DEV)]
    rest_kinds = [gather_out] + [gather_in, gather_out] * (L - 1)
    rest, rest_token = _split_start(rest_srcs, rest_lands, rest_kinds, "gather_start_rest", (first_in,))

    def weights_in(l, x_in):
        if l == 0:
            return first_in
        (win_all,) = _split_wait([rest[2 * l - 1]], [gather_in], (x_in,), f"gather_wait_in_{l}")
        return win_all

    def weights_out(l, o):
        (wout_full,) = _split_wait([rest[2 * l]], [gather_out], (o,), f"gather_wait_out_{l}")
        return wout_full

    stride = slot - LANES
    exchange_kinds = [(lambda ref, peer: ref.at[:, pl.ds(pl.multiple_of(peer * stride, LANES), slot)],
                       lambda ref, mine, k: ref.at[k - 1]),
                      (lambda ref, peer: ref.at[pl.ds(peer * rows_out, rows_out), :],
                       lambda ref, mine, k: ref.at[k - 1])]
    zero_g = jnp.zeros_like(final_g)
    zero_loss = jnp.zeros((LANES,), F32)

    def small_pack(l, norm_g_l, bias_l, pool_w_l, pool_scale_l, final, loss_row=None):
        return _pack([norm_g_l, bias_l, pool_w_l, pool_scale_l, final if l == 0 else zero_g,
                      zero_loss if loss_row is None else loss_row])[0]

    exchanges, own_parts = {}, {}

    def on_grads(l, dw, dw_bf16, d_wout, d_wout_bf16, small):
        own_parts[l] = (lax.dynamic_slice_in_dim(dw, me * stride, slot, 1)[None],
                        lax.dynamic_slice_in_dim(d_wout, me * rows_out, rows_out, 0)[None])
        srcs = [dw_bf16, d_wout_bf16]
        lands = [lax.empty((N_DEV - 1, D, slot), BF16), lax.empty((N_DEV - 1, rows_out, D), BF16)]
        kinds = list(exchange_kinds)
        if small is not None:
            packed_small = small_pack(l + 1, *small, None)
            srcs.append(packed_small)
            lands.append(landing(packed_small[None], N_DEV))
            kinds.append(gather_in)
        started, token = _split_start(srcs, lands, kinds, f"exchange_start_{l}")
        exchanges[l] = (started, kinds)
        return (token,)

    loss, dx, small_first, d_final_g = _local_step(
        x[0], loss_target[0], norm_g, forget_bias, pool_w, pool_scale, final_g,
        weights_in, weights_out, on_grads, (rest_token,))
    packed_first = small_pack(0, *small_first, d_final_g, jnp.full((LANES,), loss, F32))
    first_started, first_token = _split_start(
        [packed_first], [landing(packed_first[None], N_DEV)], [gather_in],
        "small_start", (w_in, m_w_in, v_w_in, *own_parts[0]))

    gin_sets, gout_sets, small_sets = [None] * L, [None] * L, [None] * L

    def wait_for(l, after):
        started, kinds = exchanges[l]
        got = _split_wait(started, kinds, after, f"exchange_wait_{l}")
        gin_sets[l] = [own_parts[l][0], got[0]]
        gout_sets[l] = [own_parts[l][1], got[1]]
        if len(got) > 2:
            small_sets[l + 1] = [got[2]]

    for l in range(1, L):
        wait_for(l, (dx, first_token))
    rest_in = _adamw(w_in, m_w_in, v_w_in, gin_sets[1:], "adamw_w_in_rest", TM // 2, shifted=True, first=1)
    rest_out = _adamw(w_out, m_w_out, v_w_out, gout_sets[1:], "adamw_w_out_rest", rows_out, first=1)
    wait_for(0, (rest_in[1], rest_out[1]))
    g_w_in, d_w_in, nm_w_in, nv_w_in = _adamw(w_in, m_w_in, v_w_in, gin_sets[:1], "adamw_w_in_first", TM // 2,
                                              shifted=True, into=rest_in)
    g_w_out, d_w_out, nm_w_out, nv_w_out = _adamw(w_out, m_w_out, v_w_out, gout_sets[:1], "adamw_w_out_first",
                                                  rows_out, into=rest_out)
    small_sets[0] = _split_wait(first_started, [gather_in], (d_w_in, d_w_out), "small_wait")
    loss = jnp.sum(small_sets[0][0][:, packed_first.shape[0] - 8, 0])

    def small_stack(norm_g_, bias_, pool_w_, pool_scale_, final):
        return jnp.stack([small_pack(l, norm_g_[l], bias_[l], pool_w_[l], pool_scale_[l], final) for l in range(L)])

    packed = _adamw(small_stack(norm_g, forget_bias, pool_w, pool_scale, final_g),
                    small_stack(m_norm_g, m_forget_bias, m_pool_w, m_pool_scale, m_final_g),
                    small_stack(v_norm_g, v_forget_bias, v_pool_w, v_pool_scale, v_final_g),
                    small_sets, "adamw_small", packed_first.shape[0])

    def small_unpack(p):
        like = [norm_g[0], forget_bias[0], pool_w[0], pool_scale[0], final_g]
        layers = [_unpack(p[l:l + 1], like) for l in range(L)]
        return [jnp.stack([layers[l][n] for l in range(L)]) for n in range(4)] + [layers[0][4]]

    g_s, d_s, nm_s, nv_s = [small_unpack(p) for p in packed]

    def order(big_in, big_out, small):
        return (small[0], big_in, small[1], small[2], small[3], big_out, small[4])

    return (loss, dx[None], *order(g_w_in, g_w_out, g_s), *order(d_w_in, d_w_out, d_s),
            *order(nm_w_in, nm_w_out, nm_s), *order(nv_w_in, nv_w_out, nv_s))
```

```python
import math

import jax
import jax.numpy as jnp
from jax import lax
from jax.experimental import pallas as pl
from jax.experimental.pallas import tpu as pltpu

F32 = jnp.float32
BF16 = jnp.bfloat16
MESH = pl.DeviceIdType.MESH

RMS_EPS = 1e-6
NEG_INF = -1e30
HEAD_DIM = 64
POOL_WINDOWS = (2, 4, 8, 16)
MAX_WINDOW = 16
LANES = 128
N_DEV = 8

ADAM_LR = 0.001
ADAM_B1 = 0.9
ADAM_B2 = 0.999
ADAM_EPS = 1e-08
ADAM_WD = 0.01
ADAM_STEP = 10

TM = 512
TN = 512
TQ = 512
TB = 256
VMEM_LIMIT = 56 * 1024 * 1024

NT = (((1,), (1,)), ((), ()))
TN_DIMS = (((0,), (0,)), ((), ()))

SLOT_C, SLOT_ONE, SLOT_LSE = 0, 3, 6


def _params(*sem):
    return pltpu.CompilerParams(dimension_semantics=sem, vmem_limit_bytes=VMEM_LIMIT)


def _sigmoid(x):
    return 1.0 / (1.0 + jnp.exp(-x))


def _split3(x):
    hi = x.astype(BF16).astype(F32)
    rest = x - hi
    mid = rest.astype(BF16).astype(F32)
    return hi, mid, rest - mid


def _after_specs(after):
    return [pl.BlockSpec(memory_space=pl.ANY)] * len(after)


def _slot_width(cols):
    return LANES * (-(-(cols + (N_DEV - 1) * (cols % LANES)) // LANES))


def _shift_slots(w_in):
    L, D, cols = w_in.shape
    slot = _slot_width(cols)
    tr = min(TM // 2, D)

    def body(w_ref, o_ref, pad_scr):
        pad_scr[...] = jnp.zeros(pad_scr.shape, F32)
        pad_scr[:, 0:cols] = w_ref[0]
        o_ref[0] = pltpu.roll(pad_scr[...], _index(_position()) * (cols % LANES), axis=1).astype(BF16)

    return pl.pallas_call(
        body, name="shift_slots", grid=(L, D // tr),
        in_specs=[pl.BlockSpec((1, tr, cols), lambda l, r: (l, r, 0))],
        out_specs=pl.BlockSpec((1, tr, slot), lambda l, r: (l, r, 0)),
        out_shape=jax.ShapeDtypeStruct((L, D, slot), BF16),
        scratch_shapes=[pltpu.VMEM((tr, slot), F32)],
        compiler_params=_params("parallel", "parallel"),
    )(w_in)


def _inproj_tile(xf, g_ref, s_ref, proj_ref, h_ref, z_ref, w_ref):
    n_dev, _, sw = s_ref.shape
    stride = sw - LANES
    width = w_ref.shape[1]
    N = width - LANES

    @pl.when(pl.program_id(0) == 0)
    def _():
        for n in range(n_dev):
            base = stride * n
            first = s_ref[n, :, 0:LANES]
            if n > 0:
                first = first + s_ref[n - 1, :, stride:sw]
            w_ref[:, base:base + LANES] = first
            w_ref[:, base + LANES:base + stride] = s_ref[n, :, LANES:stride]
        w_ref[:, stride * n_dev:width] = s_ref[n_dev - 1, :, stride:sw]

    r = lax.rsqrt(jnp.mean(xf * xf, axis=-1, keepdims=True) + RMS_EPS)
    h = ((xf * r) * g_ref[...]).astype(BF16)
    h_ref[...] = h
    z_ref[...] = jnp.dot(h, w_ref[:, N:width], preferred_element_type=F32)
    for n in range(N // TN):
        cols = slice(n * TN, (n + 1) * TN)
        proj_ref[:, cols] = jnp.dot(h, w_ref[:, cols], preferred_element_type=F32).astype(BF16)


def _inproj_specs(slots, S, D, tm):
    n_dev, _, sw = slots.shape
    width = (sw - LANES) * n_dev + LANES
    N = width - LANES
    once = pl.Buffered(1)
    in_specs = [pl.BlockSpec((1, D), lambda i: (0, 0)),
                pl.BlockSpec((n_dev, D, sw), lambda i: (0, 0, 0), pipeline_mode=once)]
    out_specs = [pl.BlockSpec((tm, N), lambda i: (i, 0)),
                 pl.BlockSpec((tm, D), lambda i: (i, 0)),
                 pl.BlockSpec((tm, LANES), lambda i: (i, 0)),
                 pl.BlockSpec((D, width), lambda i: (0, 0), pipeline_mode=once)]
    out_shape = [jax.ShapeDtypeStruct((S, N), BF16),
                 jax.ShapeDtypeStruct((S, D), BF16),
                 jax.ShapeDtypeStruct((S, LANES), F32),
                 jax.ShapeDtypeStruct((D, width), BF16)]
    return in_specs, out_specs, out_shape


def _inproj_fwd(x, gam, slots, after=()):
    S, D = x.shape
    tm = min(TM, S)

    def body(x_ref, g_ref, s_ref, *rest):
        _inproj_tile(x_ref[...], g_ref, s_ref, *rest[-4:])

    in_specs, out_specs, out_shape = _inproj_specs(slots, S, D, tm)
    return pl.pallas_call(
        body, name="inproj_fwd", grid=(S // tm,),
        in_specs=[pl.BlockSpec((tm, D), lambda i: (i, 0))] + in_specs + _after_specs(after),
        out_specs=out_specs, out_shape=out_shape,
        compiler_params=_params("arbitrary"),
    )(x, gam, slots, *after)


def _fgate_fwd(z, bias, n_heads):
    S = z.shape[0]
    tb = min(TB, S)
    P = n_heads // 2
    assert n_heads <= 8, "the three parts of c are packed eight lanes apart"

    def body(z_ref, b_ref, qaug_ref, kaug_ref):
        lane = lax.broadcasted_iota(jnp.int32, (tb, LANES), 1)
        tri = (lax.broadcasted_iota(jnp.int32, (tb, tb), 0)
               >= lax.broadcasted_iota(jnp.int32, (tb, tb), 1)).astype(F32)
        row = lax.broadcasted_iota(jnp.int32, (LANES, P * LANES), 0)
        col = lax.broadcasted_iota(jnp.int32, (LANES, P * LANES), 1)
        head, part_n = row & 7, row >> 3
        home = (head >> 1) * LANES + jnp.where((head & 1) == 0, HEAD_DIM, 0)
        is_part = jnp.logical_and(head < n_heads, part_n < 3)
        place_q = jnp.logical_and(is_part, col == home + SLOT_C + part_n).astype(BF16)
        place_k = jnp.logical_and(is_part, col == home + SLOT_ONE + part_n).astype(BF16)
        slot = lax.broadcasted_iota(jnp.int32, (tb, P * LANES), 1) & (HEAD_DIM - 1)
        q_ones = jnp.logical_and(slot >= SLOT_ONE, slot < SLOT_ONE + 3).astype(F32)
        k_ones = jnp.logical_or(slot < SLOT_C + 3,
                                jnp.logical_and(slot >= SLOT_LSE, slot < SLOT_LSE + 3)).astype(F32)

        local = []
        for b in range(S // tb):
            zz = z_ref[b * tb:(b + 1) * tb, :] + b_ref[...]
            lf = jnp.minimum(zz, 0.0) - jnp.log(1.0 + jnp.exp(-jnp.abs(zz)))
            lf = jnp.where(lane < n_heads, lf, 0.0)
            local.append(jnp.dot(tri, lf, preferred_element_type=F32, precision=lax.Precision.HIGHEST))
        carry = jnp.zeros((1, LANES), F32)
        for b, part_sum in enumerate(local):
            c = part_sum + carry
            carry = c[tb - 1:tb, :]
            hi, mid, lo = _split3(c)
            packed = (hi + pltpu.roll(mid, 8, axis=1) + pltpu.roll(lo, 16, axis=1)).astype(BF16)
            qaug_ref[b * tb:(b + 1) * tb, :] = (
                q_ones + jnp.dot(packed, place_q, preferred_element_type=F32)).astype(BF16)
            kaug_ref[b * tb:(b + 1) * tb, :] = (
                k_ones - jnp.dot(packed, place_k, preferred_element_type=F32)).astype(BF16)

    return pl.pallas_call(
        body, name="fgate_fwd",
        out_shape=[jax.ShapeDtypeStruct((S, P * LANES), BF16),
                   jax.ShapeDtypeStruct((S, P * LANES), BF16)],
        compiler_params=pltpu.CompilerParams(vmem_limit_bytes=VMEM_LIMIT),
    )(z, bias)


def _window_mean_minus_self(u, pad_ref, w, S):
    pad_ref[0:MAX_WINDOW, :] = jnp.zeros((MAX_WINDOW, LANES), F32)
    pad_ref[MAX_WINDOW:MAX_WINDOW + S, :] = u
    acc = u
    for j in range(1, w):
        acc = acc + pad_ref[MAX_WINDOW - j:MAX_WINDOW - j + S, :]
    t = lax.broadcasted_iota(jnp.int32, (S, LANES), 0)
    cnt = jnp.minimum(t + 1, w).astype(F32)
    return acc / cnt - u, cnt


def _pool_fwd(proj, pool_w, pool_scale):
    S = proj.shape[0]
    G = len(POOL_WINDOWS)

    def body(u_ref, w_ref, s_ref, y_ref, pad_ref):
        g = pl.program_id(0)
        for gi, w in enumerate(POOL_WINDOWS):
            @pl.when(g == gi)
            def _():
                d, _ = _window_mean_minus_self(u_ref[...].astype(F32), pad_ref, w, S)
                y = jnp.dot(d.astype(BF16), w_ref[0].astype(BF16), preferred_element_type=F32)
                y_ref[...] = (y * s_ref[...]).astype(BF16)

    return pl.pallas_call(
        body, name="pool_fwd", grid=(G,),
        in_specs=[pl.BlockSpec((S, LANES), lambda g: (0, g)),
                  pl.BlockSpec((1, LANES, LANES), lambda g: (g, 0, 0)),
                  pl.BlockSpec((1, LANES), lambda g: (0, g))],
        out_specs=pl.BlockSpec((S, LANES), lambda g: (0, g)),
        out_shape=jax.ShapeDtypeStruct((S, G * LANES), BF16),
        scratch_shapes=[pltpu.VMEM((S + MAX_WINDOW, LANES), F32)],
        compiler_params=_params("arbitrary"),
    )(proj, pool_w, pool_scale)


def _head_halves(rows):
    lane = lax.broadcasted_iota(jnp.int32, (rows, LANES), 1)
    return lane, (lane < HEAD_DIM, lane >= HEAD_DIM)


def _attn_fwd(proj, qaug, kaug):
    S = proj.shape[0]
    W = proj.shape[1] // 6
    P = W // LANES
    tk = min(TQ, S // 2)
    tq = 2 * tk
    nq = S // tq
    qc, kc, vc = 2 * P, 3 * P, 4 * P
    scale = 1.0 / math.sqrt(HEAD_DIM)

    def body(q_ref, k_ref, v_ref, qa_ref, ka_ref, o_ref, qb_ref, qm_scr, m_scr, acc_scr):
        i = pl.program_id(1)
        lane, halves = _head_halves(tq)
        key_halves = (halves[0][:tk], halves[1][:tk])
        v_ones = ((lane[:tk] & (HEAD_DIM - 1)) < 3).astype(BF16)
        qs = q_ref[...] * scale
        qm_scr[0] = jnp.where(halves[0], qs, qa_ref[...])
        qm_scr[1] = jnp.where(halves[1], qs, qa_ref[...])
        m_scr[...] = jnp.full(m_scr.shape, NEG_INF, F32)
        acc_scr[...] = jnp.zeros(acc_scr.shape, F32)
        top, bottom, both = slice(0, tk), slice(tk, tq), slice(0, tq)

        def update(rows, j, on_diagonal):
            keys = pl.ds(pl.multiple_of(j * tk, tk), tk)
            k2, v2, kaug_t = k_ref[keys, :], v_ref[keys, :], ka_ref[keys, :]
            if on_diagonal:
                keep = (lax.broadcasted_iota(jnp.int32, (tk, tk), 0)
                        >= lax.broadcasted_iota(jnp.int32, (tk, tk), 1))
            logits = [lax.dot_general(qm_scr[a, rows, :], jnp.where(key_halves[a], k2, kaug_t), NT,
                                      preferred_element_type=F32) for a in range(2)]
            for a in range(2):
                s = jnp.where(keep, logits[a], NEG_INF) if on_diagonal else logits[a]
                va = jnp.where(key_halves[a], v2, v_ones)
                m_prev = m_scr[a, rows, :]
                m_new = jnp.maximum(m_prev, jnp.max(s, axis=1, keepdims=True))
                p = jnp.exp(s - jnp.tile(m_new, (1, tk // LANES)))
                acc_scr[a, rows, :] = (jnp.exp(m_prev - m_new) * acc_scr[a, rows, :]
                                       + jnp.dot(p.astype(BF16), va, preferred_element_type=F32))
                m_scr[a, rows, :] = m_new

        def below_diagonal(jj, carry):
            update(both, 2 * jj, False)
            update(both, 2 * jj + 1, False)
            return carry

        lax.fori_loop(0, i, below_diagonal, 0)
        update(top, 2 * i, True)
        update(bottom, 2 * i, False)
        update(bottom, 2 * i + 1, True)
        acc_a, acc_b = acc_scr[0], acc_scr[1]
        l_a, l_b = acc_a[:, HEAD_DIM:HEAD_DIM + 1], acc_b[:, 0:1]
        o_ref[...] = jnp.where(halves[0], acc_a / l_a, acc_b / l_b).astype(BF16)
        lse = jnp.where(halves[0], m_scr[1] + jnp.log(l_b), m_scr[0] + jnp.log(l_a))
        slot = lane & (HEAD_DIM - 1)
        aug = qa_ref[...].astype(F32)
        for n, part in enumerate(_split3(lse)):
            aug = jnp.where(slot == SLOT_LSE + n, -part, aug)
        qb_ref[...] = aug.astype(BF16)

    tile = lambda col: pl.BlockSpec((tq, LANES), lambda p, i: (i, col + p))
    whole = lambda col: pl.BlockSpec((S, LANES), lambda p, i: (0, col + p))
    return pl.pallas_call(
        body, name="attn_fwd", grid=(P, nq),
        in_specs=[tile(qc), whole(kc), whole(vc), tile(0), whole(0)],
        out_specs=[tile(0), tile(0)],
        out_shape=[jax.ShapeDtypeStruct((S, W), BF16), jax.ShapeDtypeStruct((S, W), BF16)],
        scratch_shapes=[pltpu.VMEM((2, tq, LANES), BF16),
                        pltpu.VMEM((2, tq, LANES), F32),
                        pltpu.VMEM((2, tq, LANES), F32)],
        compiler_params=_params("parallel", "arbitrary"),
    )(proj, proj, proj, qaug, kaug)


def _outproj_fwd(ypool, o, proj, x, wout, nxt=None, head=None):
    S, D = x.shape
    W = D // 2
    tm, tn = min(TM, S), TN

    def body(y_ref, o_ref, pg_ref, ag_ref, x_ref, w_ref, *rest):
        mix_ref = rest[-1]
        out_ref = rest[2]
        pg, ag = pg_ref[...].astype(F32), ag_ref[...].astype(F32)
        mix_ref[:, 0:W] = (y_ref[...].astype(F32) * (pg * _sigmoid(pg))).astype(BF16)
        mix_ref[:, W:D] = (o_ref[...].astype(F32) * (ag * _sigmoid(ag))).astype(BF16)
        for n in range(D // tn):
            cols = slice(n * tn, (n + 1) * tn)
            out_ref[:, cols] = x_ref[:, cols] + jnp.dot(mix_ref[...], w_ref[:, cols], preferred_element_type=F32)
        if nxt:
            _inproj_tile(out_ref[...], rest[0], rest[1], *rest[3:7])
            return
        gam_ref, t_ref = rest[:2]
        loss_ref, dg_ref = rest[-3:-1]

        @pl.when(pl.program_id(0) == 0)
        def _():
            loss_ref[...] = jnp.zeros(loss_ref.shape, F32)
            dg_ref[...] = jnp.zeros(dg_ref.shape, F32)

        xf, gam_v = out_ref[...], gam_ref[...]
        r = lax.rsqrt(jnp.mean(xf * xf, axis=-1, keepdims=True) + RMS_EPS)
        xhat = xf * r
        err = xhat * gam_v - t_ref[...]
        part = jnp.sum(jnp.sum(err * err, axis=-1, keepdims=True), axis=0, keepdims=True)
        loss_ref[...] += part * (0.5 / D)
        dy = err * (1.0 / D)
        dg_ref[...] += jnp.sum(dy * xhat, axis=0, keepdims=True)
        dxhat = dy * gam_v
        out_ref[...] = r * (dxhat - xhat * jnp.mean(dxhat * xhat, axis=-1, keepdims=True))

    assert (nxt is None) != (head is None)
    rows = lambda width, col: pl.BlockSpec((tm, width), lambda i: (i, col))
    in_specs = [rows(W, 0), rows(W, 0), rows(W, 1), rows(W, 5), rows(D, 0),
                pl.BlockSpec((D, D), lambda i: (0, 0), pipeline_mode=pl.Buffered(1))]
    out_specs = [rows(D, 0)]
    out_shape = [jax.ShapeDtypeStruct((S, D), F32)]
    if nxt:
        more_in, more_out, more_shape = _inproj_specs(nxt[1], S, D, tm)
        in_specs += more_in
        out_specs += more_out
        out_shape += more_shape
    else:
        in_specs += [pl.BlockSpec((1, D), lambda i: (0, 0)), rows(D, 0)]
        out_specs += [pl.BlockSpec((8, LANES), lambda i: (0, 0)), pl.BlockSpec((1, D), lambda i: (0, 0))]
        out_shape += [jax.ShapeDtypeStruct((8, LANES), F32), jax.ShapeDtypeStruct((1, D), F32)]
    return pl.pallas_call(
        body, name="outproj_fwd_loss" if head else "outproj_inproj_fwd", grid=(S // tm,),
        in_specs=in_specs, out_specs=out_specs, out_shape=out_shape,
        scratch_shapes=[pltpu.VMEM((tm, D), BF16)],
        compiler_params=_params("arbitrary"),
    )(ypool, o, proj, proj, x, wout, *(nxt or head))


def _outproj_bwd(g, wout, ypool, o, proj, after=()):
    S, D = g.shape
    W = D // 2
    tm = min(TM, S)

    def body(g_ref, w_ref, y_ref, o_ref, pg_ref, ag_ref, *rest):
        dw_ref, dwb_ref, da_ref, dgate_ref, doaug_ref = rest[-5:]

        @pl.when(pl.program_id(0) == 0)
        def _():
            dw_ref[...] = jnp.zeros(dw_ref.shape, F32)

        gb = g_ref[...].astype(BF16)
        dmixes = [lax.dot_general(gb, w_ref[half * W:(half + 1) * W, :], NT, preferred_element_type=F32)
                  for half in range(2)]
        d_o = None
        for half, (val_ref, gate_ref) in enumerate(((y_ref, pg_ref), (o_ref, ag_ref))):
            cols = slice(half * W, (half + 1) * W)
            gt = gate_ref[...].astype(F32)
            sg = _sigmoid(gt)
            silu = gt * sg
            val = val_ref[...].astype(F32)
            dw_ref[cols, :] += lax.dot_general((val * silu).astype(BF16), gb, TN_DIMS, preferred_element_type=F32)
            d_o = (dmixes[half] * silu).astype(BF16)
            da_ref[:, cols] = d_o
            dgate_ref[:, cols] = (dmixes[half] * val * (sg * (1.0 + gt * (1.0 - sg)))).astype(BF16)

        lane, halves = _head_halves(tm)
        slot = lane & (HEAD_DIM - 1)
        for p in range(W // LANES):
            cols = slice(p * LANES, (p + 1) * LANES)
            prod = d_o[:, cols].astype(F32) * o_ref[:, cols].astype(F32)
            d_a = jnp.sum(jnp.where(halves[0], prod, 0.0), axis=1, keepdims=True)
            d_b = jnp.sum(jnp.where(halves[1], prod, 0.0), axis=1, keepdims=True)
            aug = jnp.zeros((tm, LANES), F32)
            for n, part in enumerate(_split3(jnp.where(halves[0], d_b, d_a))):
                aug = jnp.where(slot == SLOT_C + n, -part, aug)
            doaug_ref[:, cols] = aug.astype(BF16)

        @pl.when(pl.program_id(0) == S // tm - 1)
        def _():
            dwb_ref[...] = dw_ref[...].astype(BF16)

    rows = lambda width, col: pl.BlockSpec((tm, width), lambda i: (i, col))
    whole = pl.BlockSpec((D, D), lambda i: (0, 0))
    return pl.pallas_call(
        body, name="outproj_bwd", grid=(S // tm,),
        in_specs=[rows(D, 0), whole, rows(W, 0), rows(W, 0), rows(W, 1), rows(W, 5)] + _after_specs(after),
        out_specs=[whole, whole, rows(D, 0), rows(D, 0), rows(W, 0)],
        out_shape=[jax.ShapeDtypeStruct((D, D), F32),
                   jax.ShapeDtypeStruct((D, D), BF16),
                   jax.ShapeDtypeStruct((S, D), BF16),
                   jax.ShapeDtypeStruct((S, D), BF16),
                   jax.ShapeDtypeStruct((S, W), BF16)],
        compiler_params=_params("arbitrary"),
    )(g, wout, ypool, o, proj, proj, *after)


def _section_specs(sections, rows, width):
    specs = [pl.BlockSpec((rows, width), lambda k, c=c: (k, c)) for _, c in sections]
    return specs, [a for a, _ in sections]


def _inproj_bwd(sections, dzf, h=None, wxg=None, after=()):
    S = dzf.shape[0]
    n_sec = len(sections)
    D = h.shape[1] if h is not None else wxg[1].shape[1]
    W = D // 2
    N = n_sec * W
    ts = min(TM if wxg else 2 * TM, S)
    n_steps = S // ts
    once = pl.Buffered(1)

    def body(dz_ref, *rest):
        sec_refs, rest = rest[:n_sec], rest[n_sec:]
        if h is not None:
            h_ref, rest = rest[0], rest[1:]
        if wxg:
            (w_ref, wf_ref, x_ref, gam_ref, g_ref), rest = rest[:5], rest[5:]
        outs = rest[len(after):]
        step = pl.program_id(0)

        if h is not None:
            dw_ref, dwb_ref = outs[:2]

            @pl.when(step == 0)
            def _():
                dw_ref[...] = jnp.zeros(dw_ref.shape, F32)

            ht = h_ref[...].T
            dw_ref[:, N:N + LANES] += jnp.dot(ht, dz_ref[...], preferred_element_type=F32)
            for n, ref in enumerate(sec_refs):
                dw_ref[:, n * W:(n + 1) * W] += jnp.dot(ht, ref[...], preferred_element_type=F32)

            @pl.when(step == n_steps - 1)
            def _():
                dwb_ref[...] = dw_ref[...].astype(BF16)

        if wxg:
            dx_ref, dg_ref = outs[-2:]

            @pl.when(step == 0)
            def _():
                dg_ref[...] = jnp.zeros(dg_ref.shape, F32)

            dh = lax.dot_general(dz_ref[...], wf_ref[...], NT, preferred_element_type=F32)
            for n, ref in enumerate(sec_refs):
                dh = dh + lax.dot_general(ref[...], w_ref[:, n * W:(n + 1) * W], NT, preferred_element_type=F32)
            xf = x_ref[...]
            r = lax.rsqrt(jnp.mean(xf * xf, axis=-1, keepdims=True) + RMS_EPS)
            xhat = xf * r
            dg_ref[...] += jnp.sum(dh * xhat, axis=0, keepdims=True)
            dxhat = dh * gam_ref[...]
            dx_ref[...] = g_ref[...] + r * (dxhat - xhat * jnp.mean(dxhat * xhat, axis=-1, keepdims=True))

    rows = lambda width: pl.BlockSpec((ts, width), lambda k: (k, 0))
    sec_specs, operands = _section_specs(sections, ts, W)
    in_specs, out_specs, out_shape = [rows(LANES)] + sec_specs, [], []
    if h is not None:
        whole = pl.BlockSpec((D, N + LANES), lambda k: (0, 0), pipeline_mode=once)
        in_specs.append(rows(D))
        operands.append(h)
        out_specs += [whole, whole]
        out_shape += [jax.ShapeDtypeStruct((D, N + LANES), F32), jax.ShapeDtypeStruct((D, N + LANES), BF16)]
    if wxg:
        w, x, gam, g = wxg
        in_specs += [pl.BlockSpec((D, N), lambda k: (0, 0), pipeline_mode=once),
                     pl.BlockSpec((D, LANES), lambda k: (0, N // LANES), pipeline_mode=once),
                     rows(D), pl.BlockSpec((1, D), lambda k: (0, 0)), rows(D)]
        operands += [w, w, x, gam, g]
        out_specs += [rows(D), pl.BlockSpec((1, D), lambda k: (0, 0))]
        out_shape += [jax.ShapeDtypeStruct((S, D), F32), jax.ShapeDtypeStruct((1, D), F32)]
    name = "inproj_bwd" if h is not None and wxg else ("inproj_bwd_dw" if wxg is None else "inproj_bwd_dx")
    return pl.pallas_call(
        body, name=name, grid=(n_steps,),
        in_specs=in_specs + _after_specs(after), out_specs=out_specs, out_shape=out_shape,
        compiler_params=_params("arbitrary"),
    )(dzf, *operands, *after)


def _attn_bwd(proj, da, qaug, kaug, doaug, after=()):
    S = proj.shape[0]
    W = proj.shape[1] // 6
    P = W // LANES
    tq = min(TQ, S)
    nq = S // tq
    qc, kc, vc = 2 * P, 3 * P, 4 * P
    scale = 1.0 / math.sqrt(HEAD_DIM)

    def body(q_ref, k_ref, v_ref, do_ref, qa_ref, ka_ref, da_ref, *rest):
        dq_ref, dk_ref, dv_ref, drows_ref, dcols_ref, km_scr, vm_scr, dk_scr, dv_scr, dq_scr = rest[len(after):]
        pair, j = pl.program_id(0), pl.program_id(1)
        lane, halves = _head_halves(tq)

        @pl.when(jnp.logical_and(pair == 0, j == 0))
        def _():
            drows_ref[...] = jnp.zeros(drows_ref.shape, F32)
            dcols_ref[...] = jnp.zeros(dcols_ref.shape, F32)

        @pl.when(j == 0)
        def _():
            dq_scr[...] = jnp.zeros(dq_scr.shape, F32)

        v_ones = ((lane & (HEAD_DIM - 1)) < 3).astype(BF16)
        for a in range(2):
            km_scr[a] = jnp.where(halves[a], k_ref[...], ka_ref[...])
            vm_scr[a] = jnp.where(halves[a], v_ref[...], v_ones)
        dk_scr[...] = jnp.zeros(dk_scr.shape, F32)
        dv_scr[...] = jnp.zeros(dv_scr.shape, F32)

        def update(i, on_diagonal):
            rows = pl.ds(pl.multiple_of(i * tq, tq), tq)
            qs = q_ref[rows, :] * scale
            do2, qaug_t, doaug_t = do_ref[rows, :], qa_ref[rows, :], da_ref[rows, :]
            if on_diagonal:
                keep = (lax.broadcasted_iota(jnp.int32, (tq, tq), 0)
                        >= lax.broadcasted_iota(jnp.int32, (tq, tq), 1))
            qas = [jnp.where(halves[a], qs, qaug_t) for a in range(2)]
            logits = [lax.dot_general(qas[a], km_scr[a], NT, preferred_element_type=F32) for a in range(2)]
            dps = [lax.dot_general(jnp.where(halves[a], do2, doaug_t), vm_scr[a], NT, preferred_element_type=F32)
                   for a in range(2)]
            dv = None
            for a in range(2):
                s = jnp.where(keep, logits[a], NEG_INF) if on_diagonal else logits[a]
                p = jnp.exp(s)
                dsb = (p * dps[a]).astype(BF16)
                do0 = jnp.where(halves[a], do2, jnp.zeros_like(do2))
                dv_a = lax.dot_general(p.astype(BF16), do0, TN_DIMS, preferred_element_type=F32)
                dv = dv_a if dv is None else dv + dv_a
                dk_scr[a] += lax.dot_general(dsb, qas[a], TN_DIMS, preferred_element_type=F32)
                dq_scr[a, rows, :] += jnp.dot(dsb, km_scr[a], preferred_element_type=F32)
            dv_scr[...] += dv

        def below_diagonal(n, carry):
            update(j + 1 + 2 * n, False)
            update(j + 2 + 2 * n, False)
            return carry

        update(j, True)
        below = nq - 1 - j
        lax.fori_loop(0, below // 2, below_diagonal, 0)

        @pl.when(below % 2 == 1)
        def _():
            update(nq - 1, False)


        def to_head_lanes(old, first, second):
            at = lax.broadcasted_iota(jnp.int32, old.shape, 1) - 2 * pair
            return jnp.where(at == 0, first, jnp.where(at == 1, second, old))

        dk_ref[...] = jnp.where(halves[0], dk_scr[0], dk_scr[1]).astype(BF16)
        dv_ref[...] = dv_scr[...].astype(BF16)
        keys = pl.ds(pl.multiple_of(j * tq, tq), tq)
        ones_a, ones_b = HEAD_DIM + SLOT_ONE, SLOT_ONE
        dcols_ref[keys, :] = to_head_lanes(dcols_ref[keys, :], dk_scr[0][:, ones_a:ones_a + 1],
                                           dk_scr[1][:, ones_b:ones_b + 1])

        @pl.when(j == nq - 1)
        def _():
            row_lane, row_halves = _head_halves(S)
            dq_ref[...] = (jnp.where(row_halves[0], dq_scr[0], dq_scr[1]) * scale).astype(BF16)
            c_a, c_b = HEAD_DIM + SLOT_C, SLOT_C
            drows_ref[...] = to_head_lanes(drows_ref[...], dq_scr[0][:, c_a:c_a + 1], dq_scr[1][:, c_b:c_b + 1])

    tile = lambda col: pl.BlockSpec((tq, LANES), lambda p, j: (j, col + p))
    whole = lambda col: pl.BlockSpec((S, LANES), lambda p, j: (0, col + p))
    shared = pl.BlockSpec((S, LANES), lambda p, j: (0, 0))
    return pl.pallas_call(
        body, name="attn_bwd", grid=(P, nq),
        in_specs=[whole(qc), tile(kc), tile(vc), whole(P), whole(0), tile(0), whole(0)] + _after_specs(after),
        out_specs=[whole(0), tile(0), tile(0), shared, shared],
        out_shape=[jax.ShapeDtypeStruct((S, W), BF16),
                   jax.ShapeDtypeStruct((S, W), BF16),
                   jax.ShapeDtypeStruct((S, W), BF16),
                   jax.ShapeDtypeStruct((S, LANES), F32),
                   jax.ShapeDtypeStruct((S, LANES), F32)],
        scratch_shapes=[pltpu.VMEM((2, tq, LANES), BF16),
                        pltpu.VMEM((2, tq, LANES), BF16),
                        pltpu.VMEM((2, tq, LANES), F32),
                        pltpu.VMEM((tq, LANES), F32),
                        pltpu.VMEM((2, S, LANES), F32)],
        compiler_params=_params("arbitrary", "arbitrary"),
    )(proj, proj, proj, da, qaug, kaug, doaug, *after)


def _fgate_bwd(drows, dcols, z, bias):
    S = z.shape[0]
    tb = min(TB, S)
    nb = S // tb

    def body(drows_ref, dcols_ref, z_ref, b_ref, dz_ref, db_ref):
        tri = (lax.broadcasted_iota(jnp.int32, (tb, tb), 1)
               >= lax.broadcasted_iota(jnp.int32, (tb, tb), 0)).astype(F32)

        local = []
        for b in range(nb):
            rows = slice(b * tb, (b + 1) * tb)
            local.append(jnp.dot(tri, drows_ref[rows, :] - dcols_ref[rows, :], preferred_element_type=F32,
                                 precision=lax.Precision.HIGHEST))
        carry = jnp.zeros((1, LANES), F32)
        db = jnp.zeros((1, LANES), F32)
        for b in reversed(range(nb)):
            rows = slice(b * tb, (b + 1) * tb)
            rc = local[b] + carry
            carry = rc[0:1, :]
            dz = rc * _sigmoid(-(z_ref[rows, :] + b_ref[...]))
            dz_ref[rows, :] = dz.astype(BF16)
            db = db + jnp.sum(dz, axis=0, keepdims=True)
        db_ref[...] = db

    return pl.pallas_call(
        body, name="fgate_bwd",
        out_shape=[jax.ShapeDtypeStruct((S, LANES), BF16),
                   jax.ShapeDtypeStruct((1, LANES), F32)],
        compiler_params=pltpu.CompilerParams(vmem_limit_bytes=VMEM_LIMIT),
    )(drows, dcols, z, bias)


def _pool_bwd(proj, da, pool_w, pool_scale):
    S = proj.shape[0]
    G = len(POOL_WINDOWS)

    def body(u_ref, dy_ref, w_ref, s_ref, du_ref, dw_ref, ds_ref, pad_ref):
        g = pl.program_id(0)
        for gi, w in enumerate(POOL_WINDOWS):
            @pl.when(g == gi)
            def _():
                d, cnt = _window_mean_minus_self(u_ref[...].astype(F32), pad_ref, w, S)
                db = d.astype(BF16)
                wb = w_ref[0].astype(BF16)
                yraw = jnp.dot(db, wb, preferred_element_type=F32)
                dy = dy_ref[...].astype(F32)
                ds_ref[...] = jnp.sum(dy * yraw, axis=0, keepdims=True)
                dzb = (dy * s_ref[...]).astype(BF16)
                dw_ref[0] = lax.dot_general(db, dzb, TN_DIMS, preferred_element_type=F32)
                dd = lax.dot_general(dzb, wb, NT, preferred_element_type=F32)
                pad_ref[0:S, :] = dd / cnt
                pad_ref[S:S + MAX_WINDOW, :] = jnp.zeros((MAX_WINDOW, LANES), F32)
                acc = -dd
                for j in range(w):
                    acc = acc + pad_ref[j:j + S, :]
                du_ref[...] = acc.astype(BF16)

    return pl.pallas_call(
        body, name="pool_bwd", grid=(G,),
        in_specs=[pl.BlockSpec((S, LANES), lambda g: (0, g)),
                  pl.BlockSpec((S, LANES), lambda g: (0, g)),
                  pl.BlockSpec((1, LANES, LANES), lambda g: (g, 0, 0)),
                  pl.BlockSpec((1, LANES), lambda g: (0, g))],
        out_specs=[pl.BlockSpec((S, LANES), lambda g: (0, g)),
                   pl.BlockSpec((1, LANES, LANES), lambda g: (g, 0, 0)),
                   pl.BlockSpec((1, LANES), lambda g: (0, g))],
        out_shape=[jax.ShapeDtypeStruct((S, G * LANES), BF16),
                   jax.ShapeDtypeStruct((G, LANES, LANES), F32),
                   jax.ShapeDtypeStruct((1, G * LANES), F32)],
        scratch_shapes=[pltpu.VMEM((S + MAX_WINDOW, LANES), F32)],
        compiler_params=_params("arbitrary"),
    )(proj, da, pool_w, pool_scale)


def _adamw(w, m, v, gsets, name, rows, shifted=False, first=0, into=None):
    A, R, C = w.shape
    n_sets = len(gsets)
    tr = min(rows, R)
    c1 = 1.0 / (1.0 - ADAM_B1 ** ADAM_STEP)
    c2 = 1.0 / (1.0 - ADAM_B2 ** ADAM_STEP)
    counts = [len(gs) for gs in gsets]

    def body(w_ref, m_ref, v_ref, *rest):
        g_ref, d_ref, nm_ref, nv_ref = rest[-4:]
        at = 0
        for a in range(n_sets):
            part_refs = rest[at:at + counts[a]]
            at += counts[a]

            @pl.when(pl.program_id(0) == a)
            def _():
                g = None
                for ref in part_refs:
                    for s in range(ref.shape[0]):
                        term = ref[s].astype(F32)
                        g = term if g is None else g + term
                if shifted:
                    lanes = g.shape[1]
                    g = pltpu.roll(g, (lanes - _index(_position()) * (C % LANES)) % lanes, axis=1)[:, :C]
                nm = ADAM_B1 * m_ref[0] + (1.0 - ADAM_B1) * g
                nv = ADAM_B2 * v_ref[0] + (1.0 - ADAM_B2) * (g * g)
                g_ref[0] = g
                nm_ref[0] = nm
                nv_ref[0] = nv
                d_ref[0] = -ADAM_LR * ((nm * c1) / (jnp.sqrt(nv * c2) + ADAM_EPS) + ADAM_WD * w_ref[0])

    spec = pl.BlockSpec((1, tr, C), lambda a, r: (first + a, r, 0))
    part_specs = [pl.BlockSpec((part.shape[0], tr, part.shape[2]), lambda a, r, l=l: (0, jnp.where(a == l, r, 0), 0))
                  for l, gs in enumerate(gsets) for part in gs]
    parts = [part for gs in gsets for part in gs]
    shape = jax.ShapeDtypeStruct((A, R, C), F32)
    earlier = () if into is None else tuple(into)
    return pl.pallas_call(
        body, name=name, grid=(n_sets, R // tr),
        in_specs=[spec, spec, spec] + part_specs + _after_specs(earlier),
        out_specs=[spec, spec, spec, spec],
        out_shape=[shape, shape, shape, shape],
        input_output_aliases={3 + len(parts) + n: n for n in range(len(earlier))},
        compiler_params=_params("arbitrary", "arbitrary"),
    )(w, m, v, *parts, *earlier)


def _position():
    return lax.axis_index("x"), lax.axis_index("y"), lax.axis_index("c")


def _index(dev):
    return 4 * dev[0] + 2 * dev[1] + dev[2]


def _all_gather(arrs, slots, out_shapes, name):
    n_arr = len(arrs)

    def body(*refs):
        ins, outs = refs[:n_arr], refs[n_arr:2 * n_arr]
        send_sems, recv_sems, local_sems = refs[2 * n_arr:]
        x, y, c = _position()
        me, sibling = (x, y, c), (x, y, 1 - c)
        chips = [(1 - x, y), (x, 1 - y), (1 - x, 1 - y)]

        def copy(a, k, block, to, src=None):
            part = slots[a](outs[a], _index(block))
            return pltpu.make_async_remote_copy(
                src_ref=part if src is None else src, dst_ref=part,
                send_sem=send_sems.at[a, k], recv_sem=recv_sems.at[a, k],
                device_id=to, device_id_type=MESH)

        mine = [pltpu.make_async_copy(ins[a], slots[a](outs[a], _index(me)), local_sems.at[a])
                for a in range(n_arr)]
        for cp in mine:
            cp.start()
        first = []
        for a in range(n_arr):
            first.append(copy(a, 0, me, sibling, src=ins[a]))
            first += [copy(a, 1 + j, me, (*chip, c), src=ins[a]) for j, chip in enumerate(chips)]
        for cp in first:
            cp.start()
        passed = []
        for j, chip in enumerate(chips):
            for a in range(n_arr):
                copy(a, 1 + j, (*chip, c), me).wait_recv()
                fwd = copy(a, 4 + j, (*chip, c), sibling)
                fwd.start()
                passed.append(fwd)
        for a in range(n_arr):
            copy(a, 0, sibling, me).wait_recv()
            for j, chip in enumerate(chips):
                copy(a, 4 + j, (*chip, 1 - c), me).wait_recv()
        for cp in first + passed:
            cp.wait_send()
        for cp in mine:
            cp.wait()

    any_spec = pl.BlockSpec(memory_space=pl.ANY)
    return pl.pallas_call(
        body, name=name,
        in_specs=[any_spec] * n_arr, out_specs=[any_spec] * n_arr, out_shape=out_shapes,
        scratch_shapes=[pltpu.SemaphoreType.DMA((n_arr, 7)), pltpu.SemaphoreType.DMA((n_arr, 7)),
                        pltpu.SemaphoreType.DMA((n_arr,))],
    )(*arrs)


def _split_copies(srcs, lands, send_sems, recv_sems, kinds):
    x, y, c = _position()
    me = _index((x, y, c))
    copies = []
    for a, (src_part, land_part) in enumerate(kinds):
        for k in range(1, N_DEV):
            peer = (x ^ ((k >> 2) & 1), y ^ ((k >> 1) & 1), c ^ (k & 1))
            copies.append(pltpu.make_async_remote_copy(
                src_ref=src_part(srcs[a], _index(peer)), dst_ref=land_part(lands[a], me, k),
                send_sem=send_sems[a].at[k - 1], recv_sem=recv_sems[a].at[k - 1],
                device_id=peer, device_id_type=MESH))
    return copies


def _split_start(srcs, lands, kinds, name, after=()):
    n = len(srcs)

    def body(*refs):
        src_refs, land_refs = refs[:n], refs[n:2 * n]
        outs = refs[2 * n + len(after):]
        send_sems, recv_sems = outs[:n], outs[n:2 * n]
        token = outs[-1]
        for cp in _split_copies(src_refs, land_refs, send_sems, recv_sems, kinds):
            cp.start()
        token[...] = jnp.zeros(token.shape, token.dtype)

    hbm = pl.BlockSpec(memory_space=pltpu.HBM)
    sem = pl.BlockSpec(memory_space=pltpu.SEMAPHORE)
    operands = [pltpu.with_memory_space_constraint(t, pltpu.HBM) for t in (*srcs, *lands)]
    out = pl.pallas_call(
        body, name=name,
        in_specs=[hbm] * (2 * n) + _after_specs(after),
        out_specs=[sem] * (2 * n) + [hbm] * (2 * n) + [pl.BlockSpec(memory_space=pltpu.VMEM)],
        out_shape=[pltpu.SemaphoreType.DMA((N_DEV - 1,))] * (2 * n)
        + [pltpu.HBM(t.shape, t.dtype) for t in operands] + [jax.ShapeDtypeStruct((8, LANES), F32)],
        input_output_aliases={i: 2 * n + i for i in range(2 * n)},
        compiler_params=pltpu.CompilerParams(has_side_effects=pltpu.SideEffectType.DATAFLOW_SIDE_EFFECTING),
    )(*operands, *after)
    return [(out[a], out[n + a], out[2 * n + a], out[3 * n + a]) for a in range(n)], out[-1]


def _split_wait(started, kinds, after, name):
    n = len(started)
    sems = [t[0] for t in started] + [t[1] for t in started]
    srcs = [t[2] for t in started]
    lands = [t[3] for t in started]

    def body(*refs):
        src_refs, land_refs = refs[:n], refs[n:2 * n]
        send_sems, recv_sems = refs[2 * n:3 * n], refs[3 * n:4 * n]
        for cp in _split_copies(src_refs, land_refs, send_sems, recv_sems, kinds):
            cp.wait_send()
            cp.wait_recv()

    hbm = pl.BlockSpec(memory_space=pltpu.HBM)
    sem = pl.BlockSpec(memory_space=pltpu.SEMAPHORE)
    out = pl.pallas_call(
        body, name=name,
        in_specs=[hbm] * (2 * n) + [sem] * (2 * n) + _after_specs(after),
        out_specs=[hbm] * (2 * n),
        out_shape=[pltpu.HBM(t.shape, t.dtype) for t in (*srcs, *lands)],
        input_output_aliases={i: i for i in range(2 * n)},
        compiler_params=pltpu.CompilerParams(has_side_effects=pltpu.SideEffectType.DATAFLOW_SIDE_EFFECTING),
    )(*srcs, *lands, *sems, *after)
    return out[n:]


def _as_rows(p):
    if p.size % LANES == 0:
        rows = p.reshape(-1, LANES)
    else:
        rows = p.reshape(-1, p.shape[-1])
        rows = jnp.pad(rows, ((0, 0), (0, LANES - rows.shape[1])))
    return jnp.pad(rows, ((0, -rows.shape[0] % 8), (0, 0)))


def _pack(parts):
    return jnp.concatenate([_as_rows(p) for p in parts])[None]


def _unpack(packed, like):
    out, at = [], 0
    for p in like:
        whole = p.size % LANES == 0
        n = p.size // LANES if whole else p.size // p.shape[-1]
        rows = packed[0, at:at + n]
        out.append((rows if whole else rows[:, :p.shape[-1]]).reshape(p.shape))
        at += n + (-n % 8)
    return out


def _local_step(x, target, norm_g, forget_bias, pool_w, pool_scale, final_g, weights_in, weights_out, on_grads,
                first_after=(), on_first_out_grads=lambda d_wout, d_wout_bf16: ()):
    L = norm_g.shape[0]
    S, D = x.shape
    W = D // 2
    H = W // HEAD_DIM
    bias = jnp.pad(forget_bias, ((0, 0), (0, LANES - H)))

    saved = []
    nxt = _inproj_fwd(x, norm_g[0:1], weights_in(0, x), tuple(first_after))
    for l in range(L):
        x_in, (proj, h, z, w) = x, nxt
        qaug, kaug = _fgate_fwd(z, bias[l:l + 1], H)
        ypool = _pool_fwd(proj, pool_w[l], pool_scale[l:l + 1])
        o, qaug_b = _attn_fwd(proj, qaug, kaug)
        wout = weights_out(l, o)
        if l < L - 1:
            x, *nxt = _outproj_fwd(ypool, o, proj, x_in, wout, nxt=(norm_g[l + 1:l + 2], weights_in(l + 1, o)))
        else:
            g, loss, d_final_g = _outproj_fwd(ypool, o, proj, x_in, wout, head=(final_g.reshape(1, D), target))
        saved.append((x_in, proj, h, z, qaug_b, kaug, ypool, o, w, wout))

    small, after = None, ()
    for l in reversed(range(L)):
        x_in, proj, h, z, qaug_b, kaug, ypool, o, w, wout = saved[l]
        d_wout, d_wout_bf16, da, dgate, doaug = _outproj_bwd(g, wout, ypool, o, proj, after)
        early = tuple(on_first_out_grads(d_wout, d_wout_bf16)) if l == 0 else ()
        dq, dk, dv, drows, dcols = _attn_bwd(proj, da, qaug_b, kaug, doaug, early)
        dzf, db = _fgate_bwd(drows, dcols, z, bias[l:l + 1])
        dpu, dpw, dps = _pool_bwd(proj, da, pool_w[l], pool_scale[l:l + 1])
        dproj = [(dpu, 0), (dgate, 0), (dq, 0), (dk, 0), (dv, 0), (dgate, 1)]
        wxg = (w, x_in, norm_g[l:l + 1], g)
        if l > 0:
            d_w, d_w_bf16, g, dgam = _inproj_bwd(dproj, dzf, h, wxg)
            after = tuple(on_grads(l, d_w, d_w_bf16, d_wout, d_wout_bf16, small))
        else:
            d_w, d_w_bf16 = _inproj_bwd(dproj, dzf, h)
            after = tuple(on_grads(l, d_w, d_w_bf16, d_wout, d_wout_bf16, small))
            g, dgam = _inproj_bwd(dproj, dzf, None, wxg, after)
        small = (dgam[0], db[0, :H], dpw, dps[0])
    return loss[0, 0], g, small, d_final_g[0]


def kernel(x, norm_g, w_in, forget_bias, pool_w, pool_scale, w_out, final_g, loss_target, m_norm_g, m_w_in, m_forget_bias, m_pool_w, m_pool_scale, m_w_out, m_final_g, v_norm_g, v_w_in, v_forget_bias, v_pool_w, v_pool_scale, v_w_out, v_final_g):
    L, D, cols = w_in.shape
    rows_out = w_out.shape[1]
    me = _index(_position())
    slot = _slot_width(cols)
    wout_b = w_out.astype(BF16)
    win_b = _shift_slots(w_in)
    gather_in = (lambda ref, peer: ref, lambda ref, mine, k: ref.at[mine])
    gather_out = (lambda ref, peer: ref, lambda ref, mine, k: ref.at[pl.ds(mine * rows_out, rows_out), :])

    def landing(block, n_slots):
        zone = lax.empty((n_slots * block.shape[0], *block.shape[1:]), block.dtype)
        return lax.dynamic_update_slice(zone, block, (me * block.shape[0],) + (0,) * (block.ndim - 1))

    (first_in,) = _all_gather([win_b[0]], [lambda ref, n: ref.at[n]],
                              [jax.ShapeDtypeStruct((N_DEV, D, slot), BF16)], "gather_first")
    rest_srcs = [wout_b[0]] + [w[l] for l in range(1, L) for w in (win_b, wout_b)]
    rest_lands = [landing(wout_b[0], N_DEV)]
    for l in range(1, L):
        rest_lands += [landing(win_b[l][None], N_DEV), landing(wout_b[l], N_DEV)]
    rest_kinds = [gather_out] + [gather_in, gather_out] * (L - 1)
    rest, rest_token = _split_start(rest_srcs, rest_lands, rest_kinds, "gather_start_rest", (first_in,))

    def weights_in(l, x_in):
        if l == 0:
            return first_in
        (win_all,) = _split_wait([rest[2 * l - 1]], [gather_in], (x_in,), f"gather_wait_in_{l}")
        return win_all

    def weights_out(l, o):
        (wout_full,) = _split_wait([rest[2 * l]], [gather_out], (o,), f"gather_wait_out_{l}")
        return wout_full

    stride = slot - LANES
    exchange_kinds = [(lambda ref, peer: ref.at[:, pl.ds(pl.multiple_of(peer * stride, LANES), slot)],
                       lambda ref, mine, k: ref.at[k - 1]),
                      (lambda ref, peer: ref.at[pl.ds(peer * rows_out, rows_out), :],
                       lambda ref, mine, k: ref.at[k - 1])]
    zero_g = jnp.zeros_like(final_g)
    zero_loss = jnp.zeros((LANES,), F32)

    def small_pack(l, norm_g_l, bias_l, pool_w_l, pool_scale_l, final, loss_row=None):
        return _pack([norm_g_l, bias_l, pool_w_l, pool_scale_l, final if l == 0 else zero_g,
                      zero_loss if loss_row is None else loss_row])[0]

    exchanges, own_parts, first_out = {}, {}, []

    def on_first_out_grads(d_wout, d_wout_bf16):
        started, token = _split_start([d_wout_bf16], [lax.empty((N_DEV - 1, rows_out, D), BF16)], exchange_kinds[1:],
                                      "exchange_start_out_0")
        first_out.extend(started)
        return (token,)

    def on_grads(l, dw, dw_bf16, d_wout, d_wout_bf16, small):
        own_parts[l] = (lax.dynamic_slice_in_dim(dw, me * stride, slot, 1)[None],
                        lax.dynamic_slice_in_dim(d_wout, me * rows_out, rows_out, 0)[None])
        out_on_its_way = l == 0
        srcs, lands, kinds = [dw_bf16], [lax.empty((N_DEV - 1, D, slot), BF16)], exchange_kinds[:1]
        if not out_on_its_way:
            srcs.append(d_wout_bf16)
            lands.append(lax.empty((N_DEV - 1, rows_out, D), BF16))
            kinds = list(exchange_kinds)
        if small is not None:
            packed_small = small_pack(l + 1, *small, None)
            srcs.append(packed_small)
            lands.append(landing(packed_small[None], N_DEV))
            kinds.append(gather_in)
        started, token = _split_start(srcs, lands, kinds, f"exchange_start_{l}")
        if out_on_its_way:
            started[1:1], kinds[1:1] = first_out, exchange_kinds[1:]
        exchanges[l] = (started, kinds)
        return (token,)

    loss, dx, small_first, d_final_g = _local_step(
        x[0], loss_target[0], norm_g, forget_bias, pool_w, pool_scale, final_g,
        weights_in, weights_out, on_grads, (rest_token,), on_first_out_grads)
    packed_first = small_pack(0, *small_first, d_final_g, jnp.full((LANES,), loss, F32))
    first_started, first_token = _split_start(
        [packed_first], [landing(packed_first[None], N_DEV)], [gather_in],
        "small_start", (w_in, m_w_in, v_w_in, *own_parts[0]))

    gin_sets, gout_sets, small_sets = [None] * L, [None] * L, [None] * L

    def wait_for(l, after):
        started, kinds = exchanges[l]
        got = _split_wait(started, kinds, after, f"exchange_wait_{l}")
        gin_sets[l] = [own_parts[l][0], got[0]]
        gout_sets[l] = [own_parts[l][1], got[1]]
        if len(got) > 2:
            small_sets[l + 1] = [got[2]]

    for l in range(1, L):
        wait_for(l, (dx, first_token))
    rest_in = _adamw(w_in, m_w_in, v_w_in, gin_sets[1:], "adamw_w_in_rest", TM // 2, shifted=True, first=1)
    rest_out = _adamw(w_out, m_w_out, v_w_out, gout_sets[1:], "adamw_w_out_rest", rows_out, first=1)
    wait_for(0, (rest_in[1], rest_out[1]))
    g_w_in, d_w_in, nm_w_in, nv_w_in = _adamw(w_in, m_w_in, v_w_in, gin_sets[:1], "adamw_w_in_first", TM // 2,
                                              shifted=True, into=rest_in)
    g_w_out, d_w_out, nm_w_out, nv_w_out = _adamw(w_out, m_w_out, v_w_out, gout_sets[:1], "adamw_w_out_first",
                                                  rows_out, into=rest_out)
    small_sets[0] = _split_wait(first_started, [gather_in], (d_w_in, d_w_out), "small_wait")
    loss = jnp.sum(small_sets[0][0][:, packed_first.shape[0] - 8, 0])

    def small_stack(norm_g_, bias_, pool_w_, pool_scale_, final):
        return jnp.stack([small_pack(l, norm_g_[l], bias_[l], pool_w_[l], pool_scale_[l], final) for l in range(L)])

    packed = _adamw(small_stack(norm_g, forget_bias, pool_w, pool_scale, final_g),
                    small_stack(m_norm_g, m_forget_bias, m_pool_w, m_pool_scale, m_final_g),
                    small_stack(v_norm_g, v_forget_bias, v_pool_w, v_pool_scale, v_final_g),
                    small_sets, "adamw_small", packed_first.shape[0])

    def small_unpack(p):
        like = [norm_g[0], forget_bias[0], pool_w[0], pool_scale[0], final_g]
        layers = [_unpack(p[l:l + 1], like) for l in range(L)]
        return [jnp.stack([layers[l][n] for l in range(L)]) for n in range(4)] + [layers[0][4]]

    g_s, d_s, nm_s, nv_s = [small_unpack(p) for p in packed]

    def order(big_in, big_out, small):
        return (small[0], big_in, small[1], small[2], small[3], big_out, small[4])

    return (loss, dx[None], *order(g_w_in, g_w_out, g_s), *order(d_w_in, d_w_out, d_s),
            *order(nm_w_in, nm_w_out, nm_s), *order(nv_w_in, nv_w_out, nv_s))
```

```python
import math

import jax
import jax.numpy as jnp
from jax import lax
from jax.experimental import pallas as pl
from jax.experimental.pallas import tpu as pltpu

F32 = jnp.float32
BF16 = jnp.bfloat16
MESH = pl.DeviceIdType.MESH

RMS_EPS = 1e-6
NEG_INF = -1e30
HEAD_DIM = 64
POOL_WINDOWS = (2, 4, 8, 16)
MAX_WINDOW = 16
LANES = 128
N_DEV = 8

ADAM_LR = 0.001
ADAM_B1 = 0.9
ADAM_B2 = 0.999
ADAM_EPS = 1e-08
ADAM_WD = 0.01
ADAM_STEP = 10

TM = 512
TN = 512
TQ = 512
TB = 256
VMEM_LIMIT = 56 * 1024 * 1024

NT = (((1,), (1,)), ((), ()))
TN_DIMS = (((0,), (0,)), ((), ()))

SLOT_C, SLOT_ONE, SLOT_LSE = 0, 3, 6


def _params(*sem):
    return pltpu.CompilerParams(dimension_semantics=sem, vmem_limit_bytes=VMEM_LIMIT)


def _sigmoid(x):
    return 1.0 / (1.0 + jnp.exp(-x))


def _split3(x):
    hi = x.astype(BF16).astype(F32)
    rest = x - hi
    mid = rest.astype(BF16).astype(F32)
    return hi, mid, rest - mid


def _dot_exact_left(a, x):
    return sum(jnp.dot(a, part.astype(BF16), preferred_element_type=F32) for part in reversed(_split3(x)))


def _after_specs(after):
    return [pl.BlockSpec(memory_space=pl.ANY)] * len(after)


def _slot_width(cols):
    return LANES * (-(-(cols + (N_DEV - 1) * (cols % LANES)) // LANES))


def _shift_slots(w_in):
    L, D, cols = w_in.shape
    slot = _slot_width(cols)
    tr = min(TM // 2, D)

    def body(w_ref, o_ref, pad_scr):
        pad_scr[...] = jnp.zeros(pad_scr.shape, F32)
        pad_scr[:, 0:cols] = w_ref[0]
        o_ref[0] = pltpu.roll(pad_scr[...], _index(_position()) * (cols % LANES), axis=1).astype(BF16)

    return pl.pallas_call(
        body, name="shift_slots", grid=(L, D // tr),
        in_specs=[pl.BlockSpec((1, tr, cols), lambda l, r: (l, r, 0))],
        out_specs=pl.BlockSpec((1, tr, slot), lambda l, r: (l, r, 0)),
        out_shape=jax.ShapeDtypeStruct((L, D, slot), BF16),
        scratch_shapes=[pltpu.VMEM((tr, slot), F32)],
        compiler_params=_params("parallel", "parallel"),
    )(w_in)


def _inproj_tile(xf, g_ref, s_ref, proj_ref, h_ref, z_ref, w_ref):
    n_dev, _, sw = s_ref.shape
    stride = sw - LANES
    width = w_ref.shape[1]
    N = width - LANES

    @pl.when(pl.program_id(0) == 0)
    def _():
        for n in range(n_dev):
            base = stride * n
            first = s_ref[n, :, 0:LANES]
            if n > 0:
                first = first + s_ref[n - 1, :, stride:sw]
            w_ref[:, base:base + LANES] = first
            w_ref[:, base + LANES:base + stride] = s_ref[n, :, LANES:stride]
        w_ref[:, stride * n_dev:width] = s_ref[n_dev - 1, :, stride:sw]

    r = lax.rsqrt(jnp.mean(xf * xf, axis=-1, keepdims=True) + RMS_EPS)
    h = ((xf * r) * g_ref[...]).astype(BF16)
    h_ref[...] = h
    z_ref[...] = jnp.dot(h, w_ref[:, N:width], preferred_element_type=F32)
    for n in range(N // TN):
        cols = slice(n * TN, (n + 1) * TN)
        proj_ref[:, cols] = jnp.dot(h, w_ref[:, cols], preferred_element_type=F32).astype(BF16)


def _inproj_specs(slots, S, D, tm):
    n_dev, _, sw = slots.shape
    width = (sw - LANES) * n_dev + LANES
    N = width - LANES
    once = pl.Buffered(1)
    in_specs = [pl.BlockSpec((1, D), lambda i: (0, 0)),
                pl.BlockSpec((n_dev, D, sw), lambda i: (0, 0, 0), pipeline_mode=once)]
    out_specs = [pl.BlockSpec((tm, N), lambda i: (i, 0)),
                 pl.BlockSpec((tm, D), lambda i: (i, 0)),
                 pl.BlockSpec((tm, LANES), lambda i: (i, 0)),
                 pl.BlockSpec((D, width), lambda i: (0, 0), pipeline_mode=once)]
    out_shape = [jax.ShapeDtypeStruct((S, N), BF16),
                 jax.ShapeDtypeStruct((S, D), BF16),
                 jax.ShapeDtypeStruct((S, LANES), F32),
                 jax.ShapeDtypeStruct((D, width), BF16)]
    return in_specs, out_specs, out_shape


def _inproj_fwd(x, gam, slots, after=()):
    S, D = x.shape
    tm = min(TM, S)

    def body(x_ref, g_ref, s_ref, *rest):
        _inproj_tile(x_ref[...], g_ref, s_ref, *rest[-4:])

    in_specs, out_specs, out_shape = _inproj_specs(slots, S, D, tm)
    return pl.pallas_call(
        body, name="inproj_fwd", grid=(S // tm,),
        in_specs=[pl.BlockSpec((tm, D), lambda i: (i, 0))] + in_specs + _after_specs(after),
        out_specs=out_specs, out_shape=out_shape,
        compiler_params=_params("arbitrary"),
    )(x, gam, slots, *after)


def _fgate_fwd(z, bias, n_heads):
    S = z.shape[0]
    tb = min(TB, S)
    P = n_heads // 2
    assert n_heads <= 8, "the three parts of c are packed eight lanes apart"

    def body(z_ref, b_ref, qaug_ref, kaug_ref):
        lane = lax.broadcasted_iota(jnp.int32, (tb, LANES), 1)
        tri = (lax.broadcasted_iota(jnp.int32, (tb, tb), 0)
               >= lax.broadcasted_iota(jnp.int32, (tb, tb), 1)).astype(BF16)
        row = lax.broadcasted_iota(jnp.int32, (LANES, P * LANES), 0)
        col = lax.broadcasted_iota(jnp.int32, (LANES, P * LANES), 1)
        head, part_n = row & 7, row >> 3
        home = (head >> 1) * LANES + jnp.where((head & 1) == 0, HEAD_DIM, 0)
        is_part = jnp.logical_and(head < n_heads, part_n < 3)
        place_q = jnp.logical_and(is_part, col == home + SLOT_C + part_n).astype(BF16)
        place_k = jnp.logical_and(is_part, col == home + SLOT_ONE + part_n).astype(BF16)
        slot = lax.broadcasted_iota(jnp.int32, (tb, P * LANES), 1) & (HEAD_DIM - 1)
        q_ones = jnp.logical_and(slot >= SLOT_ONE, slot < SLOT_ONE + 3).astype(F32)
        k_ones = jnp.logical_or(slot < SLOT_C + 3,
                                jnp.logical_and(slot >= SLOT_LSE, slot < SLOT_LSE + 3)).astype(F32)

        local = []
        for b in range(S // tb):
            zz = z_ref[b * tb:(b + 1) * tb, :] + b_ref[...]
            lf = jnp.minimum(zz, 0.0) - jnp.log(1.0 + jnp.exp(-jnp.abs(zz)))
            lf = jnp.where(lane < n_heads, lf, 0.0)
            local.append(_dot_exact_left(tri, lf))
        carry = jnp.zeros((1, LANES), F32)
        for b, part_sum in enumerate(local):
            c = part_sum + carry
            carry = c[tb - 1:tb, :]
            hi, mid, lo = _split3(c)
            packed = (hi + pltpu.roll(mid, 8, axis=1) + pltpu.roll(lo, 16, axis=1)).astype(BF16)
            qaug_ref[b * tb:(b + 1) * tb, :] = (
                q_ones + jnp.dot(packed, place_q, preferred_element_type=F32)).astype(BF16)
            kaug_ref[b * tb:(b + 1) * tb, :] = (
                k_ones - jnp.dot(packed, place_k, preferred_element_type=F32)).astype(BF16)

    return pl.pallas_call(
        body, name="fgate_fwd",
        out_shape=[jax.ShapeDtypeStruct((S, P * LANES), BF16),
                   jax.ShapeDtypeStruct((S, P * LANES), BF16)],
        compiler_params=pltpu.CompilerParams(vmem_limit_bytes=VMEM_LIMIT),
    )(z, bias)


def _window_mean_minus_self(u, pad_ref, w, S):
    pad_ref[0:MAX_WINDOW, :] = jnp.zeros((MAX_WINDOW, LANES), F32)
    pad_ref[MAX_WINDOW:MAX_WINDOW + S, :] = u
    acc = u
    for j in range(1, w):
        acc = acc + pad_ref[MAX_WINDOW - j:MAX_WINDOW - j + S, :]
    t = lax.broadcasted_iota(jnp.int32, (S, LANES), 0)
    cnt = jnp.minimum(t + 1, w).astype(F32)
    return acc / cnt - u, cnt


def _pool_fwd(proj, pool_w, pool_scale):
    S = proj.shape[0]
    G = len(POOL_WINDOWS)

    def body(u_ref, w_ref, s_ref, y_ref, pad_ref):
        g = pl.program_id(0)
        for gi, w in enumerate(POOL_WINDOWS):
            @pl.when(g == gi)
            def _():
                d, _ = _window_mean_minus_self(u_ref[...].astype(F32), pad_ref, w, S)
                y = jnp.dot(d.astype(BF16), w_ref[0].astype(BF16), preferred_element_type=F32)
                y_ref[...] = (y * s_ref[...]).astype(BF16)

    return pl.pallas_call(
        body, name="pool_fwd", grid=(G,),
        in_specs=[pl.BlockSpec((S, LANES), lambda g: (0, g)),
                  pl.BlockSpec((1, LANES, LANES), lambda g: (g, 0, 0)),
                  pl.BlockSpec((1, LANES), lambda g: (0, g))],
        out_specs=pl.BlockSpec((S, LANES), lambda g: (0, g)),
        out_shape=jax.ShapeDtypeStruct((S, G * LANES), BF16),
        scratch_shapes=[pltpu.VMEM((S + MAX_WINDOW, LANES), F32)],
        compiler_params=_params("arbitrary"),
    )(proj, pool_w, pool_scale)


def _head_halves(rows):
    lane = lax.broadcasted_iota(jnp.int32, (rows, LANES), 1)
    return lane, (lane < HEAD_DIM, lane >= HEAD_DIM)


def _attn_fwd(proj, qaug, kaug):
    S = proj.shape[0]
    W = proj.shape[1] // 6
    P = W // LANES
    tk = min(TQ, S // 2)
    tq = 2 * tk
    nq = S // tq
    qc, kc, vc = 2 * P, 3 * P, 4 * P
    scale = 1.0 / math.sqrt(HEAD_DIM)

    def body(q_ref, k_ref, v_ref, qa_ref, ka_ref, o_ref, qb_ref, qm_scr, m_scr, acc_scr):
        i = pl.program_id(1)
        lane, halves = _head_halves(tq)
        key_halves = (halves[0][:tk], halves[1][:tk])
        v_ones = ((lane[:tk] & (HEAD_DIM - 1)) < 3).astype(BF16)
        qs = q_ref[...] * scale
        qm_scr[0] = jnp.where(halves[0], qs, qa_ref[...])
        qm_scr[1] = jnp.where(halves[1], qs, qa_ref[...])
        m_scr[...] = jnp.full(m_scr.shape, NEG_INF, F32)
        acc_scr[...] = jnp.zeros(acc_scr.shape, F32)
        top, bottom, both = slice(0, tk), slice(tk, tq), slice(0, tq)

        def update(rows, j, on_diagonal):
            keys = pl.ds(pl.multiple_of(j * tk, tk), tk)
            k2, v2, kaug_t = k_ref[keys, :], v_ref[keys, :], ka_ref[keys, :]
            if on_diagonal:
                keep = (lax.broadcasted_iota(jnp.int32, (tk, tk), 0)
                        >= lax.broadcasted_iota(jnp.int32, (tk, tk), 1))
            logits = [lax.dot_general(qm_scr[a, rows, :], jnp.where(key_halves[a], k2, kaug_t), NT,
                                      preferred_element_type=F32) for a in range(2)]
            for a in range(2):
                s = jnp.where(keep, logits[a], NEG_INF) if on_diagonal else logits[a]
                va = jnp.where(key_halves[a], v2, v_ones)
                m_prev = m_scr[a, rows, :]
                m_new = jnp.maximum(m_prev, jnp.max(s, axis=1, keepdims=True))
                p = jnp.exp(s - jnp.tile(m_new, (1, tk // LANES)))
                acc_scr[a, rows, :] = (jnp.exp(m_prev - m_new) * acc_scr[a, rows, :]
                                       + jnp.dot(p.astype(BF16), va, preferred_element_type=F32))
                m_scr[a, rows, :] = m_new

        def below_diagonal(jj, carry):
            update(both, 2 * jj, False)
            update(both, 2 * jj + 1, False)
            return carry

        lax.fori_loop(0, i, below_diagonal, 0)
        update(top, 2 * i, True)
        update(bottom, 2 * i, False)
        update(bottom, 2 * i + 1, True)
        acc_a, acc_b = acc_scr[0], acc_scr[1]
        l_a, l_b = acc_a[:, HEAD_DIM:HEAD_DIM + 1], acc_b[:, 0:1]
        o_ref[...] = jnp.where(halves[0], acc_a / l_a, acc_b / l_b).astype(BF16)
        lse = jnp.where(halves[0], m_scr[1] + jnp.log(l_b), m_scr[0] + jnp.log(l_a))
        slot = lane & (HEAD_DIM - 1)
        aug = qa_ref[...].astype(F32)
        for n, part in enumerate(_split3(lse)):
            aug = jnp.where(slot == SLOT_LSE + n, -part, aug)
        qb_ref[...] = aug.astype(BF16)

    tile = lambda col: pl.BlockSpec((tq, LANES), lambda p, i: (i, col + p))
    whole = lambda col: pl.BlockSpec((S, LANES), lambda p, i: (0, col + p))
    return pl.pallas_call(
        body, name="attn_fwd", grid=(P, nq),
        in_specs=[tile(qc), whole(kc), whole(vc), tile(0), whole(0)],
        out_specs=[tile(0), tile(0)],
        out_shape=[jax.ShapeDtypeStruct((S, W), BF16), jax.ShapeDtypeStruct((S, W), BF16)],
        scratch_shapes=[pltpu.VMEM((2, tq, LANES), BF16),
                        pltpu.VMEM((2, tq, LANES), F32),
                        pltpu.VMEM((2, tq, LANES), F32)],
        compiler_params=_params("parallel", "arbitrary"),
    )(proj, proj, proj, qaug, kaug)


def _outproj_fwd(ypool, o, proj, x, wout, nxt=None, head=None):
    S, D = x.shape
    W = D // 2
    tm, tn = min(TM, S), TN

    def body(y_ref, o_ref, pg_ref, ag_ref, x_ref, w_ref, *rest):
        mix_ref = rest[-1]
        out_ref = rest[2]
        pg, ag = pg_ref[...].astype(F32), ag_ref[...].astype(F32)
        mix_ref[:, 0:W] = (y_ref[...].astype(F32) * (pg * _sigmoid(pg))).astype(BF16)
        mix_ref[:, W:D] = (o_ref[...].astype(F32) * (ag * _sigmoid(ag))).astype(BF16)
        for n in range(D // tn):
            cols = slice(n * tn, (n + 1) * tn)
            out_ref[:, cols] = x_ref[:, cols] + jnp.dot(mix_ref[...], w_ref[:, cols], preferred_element_type=F32)
        if nxt:
            _inproj_tile(out_ref[...], rest[0], rest[1], *rest[3:7])
            return
        gam_ref, t_ref = rest[:2]
        loss_ref, dg_ref = rest[-3:-1]

        @pl.when(pl.program_id(0) == 0)
        def _():
            loss_ref[...] = jnp.zeros(loss_ref.shape, F32)
            dg_ref[...] = jnp.zeros(dg_ref.shape, F32)

        xf, gam_v = out_ref[...], gam_ref[...]
        r = lax.rsqrt(jnp.mean(xf * xf, axis=-1, keepdims=True) + RMS_EPS)
        xhat = xf * r
        err = xhat * gam_v - t_ref[...]
        part = jnp.sum(jnp.sum(err * err, axis=-1, keepdims=True), axis=0, keepdims=True)
        loss_ref[...] += part * (0.5 / D)
        dy = err * (1.0 / D)
        dg_ref[...] += jnp.sum(dy * xhat, axis=0, keepdims=True)
        dxhat = dy * gam_v
        out_ref[...] = r * (dxhat - xhat * jnp.mean(dxhat * xhat, axis=-1, keepdims=True))

    assert (nxt is None) != (head is None)
    rows = lambda width, col: pl.BlockSpec((tm, width), lambda i: (i, col))
    in_specs = [rows(W, 0), rows(W, 0), rows(W, 1), rows(W, 5), rows(D, 0),
                pl.BlockSpec((D, D), lambda i: (0, 0), pipeline_mode=pl.Buffered(1))]
    out_specs = [rows(D, 0)]
    out_shape = [jax.ShapeDtypeStruct((S, D), F32)]
    if nxt:
        more_in, more_out, more_shape = _inproj_specs(nxt[1], S, D, tm)
        in_specs += more_in
        out_specs += more_out
        out_shape += more_shape
    else:
        in_specs += [pl.BlockSpec((1, D), lambda i: (0, 0)), rows(D, 0)]
        out_specs += [pl.BlockSpec((8, LANES), lambda i: (0, 0)), pl.BlockSpec((1, D), lambda i: (0, 0))]
        out_shape += [jax.ShapeDtypeStruct((8, LANES), F32), jax.ShapeDtypeStruct((1, D), F32)]
    return pl.pallas_call(
        body, name="outproj_fwd_loss" if head else "outproj_inproj_fwd", grid=(S // tm,),
        in_specs=in_specs, out_specs=out_specs, out_shape=out_shape,
        scratch_shapes=[pltpu.VMEM((tm, D), BF16)],
        compiler_params=_params("arbitrary"),
    )(ypool, o, proj, proj, x, wout, *(nxt or head))


def _outproj_bwd(g, wout, ypool, o, proj, after=()):
    S, D = g.shape
    W = D // 2
    tm = min(TM, S)

    def body(g_ref, w_ref, y_ref, o_ref, pg_ref, ag_ref, *rest):
        dw_ref, dwb_ref, da_ref, dgate_ref, doaug_ref = rest[-5:]

        @pl.when(pl.program_id(0) == 0)
        def _():
            dw_ref[...] = jnp.zeros(dw_ref.shape, F32)

        gb = g_ref[...].astype(BF16)
        dmixes = [lax.dot_general(gb, w_ref[half * W:(half + 1) * W, :], NT, preferred_element_type=F32)
                  for half in range(2)]
        d_o = None
        for half, (val_ref, gate_ref) in enumerate(((y_ref, pg_ref), (o_ref, ag_ref))):
            cols = slice(half * W, (half + 1) * W)
            gt = gate_ref[...].astype(F32)
            sg = _sigmoid(gt)
            silu = gt * sg
            val = val_ref[...].astype(F32)
            dw_ref[cols, :] += lax.dot_general((val * silu).astype(BF16), gb, TN_DIMS, preferred_element_type=F32)
            d_o = (dmixes[half] * silu).astype(BF16)
            da_ref[:, cols] = d_o
            dgate_ref[:, cols] = (dmixes[half] * val * (sg * (1.0 + gt * (1.0 - sg)))).astype(BF16)

        lane, halves = _head_halves(tm)
        slot = lane & (HEAD_DIM - 1)
        for p in range(W // LANES):
            cols = slice(p * LANES, (p + 1) * LANES)
            prod = d_o[:, cols].astype(F32) * o_ref[:, cols].astype(F32)
            d_a = jnp.sum(jnp.where(halves[0], prod, 0.0), axis=1, keepdims=True)
            d_b = jnp.sum(jnp.where(halves[1], prod, 0.0), axis=1, keepdims=True)
            aug = jnp.zeros((tm, LANES), F32)
            for n, part in enumerate(_split3(jnp.where(halves[0], d_b, d_a))):
                aug = jnp.where(slot == SLOT_C + n, -part, aug)
            doaug_ref[:, cols] = aug.astype(BF16)

        @pl.when(pl.program_id(0) == S // tm - 1)
        def _():
            dwb_ref[...] = dw_ref[...].astype(BF16)

    rows = lambda width, col: pl.BlockSpec((tm, width), lambda i: (i, col))
    whole = pl.BlockSpec((D, D), lambda i: (0, 0))
    return pl.pallas_call(
        body, name="outproj_bwd", grid=(S // tm,),
        in_specs=[rows(D, 0), whole, rows(W, 0), rows(W, 0), rows(W, 1), rows(W, 5)] + _after_specs(after),
        out_specs=[whole, whole, rows(D, 0), rows(D, 0), rows(W, 0)],
        out_shape=[jax.ShapeDtypeStruct((D, D), F32),
                   jax.ShapeDtypeStruct((D, D), BF16),
                   jax.ShapeDtypeStruct((S, D), BF16),
                   jax.ShapeDtypeStruct((S, D), BF16),
                   jax.ShapeDtypeStruct((S, W), BF16)],
        compiler_params=_params("arbitrary"),
    )(g, wout, ypool, o, proj, proj, *after)


def _section_specs(sections, rows, width):
    specs = [pl.BlockSpec((rows, width), lambda k, c=c: (k, c)) for _, c in sections]
    return specs, [a for a, _ in sections]


def _inproj_bwd(sections, dzf, h=None, wxg=None, after=()):
    S = dzf.shape[0]
    n_sec = len(sections)
    D = h.shape[1] if h is not None else wxg[1].shape[1]
    W = D // 2
    N = n_sec * W
    ts = min(TM if wxg else 2 * TM, S)
    n_steps = S // ts
    once = pl.Buffered(1)

    def body(dz_ref, *rest):
        sec_refs, rest = rest[:n_sec], rest[n_sec:]
        if h is not None:
            h_ref, rest = rest[0], rest[1:]
        if wxg:
            (w_ref, wf_ref, x_ref, gam_ref, g_ref), rest = rest[:5], rest[5:]
        outs = rest[len(after):]
        step = pl.program_id(0)

        if h is not None:
            dw_ref, dwb_ref = outs[:2]

            @pl.when(step == 0)
            def _():
                dw_ref[...] = jnp.zeros(dw_ref.shape, F32)

            ht = h_ref[...].T
            dw_ref[:, N:N + LANES] += jnp.dot(ht, dz_ref[...], preferred_element_type=F32)
            for n, ref in enumerate(sec_refs):
                dw_ref[:, n * W:(n + 1) * W] += jnp.dot(ht, ref[...], preferred_element_type=F32)

            @pl.when(step == n_steps - 1)
            def _():
                dwb_ref[...] = dw_ref[...].astype(BF16)

        if wxg:
            dx_ref, dg_ref = outs[-2:]

            @pl.when(step == 0)
            def _():
                dg_ref[...] = jnp.zeros(dg_ref.shape, F32)

            dh = lax.dot_general(dz_ref[...], wf_ref[...], NT, preferred_element_type=F32)
            for n, ref in enumerate(sec_refs):
                dh = dh + lax.dot_general(ref[...], w_ref[:, n * W:(n + 1) * W], NT, preferred_element_type=F32)
            xf = x_ref[...]
            r = lax.rsqrt(jnp.mean(xf * xf, axis=-1, keepdims=True) + RMS_EPS)
            xhat = xf * r
            dg_ref[...] += jnp.sum(dh * xhat, axis=0, keepdims=True)
            dxhat = dh * gam_ref[...]
            dx_ref[...] = g_ref[...] + r * (dxhat - xhat * jnp.mean(dxhat * xhat, axis=-1, keepdims=True))

    rows = lambda width: pl.BlockSpec((ts, width), lambda k: (k, 0))
    sec_specs, operands = _section_specs(sections, ts, W)
    in_specs, out_specs, out_shape = [rows(LANES)] + sec_specs, [], []
    if h is not None:
        whole = pl.BlockSpec((D, N + LANES), lambda k: (0, 0), pipeline_mode=once)
        in_specs.append(rows(D))
        operands.append(h)
        out_specs += [whole, whole]
        out_shape += [jax.ShapeDtypeStruct((D, N + LANES), F32), jax.ShapeDtypeStruct((D, N + LANES), BF16)]
    if wxg:
        w, x, gam, g = wxg
        in_specs += [pl.BlockSpec((D, N), lambda k: (0, 0), pipeline_mode=once),
                     pl.BlockSpec((D, LANES), lambda k: (0, N // LANES), pipeline_mode=once),
                     rows(D), pl.BlockSpec((1, D), lambda k: (0, 0)), rows(D)]
        operands += [w, w, x, gam, g]
        out_specs += [rows(D), pl.BlockSpec((1, D), lambda k: (0, 0))]
        out_shape += [jax.ShapeDtypeStruct((S, D), F32), jax.ShapeDtypeStruct((1, D), F32)]
    name = "inproj_bwd" if h is not None and wxg else ("inproj_bwd_dw" if wxg is None else "inproj_bwd_dx")
    return pl.pallas_call(
        body, name=name, grid=(n_steps,),
        in_specs=in_specs + _after_specs(after), out_specs=out_specs, out_shape=out_shape,
        compiler_params=_params("arbitrary"),
    )(dzf, *operands, *after)


def _attn_bwd(proj, da, qaug, kaug, doaug, after=()):
    S = proj.shape[0]
    W = proj.shape[1] // 6
    P = W // LANES
    tq = min(TQ, S)
    nq = S // tq
    qc, kc, vc = 2 * P, 3 * P, 4 * P
    scale = 1.0 / math.sqrt(HEAD_DIM)

    def body(q_ref, k_ref, v_ref, do_ref, qa_ref, ka_ref, da_ref, *rest):
        dq_ref, dk_ref, dv_ref, drows_ref, dcols_ref, km_scr, vm_scr, dk_scr, dv_scr, dq_scr = rest[len(after):]
        pair, j = pl.program_id(0), pl.program_id(1)
        lane, halves = _head_halves(tq)

        @pl.when(jnp.logical_and(pair == 0, j == 0))
        def _():
            drows_ref[...] = jnp.zeros(drows_ref.shape, F32)
            dcols_ref[...] = jnp.zeros(dcols_ref.shape, F32)

        @pl.when(j == 0)
        def _():
            dq_scr[...] = jnp.zeros(dq_scr.shape, F32)

        v_ones = ((lane & (HEAD_DIM - 1)) < 3).astype(BF16)
        for a in range(2):
            km_scr[a] = jnp.where(halves[a], k_ref[...], ka_ref[...])
            vm_scr[a] = jnp.where(halves[a], v_ref[...], v_ones)
        dk_scr[...] = jnp.zeros(dk_scr.shape, F32)
        dv_scr[...] = jnp.zeros(dv_scr.shape, F32)

        def update(i, on_diagonal):
            rows = pl.ds(pl.multiple_of(i * tq, tq), tq)
            qs = q_ref[rows, :] * scale
            do2, qaug_t, doaug_t = do_ref[rows, :], qa_ref[rows, :], da_ref[rows, :]
            if on_diagonal:
                keep = (lax.broadcasted_iota(jnp.int32, (tq, tq), 0)
                        >= lax.broadcasted_iota(jnp.int32, (tq, tq), 1))
            qas = [jnp.where(halves[a], qs, qaug_t) for a in range(2)]
            logits = [lax.dot_general(qas[a], km_scr[a], NT, preferred_element_type=F32) for a in range(2)]
            dps = [lax.dot_general(jnp.where(halves[a], do2, doaug_t), vm_scr[a], NT, preferred_element_type=F32)
                   for a in range(2)]
            dv = None
            for a in range(2):
                s = jnp.where(keep, logits[a], NEG_INF) if on_diagonal else logits[a]
                p = jnp.exp(s)
                dsb = (p * dps[a]).astype(BF16)
                do0 = jnp.where(halves[a], do2, jnp.zeros_like(do2))
                dv_a = lax.dot_general(p.astype(BF16), do0, TN_DIMS, preferred_element_type=F32)
                dv = dv_a if dv is None else dv + dv_a
                dk_scr[a] += lax.dot_general(dsb, qas[a], TN_DIMS, preferred_element_type=F32)
                dq_scr[a, rows, :] += jnp.dot(dsb, km_scr[a], preferred_element_type=F32)
            dv_scr[...] += dv

        def below_diagonal(n, carry):
            update(j + 1 + 2 * n, False)
            update(j + 2 + 2 * n, False)
            return carry

        update(j, True)
        below = nq - 1 - j
        lax.fori_loop(0, below // 2, below_diagonal, 0)

        @pl.when(below % 2 == 1)
        def _():
            update(nq - 1, False)


        def to_head_lanes(old, first, second):
            at = lax.broadcasted_iota(jnp.int32, old.shape, 1) - 2 * pair
            return jnp.where(at == 0, first, jnp.where(at == 1, second, old))

        dk_ref[...] = jnp.where(halves[0], dk_scr[0], dk_scr[1]).astype(BF16)
        dv_ref[...] = dv_scr[...].astype(BF16)
        keys = pl.ds(pl.multiple_of(j * tq, tq), tq)
        ones_a, ones_b = HEAD_DIM + SLOT_ONE, SLOT_ONE
        dcols_ref[keys, :] = to_head_lanes(dcols_ref[keys, :], dk_scr[0][:, ones_a:ones_a + 1],
                                           dk_scr[1][:, ones_b:ones_b + 1])

        @pl.when(j == nq - 1)
        def _():
            row_lane, row_halves = _head_halves(S)
            dq_ref[...] = (jnp.where(row_halves[0], dq_scr[0], dq_scr[1]) * scale).astype(BF16)
            c_a, c_b = HEAD_DIM + SLOT_C, SLOT_C
            drows_ref[...] = to_head_lanes(drows_ref[...], dq_scr[0][:, c_a:c_a + 1], dq_scr[1][:, c_b:c_b + 1])

    tile = lambda col: pl.BlockSpec((tq, LANES), lambda p, j: (j, col + p))
    whole = lambda col: pl.BlockSpec((S, LANES), lambda p, j: (0, col + p))
    shared = pl.BlockSpec((S, LANES), lambda p, j: (0, 0))
    return pl.pallas_call(
        body, name="attn_bwd", grid=(P, nq),
        in_specs=[whole(qc), tile(kc), tile(vc), whole(P), whole(0), tile(0), whole(0)] + _after_specs(after),
        out_specs=[whole(0), tile(0), tile(0), shared, shared],
        out_shape=[jax.ShapeDtypeStruct((S, W), BF16),
                   jax.ShapeDtypeStruct((S, W), BF16),
                   jax.ShapeDtypeStruct((S, W), BF16),
                   jax.ShapeDtypeStruct((S, LANES), F32),
                   jax.ShapeDtypeStruct((S, LANES), F32)],
        scratch_shapes=[pltpu.VMEM((2, tq, LANES), BF16),
                        pltpu.VMEM((2, tq, LANES), BF16),
                        pltpu.VMEM((2, tq, LANES), F32),
                        pltpu.VMEM((tq, LANES), F32),
                        pltpu.VMEM((2, S, LANES), F32)],
        compiler_params=_params("arbitrary", "arbitrary"),
    )(proj, proj, proj, da, qaug, kaug, doaug, *after)


def _fgate_bwd(drows, dcols, z, bias):
    S = z.shape[0]
    tb = min(TB, S)
    nb = S // tb

    def body(drows_ref, dcols_ref, z_ref, b_ref, dz_ref, db_ref):
        tri = (lax.broadcasted_iota(jnp.int32, (tb, tb), 1)
               >= lax.broadcasted_iota(jnp.int32, (tb, tb), 0)).astype(BF16)

        local = []
        for b in range(nb):
            rows = slice(b * tb, (b + 1) * tb)
            local.append(_dot_exact_left(tri, drows_ref[rows, :] - dcols_ref[rows, :]))
        carry = jnp.zeros((1, LANES), F32)
        db = jnp.zeros((1, LANES), F32)
        for b in reversed(range(nb)):
            rows = slice(b * tb, (b + 1) * tb)
            rc = local[b] + carry
            carry = rc[0:1, :]
            dz = rc * _sigmoid(-(z_ref[rows, :] + b_ref[...]))
            dz_ref[rows, :] = dz.astype(BF16)
            db = db + jnp.sum(dz, axis=0, keepdims=True)
        db_ref[...] = db

    return pl.pallas_call(
        body, name="fgate_bwd",
        out_shape=[jax.ShapeDtypeStruct((S, LANES), BF16),
                   jax.ShapeDtypeStruct((1, LANES), F32)],
        compiler_params=pltpu.CompilerParams(vmem_limit_bytes=VMEM_LIMIT),
    )(drows, dcols, z, bias)


def _pool_bwd(proj, da, pool_w, pool_scale):
    S = proj.shape[0]
    G = len(POOL_WINDOWS)

    def body(u_ref, dy_ref, w_ref, s_ref, du_ref, dw_ref, ds_ref, pad_ref):
        g = pl.program_id(0)
        for gi, w in enumerate(POOL_WINDOWS):
            @pl.when(g == gi)
            def _():
                d, cnt = _window_mean_minus_self(u_ref[...].astype(F32), pad_ref, w, S)
                db = d.astype(BF16)
                wb = w_ref[0].astype(BF16)
                yraw = jnp.dot(db, wb, preferred_element_type=F32)
                dy = dy_ref[...].astype(F32)
                ds_ref[...] = jnp.sum(dy * yraw, axis=0, keepdims=True)
                dzb = (dy * s_ref[...]).astype(BF16)
                dw_ref[0] = lax.dot_general(db, dzb, TN_DIMS, preferred_element_type=F32)
                dd = lax.dot_general(dzb, wb, NT, preferred_element_type=F32)
                pad_ref[0:S, :] = dd / cnt
                pad_ref[S:S + MAX_WINDOW, :] = jnp.zeros((MAX_WINDOW, LANES), F32)
                acc = -dd
                for j in range(w):
                    acc = acc + pad_ref[j:j + S, :]
                du_ref[...] = acc.astype(BF16)

    return pl.pallas_call(
        body, name="pool_bwd", grid=(G,),
        in_specs=[pl.BlockSpec((S, LANES), lambda g: (0, g)),
                  pl.BlockSpec((S, LANES), lambda g: (0, g)),
                  pl.BlockSpec((1, LANES, LANES), lambda g: (g, 0, 0)),
                  pl.BlockSpec((1, LANES), lambda g: (0, g))],
        out_specs=[pl.BlockSpec((S, LANES), lambda g: (0, g)),
                   pl.BlockSpec((1, LANES, LANES), lambda g: (g, 0, 0)),
                   pl.BlockSpec((1, LANES), lambda g: (0, g))],
        out_shape=[jax.ShapeDtypeStruct((S, G * LANES), BF16),
                   jax.ShapeDtypeStruct((G, LANES, LANES), F32),
                   jax.ShapeDtypeStruct((1, G * LANES), F32)],
        scratch_shapes=[pltpu.VMEM((S + MAX_WINDOW, LANES), F32)],
        compiler_params=_params("arbitrary"),
    )(proj, da, pool_w, pool_scale)


def _adamw(w, m, v, gsets, name, rows, shifted=False, first=0, into=None):
    A, R, C = w.shape
    n_sets = len(gsets)
    tr = min(rows, R)
    c1 = 1.0 / (1.0 - ADAM_B1 ** ADAM_STEP)
    c2 = 1.0 / (1.0 - ADAM_B2 ** ADAM_STEP)
    counts = [len(gs) for gs in gsets]

    def body(w_ref, m_ref, v_ref, *rest):
        g_ref, d_ref, nm_ref, nv_ref = rest[-4:]
        at = 0
        for a in range(n_sets):
            part_refs = rest[at:at + counts[a]]
            at += counts[a]

            @pl.when(pl.program_id(0) == a)
            def _():
                g = None
                for ref in part_refs:
                    for s in range(ref.shape[0]):
                        term = ref[s].astype(F32)
                        g = term if g is None else g + term
                if shifted:
                    lanes = g.shape[1]
                    g = pltpu.roll(g, (lanes - _index(_position()) * (C % LANES)) % lanes, axis=1)[:, :C]
                nm = ADAM_B1 * m_ref[0] + (1.0 - ADAM_B1) * g
                nv = ADAM_B2 * v_ref[0] + (1.0 - ADAM_B2) * (g * g)
                g_ref[0] = g
                nm_ref[0] = nm
                nv_ref[0] = nv
                d_ref[0] = -ADAM_LR * ((nm * c1) / (jnp.sqrt(nv * c2) + ADAM_EPS) + ADAM_WD * w_ref[0])

    spec = pl.BlockSpec((1, tr, C), lambda a, r: (first + a, r, 0))
    part_specs = [pl.BlockSpec((part.shape[0], tr, part.shape[2]), lambda a, r, l=l: (0, jnp.where(a == l, r, 0), 0))
                  for l, gs in enumerate(gsets) for part in gs]
    parts = [part for gs in gsets for part in gs]
    shape = jax.ShapeDtypeStruct((A, R, C), F32)
    earlier = () if into is None else tuple(into)
    return pl.pallas_call(
        body, name=name, grid=(n_sets, R // tr),
        in_specs=[spec, spec, spec] + part_specs + _after_specs(earlier),
        out_specs=[spec, spec, spec, spec],
        out_shape=[shape, shape, shape, shape],
        input_output_aliases={3 + len(parts) + n: n for n in range(len(earlier))},
        compiler_params=_params("arbitrary", "arbitrary"),
    )(w, m, v, *parts, *earlier)


def _position():
    return lax.axis_index("x"), lax.axis_index("y"), lax.axis_index("c")


def _index(dev):
    return 4 * dev[0] + 2 * dev[1] + dev[2]


def _all_gather(arrs, slots, out_shapes, name):
    n_arr = len(arrs)

    def body(*refs):
        ins, outs = refs[:n_arr], refs[n_arr:2 * n_arr]
        send_sems, recv_sems, local_sems = refs[2 * n_arr:]
        x, y, c = _position()
        me, sibling = (x, y, c), (x, y, 1 - c)
        chips = [(1 - x, y), (x, 1 - y), (1 - x, 1 - y)]

        def copy(a, k, block, to, src=None):
            part = slots[a](outs[a], _index(block))
            return pltpu.make_async_remote_copy(
                src_ref=part if src is None else src, dst_ref=part,
                send_sem=send_sems.at[a, k], recv_sem=recv_sems.at[a, k],
                device_id=to, device_id_type=MESH)

        mine = [pltpu.make_async_copy(ins[a], slots[a](outs[a], _index(me)), local_sems.at[a])
                for a in range(n_arr)]
        for cp in mine:
            cp.start()
        first = []
        for a in range(n_arr):
            first.append(copy(a, 0, me, sibling, src=ins[a]))
            first += [copy(a, 1 + j, me, (*chip, c), src=ins[a]) for j, chip in enumerate(chips)]
        for cp in first:
            cp.start()
        passed = []
        for j, chip in enumerate(chips):
            for a in range(n_arr):
                copy(a, 1 + j, (*chip, c), me).wait_recv()
                fwd = copy(a, 4 + j, (*chip, c), sibling)
                fwd.start()
                passed.append(fwd)
        for a in range(n_arr):
            copy(a, 0, sibling, me).wait_recv()
            for j, chip in enumerate(chips):
                copy(a, 4 + j, (*chip, 1 - c), me).wait_recv()
        for cp in first + passed:
            cp.wait_send()
        for cp in mine:
            cp.wait()

    any_spec = pl.BlockSpec(memory_space=pl.ANY)
    return pl.pallas_call(
        body, name=name,
        in_specs=[any_spec] * n_arr, out_specs=[any_spec] * n_arr, out_shape=out_shapes,
        scratch_shapes=[pltpu.SemaphoreType.DMA((n_arr, 7)), pltpu.SemaphoreType.DMA((n_arr, 7)),
                        pltpu.SemaphoreType.DMA((n_arr,))],
    )(*arrs)


def _split_copies(srcs, lands, send_sems, recv_sems, kinds):
    x, y, c = _position()
    me = _index((x, y, c))
    copies = []
    for a, (src_part, land_part) in enumerate(kinds):
        for k in range(1, N_DEV):
            peer = (x ^ ((k >> 2) & 1), y ^ ((k >> 1) & 1), c ^ (k & 1))
            copies.append(pltpu.make_async_remote_copy(
                src_ref=src_part(srcs[a], _index(peer)), dst_ref=land_part(lands[a], me, k),
                send_sem=send_sems[a].at[k - 1], recv_sem=recv_sems[a].at[k - 1],
                device_id=peer, device_id_type=MESH))
    return copies


def _split_start(srcs, lands, kinds, name, after=()):
    n = len(srcs)

    def body(*refs):
        src_refs, land_refs = refs[:n], refs[n:2 * n]
        outs = refs[2 * n + len(after):]
        send_sems, recv_sems = outs[:n], outs[n:2 * n]
        token = outs[-1]
        for cp in _split_copies(src_refs, land_refs, send_sems, recv_sems, kinds):
            cp.start()
        token[...] = jnp.zeros(token.shape, token.dtype)

    hbm = pl.BlockSpec(memory_space=pltpu.HBM)
    sem = pl.BlockSpec(memory_space=pltpu.SEMAPHORE)
    operands = [pltpu.with_memory_space_constraint(t, pltpu.HBM) for t in (*srcs, *lands)]
    out = pl.pallas_call(
        body, name=name,
        in_specs=[hbm] * (2 * n) + _after_specs(after),
        out_specs=[sem] * (2 * n) + [hbm] * (2 * n) + [pl.BlockSpec(memory_space=pltpu.VMEM)],
        out_shape=[pltpu.SemaphoreType.DMA((N_DEV - 1,))] * (2 * n)
        + [pltpu.HBM(t.shape, t.dtype) for t in operands] + [jax.ShapeDtypeStruct((8, LANES), F32)],
        input_output_aliases={i: 2 * n + i for i in range(2 * n)},
        compiler_params=pltpu.CompilerParams(has_side_effects=pltpu.SideEffectType.DATAFLOW_SIDE_EFFECTING),
    )(*operands, *after)
    return [(out[a], out[n + a], out[2 * n + a], out[3 * n + a]) for a in range(n)], out[-1]


def _split_wait(started, kinds, after, name):
    n = len(started)
    sems = [t[0] for t in started] + [t[1] for t in started]
    srcs = [t[2] for t in started]
    lands = [t[3] for t in started]

    def body(*refs):
        src_refs, land_refs = refs[:n], refs[n:2 * n]
        send_sems, recv_sems = refs[2 * n:3 * n], refs[3 * n:4 * n]
        for cp in _split_copies(src_refs, land_refs, send_sems, recv_sems, kinds):
            cp.wait_send()
            cp.wait_recv()

    hbm = pl.BlockSpec(memory_space=pltpu.HBM)
    sem = pl.BlockSpec(memory_space=pltpu.SEMAPHORE)
    out = pl.pallas_call(
        body, name=name,
        in_specs=[hbm] * (2 * n) + [sem] * (2 * n) + _after_specs(after),
        out_specs=[hbm] * (2 * n),
        out_shape=[pltpu.HBM(t.shape, t.dtype) for t in (*srcs, *lands)],
        input_output_aliases={i: i for i in range(2 * n)},
        compiler_params=pltpu.CompilerParams(has_side_effects=pltpu.SideEffectType.DATAFLOW_SIDE_EFFECTING),
    )(*srcs, *lands, *sems, *after)
    return out[n:]


def _as_rows(p):
    if p.size % LANES == 0:
        rows = p.reshape(-1, LANES)
    else:
        rows = p.reshape(-1, p.shape[-1])
        rows = jnp.pad(rows, ((0, 0), (0, LANES - rows.shape[1])))
    return jnp.pad(rows, ((0, -rows.shape[0] % 8), (0, 0)))


def _pack(parts):
    return jnp.concatenate([_as_rows(p) for p in parts])[None]


def _unpack(packed, like):
    out, at = [], 0
    for p in like:
        whole = p.size % LANES == 0
        n = p.size // LANES if whole else p.size // p.shape[-1]
        rows = packed[0, at:at + n]
        out.append((rows if whole else rows[:, :p.shape[-1]]).reshape(p.shape))
        at += n + (-n % 8)
    return out


def _local_step(x, target, norm_g, forget_bias, pool_w, pool_scale, final_g, weights_in, weights_out, on_grads,
                first_after=(), on_first_out_grads=lambda d_wout, d_wout_bf16: ()):
    L = norm_g.shape[0]
    S, D = x.shape
    W = D // 2
    H = W // HEAD_DIM
    bias = jnp.pad(forget_bias, ((0, 0), (0, LANES - H)))

    saved = []
    nxt = _inproj_fwd(x, norm_g[0:1], weights_in(0, x), tuple(first_after))
    for l in range(L):
        x_in, (proj, h, z, w) = x, nxt
        qaug, kaug = _fgate_fwd(z, bias[l:l + 1], H)
        ypool = _pool_fwd(proj, pool_w[l], pool_scale[l:l + 1])
        o, qaug_b = _attn_fwd(proj, qaug, kaug)
        wout = weights_out(l, o)
        if l < L - 1:
            x, *nxt = _outproj_fwd(ypool, o, proj, x_in, wout, nxt=(norm_g[l + 1:l + 2], weights_in(l + 1, o)))
        else:
            g, loss, d_final_g = _outproj_fwd(ypool, o, proj, x_in, wout, head=(final_g.reshape(1, D), target))
        saved.append((x_in, proj, h, z, qaug_b, kaug, ypool, o, w, wout))

    small, after = None, ()
    for l in reversed(range(L)):
        x_in, proj, h, z, qaug_b, kaug, ypool, o, w, wout = saved[l]
        d_wout, d_wout_bf16, da, dgate, doaug = _outproj_bwd(g, wout, ypool, o, proj, after)
        early = tuple(on_first_out_grads(d_wout, d_wout_bf16)) if l == 0 else ()
        dq, dk, dv, drows, dcols = _attn_bwd(proj, da, qaug_b, kaug, doaug, early)
        dzf, db = _fgate_bwd(drows, dcols, z, bias[l:l + 1])
        dpu, dpw, dps = _pool_bwd(proj, da, pool_w[l], pool_scale[l:l + 1])
        dproj = [(dpu, 0), (dgate, 0), (dq, 0), (dk, 0), (dv, 0), (dgate, 1)]
        wxg = (w, x_in, norm_g[l:l + 1], g)
        if l > 0:
            d_w, d_w_bf16, g, dgam = _inproj_bwd(dproj, dzf, h, wxg)
            after = tuple(on_grads(l, d_w, d_w_bf16, d_wout, d_wout_bf16, small))
        else:
            d_w, d_w_bf16 = _inproj_bwd(dproj, dzf, h)
            after = tuple(on_grads(l, d_w, d_w_bf16, d_wout, d_wout_bf16, small))
            g, dgam = _inproj_bwd(dproj, dzf, None, wxg, after)
        small = (dgam[0], db[0, :H], dpw, dps[0])
    return loss[0, 0], g, small, d_final_g[0]


def kernel(x, norm_g, w_in, forget_bias, pool_w, pool_scale, w_out, final_g, loss_target, m_norm_g, m_w_in, m_forget_bias, m_pool_w, m_pool_scale, m_w_out, m_final_g, v_norm_g, v_w_in, v_forget_bias, v_pool_w, v_pool_scale, v_w_out, v_final_g):
    L, D, cols = w_in.shape
    rows_out = w_out.shape[1]
    me = _index(_position())
    slot = _slot_width(cols)
    wout_b = w_out.astype(BF16)
    win_b = _shift_slots(w_in)
    gather_in = (lambda ref, peer: ref, lambda ref, mine, k: ref.at[mine])
    gather_out = (lambda ref, peer: ref, lambda ref, mine, k: ref.at[pl.ds(mine * rows_out, rows_out), :])

    def landing(block, n_slots):
        zone = lax.empty((n_slots * block.shape[0], *block.shape[1:]), block.dtype)
        return lax.dynamic_update_slice(zone, block, (me * block.shape[0],) + (0,) * (block.ndim - 1))

    (first_in,) = _all_gather([win_b[0]], [lambda ref, n: ref.at[n]],
                              [jax.ShapeDtypeStruct((N_DEV, D, slot), BF16)], "gather_first")
    rest_srcs = [wout_b[0]] + [w[l] for l in range(1, L) for w in (win_b, wout_b)]
    rest_lands = [landing(wout_b[0], N_DEV)]
    for l in range(1, L):
        rest_lands += [landing(win_b[l][None], N_DEV), landing(wout_b[l], N_DEV)]
    rest_kinds = [gather_out] + [gather_in, gather_out] * (L - 1)
    rest, rest_token = _split_start(rest_srcs, rest_lands, rest_kinds, "gather_start_rest", (first_in,))

    def weights_in(l, x_in):
        if l == 0:
            return first_in
        (win_all,) = _split_wait([rest[2 * l - 1]], [gather_in], (x_in,), f"gather_wait_in_{l}")
        return win_all

    def weights_out(l, o):
        (wout_full,) = _split_wait([rest[2 * l]], [gather_out], (o,), f"gather_wait_out_{l}")
        return wout_full

    stride = slot - LANES
    exchange_kinds = [(lambda ref, peer: ref.at[:, pl.ds(pl.multiple_of(peer * stride, LANES), slot)],
                       lambda ref, mine, k: ref.at[k - 1]),
                      (lambda ref, peer: ref.at[pl.ds(peer * rows_out, rows_out), :],
                       lambda ref, mine, k: ref.at[k - 1])]
    zero_g = jnp.zeros_like(final_g)
    zero_loss = jnp.zeros((LANES,), F32)

    def small_pack(l, norm_g_l, bias_l, pool_w_l, pool_scale_l, final, loss_row=None):
        return _pack([norm_g_l, bias_l, pool_w_l, pool_scale_l, final if l == 0 else zero_g,
                      zero_loss if loss_row is None else loss_row])[0]

    exchanges, own_parts, first_out = {}, {}, []

    def on_first_out_grads(d_wout, d_wout_bf16):
        started, token = _split_start([d_wout_bf16], [lax.empty((N_DEV - 1, rows_out, D), BF16)], exchange_kinds[1:],
                                      "exchange_start_out_0")
        first_out.extend(started)
        return (token,)

    def on_grads(l, dw, dw_bf16, d_wout, d_wout_bf16, small):
        own_parts[l] = (lax.dynamic_slice_in_dim(dw, me * stride, slot, 1)[None],
                        lax.dynamic_slice_in_dim(d_wout, me * rows_out, rows_out, 0)[None])
        out_on_its_way = l == 0
        srcs, lands, kinds = [dw_bf16], [lax.empty((N_DEV - 1, D, slot), BF16)], exchange_kinds[:1]
        if not out_on_its_way:
            srcs.append(d_wout_bf16)
            lands.append(lax.empty((N_DEV - 1, rows_out, D), BF16))
            kinds = list(exchange_kinds)
        if small is not None:
            packed_small = small_pack(l + 1, *small, None)
            srcs.append(packed_small)
            lands.append(landing(packed_small[None], N_DEV))
            kinds.append(gather_in)
        started, token = _split_start(srcs, lands, kinds, f"exchange_start_{l}")
        if out_on_its_way:
            started[1:1], kinds[1:1] = first_out, exchange_kinds[1:]
        exchanges[l] = (started, kinds)
        return (token,)

    loss, dx, small_first, d_final_g = _local_step(
        x[0], loss_target[0], norm_g, forget_bias, pool_w, pool_scale, final_g,
        weights_in, weights_out, on_grads, (rest_token,), on_first_out_grads)
    packed_first = small_pack(0, *small_first, d_final_g, jnp.full((LANES,), loss, F32))
    first_started, first_token = _split_start(
        [packed_first], [landing(packed_first[None], N_DEV)], [gather_in],
        "small_start", (w_in, m_w_in, v_w_in, *own_parts[0]))

    gin_sets, gout_sets, small_sets = [None] * L, [None] * L, [None] * L

    def wait_for(l, after):
        started, kinds = exchanges[l]
        got = _split_wait(started, kinds, after, f"exchange_wait_{l}")
        gin_sets[l] = [own_parts[l][0], got[0]]
        gout_sets[l] = [own_parts[l][1], got[1]]
        if len(got) > 2:
            small_sets[l + 1] = [got[2]]

    for l in range(1, L):
        wait_for(l, (dx, first_token))
    rest_in = _adamw(w_in, m_w_in, v_w_in, gin_sets[1:], "adamw_w_in_rest", TM // 2, shifted=True, first=1)
    rest_out = _adamw(w_out, m_w_out, v_w_out, gout_sets[1:], "adamw_w_out_rest", rows_out, first=1)
    wait_for(0, (rest_in[1], rest_out[1]))
    g_w_in, d_w_in, nm_w_in, nv_w_in = _adamw(w_in, m_w_in, v_w_in, gin_sets[:1], "adamw_w_in_first", TM // 2,
                                              shifted=True, into=rest_in)
    g_w_out, d_w_out, nm_w_out, nv_w_out = _adamw(w_out, m_w_out, v_w_out, gout_sets[:1], "adamw_w_out_first",
                                                  rows_out, into=rest_out)
    small_sets[0] = _split_wait(first_started, [gather_in], (d_w_in, d_w_out), "small_wait")
    loss = jnp.sum(small_sets[0][0][:, packed_first.shape[0] - 8, 0])

    def small_stack(norm_g_, bias_, pool_w_, pool_scale_, final):
        return jnp.stack([small_pack(l, norm_g_[l], bias_[l], pool_w_[l], pool_scale_[l], final) for l in range(L)])

    packed = _adamw(small_stack(norm_g, forget_bias, pool_w, pool_scale, final_g),
                    small_stack(m_norm_g, m_forget_bias, m_pool_w, m_pool_scale, m_final_g),
                    small_stack(v_norm_g, v_forget_bias, v_pool_w, v_pool_scale, v_final_g),
                    small_sets, "adamw_small", packed_first.shape[0])

    def small_unpack(p):
        like = [norm_g[0], forget_bias[0], pool_w[0], pool_scale[0], final_g]
        layers = [_unpack(p[l:l + 1], like) for l in range(L)]
        return [jnp.stack([layers[l][n] for l in range(L)]) for n in range(4)] + [layers[0][4]]

    g_s, d_s, nm_s, nv_s = [small_unpack(p) for p in packed]

    def order(big_in, big_out, small):
        return (small[0], big_in, small[1], small[2], small[3], big_out, small[4])

    return (loss, dx[None], *order(g_w_in, g_w_out, g_s), *order(d_w_in, d_w_out, d_s),
            *order(nm_w_in, nm_w_out, nm_s), *order(nv_w_in, nv_w_out, nv_s))
```

```python
import math

import jax
import jax.numpy as jnp
from jax import lax
from jax.experimental import pallas as pl
from jax.experimental.pallas import tpu as pltpu

F32 = jnp.float32
BF16 = jnp.bfloat16
MESH = pl.DeviceIdType.MESH

RMS_EPS = 1e-6
NEG_INF = -1e30
HEAD_DIM = 64
POOL_WINDOWS = (2, 4, 8, 16)
MAX_WINDOW = 16
LANES = 128
N_DEV = 8

ADAM_LR = 0.001
ADAM_B1 = 0.9
ADAM_B2 = 0.999
ADAM_EPS = 1e-08
ADAM_WD = 0.01
ADAM_STEP = 10

TM = 512
TN = 512
TQ = 512
TB = 256
VMEM_LIMIT = 56 * 1024 * 1024

NT = (((1,), (1,)), ((), ()))
TN_DIMS = (((0,), (0,)), ((), ()))

SLOT_C, SLOT_ONE, SLOT_LSE = 0, 3, 6


def _params(*sem):
    return pltpu.CompilerParams(dimension_semantics=sem, vmem_limit_bytes=VMEM_LIMIT)


def _sigmoid(x):
    return 1.0 / (1.0 + jnp.exp(-x))


def _split3(x):
    hi = x.astype(BF16).astype(F32)
    rest = x - hi
    mid = rest.astype(BF16).astype(F32)
    return hi, mid, rest - mid


def _dot_exact_left(a, x):
    return sum(jnp.dot(a, part.astype(BF16), preferred_element_type=F32) for part in reversed(_split3(x)))


def _after_specs(after):
    return [pl.BlockSpec(memory_space=pl.ANY)] * len(after)


def _slot_width(cols):
    return LANES * (-(-(cols + (N_DEV - 1) * (cols % LANES)) // LANES))


def _shift_slots(w_in):
    L, D, cols = w_in.shape
    slot = _slot_width(cols)
    tr = min(TM // 2, D)

    def body(w_ref, o_ref, pad_scr):
        pad_scr[...] = jnp.zeros(pad_scr.shape, F32)
        pad_scr[:, 0:cols] = w_ref[0]
        o_ref[0] = pltpu.roll(pad_scr[...], _index(_position()) * (cols % LANES), axis=1).astype(BF16)

    return pl.pallas_call(
        body, name="shift_slots", grid=(L, D // tr),
        in_specs=[pl.BlockSpec((1, tr, cols), lambda l, r: (l, r, 0))],
        out_specs=pl.BlockSpec((1, tr, slot), lambda l, r: (l, r, 0)),
        out_shape=jax.ShapeDtypeStruct((L, D, slot), BF16),
        scratch_shapes=[pltpu.VMEM((tr, slot), F32)],
        compiler_params=_params("parallel", "parallel"),
    )(w_in)


def _inproj_tile(xf, g_ref, s_ref, proj_ref, h_ref, z_ref, w_ref):
    n_dev, _, sw = s_ref.shape
    stride = sw - LANES
    width = w_ref.shape[1]
    N = width - LANES

    @pl.when(pl.program_id(0) == 0)
    def _():
        for n in range(n_dev):
            base = stride * n
            first = s_ref[n, :, 0:LANES]
            if n > 0:
                first = first + s_ref[n - 1, :, stride:sw]
            w_ref[:, base:base + LANES] = first
            w_ref[:, base + LANES:base + stride] = s_ref[n, :, LANES:stride]
        w_ref[:, stride * n_dev:width] = s_ref[n_dev - 1, :, stride:sw]

    r = lax.rsqrt(jnp.mean(xf * xf, axis=-1, keepdims=True) + RMS_EPS)
    h = ((xf * r) * g_ref[...]).astype(BF16)
    h_ref[...] = h
    z_ref[...] = jnp.dot(h, w_ref[:, N:width], preferred_element_type=F32)
    for n in range(N // TN):
        cols = slice(n * TN, (n + 1) * TN)
        proj_ref[:, cols] = jnp.dot(h, w_ref[:, cols], preferred_element_type=F32).astype(BF16)


def _inproj_specs(slots, S, D, tm):
    n_dev, _, sw = slots.shape
    width = (sw - LANES) * n_dev + LANES
    N = width - LANES
    once = pl.Buffered(1)
    in_specs = [pl.BlockSpec((1, D), lambda i: (0, 0)),
                pl.BlockSpec((n_dev, D, sw), lambda i: (0, 0, 0), pipeline_mode=once)]
    out_specs = [pl.BlockSpec((tm, N), lambda i: (i, 0)),
                 pl.BlockSpec((tm, D), lambda i: (i, 0)),
                 pl.BlockSpec((tm, LANES), lambda i: (i, 0)),
                 pl.BlockSpec((D, width), lambda i: (0, 0), pipeline_mode=once)]
    out_shape = [jax.ShapeDtypeStruct((S, N), BF16),
                 jax.ShapeDtypeStruct((S, D), BF16),
                 jax.ShapeDtypeStruct((S, LANES), F32),
                 jax.ShapeDtypeStruct((D, width), BF16)]
    return in_specs, out_specs, out_shape


def _inproj_fwd(x, gam, slots, after=()):
    S, D = x.shape
    tm = min(TM, S)

    def body(x_ref, g_ref, s_ref, *rest):
        _inproj_tile(x_ref[...], g_ref, s_ref, *rest[-4:])

    in_specs, out_specs, out_shape = _inproj_specs(slots, S, D, tm)
    return pl.pallas_call(
        body, name="inproj_fwd", grid=(S // tm,),
        in_specs=[pl.BlockSpec((tm, D), lambda i: (i, 0))] + in_specs + _after_specs(after),
        out_specs=out_specs, out_shape=out_shape,
        compiler_params=_params("arbitrary"),
    )(x, gam, slots, *after)


def _fgate_fwd(z, bias, n_heads):
    S = z.shape[0]
    tb = min(TB, S)
    P = n_heads // 2
    assert n_heads <= 8, "the three parts of c are packed eight lanes apart"

    def body(z_ref, b_ref, qaug_ref, kaug_ref):
        lane = lax.broadcasted_iota(jnp.int32, (tb, LANES), 1)
        tri = (lax.broadcasted_iota(jnp.int32, (tb, tb), 0)
               >= lax.broadcasted_iota(jnp.int32, (tb, tb), 1)).astype(BF16)
        row = lax.broadcasted_iota(jnp.int32, (LANES, P * LANES), 0)
        col = lax.broadcasted_iota(jnp.int32, (LANES, P * LANES), 1)
        head, part_n = row & 7, row >> 3
        home = (head >> 1) * LANES + jnp.where((head & 1) == 0, HEAD_DIM, 0)
        is_part = jnp.logical_and(head < n_heads, part_n < 3)
        place_q = jnp.logical_and(is_part, col == home + SLOT_C + part_n).astype(BF16)
        place_k = jnp.logical_and(is_part, col == home + SLOT_ONE + part_n).astype(BF16)
        slot = lax.broadcasted_iota(jnp.int32, (tb, P * LANES), 1) & (HEAD_DIM - 1)
        q_ones = jnp.logical_and(slot >= SLOT_ONE, slot < SLOT_ONE + 3).astype(F32)
        k_ones = jnp.logical_or(slot < SLOT_C + 3,
                                jnp.logical_and(slot >= SLOT_LSE, slot < SLOT_LSE + 3)).astype(F32)

        local = []
        for b in range(S // tb):
            zz = z_ref[b * tb:(b + 1) * tb, :] + b_ref[...]
            lf = jnp.minimum(zz, 0.0) - jnp.log(1.0 + jnp.exp(-jnp.abs(zz)))
            lf = jnp.where(lane < n_heads, lf, 0.0)
            local.append(_dot_exact_left(tri, lf))
        carry = jnp.zeros((1, LANES), F32)
        for b, part_sum in enumerate(local):
            c = part_sum + carry
            carry = c[tb - 1:tb, :]
            hi, mid, lo = _split3(c)
            packed = (hi + pltpu.roll(mid, 8, axis=1) + pltpu.roll(lo, 16, axis=1)).astype(BF16)
            qaug_ref[b * tb:(b + 1) * tb, :] = (
                q_ones + jnp.dot(packed, place_q, preferred_element_type=F32)).astype(BF16)
            kaug_ref[b * tb:(b + 1) * tb, :] = (
                k_ones - jnp.dot(packed, place_k, preferred_element_type=F32)).astype(BF16)

    return pl.pallas_call(
        body, name="fgate_fwd",
        out_shape=[jax.ShapeDtypeStruct((S, P * LANES), BF16),
                   jax.ShapeDtypeStruct((S, P * LANES), BF16)],
        compiler_params=pltpu.CompilerParams(vmem_limit_bytes=VMEM_LIMIT),
    )(z, bias)


def _inner_window(w):
    inner = 1 << ((w.bit_length() - 1) // 2)
    assert w % inner == 0
    return inner


def _window_mean_minus_self(u, pad_ref, w, S):
    pad_ref[0:MAX_WINDOW, :] = jnp.zeros((MAX_WINDOW, LANES), F32)
    pad_ref[MAX_WINDOW:MAX_WINDOW + S, :] = u
    inner = _inner_window(w)
    acc = u
    for j in range(1, inner):
        acc = acc + pad_ref[MAX_WINDOW - j:MAX_WINDOW - j + S, :]
    if inner > 1:
        pad_ref[MAX_WINDOW:MAX_WINDOW + S, :] = acc
    for j in range(inner, w, inner):
        acc = acc + pad_ref[MAX_WINDOW - j:MAX_WINDOW - j + S, :]
    t = lax.broadcasted_iota(jnp.int32, (S, LANES), 0)
    cnt = jnp.minimum(t + 1, w).astype(F32)
    return acc / cnt - u, cnt


def _pool_fwd(proj, pool_w, pool_scale):
    S = proj.shape[0]
    G = len(POOL_WINDOWS)

    def body(u_ref, w_ref, s_ref, y_ref, pad_ref):
        g = pl.program_id(0)
        for gi, w in enumerate(POOL_WINDOWS):
            @pl.when(g == gi)
            def _():
                d, _ = _window_mean_minus_self(u_ref[...].astype(F32), pad_ref, w, S)
                y = jnp.dot(d.astype(BF16), w_ref[0].astype(BF16), preferred_element_type=F32)
                y_ref[...] = (y * s_ref[...]).astype(BF16)

    return pl.pallas_call(
        body, name="pool_fwd", grid=(G,),
        in_specs=[pl.BlockSpec((S, LANES), lambda g: (0, g)),
                  pl.BlockSpec((1, LANES, LANES), lambda g: (g, 0, 0)),
                  pl.BlockSpec((1, LANES), lambda g: (0, g))],
        out_specs=pl.BlockSpec((S, LANES), lambda g: (0, g)),
        out_shape=jax.ShapeDtypeStruct((S, G * LANES), BF16),
        scratch_shapes=[pltpu.VMEM((S + MAX_WINDOW, LANES), F32)],
        compiler_params=_params("arbitrary"),
    )(proj, pool_w, pool_scale)


def _head_halves(rows):
    lane = lax.broadcasted_iota(jnp.int32, (rows, LANES), 1)
    return lane, (lane < HEAD_DIM, lane >= HEAD_DIM)


def _attn_fwd(proj, qaug, kaug):
    S = proj.shape[0]
    W = proj.shape[1] // 6
    P = W // LANES
    tk = min(TQ, S // 2)
    tq = 2 * tk
    nq = S // tq
    qc, kc, vc = 2 * P, 3 * P, 4 * P
    scale = 1.0 / math.sqrt(HEAD_DIM)

    def body(q_ref, k_ref, v_ref, qa_ref, ka_ref, o_ref, qb_ref, qm_scr, m_scr, acc_scr):
        i = pl.program_id(1)
        lane, halves = _head_halves(tq)
        key_halves = (halves[0][:tk], halves[1][:tk])
        v_ones = ((lane[:tk] & (HEAD_DIM - 1)) < 3).astype(BF16)
        qs = q_ref[...] * scale
        qm_scr[0] = jnp.where(halves[0], qs, qa_ref[...])
        qm_scr[1] = jnp.where(halves[1], qs, qa_ref[...])
        m_scr[...] = jnp.full(m_scr.shape, NEG_INF, F32)
        acc_scr[...] = jnp.zeros(acc_scr.shape, F32)
        top, bottom, both = slice(0, tk), slice(tk, tq), slice(0, tq)

        def update(rows, j, on_diagonal):
            keys = pl.ds(pl.multiple_of(j * tk, tk), tk)
            k2, v2, kaug_t = k_ref[keys, :], v_ref[keys, :], ka_ref[keys, :]
            if on_diagonal:
                keep = (lax.broadcasted_iota(jnp.int32, (tk, tk), 0)
                        >= lax.broadcasted_iota(jnp.int32, (tk, tk), 1))
            logits = [lax.dot_general(qm_scr[a, rows, :], jnp.where(key_halves[a], k2, kaug_t), NT,
                                      preferred_element_type=F32) for a in range(2)]
            for a in range(2):
                s = jnp.where(keep, logits[a], NEG_INF) if on_diagonal else logits[a]
                va = jnp.where(key_halves[a], v2, v_ones)
                m_prev = m_scr[a, rows, :]
                m_new = jnp.maximum(m_prev, jnp.max(s, axis=1, keepdims=True))
                p = jnp.exp(s - jnp.tile(m_new, (1, tk // LANES)))
                acc_scr[a, rows, :] = (jnp.exp(m_prev - m_new) * acc_scr[a, rows, :]
                                       + jnp.dot(p.astype(BF16), va, preferred_element_type=F32))
                m_scr[a, rows, :] = m_new

        def below_diagonal(jj, carry):
            update(both, 2 * jj, False)
            update(both, 2 * jj + 1, False)
            return carry

        lax.fori_loop(0, i, below_diagonal, 0)
        update(top, 2 * i, True)
        update(bottom, 2 * i, False)
        update(bottom, 2 * i + 1, True)
        acc_a, acc_b = acc_scr[0], acc_scr[1]
        l_a, l_b = acc_a[:, HEAD_DIM:HEAD_DIM + 1], acc_b[:, 0:1]
        o_ref[...] = jnp.where(halves[0], acc_a / l_a, acc_b / l_b).astype(BF16)
        lse = jnp.where(halves[0], m_scr[1] + jnp.log(l_b), m_scr[0] + jnp.log(l_a))
        slot = lane & (HEAD_DIM - 1)
        aug = qa_ref[...].astype(F32)
        for n, part in enumerate(_split3(lse)):
            aug = jnp.where(slot == SLOT_LSE + n, -part, aug)
        qb_ref[...] = aug.astype(BF16)

    tile = lambda col: pl.BlockSpec((tq, LANES), lambda p, i: (i, col + p))
    whole = lambda col: pl.BlockSpec((S, LANES), lambda p, i: (0, col + p))
    return pl.pallas_call(
        body, name="attn_fwd", grid=(P, nq),
        in_specs=[tile(qc), whole(kc), whole(vc), tile(0), whole(0)],
        out_specs=[tile(0), tile(0)],
        out_shape=[jax.ShapeDtypeStruct((S, W), BF16), jax.ShapeDtypeStruct((S, W), BF16)],
        scratch_shapes=[pltpu.VMEM((2, tq, LANES), BF16),
                        pltpu.VMEM((2, tq, LANES), F32),
                        pltpu.VMEM((2, tq, LANES), F32)],
        compiler_params=_params("parallel", "arbitrary"),
    )(proj, proj, proj, qaug, kaug)


def _outproj_fwd(ypool, o, proj, x, wout, nxt=None, head=None):
    S, D = x.shape
    W = D // 2
    tm, tn = min(TM, S), TN

    def body(y_ref, o_ref, pg_ref, ag_ref, x_ref, w_ref, *rest):
        mix_ref = rest[-1]
        out_ref = rest[2]
        pg, ag = pg_ref[...].astype(F32), ag_ref[...].astype(F32)
        mix_ref[:, 0:W] = (y_ref[...].astype(F32) * (pg * _sigmoid(pg))).astype(BF16)
        mix_ref[:, W:D] = (o_ref[...].astype(F32) * (ag * _sigmoid(ag))).astype(BF16)
        for n in range(D // tn):
            cols = slice(n * tn, (n + 1) * tn)
            out_ref[:, cols] = x_ref[:, cols] + jnp.dot(mix_ref[...], w_ref[:, cols], preferred_element_type=F32)
        if nxt:
            _inproj_tile(out_ref[...], rest[0], rest[1], *rest[3:7])
            return
        gam_ref, t_ref = rest[:2]
        loss_ref, dg_ref = rest[-3:-1]

        @pl.when(pl.program_id(0) == 0)
        def _():
            loss_ref[...] = jnp.zeros(loss_ref.shape, F32)
            dg_ref[...] = jnp.zeros(dg_ref.shape, F32)

        xf, gam_v = out_ref[...], gam_ref[...]
        r = lax.rsqrt(jnp.mean(xf * xf, axis=-1, keepdims=True) + RMS_EPS)
        xhat = xf * r
        err = xhat * gam_v - t_ref[...]
        part = jnp.sum(jnp.sum(err * err, axis=-1, keepdims=True), axis=0, keepdims=True)
        loss_ref[...] += part * (0.5 / D)
        dy = err * (1.0 / D)
        dg_ref[...] += jnp.sum(dy * xhat, axis=0, keepdims=True)
        dxhat = dy * gam_v
        out_ref[...] = r * (dxhat - xhat * jnp.mean(dxhat * xhat, axis=-1, keepdims=True))

    assert (nxt is None) != (head is None)
    rows = lambda width, col: pl.BlockSpec((tm, width), lambda i: (i, col))
    in_specs = [rows(W, 0), rows(W, 0), rows(W, 1), rows(W, 5), rows(D, 0),
                pl.BlockSpec((D, D), lambda i: (0, 0), pipeline_mode=pl.Buffered(1))]
    out_specs = [rows(D, 0)]
    out_shape = [jax.ShapeDtypeStruct((S, D), F32)]
    if nxt:
        more_in, more_out, more_shape = _inproj_specs(nxt[1], S, D, tm)
        in_specs += more_in
        out_specs += more_out
        out_shape += more_shape
    else:
        in_specs += [pl.BlockSpec((1, D), lambda i: (0, 0)), rows(D, 0)]
        out_specs += [pl.BlockSpec((8, LANES), lambda i: (0, 0)), pl.BlockSpec((1, D), lambda i: (0, 0))]
        out_shape += [jax.ShapeDtypeStruct((8, LANES), F32), jax.ShapeDtypeStruct((1, D), F32)]
    return pl.pallas_call(
        body, name="outproj_fwd_loss" if head else "outproj_inproj_fwd", grid=(S // tm,),
        in_specs=in_specs, out_specs=out_specs, out_shape=out_shape,
        scratch_shapes=[pltpu.VMEM((tm, D), BF16)],
        compiler_params=_params("arbitrary"),
    )(ypool, o, proj, proj, x, wout, *(nxt or head))


def _outproj_bwd(g, wout, ypool, o, proj, after=()):
    S, D = g.shape
    W = D // 2
    tm = min(TM, S)

    def body(g_ref, w_ref, y_ref, o_ref, pg_ref, ag_ref, *rest):
        dw_ref, dwb_ref, da_ref, dgate_ref, doaug_ref = rest[-5:]

        @pl.when(pl.program_id(0) == 0)
        def _():
            dw_ref[...] = jnp.zeros(dw_ref.shape, F32)

        gb = g_ref[...].astype(BF16)
        dmixes = [lax.dot_general(gb, w_ref[half * W:(half + 1) * W, :], NT, preferred_element_type=F32)
                  for half in range(2)]
        d_o = None
        for half, (val_ref, gate_ref) in enumerate(((y_ref, pg_ref), (o_ref, ag_ref))):
            cols = slice(half * W, (half + 1) * W)
            gt = gate_ref[...].astype(F32)
            sg = _sigmoid(gt)
            silu = gt * sg
            val = val_ref[...].astype(F32)
            dw_ref[cols, :] += lax.dot_general((val * silu).astype(BF16), gb, TN_DIMS, preferred_element_type=F32)
            d_o = (dmixes[half] * silu).astype(BF16)
            da_ref[:, cols] = d_o
            dgate_ref[:, cols] = (dmixes[half] * val * (sg * (1.0 + gt * (1.0 - sg)))).astype(BF16)

        lane, halves = _head_halves(tm)
        slot = lane & (HEAD_DIM - 1)
        for p in range(W // LANES):
            cols = slice(p * LANES, (p + 1) * LANES)
            prod = d_o[:, cols].astype(F32) * o_ref[:, cols].astype(F32)
            d_a = jnp.sum(jnp.where(halves[0], prod, 0.0), axis=1, keepdims=True)
            d_b = jnp.sum(jnp.where(halves[1], prod, 0.0), axis=1, keepdims=True)
            aug = jnp.zeros((tm, LANES), F32)
            for n, part in enumerate(_split3(jnp.where(halves[0], d_b, d_a))):
                aug = jnp.where(slot == SLOT_C + n, -part, aug)
            doaug_ref[:, cols] = aug.astype(BF16)

        @pl.when(pl.program_id(0) == S // tm - 1)
        def _():
            dwb_ref[...] = dw_ref[...].astype(BF16)

    rows = lambda width, col: pl.BlockSpec((tm, width), lambda i: (i, col))
    whole = pl.BlockSpec((D, D), lambda i: (0, 0))
    return pl.pallas_call(
        body, name="outproj_bwd", grid=(S // tm,),
        in_specs=[rows(D, 0), whole, rows(W, 0), rows(W, 0), rows(W, 1), rows(W, 5)] + _after_specs(after),
        out_specs=[whole, whole, rows(D, 0), rows(D, 0), rows(W, 0)],
        out_shape=[jax.ShapeDtypeStruct((D, D), F32),
                   jax.ShapeDtypeStruct((D, D), BF16),
                   jax.ShapeDtypeStruct((S, D), BF16),
                   jax.ShapeDtypeStruct((S, D), BF16),
                   jax.ShapeDtypeStruct((S, W), BF16)],
        compiler_params=_params("arbitrary"),
    )(g, wout, ypool, o, proj, proj, *after)


def _section_specs(sections, rows, width):
    specs = [pl.BlockSpec((rows, width), lambda k, c=c: (k, c)) for _, c in sections]
    return specs, [a for a, _ in sections]


def _inproj_bwd(sections, dzf, h=None, wxg=None, after=()):
    S = dzf.shape[0]
    n_sec = len(sections)
    D = h.shape[1] if h is not None else wxg[1].shape[1]
    W = D // 2
    N = n_sec * W
    ts = min(TM if wxg else 2 * TM, S)
    n_steps = S // ts
    once = pl.Buffered(1)

    def body(dz_ref, *rest):
        sec_refs, rest = rest[:n_sec], rest[n_sec:]
        if h is not None:
            h_ref, rest = rest[0], rest[1:]
        if wxg:
            (w_ref, wf_ref, x_ref, gam_ref, g_ref), rest = rest[:5], rest[5:]
        outs = rest[len(after):]
        step = pl.program_id(0)

        if h is not None:
            dw_ref, dwb_ref = outs[:2]

            @pl.when(step == 0)
            def _():
                dw_ref[...] = jnp.zeros(dw_ref.shape, F32)

            ht = h_ref[...].T
            dw_ref[:, N:N + LANES] += jnp.dot(ht, dz_ref[...], preferred_element_type=F32)
            for n, ref in enumerate(sec_refs):
                dw_ref[:, n * W:(n + 1) * W] += jnp.dot(ht, ref[...], preferred_element_type=F32)

            @pl.when(step == n_steps - 1)
            def _():
                dwb_ref[...] = dw_ref[...].astype(BF16)

        if wxg:
            dx_ref, dg_ref = outs[-2:]

            @pl.when(step == 0)
            def _():
                dg_ref[...] = jnp.zeros(dg_ref.shape, F32)

            dh = lax.dot_general(dz_ref[...], wf_ref[...], NT, preferred_element_type=F32)
            for n, ref in enumerate(sec_refs):
                dh = dh + lax.dot_general(ref[...], w_ref[:, n * W:(n + 1) * W], NT, preferred_element_type=F32)
            xf = x_ref[...]
            r = lax.rsqrt(jnp.mean(xf * xf, axis=-1, keepdims=True) + RMS_EPS)
            xhat = xf * r
            dg_ref[...] += jnp.sum(dh * xhat, axis=0, keepdims=True)
            dxhat = dh * gam_ref[...]
            dx_ref[...] = g_ref[...] + r * (dxhat - xhat * jnp.mean(dxhat * xhat, axis=-1, keepdims=True))

    rows = lambda width: pl.BlockSpec((ts, width), lambda k: (k, 0))
    sec_specs, operands = _section_specs(sections, ts, W)
    in_specs, out_specs, out_shape = [rows(LANES)] + sec_specs, [], []
    if h is not None:
        whole = pl.BlockSpec((D, N + LANES), lambda k: (0, 0), pipeline_mode=once)
        in_specs.append(rows(D))
        operands.append(h)
        out_specs += [whole, whole]
        out_shape += [jax.ShapeDtypeStruct((D, N + LANES), F32), jax.ShapeDtypeStruct((D, N + LANES), BF16)]
    if wxg:
        w, x, gam, g = wxg
        in_specs += [pl.BlockSpec((D, N), lambda k: (0, 0), pipeline_mode=once),
                     pl.BlockSpec((D, LANES), lambda k: (0, N // LANES), pipeline_mode=once),
                     rows(D), pl.BlockSpec((1, D), lambda k: (0, 0)), rows(D)]
        operands += [w, w, x, gam, g]
        out_specs += [rows(D), pl.BlockSpec((1, D), lambda k: (0, 0))]
        out_shape += [jax.ShapeDtypeStruct((S, D), F32), jax.ShapeDtypeStruct((1, D), F32)]
    name = "inproj_bwd" if h is not None and wxg else ("inproj_bwd_dw" if wxg is None else "inproj_bwd_dx")
    return pl.pallas_call(
        body, name=name, grid=(n_steps,),
        in_specs=in_specs + _after_specs(after), out_specs=out_specs, out_shape=out_shape,
        compiler_params=_params("arbitrary"),
    )(dzf, *operands, *after)


def _attn_bwd(proj, da, qaug, kaug, doaug, after=()):
    S = proj.shape[0]
    W = proj.shape[1] // 6
    P = W // LANES
    tq = min(TQ, S)
    nq = S // tq
    qc, kc, vc = 2 * P, 3 * P, 4 * P
    scale = 1.0 / math.sqrt(HEAD_DIM)

    def body(q_ref, k_ref, v_ref, do_ref, qa_ref, ka_ref, da_ref, *rest):
        dq_ref, dk_ref, dv_ref, drows_ref, dcols_ref, km_scr, vm_scr, dk_scr, dv_scr, dq_scr = rest[len(after):]
        pair, j = pl.program_id(0), pl.program_id(1)
        lane, halves = _head_halves(tq)

        @pl.when(jnp.logical_and(pair == 0, j == 0))
        def _():
            drows_ref[...] = jnp.zeros(drows_ref.shape, F32)
            dcols_ref[...] = jnp.zeros(dcols_ref.shape, F32)

        @pl.when(j == 0)
        def _():
            dq_scr[...] = jnp.zeros(dq_scr.shape, F32)

        v_ones = ((lane & (HEAD_DIM - 1)) < 3).astype(BF16)
        for a in range(2):
            km_scr[a] = jnp.where(halves[a], k_ref[...], ka_ref[...])
            vm_scr[a] = jnp.where(halves[a], v_ref[...], v_ones)
        dk_scr[...] = jnp.zeros(dk_scr.shape, F32)
        dv_scr[...] = jnp.zeros(dv_scr.shape, F32)

        def update(i, on_diagonal):
            rows = pl.ds(pl.multiple_of(i * tq, tq), tq)
            qs = q_ref[rows, :] * scale
            do2, qaug_t, doaug_t = do_ref[rows, :], qa_ref[rows, :], da_ref[rows, :]
            if on_diagonal:
                keep = (lax.broadcasted_iota(jnp.int32, (tq, tq), 0)
                        >= lax.broadcasted_iota(jnp.int32, (tq, tq), 1))
            qas = [jnp.where(halves[a], qs, qaug_t) for a in range(2)]
            logits = [lax.dot_general(qas[a], km_scr[a], NT, preferred_element_type=F32) for a in range(2)]
            dps = [lax.dot_general(jnp.where(halves[a], do2, doaug_t), vm_scr[a], NT, preferred_element_type=F32)
                   for a in range(2)]
            dv = None
            for a in range(2):
                s = jnp.where(keep, logits[a], NEG_INF) if on_diagonal else logits[a]
                p = jnp.exp(s)
                dsb = (p * dps[a]).astype(BF16)
                do0 = jnp.where(halves[a], do2, jnp.zeros_like(do2))
                dv_a = lax.dot_general(p.astype(BF16), do0, TN_DIMS, preferred_element_type=F32)
                dv = dv_a if dv is None else dv + dv_a
                dk_scr[a] += lax.dot_general(dsb, qas[a], TN_DIMS, preferred_element_type=F32)
                dq_scr[a, rows, :] += jnp.dot(dsb, km_scr[a], preferred_element_type=F32)
            dv_scr[...] += dv

        def below_diagonal(n, carry):
            update(j + 1 + 2 * n, False)
            update(j + 2 + 2 * n, False)
            return carry

        update(j, True)
        below = nq - 1 - j
        lax.fori_loop(0, below // 2, below_diagonal, 0)

        @pl.when(below % 2 == 1)
        def _():
            update(nq - 1, False)


        def to_head_lanes(old, first, second):
            at = lax.broadcasted_iota(jnp.int32, old.shape, 1) - 2 * pair
            return jnp.where(at == 0, first, jnp.where(at == 1, second, old))

        dk_ref[...] = jnp.where(halves[0], dk_scr[0], dk_scr[1]).astype(BF16)
        dv_ref[...] = dv_scr[...].astype(BF16)
        keys = pl.ds(pl.multiple_of(j * tq, tq), tq)
        ones_a, ones_b = HEAD_DIM + SLOT_ONE, SLOT_ONE
        dcols_ref[keys, :] = to_head_lanes(dcols_ref[keys, :], dk_scr[0][:, ones_a:ones_a + 1],
                                           dk_scr[1][:, ones_b:ones_b + 1])

        @pl.when(j == nq - 1)
        def _():
            row_lane, row_halves = _head_halves(S)
            dq_ref[...] = (jnp.where(row_halves[0], dq_scr[0], dq_scr[1]) * scale).astype(BF16)
            c_a, c_b = HEAD_DIM + SLOT_C, SLOT_C
            drows_ref[...] = to_head_lanes(drows_ref[...], dq_scr[0][:, c_a:c_a + 1], dq_scr[1][:, c_b:c_b + 1])

    tile = lambda col: pl.BlockSpec((tq, LANES), lambda p, j: (j, col + p))
    whole = lambda col: pl.BlockSpec((S, LANES), lambda p, j: (0, col + p))
    shared = pl.BlockSpec((S, LANES), lambda p, j: (0, 0))
    return pl.pallas_call(
        body, name="attn_bwd", grid=(P, nq),
        in_specs=[whole(qc), tile(kc), tile(vc), whole(P), whole(0), tile(0), whole(0)] + _after_specs(after),
        out_specs=[whole(0), tile(0), tile(0), shared, shared],
        out_shape=[jax.ShapeDtypeStruct((S, W), BF16),
                   jax.ShapeDtypeStruct((S, W), BF16),
                   jax.ShapeDtypeStruct((S, W), BF16),
                   jax.ShapeDtypeStruct((S, LANES), F32),
                   jax.ShapeDtypeStruct((S, LANES), F32)],
        scratch_shapes=[pltpu.VMEM((2, tq, LANES), BF16),
                        pltpu.VMEM((2, tq, LANES), BF16),
                        pltpu.VMEM((2, tq, LANES), F32),
                        pltpu.VMEM((tq, LANES), F32),
                        pltpu.VMEM((2, S, LANES), F32)],
        compiler_params=_params("arbitrary", "arbitrary"),
    )(proj, proj, proj, da, qaug, kaug, doaug, *after)


def _fgate_bwd(drows, dcols, z, bias):
    S = z.shape[0]
    tb = min(TB, S)
    nb = S // tb

    def body(drows_ref, dcols_ref, z_ref, b_ref, dz_ref, db_ref):
        tri = (lax.broadcasted_iota(jnp.int32, (tb, tb), 1)
               >= lax.broadcasted_iota(jnp.int32, (tb, tb), 0)).astype(BF16)

        local = []
        for b in range(nb):
            rows = slice(b * tb, (b + 1) * tb)
            local.append(_dot_exact_left(tri, drows_ref[rows, :] - dcols_ref[rows, :]))
        carry = jnp.zeros((1, LANES), F32)
        db = jnp.zeros((1, LANES), F32)
        for b in reversed(range(nb)):
            rows = slice(b * tb, (b + 1) * tb)
            rc = local[b] + carry
            carry = rc[0:1, :]
            dz = rc * _sigmoid(-(z_ref[rows, :] + b_ref[...]))
            dz_ref[rows, :] = dz.astype(BF16)
            db = db + jnp.sum(dz, axis=0, keepdims=True)
        db_ref[...] = db

    return pl.pallas_call(
        body, name="fgate_bwd",
        out_shape=[jax.ShapeDtypeStruct((S, LANES), BF16),
                   jax.ShapeDtypeStruct((1, LANES), F32)],
        compiler_params=pltpu.CompilerParams(vmem_limit_bytes=VMEM_LIMIT),
    )(drows, dcols, z, bias)


def _pool_bwd(proj, da, pool_w, pool_scale):
    S = proj.shape[0]
    G = len(POOL_WINDOWS)

    def body(u_ref, dy_ref, w_ref, s_ref, du_ref, dw_ref, ds_ref, pad_ref):
        g = pl.program_id(0)
        for gi, w in enumerate(POOL_WINDOWS):
            @pl.when(g == gi)
            def _():
                d, cnt = _window_mean_minus_self(u_ref[...].astype(F32), pad_ref, w, S)
                db = d.astype(BF16)
                wb = w_ref[0].astype(BF16)
                yraw = jnp.dot(db, wb, preferred_element_type=F32)
                dy = dy_ref[...].astype(F32)
                ds_ref[...] = jnp.sum(dy * yraw, axis=0, keepdims=True)
                dzb = (dy * s_ref[...]).astype(BF16)
                dw_ref[0] = lax.dot_general(db, dzb, TN_DIMS, preferred_element_type=F32)
                dd = lax.dot_general(dzb, wb, NT, preferred_element_type=F32)
                acc = dd / cnt
                pad_ref[0:S, :] = acc
                pad_ref[S:S + MAX_WINDOW, :] = jnp.zeros((MAX_WINDOW, LANES), F32)
                inner = _inner_window(w)
                for j in range(1, inner):
                    acc = acc + pad_ref[j:j + S, :]
                if inner > 1:
                    pad_ref[0:S, :] = acc
                for j in range(inner, w, inner):
                    acc = acc + pad_ref[j:j + S, :]
                du_ref[...] = (acc - dd).astype(BF16)

    return pl.pallas_call(
        body, name="pool_bwd", grid=(G,),
        in_specs=[pl.BlockSpec((S, LANES), lambda g: (0, g)),
                  pl.BlockSpec((S, LANES), lambda g: (0, g)),
                  pl.BlockSpec((1, LANES, LANES), lambda g: (g, 0, 0)),
                  pl.BlockSpec((1, LANES), lambda g: (0, g))],
        out_specs=[pl.BlockSpec((S, LANES), lambda g: (0, g)),
                   pl.BlockSpec((1, LANES, LANES), lambda g: (g, 0, 0)),
                   pl.BlockSpec((1, LANES), lambda g: (0, g))],
        out_shape=[jax.ShapeDtypeStruct((S, G * LANES), BF16),
                   jax.ShapeDtypeStruct((G, LANES, LANES), F32),
                   jax.ShapeDtypeStruct((1, G * LANES), F32)],
        scratch_shapes=[pltpu.VMEM((S + MAX_WINDOW, LANES), F32)],
        compiler_params=_params("arbitrary"),
    )(proj, da, pool_w, pool_scale)


def _adamw(w, m, v, gsets, name, rows, shifted=False, first=0, into=None):
    A, R, C = w.shape
    n_sets = len(gsets)
    tr = min(rows, R)
    c1 = 1.0 / (1.0 - ADAM_B1 ** ADAM_STEP)
    c2 = 1.0 / (1.0 - ADAM_B2 ** ADAM_STEP)
    counts = [len(gs) for gs in gsets]

    def body(w_ref, m_ref, v_ref, *rest):
        g_ref, d_ref, nm_ref, nv_ref = rest[-4:]
        at = 0
        for a in range(n_sets):
            part_refs = rest[at:at + counts[a]]
            at += counts[a]

            @pl.when(pl.program_id(0) == a)
            def _():
                g = None
                for ref in part_refs:
                    for s in range(ref.shape[0]):
                        term = ref[s].astype(F32)
                        g = term if g is None else g + term
                if shifted:
                    lanes = g.shape[1]
                    g = pltpu.roll(g, (lanes - _index(_position()) * (C % LANES)) % lanes, axis=1)[:, :C]
                nm = ADAM_B1 * m_ref[0] + (1.0 - ADAM_B1) * g
                nv = ADAM_B2 * v_ref[0] + (1.0 - ADAM_B2) * (g * g)
                g_ref[0] = g
                nm_ref[0] = nm
                nv_ref[0] = nv
                d_ref[0] = -ADAM_LR * ((nm * c1) / (jnp.sqrt(nv * c2) + ADAM_EPS) + ADAM_WD * w_ref[0])

    spec = pl.BlockSpec((1, tr, C), lambda a, r: (first + a, r, 0))
    part_specs = [pl.BlockSpec((part.shape[0], tr, part.shape[2]), lambda a, r, l=l: (0, jnp.where(a == l, r, 0), 0))
                  for l, gs in enumerate(gsets) for part in gs]
    parts = [part for gs in gsets for part in gs]
    shape = jax.ShapeDtypeStruct((A, R, C), F32)
    earlier = () if into is None else tuple(into)
    return pl.pallas_call(
        body, name=name, grid=(n_sets, R // tr),
        in_specs=[spec, spec, spec] + part_specs + _after_specs(earlier),
        out_specs=[spec, spec, spec, spec],
        out_shape=[shape, shape, shape, shape],
        input_output_aliases={3 + len(parts) + n: n for n in range(len(earlier))},
        compiler_params=_params("arbitrary", "arbitrary"),
    )(w, m, v, *parts, *earlier)


def _position():
    return lax.axis_index("x"), lax.axis_index("y"), lax.axis_index("c")


def _index(dev):
    return 4 * dev[0] + 2 * dev[1] + dev[2]


def _all_gather(arrs, slots, out_shapes, name):
    n_arr = len(arrs)

    def body(*refs):
        ins, outs = refs[:n_arr], refs[n_arr:2 * n_arr]
        send_sems, recv_sems, local_sems = refs[2 * n_arr:]
        x, y, c = _position()
        me, sibling = (x, y, c), (x, y, 1 - c)
        chips = [(1 - x, y), (x, 1 - y), (1 - x, 1 - y)]

        def copy(a, k, block, to, src=None):
            part = slots[a](outs[a], _index(block))
            return pltpu.make_async_remote_copy(
                src_ref=part if src is None else src, dst_ref=part,
                send_sem=send_sems.at[a, k], recv_sem=recv_sems.at[a, k],
                device_id=to, device_id_type=MESH)

        mine = [pltpu.make_async_copy(ins[a], slots[a](outs[a], _index(me)), local_sems.at[a])
                for a in range(n_arr)]
        for cp in mine:
            cp.start()
        first = []
        for a in range(n_arr):
            first.append(copy(a, 0, me, sibling, src=ins[a]))
            first += [copy(a, 1 + j, me, (*chip, c), src=ins[a]) for j, chip in enumerate(chips)]
        for cp in first:
            cp.start()
        passed = []
        for j, chip in enumerate(chips):
            for a in range(n_arr):
                copy(a, 1 + j, (*chip, c), me).wait_recv()
                fwd = copy(a, 4 + j, (*chip, c), sibling)
                fwd.start()
                passed.append(fwd)
        for a in range(n_arr):
            copy(a, 0, sibling, me).wait_recv()
            for j, chip in enumerate(chips):
                copy(a, 4 + j, (*chip, 1 - c), me).wait_recv()
        for cp in first + passed:
            cp.wait_send()
        for cp in mine:
            cp.wait()

    any_spec = pl.BlockSpec(memory_space=pl.ANY)
    return pl.pallas_call(
        body, name=name,
        in_specs=[any_spec] * n_arr, out_specs=[any_spec] * n_arr, out_shape=out_shapes,
        scratch_shapes=[pltpu.SemaphoreType.DMA((n_arr, 7)), pltpu.SemaphoreType.DMA((n_arr, 7)),
                        pltpu.SemaphoreType.DMA((n_arr,))],
    )(*arrs)


def _split_copies(srcs, lands, send_sems, recv_sems, kinds):
    x, y, c = _position()
    me = _index((x, y, c))
    copies = []
    for a, (src_part, land_part) in enumerate(kinds):
        for k in range(1, N_DEV):
            peer = (x ^ ((k >> 2) & 1), y ^ ((k >> 1) & 1), c ^ (k & 1))
            copies.append(pltpu.make_async_remote_copy(
                src_ref=src_part(srcs[a], _index(peer)), dst_ref=land_part(lands[a], me, k),
                send_sem=send_sems[a].at[k - 1], recv_sem=recv_sems[a].at[k - 1],
                device_id=peer, device_id_type=MESH))
    return copies


def _split_start(srcs, lands, kinds, name, after=()):
    n = len(srcs)

    def body(*refs):
        src_refs, land_refs = refs[:n], refs[n:2 * n]
        outs = refs[2 * n + len(after):]
        send_sems, recv_sems = outs[:n], outs[n:2 * n]
        token = outs[-1]
        for cp in _split_copies(src_refs, land_refs, send_sems, recv_sems, kinds):
            cp.start()
        token[...] = jnp.zeros(token.shape, token.dtype)

    hbm = pl.BlockSpec(memory_space=pltpu.HBM)
    sem = pl.BlockSpec(memory_space=pltpu.SEMAPHORE)
    operands = [pltpu.with_memory_space_constraint(t, pltpu.HBM) for t in (*srcs, *lands)]
    out = pl.pallas_call(
        body, name=name,
        in_specs=[hbm] * (2 * n) + _after_specs(after),
        out_specs=[sem] * (2 * n) + [hbm] * (2 * n) + [pl.BlockSpec(memory_space=pltpu.VMEM)],
        out_shape=[pltpu.SemaphoreType.DMA((N_DEV - 1,))] * (2 * n)
        + [pltpu.HBM(t.shape, t.dtype) for t in operands] + [jax.ShapeDtypeStruct((8, LANES), F32)],
        input_output_aliases={i: 2 * n + i for i in range(2 * n)},
        compiler_params=pltpu.CompilerParams(has_side_effects=pltpu.SideEffectType.DATAFLOW_SIDE_EFFECTING),
    )(*operands, *after)
    return [(out[a], out[n + a], out[2 * n + a], out[3 * n + a]) for a in range(n)], out[-1]


def _split_wait(started, kinds, after, name):
    n = len(started)
    sems = [t[0] for t in started] + [t[1] for t in started]
    srcs = [t[2] for t in started]
    lands = [t[3] for t in started]

    def body(*refs):
        src_refs, land_refs = refs[:n], refs[n:2 * n]
        send_sems, recv_sems = refs[2 * n:3 * n], refs[3 * n:4 * n]
        for cp in _split_copies(src_refs, land_refs, send_sems, recv_sems, kinds):
            cp.wait_send()
            cp.wait_recv()

    hbm = pl.BlockSpec(memory_space=pltpu.HBM)
    sem = pl.BlockSpec(memory_space=pltpu.SEMAPHORE)
    out = pl.pallas_call(
        body, name=name,
        in_specs=[hbm] * (2 * n) + [sem] * (2 * n) + _after_specs(after),
        out_specs=[hbm] * (2 * n),
        out_shape=[pltpu.HBM(t.shape, t.dtype) for t in (*srcs, *lands)],
        input_output_aliases={i: i for i in range(2 * n)},
        compiler_params=pltpu.CompilerParams(has_side_effects=pltpu.SideEffectType.DATAFLOW_SIDE_EFFECTING),
    )(*srcs, *lands, *sems, *after)
    return out[n:]


def _as_rows(p):
    if p.size % LANES == 0:
        rows = p.reshape(-1, LANES)
    else:
        rows = p.reshape(-1, p.shape[-1])
        rows = jnp.pad(rows, ((0, 0), (0, LANES - rows.shape[1])))
    return jnp.pad(rows, ((0, -rows.shape[0] % 8), (0, 0)))


def _pack(parts):
    return jnp.concatenate([_as_rows(p) for p in parts])[None]


def _unpack(packed, like):
    out, at = [], 0
    for p in like:
        whole = p.size % LANES == 0
        n = p.size // LANES if whole else p.size // p.shape[-1]
        rows = packed[0, at:at + n]
        out.append((rows if whole else rows[:, :p.shape[-1]]).reshape(p.shape))
        at += n + (-n % 8)
    return out


def _local_step(x, target, norm_g, forget_bias, pool_w, pool_scale, final_g, weights_in, weights_out, on_grads,
                first_after=(), on_first_out_grads=lambda d_wout, d_wout_bf16: ()):
    L = norm_g.shape[0]
    S, D = x.shape
    W = D // 2
    H = W // HEAD_DIM
    bias = jnp.pad(forget_bias, ((0, 0), (0, LANES - H)))

    saved = []
    nxt = _inproj_fwd(x, norm_g[0:1], weights_in(0, x), tuple(first_after))
    for l in range(L):
        x_in, (proj, h, z, w) = x, nxt
        qaug, kaug = _fgate_fwd(z, bias[l:l + 1], H)
        ypool = _pool_fwd(proj, pool_w[l], pool_scale[l:l + 1])
        o, qaug_b = _attn_fwd(proj, qaug, kaug)
        wout = weights_out(l, o)
        if l < L - 1:
            x, *nxt = _outproj_fwd(ypool, o, proj, x_in, wout, nxt=(norm_g[l + 1:l + 2], weights_in(l + 1, o)))
        else:
            g, loss, d_final_g = _outproj_fwd(ypool, o, proj, x_in, wout, head=(final_g.reshape(1, D), target))
        saved.append((x_in, proj, h, z, qaug_b, kaug, ypool, o, w, wout))

    small, after = None, ()
    for l in reversed(range(L)):
        x_in, proj, h, z, qaug_b, kaug, ypool, o, w, wout = saved[l]
        d_wout, d_wout_bf16, da, dgate, doaug = _outproj_bwd(g, wout, ypool, o, proj, after)
        early = tuple(on_first_out_grads(d_wout, d_wout_bf16)) if l == 0 else ()
        dq, dk, dv, drows, dcols = _attn_bwd(proj, da, qaug_b, kaug, doaug, early)
        dzf, db = _fgate_bwd(drows, dcols, z, bias[l:l + 1])
        dpu, dpw, dps = _pool_bwd(proj, da, pool_w[l], pool_scale[l:l + 1])
        dproj = [(dpu, 0), (dgate, 0), (dq, 0), (dk, 0), (dv, 0), (dgate, 1)]
        wxg = (w, x_in, norm_g[l:l + 1], g)
        if l > 0:
            d_w, d_w_bf16, g, dgam = _inproj_bwd(dproj, dzf, h, wxg)
            after = tuple(on_grads(l, d_w, d_w_bf16, d_wout, d_wout_bf16, small))
        else:
            d_w, d_w_bf16 = _inproj_bwd(dproj, dzf, h)
            after = tuple(on_grads(l, d_w, d_w_bf16, d_wout, d_wout_bf16, small))
            g, dgam = _inproj_bwd(dproj, dzf, None, wxg, after)
        small = (dgam[0], db[0, :H], dpw, dps[0])
    return loss[0, 0], g, small, d_final_g[0]


def kernel(x, norm_g, w_in, forget_bias, pool_w, pool_scale, w_out, final_g, loss_target, m_norm_g, m_w_in, m_forget_bias, m_pool_w, m_pool_scale, m_w_out, m_final_g, v_norm_g, v_w_in, v_forget_bias, v_pool_w, v_pool_scale, v_w_out, v_final_g):
    L, D, cols = w_in.shape
    rows_out = w_out.shape[1]
    me = _index(_position())
    slot = _slot_width(cols)
    wout_b = w_out.astype(BF16)
    win_b = _shift_slots(w_in)
    gather_in = (lambda ref, peer: ref, lambda ref, mine, k: ref.at[mine])
    gather_out = (lambda ref, peer: ref, lambda ref, mine, k: ref.at[pl.ds(mine * rows_out, rows_out), :])

    def landing(block, n_slots):
        zone = lax.empty((n_slots * block.shape[0], *block.shape[1:]), block.dtype)
        return lax.dynamic_update_slice(zone, block, (me * block.shape[0],) + (0,) * (block.ndim - 1))

    (first_in,) = _all_gather([win_b[0]], [lambda ref, n: ref.at[n]],
                              [jax.ShapeDtypeStruct((N_DEV, D, slot), BF16)], "gather_first")
    rest_srcs = [wout_b[0]] + [w[l] for l in range(1, L) for w in (win_b, wout_b)]
    rest_lands = [landing(wout_b[0], N_DEV)]
    for l in range(1, L):
        rest_lands += [landing(win_b[l][None], N_DEV), landing(wout_b[l], N_DEV)]
    rest_kinds = [gather_out] + [gather_in, gather_out] * (L - 1)
    rest, rest_token = _split_start(rest_srcs, rest_lands, rest_kinds, "gather_start_rest", (first_in,))

    def weights_in(l, x_in):
        if l == 0:
            return first_in
        (win_all,) = _split_wait([rest[2 * l - 1]], [gather_in], (x_in,), f"gather_wait_in_{l}")
        return win_all

    def weights_out(l, o):
        (wout_full,) = _split_wait([rest[2 * l]], [gather_out], (o,), f"gather_wait_out_{l}")
        return wout_full

    stride = slot - LANES
    exchange_kinds = [(lambda ref, peer: ref.at[:, pl.ds(pl.multiple_of(peer * stride, LANES), slot)],
                       lambda ref, mine, k: ref.at[k - 1]),
                      (lambda ref, peer: ref.at[pl.ds(peer * rows_out, rows_out), :],
                       lambda ref, mine, k: ref.at[k - 1])]
    zero_g = jnp.zeros_like(final_g)
    zero_loss = jnp.zeros((LANES,), F32)

    def small_pack(l, norm_g_l, bias_l, pool_w_l, pool_scale_l, final, loss_row=None):
        return _pack([norm_g_l, bias_l, pool_w_l, pool_scale_l, final if l == 0 else zero_g,
                      zero_loss if loss_row is None else loss_row])[0]

    exchanges, own_parts, first_out = {}, {}, []

    def on_first_out_grads(d_wout, d_wout_bf16):
        started, token = _split_start([d_wout_bf16], [lax.empty((N_DEV - 1, rows_out, D), BF16)], exchange_kinds[1:],
                                      "exchange_start_out_0")
        first_out.extend(started)
        return (token,)

    def on_grads(l, dw, dw_bf16, d_wout, d_wout_bf16, small):
        own_parts[l] = (lax.dynamic_slice_in_dim(dw, me * stride, slot, 1)[None],
                        lax.dynamic_slice_in_dim(d_wout, me * rows_out, rows_out, 0)[None])
        out_on_its_way = l == 0
        srcs, lands, kinds = [dw_bf16], [lax.empty((N_DEV - 1, D, slot), BF16)], exchange_kinds[:1]
        if not out_on_its_way:
            srcs.append(d_wout_bf16)
            lands.append(lax.empty((N_DEV - 1, rows_out, D), BF16))
            kinds = list(exchange_kinds)
        if small is not None:
            packed_small = small_pack(l + 1, *small, None)
            srcs.append(packed_small)
            lands.append(landing(packed_small[None], N_DEV))
            kinds.append(gather_in)
        started, token = _split_start(srcs, lands, kinds, f"exchange_start_{l}")
        if out_on_its_way:
            started[1:1], kinds[1:1] = first_out, exchange_kinds[1:]
        exchanges[l] = (started, kinds)
        return (token,)

    loss, dx, small_first, d_final_g = _local_step(
        x[0], loss_target[0], norm_g, forget_bias, pool_w, pool_scale, final_g,
        weights_in, weights_out, on_grads, (rest_token,), on_first_out_grads)
    packed_first = small_pack(0, *small_first, d_final_g, jnp.full((LANES,), loss, F32))
    first_started, first_token = _split_start(
        [packed_first], [landing(packed_first[None], N_DEV)], [gather_in],
        "small_start", (w_in, m_w_in, v_w_in, *own_parts[0]))

    gin_sets, gout_sets, small_sets = [None] * L, [None] * L, [None] * L

    def wait_for(l, after):
        started, kinds = exchanges[l]
        got = _split_wait(started, kinds, after, f"exchange_wait_{l}")
        gin_sets[l] = [own_parts[l][0], got[0]]
        gout_sets[l] = [own_parts[l][1], got[1]]
        if len(got) > 2:
            small_sets[l + 1] = [got[2]]

    for l in range(1, L):
        wait_for(l, (dx, first_token))
    rest_in = _adamw(w_in, m_w_in, v_w_in, gin_sets[1:], "adamw_w_in_rest", TM // 2, shifted=True, first=1)
    rest_out = _adamw(w_out, m_w_out, v_w_out, gout_sets[1:], "adamw_w_out_rest", rows_out, first=1)
    wait_for(0, (rest_in[1], rest_out[1]))
    g_w_in, d_w_in, nm_w_in, nv_w_in = _adamw(w_in, m_w_in, v_w_in, gin_sets[:1], "adamw_w_in_first", TM // 2,
                                              shifted=True, into=rest_in)
    g_w_out, d_w_out, nm_w_out, nv_w_out = _adamw(w_out, m_w_out, v_w_out, gout_sets[:1], "adamw_w_out_first",
                                                  rows_out, into=rest_out)
    small_sets[0] = _split_wait(first_started, [gather_in], (d_w_in, d_w_out), "small_wait")
    loss = jnp.sum(small_sets[0][0][:, packed_first.shape[0] - 8, 0])

    def small_stack(norm_g_, bias_, pool_w_, pool_scale_, final):
        return jnp.stack([small_pack(l, norm_g_[l], bias_[l], pool_w_[l], pool_scale_[l], final) for l in range(L)])

    packed = _adamw(small_stack(norm_g, forget_bias, pool_w, pool_scale, final_g),
                    small_stack(m_norm_g, m_forget_bias, m_pool_w, m_pool_scale, m_final_g),
                    small_stack(v_norm_g, v_forget_bias, v_pool_w, v_pool_scale, v_final_g),
                    small_sets, "adamw_small", packed_first.shape[0])

    def small_unpack(p):
        like = [norm_g[0], forget_bias[0], pool_w[0], pool_scale[0], final_g]
        layers = [_unpack(p[l:l + 1], like) for l in range(L)]
        return [jnp.stack([layers[l][n] for l in range(L)]) for n in range(4)] + [layers[0][4]]

    g_s, d_s, nm_s, nv_s = [small_unpack(p) for p in packed]

    def order(big_in, big_out, small):
        return (small[0], big_in, small[1], small[2], small[3], big_out, small[4])

    return (loss, dx[None], *order(g_w_in, g_w_out, g_s), *order(d_w_in, d_w_out, d_s),
            *order(nm_w_in, nm_w_out, nm_s), *order(nv_w_in, nv_w_out, nv_s))
```

```python
import math

import jax
import jax.numpy as jnp
from jax import lax
from jax.experimental import pallas as pl
from jax.experimental.pallas import tpu as pltpu

F32 = jnp.float32
BF16 = jnp.bfloat16
MESH = pl.DeviceIdType.MESH

RMS_EPS = 1e-6
NEG_INF = -1e30
HEAD_DIM = 64
POOL_WINDOWS = (2, 4, 8, 16)
MAX_WINDOW = 16
LANES = 128
N_DEV = 8

ADAM_LR = 0.001
ADAM_B1 = 0.9
ADAM_B2 = 0.999
ADAM_EPS = 1e-08
ADAM_WD = 0.01
ADAM_STEP = 10

TM = 512
TN = 512
TQ = 512
TB = 256
VMEM_LIMIT = 56 * 1024 * 1024

NT = (((1,), (1,)), ((), ()))
TN_DIMS = (((0,), (0,)), ((), ()))

SLOT_C, SLOT_ONE, SLOT_LSE = 0, 3, 6


def _params(*sem):
    return pltpu.CompilerParams(dimension_semantics=sem, vmem_limit_bytes=VMEM_LIMIT)


def _sigmoid(x):
    return 1.0 / (1.0 + jnp.exp(-x))


def _split3(x):
    hi = x.astype(BF16).astype(F32)
    rest = x - hi
    mid = rest.astype(BF16).astype(F32)
    return hi, mid, rest - mid


def _dot_exact_left(a, x):
    return sum(jnp.dot(a, part.astype(BF16), preferred_element_type=F32) for part in reversed(_split3(x)))


def _after_specs(after):
    return [pl.BlockSpec(memory_space=pl.ANY)] * len(after)


def _slot_width(cols):
    return LANES * (-(-(cols + (N_DEV - 1) * (cols % LANES)) // LANES))


def _shift_slots(w_in):
    L, D, cols = w_in.shape
    slot = _slot_width(cols)
    tr = min(TM // 2, D)

    def body(w_ref, o_ref, pad_scr):
        pad_scr[...] = jnp.zeros(pad_scr.shape, F32)
        pad_scr[:, 0:cols] = w_ref[0]
        o_ref[0] = pltpu.roll(pad_scr[...], _index(_position()) * (cols % LANES), axis=1).astype(BF16)

    return pl.pallas_call(
        body, name="shift_slots", grid=(L, D // tr),
        in_specs=[pl.BlockSpec((1, tr, cols), lambda l, r: (l, r, 0))],
        out_specs=pl.BlockSpec((1, tr, slot), lambda l, r: (l, r, 0)),
        out_shape=jax.ShapeDtypeStruct((L, D, slot), BF16),
        scratch_shapes=[pltpu.VMEM((tr, slot), F32)],
        compiler_params=_params("parallel", "parallel"),
    )(w_in)


def _inproj_tile(xf, g_ref, s_ref, proj_ref, h_ref, z_ref, w_ref):
    n_dev, _, sw = s_ref.shape
    stride = sw - LANES
    width = w_ref.shape[1]
    N = width - LANES

    @pl.when(pl.program_id(0) == 0)
    def _():
        for n in range(n_dev):
            base = stride * n
            first = s_ref[n, :, 0:LANES]
            if n > 0:
                first = first + s_ref[n - 1, :, stride:sw]
            w_ref[:, base:base + LANES] = first
            w_ref[:, base + LANES:base + stride] = s_ref[n, :, LANES:stride]
        w_ref[:, stride * n_dev:width] = s_ref[n_dev - 1, :, stride:sw]

    r = lax.rsqrt(jnp.mean(xf * xf, axis=-1, keepdims=True) + RMS_EPS)
    h = ((xf * r) * g_ref[...]).astype(BF16)
    h_ref[...] = h
    z_ref[...] = jnp.dot(h, w_ref[:, N:width], preferred_element_type=F32)
    for n in range(N // TN):
        cols = slice(n * TN, (n + 1) * TN)
        proj_ref[:, cols] = jnp.dot(h, w_ref[:, cols], preferred_element_type=F32).astype(BF16)


def _inproj_specs(slots, S, D, tm):
    n_dev, _, sw = slots.shape
    width = (sw - LANES) * n_dev + LANES
    N = width - LANES
    once = pl.Buffered(1)
    in_specs = [pl.BlockSpec((1, D), lambda i: (0, 0)),
                pl.BlockSpec((n_dev, D, sw), lambda i: (0, 0, 0), pipeline_mode=once)]
    out_specs = [pl.BlockSpec((tm, N), lambda i: (i, 0)),
                 pl.BlockSpec((tm, D), lambda i: (i, 0)),
                 pl.BlockSpec((tm, LANES), lambda i: (i, 0)),
                 pl.BlockSpec((D, width), lambda i: (0, 0), pipeline_mode=once)]
    out_shape = [jax.ShapeDtypeStruct((S, N), BF16),
                 jax.ShapeDtypeStruct((S, D), BF16),
                 jax.ShapeDtypeStruct((S, LANES), F32),
                 jax.ShapeDtypeStruct((D, width), BF16)]
    return in_specs, out_specs, out_shape


def _inproj_fwd(x, gam, slots, after=()):
    S, D = x.shape
    tm = min(TM, S)

    def body(x_ref, g_ref, s_ref, *rest):
        _inproj_tile(x_ref[...], g_ref, s_ref, *rest[-4:])

    in_specs, out_specs, out_shape = _inproj_specs(slots, S, D, tm)
    return pl.pallas_call(
        body, name="inproj_fwd", grid=(S // tm,),
        in_specs=[pl.BlockSpec((tm, D), lambda i: (i, 0))] + in_specs + _after_specs(after),
        out_specs=out_specs, out_shape=out_shape,
        compiler_params=_params("arbitrary"),
    )(x, gam, slots, *after)


def _fgate_fwd(z, bias, n_heads):
    S = z.shape[0]
    tb = min(TB, S)
    P = n_heads // 2
    assert n_heads <= 8, "the three parts of c are packed eight lanes apart"

    def body(z_ref, b_ref, qaug_ref, kaug_ref):
        lane = lax.broadcasted_iota(jnp.int32, (tb, LANES), 1)
        tri = (lax.broadcasted_iota(jnp.int32, (tb, tb), 0)
               >= lax.broadcasted_iota(jnp.int32, (tb, tb), 1)).astype(BF16)
        row = lax.broadcasted_iota(jnp.int32, (LANES, P * LANES), 0)
        col = lax.broadcasted_iota(jnp.int32, (LANES, P * LANES), 1)
        head, part_n = row & 7, row >> 3
        home = (head >> 1) * LANES + jnp.where((head & 1) == 0, HEAD_DIM, 0)
        is_part = jnp.logical_and(head < n_heads, part_n < 3)
        place_q = jnp.logical_and(is_part, col == home + SLOT_C + part_n).astype(BF16)
        place_k = jnp.logical_and(is_part, col == home + SLOT_ONE + part_n).astype(BF16)
        slot = lax.broadcasted_iota(jnp.int32, (tb, P * LANES), 1) & (HEAD_DIM - 1)
        q_ones = jnp.logical_and(slot >= SLOT_ONE, slot < SLOT_ONE + 3).astype(F32)
        k_ones = jnp.logical_or(slot < SLOT_C + 3,
                                jnp.logical_and(slot >= SLOT_LSE, slot < SLOT_LSE + 3)).astype(F32)

        local = []
        for b in range(S // tb):
            zz = z_ref[b * tb:(b + 1) * tb, :] + b_ref[...]
            lf = jnp.minimum(zz, 0.0) - jnp.log(1.0 + jnp.exp(-jnp.abs(zz)))
            lf = jnp.where(lane < n_heads, lf, 0.0)
            local.append(_dot_exact_left(tri, lf))
        carry = jnp.zeros((1, LANES), F32)
        for b, part_sum in enumerate(local):
            c = part_sum + carry
            carry = c[tb - 1:tb, :]
            hi, mid, lo = _split3(c)
            packed = (hi + pltpu.roll(mid, 8, axis=1) + pltpu.roll(lo, 16, axis=1)).astype(BF16)
            qaug_ref[b * tb:(b + 1) * tb, :] = (
                q_ones + jnp.dot(packed, place_q, preferred_element_type=F32)).astype(BF16)
            kaug_ref[b * tb:(b + 1) * tb, :] = (
                k_ones - jnp.dot(packed, place_k, preferred_element_type=F32)).astype(BF16)

    return pl.pallas_call(
        body, name="fgate_fwd",
        out_shape=[jax.ShapeDtypeStruct((S, P * LANES), BF16),
                   jax.ShapeDtypeStruct((S, P * LANES), BF16)],
        compiler_params=pltpu.CompilerParams(vmem_limit_bytes=VMEM_LIMIT),
    )(z, bias)


def _inner_window(w):
    inner = 1 << ((w.bit_length() - 1) // 2)
    assert w % inner == 0
    return inner


def _over_count(x, cnt, w):
    assert w & (w - 1) == 0 and w <= MAX_WINDOW < x.shape[0]
    return jnp.concatenate([x[:MAX_WINDOW] / cnt[:MAX_WINDOW], x[MAX_WINDOW:] * (1.0 / w)], axis=0)


def _window_mean_minus_self(u, pad_ref, w, S):
    pad_ref[0:MAX_WINDOW, :] = jnp.zeros((MAX_WINDOW, LANES), F32)
    pad_ref[MAX_WINDOW:MAX_WINDOW + S, :] = u
    inner = _inner_window(w)
    acc = u
    for j in range(1, inner):
        acc = acc + pad_ref[MAX_WINDOW - j:MAX_WINDOW - j + S, :]
    if inner > 1:
        pad_ref[MAX_WINDOW:MAX_WINDOW + S, :] = acc
    for j in range(inner, w, inner):
        acc = acc + pad_ref[MAX_WINDOW - j:MAX_WINDOW - j + S, :]
    t = lax.broadcasted_iota(jnp.int32, (S, LANES), 0)
    cnt = jnp.minimum(t + 1, w).astype(F32)
    return _over_count(acc, cnt, w) - u, cnt


def _pool_fwd(proj, pool_w, pool_scale):
    S = proj.shape[0]
    G = len(POOL_WINDOWS)

    def body(u_ref, w_ref, s_ref, y_ref, pad_ref):
        g = pl.program_id(0)
        for gi, w in enumerate(POOL_WINDOWS):
            @pl.when(g == gi)
            def _():
                d, _ = _window_mean_minus_self(u_ref[...].astype(F32), pad_ref, w, S)
                y = jnp.dot(d.astype(BF16), w_ref[0].astype(BF16), preferred_element_type=F32)
                y_ref[...] = (y * s_ref[...]).astype(BF16)

    return pl.pallas_call(
        body, name="pool_fwd", grid=(G,),
        in_specs=[pl.BlockSpec((S, LANES), lambda g: (0, g)),
                  pl.BlockSpec((1, LANES, LANES), lambda g: (g, 0, 0)),
                  pl.BlockSpec((1, LANES), lambda g: (0, g))],
        out_specs=pl.BlockSpec((S, LANES), lambda g: (0, g)),
        out_shape=jax.ShapeDtypeStruct((S, G * LANES), BF16),
        scratch_shapes=[pltpu.VMEM((S + MAX_WINDOW, LANES), F32)],
        compiler_params=_params("arbitrary"),
    )(proj, pool_w, pool_scale)


def _head_halves(rows):
    lane = lax.broadcasted_iota(jnp.int32, (rows, LANES), 1)
    return lane, (lane < HEAD_DIM, lane >= HEAD_DIM)


def _attn_fwd(proj, qaug, kaug):
    S = proj.shape[0]
    W = proj.shape[1] // 6
    P = W // LANES
    tk = min(TQ, S // 2)
    tq = 2 * tk
    nq = S // tq
    qc, kc, vc = 2 * P, 3 * P, 4 * P
    scale = 1.0 / math.sqrt(HEAD_DIM)

    def body(q_ref, k_ref, v_ref, qa_ref, ka_ref, o_ref, qb_ref, qm_scr, m_scr, acc_scr):
        i = pl.program_id(1)
        lane, halves = _head_halves(tq)
        key_halves = (halves[0][:tk], halves[1][:tk])
        v_ones = ((lane[:tk] & (HEAD_DIM - 1)) < 3).astype(BF16)
        qs = q_ref[...] * scale
        qm_scr[0] = jnp.where(halves[0], qs, qa_ref[...])
        qm_scr[1] = jnp.where(halves[1], qs, qa_ref[...])
        m_scr[...] = jnp.full(m_scr.shape, NEG_INF, F32)
        acc_scr[...] = jnp.zeros(acc_scr.shape, F32)
        top, bottom, both = slice(0, tk), slice(tk, tq), slice(0, tq)

        def update(rows, j, on_diagonal):
            keys = pl.ds(pl.multiple_of(j * tk, tk), tk)
            k2, v2, kaug_t = k_ref[keys, :], v_ref[keys, :], ka_ref[keys, :]
            if on_diagonal:
                keep = (lax.broadcasted_iota(jnp.int32, (tk, tk), 0)
                        >= lax.broadcasted_iota(jnp.int32, (tk, tk), 1))
            logits = [lax.dot_general(qm_scr[a, rows, :], jnp.where(key_halves[a], k2, kaug_t), NT,
                                      preferred_element_type=F32) for a in range(2)]
            for a in range(2):
                s = jnp.where(keep, logits[a], NEG_INF) if on_diagonal else logits[a]
                va = jnp.where(key_halves[a], v2, v_ones)
                m_prev = m_scr[a, rows, :]
                m_new = jnp.maximum(m_prev, jnp.max(s, axis=1, keepdims=True))
                p = jnp.exp(s - jnp.tile(m_new, (1, tk // LANES)))
                acc_scr[a, rows, :] = (jnp.exp(m_prev - m_new) * acc_scr[a, rows, :]
                                       + jnp.dot(p.astype(BF16), va, preferred_element_type=F32))
                m_scr[a, rows, :] = m_new

        def below_diagonal(jj, carry):
            update(both, 2 * jj, False)
            update(both, 2 * jj + 1, False)
            return carry

        lax.fori_loop(0, i, below_diagonal, 0)
        update(top, 2 * i, True)
        update(bottom, 2 * i, False)
        update(bottom, 2 * i + 1, True)
        acc_a, acc_b = acc_scr[0], acc_scr[1]
        l_a, l_b = acc_a[:, HEAD_DIM:HEAD_DIM + 1], acc_b[:, 0:1]
        o_ref[...] = jnp.where(halves[0], acc_a / l_a, acc_b / l_b).astype(BF16)
        lse = jnp.where(halves[0], m_scr[1] + jnp.log(l_b), m_scr[0] + jnp.log(l_a))
        slot = lane & (HEAD_DIM - 1)
        aug = qa_ref[...].astype(F32)
        for n, part in enumerate(_split3(lse)):
            aug = jnp.where(slot == SLOT_LSE + n, -part, aug)
        qb_ref[...] = aug.astype(BF16)

    tile = lambda col: pl.BlockSpec((tq, LANES), lambda p, i: (i, col + p))
    whole = lambda col: pl.BlockSpec((S, LANES), lambda p, i: (0, col + p))
    return pl.pallas_call(
        body, name="attn_fwd", grid=(P, nq),
        in_specs=[tile(qc), whole(kc), whole(vc), tile(0), whole(0)],
        out_specs=[tile(0), tile(0)],
        out_shape=[jax.ShapeDtypeStruct((S, W), BF16), jax.ShapeDtypeStruct((S, W), BF16)],
        scratch_shapes=[pltpu.VMEM((2, tq, LANES), BF16),
                        pltpu.VMEM((2, tq, LANES), F32),
                        pltpu.VMEM((2, tq, LANES), F32)],
        compiler_params=_params("parallel", "arbitrary"),
    )(proj, proj, proj, qaug, kaug)


def _outproj_fwd(ypool, o, proj, x, wout, nxt=None, head=None):
    S, D = x.shape
    W = D // 2
    tm, tn = min(TM, S), TN

    def body(y_ref, o_ref, pg_ref, ag_ref, x_ref, w_ref, *rest):
        mix_ref = rest[-1]
        out_ref = rest[2]
        pg, ag = pg_ref[...].astype(F32), ag_ref[...].astype(F32)
        mix_ref[:, 0:W] = (y_ref[...].astype(F32) * (pg * _sigmoid(pg))).astype(BF16)
        mix_ref[:, W:D] = (o_ref[...].astype(F32) * (ag * _sigmoid(ag))).astype(BF16)
        for n in range(D // tn):
            cols = slice(n * tn, (n + 1) * tn)
            out_ref[:, cols] = x_ref[:, cols] + jnp.dot(mix_ref[...], w_ref[:, cols], preferred_element_type=F32)
        if nxt:
            _inproj_tile(out_ref[...], rest[0], rest[1], *rest[3:7])
            return
        gam_ref, t_ref = rest[:2]
        loss_ref, dg_ref = rest[-3:-1]

        @pl.when(pl.program_id(0) == 0)
        def _():
            loss_ref[...] = jnp.zeros(loss_ref.shape, F32)
            dg_ref[...] = jnp.zeros(dg_ref.shape, F32)

        xf, gam_v = out_ref[...], gam_ref[...]
        r = lax.rsqrt(jnp.mean(xf * xf, axis=-1, keepdims=True) + RMS_EPS)
        xhat = xf * r
        err = xhat * gam_v - t_ref[...]
        part = jnp.sum(jnp.sum(err * err, axis=-1, keepdims=True), axis=0, keepdims=True)
        loss_ref[...] += part * (0.5 / D)
        dy = err * (1.0 / D)
        dg_ref[...] += jnp.sum(dy * xhat, axis=0, keepdims=True)
        dxhat = dy * gam_v
        out_ref[...] = r * (dxhat - xhat * jnp.mean(dxhat * xhat, axis=-1, keepdims=True))

    assert (nxt is None) != (head is None)
    rows = lambda width, col: pl.BlockSpec((tm, width), lambda i: (i, col))
    in_specs = [rows(W, 0), rows(W, 0), rows(W, 1), rows(W, 5), rows(D, 0),
                pl.BlockSpec((D, D), lambda i: (0, 0), pipeline_mode=pl.Buffered(1))]
    out_specs = [rows(D, 0)]
    out_shape = [jax.ShapeDtypeStruct((S, D), F32)]
    if nxt:
        more_in, more_out, more_shape = _inproj_specs(nxt[1], S, D, tm)
        in_specs += more_in
        out_specs += more_out
        out_shape += more_shape
    else:
        in_specs += [pl.BlockSpec((1, D), lambda i: (0, 0)), rows(D, 0)]
        out_specs += [pl.BlockSpec((8, LANES), lambda i: (0, 0)), pl.BlockSpec((1, D), lambda i: (0, 0))]
        out_shape += [jax.ShapeDtypeStruct((8, LANES), F32), jax.ShapeDtypeStruct((1, D), F32)]
    return pl.pallas_call(
        body, name="outproj_fwd_loss" if head else "outproj_inproj_fwd", grid=(S // tm,),
        in_specs=in_specs, out_specs=out_specs, out_shape=out_shape,
        scratch_shapes=[pltpu.VMEM((tm, D), BF16)],
        compiler_params=_params("arbitrary"),
    )(ypool, o, proj, proj, x, wout, *(nxt or head))


def _outproj_bwd(g, wout, ypool, o, proj, after=()):
    S, D = g.shape
    W = D // 2
    tm = min(TM, S)

    def body(g_ref, w_ref, y_ref, o_ref, pg_ref, ag_ref, *rest):
        dw_ref, dwb_ref, da_ref, dgate_ref, doaug_ref = rest[-5:]

        @pl.when(pl.program_id(0) == 0)
        def _():
            dw_ref[...] = jnp.zeros(dw_ref.shape, F32)

        gb = g_ref[...].astype(BF16)
        dmixes = [lax.dot_general(gb, w_ref[half * W:(half + 1) * W, :], NT, preferred_element_type=F32)
                  for half in range(2)]
        d_o = None
        for half, (val_ref, gate_ref) in enumerate(((y_ref, pg_ref), (o_ref, ag_ref))):
            cols = slice(half * W, (half + 1) * W)
            gt = gate_ref[...].astype(F32)
            sg = _sigmoid(gt)
            silu = gt * sg
            val = val_ref[...].astype(F32)
            dw_ref[cols, :] += lax.dot_general((val * silu).astype(BF16), gb, TN_DIMS, preferred_element_type=F32)
            d_o = (dmixes[half] * silu).astype(BF16)
            da_ref[:, cols] = d_o
            dgate_ref[:, cols] = (dmixes[half] * val * (sg * (1.0 + gt * (1.0 - sg)))).astype(BF16)

        lane, halves = _head_halves(tm)
        slot = lane & (HEAD_DIM - 1)
        for p in range(W // LANES):
            cols = slice(p * LANES, (p + 1) * LANES)
            prod = d_o[:, cols].astype(F32) * o_ref[:, cols].astype(F32)
            d_a = jnp.sum(jnp.where(halves[0], prod, 0.0), axis=1, keepdims=True)
            d_b = jnp.sum(jnp.where(halves[1], prod, 0.0), axis=1, keepdims=True)
            aug = jnp.zeros((tm, LANES), F32)
            for n, part in enumerate(_split3(jnp.where(halves[0], d_b, d_a))):
                aug = jnp.where(slot == SLOT_C + n, -part, aug)
            doaug_ref[:, cols] = aug.astype(BF16)

        @pl.when(pl.program_id(0) == S // tm - 1)
        def _():
            dwb_ref[...] = dw_ref[...].astype(BF16)

    rows = lambda width, col: pl.BlockSpec((tm, width), lambda i: (i, col))
    whole = pl.BlockSpec((D, D), lambda i: (0, 0))
    return pl.pallas_call(
        body, name="outproj_bwd", grid=(S // tm,),
        in_specs=[rows(D, 0), whole, rows(W, 0), rows(W, 0), rows(W, 1), rows(W, 5)] + _after_specs(after),
        out_specs=[whole, whole, rows(D, 0), rows(D, 0), rows(W, 0)],
        out_shape=[jax.ShapeDtypeStruct((D, D), F32),
                   jax.ShapeDtypeStruct((D, D), BF16),
                   jax.ShapeDtypeStruct((S, D), BF16),
                   jax.ShapeDtypeStruct((S, D), BF16),
                   jax.ShapeDtypeStruct((S, W), BF16)],
        compiler_params=_params("arbitrary"),
    )(g, wout, ypool, o, proj, proj, *after)


def _section_specs(sections, rows, width):
    specs = [pl.BlockSpec((rows, width), lambda k, c=c: (k, c)) for _, c in sections]
    return specs, [a for a, _ in sections]


def _inproj_bwd(sections, dzf, h=None, wxg=None, after=()):
    S = dzf.shape[0]
    n_sec = len(sections)
    D = h.shape[1] if h is not None else wxg[1].shape[1]
    W = D // 2
    N = n_sec * W
    ts = min(TM if wxg else 2 * TM, S)
    n_steps = S // ts
    once = pl.Buffered(1)

    def body(dz_ref, *rest):
        sec_refs, rest = rest[:n_sec], rest[n_sec:]
        if h is not None:
            h_ref, rest = rest[0], rest[1:]
        if wxg:
            (w_ref, wf_ref, x_ref, gam_ref, g_ref), rest = rest[:5], rest[5:]
        outs = rest[len(after):]
        step = pl.program_id(0)

        if h is not None:
            dw_ref, dwb_ref = outs[:2]

            @pl.when(step == 0)
            def _():
                dw_ref[...] = jnp.zeros(dw_ref.shape, F32)

            ht = h_ref[...].T
            dw_ref[:, N:N + LANES] += jnp.dot(ht, dz_ref[...], preferred_element_type=F32)
            for n, ref in enumerate(sec_refs):
                dw_ref[:, n * W:(n + 1) * W] += jnp.dot(ht, ref[...], preferred_element_type=F32)

            @pl.when(step == n_steps - 1)
            def _():
                dwb_ref[...] = dw_ref[...].astype(BF16)

        if wxg:
            dx_ref, dg_ref = outs[-2:]

            @pl.when(step == 0)
            def _():
                dg_ref[...] = jnp.zeros(dg_ref.shape, F32)

            dh = lax.dot_general(dz_ref[...], wf_ref[...], NT, preferred_element_type=F32)
            for n, ref in enumerate(sec_refs):
                dh = dh + lax.dot_general(ref[...], w_ref[:, n * W:(n + 1) * W], NT, preferred_element_type=F32)
            xf = x_ref[...]
            r = lax.rsqrt(jnp.mean(xf * xf, axis=-1, keepdims=True) + RMS_EPS)
            xhat = xf * r
            dg_ref[...] += jnp.sum(dh * xhat, axis=0, keepdims=True)
            dxhat = dh * gam_ref[...]
            dx_ref[...] = g_ref[...] + r * (dxhat - xhat * jnp.mean(dxhat * xhat, axis=-1, keepdims=True))

    rows = lambda width: pl.BlockSpec((ts, width), lambda k: (k, 0))
    sec_specs, operands = _section_specs(sections, ts, W)
    in_specs, out_specs, out_shape = [rows(LANES)] + sec_specs, [], []
    if h is not None:
        whole = pl.BlockSpec((D, N + LANES), lambda k: (0, 0), pipeline_mode=once)
        in_specs.append(rows(D))
        operands.append(h)
        out_specs += [whole, whole]
        out_shape += [jax.ShapeDtypeStruct((D, N + LANES), F32), jax.ShapeDtypeStruct((D, N + LANES), BF16)]
    if wxg:
        w, x, gam, g = wxg
        in_specs += [pl.BlockSpec((D, N), lambda k: (0, 0), pipeline_mode=once),
                     pl.BlockSpec((D, LANES), lambda k: (0, N // LANES), pipeline_mode=once),
                     rows(D), pl.BlockSpec((1, D), lambda k: (0, 0)), rows(D)]
        operands += [w, w, x, gam, g]
        out_specs += [rows(D), pl.BlockSpec((1, D), lambda k: (0, 0))]
        out_shape += [jax.ShapeDtypeStruct((S, D), F32), jax.ShapeDtypeStruct((1, D), F32)]
    name = "inproj_bwd" if h is not None and wxg else ("inproj_bwd_dw" if wxg is None else "inproj_bwd_dx")
    return pl.pallas_call(
        body, name=name, grid=(n_steps,),
        in_specs=in_specs + _after_specs(after), out_specs=out_specs, out_shape=out_shape,
        compiler_params=_params("arbitrary"),
    )(dzf, *operands, *after)


def _attn_bwd(proj, da, qaug, kaug, doaug, after=()):
    S = proj.shape[0]
    W = proj.shape[1] // 6
    P = W // LANES
    tq = min(TQ, S)
    nq = S // tq
    qc, kc, vc = 2 * P, 3 * P, 4 * P
    scale = 1.0 / math.sqrt(HEAD_DIM)

    def body(q_ref, k_ref, v_ref, do_ref, qa_ref, ka_ref, da_ref, *rest):
        dq_ref, dk_ref, dv_ref, drows_ref, dcols_ref, km_scr, vm_scr, dk_scr, dv_scr, dq_scr = rest[len(after):]
        pair, j = pl.program_id(0), pl.program_id(1)
        lane, halves = _head_halves(tq)

        @pl.when(jnp.logical_and(pair == 0, j == 0))
        def _():
            drows_ref[...] = jnp.zeros(drows_ref.shape, F32)
            dcols_ref[...] = jnp.zeros(dcols_ref.shape, F32)

        @pl.when(j == 0)
        def _():
            dq_scr[...] = jnp.zeros(dq_scr.shape, F32)

        v_ones = ((lane & (HEAD_DIM - 1)) < 3).astype(BF16)
        for a in range(2):
            km_scr[a] = jnp.where(halves[a], k_ref[...], ka_ref[...])
            vm_scr[a] = jnp.where(halves[a], v_ref[...], v_ones)
        dk_scr[...] = jnp.zeros(dk_scr.shape, F32)
        dv_scr[...] = jnp.zeros(dv_scr.shape, F32)

        def update(i, on_diagonal):
            rows = pl.ds(pl.multiple_of(i * tq, tq), tq)
            qs = q_ref[rows, :] * scale
            do2, qaug_t, doaug_t = do_ref[rows, :], qa_ref[rows, :], da_ref[rows, :]
            if on_diagonal:
                keep = (lax.broadcasted_iota(jnp.int32, (tq, tq), 0)
                        >= lax.broadcasted_iota(jnp.int32, (tq, tq), 1))
            qas = [jnp.where(halves[a], qs, qaug_t) for a in range(2)]
            logits = [lax.dot_general(qas[a], km_scr[a], NT, preferred_element_type=F32) for a in range(2)]
            dps = [lax.dot_general(jnp.where(halves[a], do2, doaug_t), vm_scr[a], NT, preferred_element_type=F32)
                   for a in range(2)]
            dv = None
            for a in range(2):
                s = jnp.where(keep, logits[a], NEG_INF) if on_diagonal else logits[a]
                p = jnp.exp(s)
                dsb = (p * dps[a]).astype(BF16)
                do0 = jnp.where(halves[a], do2, jnp.zeros_like(do2))
                dv_a = lax.dot_general(p.astype(BF16), do0, TN_DIMS, preferred_element_type=F32)
                dv = dv_a if dv is None else dv + dv_a
                dk_scr[a] += lax.dot_general(dsb, qas[a], TN_DIMS, preferred_element_type=F32)
                dq_scr[a, rows, :] += jnp.dot(dsb, km_scr[a], preferred_element_type=F32)
            dv_scr[...] += dv

        def below_diagonal(n, carry):
            update(j + 1 + 2 * n, False)
            update(j + 2 + 2 * n, False)
            return carry

        update(j, True)
        below = nq - 1 - j
        lax.fori_loop(0, below // 2, below_diagonal, 0)

        @pl.when(below % 2 == 1)
        def _():
            update(nq - 1, False)


        def to_head_lanes(old, first, second):
            at = lax.broadcasted_iota(jnp.int32, old.shape, 1) - 2 * pair
            return jnp.where(at == 0, first, jnp.where(at == 1, second, old))

        dk_ref[...] = jnp.where(halves[0], dk_scr[0], dk_scr[1]).astype(BF16)
        dv_ref[...] = dv_scr[...].astype(BF16)
        keys = pl.ds(pl.multiple_of(j * tq, tq), tq)
        ones_a, ones_b = HEAD_DIM + SLOT_ONE, SLOT_ONE
        dcols_ref[keys, :] = to_head_lanes(dcols_ref[keys, :], dk_scr[0][:, ones_a:ones_a + 1],
                                           dk_scr[1][:, ones_b:ones_b + 1])

        @pl.when(j == nq - 1)
        def _():
            row_lane, row_halves = _head_halves(S)
            dq_ref[...] = (jnp.where(row_halves[0], dq_scr[0], dq_scr[1]) * scale).astype(BF16)
            c_a, c_b = HEAD_DIM + SLOT_C, SLOT_C
            drows_ref[...] = to_head_lanes(drows_ref[...], dq_scr[0][:, c_a:c_a + 1], dq_scr[1][:, c_b:c_b + 1])

    tile = lambda col: pl.BlockSpec((tq, LANES), lambda p, j: (j, col + p))
    whole = lambda col: pl.BlockSpec((S, LANES), lambda p, j: (0, col + p))
    shared = pl.BlockSpec((S, LANES), lambda p, j: (0, 0))
    return pl.pallas_call(
        body, name="attn_bwd", grid=(P, nq),
        in_specs=[whole(qc), tile(kc), tile(vc), whole(P), whole(0), tile(0), whole(0)] + _after_specs(after),
        out_specs=[whole(0), tile(0), tile(0), shared, shared],
        out_shape=[jax.ShapeDtypeStruct((S, W), BF16),
                   jax.ShapeDtypeStruct((S, W), BF16),
                   jax.ShapeDtypeStruct((S, W), BF16),
                   jax.ShapeDtypeStruct((S, LANES), F32),
                   jax.ShapeDtypeStruct((S, LANES), F32)],
        scratch_shapes=[pltpu.VMEM((2, tq, LANES), BF16),
                        pltpu.VMEM((2, tq, LANES), BF16),
                        pltpu.VMEM((2, tq, LANES), F32),
                        pltpu.VMEM((tq, LANES), F32),
                        pltpu.VMEM((2, S, LANES), F32)],
        compiler_params=_params("arbitrary", "arbitrary"),
    )(proj, proj, proj, da, qaug, kaug, doaug, *after)


def _fgate_bwd(drows, dcols, z, bias):
    S = z.shape[0]
    tb = min(TB, S)
    nb = S // tb

    def body(drows_ref, dcols_ref, z_ref, b_ref, dz_ref, db_ref):
        tri = (lax.broadcasted_iota(jnp.int32, (tb, tb), 1)
               >= lax.broadcasted_iota(jnp.int32, (tb, tb), 0)).astype(BF16)

        local = []
        for b in range(nb):
            rows = slice(b * tb, (b + 1) * tb)
            local.append(_dot_exact_left(tri, drows_ref[rows, :] - dcols_ref[rows, :]))
        carry = jnp.zeros((1, LANES), F32)
        db = jnp.zeros((1, LANES), F32)
        for b in reversed(range(nb)):
            rows = slice(b * tb, (b + 1) * tb)
            rc = local[b] + carry
            carry = rc[0:1, :]
            dz = rc * _sigmoid(-(z_ref[rows, :] + b_ref[...]))
            dz_ref[rows, :] = dz.astype(BF16)
            db = db + jnp.sum(dz, axis=0, keepdims=True)
        db_ref[...] = db

    return pl.pallas_call(
        body, name="fgate_bwd",
        out_shape=[jax.ShapeDtypeStruct((S, LANES), BF16),
                   jax.ShapeDtypeStruct((1, LANES), F32)],
        compiler_params=pltpu.CompilerParams(vmem_limit_bytes=VMEM_LIMIT),
    )(drows, dcols, z, bias)


def _pool_bwd(proj, da, pool_w, pool_scale):
    S = proj.shape[0]
    G = len(POOL_WINDOWS)

    def body(u_ref, dy_ref, w_ref, s_ref, du_ref, dw_ref, ds_ref, pad_ref):
        g = pl.program_id(0)
        for gi, w in enumerate(POOL_WINDOWS):
            @pl.when(g == gi)
            def _():
                d, cnt = _window_mean_minus_self(u_ref[...].astype(F32), pad_ref, w, S)
                db = d.astype(BF16)
                wb = w_ref[0].astype(BF16)
                yraw = jnp.dot(db, wb, preferred_element_type=F32)
                dy = dy_ref[...].astype(F32)
                ds_ref[...] = jnp.sum(dy * yraw, axis=0, keepdims=True)
                dzb = (dy * s_ref[...]).astype(BF16)
                dw_ref[0] = lax.dot_general(db, dzb, TN_DIMS, preferred_element_type=F32)
                dd = lax.dot_general(dzb, wb, NT, preferred_element_type=F32)
                acc = _over_count(dd, cnt, w)
                pad_ref[0:S, :] = acc
                pad_ref[S:S + MAX_WINDOW, :] = jnp.zeros((MAX_WINDOW, LANES), F32)
                inner = _inner_window(w)
                for j in range(1, inner):
                    acc = acc + pad_ref[j:j + S, :]
                if inner > 1:
                    pad_ref[0:S, :] = acc
                for j in range(inner, w, inner):
                    acc = acc + pad_ref[j:j + S, :]
                du_ref[...] = (acc - dd).astype(BF16)

    return pl.pallas_call(
        body, name="pool_bwd", grid=(G,),
        in_specs=[pl.BlockSpec((S, LANES), lambda g: (0, g)),
                  pl.BlockSpec((S, LANES), lambda g: (0, g)),
                  pl.BlockSpec((1, LANES, LANES), lambda g: (g, 0, 0)),
                  pl.BlockSpec((1, LANES), lambda g: (0, g))],
        out_specs=[pl.BlockSpec((S, LANES), lambda g: (0, g)),
                   pl.BlockSpec((1, LANES, LANES), lambda g: (g, 0, 0)),
                   pl.BlockSpec((1, LANES), lambda g: (0, g))],
        out_shape=[jax.ShapeDtypeStruct((S, G * LANES), BF16),
                   jax.ShapeDtypeStruct((G, LANES, LANES), F32),
                   jax.ShapeDtypeStruct((1, G * LANES), F32)],
        scratch_shapes=[pltpu.VMEM((S + MAX_WINDOW, LANES), F32)],
        compiler_params=_params("arbitrary"),
    )(proj, da, pool_w, pool_scale)


def _adamw(w, m, v, gsets, name, rows, shifted=False, first=0, into=None):
    A, R, C = w.shape
    n_sets = len(gsets)
    tr = min(rows, R)
    c1 = 1.0 / (1.0 - ADAM_B1 ** ADAM_STEP)
    c2 = 1.0 / (1.0 - ADAM_B2 ** ADAM_STEP)
    counts = [len(gs) for gs in gsets]

    def body(w_ref, m_ref, v_ref, *rest):
        g_ref, d_ref, nm_ref, nv_ref = rest[-4:]
        at = 0
        for a in range(n_sets):
            part_refs = rest[at:at + counts[a]]
            at += counts[a]

            @pl.when(pl.program_id(0) == a)
            def _():
                g = None
                for ref in part_refs:
                    for s in range(ref.shape[0]):
                        term = ref[s].astype(F32)
                        g = term if g is None else g + term
                if shifted:
                    lanes = g.shape[1]
                    g = pltpu.roll(g, (lanes - _index(_position()) * (C % LANES)) % lanes, axis=1)[:, :C]
                nm = ADAM_B1 * m_ref[0] + (1.0 - ADAM_B1) * g
                nv = ADAM_B2 * v_ref[0] + (1.0 - ADAM_B2) * (g * g)
                g_ref[0] = g
                nm_ref[0] = nm
                nv_ref[0] = nv
                d_ref[0] = -ADAM_LR * ((nm * c1) / (jnp.sqrt(nv * c2) + ADAM_EPS) + ADAM_WD * w_ref[0])

    spec = pl.BlockSpec((1, tr, C), lambda a, r: (first + a, r, 0))
    part_specs = [pl.BlockSpec((part.shape[0], tr, part.shape[2]), lambda a, r, l=l: (0, jnp.where(a == l, r, 0), 0))
                  for l, gs in enumerate(gsets) for part in gs]
    parts = [part for gs in gsets for part in gs]
    shape = jax.ShapeDtypeStruct((A, R, C), F32)
    earlier = () if into is None else tuple(into)
    return pl.pallas_call(
        body, name=name, grid=(n_sets, R // tr),
        in_specs=[spec, spec, spec] + part_specs + _after_specs(earlier),
        out_specs=[spec, spec, spec, spec],
        out_shape=[shape, shape, shape, shape],
        input_output_aliases={3 + len(parts) + n: n for n in range(len(earlier))},
        compiler_params=_params("arbitrary", "arbitrary"),
    )(w, m, v, *parts, *earlier)


def _position():
    return lax.axis_index("x"), lax.axis_index("y"), lax.axis_index("c")


def _index(dev):
    return 4 * dev[0] + 2 * dev[1] + dev[2]


def _all_gather(arrs, slots, out_shapes, name):
    n_arr = len(arrs)

    def body(*refs):
        ins, outs = refs[:n_arr], refs[n_arr:2 * n_arr]
        send_sems, recv_sems, local_sems = refs[2 * n_arr:]
        x, y, c = _position()
        me, sibling = (x, y, c), (x, y, 1 - c)
        chips = [(1 - x, y), (x, 1 - y), (1 - x, 1 - y)]

        def copy(a, k, block, to, src=None):
            part = slots[a](outs[a], _index(block))
            return pltpu.make_async_remote_copy(
                src_ref=part if src is None else src, dst_ref=part,
                send_sem=send_sems.at[a, k], recv_sem=recv_sems.at[a, k],
                device_id=to, device_id_type=MESH)

        mine = [pltpu.make_async_copy(ins[a], slots[a](outs[a], _index(me)), local_sems.at[a])
                for a in range(n_arr)]
        for cp in mine:
            cp.start()
        first = []
        for a in range(n_arr):
            first.append(copy(a, 0, me, sibling, src=ins[a]))
            first += [copy(a, 1 + j, me, (*chip, c), src=ins[a]) for j, chip in enumerate(chips)]
        for cp in first:
            cp.start()
        passed = []
        for j, chip in enumerate(chips):
            for a in range(n_arr):
                copy(a, 1 + j, (*chip, c), me).wait_recv()
                fwd = copy(a, 4 + j, (*chip, c), sibling)
                fwd.start()
                passed.append(fwd)
        for a in range(n_arr):
            copy(a, 0, sibling, me).wait_recv()
            for j, chip in enumerate(chips):
                copy(a, 4 + j, (*chip, 1 - c), me).wait_recv()
        for cp in first + passed:
            cp.wait_send()
        for cp in mine:
            cp.wait()

    any_spec = pl.BlockSpec(memory_space=pl.ANY)
    return pl.pallas_call(
        body, name=name,
        in_specs=[any_spec] * n_arr, out_specs=[any_spec] * n_arr, out_shape=out_shapes,
        scratch_shapes=[pltpu.SemaphoreType.DMA((n_arr, 7)), pltpu.SemaphoreType.DMA((n_arr, 7)),
                        pltpu.SemaphoreType.DMA((n_arr,))],
    )(*arrs)


def _split_copies(srcs, lands, send_sems, recv_sems, kinds):
    x, y, c = _position()
    me = _index((x, y, c))
    copies = []
    for a, (src_part, land_part) in enumerate(kinds):
        for k in range(1, N_DEV):
            peer = (x ^ ((k >> 2) & 1), y ^ ((k >> 1) & 1), c ^ (k & 1))
            copies.append(pltpu.make_async_remote_copy(
                src_ref=src_part(srcs[a], _index(peer)), dst_ref=land_part(lands[a], me, k),
                send_sem=send_sems[a].at[k - 1], recv_sem=recv_sems[a].at[k - 1],
                device_id=peer, device_id_type=MESH))
    return copies


def _split_start(srcs, lands, kinds, name, after=()):
    n = len(srcs)

    def body(*refs):
        src_refs, land_refs = refs[:n], refs[n:2 * n]
        outs = refs[2 * n + len(after):]
        send_sems, recv_sems = outs[:n], outs[n:2 * n]
        token = outs[-1]
        for cp in _split_copies(src_refs, land_refs, send_sems, recv_sems, kinds):
            cp.start()
        token[...] = jnp.zeros(token.shape, token.dtype)

    hbm = pl.BlockSpec(memory_space=pltpu.HBM)
    sem = pl.BlockSpec(memory_space=pltpu.SEMAPHORE)
    operands = [pltpu.with_memory_space_constraint(t, pltpu.HBM) for t in (*srcs, *lands)]
    out = pl.pallas_call(
        body, name=name,
        in_specs=[hbm] * (2 * n) + _after_specs(after),
        out_specs=[sem] * (2 * n) + [hbm] * (2 * n) + [pl.BlockSpec(memory_space=pltpu.VMEM)],
        out_shape=[pltpu.SemaphoreType.DMA((N_DEV - 1,))] * (2 * n)
        + [pltpu.HBM(t.shape, t.dtype) for t in operands] + [jax.ShapeDtypeStruct((8, LANES), F32)],
        input_output_aliases={i: 2 * n + i for i in range(2 * n)},
        compiler_params=pltpu.CompilerParams(has_side_effects=pltpu.SideEffectType.DATAFLOW_SIDE_EFFECTING),
    )(*operands, *after)
    return [(out[a], out[n + a], out[2 * n + a], out[3 * n + a]) for a in range(n)], out[-1]


def _split_wait(started, kinds, after, name):
    n = len(started)
    sems = [t[0] for t in started] + [t[1] for t in started]
    srcs = [t[2] for t in started]
    lands = [t[3] for t in started]

    def body(*refs):
        src_refs, land_refs = refs[:n], refs[n:2 * n]
        send_sems, recv_sems = refs[2 * n:3 * n], refs[3 * n:4 * n]
        for cp in _split_copies(src_refs, land_refs, send_sems, recv_sems, kinds):
            cp.wait_send()
            cp.wait_recv()

    hbm = pl.BlockSpec(memory_space=pltpu.HBM)
    sem = pl.BlockSpec(memory_space=pltpu.SEMAPHORE)
    out = pl.pallas_call(
        body, name=name,
        in_specs=[hbm] * (2 * n) + [sem] * (2 * n) + _after_specs(after),
        out_specs=[hbm] * (2 * n),
        out_shape=[pltpu.HBM(t.shape, t.dtype) for t in (*srcs, *lands)],
        input_output_aliases={i: i for i in range(2 * n)},
        compiler_params=pltpu.CompilerParams(has_side_effects=pltpu.SideEffectType.DATAFLOW_SIDE_EFFECTING),
    )(*srcs, *lands, *sems, *after)
    return out[n:]


def _as_rows(p):
    if p.size % LANES == 0:
        rows = p.reshape(-1, LANES)
    else:
        rows = p.reshape(-1, p.shape[-1])
        rows = jnp.pad(rows, ((0, 0), (0, LANES - rows.shape[1])))
    return jnp.pad(rows, ((0, -rows.shape[0] % 8), (0, 0)))


def _pack(parts):
    return jnp.concatenate([_as_rows(p) for p in parts])[None]


def _unpack(packed, like):
    out, at = [], 0
    for p in like:
        whole = p.size % LANES == 0
        n = p.size // LANES if whole else p.size // p.shape[-1]
        rows = packed[0, at:at + n]
        out.append((rows if whole else rows[:, :p.shape[-1]]).reshape(p.shape))
        at += n + (-n % 8)
    return out


def _local_step(x, target, norm_g, forget_bias, pool_w, pool_scale, final_g, weights_in, weights_out, on_grads,
                first_after=(), on_first_out_grads=lambda d_wout, d_wout_bf16: ()):
    L = norm_g.shape[0]
    S, D = x.shape
    W = D // 2
    H = W // HEAD_DIM
    bias = jnp.pad(forget_bias, ((0, 0), (0, LANES - H)))

    saved = []
    nxt = _inproj_fwd(x, norm_g[0:1], weights_in(0, x), tuple(first_after))
    for l in range(L):
        x_in, (proj, h, z, w) = x, nxt
        qaug, kaug = _fgate_fwd(z, bias[l:l + 1], H)
        ypool = _pool_fwd(proj, pool_w[l], pool_scale[l:l + 1])
        o, qaug_b = _attn_fwd(proj, qaug, kaug)
        wout = weights_out(l, o)
        if l < L - 1:
            x, *nxt = _outproj_fwd(ypool, o, proj, x_in, wout, nxt=(norm_g[l + 1:l + 2], weights_in(l + 1, o)))
        else:
            g, loss, d_final_g = _outproj_fwd(ypool, o, proj, x_in, wout, head=(final_g.reshape(1, D), target))
        saved.append((x_in, proj, h, z, qaug_b, kaug, ypool, o, w, wout))

    small, after = None, ()
    for l in reversed(range(L)):
        x_in, proj, h, z, qaug_b, kaug, ypool, o, w, wout = saved[l]
        d_wout, d_wout_bf16, da, dgate, doaug = _outproj_bwd(g, wout, ypool, o, proj, after)
        early = tuple(on_first_out_grads(d_wout, d_wout_bf16)) if l == 0 else ()
        dq, dk, dv, drows, dcols = _attn_bwd(proj, da, qaug_b, kaug, doaug, early)
        dzf, db = _fgate_bwd(drows, dcols, z, bias[l:l + 1])
        dpu, dpw, dps = _pool_bwd(proj, da, pool_w[l], pool_scale[l:l + 1])
        dproj = [(dpu, 0), (dgate, 0), (dq, 0), (dk, 0), (dv, 0), (dgate, 1)]
        wxg = (w, x_in, norm_g[l:l + 1], g)
        if l > 0:
            d_w, d_w_bf16, g, dgam = _inproj_bwd(dproj, dzf, h, wxg)
            after = tuple(on_grads(l, d_w, d_w_bf16, d_wout, d_wout_bf16, small))
        else:
            d_w, d_w_bf16 = _inproj_bwd(dproj, dzf, h)
            after = tuple(on_grads(l, d_w, d_w_bf16, d_wout, d_wout_bf16, small))
            g, dgam = _inproj_bwd(dproj, dzf, None, wxg, after)
        small = (dgam[0], db[0, :H], dpw, dps[0])
    return loss[0, 0], g, small, d_final_g[0]


def kernel(x, norm_g, w_in, forget_bias, pool_w, pool_scale, w_out, final_g, loss_target, m_norm_g, m_w_in, m_forget_bias, m_pool_w, m_pool_scale, m_w_out, m_final_g, v_norm_g, v_w_in, v_forget_bias, v_pool_w, v_pool_scale, v_w_out, v_final_g):
    L, D, cols = w_in.shape
    rows_out = w_out.shape[1]
    me = _index(_position())
    slot = _slot_width(cols)
    wout_b = w_out.astype(BF16)
    win_b = _shift_slots(w_in)
    gather_in = (lambda ref, peer: ref, lambda ref, mine, k: ref.at[mine])
    gather_out = (lambda ref, peer: ref, lambda ref, mine, k: ref.at[pl.ds(mine * rows_out, rows_out), :])

    def landing(block, n_slots):
        zone = lax.empty((n_slots * block.shape[0], *block.shape[1:]), block.dtype)
        return lax.dynamic_update_slice(zone, block, (me * block.shape[0],) + (0,) * (block.ndim - 1))

    (first_in,) = _all_gather([win_b[0]], [lambda ref, n: ref.at[n]],
                              [jax.ShapeDtypeStruct((N_DEV, D, slot), BF16)], "gather_first")
    rest_srcs = [wout_b[0]] + [w[l] for l in range(1, L) for w in (win_b, wout_b)]
    rest_lands = [landing(wout_b[0], N_DEV)]
    for l in range(1, L):
        rest_lands += [landing(win_b[l][None], N_DEV), landing(wout_b[l], N_DEV)]
    rest_kinds = [gather_out] + [gather_in, gather_out] * (L - 1)
    rest, rest_token = _split_start(rest_srcs, rest_lands, rest_kinds, "gather_start_rest", (first_in,))

    def weights_in(l, x_in):
        if l == 0:
            return first_in
        (win_all,) = _split_wait([rest[2 * l - 1]], [gather_in], (x_in,), f"gather_wait_in_{l}")
        return win_all

    def weights_out(l, o):
        (wout_full,) = _split_wait([rest[2 * l]], [gather_out], (o,), f"gather_wait_out_{l}")
        return wout_full

    stride = slot - LANES
    exchange_kinds = [(lambda ref, peer: ref.at[:, pl.ds(pl.multiple_of(peer * stride, LANES), slot)],
                       lambda ref, mine, k: ref.at[k - 1]),
                      (lambda ref, peer: ref.at[pl.ds(peer * rows_out, rows_out), :],
                       lambda ref, mine, k: ref.at[k - 1])]
    zero_g = jnp.zeros_like(final_g)
    zero_loss = jnp.zeros((LANES,), F32)

    def small_pack(l, norm_g_l, bias_l, pool_w_l, pool_scale_l, final, loss_row=None):
        return _pack([norm_g_l, bias_l, pool_w_l, pool_scale_l, final if l == 0 else zero_g,
                      zero_loss if loss_row is None else loss_row])[0]

    exchanges, own_parts, first_out = {}, {}, []

    def on_first_out_grads(d_wout, d_wout_bf16):
        started, token = _split_start([d_wout_bf16], [lax.empty((N_DEV - 1, rows_out, D), BF16)], exchange_kinds[1:],
                                      "exchange_start_out_0")
        first_out.extend(started)
        return (token,)

    def on_grads(l, dw, dw_bf16, d_wout, d_wout_bf16, small):
        own_parts[l] = (lax.dynamic_slice_in_dim(dw, me * stride, slot, 1)[None],
                        lax.dynamic_slice_in_dim(d_wout, me * rows_out, rows_out, 0)[None])
        out_on_its_way = l == 0
        srcs, lands, kinds = [dw_bf16], [lax.empty((N_DEV - 1, D, slot), BF16)], exchange_kinds[:1]
        if not out_on_its_way:
            srcs.append(d_wout_bf16)
            lands.append(lax.empty((N_DEV - 1, rows_out, D), BF16))
            kinds = list(exchange_kinds)
        if small is not None:
            packed_small = small_pack(l + 1, *small, None)
            srcs.append(packed_small)
            lands.append(landing(packed_small[None], N_DEV))
            kinds.append(gather_in)
        started, token = _split_start(srcs, lands, kinds, f"exchange_start_{l}")
        if out_on_its_way:
            started[1:1], kinds[1:1] = first_out, exchange_kinds[1:]
        exchanges[l] = (started, kinds)
        return (token,)

    loss, dx, small_first, d_final_g = _local_step(
        x[0], loss_target[0], norm_g, forget_bias, pool_w, pool_scale, final_g,
        weights_in, weights_out, on_grads, (rest_token,), on_first_out_grads)
    packed_first = small_pack(0, *small_first, d_final_g, jnp.full((LANES,), loss, F32))
    first_started, first_token = _split_start(
        [packed_first], [landing(packed_first[None], N_DEV)], [gather_in],
        "small_start", (w_in, m_w_in, v_w_in, *own_parts[0]))

    gin_sets, gout_sets, small_sets = [None] * L, [None] * L, [None] * L

    def wait_for(l, after):
        started, kinds = exchanges[l]
        got = _split_wait(started, kinds, after, f"exchange_wait_{l}")
        gin_sets[l] = [own_parts[l][0], got[0]]
        gout_sets[l] = [own_parts[l][1], got[1]]
        if len(got) > 2:
            small_sets[l + 1] = [got[2]]

    for l in range(1, L):
        wait_for(l, (dx, first_token))
    rest_in = _adamw(w_in, m_w_in, v_w_in, gin_sets[1:], "adamw_w_in_rest", TM // 2, shifted=True, first=1)
    rest_out = _adamw(w_out, m_w_out, v_w_out, gout_sets[1:], "adamw_w_out_rest", rows_out, first=1)
    wait_for(0, (rest_in[1], rest_out[1]))
    g_w_in, d_w_in, nm_w_in, nv_w_in = _adamw(w_in, m_w_in, v_w_in, gin_sets[:1], "adamw_w_in_first", TM // 2,
                                              shifted=True, into=rest_in)
    g_w_out, d_w_out, nm_w_out, nv_w_out = _adamw(w_out, m_w_out, v_w_out, gout_sets[:1], "adamw_w_out_first",
                                                  rows_out, into=rest_out)
    small_sets[0] = _split_wait(first_started, [gather_in], (d_w_in, d_w_out), "small_wait")
    loss = jnp.sum(small_sets[0][0][:, packed_first.shape[0] - 8, 0])

    def small_stack(norm_g_, bias_, pool_w_, pool_scale_, final):
        return jnp.stack([small_pack(l, norm_g_[l], bias_[l], pool_w_[l], pool_scale_[l], final) for l in range(L)])

    packed = _adamw(small_stack(norm_g, forget_bias, pool_w, pool_scale, final_g),
                    small_stack(m_norm_g, m_forget_bias, m_pool_w, m_pool_scale, m_final_g),
                    small_stack(v_norm_g, v_forget_bias, v_pool_w, v_pool_scale, v_final_g),
                    small_sets, "adamw_small", packed_first.shape[0])

    def small_unpack(p):
        like = [norm_g[0], forget_bias[0], pool_w[0], pool_scale[0], final_g]
        layers = [_unpack(p[l:l + 1], like) for l in range(L)]
        return [jnp.stack([layers[l][n] for l in range(L)]) for n in range(4)] + [layers[0][4]]

    g_s, d_s, nm_s, nv_s = [small_unpack(p) for p in packed]

    def order(big_in, big_out, small):
        return (small[0], big_in, small[1], small[2], small[3], big_out, small[4])

    return (loss, dx[None], *order(g_w_in, g_w_out, g_s), *order(d_w_in, d_w_out, d_s),
            *order(nm_w_in, nm_w_out, nm_s), *order(nv_w_in, nv_w_out, nv_s))
```

```python
import math

import jax
import jax.numpy as jnp
from jax import lax
from jax.experimental import pallas as pl
from jax.experimental.pallas import tpu as pltpu

F32 = jnp.float32
BF16 = jnp.bfloat16
MESH = pl.DeviceIdType.MESH

RMS_EPS = 1e-6
NEG_INF = -1e30
HEAD_DIM = 64
POOL_WINDOWS = (2, 4, 8, 16)
MAX_WINDOW = 16
LANES = 128
N_DEV = 8

ADAM_LR = 0.001
ADAM_B1 = 0.9
ADAM_B2 = 0.999
ADAM_EPS = 1e-08
ADAM_WD = 0.01
ADAM_STEP = 10

TM = 512
TN = 512
TQ = 512
TB = 256
VMEM_LIMIT = 56 * 1024 * 1024

NT = (((1,), (1,)), ((), ()))
TN_DIMS = (((0,), (0,)), ((), ()))

SLOT_C, SLOT_ONE, SLOT_LSE = 0, 3, 6


def _params(*sem):
    return pltpu.CompilerParams(dimension_semantics=sem, vmem_limit_bytes=VMEM_LIMIT)


def _sigmoid(x):
    return 1.0 / (1.0 + jnp.exp(-x))


def _split3(x):
    hi = x.astype(BF16).astype(F32)
    rest = x - hi
    mid = rest.astype(BF16).astype(F32)
    return hi, mid, rest - mid


def _dot_exact_left(a, x):
    return sum(jnp.dot(a, part.astype(BF16), preferred_element_type=F32) for part in reversed(_split3(x)))


def _after_specs(after):
    return [pl.BlockSpec(memory_space=pl.ANY)] * len(after)


def _slot_width(cols):
    return LANES * (-(-(cols + (N_DEV - 1) * (cols % LANES)) // LANES))


def _shift_meanwhile(w_in, first):
    L, D, cols = w_in.shape
    slot = _slot_width(cols)
    n = L - first
    tr = min(TM // 2, D)

    def work(in_refs, out_refs, scratch):
        (w_ref,), (o_ref,) = in_refs, out_refs
        in_scr, out_scr, pad_scr, sems = scratch
        load = pltpu.make_async_copy(w_ref.at[pl.ds(first, n)], in_scr, sems.at[0])
        load.start()
        pad_scr[...] = jnp.zeros(pad_scr.shape, F32)
        load.wait()
        shift = _index(_position()) * (cols % LANES)
        for l in range(n):
            for r in range(D // tr):
                rows = slice(r * tr, (r + 1) * tr)
                pad_scr[:, 0:cols] = in_scr[l, rows, :]
                out_scr[l, rows, :] = pltpu.roll(pad_scr[...], shift, axis=1).astype(BF16)
        store = pltpu.make_async_copy(out_scr, o_ref, sems.at[1])
        store.start()
        store.wait()

    return ([w_in], [jax.ShapeDtypeStruct((n, D, slot), BF16)],
            [pltpu.VMEM((n, D, cols), F32), pltpu.VMEM((n, D, slot), BF16), pltpu.VMEM((tr, slot), F32),
             pltpu.SemaphoreType.DMA((2,))], work)


def _shift_slots(w_in, L):
    _, D, cols = w_in.shape
    slot = _slot_width(cols)
    tr = min(TM // 2, D)

    def body(w_ref, o_ref, pad_scr):
        pad_scr[...] = jnp.zeros(pad_scr.shape, F32)
        pad_scr[:, 0:cols] = w_ref[0]
        o_ref[0] = pltpu.roll(pad_scr[...], _index(_position()) * (cols % LANES), axis=1).astype(BF16)

    return pl.pallas_call(
        body, name="shift_slots", grid=(L, D // tr),
        in_specs=[pl.BlockSpec((1, tr, cols), lambda l, r: (l, r, 0))],
        out_specs=pl.BlockSpec((1, tr, slot), lambda l, r: (l, r, 0)),
        out_shape=jax.ShapeDtypeStruct((L, D, slot), BF16),
        scratch_shapes=[pltpu.VMEM((tr, slot), F32)],
        compiler_params=_params("parallel", "parallel"),
    )(w_in)


def _inproj_tile(xf, g_ref, s_ref, proj_ref, h_ref, z_ref, w_ref):
    n_dev, _, sw = s_ref.shape
    stride = sw - LANES
    width = w_ref.shape[1]
    N = width - LANES

    @pl.when(pl.program_id(0) == 0)
    def _():
        for n in range(n_dev):
            base = stride * n
            first = s_ref[n, :, 0:LANES]
            if n > 0:
                first = first + s_ref[n - 1, :, stride:sw]
            w_ref[:, base:base + LANES] = first
            w_ref[:, base + LANES:base + stride] = s_ref[n, :, LANES:stride]
        w_ref[:, stride * n_dev:width] = s_ref[n_dev - 1, :, stride:sw]

    r = lax.rsqrt(jnp.mean(xf * xf, axis=-1, keepdims=True) + RMS_EPS)
    h = ((xf * r) * g_ref[...]).astype(BF16)
    h_ref[...] = h
    z_ref[...] = jnp.dot(h, w_ref[:, N:width], preferred_element_type=F32)
    for n in range(N // TN):
        cols = slice(n * TN, (n + 1) * TN)
        proj_ref[:, cols] = jnp.dot(h, w_ref[:, cols], preferred_element_type=F32).astype(BF16)


def _inproj_specs(slots, S, D, tm):
    n_dev, _, sw = slots.shape
    width = (sw - LANES) * n_dev + LANES
    N = width - LANES
    once = pl.Buffered(1)
    in_specs = [pl.BlockSpec((1, D), lambda i: (0, 0)),
                pl.BlockSpec((n_dev, D, sw), lambda i: (0, 0, 0), pipeline_mode=once)]
    out_specs = [pl.BlockSpec((tm, N), lambda i: (i, 0)),
                 pl.BlockSpec((tm, D), lambda i: (i, 0)),
                 pl.BlockSpec((tm, LANES), lambda i: (i, 0)),
                 pl.BlockSpec((D, width), lambda i: (0, 0), pipeline_mode=once)]
    out_shape = [jax.ShapeDtypeStruct((S, N), BF16),
                 jax.ShapeDtypeStruct((S, D), BF16),
                 jax.ShapeDtypeStruct((S, LANES), F32),
                 jax.ShapeDtypeStruct((D, width), BF16)]
    return in_specs, out_specs, out_shape


def _inproj_fwd(x, gam, slots, after=()):
    S, D = x.shape
    tm = min(TM, S)

    def body(x_ref, g_ref, s_ref, *rest):
        _inproj_tile(x_ref[...], g_ref, s_ref, *rest[-4:])

    in_specs, out_specs, out_shape = _inproj_specs(slots, S, D, tm)
    return pl.pallas_call(
        body, name="inproj_fwd", grid=(S // tm,),
        in_specs=[pl.BlockSpec((tm, D), lambda i: (i, 0))] + in_specs + _after_specs(after),
        out_specs=out_specs, out_shape=out_shape,
        compiler_params=_params("arbitrary"),
    )(x, gam, slots, *after)


def _fgate_fwd(z, bias, n_heads):
    S = z.shape[0]
    tb = min(TB, S)
    P = n_heads // 2
    assert n_heads <= 8, "the three parts of c are packed eight lanes apart"

    def body(z_ref, b_ref, qaug_ref, kaug_ref):
        lane = lax.broadcasted_iota(jnp.int32, (tb, LANES), 1)
        tri = (lax.broadcasted_iota(jnp.int32, (tb, tb), 0)
               >= lax.broadcasted_iota(jnp.int32, (tb, tb), 1)).astype(BF16)
        row = lax.broadcasted_iota(jnp.int32, (LANES, P * LANES), 0)
        col = lax.broadcasted_iota(jnp.int32, (LANES, P * LANES), 1)
        head, part_n = row & 7, row >> 3
        home = (head >> 1) * LANES + jnp.where((head & 1) == 0, HEAD_DIM, 0)
        is_part = jnp.logical_and(head < n_heads, part_n < 3)
        place_q = jnp.logical_and(is_part, col == home + SLOT_C + part_n).astype(BF16)
        place_k = jnp.logical_and(is_part, col == home + SLOT_ONE + part_n).astype(BF16)
        slot = lax.broadcasted_iota(jnp.int32, (tb, P * LANES), 1) & (HEAD_DIM - 1)
        q_ones = jnp.logical_and(slot >= SLOT_ONE, slot < SLOT_ONE + 3).astype(F32)
        k_ones = jnp.logical_or(slot < SLOT_C + 3,
                                jnp.logical_and(slot >= SLOT_LSE, slot < SLOT_LSE + 3)).astype(F32)

        local = []
        for b in range(S // tb):
            zz = z_ref[b * tb:(b + 1) * tb, :] + b_ref[...]
            lf = jnp.minimum(zz, 0.0) - jnp.log(1.0 + jnp.exp(-jnp.abs(zz)))
            lf = jnp.where(lane < n_heads, lf, 0.0)
            local.append(_dot_exact_left(tri, lf))
        carry = jnp.zeros((1, LANES), F32)
        for b, part_sum in enumerate(local):
            c = part_sum + carry
            carry = c[tb - 1:tb, :]
            hi, mid, lo = _split3(c)
            packed = (hi + pltpu.roll(mid, 8, axis=1) + pltpu.roll(lo, 16, axis=1)).astype(BF16)
            qaug_ref[b * tb:(b + 1) * tb, :] = (
                q_ones + jnp.dot(packed, place_q, preferred_element_type=F32)).astype(BF16)
            kaug_ref[b * tb:(b + 1) * tb, :] = (
                k_ones - jnp.dot(packed, place_k, preferred_element_type=F32)).astype(BF16)

    return pl.pallas_call(
        body, name="fgate_fwd",
        out_shape=[jax.ShapeDtypeStruct((S, P * LANES), BF16),
                   jax.ShapeDtypeStruct((S, P * LANES), BF16)],
        compiler_params=pltpu.CompilerParams(vmem_limit_bytes=VMEM_LIMIT),
    )(z, bias)


def _inner_window(w):
    inner = 1 << ((w.bit_length() - 1) // 2)
    assert w % inner == 0
    return inner


def _window_mean_minus_self(u, pad_ref, w, S):
    pad_ref[0:MAX_WINDOW, :] = jnp.zeros((MAX_WINDOW, LANES), F32)
    pad_ref[MAX_WINDOW:MAX_WINDOW + S, :] = u
    inner = _inner_window(w)
    acc = u
    for j in range(1, inner):
        acc = acc + pad_ref[MAX_WINDOW - j:MAX_WINDOW - j + S, :]
    if inner > 1:
        pad_ref[MAX_WINDOW:MAX_WINDOW + S, :] = acc
    for j in range(inner, w, inner):
        acc = acc + pad_ref[MAX_WINDOW - j:MAX_WINDOW - j + S, :]
    t = lax.broadcasted_iota(jnp.int32, (S, LANES), 0)
    cnt = jnp.minimum(t + 1, w).astype(F32)
    return acc / cnt - u, cnt


def _pool_fwd(proj, pool_w, pool_scale):
    S = proj.shape[0]
    G = len(POOL_WINDOWS)

    def body(u_ref, w_ref, s_ref, y_ref, pad_ref):
        g = pl.program_id(0)
        for gi, w in enumerate(POOL_WINDOWS):
            @pl.when(g == gi)
            def _():
                d, _ = _window_mean_minus_self(u_ref[...].astype(F32), pad_ref, w, S)
                y = jnp.dot(d.astype(BF16), w_ref[0].astype(BF16), preferred_element_type=F32)
                y_ref[...] = (y * s_ref[...]).astype(BF16)

    return pl.pallas_call(
        body, name="pool_fwd", grid=(G,),
        in_specs=[pl.BlockSpec((S, LANES), lambda g: (0, g)),
                  pl.BlockSpec((1, LANES, LANES), lambda g: (g, 0, 0)),
                  pl.BlockSpec((1, LANES), lambda g: (0, g))],
        out_specs=pl.BlockSpec((S, LANES), lambda g: (0, g)),
        out_shape=jax.ShapeDtypeStruct((S, G * LANES), BF16),
        scratch_shapes=[pltpu.VMEM((S + MAX_WINDOW, LANES), F32)],
        compiler_params=_params("arbitrary"),
    )(proj, pool_w, pool_scale)


def _head_halves(rows):
    lane = lax.broadcasted_iota(jnp.int32, (rows, LANES), 1)
    return lane, (lane < HEAD_DIM, lane >= HEAD_DIM)


def _attn_fwd(proj, qaug, kaug):
    S = proj.shape[0]
    W = proj.shape[1] // 6
    P = W // LANES
    tk = min(TQ, S // 2)
    tq = 2 * tk
    nq = S // tq
    qc, kc, vc = 2 * P, 3 * P, 4 * P
    scale = 1.0 / math.sqrt(HEAD_DIM)

    def body(q_ref, k_ref, v_ref, qa_ref, ka_ref, o_ref, qb_ref, qm_scr, m_scr, acc_scr):
        i = pl.program_id(1)
        lane, halves = _head_halves(tq)
        key_halves = (halves[0][:tk], halves[1][:tk])
        v_ones = ((lane[:tk] & (HEAD_DIM - 1)) < 3).astype(BF16)
        qs = q_ref[...] * scale
        qm_scr[0] = jnp.where(halves[0], qs, qa_ref[...])
        qm_scr[1] = jnp.where(halves[1], qs, qa_ref[...])
        m_scr[...] = jnp.full(m_scr.shape, NEG_INF, F32)
        acc_scr[...] = jnp.zeros(acc_scr.shape, F32)
        top, bottom, both = slice(0, tk), slice(tk, tq), slice(0, tq)

        def update(rows, j, on_diagonal):
            keys = pl.ds(pl.multiple_of(j * tk, tk), tk)
            k2, v2, kaug_t = k_ref[keys, :], v_ref[keys, :], ka_ref[keys, :]
            if on_diagonal:
                keep = (lax.broadcasted_iota(jnp.int32, (tk, tk), 0)
                        >= lax.broadcasted_iota(jnp.int32, (tk, tk), 1))
            logits = [lax.dot_general(qm_scr[a, rows, :], jnp.where(key_halves[a], k2, kaug_t), NT,
                                      preferred_element_type=F32) for a in range(2)]
            for a in range(2):
                s = jnp.where(keep, logits[a], NEG_INF) if on_diagonal else logits[a]
                va = jnp.where(key_halves[a], v2, v_ones)
                m_prev = m_scr[a, rows, :]
                m_new = jnp.maximum(m_prev, jnp.max(s, axis=1, keepdims=True))
                p = jnp.exp(s - jnp.tile(m_new, (1, tk // LANES)))
                acc_scr[a, rows, :] = (jnp.exp(m_prev - m_new) * acc_scr[a, rows, :]
                                       + jnp.dot(p.astype(BF16), va, preferred_element_type=F32))
                m_scr[a, rows, :] = m_new

        def below_diagonal(jj, carry):
            update(both, 2 * jj, False)
            update(both, 2 * jj + 1, False)
            return carry

        lax.fori_loop(0, i, below_diagonal, 0)
        update(top, 2 * i, True)
        update(bottom, 2 * i, False)
        update(bottom, 2 * i + 1, True)
        acc_a, acc_b = acc_scr[0], acc_scr[1]
        l_a, l_b = acc_a[:, HEAD_DIM:HEAD_DIM + 1], acc_b[:, 0:1]
        o_ref[...] = jnp.where(halves[0], acc_a / l_a, acc_b / l_b).astype(BF16)
        lse = jnp.where(halves[0], m_scr[1] + jnp.log(l_b), m_scr[0] + jnp.log(l_a))
        slot = lane & (HEAD_DIM - 1)
        aug = qa_ref[...].astype(F32)
        for n, part in enumerate(_split3(lse)):
            aug = jnp.where(slot == SLOT_LSE + n, -part, aug)
        qb_ref[...] = aug.astype(BF16)

    tile = lambda col: pl.BlockSpec((tq, LANES), lambda p, i: (i, col + p))
    whole = lambda col: pl.BlockSpec((S, LANES), lambda p, i: (0, col + p))
    return pl.pallas_call(
        body, name="attn_fwd", grid=(P, nq),
        in_specs=[tile(qc), whole(kc), whole(vc), tile(0), whole(0)],
        out_specs=[tile(0), tile(0)],
        out_shape=[jax.ShapeDtypeStruct((S, W), BF16), jax.ShapeDtypeStruct((S, W), BF16)],
        scratch_shapes=[pltpu.VMEM((2, tq, LANES), BF16),
                        pltpu.VMEM((2, tq, LANES), F32),
                        pltpu.VMEM((2, tq, LANES), F32)],
        compiler_params=_params("parallel", "arbitrary"),
    )(proj, proj, proj, qaug, kaug)


def _outproj_fwd(ypool, o, proj, x, wout, nxt=None, head=None):
    S, D = x.shape
    W = D // 2
    tm, tn = min(TM, S), TN

    def body(y_ref, o_ref, pg_ref, ag_ref, x_ref, w_ref, *rest):
        mix_ref = rest[-1]
        out_ref = rest[2]
        pg, ag = pg_ref[...].astype(F32), ag_ref[...].astype(F32)
        mix_ref[:, 0:W] = (y_ref[...].astype(F32) * (pg * _sigmoid(pg))).astype(BF16)
        mix_ref[:, W:D] = (o_ref[...].astype(F32) * (ag * _sigmoid(ag))).astype(BF16)
        for n in range(D // tn):
            cols = slice(n * tn, (n + 1) * tn)
            out_ref[:, cols] = x_ref[:, cols] + jnp.dot(mix_ref[...], w_ref[:, cols], preferred_element_type=F32)
        if nxt:
            _inproj_tile(out_ref[...], rest[0], rest[1], *rest[3:7])
            return
        gam_ref, t_ref = rest[:2]
        loss_ref, dg_ref = rest[-3:-1]

        @pl.when(pl.program_id(0) == 0)
        def _():
            loss_ref[...] = jnp.zeros(loss_ref.shape, F32)
            dg_ref[...] = jnp.zeros(dg_ref.shape, F32)

        xf, gam_v = out_ref[...], gam_ref[...]
        r = lax.rsqrt(jnp.mean(xf * xf, axis=-1, keepdims=True) + RMS_EPS)
        xhat = xf * r
        err = xhat * gam_v - t_ref[...]
        part = jnp.sum(jnp.sum(err * err, axis=-1, keepdims=True), axis=0, keepdims=True)
        loss_ref[...] += part * (0.5 / D)
        dy = err * (1.0 / D)
        dg_ref[...] += jnp.sum(dy * xhat, axis=0, keepdims=True)
        dxhat = dy * gam_v
        out_ref[...] = r * (dxhat - xhat * jnp.mean(dxhat * xhat, axis=-1, keepdims=True))

    assert (nxt is None) != (head is None)
    rows = lambda width, col: pl.BlockSpec((tm, width), lambda i: (i, col))
    in_specs = [rows(W, 0), rows(W, 0), rows(W, 1), rows(W, 5), rows(D, 0),
                pl.BlockSpec((D, D), lambda i: (0, 0), pipeline_mode=pl.Buffered(1))]
    out_specs = [rows(D, 0)]
    out_shape = [jax.ShapeDtypeStruct((S, D), F32)]
    if nxt:
        more_in, more_out, more_shape = _inproj_specs(nxt[1], S, D, tm)
        in_specs += more_in
        out_specs += more_out
        out_shape += more_shape
    else:
        in_specs += [pl.BlockSpec((1, D), lambda i: (0, 0)), rows(D, 0)]
        out_specs += [pl.BlockSpec((8, LANES), lambda i: (0, 0)), pl.BlockSpec((1, D), lambda i: (0, 0))]
        out_shape += [jax.ShapeDtypeStruct((8, LANES), F32), jax.ShapeDtypeStruct((1, D), F32)]
    return pl.pallas_call(
        body, name="outproj_fwd_loss" if head else "outproj_inproj_fwd", grid=(S // tm,),
        in_specs=in_specs, out_specs=out_specs, out_shape=out_shape,
        scratch_shapes=[pltpu.VMEM((tm, D), BF16)],
        compiler_params=_params("arbitrary"),
    )(ypool, o, proj, proj, x, wout, *(nxt or head))


def _outproj_bwd(g, wout, ypool, o, proj, after=()):
    S, D = g.shape
    W = D // 2
    tm = min(TM, S)

    def body(g_ref, w_ref, y_ref, o_ref, pg_ref, ag_ref, *rest):
        dw_ref, dwb_ref, da_ref, dgate_ref, doaug_ref = rest[-5:]

        @pl.when(pl.program_id(0) == 0)
        def _():
            dw_ref[...] = jnp.zeros(dw_ref.shape, F32)

        gb = g_ref[...].astype(BF16)
        dmixes = [lax.dot_general(gb, w_ref[half * W:(half + 1) * W, :], NT, preferred_element_type=F32)
                  for half in range(2)]
        d_o = None
        for half, (val_ref, gate_ref) in enumerate(((y_ref, pg_ref), (o_ref, ag_ref))):
            cols = slice(half * W, (half + 1) * W)
            gt = gate_ref[...].astype(F32)
            sg = _sigmoid(gt)
            silu = gt * sg
            val = val_ref[...].astype(F32)
            dw_ref[cols, :] += lax.dot_general((val * silu).astype(BF16), gb, TN_DIMS, preferred_element_type=F32)
            d_o = (dmixes[half] * silu).astype(BF16)
            da_ref[:, cols] = d_o
            dgate_ref[:, cols] = (dmixes[half] * val * (sg * (1.0 + gt * (1.0 - sg)))).astype(BF16)

        lane, halves = _head_halves(tm)
        slot = lane & (HEAD_DIM - 1)
        for p in range(W // LANES):
            cols = slice(p * LANES, (p + 1) * LANES)
            prod = d_o[:, cols].astype(F32) * o_ref[:, cols].astype(F32)
            d_a = jnp.sum(jnp.where(halves[0], prod, 0.0), axis=1, keepdims=True)
            d_b = jnp.sum(jnp.where(halves[1], prod, 0.0), axis=1, keepdims=True)
            aug = jnp.zeros((tm, LANES), F32)
            for n, part in enumerate(_split3(jnp.where(halves[0], d_b, d_a))):
                aug = jnp.where(slot == SLOT_C + n, -part, aug)
            doaug_ref[:, cols] = aug.astype(BF16)

        @pl.when(pl.program_id(0) == S // tm - 1)
        def _():
            dwb_ref[...] = dw_ref[...].astype(BF16)

    rows = lambda width, col: pl.BlockSpec((tm, width), lambda i: (i, col))
    whole = pl.BlockSpec((D, D), lambda i: (0, 0))
    return pl.pallas_call(
        body, name="outproj_bwd", grid=(S // tm,),
        in_specs=[rows(D, 0), whole, rows(W, 0), rows(W, 0), rows(W, 1), rows(W, 5)] + _after_specs(after),
        out_specs=[whole, whole, rows(D, 0), rows(D, 0), rows(W, 0)],
        out_shape=[jax.ShapeDtypeStruct((D, D), F32),
                   jax.ShapeDtypeStruct((D, D), BF16),
                   jax.ShapeDtypeStruct((S, D), BF16),
                   jax.ShapeDtypeStruct((S, D), BF16),
                   jax.ShapeDtypeStruct((S, W), BF16)],
        compiler_params=_params("arbitrary"),
    )(g, wout, ypool, o, proj, proj, *after)


def _section_specs(sections, rows, width):
    specs = [pl.BlockSpec((rows, width), lambda k, c=c: (k, c)) for _, c in sections]
    return specs, [a for a, _ in sections]


def _inproj_bwd(sections, dzf, h=None, wxg=None, after=()):
    S = dzf.shape[0]
    n_sec = len(sections)
    D = h.shape[1] if h is not None else wxg[1].shape[1]
    W = D // 2
    N = n_sec * W
    ts = min(TM if wxg else 2 * TM, S)
    n_steps = S // ts
    once = pl.Buffered(1)

    def body(dz_ref, *rest):
        sec_refs, rest = rest[:n_sec], rest[n_sec:]
        if h is not None:
            h_ref, rest = rest[0], rest[1:]
        if wxg:
            (w_ref, wf_ref, x_ref, gam_ref, g_ref), rest = rest[:5], rest[5:]
        outs = rest[len(after):]
        step = pl.program_id(0)

        if h is not None:
            dw_ref, dwb_ref = outs[:2]

            @pl.when(step == 0)
            def _():
                dw_ref[...] = jnp.zeros(dw_ref.shape, F32)

            ht = h_ref[...].T
            dw_ref[:, N:N + LANES] += jnp.dot(ht, dz_ref[...], preferred_element_type=F32)
            for n, ref in enumerate(sec_refs):
                dw_ref[:, n * W:(n + 1) * W] += jnp.dot(ht, ref[...], preferred_element_type=F32)

            @pl.when(step == n_steps - 1)
            def _():
                dwb_ref[...] = dw_ref[...].astype(BF16)

        if wxg:
            dx_ref, dg_ref = outs[-2:]

            @pl.when(step == 0)
            def _():
                dg_ref[...] = jnp.zeros(dg_ref.shape, F32)

            dh = lax.dot_general(dz_ref[...], wf_ref[...], NT, preferred_element_type=F32)
            for n, ref in enumerate(sec_refs):
                dh = dh + lax.dot_general(ref[...], w_ref[:, n * W:(n + 1) * W], NT, preferred_element_type=F32)
            xf = x_ref[...]
            r = lax.rsqrt(jnp.mean(xf * xf, axis=-1, keepdims=True) + RMS_EPS)
            xhat = xf * r
            dg_ref[...] += jnp.sum(dh * xhat, axis=0, keepdims=True)
            dxhat = dh * gam_ref[...]
            dx_ref[...] = g_ref[...] + r * (dxhat - xhat * jnp.mean(dxhat * xhat, axis=-1, keepdims=True))

    rows = lambda width: pl.BlockSpec((ts, width), lambda k: (k, 0))
    sec_specs, operands = _section_specs(sections, ts, W)
    in_specs, out_specs, out_shape = [rows(LANES)] + sec_specs, [], []
    if h is not None:
        whole = pl.BlockSpec((D, N + LANES), lambda k: (0, 0), pipeline_mode=once)
        in_specs.append(rows(D))
        operands.append(h)
        out_specs += [whole, whole]
        out_shape += [jax.ShapeDtypeStruct((D, N + LANES), F32), jax.ShapeDtypeStruct((D, N + LANES), BF16)]
    if wxg:
        w, x, gam, g = wxg
        in_specs += [pl.BlockSpec((D, N), lambda k: (0, 0), pipeline_mode=once),
                     pl.BlockSpec((D, LANES), lambda k: (0, N // LANES), pipeline_mode=once),
                     rows(D), pl.BlockSpec((1, D), lambda k: (0, 0)), rows(D)]
        operands += [w, w, x, gam, g]
        out_specs += [rows(D), pl.BlockSpec((1, D), lambda k: (0, 0))]
        out_shape += [jax.ShapeDtypeStruct((S, D), F32), jax.ShapeDtypeStruct((1, D), F32)]
    name = "inproj_bwd" if h is not None and wxg else ("inproj_bwd_dw" if wxg is None else "inproj_bwd_dx")
    return pl.pallas_call(
        body, name=name, grid=(n_steps,),
        in_specs=in_specs + _after_specs(after), out_specs=out_specs, out_shape=out_shape,
        compiler_params=_params("arbitrary"),
    )(dzf, *operands, *after)


def _attn_bwd(proj, da, qaug, kaug, doaug, after=()):
    S = proj.shape[0]
    W = proj.shape[1] // 6
    P = W // LANES
    tq = min(TQ, S)
    nq = S // tq
    qc, kc, vc = 2 * P, 3 * P, 4 * P
    scale = 1.0 / math.sqrt(HEAD_DIM)

    def body(q_ref, k_ref, v_ref, do_ref, qa_ref, ka_ref, da_ref, *rest):
        dq_ref, dk_ref, dv_ref, drows_ref, dcols_ref, km_scr, vm_scr, dk_scr, dv_scr, dq_scr = rest[len(after):]
        pair, j = pl.program_id(0), pl.program_id(1)
        lane, halves = _head_halves(tq)

        @pl.when(jnp.logical_and(pair == 0, j == 0))
        def _():
            drows_ref[...] = jnp.zeros(drows_ref.shape, F32)
            dcols_ref[...] = jnp.zeros(dcols_ref.shape, F32)

        @pl.when(j == 0)
        def _():
            dq_scr[...] = jnp.zeros(dq_scr.shape, F32)

        v_ones = ((lane & (HEAD_DIM - 1)) < 3).astype(BF16)
        for a in range(2):
            km_scr[a] = jnp.where(halves[a], k_ref[...], ka_ref[...])
            vm_scr[a] = jnp.where(halves[a], v_ref[...], v_ones)
        dk_scr[...] = jnp.zeros(dk_scr.shape, F32)
        dv_scr[...] = jnp.zeros(dv_scr.shape, F32)

        def update(i, on_diagonal):
            rows = pl.ds(pl.multiple_of(i * tq, tq), tq)
            qs = q_ref[rows, :] * scale
            do2, qaug_t, doaug_t = do_ref[rows, :], qa_ref[rows, :], da_ref[rows, :]
            if on_diagonal:
                keep = (lax.broadcasted_iota(jnp.int32, (tq, tq), 0)
                        >= lax.broadcasted_iota(jnp.int32, (tq, tq), 1))
            qas = [jnp.where(halves[a], qs, qaug_t) for a in range(2)]
            logits = [lax.dot_general(qas[a], km_scr[a], NT, preferred_element_type=F32) for a in range(2)]
            dps = [lax.dot_general(jnp.where(halves[a], do2, doaug_t), vm_scr[a], NT, preferred_element_type=F32)
                   for a in range(2)]
            dv = None
            for a in range(2):
                s = jnp.where(keep, logits[a], NEG_INF) if on_diagonal else logits[a]
                p = jnp.exp(s)
                dsb = (p * dps[a]).astype(BF16)
                do0 = jnp.where(halves[a], do2, jnp.zeros_like(do2))
                dv_a = lax.dot_general(p.astype(BF16), do0, TN_DIMS, preferred_element_type=F32)
                dv = dv_a if dv is None else dv + dv_a
                dk_scr[a] += lax.dot_general(dsb, qas[a], TN_DIMS, preferred_element_type=F32)
                dq_scr[a, rows, :] += jnp.dot(dsb, km_scr[a], preferred_element_type=F32)
            dv_scr[...] += dv

        def below_diagonal(n, carry):
            update(j + 1 + 2 * n, False)
            update(j + 2 + 2 * n, False)
            return carry

        update(j, True)
        below = nq - 1 - j
        lax.fori_loop(0, below // 2, below_diagonal, 0)

        @pl.when(below % 2 == 1)
        def _():
            update(nq - 1, False)


        def to_head_lanes(old, first, second):
            at = lax.broadcasted_iota(jnp.int32, old.shape, 1) - 2 * pair
            return jnp.where(at == 0, first, jnp.where(at == 1, second, old))

        dk_ref[...] = jnp.where(halves[0], dk_scr[0], dk_scr[1]).astype(BF16)
        dv_ref[...] = dv_scr[...].astype(BF16)
        keys = pl.ds(pl.multiple_of(j * tq, tq), tq)
        ones_a, ones_b = HEAD_DIM + SLOT_ONE, SLOT_ONE
        dcols_ref[keys, :] = to_head_lanes(dcols_ref[keys, :], dk_scr[0][:, ones_a:ones_a + 1],
                                           dk_scr[1][:, ones_b:ones_b + 1])

        @pl.when(j == nq - 1)
        def _():
            row_lane, row_halves = _head_halves(S)
            dq_ref[...] = (jnp.where(row_halves[0], dq_scr[0], dq_scr[1]) * scale).astype(BF16)
            c_a, c_b = HEAD_DIM + SLOT_C, SLOT_C
            drows_ref[...] = to_head_lanes(drows_ref[...], dq_scr[0][:, c_a:c_a + 1], dq_scr[1][:, c_b:c_b + 1])

    tile = lambda col: pl.BlockSpec((tq, LANES), lambda p, j: (j, col + p))
    whole = lambda col: pl.BlockSpec((S, LANES), lambda p, j: (0, col + p))
    shared = pl.BlockSpec((S, LANES), lambda p, j: (0, 0))
    return pl.pallas_call(
        body, name="attn_bwd", grid=(P, nq),
        in_specs=[whole(qc), tile(kc), tile(vc), whole(P), whole(0), tile(0), whole(0)] + _after_specs(after),
        out_specs=[whole(0), tile(0), tile(0), shared, shared],
        out_shape=[jax.ShapeDtypeStruct((S, W), BF16),
                   jax.ShapeDtypeStruct((S, W), BF16),
                   jax.ShapeDtypeStruct((S, W), BF16),
                   jax.ShapeDtypeStruct((S, LANES), F32),
                   jax.ShapeDtypeStruct((S, LANES), F32)],
        scratch_shapes=[pltpu.VMEM((2, tq, LANES), BF16),
                        pltpu.VMEM((2, tq, LANES), BF16),
                        pltpu.VMEM((2, tq, LANES), F32),
                        pltpu.VMEM((tq, LANES), F32),
                        pltpu.VMEM((2, S, LANES), F32)],
        compiler_params=_params("arbitrary", "arbitrary"),
    )(proj, proj, proj, da, qaug, kaug, doaug, *after)


def _fgate_bwd(drows, dcols, z, bias):
    S = z.shape[0]
    tb = min(TB, S)
    nb = S // tb

    def body(drows_ref, dcols_ref, z_ref, b_ref, dz_ref, db_ref):
        tri = (lax.broadcasted_iota(jnp.int32, (tb, tb), 1)
               >= lax.broadcasted_iota(jnp.int32, (tb, tb), 0)).astype(BF16)

        local = []
        for b in range(nb):
            rows = slice(b * tb, (b + 1) * tb)
            local.append(_dot_exact_left(tri, drows_ref[rows, :] - dcols_ref[rows, :]))
        carry = jnp.zeros((1, LANES), F32)
        db = jnp.zeros((1, LANES), F32)
        for b in reversed(range(nb)):
            rows = slice(b * tb, (b + 1) * tb)
            rc = local[b] + carry
            carry = rc[0:1, :]
            dz = rc * _sigmoid(-(z_ref[rows, :] + b_ref[...]))
            dz_ref[rows, :] = dz.astype(BF16)
            db = db + jnp.sum(dz, axis=0, keepdims=True)
        db_ref[...] = db

    return pl.pallas_call(
        body, name="fgate_bwd",
        out_shape=[jax.ShapeDtypeStruct((S, LANES), BF16),
                   jax.ShapeDtypeStruct((1, LANES), F32)],
        compiler_params=pltpu.CompilerParams(vmem_limit_bytes=VMEM_LIMIT),
    )(drows, dcols, z, bias)


def _pool_bwd(proj, da, pool_w, pool_scale):
    S = proj.shape[0]
    G = len(POOL_WINDOWS)

    def body(u_ref, dy_ref, w_ref, s_ref, du_ref, dw_ref, ds_ref, pad_ref):
        g = pl.program_id(0)
        for gi, w in enumerate(POOL_WINDOWS):
            @pl.when(g == gi)
            def _():
                d, cnt = _window_mean_minus_self(u_ref[...].astype(F32), pad_ref, w, S)
                db = d.astype(BF16)
                wb = w_ref[0].astype(BF16)
                yraw = jnp.dot(db, wb, preferred_element_type=F32)
                dy = dy_ref[...].astype(F32)
                ds_ref[...] = jnp.sum(dy * yraw, axis=0, keepdims=True)
                dzb = (dy * s_ref[...]).astype(BF16)
                dw_ref[0] = lax.dot_general(db, dzb, TN_DIMS, preferred_element_type=F32)
                dd = lax.dot_general(dzb, wb, NT, preferred_element_type=F32)
                acc = dd / cnt
                pad_ref[0:S, :] = acc
                pad_ref[S:S + MAX_WINDOW, :] = jnp.zeros((MAX_WINDOW, LANES), F32)
                inner = _inner_window(w)
                for j in range(1, inner):
                    acc = acc + pad_ref[j:j + S, :]
                if inner > 1:
                    pad_ref[0:S, :] = acc
                for j in range(inner, w, inner):
                    acc = acc + pad_ref[j:j + S, :]
                du_ref[...] = (acc - dd).astype(BF16)

    return pl.pallas_call(
        body, name="pool_bwd", grid=(G,),
        in_specs=[pl.BlockSpec((S, LANES), lambda g: (0, g)),
                  pl.BlockSpec((S, LANES), lambda g: (0, g)),
                  pl.BlockSpec((1, LANES, LANES), lambda g: (g, 0, 0)),
                  pl.BlockSpec((1, LANES), lambda g: (0, g))],
        out_specs=[pl.BlockSpec((S, LANES), lambda g: (0, g)),
                   pl.BlockSpec((1, LANES, LANES), lambda g: (g, 0, 0)),
                   pl.BlockSpec((1, LANES), lambda g: (0, g))],
        out_shape=[jax.ShapeDtypeStruct((S, G * LANES), BF16),
                   jax.ShapeDtypeStruct((G, LANES, LANES), F32),
                   jax.ShapeDtypeStruct((1, G * LANES), F32)],
        scratch_shapes=[pltpu.VMEM((S + MAX_WINDOW, LANES), F32)],
        compiler_params=_params("arbitrary"),
    )(proj, da, pool_w, pool_scale)


def _adamw(w, m, v, gsets, name, rows, shifted=False, first=0, into=None):
    A, R, C = w.shape
    n_sets = len(gsets)
    tr = min(rows, R)
    c1 = 1.0 / (1.0 - ADAM_B1 ** ADAM_STEP)
    c2 = 1.0 / (1.0 - ADAM_B2 ** ADAM_STEP)
    counts = [len(gs) for gs in gsets]

    def body(w_ref, m_ref, v_ref, *rest):
        g_ref, d_ref, nm_ref, nv_ref = rest[-4:]
        at = 0
        for a in range(n_sets):
            part_refs = rest[at:at + counts[a]]
            at += counts[a]

            @pl.when(pl.program_id(0) == a)
            def _():
                g = None
                for ref in part_refs:
                    for s in range(ref.shape[0]):
                        term = ref[s].astype(F32)
                        g = term if g is None else g + term
                if shifted:
                    lanes = g.shape[1]
                    g = pltpu.roll(g, (lanes - _index(_position()) * (C % LANES)) % lanes, axis=1)[:, :C]
                nm = ADAM_B1 * m_ref[0] + (1.0 - ADAM_B1) * g
                nv = ADAM_B2 * v_ref[0] + (1.0 - ADAM_B2) * (g * g)
                g_ref[0] = g
                nm_ref[0] = nm
                nv_ref[0] = nv
                d_ref[0] = -ADAM_LR * ((nm * c1) / (jnp.sqrt(nv * c2) + ADAM_EPS) + ADAM_WD * w_ref[0])

    spec = pl.BlockSpec((1, tr, C), lambda a, r: (first + a, r, 0))
    part_specs = [pl.BlockSpec((part.shape[0], tr, part.shape[2]), lambda a, r, l=l: (0, jnp.where(a == l, r, 0), 0))
                  for l, gs in enumerate(gsets) for part in gs]
    parts = [part for gs in gsets for part in gs]
    shape = jax.ShapeDtypeStruct((A, R, C), F32)
    earlier = () if into is None else tuple(into)
    return pl.pallas_call(
        body, name=name, grid=(n_sets, R // tr),
        in_specs=[spec, spec, spec] + part_specs + _after_specs(earlier),
        out_specs=[spec, spec, spec, spec],
        out_shape=[shape, shape, shape, shape],
        input_output_aliases={3 + len(parts) + n: n for n in range(len(earlier))},
        compiler_params=_params("arbitrary", "arbitrary"),
    )(w, m, v, *parts, *earlier)


def _position():
    return lax.axis_index("x"), lax.axis_index("y"), lax.axis_index("c")


def _index(dev):
    return 4 * dev[0] + 2 * dev[1] + dev[2]


def _all_gather(arrs, slots, out_shapes, name, meanwhile):
    n_arr = len(arrs)
    side_in, side_shapes, side_scratch, side_work = meanwhile
    n_side = len(side_in)

    def body(*refs):
        ins, outs = refs[:n_arr], refs[n_arr + n_side:2 * n_arr + n_side]
        at = 2 * n_arr + n_side + len(side_shapes)
        send_sems, recv_sems, local_sems = refs[at:at + 3]
        x, y, c = _position()
        me, sibling = (x, y, c), (x, y, 1 - c)
        chips = [(1 - x, y), (x, 1 - y), (1 - x, 1 - y)]

        def copy(a, k, block, to, src=None):
            part = slots[a](outs[a], _index(block))
            return pltpu.make_async_remote_copy(
                src_ref=part if src is None else src, dst_ref=part,
                send_sem=send_sems.at[a, k], recv_sem=recv_sems.at[a, k],
                device_id=to, device_id_type=MESH)

        mine = [pltpu.make_async_copy(ins[a], slots[a](outs[a], _index(me)), local_sems.at[a])
                for a in range(n_arr)]
        for cp in mine:
            cp.start()
        first = []
        for a in range(n_arr):
            first.append(copy(a, 0, me, sibling, src=ins[a]))
            first += [copy(a, 1 + j, me, (*chip, c), src=ins[a]) for j, chip in enumerate(chips)]
        for cp in first:
            cp.start()
        side_work(refs[n_arr:n_arr + n_side], refs[2 * n_arr + n_side:at], refs[at + 3:])
        passed = []
        for j, chip in enumerate(chips):
            for a in range(n_arr):
                copy(a, 1 + j, (*chip, c), me).wait_recv()
                fwd = copy(a, 4 + j, (*chip, c), sibling)
                fwd.start()
                passed.append(fwd)
        for a in range(n_arr):
            copy(a, 0, sibling, me).wait_recv()
            for j, chip in enumerate(chips):
                copy(a, 4 + j, (*chip, 1 - c), me).wait_recv()
        for cp in first + passed:
            cp.wait_send()
        for cp in mine:
            cp.wait()

    any_spec = pl.BlockSpec(memory_space=pl.ANY)
    return pl.pallas_call(
        body, name=name,
        in_specs=[any_spec] * (n_arr + n_side), out_specs=[any_spec] * (n_arr + len(side_shapes)),
        out_shape=[*out_shapes, *side_shapes],
        scratch_shapes=[pltpu.SemaphoreType.DMA((n_arr, 7)), pltpu.SemaphoreType.DMA((n_arr, 7)),
                        pltpu.SemaphoreType.DMA((n_arr,)), *side_scratch],
        compiler_params=pltpu.CompilerParams(vmem_limit_bytes=VMEM_LIMIT),
    )(*arrs, *side_in)


def _split_copies(srcs, lands, send_sems, recv_sems, kinds):
    x, y, c = _position()
    me = _index((x, y, c))
    copies = []
    for a, (src_part, land_part) in enumerate(kinds):
        for k in range(1, N_DEV):
            peer = (x ^ ((k >> 2) & 1), y ^ ((k >> 1) & 1), c ^ (k & 1))
            copies.append(pltpu.make_async_remote_copy(
                src_ref=src_part(srcs[a], _index(peer)), dst_ref=land_part(lands[a], me, k),
                send_sem=send_sems[a].at[k - 1], recv_sem=recv_sems[a].at[k - 1],
                device_id=peer, device_id_type=MESH))
    return copies


def _split_start(srcs, lands, kinds, name, after=()):
    n = len(srcs)

    def body(*refs):
        src_refs, land_refs = refs[:n], refs[n:2 * n]
        outs = refs[2 * n + len(after):]
        send_sems, recv_sems = outs[:n], outs[n:2 * n]
        token = outs[-1]
        for cp in _split_copies(src_refs, land_refs, send_sems, recv_sems, kinds):
            cp.start()
        token[...] = jnp.zeros(token.shape, token.dtype)

    hbm = pl.BlockSpec(memory_space=pltpu.HBM)
    sem = pl.BlockSpec(memory_space=pltpu.SEMAPHORE)
    operands = [pltpu.with_memory_space_constraint(t, pltpu.HBM) for t in (*srcs, *lands)]
    out = pl.pallas_call(
        body, name=name,
        in_specs=[hbm] * (2 * n) + _after_specs(after),
        out_specs=[sem] * (2 * n) + [hbm] * (2 * n) + [pl.BlockSpec(memory_space=pltpu.VMEM)],
        out_shape=[pltpu.SemaphoreType.DMA((N_DEV - 1,))] * (2 * n)
        + [pltpu.HBM(t.shape, t.dtype) for t in operands] + [jax.ShapeDtypeStruct((8, LANES), F32)],
        input_output_aliases={i: 2 * n + i for i in range(2 * n)},
        compiler_params=pltpu.CompilerParams(has_side_effects=pltpu.SideEffectType.DATAFLOW_SIDE_EFFECTING),
    )(*operands, *after)
    return [(out[a], out[n + a], out[2 * n + a], out[3 * n + a]) for a in range(n)], out[-1]


def _split_wait(started, kinds, after, name):
    n = len(started)
    sems = [t[0] for t in started] + [t[1] for t in started]
    srcs = [t[2] for t in started]
    lands = [t[3] for t in started]

    def body(*refs):
        src_refs, land_refs = refs[:n], refs[n:2 * n]
        send_sems, recv_sems = refs[2 * n:3 * n], refs[3 * n:4 * n]
        for cp in _split_copies(src_refs, land_refs, send_sems, recv_sems, kinds):
            cp.wait_send()
            cp.wait_recv()

    hbm = pl.BlockSpec(memory_space=pltpu.HBM)
    sem = pl.BlockSpec(memory_space=pltpu.SEMAPHORE)
    out = pl.pallas_call(
        body, name=name,
        in_specs=[hbm] * (2 * n) + [sem] * (2 * n) + _after_specs(after),
        out_specs=[hbm] * (2 * n),
        out_shape=[pltpu.HBM(t.shape, t.dtype) for t in (*srcs, *lands)],
        input_output_aliases={i: i for i in range(2 * n)},
        compiler_params=pltpu.CompilerParams(has_side_effects=pltpu.SideEffectType.DATAFLOW_SIDE_EFFECTING),
    )(*srcs, *lands, *sems, *after)
    return out[n:]


def _as_rows(p):
    if p.size % LANES == 0:
        rows = p.reshape(-1, LANES)
    else:
        rows = p.reshape(-1, p.shape[-1])
        rows = jnp.pad(rows, ((0, 0), (0, LANES - rows.shape[1])))
    return jnp.pad(rows, ((0, -rows.shape[0] % 8), (0, 0)))


def _pack(parts):
    return jnp.concatenate([_as_rows(p) for p in parts])[None]


def _unpack(packed, like):
    out, at = [], 0
    for p in like:
        whole = p.size % LANES == 0
        n = p.size // LANES if whole else p.size // p.shape[-1]
        rows = packed[0, at:at + n]
        out.append((rows if whole else rows[:, :p.shape[-1]]).reshape(p.shape))
        at += n + (-n % 8)
    return out


def _local_step(x, target, norm_g, forget_bias, pool_w, pool_scale, final_g, weights_in, weights_out, on_grads,
                first_after=(), on_first_out_grads=lambda d_wout, d_wout_bf16: ()):
    L = norm_g.shape[0]
    S, D = x.shape
    W = D // 2
    H = W // HEAD_DIM
    bias = jnp.pad(forget_bias, ((0, 0), (0, LANES - H)))

    saved = []
    nxt = _inproj_fwd(x, norm_g[0:1], weights_in(0, x), tuple(first_after))
    for l in range(L):
        x_in, (proj, h, z, w) = x, nxt
        qaug, kaug = _fgate_fwd(z, bias[l:l + 1], H)
        ypool = _pool_fwd(proj, pool_w[l], pool_scale[l:l + 1])
        o, qaug_b = _attn_fwd(proj, qaug, kaug)
        wout = weights_out(l, o)
        if l < L - 1:
            x, *nxt = _outproj_fwd(ypool, o, proj, x_in, wout, nxt=(norm_g[l + 1:l + 2], weights_in(l + 1, o)))
        else:
            g, loss, d_final_g = _outproj_fwd(ypool, o, proj, x_in, wout, head=(final_g.reshape(1, D), target))
        saved.append((x_in, proj, h, z, qaug_b, kaug, ypool, o, w, wout))

    small, after = None, ()
    for l in reversed(range(L)):
        x_in, proj, h, z, qaug_b, kaug, ypool, o, w, wout = saved[l]
        d_wout, d_wout_bf16, da, dgate, doaug = _outproj_bwd(g, wout, ypool, o, proj, after)
        early = tuple(on_first_out_grads(d_wout, d_wout_bf16)) if l == 0 else ()
        dq, dk, dv, drows, dcols = _attn_bwd(proj, da, qaug_b, kaug, doaug, early)
        dzf, db = _fgate_bwd(drows, dcols, z, bias[l:l + 1])
        dpu, dpw, dps = _pool_bwd(proj, da, pool_w[l], pool_scale[l:l + 1])
        dproj = [(dpu, 0), (dgate, 0), (dq, 0), (dk, 0), (dv, 0), (dgate, 1)]
        wxg = (w, x_in, norm_g[l:l + 1], g)
        if l > 0:
            d_w, d_w_bf16, g, dgam = _inproj_bwd(dproj, dzf, h, wxg)
            after = tuple(on_grads(l, d_w, d_w_bf16, d_wout, d_wout_bf16, small))
        else:
            d_w, d_w_bf16 = _inproj_bwd(dproj, dzf, h)
            after = tuple(on_grads(l, d_w, d_w_bf16, d_wout, d_wout_bf16, small))
            g, dgam = _inproj_bwd(dproj, dzf, None, wxg, after)
        small = (dgam[0], db[0, :H], dpw, dps[0])
    return loss[0, 0], g, small, d_final_g[0]


def kernel(x, norm_g, w_in, forget_bias, pool_w, pool_scale, w_out, final_g, loss_target, m_norm_g, m_w_in, m_forget_bias, m_pool_w, m_pool_scale, m_w_out, m_final_g, v_norm_g, v_w_in, v_forget_bias, v_pool_w, v_pool_scale, v_w_out, v_final_g):
    L, D, cols = w_in.shape
    rows_out = w_out.shape[1]
    me = _index(_position())
    slot = _slot_width(cols)
    wout_b = w_out.astype(BF16)
    win_first = _shift_slots(w_in, 1)
    gather_in = (lambda ref, peer: ref, lambda ref, mine, k: ref.at[mine])
    gather_out = (lambda ref, peer: ref, lambda ref, mine, k: ref.at[pl.ds(mine * rows_out, rows_out), :])

    def landing(block, n_slots):
        zone = lax.empty((n_slots * block.shape[0], *block.shape[1:]), block.dtype)
        return lax.dynamic_update_slice(zone, block, (me * block.shape[0],) + (0,) * (block.ndim - 1))

    first_in, win_rest = _all_gather([win_first[0]], [lambda ref, n: ref.at[n]],
                                     [jax.ShapeDtypeStruct((N_DEV, D, slot), BF16)], "gather_first",
                                     _shift_meanwhile(w_in, 1))
    win_b = [win_first[0]] + [win_rest[l - 1] for l in range(1, L)]
    rest_srcs = [wout_b[0]] + [w[l] for l in range(1, L) for w in (win_b, wout_b)]
    rest_lands = [landing(wout_b[0], N_DEV)]
    for l in range(1, L):
        rest_lands += [landing(win_b[l][None], N_DEV), landing(wout_b[l], N_DEV)]
    rest_kinds = [gather_out] + [gather_in, gather_out] * (L - 1)
    rest, rest_token = _split_start(rest_srcs, rest_lands, rest_kinds, "gather_start_rest", (first_in,))

    def weights_in(l, x_in):
        if l == 0:
            return first_in
        (win_all,) = _split_wait([rest[2 * l - 1]], [gather_in], (x_in,), f"gather_wait_in_{l}")
        return win_all

    def weights_out(l, o):
        (wout_full,) = _split_wait([rest[2 * l]], [gather_out], (o,), f"gather_wait_out_{l}")
        return wout_full

    stride = slot - LANES
    exchange_kinds = [(lambda ref, peer: ref.at[:, pl.ds(pl.multiple_of(peer * stride, LANES), slot)],
                       lambda ref, mine, k: ref.at[k - 1]),
                      (lambda ref, peer: ref.at[pl.ds(peer * rows_out, rows_out), :],
                       lambda ref, mine, k: ref.at[k - 1])]
    zero_g = jnp.zeros_like(final_g)
    zero_loss = jnp.zeros((LANES,), F32)

    def small_pack(l, norm_g_l, bias_l, pool_w_l, pool_scale_l, final, loss_row=None):
        return _pack([norm_g_l, bias_l, pool_w_l, pool_scale_l, final if l == 0 else zero_g,
                      zero_loss if loss_row is None else loss_row])[0]

    exchanges, own_parts, first_out = {}, {}, []

    def on_first_out_grads(d_wout, d_wout_bf16):
        started, token = _split_start([d_wout_bf16], [lax.empty((N_DEV - 1, rows_out, D), BF16)], exchange_kinds[1:],
                                      "exchange_start_out_0")
        first_out.extend(started)
        return (token,)

    def on_grads(l, dw, dw_bf16, d_wout, d_wout_bf16, small):
        own_parts[l] = (lax.dynamic_slice_in_dim(dw, me * stride, slot, 1)[None],
                        lax.dynamic_slice_in_dim(d_wout, me * rows_out, rows_out, 0)[None])
        out_on_its_way = l == 0
        srcs, lands, kinds = [dw_bf16], [lax.empty((N_DEV - 1, D, slot), BF16)], exchange_kinds[:1]
        if not out_on_its_way:
            srcs.append(d_wout_bf16)
            lands.append(lax.empty((N_DEV - 1, rows_out, D), BF16))
            kinds = list(exchange_kinds)
        if small is not None:
            packed_small = small_pack(l + 1, *small, None)
            srcs.append(packed_small)
            lands.append(landing(packed_small[None], N_DEV))
            kinds.append(gather_in)
        started, token = _split_start(srcs, lands, kinds, f"exchange_start_{l}")
        if out_on_its_way:
            started[1:1], kinds[1:1] = first_out, exchange_kinds[1:]
        exchanges[l] = (started, kinds)
        return (token,)

    loss, dx, small_first, d_final_g = _local_step(
        x[0], loss_target[0], norm_g, forget_bias, pool_w, pool_scale, final_g,
        weights_in, weights_out, on_grads, (rest_token,), on_first_out_grads)
    packed_first = small_pack(0, *small_first, d_final_g, jnp.full((LANES,), loss, F32))
    first_started, first_token = _split_start(
        [packed_first], [landing(packed_first[None], N_DEV)], [gather_in],
        "small_start", (w_in, m_w_in, v_w_in, *own_parts[0]))

    gin_sets, gout_sets, small_sets = [None] * L, [None] * L, [None] * L

    def wait_for(l, after):
        started, kinds = exchanges[l]
        got = _split_wait(started, kinds, after, f"exchange_wait_{l}")
        gin_sets[l] = [own_parts[l][0], got[0]]
        gout_sets[l] = [own_parts[l][1], got[1]]
        if len(got) > 2:
            small_sets[l + 1] = [got[2]]

    for l in range(1, L):
        wait_for(l, (dx, first_token))
    rest_in = _adamw(w_in, m_w_in, v_w_in, gin_sets[1:], "adamw_w_in_rest", TM // 2, shifted=True, first=1)
    rest_out = _adamw(w_out, m_w_out, v_w_out, gout_sets[1:], "adamw_w_out_rest", rows_out, first=1)
    wait_for(0, (rest_in[1], rest_out[1]))
    g_w_in, d_w_in, nm_w_in, nv_w_in = _adamw(w_in, m_w_in, v_w_in, gin_sets[:1], "adamw_w_in_first", TM // 2,
                                              shifted=True, into=rest_in)
    g_w_out, d_w_out, nm_w_out, nv_w_out = _adamw(w_out, m_w_out, v_w_out, gout_sets[:1], "adamw_w_out_first",
                                                  rows_out, into=rest_out)
    small_sets[0] = _split_wait(first_started, [gather_in], (d_w_in, d_w_out), "small_wait")
    loss = jnp.sum(small_sets[0][0][:, packed_first.shape[0] - 8, 0])

    def small_stack(norm_g_, bias_, pool_w_, pool_scale_, final):
        return jnp.stack([small_pack(l, norm_g_[l], bias_[l], pool_w_[l], pool_scale_[l], final) for l in range(L)])

    packed = _adamw(small_stack(norm_g, forget_bias, pool_w, pool_scale, final_g),
                    small_stack(m_norm_g, m_forget_bias, m_pool_w, m_pool_scale, m_final_g),
                    small_stack(v_norm_g, v_forget_bias, v_pool_w, v_pool_scale, v_final_g),
                    small_sets, "adamw_small", packed_first.shape[0])

    def small_unpack(p):
        like = [norm_g[0], forget_bias[0], pool_w[0], pool_scale[0], final_g]
        layers = [_unpack(p[l:l + 1], like) for l in range(L)]
        return [jnp.stack([layers[l][n] for l in range(L)]) for n in range(4)] + [layers[0][4]]

    g_s, d_s, nm_s, nv_s = [small_unpack(p) for p in packed]

    def order(big_in, big_out, small):
        return (small[0], big_in, small[1], small[2], small[3], big_out, small[4])

    return (loss, dx[None], *order(g_w_in, g_w_out, g_s), *order(d_w_in, d_w_out, d_s),
            *order(nm_w_in, nm_w_out, nm_s), *order(nv_w_in, nv_w_out, nv_s))
```

```python
import math

import jax
import jax.numpy as jnp
from jax import lax
from jax.experimental import pallas as pl
from jax.experimental.pallas import tpu as pltpu

F32 = jnp.float32
BF16 = jnp.bfloat16
MESH = pl.DeviceIdType.MESH

RMS_EPS = 1e-6
NEG_INF = -1e30
HEAD_DIM = 64
POOL_WINDOWS = (2, 4, 8, 16)
MAX_WINDOW = 16
LANES = 128
N_DEV = 8

ADAM_LR = 0.001
ADAM_B1 = 0.9
ADAM_B2 = 0.999
ADAM_EPS = 1e-08
ADAM_WD = 0.01
ADAM_STEP = 10

TM = 512
TN = 512
TQ = 512
TB = 256
VMEM_LIMIT = 56 * 1024 * 1024

NT = (((1,), (1,)), ((), ()))
TN_DIMS = (((0,), (0,)), ((), ()))

SLOT_C, SLOT_ONE, SLOT_LSE = 0, 3, 6


def _params(*sem):
    return pltpu.CompilerParams(dimension_semantics=sem, vmem_limit_bytes=VMEM_LIMIT)


def _sigmoid(x):
    return 1.0 / (1.0 + jnp.exp(-x))


def _split3(x):
    hi = x.astype(BF16).astype(F32)
    rest = x - hi
    mid = rest.astype(BF16).astype(F32)
    return hi, mid, rest - mid


def _dot_exact_left(a, x):
    return sum(jnp.dot(a, part.astype(BF16), preferred_element_type=F32) for part in reversed(_split3(x)))


def _after_specs(after):
    return [pl.BlockSpec(memory_space=pl.ANY)] * len(after)


def _slot_width(cols):
    return LANES * (-(-(cols + (N_DEV - 1) * (cols % LANES)) // LANES))


def _shift_meanwhile(w_in, first):
    L, D, cols = w_in.shape
    slot = _slot_width(cols)
    n = L - first
    tr = min(TM // 2, D)

    def work(in_refs, out_refs, scratch):
        (w_ref,), (o_ref, *land_refs) = in_refs, out_refs
        in_scr, out_scr, pad_scr, sems = scratch
        load = pltpu.make_async_copy(w_ref.at[pl.ds(first, n)], in_scr, sems.at[0])
        load.start()
        pad_scr[...] = jnp.zeros(pad_scr.shape, F32)
        load.wait()
        shift = _index(_position()) * (cols % LANES)
        for l in range(n):
            for r in range(D // tr):
                rows = slice(r * tr, (r + 1) * tr)
                pad_scr[:, 0:cols] = in_scr[l, rows, :]
                out_scr[l, rows, :] = pltpu.roll(pad_scr[...], shift, axis=1).astype(BF16)
        me = _index(_position())
        stores = [pltpu.make_async_copy(out_scr, o_ref, sems.at[1])]
        stores += [pltpu.make_async_copy(out_scr.at[l], land_refs[l].at[me], sems.at[2 + l]) for l in range(n)]
        for store in stores:
            store.start()
        for store in stores:
            store.wait()

    return ([w_in], [jax.ShapeDtypeStruct((n, D, slot), BF16)] + [jax.ShapeDtypeStruct((N_DEV, D, slot), BF16)] * n,
            [pltpu.VMEM((n, D, cols), F32), pltpu.VMEM((n, D, slot), BF16), pltpu.VMEM((tr, slot), F32),
             pltpu.SemaphoreType.DMA((2 + n,))], work)


def _shift_slots(w_in, L):
    _, D, cols = w_in.shape
    slot = _slot_width(cols)
    tr = min(TM // 2, D)

    def body(w_ref, o_ref, pad_scr):
        pad_scr[...] = jnp.zeros(pad_scr.shape, F32)
        pad_scr[:, 0:cols] = w_ref[0]
        o_ref[0] = pltpu.roll(pad_scr[...], _index(_position()) * (cols % LANES), axis=1).astype(BF16)

    return pl.pallas_call(
        body, name="shift_slots", grid=(L, D // tr),
        in_specs=[pl.BlockSpec((1, tr, cols), lambda l, r: (l, r, 0))],
        out_specs=pl.BlockSpec((1, tr, slot), lambda l, r: (l, r, 0)),
        out_shape=jax.ShapeDtypeStruct((L, D, slot), BF16),
        scratch_shapes=[pltpu.VMEM((tr, slot), F32)],
        compiler_params=_params("parallel", "parallel"),
    )(w_in)


def _inproj_tile(xf, g_ref, s_ref, proj_ref, h_ref, z_ref, w_ref):
    n_dev, _, sw = s_ref.shape
    stride = sw - LANES
    width = w_ref.shape[1]
    N = width - LANES

    @pl.when(pl.program_id(0) == 0)
    def _():
        for n in range(n_dev):
            base = stride * n
            first = s_ref[n, :, 0:LANES]
            if n > 0:
                first = first + s_ref[n - 1, :, stride:sw]
            w_ref[:, base:base + LANES] = first
            w_ref[:, base + LANES:base + stride] = s_ref[n, :, LANES:stride]
        w_ref[:, stride * n_dev:width] = s_ref[n_dev - 1, :, stride:sw]

    r = lax.rsqrt(jnp.mean(xf * xf, axis=-1, keepdims=True) + RMS_EPS)
    h = ((xf * r) * g_ref[...]).astype(BF16)
    h_ref[...] = h
    z_ref[...] = jnp.dot(h, w_ref[:, N:width], preferred_element_type=F32)
    for n in range(N // TN):
        cols = slice(n * TN, (n + 1) * TN)
        proj_ref[:, cols] = jnp.dot(h, w_ref[:, cols], preferred_element_type=F32).astype(BF16)


def _inproj_specs(slots, S, D, tm):
    n_dev, _, sw = slots.shape
    width = (sw - LANES) * n_dev + LANES
    N = width - LANES
    once = pl.Buffered(1)
    in_specs = [pl.BlockSpec((1, D), lambda i: (0, 0)),
                pl.BlockSpec((n_dev, D, sw), lambda i: (0, 0, 0), pipeline_mode=once)]
    out_specs = [pl.BlockSpec((tm, N), lambda i: (i, 0)),
                 pl.BlockSpec((tm, D), lambda i: (i, 0)),
                 pl.BlockSpec((tm, LANES), lambda i: (i, 0)),
                 pl.BlockSpec((D, width), lambda i: (0, 0), pipeline_mode=once)]
    out_shape = [jax.ShapeDtypeStruct((S, N), BF16),
                 jax.ShapeDtypeStruct((S, D), BF16),
                 jax.ShapeDtypeStruct((S, LANES), F32),
                 jax.ShapeDtypeStruct((D, width), BF16)]
    return in_specs, out_specs, out_shape


def _inproj_fwd(x, gam, slots, after=()):
    S, D = x.shape
    tm = min(TM, S)

    def body(x_ref, g_ref, s_ref, *rest):
        _inproj_tile(x_ref[...], g_ref, s_ref, *rest[-4:])

    in_specs, out_specs, out_shape = _inproj_specs(slots, S, D, tm)
    return pl.pallas_call(
        body, name="inproj_fwd", grid=(S // tm,),
        in_specs=[pl.BlockSpec((tm, D), lambda i: (i, 0))] + in_specs + _after_specs(after),
        out_specs=out_specs, out_shape=out_shape,
        compiler_params=_params("arbitrary"),
    )(x, gam, slots, *after)


def _fgate_fwd(z, bias, n_heads):
    S = z.shape[0]
    tb = min(TB, S)
    P = n_heads // 2
    assert n_heads <= 8, "the three parts of c are packed eight lanes apart"

    def body(z_ref, b_ref, qaug_ref, kaug_ref):
        lane = lax.broadcasted_iota(jnp.int32, (tb, LANES), 1)
        tri = (lax.broadcasted_iota(jnp.int32, (tb, tb), 0)
               >= lax.broadcasted_iota(jnp.int32, (tb, tb), 1)).astype(BF16)
        row = lax.broadcasted_iota(jnp.int32, (LANES, P * LANES), 0)
        col = lax.broadcasted_iota(jnp.int32, (LANES, P * LANES), 1)
        head, part_n = row & 7, row >> 3
        home = (head >> 1) * LANES + jnp.where((head & 1) == 0, HEAD_DIM, 0)
        is_part = jnp.logical_and(head < n_heads, part_n < 3)
        place_q = jnp.logical_and(is_part, col == home + SLOT_C + part_n).astype(BF16)
        place_k = jnp.logical_and(is_part, col == home + SLOT_ONE + part_n).astype(BF16)
        slot = lax.broadcasted_iota(jnp.int32, (tb, P * LANES), 1) & (HEAD_DIM - 1)
        q_ones = jnp.logical_and(slot >= SLOT_ONE, slot < SLOT_ONE + 3).astype(F32)
        k_ones = jnp.logical_or(slot < SLOT_C + 3,
                                jnp.logical_and(slot >= SLOT_LSE, slot < SLOT_LSE + 3)).astype(F32)

        local = []
        for b in range(S // tb):
            zz = z_ref[b * tb:(b + 1) * tb, :] + b_ref[...]
            lf = jnp.minimum(zz, 0.0) - jnp.log(1.0 + jnp.exp(-jnp.abs(zz)))
            lf = jnp.where(lane < n_heads, lf, 0.0)
            local.append(_dot_exact_left(tri, lf))
        carry = jnp.zeros((1, LANES), F32)
        for b, part_sum in enumerate(local):
            c = part_sum + carry
            carry = c[tb - 1:tb, :]
            hi, mid, lo = _split3(c)
            packed = (hi + pltpu.roll(mid, 8, axis=1) + pltpu.roll(lo, 16, axis=1)).astype(BF16)
            qaug_ref[b * tb:(b + 1) * tb, :] = (
                q_ones + jnp.dot(packed, place_q, preferred_element_type=F32)).astype(BF16)
            kaug_ref[b * tb:(b + 1) * tb, :] = (
                k_ones - jnp.dot(packed, place_k, preferred_element_type=F32)).astype(BF16)

    return pl.pallas_call(
        body, name="fgate_fwd",
        out_shape=[jax.ShapeDtypeStruct((S, P * LANES), BF16),
                   jax.ShapeDtypeStruct((S, P * LANES), BF16)],
        compiler_params=pltpu.CompilerParams(vmem_limit_bytes=VMEM_LIMIT),
    )(z, bias)


def _inner_window(w):
    inner = 1 << ((w.bit_length() - 1) // 2)
    assert w % inner == 0
    return inner


def _window_mean_minus_self(u, pad_ref, w, S):
    pad_ref[0:MAX_WINDOW, :] = jnp.zeros((MAX_WINDOW, LANES), F32)
    pad_ref[MAX_WINDOW:MAX_WINDOW + S, :] = u
    inner = _inner_window(w)
    acc = u
    for j in range(1, inner):
        acc = acc + pad_ref[MAX_WINDOW - j:MAX_WINDOW - j + S, :]
    if inner > 1:
        pad_ref[MAX_WINDOW:MAX_WINDOW + S, :] = acc
    for j in range(inner, w, inner):
        acc = acc + pad_ref[MAX_WINDOW - j:MAX_WINDOW - j + S, :]
    t = lax.broadcasted_iota(jnp.int32, (S, LANES), 0)
    cnt = jnp.minimum(t + 1, w).astype(F32)
    return acc / cnt - u, cnt


def _pool_fwd(proj, pool_w, pool_scale):
    S = proj.shape[0]
    G = len(POOL_WINDOWS)

    def body(u_ref, w_ref, s_ref, y_ref, pad_ref):
        g = pl.program_id(0)
        for gi, w in enumerate(POOL_WINDOWS):
            @pl.when(g == gi)
            def _():
                d, _ = _window_mean_minus_self(u_ref[...].astype(F32), pad_ref, w, S)
                y = jnp.dot(d.astype(BF16), w_ref[0].astype(BF16), preferred_element_type=F32)
                y_ref[...] = (y * s_ref[...]).astype(BF16)

    return pl.pallas_call(
        body, name="pool_fwd", grid=(G,),
        in_specs=[pl.BlockSpec((S, LANES), lambda g: (0, g)),
                  pl.BlockSpec((1, LANES, LANES), lambda g: (g, 0, 0)),
                  pl.BlockSpec((1, LANES), lambda g: (0, g))],
        out_specs=pl.BlockSpec((S, LANES), lambda g: (0, g)),
        out_shape=jax.ShapeDtypeStruct((S, G * LANES), BF16),
        scratch_shapes=[pltpu.VMEM((S + MAX_WINDOW, LANES), F32)],
        compiler_params=_params("arbitrary"),
    )(proj, pool_w, pool_scale)


def _head_halves(rows):
    lane = lax.broadcasted_iota(jnp.int32, (rows, LANES), 1)
    return lane, (lane < HEAD_DIM, lane >= HEAD_DIM)


def _attn_fwd(proj, qaug, kaug):
    S = proj.shape[0]
    W = proj.shape[1] // 6
    P = W // LANES
    tk = min(TQ, S // 2)
    tq = 2 * tk
    nq = S // tq
    qc, kc, vc = 2 * P, 3 * P, 4 * P
    scale = 1.0 / math.sqrt(HEAD_DIM)

    def body(q_ref, k_ref, v_ref, qa_ref, ka_ref, o_ref, qb_ref, qm_scr, m_scr, acc_scr):
        i = pl.program_id(1)
        lane, halves = _head_halves(tq)
        key_halves = (halves[0][:tk], halves[1][:tk])
        v_ones = ((lane[:tk] & (HEAD_DIM - 1)) < 3).astype(BF16)
        qs = q_ref[...] * scale
        qm_scr[0] = jnp.where(halves[0], qs, qa_ref[...])
        qm_scr[1] = jnp.where(halves[1], qs, qa_ref[...])
        m_scr[...] = jnp.full(m_scr.shape, NEG_INF, F32)
        acc_scr[...] = jnp.zeros(acc_scr.shape, F32)
        top, bottom, both = slice(0, tk), slice(tk, tq), slice(0, tq)

        def update(rows, j, on_diagonal):
            keys = pl.ds(pl.multiple_of(j * tk, tk), tk)
            k2, v2, kaug_t = k_ref[keys, :], v_ref[keys, :], ka_ref[keys, :]
            if on_diagonal:
                keep = (lax.broadcasted_iota(jnp.int32, (tk, tk), 0)
                        >= lax.broadcasted_iota(jnp.int32, (tk, tk), 1))
            logits = [lax.dot_general(qm_scr[a, rows, :], jnp.where(key_halves[a], k2, kaug_t), NT,
                                      preferred_element_type=F32) for a in range(2)]
            for a in range(2):
                s = jnp.where(keep, logits[a], NEG_INF) if on_diagonal else logits[a]
                va = jnp.where(key_halves[a], v2, v_ones)
                m_prev = m_scr[a, rows, :]
                m_new = jnp.maximum(m_prev, jnp.max(s, axis=1, keepdims=True))
                p = jnp.exp(s - jnp.tile(m_new, (1, tk // LANES)))
                acc_scr[a, rows, :] = (jnp.exp(m_prev - m_new) * acc_scr[a, rows, :]
                                       + jnp.dot(p.astype(BF16), va, preferred_element_type=F32))
                m_scr[a, rows, :] = m_new

        def below_diagonal(jj, carry):
            update(both, 2 * jj, False)
            update(both, 2 * jj + 1, False)
            return carry

        lax.fori_loop(0, i, below_diagonal, 0)
        update(top, 2 * i, True)
        update(bottom, 2 * i, False)
        update(bottom, 2 * i + 1, True)
        acc_a, acc_b = acc_scr[0], acc_scr[1]
        l_a, l_b = acc_a[:, HEAD_DIM:HEAD_DIM + 1], acc_b[:, 0:1]
        o_ref[...] = jnp.where(halves[0], acc_a / l_a, acc_b / l_b).astype(BF16)
        lse = jnp.where(halves[0], m_scr[1] + jnp.log(l_b), m_scr[0] + jnp.log(l_a))
        slot = lane & (HEAD_DIM - 1)
        aug = qa_ref[...].astype(F32)
        for n, part in enumerate(_split3(lse)):
            aug = jnp.where(slot == SLOT_LSE + n, -part, aug)
        qb_ref[...] = aug.astype(BF16)

    tile = lambda col: pl.BlockSpec((tq, LANES), lambda p, i: (i, col + p))
    whole = lambda col: pl.BlockSpec((S, LANES), lambda p, i: (0, col + p))
    return pl.pallas_call(
        body, name="attn_fwd", grid=(P, nq),
        in_specs=[tile(qc), whole(kc), whole(vc), tile(0), whole(0)],
        out_specs=[tile(0), tile(0)],
        out_shape=[jax.ShapeDtypeStruct((S, W), BF16), jax.ShapeDtypeStruct((S, W), BF16)],
        scratch_shapes=[pltpu.VMEM((2, tq, LANES), BF16),
                        pltpu.VMEM((2, tq, LANES), F32),
                        pltpu.VMEM((2, tq, LANES), F32)],
        compiler_params=_params("parallel", "arbitrary"),
    )(proj, proj, proj, qaug, kaug)


def _outproj_fwd(ypool, o, proj, x, wout, nxt=None, head=None):
    S, D = x.shape
    W = D // 2
    tm, tn = min(TM, S), TN

    def body(y_ref, o_ref, pg_ref, ag_ref, x_ref, w_ref, *rest):
        mix_ref = rest[-1]
        out_ref = rest[2]
        pg, ag = pg_ref[...].astype(F32), ag_ref[...].astype(F32)
        mix_ref[:, 0:W] = (y_ref[...].astype(F32) * (pg * _sigmoid(pg))).astype(BF16)
        mix_ref[:, W:D] = (o_ref[...].astype(F32) * (ag * _sigmoid(ag))).astype(BF16)
        for n in range(D // tn):
            cols = slice(n * tn, (n + 1) * tn)
            out_ref[:, cols] = x_ref[:, cols] + jnp.dot(mix_ref[...], w_ref[:, cols], preferred_element_type=F32)
        if nxt:
            _inproj_tile(out_ref[...], rest[0], rest[1], *rest[3:7])
            return
        gam_ref, t_ref = rest[:2]
        loss_ref, dg_ref = rest[-3:-1]

        @pl.when(pl.program_id(0) == 0)
        def _():
            loss_ref[...] = jnp.zeros(loss_ref.shape, F32)
            dg_ref[...] = jnp.zeros(dg_ref.shape, F32)

        xf, gam_v = out_ref[...], gam_ref[...]
        r = lax.rsqrt(jnp.mean(xf * xf, axis=-1, keepdims=True) + RMS_EPS)
        xhat = xf * r
        err = xhat * gam_v - t_ref[...]
        part = jnp.sum(jnp.sum(err * err, axis=-1, keepdims=True), axis=0, keepdims=True)
        loss_ref[...] += part * (0.5 / D)
        dy = err * (1.0 / D)
        dg_ref[...] += jnp.sum(dy * xhat, axis=0, keepdims=True)
        dxhat = dy * gam_v
        out_ref[...] = r * (dxhat - xhat * jnp.mean(dxhat * xhat, axis=-1, keepdims=True))

    assert (nxt is None) != (head is None)
    rows = lambda width, col: pl.BlockSpec((tm, width), lambda i: (i, col))
    in_specs = [rows(W, 0), rows(W, 0), rows(W, 1), rows(W, 5), rows(D, 0),
                pl.BlockSpec((D, D), lambda i: (0, 0), pipeline_mode=pl.Buffered(1))]
    out_specs = [rows(D, 0)]
    out_shape = [jax.ShapeDtypeStruct((S, D), F32)]
    if nxt:
        more_in, more_out, more_shape = _inproj_specs(nxt[1], S, D, tm)
        in_specs += more_in
        out_specs += more_out
        out_shape += more_shape
    else:
        in_specs += [pl.BlockSpec((1, D), lambda i: (0, 0)), rows(D, 0)]
        out_specs += [pl.BlockSpec((8, LANES), lambda i: (0, 0)), pl.BlockSpec((1, D), lambda i: (0, 0))]
        out_shape += [jax.ShapeDtypeStruct((8, LANES), F32), jax.ShapeDtypeStruct((1, D), F32)]
    return pl.pallas_call(
        body, name="outproj_fwd_loss" if head else "outproj_inproj_fwd", grid=(S // tm,),
        in_specs=in_specs, out_specs=out_specs, out_shape=out_shape,
        scratch_shapes=[pltpu.VMEM((tm, D), BF16)],
        compiler_params=_params("arbitrary"),
    )(ypool, o, proj, proj, x, wout, *(nxt or head))


def _outproj_bwd(g, wout, ypool, o, proj, after=()):
    S, D = g.shape
    W = D // 2
    tm = min(TM, S)

    def body(g_ref, w_ref, y_ref, o_ref, pg_ref, ag_ref, *rest):
        dw_ref, dwb_ref, da_ref, dgate_ref, doaug_ref = rest[-5:]

        @pl.when(pl.program_id(0) == 0)
        def _():
            dw_ref[...] = jnp.zeros(dw_ref.shape, F32)

        gb = g_ref[...].astype(BF16)
        dmixes = [lax.dot_general(gb, w_ref[half * W:(half + 1) * W, :], NT, preferred_element_type=F32)
                  for half in range(2)]
        d_o = None
        for half, (val_ref, gate_ref) in enumerate(((y_ref, pg_ref), (o_ref, ag_ref))):
            cols = slice(half * W, (half + 1) * W)
            gt = gate_ref[...].astype(F32)
            sg = _sigmoid(gt)
            silu = gt * sg
            val = val_ref[...].astype(F32)
            dw_ref[cols, :] += lax.dot_general((val * silu).astype(BF16), gb, TN_DIMS, preferred_element_type=F32)
            d_o = (dmixes[half] * silu).astype(BF16)
            da_ref[:, cols] = d_o
            dgate_ref[:, cols] = (dmixes[half] * val * (sg * (1.0 + gt * (1.0 - sg)))).astype(BF16)

        lane, halves = _head_halves(tm)
        slot = lane & (HEAD_DIM - 1)
        for p in range(W // LANES):
            cols = slice(p * LANES, (p + 1) * LANES)
            prod = d_o[:, cols].astype(F32) * o_ref[:, cols].astype(F32)
            d_a = jnp.sum(jnp.where(halves[0], prod, 0.0), axis=1, keepdims=True)
            d_b = jnp.sum(jnp.where(halves[1], prod, 0.0), axis=1, keepdims=True)
            aug = jnp.zeros((tm, LANES), F32)
            for n, part in enumerate(_split3(jnp.where(halves[0], d_b, d_a))):
                aug = jnp.where(slot == SLOT_C + n, -part, aug)
            doaug_ref[:, cols] = aug.astype(BF16)

        @pl.when(pl.program_id(0) == S // tm - 1)
        def _():
            dwb_ref[...] = dw_ref[...].astype(BF16)

    rows = lambda width, col: pl.BlockSpec((tm, width), lambda i: (i, col))
    whole = pl.BlockSpec((D, D), lambda i: (0, 0))
    return pl.pallas_call(
        body, name="outproj_bwd", grid=(S // tm,),
        in_specs=[rows(D, 0), whole, rows(W, 0), rows(W, 0), rows(W, 1), rows(W, 5)] + _after_specs(after),
        out_specs=[whole, whole, rows(D, 0), rows(D, 0), rows(W, 0)],
        out_shape=[jax.ShapeDtypeStruct((D, D), F32),
                   jax.ShapeDtypeStruct((D, D), BF16),
                   jax.ShapeDtypeStruct((S, D), BF16),
                   jax.ShapeDtypeStruct((S, D), BF16),
                   jax.ShapeDtypeStruct((S, W), BF16)],
        compiler_params=_params("arbitrary"),
    )(g, wout, ypool, o, proj, proj, *after)


def _section_specs(sections, rows, width):
    specs = [pl.BlockSpec((rows, width), lambda k, c=c: (k, c)) for _, c in sections]
    return specs, [a for a, _ in sections]


def _inproj_bwd(sections, dzf, h=None, wxg=None, after=()):
    S = dzf.shape[0]
    n_sec = len(sections)
    D = h.shape[1] if h is not None else wxg[1].shape[1]
    W = D // 2
    N = n_sec * W
    ts = min(TM if wxg else 2 * TM, S)
    n_steps = S // ts
    once = pl.Buffered(1)

    def body(dz_ref, *rest):
        sec_refs, rest = rest[:n_sec], rest[n_sec:]
        if h is not None:
            h_ref, rest = rest[0], rest[1:]
        if wxg:
            (w_ref, wf_ref, x_ref, gam_ref, g_ref), rest = rest[:5], rest[5:]
        outs = rest[len(after):]
        step = pl.program_id(0)

        if h is not None:
            dw_ref, dwb_ref = outs[:2]

            @pl.when(step == 0)
            def _():
                dw_ref[...] = jnp.zeros(dw_ref.shape, F32)

            ht = h_ref[...].T
            dw_ref[:, N:N + LANES] += jnp.dot(ht, dz_ref[...], preferred_element_type=F32)
            for n, ref in enumerate(sec_refs):
                dw_ref[:, n * W:(n + 1) * W] += jnp.dot(ht, ref[...], preferred_element_type=F32)

            @pl.when(step == n_steps - 1)
            def _():
                dwb_ref[...] = dw_ref[...].astype(BF16)

        if wxg:
            dx_ref, dg_ref = outs[-2:]

            @pl.when(step == 0)
            def _():
                dg_ref[...] = jnp.zeros(dg_ref.shape, F32)

            dh = lax.dot_general(dz_ref[...], wf_ref[...], NT, preferred_element_type=F32)
            for n, ref in enumerate(sec_refs):
                dh = dh + lax.dot_general(ref[...], w_ref[:, n * W:(n + 1) * W], NT, preferred_element_type=F32)
            xf = x_ref[...]
            r = lax.rsqrt(jnp.mean(xf * xf, axis=-1, keepdims=True) + RMS_EPS)
            xhat = xf * r
            dg_ref[...] += jnp.sum(dh * xhat, axis=0, keepdims=True)
            dxhat = dh * gam_ref[...]
            dx_ref[...] = g_ref[...] + r * (dxhat - xhat * jnp.mean(dxhat * xhat, axis=-1, keepdims=True))

    rows = lambda width: pl.BlockSpec((ts, width), lambda k: (k, 0))
    sec_specs, operands = _section_specs(sections, ts, W)
    in_specs, out_specs, out_shape = [rows(LANES)] + sec_specs, [], []
    if h is not None:
        whole = pl.BlockSpec((D, N + LANES), lambda k: (0, 0), pipeline_mode=once)
        in_specs.append(rows(D))
        operands.append(h)
        out_specs += [whole, whole]
        out_shape += [jax.ShapeDtypeStruct((D, N + LANES), F32), jax.ShapeDtypeStruct((D, N + LANES), BF16)]
    if wxg:
        w, x, gam, g = wxg
        in_specs += [pl.BlockSpec((D, N), lambda k: (0, 0), pipeline_mode=once),
                     pl.BlockSpec((D, LANES), lambda k: (0, N // LANES), pipeline_mode=once),
                     rows(D), pl.BlockSpec((1, D), lambda k: (0, 0)), rows(D)]
        operands += [w, w, x, gam, g]
        out_specs += [rows(D), pl.BlockSpec((1, D), lambda k: (0, 0))]
        out_shape += [jax.ShapeDtypeStruct((S, D), F32), jax.ShapeDtypeStruct((1, D), F32)]
    name = "inproj_bwd" if h is not None and wxg else ("inproj_bwd_dw" if wxg is None else "inproj_bwd_dx")
    return pl.pallas_call(
        body, name=name, grid=(n_steps,),
        in_specs=in_specs + _after_specs(after), out_specs=out_specs, out_shape=out_shape,
        compiler_params=_params("arbitrary"),
    )(dzf, *operands, *after)


def _attn_bwd(proj, da, qaug, kaug, doaug, after=()):
    S = proj.shape[0]
    W = proj.shape[1] // 6
    P = W // LANES
    tq = min(TQ, S)
    nq = S // tq
    qc, kc, vc = 2 * P, 3 * P, 4 * P
    scale = 1.0 / math.sqrt(HEAD_DIM)

    def body(q_ref, k_ref, v_ref, do_ref, qa_ref, ka_ref, da_ref, *rest):
        dq_ref, dk_ref, dv_ref, drows_ref, dcols_ref, km_scr, vm_scr, dk_scr, dv_scr, dq_scr = rest[len(after):]
        pair, j = pl.program_id(0), pl.program_id(1)
        lane, halves = _head_halves(tq)

        @pl.when(jnp.logical_and(pair == 0, j == 0))
        def _():
            drows_ref[...] = jnp.zeros(drows_ref.shape, F32)
            dcols_ref[...] = jnp.zeros(dcols_ref.shape, F32)

        @pl.when(j == 0)
        def _():
            dq_scr[...] = jnp.zeros(dq_scr.shape, F32)

        v_ones = ((lane & (HEAD_DIM - 1)) < 3).astype(BF16)
        for a in range(2):
            km_scr[a] = jnp.where(halves[a], k_ref[...], ka_ref[...])
            vm_scr[a] = jnp.where(halves[a], v_ref[...], v_ones)
        dk_scr[...] = jnp.zeros(dk_scr.shape, F32)
        dv_scr[...] = jnp.zeros(dv_scr.shape, F32)

        def update(i, on_diagonal):
            rows = pl.ds(pl.multiple_of(i * tq, tq), tq)
            qs = q_ref[rows, :] * scale
            do2, qaug_t, doaug_t = do_ref[rows, :], qa_ref[rows, :], da_ref[rows, :]
            if on_diagonal:
                keep = (lax.broadcasted_iota(jnp.int32, (tq, tq), 0)
                        >= lax.broadcasted_iota(jnp.int32, (tq, tq), 1))
            qas = [jnp.where(halves[a], qs, qaug_t) for a in range(2)]
            logits = [lax.dot_general(qas[a], km_scr[a], NT, preferred_element_type=F32) for a in range(2)]
            dps = [lax.dot_general(jnp.where(halves[a], do2, doaug_t), vm_scr[a], NT, preferred_element_type=F32)
                   for a in range(2)]
            dv = None
            for a in range(2):
                s = jnp.where(keep, logits[a], NEG_INF) if on_diagonal else logits[a]
                p = jnp.exp(s)
                dsb = (p * dps[a]).astype(BF16)
                do0 = jnp.where(halves[a], do2, jnp.zeros_like(do2))
                dv_a = lax.dot_general(p.astype(BF16), do0, TN_DIMS, preferred_element_type=F32)
                dv = dv_a if dv is None else dv + dv_a
                dk_scr[a] += lax.dot_general(dsb, qas[a], TN_DIMS, preferred_element_type=F32)
                dq_scr[a, rows, :] += jnp.dot(dsb, km_scr[a], preferred_element_type=F32)
            dv_scr[...] += dv

        def below_diagonal(n, carry):
            update(j + 1 + 2 * n, False)
            update(j + 2 + 2 * n, False)
            return carry

        update(j, True)
        below = nq - 1 - j
        lax.fori_loop(0, below // 2, below_diagonal, 0)

        @pl.when(below % 2 == 1)
        def _():
            update(nq - 1, False)


        def to_head_lanes(old, first, second):
            at = lax.broadcasted_iota(jnp.int32, old.shape, 1) - 2 * pair
            return jnp.where(at == 0, first, jnp.where(at == 1, second, old))

        dk_ref[...] = jnp.where(halves[0], dk_scr[0], dk_scr[1]).astype(BF16)
        dv_ref[...] = dv_scr[...].astype(BF16)
        keys = pl.ds(pl.multiple_of(j * tq, tq), tq)
        ones_a, ones_b = HEAD_DIM + SLOT_ONE, SLOT_ONE
        dcols_ref[keys, :] = to_head_lanes(dcols_ref[keys, :], dk_scr[0][:, ones_a:ones_a + 1],
                                           dk_scr[1][:, ones_b:ones_b + 1])

        @pl.when(j == nq - 1)
        def _():
            row_lane, row_halves = _head_halves(S)
            dq_ref[...] = (jnp.where(row_halves[0], dq_scr[0], dq_scr[1]) * scale).astype(BF16)
            c_a, c_b = HEAD_DIM + SLOT_C, SLOT_C
            drows_ref[...] = to_head_lanes(drows_ref[...], dq_scr[0][:, c_a:c_a + 1], dq_scr[1][:, c_b:c_b + 1])

    tile = lambda col: pl.BlockSpec((tq, LANES), lambda p, j: (j, col + p))
    whole = lambda col: pl.BlockSpec((S, LANES), lambda p, j: (0, col + p))
    shared = pl.BlockSpec((S, LANES), lambda p, j: (0, 0))
    return pl.pallas_call(
        body, name="attn_bwd", grid=(P, nq),
        in_specs=[whole(qc), tile(kc), tile(vc), whole(P), whole(0), tile(0), whole(0)] + _after_specs(after),
        out_specs=[whole(0), tile(0), tile(0), shared, shared],
        out_shape=[jax.ShapeDtypeStruct((S, W), BF16),
                   jax.ShapeDtypeStruct((S, W), BF16),
                   jax.ShapeDtypeStruct((S, W), BF16),
                   jax.ShapeDtypeStruct((S, LANES), F32),
                   jax.ShapeDtypeStruct((S, LANES), F32)],
        scratch_shapes=[pltpu.VMEM((2, tq, LANES), BF16),
                        pltpu.VMEM((2, tq, LANES), BF16),
                        pltpu.VMEM((2, tq, LANES), F32),
                        pltpu.VMEM((tq, LANES), F32),
                        pltpu.VMEM((2, S, LANES), F32)],
        compiler_params=_params("arbitrary", "arbitrary"),
    )(proj, proj, proj, da, qaug, kaug, doaug, *after)


def _fgate_bwd(drows, dcols, z, bias):
    S = z.shape[0]
    tb = min(TB, S)
    nb = S // tb

    def body(drows_ref, dcols_ref, z_ref, b_ref, dz_ref, db_ref):
        tri = (lax.broadcasted_iota(jnp.int32, (tb, tb), 1)
               >= lax.broadcasted_iota(jnp.int32, (tb, tb), 0)).astype(BF16)

        local = []
        for b in range(nb):
            rows = slice(b * tb, (b + 1) * tb)
            local.append(_dot_exact_left(tri, drows_ref[rows, :] - dcols_ref[rows, :]))
        carry = jnp.zeros((1, LANES), F32)
        db = jnp.zeros((1, LANES), F32)
        for b in reversed(range(nb)):
            rows = slice(b * tb, (b + 1) * tb)
            rc = local[b] + carry
            carry = rc[0:1, :]
            dz = rc * _sigmoid(-(z_ref[rows, :] + b_ref[...]))
            dz_ref[rows, :] = dz.astype(BF16)
            db = db + jnp.sum(dz, axis=0, keepdims=True)
        db_ref[...] = db

    return pl.pallas_call(
        body, name="fgate_bwd",
        out_shape=[jax.ShapeDtypeStruct((S, LANES), BF16),
                   jax.ShapeDtypeStruct((1, LANES), F32)],
        compiler_params=pltpu.CompilerParams(vmem_limit_bytes=VMEM_LIMIT),
    )(drows, dcols, z, bias)


def _pool_bwd(proj, da, pool_w, pool_scale):
    S = proj.shape[0]
    G = len(POOL_WINDOWS)

    def body(u_ref, dy_ref, w_ref, s_ref, du_ref, dw_ref, ds_ref, pad_ref):
        g = pl.program_id(0)
        for gi, w in enumerate(POOL_WINDOWS):
            @pl.when(g == gi)
            def _():
                d, cnt = _window_mean_minus_self(u_ref[...].astype(F32), pad_ref, w, S)
                db = d.astype(BF16)
                wb = w_ref[0].astype(BF16)
                yraw = jnp.dot(db, wb, preferred_element_type=F32)
                dy = dy_ref[...].astype(F32)
                ds_ref[...] = jnp.sum(dy * yraw, axis=0, keepdims=True)
                dzb = (dy * s_ref[...]).astype(BF16)
                dw_ref[0] = lax.dot_general(db, dzb, TN_DIMS, preferred_element_type=F32)
                dd = lax.dot_general(dzb, wb, NT, preferred_element_type=F32)
                acc = dd / cnt
                pad_ref[0:S, :] = acc
                pad_ref[S:S + MAX_WINDOW, :] = jnp.zeros((MAX_WINDOW, LANES), F32)
                inner = _inner_window(w)
                for j in range(1, inner):
                    acc = acc + pad_ref[j:j + S, :]
                if inner > 1:
                    pad_ref[0:S, :] = acc
                for j in range(inner, w, inner):
                    acc = acc + pad_ref[j:j + S, :]
                du_ref[...] = (acc - dd).astype(BF16)

    return pl.pallas_call(
        body, name="pool_bwd", grid=(G,),
        in_specs=[pl.BlockSpec((S, LANES), lambda g: (0, g)),
                  pl.BlockSpec((S, LANES), lambda g: (0, g)),
                  pl.BlockSpec((1, LANES, LANES), lambda g: (g, 0, 0)),
                  pl.BlockSpec((1, LANES), lambda g: (0, g))],
        out_specs=[pl.BlockSpec((S, LANES), lambda g: (0, g)),
                   pl.BlockSpec((1, LANES, LANES), lambda g: (g, 0, 0)),
                   pl.BlockSpec((1, LANES), lambda g: (0, g))],
        out_shape=[jax.ShapeDtypeStruct((S, G * LANES), BF16),
                   jax.ShapeDtypeStruct((G, LANES, LANES), F32),
                   jax.ShapeDtypeStruct((1, G * LANES), F32)],
        scratch_shapes=[pltpu.VMEM((S + MAX_WINDOW, LANES), F32)],
        compiler_params=_params("arbitrary"),
    )(proj, da, pool_w, pool_scale)


def _adamw(w, m, v, gsets, name, rows, shifted=False, first=0, into=None):
    A, R, C = w.shape
    n_sets = len(gsets)
    tr = min(rows, R)
    c1 = 1.0 / (1.0 - ADAM_B1 ** ADAM_STEP)
    c2 = 1.0 / (1.0 - ADAM_B2 ** ADAM_STEP)
    counts = [len(gs) for gs in gsets]

    def body(w_ref, m_ref, v_ref, *rest):
        g_ref, d_ref, nm_ref, nv_ref = rest[-4:]
        at = 0
        for a in range(n_sets):
            part_refs = rest[at:at + counts[a]]
            at += counts[a]

            @pl.when(pl.program_id(0) == a)
            def _():
                g = None
                for ref in part_refs:
                    for s in range(ref.shape[0]):
                        term = ref[s].astype(F32)
                        g = term if g is None else g + term
                if shifted:
                    lanes = g.shape[1]
                    g = pltpu.roll(g, (lanes - _index(_position()) * (C % LANES)) % lanes, axis=1)[:, :C]
                nm = ADAM_B1 * m_ref[0] + (1.0 - ADAM_B1) * g
                nv = ADAM_B2 * v_ref[0] + (1.0 - ADAM_B2) * (g * g)
                g_ref[0] = g
                nm_ref[0] = nm
                nv_ref[0] = nv
                d_ref[0] = -ADAM_LR * ((nm * c1) / (jnp.sqrt(nv * c2) + ADAM_EPS) + ADAM_WD * w_ref[0])

    spec = pl.BlockSpec((1, tr, C), lambda a, r: (first + a, r, 0))
    part_specs = [pl.BlockSpec((part.shape[0], tr, part.shape[2]), lambda a, r, l=l: (0, jnp.where(a == l, r, 0), 0))
                  for l, gs in enumerate(gsets) for part in gs]
    parts = [part for gs in gsets for part in gs]
    shape = jax.ShapeDtypeStruct((A, R, C), F32)
    earlier = () if into is None else tuple(into)
    return pl.pallas_call(
        body, name=name, grid=(n_sets, R // tr),
        in_specs=[spec, spec, spec] + part_specs + _after_specs(earlier),
        out_specs=[spec, spec, spec, spec],
        out_shape=[shape, shape, shape, shape],
        input_output_aliases={3 + len(parts) + n: n for n in range(len(earlier))},
        compiler_params=_params("arbitrary", "arbitrary"),
    )(w, m, v, *parts, *earlier)


def _position():
    return lax.axis_index("x"), lax.axis_index("y"), lax.axis_index("c")


def _index(dev):
    return 4 * dev[0] + 2 * dev[1] + dev[2]


def _all_gather(arrs, slots, out_shapes, name, meanwhile):
    n_arr = len(arrs)
    side_in, side_shapes, side_scratch, side_work = meanwhile
    n_side = len(side_in)

    def body(*refs):
        ins, outs = refs[:n_arr], refs[n_arr + n_side:2 * n_arr + n_side]
        at = 2 * n_arr + n_side + len(side_shapes)
        send_sems, recv_sems, local_sems = refs[at:at + 3]
        x, y, c = _position()
        me, sibling = (x, y, c), (x, y, 1 - c)
        chips = [(1 - x, y), (x, 1 - y), (1 - x, 1 - y)]

        def copy(a, k, block, to, src=None):
            part = slots[a](outs[a], _index(block))
            return pltpu.make_async_remote_copy(
                src_ref=part if src is None else src, dst_ref=part,
                send_sem=send_sems.at[a, k], recv_sem=recv_sems.at[a, k],
                device_id=to, device_id_type=MESH)

        mine = [pltpu.make_async_copy(ins[a], slots[a](outs[a], _index(me)), local_sems.at[a])
                for a in range(n_arr)]
        for cp in mine:
            cp.start()
        first = []
        for a in range(n_arr):
            first.append(copy(a, 0, me, sibling, src=ins[a]))
            first += [copy(a, 1 + j, me, (*chip, c), src=ins[a]) for j, chip in enumerate(chips)]
        for cp in first:
            cp.start()
        side_work(refs[n_arr:n_arr + n_side], refs[2 * n_arr + n_side:at], refs[at + 3:])
        passed = []
        for j, chip in enumerate(chips):
            for a in range(n_arr):
                copy(a, 1 + j, (*chip, c), me).wait_recv()
                fwd = copy(a, 4 + j, (*chip, c), sibling)
                fwd.start()
                passed.append(fwd)
        for a in range(n_arr):
            copy(a, 0, sibling, me).wait_recv()
            for j, chip in enumerate(chips):
                copy(a, 4 + j, (*chip, 1 - c), me).wait_recv()
        for cp in first + passed:
            cp.wait_send()
        for cp in mine:
            cp.wait()

    any_spec = pl.BlockSpec(memory_space=pl.ANY)
    return pl.pallas_call(
        body, name=name,
        in_specs=[any_spec] * (n_arr + n_side), out_specs=[any_spec] * (n_arr + len(side_shapes)),
        out_shape=[*out_shapes, *side_shapes],
        scratch_shapes=[pltpu.SemaphoreType.DMA((n_arr, 7)), pltpu.SemaphoreType.DMA((n_arr, 7)),
                        pltpu.SemaphoreType.DMA((n_arr,)), *side_scratch],
        compiler_params=pltpu.CompilerParams(vmem_limit_bytes=VMEM_LIMIT),
    )(*arrs, *side_in)


def _split_copies(srcs, lands, send_sems, recv_sems, kinds):
    x, y, c = _position()
    me = _index((x, y, c))
    copies = []
    for a, (src_part, land_part) in enumerate(kinds):
        for k in range(1, N_DEV):
            peer = (x ^ ((k >> 2) & 1), y ^ ((k >> 1) & 1), c ^ (k & 1))
            copies.append(pltpu.make_async_remote_copy(
                src_ref=src_part(srcs[a], _index(peer)), dst_ref=land_part(lands[a], me, k),
                send_sem=send_sems[a].at[k - 1], recv_sem=recv_sems[a].at[k - 1],
                device_id=peer, device_id_type=MESH))
    return copies


def _split_start(srcs, lands, kinds, name, after=()):
    n = len(srcs)

    def body(*refs):
        src_refs, land_refs = refs[:n], refs[n:2 * n]
        outs = refs[2 * n + len(after):]
        send_sems, recv_sems = outs[:n], outs[n:2 * n]
        token = outs[-1]
        for cp in _split_copies(src_refs, land_refs, send_sems, recv_sems, kinds):
            cp.start()
        token[...] = jnp.zeros(token.shape, token.dtype)

    hbm = pl.BlockSpec(memory_space=pltpu.HBM)
    sem = pl.BlockSpec(memory_space=pltpu.SEMAPHORE)
    operands = [pltpu.with_memory_space_constraint(t, pltpu.HBM) for t in (*srcs, *lands)]
    out = pl.pallas_call(
        body, name=name,
        in_specs=[hbm] * (2 * n) + _after_specs(after),
        out_specs=[sem] * (2 * n) + [hbm] * (2 * n) + [pl.BlockSpec(memory_space=pltpu.VMEM)],
        out_shape=[pltpu.SemaphoreType.DMA((N_DEV - 1,))] * (2 * n)
        + [pltpu.HBM(t.shape, t.dtype) for t in operands] + [jax.ShapeDtypeStruct((8, LANES), F32)],
        input_output_aliases={i: 2 * n + i for i in range(2 * n)},
        compiler_params=pltpu.CompilerParams(has_side_effects=pltpu.SideEffectType.DATAFLOW_SIDE_EFFECTING),
    )(*operands, *after)
    return [(out[a], out[n + a], out[2 * n + a], out[3 * n + a]) for a in range(n)], out[-1]


def _split_wait(started, kinds, after, name):
    n = len(started)
    sems = [t[0] for t in started] + [t[1] for t in started]
    srcs = [t[2] for t in started]
    lands = [t[3] for t in started]

    def body(*refs):
        src_refs, land_refs = refs[:n], refs[n:2 * n]
        send_sems, recv_sems = refs[2 * n:3 * n], refs[3 * n:4 * n]
        for cp in _split_copies(src_refs, land_refs, send_sems, recv_sems, kinds):
            cp.wait_send()
            cp.wait_recv()

    hbm = pl.BlockSpec(memory_space=pltpu.HBM)
    sem = pl.BlockSpec(memory_space=pltpu.SEMAPHORE)
    out = pl.pallas_call(
        body, name=name,
        in_specs=[hbm] * (2 * n) + [sem] * (2 * n) + _after_specs(after),
        out_specs=[hbm] * (2 * n),
        out_shape=[pltpu.HBM(t.shape, t.dtype) for t in (*srcs, *lands)],
        input_output_aliases={i: i for i in range(2 * n)},
        compiler_params=pltpu.CompilerParams(has_side_effects=pltpu.SideEffectType.DATAFLOW_SIDE_EFFECTING),
    )(*srcs, *lands, *sems, *after)
    return out[n:]


def _as_rows(p):
    if p.size % LANES == 0:
        rows = p.reshape(-1, LANES)
    else:
        rows = p.reshape(-1, p.shape[-1])
        rows = jnp.pad(rows, ((0, 0), (0, LANES - rows.shape[1])))
    return jnp.pad(rows, ((0, -rows.shape[0] % 8), (0, 0)))


def _pack(parts):
    return jnp.concatenate([_as_rows(p) for p in parts])[None]


def _unpack(packed, like):
    out, at = [], 0
    for p in like:
        whole = p.size % LANES == 0
        n = p.size // LANES if whole else p.size // p.shape[-1]
        rows = packed[0, at:at + n]
        out.append((rows if whole else rows[:, :p.shape[-1]]).reshape(p.shape))
        at += n + (-n % 8)
    return out


def _local_step(x, target, norm_g, forget_bias, pool_w, pool_scale, final_g, weights_in, weights_out, on_grads,
                first_after=(), on_first_out_grads=lambda d_wout, d_wout_bf16: ()):
    L = norm_g.shape[0]
    S, D = x.shape
    W = D // 2
    H = W // HEAD_DIM
    bias = jnp.pad(forget_bias, ((0, 0), (0, LANES - H)))

    saved = []
    nxt = _inproj_fwd(x, norm_g[0:1], weights_in(0, x), tuple(first_after))
    for l in range(L):
        x_in, (proj, h, z, w) = x, nxt
        qaug, kaug = _fgate_fwd(z, bias[l:l + 1], H)
        ypool = _pool_fwd(proj, pool_w[l], pool_scale[l:l + 1])
        o, qaug_b = _attn_fwd(proj, qaug, kaug)
        wout = weights_out(l, o)
        if l < L - 1:
            x, *nxt = _outproj_fwd(ypool, o, proj, x_in, wout, nxt=(norm_g[l + 1:l + 2], weights_in(l + 1, o)))
        else:
            g, loss, d_final_g = _outproj_fwd(ypool, o, proj, x_in, wout, head=(final_g.reshape(1, D), target))
        saved.append((x_in, proj, h, z, qaug_b, kaug, ypool, o, w, wout))

    small, after = None, ()
    for l in reversed(range(L)):
        x_in, proj, h, z, qaug_b, kaug, ypool, o, w, wout = saved[l]
        d_wout, d_wout_bf16, da, dgate, doaug = _outproj_bwd(g, wout, ypool, o, proj, after)
        early = tuple(on_first_out_grads(d_wout, d_wout_bf16)) if l == 0 else ()
        dq, dk, dv, drows, dcols = _attn_bwd(proj, da, qaug_b, kaug, doaug, early)
        dzf, db = _fgate_bwd(drows, dcols, z, bias[l:l + 1])
        dpu, dpw, dps = _pool_bwd(proj, da, pool_w[l], pool_scale[l:l + 1])
        dproj = [(dpu, 0), (dgate, 0), (dq, 0), (dk, 0), (dv, 0), (dgate, 1)]
        wxg = (w, x_in, norm_g[l:l + 1], g)
        if l > 0:
            d_w, d_w_bf16, g, dgam = _inproj_bwd(dproj, dzf, h, wxg)
            after = tuple(on_grads(l, d_w, d_w_bf16, d_wout, d_wout_bf16, small))
        else:
            d_w, d_w_bf16 = _inproj_bwd(dproj, dzf, h)
            after = tuple(on_grads(l, d_w, d_w_bf16, d_wout, d_wout_bf16, small))
            g, dgam = _inproj_bwd(dproj, dzf, None, wxg, after)
        small = (dgam[0], db[0, :H], dpw, dps[0])
    return loss[0, 0], g, small, d_final_g[0]


def kernel(x, norm_g, w_in, forget_bias, pool_w, pool_scale, w_out, final_g, loss_target, m_norm_g, m_w_in, m_forget_bias, m_pool_w, m_pool_scale, m_w_out, m_final_g, v_norm_g, v_w_in, v_forget_bias, v_pool_w, v_pool_scale, v_w_out, v_final_g):
    L, D, cols = w_in.shape
    rows_out = w_out.shape[1]
    me = _index(_position())
    slot = _slot_width(cols)
    wout_b = w_out.astype(BF16)
    win_first = _shift_slots(w_in, 1)
    gather_in = (lambda ref, peer: ref, lambda ref, mine, k: ref.at[mine])
    gather_out = (lambda ref, peer: ref, lambda ref, mine, k: ref.at[pl.ds(mine * rows_out, rows_out), :])

    def landing(block, n_slots):
        zone = lax.empty((n_slots * block.shape[0], *block.shape[1:]), block.dtype)
        return lax.dynamic_update_slice(zone, block, (me * block.shape[0],) + (0,) * (block.ndim - 1))

    first_in, win_rest, *win_lands = _all_gather([win_first[0]], [lambda ref, n: ref.at[n]],
                                     [jax.ShapeDtypeStruct((N_DEV, D, slot), BF16)], "gather_first",
                                     _shift_meanwhile(w_in, 1))
    win_b = [win_first[0]] + [win_rest[l - 1] for l in range(1, L)]
    rest_srcs = [wout_b[0]] + [w[l] for l in range(1, L) for w in (win_b, wout_b)]
    rest_lands = [landing(wout_b[0], N_DEV)]
    for l in range(1, L):
        rest_lands += [win_lands[l - 1], landing(wout_b[l], N_DEV)]
    rest_kinds = [gather_out] + [gather_in, gather_out] * (L - 1)
    rest, rest_token = _split_start(rest_srcs, rest_lands, rest_kinds, "gather_start_rest", (first_in,))

    def weights_in(l, x_in):
        if l == 0:
            return first_in
        (win_all,) = _split_wait([rest[2 * l - 1]], [gather_in], (x_in,), f"gather_wait_in_{l}")
        return win_all

    def weights_out(l, o):
        (wout_full,) = _split_wait([rest[2 * l]], [gather_out], (o,), f"gather_wait_out_{l}")
        return wout_full

    stride = slot - LANES
    exchange_kinds = [(lambda ref, peer: ref.at[:, pl.ds(pl.multiple_of(peer * stride, LANES), slot)],
                       lambda ref, mine, k: ref.at[k - 1]),
                      (lambda ref, peer: ref.at[pl.ds(peer * rows_out, rows_out), :],
                       lambda ref, mine, k: ref.at[k - 1])]
    zero_g = jnp.zeros_like(final_g)
    zero_loss = jnp.zeros((LANES,), F32)

    def small_pack(l, norm_g_l, bias_l, pool_w_l, pool_scale_l, final, loss_row=None):
        return _pack([norm_g_l, bias_l, pool_w_l, pool_scale_l, final if l == 0 else zero_g,
                      zero_loss if loss_row is None else loss_row])[0]

    exchanges, own_parts, first_out = {}, {}, []

    def on_first_out_grads(d_wout, d_wout_bf16):
        started, token = _split_start([d_wout_bf16], [lax.empty((N_DEV - 1, rows_out, D), BF16)], exchange_kinds[1:],
                                      "exchange_start_out_0")
        first_out.extend(started)
        return (token,)

    def on_grads(l, dw, dw_bf16, d_wout, d_wout_bf16, small):
        own_parts[l] = (lax.dynamic_slice_in_dim(dw, me * stride, slot, 1)[None],
                        lax.dynamic_slice_in_dim(d_wout, me * rows_out, rows_out, 0)[None])
        out_on_its_way = l == 0
        srcs, lands, kinds = [dw_bf16], [lax.empty((N_DEV - 1, D, slot), BF16)], exchange_kinds[:1]
        if not out_on_its_way:
            srcs.append(d_wout_bf16)
            lands.append(lax.empty((N_DEV - 1, rows_out, D), BF16))
            kinds = list(exchange_kinds)
        if small is not None:
            packed_small = small_pack(l + 1, *small, None)
            srcs.append(packed_small)
            lands.append(landing(packed_small[None], N_DEV))
            kinds.append(gather_in)
        started, token = _split_start(srcs, lands, kinds, f"exchange_start_{l}")
        if out_on_its_way:
            started[1:1], kinds[1:1] = first_out, exchange_kinds[1:]
        exchanges[l] = (started, kinds)
        return (token,)

    loss, dx, small_first, d_final_g = _local_step(
        x[0], loss_target[0], norm_g, forget_bias, pool_w, pool_scale, final_g,
        weights_in, weights_out, on_grads, (rest_token,), on_first_out_grads)
    packed_first = small_pack(0, *small_first, d_final_g, jnp.full((LANES,), loss, F32))
    first_started, first_token = _split_start(
        [packed_first], [landing(packed_first[None], N_DEV)], [gather_in],
        "small_start", (w_in, m_w_in, v_w_in, *own_parts[0]))

    gin_sets, gout_sets, small_sets = [None] * L, [None] * L, [None] * L

    def wait_for(l, after):
        started, kinds = exchanges[l]
        got = _split_wait(started, kinds, after, f"exchange_wait_{l}")
        gin_sets[l] = [own_parts[l][0], got[0]]
        gout_sets[l] = [own_parts[l][1], got[1]]
        if len(got) > 2:
            small_sets[l + 1] = [got[2]]

    for l in range(1, L):
        wait_for(l, (dx, first_token))
    rest_in = _adamw(w_in, m_w_in, v_w_in, gin_sets[1:], "adamw_w_in_rest", TM // 2, shifted=True, first=1)
    rest_out = _adamw(w_out, m_w_out, v_w_out, gout_sets[1:], "adamw_w_out_rest", rows_out, first=1)
    wait_for(0, (rest_in[1], rest_out[1]))
    g_w_in, d_w_in, nm_w_in, nv_w_in = _adamw(w_in, m_w_in, v_w_in, gin_sets[:1], "adamw_w_in_first", TM // 2,
                                              shifted=True, into=rest_in)
    g_w_out, d_w_out, nm_w_out, nv_w_out = _adamw(w_out, m_w_out, v_w_out, gout_sets[:1], "adamw_w_out_first",
                                                  rows_out, into=rest_out)
    small_sets[0] = _split_wait(first_started, [gather_in], (d_w_in, d_w_out), "small_wait")
    loss = jnp.sum(small_sets[0][0][:, packed_first.shape[0] - 8, 0])

    def small_stack(norm_g_, bias_, pool_w_, pool_scale_, final):
        return jnp.stack([small_pack(l, norm_g_[l], bias_[l], pool_w_[l], pool_scale_[l], final) for l in range(L)])

    packed = _adamw(small_stack(norm_g, forget_bias, pool_w, pool_scale, final_g),
                    small_stack(m_norm_g, m_forget_bias, m_pool_w, m_pool_scale, m_final_g),
                    small_stack(v_norm_g, v_forget_bias, v_pool_w, v_pool_scale, v_final_g),
                    small_sets, "adamw_small", packed_first.shape[0])

    def small_unpack(p):
        like = [norm_g[0], forget_bias[0], pool_w[0], pool_scale[0], final_g]
        layers = [_unpack(p[l:l + 1], like) for l in range(L)]
        return [jnp.stack([layers[l][n] for l in range(L)]) for n in range(4)] + [layers[0][4]]

    g_s, d_s, nm_s, nv_s = [small_unpack(p) for p in packed]

    def order(big_in, big_out, small):
        return (small[0], big_in, small[1], small[2], small[3], big_out, small[4])

    return (loss, dx[None], *order(g_w_in, g_w_out, g_s), *order(d_w_in, d_w_out, d_s),
            *order(nm_w_in, nm_w_out, nm_s), *order(nv_w_in, nv_w_out, nv_s))
```
